```python
import jax
import jax.numpy as jnp
from jax import lax
import numpy as np

D_MODEL = 1024
BATCH = 8
SEQ = 2048
DEPTH = 2

EPS = 1e-6
MIX_WIDTH = D_MODEL
A_WIDTH = D_MODEL // 2
A_HEAD_DIM = 64
A_HEADS = A_WIDTH // A_HEAD_DIM
A_LORA_W = 64
A_LORA_A = 64
A_LORA_G = 128
A_IN = 3 * A_WIDTH + A_LORA_W + A_LORA_A + A_LORA_G
A_GN_EPS = 64e-5
B_WIDTH = D_MODEL // 2
B_GROUPS = 4
B_GROUP_DIM = B_WIDTH // B_GROUPS
B_CHUNK = 128
EVEN_IN = A_IN + 2 * B_WIDTH
C_WIDTH = D_MODEL // 2
C_GROUP_CH = 16
C_GROUPS = C_WIDTH // C_GROUP_CH
C_STATE = 64
D_WIDTH = D_MODEL // 2
D_HEAD_DIM = 64
D_HEADS = D_WIDTH // D_HEAD_DIM
D_NGROUPS = 2
D_STATE = 128
D_CONV = 4
D_CHUNK = 128
D_XBC = D_WIDTH + 2 * D_NGROUPS * D_STATE
ODD_IN = C_WIDTH + D_WIDTH + D_XBC + D_HEADS
FFN_DENSE = 2816
N_EXPERTS = 8
TOP_K = 2
FFN_EXPERT = 3584
MOE_BLOCK = 256

kernel_name = 'hybrid_rwkv7_gmlp_s5_ssd_moe'


def rms_norm(z, g):
    zf = z.astype(jnp.float32)
    y = zf * lax.rsqrt(jnp.mean(zf * zf, axis=-1, keepdims=True) + EPS)
    return (y * g.astype(jnp.float32)).astype(z.dtype)


def layer_norm(z, w, b, eps):
    zf = z.astype(jnp.float32)
    mean = jnp.mean(zf, axis=-1, keepdims=True)
    var = jnp.mean(jnp.square(zf - mean), axis=-1, keepdims=True)
    return (zf - mean) * lax.rsqrt(var + eps) * w + b


def token_shift(z):
    return jnp.pad(z, ((0, 0), (1, 0), (0, 0)))[:, :-1]


def causal_dwconv(z, w, b):
    t = z.shape[1]
    zp = jnp.pad(z, ((0, 0), (D_CONV - 1, 0), (0, 0)))
    out = b
    for j in range(D_CONV):
        out = out + w[j] * zp[:, j:j + t]
    return out


def swiglu(h, w_gate, w_up, w_down):
    return (jax.nn.silu(h @ w_gate) * (h @ w_up)) @ w_down


def rwkv7_step(state, inp):
    r_t, w_t, k_t, v_t, a_t, b_t = inp
    sa = jnp.einsum('bhvk,bhk->bhv', state, a_t)
    state = (state * w_t[:, :, None, :] + sa[..., None] * b_t[:, :, None, :]
             + v_t[..., None] * k_t[:, :, None, :])
    return state, jnp.einsum('bhvk,bhk->bhv', state, r_t)


def rwkv7_mix(p, mu, w0, w2, a0, a2, g2, k_k, k_a, r_k, ln_w, ln_b):
    f32 = jnp.float32
    bsz, t, _ = p.shape
    p = p + (token_shift(p) - p) * mu
    i3 = 3 * A_WIDTH
    r, k, v, xw, xa, xg = jnp.split(
        p, [A_WIDTH, 2 * A_WIDTH, i3, i3 + A_LORA_W, i3 + A_LORA_W + A_LORA_A], axis=-1)
    w = -jax.nn.softplus(-(w0 + jnp.tanh(xw) @ w2).astype(f32)) - 0.5
    decay = jnp.exp(-jnp.exp(w))
    a = jax.nn.sigmoid((a0 + xa @ a2).astype(f32))
    g = jax.nn.sigmoid(xg) @ g2
    k = k.astype(f32)
    heads = lambda z: z.astype(f32).reshape(bsz, t, A_HEADS, A_HEAD_DIM)
    kk = heads(k * k_k)
    kk = kk / jnp.maximum(jnp.sqrt(jnp.sum(kk * kk, axis=-1, keepdims=True)), 1e-12)
    k = heads(k * (1.0 + (a - 1.0) * k_a))
    r, v, decay, a = heads(r), heads(v), heads(decay), heads(a)
    tm = lambda z: jnp.swapaxes(z, 0, 1)
    xs = (tm(r), tm(decay), tm(k), tm(v), tm(-kk), tm(kk * a))
    s0 = jnp.zeros((bsz, A_HEADS, A_HEAD_DIM, A_HEAD_DIM), f32)
    _, y = lax.scan(rwkv7_step, s0, xs)
    y = jnp.swapaxes(y, 0, 1)
    y = layer_norm(y, ln_w.reshape(A_HEADS, A_HEAD_DIM), ln_b.reshape(A_HEADS, A_HEAD_DIM), A_GN_EPS)
    y = y + jnp.sum(r * k * r_k, axis=-1, keepdims=True) * v
    y = y.reshape(bsz, t, A_WIDTH) * g
    return y.astype(p.dtype)


def gmlp_mix(p, ln_w, ln_b, ws, bs):
    bsz, t, _ = p.shape
    u, v = jnp.split(jax.nn.gelu(p), 2, axis=-1)
    nc = t // B_CHUNK
    v = v.reshape(bsz, nc, B_CHUNK, B_GROUPS, B_GROUP_DIM)
    v = layer_norm(v, ln_w, ln_b, EPS)
    causal = jnp.tril(jnp.ones((B_CHUNK, B_CHUNK), bool))
    ws_c = jnp.where(causal[None], ws, 0.0)
    s = jnp.einsum('gij,bcjgd->bcigd', ws_c, v) + bs.T[None, None, :, :, None]
    return (u * s.reshape(bsz, t, B_WIDTH).astype(u.dtype)).astype(p.dtype)


def s5_combine(e1, e2):
    a1r, a1i, b1r, b1i = e1
    a2r, a2i, b2r, b2i = e2
    return (a2r * a1r - a2i * a1i,
            a2r * a1i + a2i * a1r,
            a2r * b1r - a2i * b1i + b2r,
            a2r * b1i + a2i * b1r + b2i)


def s5_mix(u, a_re, a_im, log_dt, b_re, b_im, c_re, c_im, d_skip, glu_w, glu_b):
    f32 = jnp.float32
    bsz, t, _ = u.shape
    uf = u.astype(f32).reshape(bsz, t, C_GROUPS, C_GROUP_CH)
    lr, li = a_re.astype(f32), a_im.astype(f32)
    dt = jnp.exp(log_dt.astype(f32))[:, None]
    mag = jnp.exp(lr * dt)
    abar_r, abar_i = mag * jnp.cos(li * dt), mag * jnp.sin(li * dt)
    nr, ni = abar_r - 1.0, abar_i
    den = lr * lr + li * li
    fr, fi = (nr * lr + ni * li) / den, (ni * lr - nr * li) / den
    br, bi = b_re.astype(f32), b_im.astype(f32)
    bbar_r = fr[..., None] * br - fi[..., None] * bi
    bbar_i = fr[..., None] * bi + fi[..., None] * br
    bu_r = jnp.einsum('btgc,gpc->tbgp', uf, bbar_r)
    bu_i = jnp.einsum('btgc,gpc->tbgp', uf, bbar_i)
    shp = bu_r.shape
    a_r = jnp.broadcast_to(abar_r, shp)
    a_i = jnp.broadcast_to(abar_i, shp)
    _, _, xr, xi = lax.associative_scan(s5_combine, (a_r, a_i, bu_r, bu_i), axis=0)
    y = (jnp.einsum('tbgp,gcp->btgc', xr, c_re.astype(f32))
         - jnp.einsum('tbgp,gcp->btgc', xi, c_im.astype(f32)))
    y = y + d_skip.astype(f32).reshape(C_GROUPS, C_GROUP_CH) * uf
    y = jax.nn.gelu(y.reshape(bsz, t, C_WIDTH))
    y = y * jax.nn.sigmoid(y @ glu_w.astype(f32) + glu_b.astype(f32))
    return y.astype(u.dtype)


def segsum(a):
    l = a.shape[-1]
    z = jnp.broadcast_to(a[..., :, None], a.shape + (l,))
    z = jnp.where(jnp.tril(jnp.ones((l, l), bool), -1), z, 0.0)
    cs = jnp.cumsum(z, axis=-2)
    return jnp.where(jnp.tril(jnp.ones((l, l), bool)), cs, -jnp.inf)


def ssd_chunked(x, a, bm, cm):
    b, t, h, pd = x.shape
    g, n = bm.shape[2], bm.shape[3]
    r = h // g
    c, l = t // D_CHUNK, D_CHUNK
    x = x.reshape(b, c, l, g, r, pd)
    a = a.reshape(b, c, l, g, r).transpose(0, 3, 4, 1, 2)
    bm = bm.reshape(b, c, l, g, n)
    cm = cm.reshape(b, c, l, g, n)
    a_cs = jnp.cumsum(a, axis=-1)
    lmat = jnp.exp(segsum(a))
    cb = jnp.einsum('bclgn,bcsgn->bgcls', cm, bm)
    y_diag = jnp.einsum('bgcls,bgrcls,bcsgrp->bclgrp', cb, lmat, x)
    decay_states = jnp.exp(a_cs[..., -1:] - a_cs)
    states = jnp.einsum('bclgn,bgrcl,bclgrp->bcgrpn', bm, decay_states, x)
    states = jnp.concatenate([jnp.zeros_like(states[:, :1]), states], axis=1)
    chunk_tot = jnp.pad(a_cs[..., -1], ((0, 0), (0, 0), (0, 0), (1, 0)))
    decay_chunk = jnp.exp(segsum(chunk_tot))
    states = jnp.einsum('bgrzc,bcgrpn->bzgrpn', decay_chunk, states)[:, :-1]
    y_off = jnp.einsum('bclgn,bcgrpn,bgrcl->bclgrp', cm, states, jnp.exp(a_cs))
    return (y_diag + y_off).reshape(b, t, h, pd)


def mamba2_mix(p, conv_w, conv_b, dt_bias, a_log, d_skip, norm_w):
    f32 = jnp.float32
    bsz, t, _ = p.shape
    z, xbc, dt = jnp.split(p, [D_WIDTH, D_WIDTH + D_XBC], axis=-1)
    xbc = jax.nn.silu(causal_dwconv(xbc, conv_w, conv_b)).astype(f32)
    xh, bm, cm = jnp.split(xbc, [D_WIDTH, D_WIDTH + D_NGROUPS * D_STATE], axis=-1)
    dt = jax.nn.softplus((dt + dt_bias).astype(f32))
    a = -jnp.exp(a_log.astype(f32))
    xh = xh.reshape(bsz, t, D_HEADS, D_HEAD_DIM)
    bm = bm.reshape(bsz, t, D_NGROUPS, D_STATE)
    cm = cm.reshape(bsz, t, D_NGROUPS, D_STATE)
    y = ssd_chunked(xh * dt[..., None], a * dt, bm, cm)
    y = y + d_skip.astype(f32)[None, None, :, None] * xh
    y = y.reshape(bsz, t, D_WIDTH) * jax.nn.silu(z.astype(f32))
    yg = y.reshape(bsz, t, D_NGROUPS, D_WIDTH // D_NGROUPS)
    yg = yg * lax.rsqrt(jnp.mean(yg * yg, axis=-1, keepdims=True) + EPS)
    return (yg.reshape(bsz, t, D_WIDTH) * norm_w.astype(f32)).astype(p.dtype)


def moe_swiglu(h, w_router, w_gate, w_up, w_down):
    bsz, t, d = h.shape
    n = bsz * t
    hf = h.reshape(n, d)
    logits = (hf @ w_router).astype(jnp.float32)
    top_vals, top_idx = lax.top_k(logits, TOP_K)
    gates = jax.nn.softmax(top_vals, axis=-1).astype(h.dtype)
    flat_e = top_idx.reshape(-1).astype(jnp.int32)
    flat_tok = jnp.repeat(jnp.arange(n, dtype=jnp.int32), TOP_K)
    flat_gate = gates.reshape(-1)
    order = jnp.argsort(flat_e)
    sorted_e = flat_e[order]
    counts = jnp.bincount(flat_e, length=N_EXPERTS)
    padded = (counts + MOE_BLOCK - 1) // MOE_BLOCK * MOE_BLOCK
    start = jnp.cumsum(counts) - counts
    pend = jnp.cumsum(padded)
    pstart = pend - padded
    rank = jnp.arange(n * TOP_K, dtype=jnp.int32) - start[sorted_e]
    dest = pstart[sorted_e] + rank
    n_blocks = -(-(n * TOP_K + N_EXPERTS * MOE_BLOCK) // MOE_BLOCK)
    n_slots = n_blocks * MOE_BLOCK
    slot_tok = jnp.full((n_slots,), n, jnp.int32).at[dest].set(flat_tok[order])
    slot_gate = jnp.zeros((n_slots,), h.dtype).at[dest].set(flat_gate[order])
    block_start = jnp.arange(n_blocks, dtype=jnp.int32) * MOE_BLOCK
    block_expert = jnp.minimum(jnp.searchsorted(pend, block_start, side='right'), N_EXPERTS - 1)
    hpad = jnp.concatenate([hf, jnp.zeros((1, d), h.dtype)], axis=0)
    xs = hpad[slot_tok].reshape(n_blocks, MOE_BLOCK, d)

    def expert_block(args):
        xb, e = args
        return swiglu(xb, w_gate[e], w_up[e], w_down[e])

    ys = lax.map(expert_block, (xs, block_expert)).reshape(n_slots, d)
    out = jnp.zeros((n + 1, d), h.dtype).at[slot_tok].add(ys * slot_gate[:, None])
    return out[:n].reshape(bsz, t, d)


def even_layer(x, norm_pre_mix, w_in, rwkv_p, gmlp_p, w_out, norm_post_mix,
               norm_pre_ffn, ffn_p, norm_post_ffn):
    p = rms_norm(x, norm_pre_mix) @ w_in
    ya = rwkv7_mix(p[..., :A_IN], *rwkv_p)
    yb = gmlp_mix(p[..., A_IN:], *gmlp_p)
    y = jnp.concatenate([ya, yb], axis=-1) @ w_out
    x = x + rms_norm(y, norm_post_mix)
    x = x + rms_norm(swiglu(rms_norm(x, norm_pre_ffn), *ffn_p), norm_post_ffn)
    return x


def odd_layer(x, norm_pre_mix, w_in, s5_p, m2_p, w_out, norm_post_mix,
              norm_pre_ffn, moe_p, norm_post_ffn):
    p = rms_norm(x, norm_pre_mix) @ w_in
    yc = s5_mix(p[..., :C_WIDTH], *s5_p)
    yd = mamba2_mix(p[..., C_WIDTH:], *m2_p)
    y = jnp.concatenate([yc, yd], axis=-1) @ w_out
    x = x + rms_norm(y, norm_post_mix)
    x = x + rms_norm(moe_swiglu(rms_norm(x, norm_pre_ffn), *moe_p), norm_post_ffn)
    return x


def setup_inputs(seed: int = 0) -> dict:
    key = jax.random.key(seed)
    ks = iter(jax.random.split(key, 80))
    f32 = jnp.float32

    def nrm(shape, scale):
        return scale * jax.random.normal(next(ks), shape, f32)

    def unif(shape, lo, hi):
        return jax.random.uniform(next(ks), shape, f32, lo, hi)

    def gain(shape):
        return 1.0 + nrm(shape, 0.02)

    D = D_MODEL
    x = nrm((BATCH, SEQ, D), 1.0)
    l0_norm_pre_mix = gain((D,))
    l0_w_in = nrm((D, EVEN_IN), D ** -0.5)
    l0_rwkv_mu = unif((A_IN,), 0.0, 1.0)
    ramp = jnp.arange(A_WIDTH, dtype=f32) / (A_WIDTH - 1)
    l0_rwkv_w0 = -6.5 + 5.0 * ramp ** 0.9 + nrm((A_WIDTH,), 0.01)
    l0_rwkv_w2 = nrm((A_LORA_W, A_WIDTH), 0.1 * A_LORA_W ** -0.5)
    l0_rwkv_a0 = nrm((A_WIDTH,), 0.1)
    l0_rwkv_a2 = nrm((A_LORA_A, A_WIDTH), 0.1 * A_LORA_A ** -0.5)
    l0_rwkv_g2 = nrm((A_LORA_G, A_WIDTH), A_LORA_G ** -0.5)
    l0_rwkv_k_k = 0.85 + nrm((A_WIDTH,), 0.02)
    l0_rwkv_k_a = 1.0 + nrm((A_WIDTH,), 0.02)
    l0_rwkv_r_k = nrm((A_HEADS, A_HEAD_DIM), 0.1)
    l0_rwkv_ln_w = gain((A_WIDTH,))
    l0_rwkv_ln_b = nrm((A_WIDTH,), 0.01)
    l0_gmlp_ln_w = gain((B_GROUPS, B_GROUP_DIM))
    l0_gmlp_ln_b = nrm((B_GROUPS, B_GROUP_DIM), 0.01)
    l0_gmlp_ws = nrm((B_GROUPS, B_CHUNK, B_CHUNK), B_CHUNK ** -0.5)
    l0_gmlp_bs = 1.0 + nrm((B_GROUPS, B_CHUNK), 0.02)
    l0_w_out = nrm((MIX_WIDTH, D), MIX_WIDTH ** -0.5)
    l0_norm_post_mix = gain((D,))
    l0_norm_pre_ffn = gain((D,))
    l0_ffn_w_gate = nrm((D, FFN_DENSE), D ** -0.5)
    l0_ffn_w_up = nrm((D, FFN_DENSE), D ** -0.5)
    l0_ffn_w_down = nrm((FFN_DENSE, D), FFN_DENSE ** -0.5)
    l0_norm_post_ffn = gain((D,))
    l1_norm_pre_mix = gain((D,))
    l1_w_in = nrm((D, ODD_IN), D ** -0.5)
    l1_s5_a_re = -0.5 + nrm((C_GROUPS, C_STATE), 0.01)
    l1_s5_a_im = jnp.pi * jnp.arange(C_STATE, dtype=f32)[None, :] + nrm((C_GROUPS, C_STATE), 0.01)
    l1_s5_log_dt = unif((C_GROUPS,), float(np.log(1e-3)), float(np.log(1e-1)))
    l1_s5_b_re = nrm((C_GROUPS, C_STATE, C_GROUP_CH), (2 * C_GROUP_CH) ** -0.5)
    l1_s5_b_im = nrm((C_GROUPS, C_STATE, C_GROUP_CH), (2 * C_GROUP_CH) ** -0.5)
    l1_s5_c_re = nrm((C_GROUPS, C_GROUP_CH, C_STATE), C_STATE ** -0.5)
    l1_s5_c_im = nrm((C_GROUPS, C_GROUP_CH, C_STATE), C_STATE ** -0.5)
    l1_s5_d = nrm((C_WIDTH,), 0.5)
    l1_s5_glu_w = nrm((C_WIDTH, C_WIDTH), C_WIDTH ** -0.5)
    l1_s5_glu_b = nrm((C_WIDTH,), 0.01)
    l1_m2_conv_w = nrm((D_CONV, D_XBC), D_CONV ** -0.5)
    l1_m2_conv_b = nrm((D_XBC,), 0.01)
    dt0 = jnp.exp(unif((D_HEADS,), float(np.log(1e-3)), float(np.log(1e-1))))
    l1_m2_dt_bias = dt0 + jnp.log(-jnp.expm1(-dt0))
    l1_m2_a_log = jnp.log(unif((D_HEADS,), 1.0, 16.0))
    l1_m2_d = gain((D_HEADS,))
    l1_m2_norm_w = gain((D_WIDTH,))
    l1_w_out = nrm((MIX_WIDTH, D), MIX_WIDTH ** -0.5)
    l1_norm_post_mix = gain((D,))
    l1_norm_pre_ffn = gain((D,))
    l1_moe_router = nrm((D, N_EXPERTS), D ** -0.5)
    l1_moe_w_gate = nrm((N_EXPERTS, D, FFN_EXPERT), D ** -0.5)
    l1_moe_w_up = nrm((N_EXPERTS, D, FFN_EXPERT), D ** -0.5)
    l1_moe_w_down = nrm((N_EXPERTS, FFN_EXPERT, D), FFN_EXPERT ** -0.5)
    l1_norm_post_ffn = gain((D,))
    return {
        'x': x,
        'l0_norm_pre_mix': l0_norm_pre_mix, 'l0_w_in': l0_w_in,
        'l0_rwkv_mu': l0_rwkv_mu, 'l0_rwkv_w0': l0_rwkv_w0, 'l0_rwkv_w2': l0_rwkv_w2,
        'l0_rwkv_a0': l0_rwkv_a0, 'l0_rwkv_a2': l0_rwkv_a2, 'l0_rwkv_g2': l0_rwkv_g2,
        'l0_rwkv_k_k': l0_rwkv_k_k, 'l0_rwkv_k_a': l0_rwkv_k_a, 'l0_rwkv_r_k': l0_rwkv_r_k,
        'l0_rwkv_ln_w': l0_rwkv_ln_w, 'l0_rwkv_ln_b': l0_rwkv_ln_b,
        'l0_gmlp_ln_w': l0_gmlp_ln_w, 'l0_gmlp_ln_b': l0_gmlp_ln_b,
        'l0_gmlp_ws': l0_gmlp_ws, 'l0_gmlp_bs': l0_gmlp_bs,
        'l0_w_out': l0_w_out, 'l0_norm_post_mix': l0_norm_post_mix,
        'l0_norm_pre_ffn': l0_norm_pre_ffn, 'l0_ffn_w_gate': l0_ffn_w_gate,
        'l0_ffn_w_up': l0_ffn_w_up, 'l0_ffn_w_down': l0_ffn_w_down,
        'l0_norm_post_ffn': l0_norm_post_ffn,
        'l1_norm_pre_mix': l1_norm_pre_mix, 'l1_w_in': l1_w_in,
        'l1_s5_a_re': l1_s5_a_re, 'l1_s5_a_im': l1_s5_a_im, 'l1_s5_log_dt': l1_s5_log_dt,
        'l1_s5_b_re': l1_s5_b_re, 'l1_s5_b_im': l1_s5_b_im,
        'l1_s5_c_re': l1_s5_c_re, 'l1_s5_c_im': l1_s5_c_im, 'l1_s5_d': l1_s5_d,
        'l1_s5_glu_w': l1_s5_glu_w, 'l1_s5_glu_b': l1_s5_glu_b,
        'l1_m2_conv_w': l1_m2_conv_w, 'l1_m2_conv_b': l1_m2_conv_b,
        'l1_m2_dt_bias': l1_m2_dt_bias, 'l1_m2_a_log': l1_m2_a_log,
        'l1_m2_d': l1_m2_d, 'l1_m2_norm_w': l1_m2_norm_w,
        'l1_w_out': l1_w_out, 'l1_norm_post_mix': l1_norm_post_mix,
        'l1_norm_pre_ffn': l1_norm_pre_ffn, 'l1_moe_router': l1_moe_router,
        'l1_moe_w_gate': l1_moe_w_gate, 'l1_moe_w_up': l1_moe_w_up,
        'l1_moe_w_down': l1_moe_w_down, 'l1_norm_post_ffn': l1_norm_post_ffn,
    }


def reference(x,
              l0_norm_pre_mix, l0_w_in,
              l0_rwkv_mu, l0_rwkv_w0, l0_rwkv_w2, l0_rwkv_a0, l0_rwkv_a2, l0_rwkv_g2,
              l0_rwkv_k_k, l0_rwkv_k_a, l0_rwkv_r_k, l0_rwkv_ln_w, l0_rwkv_ln_b,
              l0_gmlp_ln_w, l0_gmlp_ln_b, l0_gmlp_ws, l0_gmlp_bs,
              l0_w_out, l0_norm_post_mix, l0_norm_pre_ffn,
              l0_ffn_w_gate, l0_ffn_w_up, l0_ffn_w_down, l0_norm_post_ffn,
              l1_norm_pre_mix, l1_w_in,
              l1_s5_a_re, l1_s5_a_im, l1_s5_log_dt, l1_s5_b_re, l1_s5_b_im,
              l1_s5_c_re, l1_s5_c_im, l1_s5_d, l1_s5_glu_w, l1_s5_glu_b,
              l1_m2_conv_w, l1_m2_conv_b, l1_m2_dt_bias, l1_m2_a_log, l1_m2_d, l1_m2_norm_w,
              l1_w_out, l1_norm_post_mix, l1_norm_pre_ffn,
              l1_moe_router, l1_moe_w_gate, l1_moe_w_up, l1_moe_w_down, l1_norm_post_ffn):
    layer_params = (
        (l0_norm_pre_mix, l0_w_in,
         (l0_rwkv_mu, l0_rwkv_w0, l0_rwkv_w2, l0_rwkv_a0, l0_rwkv_a2, l0_rwkv_g2,
          l0_rwkv_k_k, l0_rwkv_k_a, l0_rwkv_r_k, l0_rwkv_ln_w, l0_rwkv_ln_b),
         (l0_gmlp_ln_w, l0_gmlp_ln_b, l0_gmlp_ws, l0_gmlp_bs),
         l0_w_out, l0_norm_post_mix, l0_norm_pre_ffn,
         (l0_ffn_w_gate, l0_ffn_w_up, l0_ffn_w_down), l0_norm_post_ffn),
        (l1_norm_pre_mix, l1_w_in,
         (l1_s5_a_re, l1_s5_a_im, l1_s5_log_dt, l1_s5_b_re, l1_s5_b_im,
          l1_s5_c_re, l1_s5_c_im, l1_s5_d, l1_s5_glu_w, l1_s5_glu_b),
         (l1_m2_conv_w, l1_m2_conv_b, l1_m2_dt_bias, l1_m2_a_log, l1_m2_d, l1_m2_norm_w),
         l1_w_out, l1_norm_post_mix, l1_norm_pre_ffn,
         (l1_moe_router, l1_moe_w_gate, l1_moe_w_up, l1_moe_w_down), l1_norm_post_ffn),
    )
    for i in range(DEPTH):
        if i % 2 == 0:
            x = even_layer(x, *layer_params[i])
        else:
            x = odd_layer(x, *layer_params[i])
    return x
```

```python
import functools

import jax
import jax.numpy as jnp
from jax import lax
from jax.experimental import pallas as pl
from jax.experimental.pallas import tpu as pltpu

F32 = jnp.float32
BF16 = jnp.bfloat16

EPS = 1e-6
RWKV_GN_EPS = 64e-5
RWKV_HEAD = 64
RWKV_CHUNK = 64
GMLP_CHUNK = 128
GMLP_GROUPS = 4
S5_GROUP_CH = 16
S5_STATE = 64
S5_CHUNK = 64
SSD_HEAD = 64
SSD_HEADS = 8
SSD_GROUPS = 2
SSD_STATE = 128
SSD_CONV = 4
SSD_CHUNK = 128
MOE_EXPERTS = 8
MOE_ROWS = 512
LANES = 128
SUBLANES = 8
VMEM_LIMIT = 56 * 1024 * 1024


def _cparams(sem):
    return pltpu.CompilerParams(dimension_semantics=sem, vmem_limit_bytes=VMEM_LIMIT)


def _bdot(a, b):
    return jnp.dot(a.astype(BF16), b.astype(BF16), preferred_element_type=F32)


def _bdot_nt(a, b):
    return lax.dot_general(a.astype(BF16), b.astype(BF16), (((1,), (1,)), ((), ())),
                           preferred_element_type=F32)


def _bdot_tn(a, b):
    return lax.dot_general(a.astype(BF16), b.astype(BF16), (((0,), (0,)), ((), ())),
                           preferred_element_type=F32)


def _split3(x):
    h = x.astype(BF16)
    r1 = x - h.astype(F32)
    m = r1.astype(BF16)
    l = (r1 - m.astype(F32)).astype(BF16)
    return h, m, l


def _dot_x_exact(x, e):
    h, m, l = _split3(x)
    e = e.astype(BF16)
    return (jnp.dot(h, e, preferred_element_type=F32) + jnp.dot(m, e, preferred_element_type=F32)
            + jnp.dot(l, e, preferred_element_type=F32))


def _dot_exact_x(e, x):
    h, m, l = _split3(x)
    e = e.astype(BF16)
    return (jnp.dot(e, h, preferred_element_type=F32) + jnp.dot(e, m, preferred_element_type=F32)
            + jnp.dot(e, l, preferred_element_type=F32))


def _rms(x, g):
    return x * lax.rsqrt(jnp.mean(x * x, axis=-1, keepdims=True) + EPS) * g


def _sigmoid(x):
    return 1.0 / (1.0 + jnp.exp(-x))


def _silu(x):
    return x * _sigmoid(x)


def _softplus(x):
    return jnp.maximum(x, 0.0) + jnp.log(1.0 + jnp.exp(-jnp.abs(x)))


def _gelu_tanh(x):
    return 0.5 * x * (1.0 + jnp.tanh(0.7978845608028654 * (x + 0.044715 * x * x * x)))


def _iota2(shape, dim):
    return lax.broadcasted_iota(jnp.int32, shape, dim)


def _norm_proj_kernel(n_out, x_ref, g_ref, *refs):
    w_refs = refs[:n_out]
    o_refs = refs[n_out:]
    xn = _rms(x_ref[...], g_ref[...]).astype(BF16)
    for w_ref, o_ref in zip(w_refs, o_refs):
        o_ref[...] = jnp.dot(xn, w_ref[...], preferred_element_type=F32)


def _norm_proj(x2, g, ws, tm=512):
    n, d = x2.shape
    in_specs = [pl.BlockSpec((tm, d), lambda i: (i, 0)), pl.BlockSpec((1, d), lambda i: (0, 0))]
    in_specs += [pl.BlockSpec(w.shape, lambda i: (0, 0)) for w in ws]
    out_specs = [pl.BlockSpec((tm, w.shape[1]), lambda i: (i, 0)) for w in ws]
    out_shape = [jax.ShapeDtypeStruct((n, w.shape[1]), F32) for w in ws]
    return pl.pallas_call(
        functools.partial(_norm_proj_kernel, len(ws)),
        grid=(n // tm,), in_specs=in_specs, out_specs=out_specs, out_shape=out_shape,
        compiler_params=_cparams(("parallel",)), name="norm_proj",
    )(x2, g.reshape(1, d), *ws)


def _rwkv_kernel(heads, p_ref, pp_ref, mu_ref, w0_ref, w2_ref, a0_ref, a2_ref, g2_ref, kk_ref,
                 ka_ref, rk_ref, lnw_ref, lnb_ref, gs_ref, o_ref, z_ref, y_ref):
    c = pl.program_id(1)
    ln = RWKV_CHUNK
    hd = RWKV_HEAD
    aw = heads * hd

    @pl.when(c == 0)
    def _():
        z_ref[...] = jnp.zeros_like(z_ref)

    p = p_ref[0]
    prev = jnp.where(c == 0, 0.0, pp_ref[0][SUBLANES - 1:SUBLANES, :])
    rows = _iota2((ln, 1), 0)
    ps = jnp.where(rows == 0, prev, pltpu.roll(p, 1, axis=0))
    pm = p + (ps - p) * mu_ref[...]
    r = pm[:, 0:aw]
    k = pm[:, aw:2 * aw]
    v = pm[:, 2 * aw:3 * aw]
    xw = pm[:, 3 * aw:3 * aw + LANES]
    xa = pm[:, 3 * aw + LANES:3 * aw + 2 * LANES]
    xg = pm[:, 3 * aw + 2 * LANES:3 * aw + 3 * LANES]

    w = w0_ref[...] + _bdot(jnp.tanh(xw), w2_ref[...])
    w = -_softplus(-w) - 0.5
    lw = -jnp.exp(w)
    a = _sigmoid(a0_ref[...] + _bdot(xa, a2_ref[...]))
    g = _bdot(_sigmoid(xg), g2_ref[...])

    gs = gs_ref[...]
    kk = k * kk_ref[...]
    kk = kk / jnp.maximum(jnp.sqrt(_dot_x_exact(kk * kk, gs)), 1e-12)
    kmod = k * (1.0 + (a - 1.0) * ka_ref[...])
    bonus = _dot_x_exact(r * kmod * rk_ref[...], gs)
    avec = -kk
    bvec = kk * a

    tril = _iota2((ln, ln), 0) >= _iota2((ln, ln), 1)
    stril = _iota2((ln, ln), 0) > _iota2((ln, ln), 1)
    eye = _iota2((ln, ln), 0) == _iota2((ln, ln), 1)
    cs = _dot_exact_x(jnp.where(tril, 1.0, 0.0), lw)
    cs_last = cs[ln - 1:ln, :]
    ecs = jnp.exp(cs)
    encs = jnp.exp(-cs)
    dec_end = jnp.exp(cs_last - cs)
    rt = r * ecs
    kt = kmod * encs
    bt = bvec * encs
    at = avec * jnp.exp(cs - lw)
    bh = bvec * dec_end
    kh = kmod * dec_end
    wl = jnp.exp(cs_last)
    eye_f = jnp.where(eye, 1.0, 0.0)

    for h in range(heads):
        sl = slice(h * hd, (h + 1) * hd)
        at_h, rt_h, bt_h, kt_h, v_h = at[:, sl], rt[:, sl], bt[:, sl], kt[:, sl], v[:, sl]
        lhs = jnp.concatenate([at_h, rt_h], axis=0)
        ab = _bdot_nt(lhs, bt_h)
        ak = _bdot_nt(lhs, kt_h)
        nmat = jnp.where(stril, ab[:ln], 0.0)
        tinv = eye_f + nmat
        npow = nmat
        for _ in range(5):
            npow = _bdot(npow, npow)
            tinv = tinv + _bdot(tinv, npow)
        akv = _bdot(jnp.where(stril, ak[:ln], 0.0), v_h)
        pmat = _bdot(tinv, at_h)
        qmat = _bdot(tinv, akv)
        arb = jnp.where(tril, ab[ln:], 0.0)
        ark = jnp.where(tril, ak[ln:], 0.0)
        rp = rt_h + _bdot(arb, pmat)
        y0 = _bdot(arb, qmat) + _bdot(ark, v_h)
        z = z_ref[h]
        u = _bdot(pmat, z) + qmat
        y_ref[:, sl] = _bdot(rp, z) + y0
        wl_col = jnp.sum(jnp.where(eye, wl[:, sl], 0.0), axis=1, keepdims=True)
        z_ref[h] = wl_col * z + _bdot_tn(bh[:, sl], u) + _bdot_tn(kh[:, sl], v_h)

    y = y_ref[...]
    inv = 1.0 / hd
    mean = _dot_x_exact(y, gs) * inv
    d = y - mean
    var = _dot_x_exact(d * d, gs) * inv
    yn = d * lax.rsqrt(var + RWKV_GN_EPS) * lnw_ref[...] + lnb_ref[...]
    o_ref[0] = (yn + bonus * v) * g


def _rwkv_mix(p_a, prm, heads):
    b, t, cin = p_a.shape
    aw = heads * RWKV_HEAD
    ln = RWKV_CHUNK
    nc = t // ln
    full = lambda arr: pl.BlockSpec(arr.shape, lambda i, j: (0,) * arr.ndim)
    in_specs = [pl.BlockSpec((1, ln, cin), lambda i, j: (i, j, 0)),
                pl.BlockSpec((1, SUBLANES, cin),
                             lambda i, j: (i, jnp.maximum(j * (ln // SUBLANES) - 1, 0), 0))]
    in_specs += [full(x) for x in prm]
    return pl.pallas_call(
        functools.partial(_rwkv_kernel, heads),
        grid=(b, nc), in_specs=in_specs,
        out_specs=pl.BlockSpec((1, ln, aw), lambda i, j: (i, j, 0)),
        out_shape=jax.ShapeDtypeStruct((b, t, aw), F32),
        scratch_shapes=[pltpu.VMEM((heads, RWKV_HEAD, RWKV_HEAD), F32), pltpu.VMEM((ln, aw), F32)],
        compiler_params=_cparams(("parallel", "arbitrary")), name="rwkv7",
    )(p_a, p_a, *prm)


def _gmlp_kernel(p_ref, lnw_ref, lnb_ref, ws_ref, bs_ref, o_ref):
    x = _gelu_tanh(p_ref[0])
    ln = GMLP_CHUNK
    bw = x.shape[1] // 2
    gd = bw // GMLP_GROUPS
    tril = _iota2((ln, ln), 0) >= _iota2((ln, ln), 1)
    for gi in range(GMLP_GROUPS):
        u = x[:, gi * gd:(gi + 1) * gd]
        v = x[:, bw + gi * gd:bw + (gi + 1) * gd]
        mean = jnp.mean(v, axis=-1, keepdims=True)
        d = v - mean
        var = jnp.mean(d * d, axis=-1, keepdims=True)
        vn = d * lax.rsqrt(var + EPS) * lnw_ref[gi:gi + 1, :] + lnb_ref[gi:gi + 1, :]
        s = _bdot(jnp.where(tril, ws_ref[gi], 0.0), vn) + bs_ref[gi]
        o_ref[0, :, gi * gd:(gi + 1) * gd] = u * s


def _gmlp_mix(p_b, ln_w, ln_b, ws, bs):
    b, t, cin = p_b.shape
    bw = cin // 2
    gd = bw // GMLP_GROUPS
    ln = GMLP_CHUNK
    bs_b = jnp.broadcast_to(bs[:, :, None], (GMLP_GROUPS, ln, gd))
    full = lambda arr: pl.BlockSpec(arr.shape, lambda i, j: (0,) * arr.ndim)
    return pl.pallas_call(
        _gmlp_kernel, grid=(b, t // ln),
        in_specs=[pl.BlockSpec((1, ln, cin), lambda i, j: (i, j, 0)),
                  full(ln_w), full(ln_b), full(ws), full(bs_b)],
        out_specs=pl.BlockSpec((1, ln, bw), lambda i, j: (i, j, 0)),
        out_shape=jax.ShapeDtypeStruct((b, t, bw), F32),
        compiler_params=_cparams(("parallel", "parallel")), name="gmlp",
    )(p_b, ln_w, ln_b, ws, bs_b)


def _mix_out0_kernel(x_ref, ya_ref, yb_ref, wa_ref, wb_ref, g_ref, o_ref):
    y = _bdot(ya_ref[...], wa_ref[...]) + _bdot(yb_ref[...], wb_ref[...])
    o_ref[...] = x_ref[...] + _rms(y, g_ref[...])


def _mix_out0(x2, ya, yb, wa, wb, g, tm=512):
    n, d = x2.shape
    row = lambda arr: pl.BlockSpec((tm, arr.shape[1]), lambda i: (i, 0))
    full = lambda arr: pl.BlockSpec(arr.shape, lambda i: (0,) * arr.ndim)
    return pl.pallas_call(
        _mix_out0_kernel, grid=(n // tm,),
        in_specs=[row(x2), row(ya), row(yb), full(wa), full(wb), full(g)],
        out_specs=row(x2), out_shape=jax.ShapeDtypeStruct((n, d), F32),
        compiler_params=_cparams(("parallel",)), name="mix_out0",
    )(x2, ya, yb, wa, wb, g)


def _ffn_kernel(x_ref, gpre_ref, wg_ref, wu_ref, wd_ref, gpost_ref, o_ref, h_ref, acc_ref):
    f = pl.program_id(1)

    @pl.when(f == 0)
    def _():
        h_ref[...] = _rms(x_ref[...], gpre_ref[...]).astype(BF16)
        acc_ref[...] = jnp.zeros_like(acc_ref)

    h = h_ref[...]
    gate = jnp.dot(h, wg_ref[...], preferred_element_type=F32)
    up = jnp.dot(h, wu_ref[...], preferred_element_type=F32)
    acc_ref[...] += jnp.dot((_silu(gate) * up).astype(BF16), wd_ref[...], preferred_element_type=F32)

    @pl.when(f == pl.num_programs(1) - 1)
    def _():
        o_ref[...] = x_ref[...] + _rms(acc_ref[...], gpost_ref[...])


def _ffn(x2, gpre, wg, wu, wd, gpost, tm=512, tf=1408):
    n, d = x2.shape
    ff = wg.shape[1]
    return pl.pallas_call(
        _ffn_kernel, grid=(n // tm, ff // tf),
        in_specs=[pl.BlockSpec((tm, d), lambda i, j: (i, 0)),
                  pl.BlockSpec((1, d), lambda i, j: (0, 0)),
                  pl.BlockSpec((d, tf), lambda i, j: (0, j)),
                  pl.BlockSpec((d, tf), lambda i, j: (0, j)),
                  pl.BlockSpec((tf, d), lambda i, j: (j, 0)),
                  pl.BlockSpec((1, d), lambda i, j: (0, 0))],
        out_specs=pl.BlockSpec((tm, d), lambda i, j: (i, 0)),
        out_shape=jax.ShapeDtypeStruct((n, d), F32),
        scratch_shapes=[pltpu.VMEM((tm, d), BF16), pltpu.VMEM((tm, d), F32)],
        compiler_params=_cparams(("parallel", "arbitrary")), name="ffn",
    )(x2, gpre, wg, wu, wd, gpost)


def _s5_kernel(nc, nb, u_ref, wt_ref, wsr_ref, wsi_ref, wcr_ref, wci_ref, alr_ref, ali_ref, d_ref, o_ref):
    u = u_ref[0]
    y = jnp.dot(u, wt_ref[0], preferred_element_type=F32)
    xer = jnp.dot(u, wsr_ref[0], preferred_element_type=F32)
    xei = jnp.dot(u, wsi_ref[0], preferred_element_type=F32)
    alr = alr_ref[0]
    ali = ali_ref[0]
    cr = jnp.zeros((nb, xer.shape[1]), F32)
    ci = jnp.zeros((nb, xer.shape[1]), F32)
    prs, pis = [], []
    for c in range(nc):
        prs.append(cr)
        pis.append(ci)
        er = xer[c * nb:(c + 1) * nb]
        ei = xei[c * nb:(c + 1) * nb]
        cr, ci = alr * cr - ali * ci + er, alr * ci + ali * cr + ei
    pr = jnp.concatenate(prs, axis=0)
    pi = jnp.concatenate(pis, axis=0)
    y = y + _bdot(pr, wcr_ref[0]) + _bdot(pi, wci_ref[0])
    o_ref[0] = y + d_ref[0] * u.astype(F32)


def _s5_weights(a_re, a_im, log_dt, b_re, b_im, c_re, c_im, d_skip, ln):
    g, st = a_re.shape
    ch = b_re.shape[2]
    dt = jnp.exp(log_dt)[:, None]
    lr, li = a_re, a_im
    tau = jnp.arange(ln + 1, dtype=F32)[:, None, None]
    mag = jnp.exp(lr[None] * dt[None] * tau)
    pw_r = mag * jnp.cos(li[None] * dt[None] * tau)
    pw_i = mag * jnp.sin(li[None] * dt[None] * tau)
    ab_r, ab_i = pw_r[1], pw_i[1]
    nr, ni = ab_r - 1.0, ab_i
    den = lr * lr + li * li
    fr, fi = (nr * lr + ni * li) / den, (ni * lr - nr * li) / den
    bb_r = fr[..., None] * b_re - fi[..., None] * b_im
    bb_i = fr[..., None] * b_im + fi[..., None] * b_re
    cp_r = c_re[None] * pw_r[:ln, :, None, :] - c_im[None] * pw_i[:ln, :, None, :]
    cp_i = c_re[None] * pw_i[:ln, :, None, :] + c_im[None] * pw_r[:ln, :, None, :]
    hp = lax.Precision.HIGHEST
    taps = (jnp.einsum('tgcp,gpd->gtdc', cp_r, bb_r, precision=hp)
            - jnp.einsum('tgcp,gpd->gtdc', cp_i, bb_i, precision=hp))
    s_idx = jnp.arange(ln)[:, None]
    t_idx = jnp.arange(ln)[None, :]
    lag = t_idx - s_idx
    toep = jnp.where((lag >= 0)[None, :, :, None, None], taps[:, jnp.clip(lag, 0, ln - 1)], 0.0)
    toep = toep.transpose(0, 1, 3, 2, 4).reshape(g, ln * ch, ln * ch)
    rev_r, rev_i = pw_r[:ln][::-1], pw_i[:ln][::-1]
    ws_r = rev_r[..., None] * bb_r[None] - rev_i[..., None] * bb_i[None]
    ws_i = rev_r[..., None] * bb_i[None] + rev_i[..., None] * bb_r[None]
    ws_r = ws_r.transpose(1, 0, 3, 2).reshape(g, ln * ch, st)
    ws_i = ws_i.transpose(1, 0, 3, 2).reshape(g, ln * ch, st)
    q_r, q_i = pw_r[1:ln + 1], pw_i[1:ln + 1]
    wc_r = c_re[None] * q_r[:, :, None, :] - c_im[None] * q_i[:, :, None, :]
    wc_i = -(c_re[None] * q_i[:, :, None, :] + c_im[None] * q_r[:, :, None, :])
    wc_r = wc_r.transpose(1, 3, 0, 2).reshape(g, st, ln * ch)
    wc_i = wc_i.transpose(1, 3, 0, 2).reshape(g, st, ln * ch)
    al_r = pw_r[ln].reshape(g, 1, st)
    al_i = pw_i[ln].reshape(g, 1, st)
    d_t = jnp.tile(d_skip.reshape(g, 1, ch), (1, ln, 1)).reshape(g, 1, ln * ch)
    return (toep.astype(BF16), ws_r.astype(BF16), ws_i.astype(BF16), wc_r.astype(BF16),
            wc_i.astype(BF16), al_r, al_i, d_t)


def _s5_core(u, weights):
    b, t, cw = u.shape
    ln, ch = S5_CHUNK, S5_GROUP_CH
    g = cw // ch
    nc = t // ln
    ug = u.astype(BF16).reshape(b, nc, ln, g, ch).transpose(3, 1, 0, 2, 4).reshape(g, nc * b, ln * ch)
    per_g = lambda arr: pl.BlockSpec((1,) + arr.shape[1:], lambda i: (i, 0, 0))
    yg = pl.pallas_call(
        functools.partial(_s5_kernel, nc, b), grid=(g,),
        in_specs=[per_g(ug)] + [per_g(w) for w in weights],
        out_specs=per_g(ug), out_shape=jax.ShapeDtypeStruct(ug.shape, F32),
        compiler_params=_cparams(("parallel",)), name="s5",
    )(ug, *weights)
    return yg.reshape(g, nc, b, ln, ch).transpose(2, 1, 3, 0, 4).reshape(b, t, cw)


def _ssd_kernel(z_ref, xbc_ref, xp_ref, dt_ref, cw_ref, cb_ref, dtb_ref, alog_ref, dsk_ref, nw_ref,
                o_ref, s_ref, y_ref):
    c = pl.program_id(1)
    ln = SSD_CHUNK
    hd = SSD_HEAD
    dw = SSD_HEADS * hd
    gn = SSD_STATE

    @pl.when(c == 0)
    def _():
        s_ref[...] = jnp.zeros_like(s_ref)

    xbc = xbc_ref[0]
    prev = jnp.where(c == 0, 0.0, xp_ref[0])
    full = jnp.concatenate([prev, xbc], axis=0)
    conv = cb_ref[...]
    for j in range(SSD_CONV):
        off = SUBLANES - (SSD_CONV - 1) + j
        conv = conv + cw_ref[j:j + 1, :] * full[off:off + ln]
    act = _silu(conv)
    xh = act[:, :dw]
    dt = _softplus(dt_ref[0] + dtb_ref[...])
    adt = -jnp.exp(alog_ref[...]) * dt
    tril = _iota2((ln, ln), 0) >= _iota2((ln, ln), 1)
    acs = _dot_exact_x(jnp.where(tril, 1.0, 0.0), adt)
    acs_t = acs.T
    tot = acs[ln - 1:ln, :]
    hg = SSD_HEADS // SSD_GROUPS
    for gi in range(SSD_GROUPS):
        bm = act[:, dw + gi * gn:dw + (gi + 1) * gn]
        cm = act[:, dw + SSD_GROUPS * gn + gi * gn:dw + SSD_GROUPS * gn + (gi + 1) * gn]
        cb = _bdot_nt(cm, bm)
        for hh in range(hg):
            h = gi * hg + hh
            sl = slice(h * hd, (h + 1) * hd)
            col = acs[:, h:h + 1]
            rowv = acs_t[h:h + 1, :]
            lmat = jnp.exp(jnp.where(tril, col - rowv, -jnp.inf))
            xh_h = xh[:, sl]
            xdt = xh_h * dt[:, h:h + 1]
            tot_h = tot[:, h:h + 1]
            st = s_ref[h]
            y_h = _bdot(cb * lmat, xdt) + jnp.exp(col) * _bdot(cm, st)
            s_ref[h] = jnp.exp(tot_h) * st + _bdot_tn(bm * jnp.exp(tot_h - col), xdt)
            y_ref[:, sl] = y_h + dsk_ref[:, sl] * xh_h
    y = y_ref[...] * _silu(z_ref[0])
    gw = dw // SSD_GROUPS
    for gi in range(SSD_GROUPS):
        yg = y[:, gi * gw:(gi + 1) * gw]
        yg = yg * lax.rsqrt(jnp.mean(yg * yg, axis=-1, keepdims=True) + EPS)
        o_ref[0, :, gi * gw:(gi + 1) * gw] = yg * nw_ref[:, gi * gw:(gi + 1) * gw]


def _ssd_mix(z, xbc, dtp, conv_w, conv_b, dt_bias, a_log, d_skip, norm_w):
    b, t, dw = z.shape
    ln = SSD_CHUNK
    xw = xbc.shape[2]
    pad = lambda vec: jnp.pad(vec, (0, LANES - vec.shape[0])).reshape(1, LANES)
    dsk = jnp.repeat(d_skip, SSD_HEAD).reshape(1, dw)
    prm = [conv_w, conv_b.reshape(1, xw), pad(dt_bias), pad(a_log), dsk, norm_w.reshape(1, dw)]
    full = lambda arr: pl.BlockSpec(arr.shape, lambda i, j: (0,) * arr.ndim)
    blk = lambda w: pl.BlockSpec((1, ln, w), lambda i, j: (i, j, 0))
    return pl.pallas_call(
        _ssd_kernel, grid=(b, t // ln),
        in_specs=[blk(dw), blk(xw),
                  pl.BlockSpec((1, SUBLANES, xw),
                               lambda i, j: (i, jnp.maximum(j * (ln // SUBLANES) - 1, 0), 0)),
                  blk(LANES)] + [full(x) for x in prm],
        out_specs=blk(dw), out_shape=jax.ShapeDtypeStruct((b, t, dw), F32),
        scratch_shapes=[pltpu.VMEM((SSD_HEADS, SSD_STATE, SSD_HEAD), F32), pltpu.VMEM((ln, dw), F32)],
        compiler_params=_cparams(("parallel", "arbitrary")), name="ssd",
    )(z, xbc, xbc, dtp, *prm)


def _mix_out1_kernel(x_ref, yc_ref, yd_ref, gw_ref, gb_ref, wc_ref, wd_ref, gpost_ref, gpre_ref,
                     wrh_ref, wrl_ref, x1_ref, h_ref, idx_ref, gate_ref):
    yc = _gelu_tanh(yc_ref[...])
    yc = yc * _sigmoid(_bdot(yc, gw_ref[...]) + gb_ref[...])
    y = _bdot(yc, wc_ref[...]) + _bdot(yd_ref[...], wd_ref[...])
    x1 = x_ref[...] + _rms(y, gpost_ref[...])
    x1_ref[...] = x1
    h = _rms(x1, gpre_ref[...])
    h_ref[...] = h
    hh = h.astype(BF16)
    hl = (h - hh.astype(F32)).astype(BF16)
    wrh = wrh_ref[...]
    logits = (jnp.dot(hh, wrh, preferred_element_type=F32) + jnp.dot(hl, wrh, preferred_element_type=F32)
              + jnp.dot(hh, wrl_ref[...], preferred_element_type=F32))
    lane = _iota2(logits.shape, 1)
    lane_f = lane.astype(F32)
    logits = jnp.where(lane < MOE_EXPERTS, logits, -jnp.inf)
    m1 = jnp.max(logits, axis=-1, keepdims=True)
    i1 = jnp.min(jnp.where(logits == m1, lane_f, float(LANES)), axis=-1, keepdims=True)
    rest = jnp.where(lane_f == i1, -jnp.inf, logits)
    m2 = jnp.max(rest, axis=-1, keepdims=True)
    i2 = jnp.min(jnp.where(rest == m2, lane_f, float(LANES)), axis=-1, keepdims=True)
    e2 = jnp.exp(m2 - m1)
    g1 = 1.0 / (1.0 + e2)
    g2 = e2 / (1.0 + e2)
    idx_ref[...] = jnp.where(lane == 0, i1, jnp.where(lane == 1, i2, 0.0)).astype(jnp.int32)
    gate_ref[...] = jnp.where(lane == 0, g1, jnp.where(lane == 1, g2, 0.0))


def _mix_out1(x2, yc, yd, glu_w, glu_b, wc, wd, gpost, gpre, wr, tm=512):
    n, d = x2.shape
    wr_p = jnp.pad(wr, ((0, 0), (0, LANES - wr.shape[1])))
    wrh = wr_p.astype(BF16)
    wrl = (wr_p - wrh.astype(F32)).astype(BF16)
    row = lambda w: pl.BlockSpec((tm, w), lambda i: (i, 0))
    full = lambda arr: pl.BlockSpec(arr.shape, lambda i: (0,) * arr.ndim)
    prm = [glu_w, glu_b, wc, wd, gpost, gpre, wrh, wrl]
    return pl.pallas_call(
        _mix_out1_kernel, grid=(n // tm,),
        in_specs=[row(d), row(yc.shape[1]), row(yd.shape[1])] + [full(p) for p in prm],
        out_specs=[row(d), row(d), row(LANES), row(LANES)],
        out_shape=[jax.ShapeDtypeStruct((n, d), F32), jax.ShapeDtypeStruct((n, d), F32),
                   jax.ShapeDtypeStruct((n, LANES), jnp.int32), jax.ShapeDtypeStruct((n, LANES), F32)],
        compiler_params=_cparams(("parallel",)), name="mix_out1",
    )(x2, yc, yd, *prm)


def _moe_kernel(be_ref, tok_ref, nact_ref, h_hbm, wg_ref, wu_ref, wd_ref, o_ref, xs_ref, xb_ref,
                acc_ref, sem):
    i = pl.program_id(0)
    f = pl.program_id(1)
    tm = xs_ref.shape[0]
    active = i < nact_ref[0]

    def row_copy(r):
        tok = tok_ref[i * tm + r]
        return pltpu.make_async_copy(h_hbm.at[pl.ds(tok, 1)], xs_ref.at[pl.ds(r, 1)], sem)

    @pl.when(jnp.logical_and(active, f == 0))
    def _():
        def start(r, carry):
            row_copy(r).start()
            return carry

        def wait(r, carry):
            row_copy(r).wait()
            return carry

        lax.fori_loop(0, tm, start, 0)
        lax.fori_loop(0, tm, wait, 0)
        xb_ref[...] = xs_ref[...].astype(BF16)
        acc_ref[...] = jnp.zeros_like(acc_ref)

    @pl.when(active)
    def _():
        x = xb_ref[...]
        gate = jnp.dot(x, wg_ref[0], preferred_element_type=F32)
        up = jnp.dot(x, wu_ref[0], preferred_element_type=F32)
        acc_ref[...] += jnp.dot((_silu(gate) * up).astype(BF16), wd_ref[0], preferred_element_type=F32)

    @pl.when(f == pl.num_programs(1) - 1)
    def _():
        o_ref[...] = jnp.where(active, acc_ref[...], 0.0)


def _moe_experts(h, block_expert, slot_tok, nact, wg, wu, wd, tf=896):
    n, d = h.shape
    tm = MOE_ROWS
    n_slots = slot_tok.shape[0]
    ff = wg.shape[2]
    grid_spec = pltpu.PrefetchScalarGridSpec(
        num_scalar_prefetch=3, grid=(n_slots // tm, ff // tf),
        in_specs=[pl.BlockSpec(memory_space=pl.ANY),
                  pl.BlockSpec((1, d, tf), lambda i, j, be, tok, na: (be[i], 0, j)),
                  pl.BlockSpec((1, d, tf), lambda i, j, be, tok, na: (be[i], 0, j)),
                  pl.BlockSpec((1, tf, d), lambda i, j, be, tok, na: (be[i], j, 0))],
        out_specs=pl.BlockSpec((tm, d), lambda i, j, be, tok, na: (i, 0)),
        scratch_shapes=[pltpu.VMEM((tm, d), F32), pltpu.VMEM((tm, d), BF16), pltpu.VMEM((tm, d), F32),
                        pltpu.SemaphoreType.DMA(())])
    return pl.pallas_call(
        _moe_kernel, grid_spec=grid_spec, out_shape=jax.ShapeDtypeStruct((n_slots, d), F32),
        compiler_params=_cparams(("arbitrary", "arbitrary")), name="moe_experts",
    )(block_expert, slot_tok, nact, h, wg, wu, wd)


def _combine_kernel(pos_ref, ys_hbm, x_ref, gate_ref, gpost_ref, o_ref, y1_ref, y2_ref, sem):
    i = pl.program_id(0)
    tm = y1_ref.shape[0]

    def copies(r):
        p1 = pos_ref[2 * (i * tm + r)]
        p2 = pos_ref[2 * (i * tm + r) + 1]
        return (pltpu.make_async_copy(ys_hbm.at[pl.ds(p1, 1)], y1_ref.at[pl.ds(r, 1)], sem.at[0]),
                pltpu.make_async_copy(ys_hbm.at[pl.ds(p2, 1)], y2_ref.at[pl.ds(r, 1)], sem.at[1]))

    def start(r, carry):
        c1, c2 = copies(r)
        c1.start()
        c2.start()
        return carry

    def wait(r, carry):
        c1, c2 = copies(r)
        c1.wait()
        c2.wait()
        return carry

    lax.fori_loop(0, tm, start, 0)
    lax.fori_loop(0, tm, wait, 0)
    gates = gate_ref[...]
    y = gates[:, 0:1] * y1_ref[...] + gates[:, 1:2] * y2_ref[...]
    o_ref[...] = x_ref[...] + _rms(y, gpost_ref[...])


def _moe_combine(x1, ys, pos, gates, gpost, tm=256):
    n, d = x1.shape
    grid_spec = pltpu.PrefetchScalarGridSpec(
        num_scalar_prefetch=1, grid=(n // tm,),
        in_specs=[pl.BlockSpec(memory_space=pl.ANY),
                  pl.BlockSpec((tm, d), lambda i, pos: (i, 0)),
                  pl.BlockSpec((tm, LANES), lambda i, pos: (i, 0)),
                  pl.BlockSpec((1, d), lambda i, pos: (0, 0))],
        out_specs=pl.BlockSpec((tm, d), lambda i, pos: (i, 0)),
        scratch_shapes=[pltpu.VMEM((tm, d), F32), pltpu.VMEM((tm, d), F32),
                        pltpu.SemaphoreType.DMA((2,))])
    return pl.pallas_call(
        _combine_kernel, grid_spec=grid_spec, out_shape=jax.ShapeDtypeStruct((n, d), F32),
        compiler_params=_cparams(("arbitrary",)), name="moe_combine",
    )(pos, ys, x1, gates, gpost)


def _moe_plan(idx, n):
    tm = MOE_ROWS
    flat_e = idx[:, :2].reshape(-1)
    onehot = (flat_e[:, None] == jnp.arange(MOE_EXPERTS, dtype=jnp.int32)[None, :]).astype(jnp.int32)
    csum = jnp.cumsum(onehot, axis=0)
    counts = csum[-1]
    rank = jnp.sum((csum - onehot) * onehot, axis=1)
    padded = (counts + tm - 1) // tm * tm
    pend = jnp.cumsum(padded)
    pstart = pend - padded
    dest = (jnp.sum(onehot * pstart[None, :], axis=1) + rank).astype(jnp.int32)
    n_blocks = (2 * n) // tm + MOE_EXPERTS
    n_slots = n_blocks * tm
    flat_tok = jnp.arange(2 * n, dtype=jnp.int32) // 2
    slot_tok = jnp.zeros((n_slots,), jnp.int32).at[dest].set(flat_tok)
    block_start = jnp.arange(n_blocks, dtype=jnp.int32) * tm
    block_expert = jnp.minimum(jnp.searchsorted(pend, block_start, side='right'),
                               MOE_EXPERTS - 1).astype(jnp.int32)
    nact = (pend[-1] // tm).astype(jnp.int32).reshape(1)
    return block_expert, slot_tok, nact, dest


def kernel(x, l0_norm_pre_mix, l0_w_in, l0_rwkv_mu, l0_rwkv_w0, l0_rwkv_w2, l0_rwkv_a0, l0_rwkv_a2, l0_rwkv_g2, l0_rwkv_k_k, l0_rwkv_k_a, l0_rwkv_r_k, l0_rwkv_ln_w, l0_rwkv_ln_b, l0_gmlp_ln_w, l0_gmlp_ln_b, l0_gmlp_ws, l0_gmlp_bs, l0_w_out, l0_norm_post_mix, l0_norm_pre_ffn, l0_ffn_w_gate, l0_ffn_w_up, l0_ffn_w_down, l0_norm_post_ffn, l1_norm_pre_mix, l1_w_in, l1_s5_a_re, l1_s5_a_im, l1_s5_log_dt, l1_s5_b_re, l1_s5_b_im, l1_s5_c_re, l1_s5_c_im, l1_s5_d, l1_s5_glu_w, l1_s5_glu_b, l1_m2_conv_w, l1_m2_conv_b, l1_m2_dt_bias, l1_m2_a_log, l1_m2_d, l1_m2_norm_w, l1_w_out, l1_norm_post_mix, l1_norm_pre_ffn, l1_moe_router, l1_moe_w_gate, l1_moe_w_up, l1_moe_w_down, l1_norm_post_ffn):
    b, t, d = x.shape
    n = b * t
    x2 = x.reshape(n, d)
    row = lambda vec: vec.reshape(1, -1)

    aw = l0_rwkv_w0.shape[0]
    heads = aw // RWKV_HEAD
    lw_, la_, lg_ = l0_rwkv_w2.shape[0], l0_rwkv_a2.shape[0], l0_rwkv_g2.shape[0]
    a_in = 3 * aw + lw_ + la_ + lg_
    padc = lambda m, wdt: jnp.pad(m, ((0, 0), (0, LANES - wdt)))
    o = 3 * aw
    w_a = jnp.concatenate([l0_w_in[:, :o], padc(l0_w_in[:, o:o + lw_], lw_),
                           padc(l0_w_in[:, o + lw_:o + lw_ + la_], la_),
                           padc(l0_w_in[:, o + lw_ + la_:a_in], lg_)], axis=1).astype(BF16)
    w_b = l0_w_in[:, a_in:].astype(BF16)
    p_a, p_b = _norm_proj(x2, l0_norm_pre_mix, [w_a, w_b])
    padv = lambda vec, wdt: jnp.pad(vec, (0, LANES - wdt))
    mu = l0_rwkv_mu
    mu_p = jnp.concatenate([mu[:o], padv(mu[o:o + lw_], lw_), padv(mu[o + lw_:o + lw_ + la_], la_),
                            padv(mu[o + lw_ + la_:], lg_)])
    padr = lambda m: jnp.pad(m, ((0, LANES - m.shape[0]), (0, 0))).astype(BF16)
    hid = jnp.arange(aw, dtype=jnp.int32) // RWKV_HEAD
    gsum = (hid[:, None] == hid[None, :]).astype(BF16)
    rwkv_prm = [row(mu_p), row(l0_rwkv_w0), padr(l0_rwkv_w2), row(l0_rwkv_a0), padr(l0_rwkv_a2),
                padr(l0_rwkv_g2), row(l0_rwkv_k_k), row(l0_rwkv_k_a), row(l0_rwkv_r_k),
                row(l0_rwkv_ln_w), row(l0_rwkv_ln_b), gsum]
    ya = _rwkv_mix(p_a.reshape(b, t, -1), rwkv_prm, heads)
    yb = _gmlp_mix(p_b.reshape(b, t, -1), l0_gmlp_ln_w, l0_gmlp_ln_b, l0_gmlp_ws, l0_gmlp_bs)
    wo = l0_w_out.astype(BF16)
    x2 = _mix_out0(x2, ya.reshape(n, -1), yb.reshape(n, -1), wo[:aw], wo[aw:], row(l0_norm_post_mix))
    x2 = _ffn(x2, row(l0_norm_pre_ffn), l0_ffn_w_gate.astype(BF16), l0_ffn_w_up.astype(BF16),
              l0_ffn_w_down.astype(BF16), row(l0_norm_post_ffn))

    cw = l1_s5_d.shape[0]
    dw = l1_m2_norm_w.shape[0]
    xw = l1_m2_conv_w.shape[1]
    nh = l1_m2_dt_bias.shape[0]
    w1 = l1_w_in
    w_parts = [w1[:, :cw], w1[:, cw:cw + dw], w1[:, cw + dw:cw + dw + xw],
               padc(w1[:, cw + dw + xw:], nh)]
    u_c, z_d, xbc, dtp = _norm_proj(x2, l1_norm_pre_mix, [w.astype(BF16) for w in w_parts])
    s5_w = _s5_weights(l1_s5_a_re, l1_s5_a_im, l1_s5_log_dt, l1_s5_b_re, l1_s5_b_im, l1_s5_c_re,
                       l1_s5_c_im, l1_s5_d, S5_CHUNK)
    yc = _s5_core(u_c.reshape(b, t, cw), s5_w)
    yd = _ssd_mix(z_d.reshape(b, t, dw), xbc.reshape(b, t, xw), dtp.reshape(b, t, LANES),
                  l1_m2_conv_w, l1_m2_conv_b, l1_m2_dt_bias, l1_m2_a_log, l1_m2_d, l1_m2_norm_w)
    wo1 = l1_w_out.astype(BF16)
    x1, h, idx, gates = _mix_out1(x2, yc.reshape(n, cw), yd.reshape(n, dw), l1_s5_glu_w.astype(BF16),
                                  row(l1_s5_glu_b), wo1[:cw], wo1[cw:], row(l1_norm_post_mix),
                                  row(l1_norm_pre_ffn), l1_moe_router)
    block_expert, slot_tok, nact, dest = _moe_plan(idx, n)
    ys = _moe_experts(h, block_expert, slot_tok, nact, l1_moe_w_gate.astype(BF16),
                      l1_moe_w_up.astype(BF16), l1_moe_w_down.astype(BF16))
    out = _moe_combine(x1, ys, dest, gates, row(l1_norm_post_ffn))
    return out.reshape(b, t, d)
```

```python
import functools

import jax
import jax.numpy as jnp
from jax import lax
from jax.experimental import pallas as pl
from jax.experimental.pallas import tpu as pltpu

F32 = jnp.float32
BF16 = jnp.bfloat16

EPS = 1e-6
RWKV_GN_EPS = 64e-5
RWKV_HEAD = 64
RWKV_CHUNK = 64
GMLP_CHUNK = 128
GMLP_GROUPS = 4
S5_GROUP_CH = 16
S5_STATE = 64
S5_CHUNK = 64
SSD_HEAD = 64
SSD_HEADS = 8
SSD_GROUPS = 2
SSD_STATE = 128
SSD_CONV = 4
SSD_CHUNK = 128
MOE_EXPERTS = 8
MOE_ROWS = 512
LANES = 128
SUBLANES = 8
VMEM_LIMIT = 56 * 1024 * 1024


def _cparams(sem):
    return pltpu.CompilerParams(dimension_semantics=sem, vmem_limit_bytes=VMEM_LIMIT)


def _bdot(a, b):
    return jnp.dot(a.astype(BF16), b.astype(BF16), preferred_element_type=F32)


def _bdot_nt(a, b):
    return lax.dot_general(a.astype(BF16), b.astype(BF16), (((1,), (1,)), ((), ())),
                           preferred_element_type=F32)


def _bdot_tn(a, b):
    return lax.dot_general(a.astype(BF16), b.astype(BF16), (((0,), (0,)), ((), ())),
                           preferred_element_type=F32)


def _split3(x):
    h = x.astype(BF16)
    r1 = x - h.astype(F32)
    m = r1.astype(BF16)
    l = (r1 - m.astype(F32)).astype(BF16)
    return h, m, l


def _dot_x_exact(x, e):
    h, m, l = _split3(x)
    e = e.astype(BF16)
    return (jnp.dot(h, e, preferred_element_type=F32) + jnp.dot(m, e, preferred_element_type=F32)
            + jnp.dot(l, e, preferred_element_type=F32))


def _dot_exact_x(e, x):
    h, m, l = _split3(x)
    e = e.astype(BF16)
    return (jnp.dot(e, h, preferred_element_type=F32) + jnp.dot(e, m, preferred_element_type=F32)
            + jnp.dot(e, l, preferred_element_type=F32))


def _rms(x, g):
    return x * lax.rsqrt(jnp.mean(x * x, axis=-1, keepdims=True) + EPS) * g


def _sigmoid(x):
    return 1.0 / (1.0 + jnp.exp(-x))


def _silu(x):
    return x * _sigmoid(x)


def _softplus(x):
    return jnp.maximum(x, 0.0) + jnp.log(1.0 + jnp.exp(-jnp.abs(x)))


def _gelu_tanh(x):
    return 0.5 * x * (1.0 + jnp.tanh(0.7978845608028654 * (x + 0.044715 * x * x * x)))


def _iota2(shape, dim):
    return lax.broadcasted_iota(jnp.int32, shape, dim)


def _norm_proj_kernel(n_out, x_ref, g_ref, *refs):
    w_refs = refs[:n_out]
    o_refs = refs[n_out:]
    xn = _rms(x_ref[...], g_ref[...]).astype(BF16)
    for w_ref, o_ref in zip(w_refs, o_refs):
        o_ref[...] = jnp.dot(xn, w_ref[...], preferred_element_type=F32)


def _norm_proj(x2, g, ws, tm=512):
    n, d = x2.shape
    in_specs = [pl.BlockSpec((tm, d), lambda i: (i, 0)), pl.BlockSpec((1, d), lambda i: (0, 0))]
    in_specs += [pl.BlockSpec(w.shape, lambda i: (0, 0)) for w in ws]
    out_specs = [pl.BlockSpec((tm, w.shape[1]), lambda i: (i, 0)) for w in ws]
    out_shape = [jax.ShapeDtypeStruct((n, w.shape[1]), F32) for w in ws]
    return pl.pallas_call(
        functools.partial(_norm_proj_kernel, len(ws)),
        grid=(n // tm,), in_specs=in_specs, out_specs=out_specs, out_shape=out_shape,
        compiler_params=_cparams(("parallel",)), name="norm_proj",
    )(x2, g.reshape(1, d), *ws)


def _rwkv_kernel(heads, nb, p_ref, pp_ref, mu_ref, w0_ref, w2_ref, a0_ref, a2_ref, g2_ref, kk_ref,
                 ka_ref, rk_ref, lnw_ref, lnb_ref, gs_ref, o_ref, z_ref):
    c = pl.program_id(1)
    ln = RWKV_CHUNK
    hd = RWKV_HEAD
    aw = heads * hd

    @pl.when(c == 0)
    def _():
        z_ref[...] = jnp.zeros_like(z_ref)

    tril = _iota2((ln, ln), 0) >= _iota2((ln, ln), 1)
    stril = _iota2((ln, ln), 0) > _iota2((ln, ln), 1)
    eye = _iota2((ln, ln), 0) == _iota2((ln, ln), 1)
    eye_f = jnp.where(eye, 1.0, 0.0)
    tril_f = jnp.where(tril, 1.0, 0.0)
    rows = _iota2((ln, 1), 0)
    gs = gs_ref[...]

    per_b = []
    for bi in range(nb):
        p = p_ref[bi]
        prev = jnp.where(c == 0, 0.0, pp_ref[bi][SUBLANES - 1:SUBLANES, :])
        ps = jnp.where(rows == 0, prev, pltpu.roll(p, 1, axis=0))
        pm = p + (ps - p) * mu_ref[...]
        r = pm[:, 0:aw]
        k = pm[:, aw:2 * aw]
        v = pm[:, 2 * aw:3 * aw]
        xw = pm[:, 3 * aw:3 * aw + LANES]
        xa = pm[:, 3 * aw + LANES:3 * aw + 2 * LANES]
        xg = pm[:, 3 * aw + 2 * LANES:3 * aw + 3 * LANES]
        w = w0_ref[...] + _bdot(jnp.tanh(xw), w2_ref[...])
        w = -_softplus(-w) - 0.5
        lw = -jnp.exp(w)
        a = _sigmoid(a0_ref[...] + _bdot(xa, a2_ref[...]))
        g = _bdot(_sigmoid(xg), g2_ref[...])
        kk = k * kk_ref[...]
        kk = kk / jnp.maximum(jnp.sqrt(_dot_x_exact(kk * kk, gs)), 1e-12)
        kmod = k * (1.0 + (a - 1.0) * ka_ref[...])
        bonus = _dot_x_exact(r * kmod * rk_ref[...], gs)
        bvec = kk * a
        cs = _dot_exact_x(tril_f, lw)
        cs_last = cs[ln - 1:ln, :]
        encs = jnp.exp(-cs)
        dec_end = jnp.exp(cs_last - cs)
        per_b.append(dict(
            v=v, g=g, bonus=bonus, rt=r * jnp.exp(cs), kt=kmod * encs, bt=bvec * encs,
            at=-kk * jnp.exp(cs - lw), bh=bvec * dec_end, kh=kmod * dec_end, wl=jnp.exp(cs_last)))

    chains = [(bi, h) for bi in range(nb) for h in range(heads)]
    cut = lambda name: [per_b[bi][name][:, h * hd:(h + 1) * hd] for bi, h in chains]
    at, rt, bt, kt, vv, bh, kh, wl = (cut(n) for n in ("at", "rt", "bt", "kt", "v", "bh", "kh", "wl"))
    z_all = z_ref[...]
    zs = [z_all[bi, h] for bi, h in chains]
    nch = range(len(chains))
    lhs = [jnp.concatenate([at[i], rt[i]], axis=0) for i in nch]
    ab = [_bdot_nt(lhs[i], bt[i]) for i in nch]
    ak = [_bdot_nt(lhs[i], kt[i]) for i in nch]
    nmat = [jnp.where(stril, ab[i][:ln], 0.0) for i in nch]
    akv = [_bdot(jnp.where(stril, ak[i][:ln], 0.0), vv[i]) for i in nch]
    arkv = [_bdot(jnp.where(tril, ak[i][ln:], 0.0), vv[i]) for i in nch]
    tinv = [eye_f + nmat[i] for i in nch]
    npow = [_bdot(nmat[i], nmat[i]) for i in nch]
    for step in range(5):
        tinv = [tinv[i] + _bdot(tinv[i], npow[i]) for i in nch]
        if step < 4:
            npow = [_bdot(npow[i], npow[i]) for i in nch]
    pmat = [_bdot(tinv[i], at[i]) for i in nch]
    qmat = [_bdot(tinv[i], akv[i]) for i in nch]
    arb = [jnp.where(tril, ab[i][ln:], 0.0) for i in nch]
    rp = [rt[i] + _bdot(arb[i], pmat[i]) for i in nch]
    y0 = [_bdot(arb[i], qmat[i]) + arkv[i] for i in nch]
    u = [_bdot(pmat[i], zs[i]) + qmat[i] for i in nch]
    ys = [_bdot(rp[i], zs[i]) + y0[i] for i in nch]
    wl_col = [jnp.sum(jnp.where(eye, wl[i], 0.0), axis=1, keepdims=True) for i in nch]
    z_new = [wl_col[i] * zs[i] + _bdot_tn(bh[i], u[i]) + _bdot_tn(kh[i], vv[i]) for i in nch]
    z_ref[...] = jnp.stack(z_new, axis=0).reshape(z_ref.shape)

    inv = 1.0 / hd
    for bi in range(nb):
        y = jnp.concatenate(ys[bi * heads:(bi + 1) * heads], axis=1)
        mean = _dot_x_exact(y, gs) * inv
        d = y - mean
        var = _dot_x_exact(d * d, gs) * inv
        yn = d * lax.rsqrt(var + RWKV_GN_EPS) * lnw_ref[...] + lnb_ref[...]
        o_ref[bi] = (yn + per_b[bi]["bonus"] * per_b[bi]["v"]) * per_b[bi]["g"]


def _rwkv_mix(p_a, prm, heads, nb=2):
    b, t, cin = p_a.shape
    aw = heads * RWKV_HEAD
    ln = RWKV_CHUNK
    nc = t // ln
    full = lambda arr: pl.BlockSpec(arr.shape, lambda i, j: (0,) * arr.ndim)
    in_specs = [pl.BlockSpec((nb, ln, cin), lambda i, j: (i, j, 0)),
                pl.BlockSpec((nb, SUBLANES, cin),
                             lambda i, j: (i, jnp.maximum(j * (ln // SUBLANES) - 1, 0), 0))]
    in_specs += [full(x) for x in prm]
    return pl.pallas_call(
        functools.partial(_rwkv_kernel, heads, nb),
        grid=(b // nb, nc), in_specs=in_specs,
        out_specs=pl.BlockSpec((nb, ln, aw), lambda i, j: (i, j, 0)),
        out_shape=jax.ShapeDtypeStruct((b, t, aw), F32),
        scratch_shapes=[pltpu.VMEM((nb, heads, RWKV_HEAD, RWKV_HEAD), F32)],
        compiler_params=_cparams(("parallel", "arbitrary")), name="rwkv7",
    )(p_a, p_a, *prm)


def _gmlp_kernel(p_ref, lnw_ref, lnb_ref, ws_ref, bs_ref, o_ref):
    x = _gelu_tanh(p_ref[0])
    ln = GMLP_CHUNK
    bw = x.shape[1] // 2
    gd = bw // GMLP_GROUPS
    tril = _iota2((ln, ln), 0) >= _iota2((ln, ln), 1)
    for gi in range(GMLP_GROUPS):
        u = x[:, gi * gd:(gi + 1) * gd]
        v = x[:, bw + gi * gd:bw + (gi + 1) * gd]
        mean = jnp.mean(v, axis=-1, keepdims=True)
        d = v - mean
        var = jnp.mean(d * d, axis=-1, keepdims=True)
        vn = d * lax.rsqrt(var + EPS) * lnw_ref[gi:gi + 1, :] + lnb_ref[gi:gi + 1, :]
        s = _bdot(jnp.where(tril, ws_ref[gi], 0.0), vn) + bs_ref[gi]
        o_ref[0, :, gi * gd:(gi + 1) * gd] = u * s


def _gmlp_mix(p_b, ln_w, ln_b, ws, bs):
    b, t, cin = p_b.shape
    bw = cin // 2
    gd = bw // GMLP_GROUPS
    ln = GMLP_CHUNK
    bs_b = jnp.broadcast_to(bs[:, :, None], (GMLP_GROUPS, ln, gd))
    full = lambda arr: pl.BlockSpec(arr.shape, lambda i, j: (0,) * arr.ndim)
    return pl.pallas_call(
        _gmlp_kernel, grid=(b, t // ln),
        in_specs=[pl.BlockSpec((1, ln, cin), lambda i, j: (i, j, 0)),
                  full(ln_w), full(ln_b), full(ws), full(bs_b)],
        out_specs=pl.BlockSpec((1, ln, bw), lambda i, j: (i, j, 0)),
        out_shape=jax.ShapeDtypeStruct((b, t, bw), F32),
        compiler_params=_cparams(("parallel", "parallel")), name="gmlp",
    )(p_b, ln_w, ln_b, ws, bs_b)


def _mix_out0_kernel(x_ref, ya_ref, yb_ref, wa_ref, wb_ref, g_ref, o_ref):
    y = _bdot(ya_ref[...], wa_ref[...]) + _bdot(yb_ref[...], wb_ref[...])
    o_ref[...] = x_ref[...] + _rms(y, g_ref[...])


def _mix_out0(x2, ya, yb, wa, wb, g, tm=512):
    n, d = x2.shape
    row = lambda arr: pl.BlockSpec((tm, arr.shape[1]), lambda i: (i, 0))
    full = lambda arr: pl.BlockSpec(arr.shape, lambda i: (0,) * arr.ndim)
    return pl.pallas_call(
        _mix_out0_kernel, grid=(n // tm,),
        in_specs=[row(x2), row(ya), row(yb), full(wa), full(wb), full(g)],
        out_specs=row(x2), out_shape=jax.ShapeDtypeStruct((n, d), F32),
        compiler_params=_cparams(("parallel",)), name="mix_out0",
    )(x2, ya, yb, wa, wb, g)


def _ffn_kernel(x_ref, gpre_ref, wg_ref, wu_ref, wd_ref, gpost_ref, o_ref, h_ref, acc_ref):
    f = pl.program_id(1)

    @pl.when(f == 0)
    def _():
        h_ref[...] = _rms(x_ref[...], gpre_ref[...]).astype(BF16)
        acc_ref[...] = jnp.zeros_like(acc_ref)

    h = h_ref[...]
    gate = jnp.dot(h, wg_ref[...], preferred_element_type=F32)
    up = jnp.dot(h, wu_ref[...], preferred_element_type=F32)
    acc_ref[...] += jnp.dot((_silu(gate) * up).astype(BF16), wd_ref[...], preferred_element_type=F32)

    @pl.when(f == pl.num_programs(1) - 1)
    def _():
        o_ref[...] = x_ref[...] + _rms(acc_ref[...], gpost_ref[...])


def _ffn(x2, gpre, wg, wu, wd, gpost, tm=512, tf=1408):
    n, d = x2.shape
    ff = wg.shape[1]
    return pl.pallas_call(
        _ffn_kernel, grid=(n // tm, ff // tf),
        in_specs=[pl.BlockSpec((tm, d), lambda i, j: (i, 0)),
                  pl.BlockSpec((1, d), lambda i, j: (0, 0)),
                  pl.BlockSpec((d, tf), lambda i, j: (0, j)),
                  pl.BlockSpec((d, tf), lambda i, j: (0, j)),
                  pl.BlockSpec((tf, d), lambda i, j: (j, 0)),
                  pl.BlockSpec((1, d), lambda i, j: (0, 0))],
        out_specs=pl.BlockSpec((tm, d), lambda i, j: (i, 0)),
        out_shape=jax.ShapeDtypeStruct((n, d), F32),
        scratch_shapes=[pltpu.VMEM((tm, d), BF16), pltpu.VMEM((tm, d), F32)],
        compiler_params=_cparams(("parallel", "arbitrary")), name="ffn",
    )(x2, gpre, wg, wu, wd, gpost)


def _s5_kernel(nc, nb, u_ref, tap_ref, wsr_ref, wsi_ref, wcr_ref, wci_ref, alr_ref, ali_ref, d_ref, o_ref,
               toep_ref):
    u = u_ref[0]
    taps = tap_ref[0]
    ch = taps.shape[0]
    lane = _iota2(taps.shape, 1)
    for s in range(taps.shape[1] // ch):
        blk = taps if s == 0 else jnp.where(lane >= ch * s, pltpu.roll(taps, ch * s, axis=1), 0.0)
        toep_ref[ch * s:ch * (s + 1), :] = blk.astype(BF16)
    y = jnp.dot(u, toep_ref[...], preferred_element_type=F32)
    xer = jnp.dot(u, wsr_ref[0], preferred_element_type=F32)
    xei = jnp.dot(u, wsi_ref[0], preferred_element_type=F32)
    alr = alr_ref[0]
    ali = ali_ref[0]
    cr = jnp.zeros((nb, xer.shape[1]), F32)
    ci = jnp.zeros((nb, xer.shape[1]), F32)
    prs, pis = [], []
    for c in range(nc):
        prs.append(cr)
        pis.append(ci)
        er = xer[c * nb:(c + 1) * nb]
        ei = xei[c * nb:(c + 1) * nb]
        cr, ci = alr * cr - ali * ci + er, alr * ci + ali * cr + ei
    pr = jnp.concatenate(prs, axis=0)
    pi = jnp.concatenate(pis, axis=0)
    y = y + _bdot(pr, wcr_ref[0]) + _bdot(pi, wci_ref[0])
    o_ref[0] = y + d_ref[0] * u.astype(F32)


def _s5_weights(a_re, a_im, log_dt, b_re, b_im, c_re, c_im, d_skip, ln):
    g, st = a_re.shape
    ch = b_re.shape[2]
    dt = jnp.exp(log_dt)[:, None]
    lr, li = a_re, a_im
    tau = jnp.arange(ln + 1, dtype=F32)[:, None, None]
    mag = jnp.exp(lr[None] * dt[None] * tau)
    pw_r = mag * jnp.cos(li[None] * dt[None] * tau)
    pw_i = mag * jnp.sin(li[None] * dt[None] * tau)
    ab_r, ab_i = pw_r[1], pw_i[1]
    nr, ni = ab_r - 1.0, ab_i
    den = lr * lr + li * li
    fr, fi = (nr * lr + ni * li) / den, (ni * lr - nr * li) / den
    bb_r = fr[..., None] * b_re - fi[..., None] * b_im
    bb_i = fr[..., None] * b_im + fi[..., None] * b_re
    cp_r = c_re[None] * pw_r[:ln, :, None, :] - c_im[None] * pw_i[:ln, :, None, :]
    cp_i = c_re[None] * pw_i[:ln, :, None, :] + c_im[None] * pw_r[:ln, :, None, :]
    hp = lax.Precision.HIGHEST
    taps = (jnp.einsum('tgcp,gpd->gdtc', cp_r, bb_r, precision=hp)
            - jnp.einsum('tgcp,gpd->gdtc', cp_i, bb_i, precision=hp))
    taps = taps.reshape(g, ch, ln * ch)
    rev_r, rev_i = pw_r[:ln][::-1], pw_i[:ln][::-1]
    ws_r = rev_r[..., None] * bb_r[None] - rev_i[..., None] * bb_i[None]
    ws_i = rev_r[..., None] * bb_i[None] + rev_i[..., None] * bb_r[None]
    ws_r = ws_r.transpose(1, 0, 3, 2).reshape(g, ln * ch, st)
    ws_i = ws_i.transpose(1, 0, 3, 2).reshape(g, ln * ch, st)
    q_r, q_i = pw_r[1:ln + 1], pw_i[1:ln + 1]
    wc_r = c_re[None] * q_r[:, :, None, :] - c_im[None] * q_i[:, :, None, :]
    wc_i = -(c_re[None] * q_i[:, :, None, :] + c_im[None] * q_r[:, :, None, :])
    wc_r = wc_r.transpose(1, 3, 0, 2).reshape(g, st, ln * ch)
    wc_i = wc_i.transpose(1, 3, 0, 2).reshape(g, st, ln * ch)
    al_r = pw_r[ln].reshape(g, 1, st)
    al_i = pw_i[ln].reshape(g, 1, st)
    d_t = jnp.tile(d_skip.reshape(g, 1, ch), (1, ln, 1)).reshape(g, 1, ln * ch)
    return (taps, ws_r.astype(BF16), ws_i.astype(BF16), wc_r.astype(BF16),
            wc_i.astype(BF16), al_r, al_i, d_t)


def _s5_core(u, weights):
    b, t, cw = u.shape
    ln, ch = S5_CHUNK, S5_GROUP_CH
    g = cw // ch
    nc = t // ln
    ug = u.astype(BF16).reshape(b, nc, ln, g, ch).transpose(3, 1, 0, 2, 4).reshape(g, nc * b, ln * ch)
    per_g = lambda arr: pl.BlockSpec((1,) + arr.shape[1:], lambda i: (i, 0, 0))
    yg = pl.pallas_call(
        functools.partial(_s5_kernel, nc, b), grid=(g,),
        in_specs=[per_g(ug)] + [per_g(w) for w in weights],
        out_specs=per_g(ug), out_shape=jax.ShapeDtypeStruct(ug.shape, F32),
        scratch_shapes=[pltpu.VMEM((ln * ch, ln * ch), BF16)],
        compiler_params=_cparams(("parallel",)), name="s5",
    )(ug, *weights)
    return yg.reshape(g, nc, b, ln, ch).transpose(2, 1, 3, 0, 4).reshape(b, t, cw)


def _ssd_kernel(z_ref, xbc_ref, xp_ref, dt_ref, cw_ref, cb_ref, dtb_ref, alog_ref, dsk_ref, nw_ref,
                o_ref, s_ref):
    c = pl.program_id(1)
    ln = SSD_CHUNK
    hd = SSD_HEAD
    dw = SSD_HEADS * hd
    gn = SSD_STATE

    @pl.when(c == 0)
    def _():
        s_ref[...] = jnp.zeros_like(s_ref)

    xbc = xbc_ref[0]
    prev = jnp.where(c == 0, 0.0, xp_ref[0])
    full = jnp.concatenate([prev, xbc], axis=0)
    conv = cb_ref[...]
    for j in range(SSD_CONV):
        off = SUBLANES - (SSD_CONV - 1) + j
        conv = conv + cw_ref[j:j + 1, :] * full[off:off + ln]
    act = _silu(conv)
    xh = act[:, :dw]
    dt = _softplus(dt_ref[0] + dtb_ref[...])
    adt = -jnp.exp(alog_ref[...]) * dt
    tril = _iota2((ln, ln), 0) >= _iota2((ln, ln), 1)
    acs = _dot_exact_x(jnp.where(tril, 1.0, 0.0), adt)
    acs_t = acs.T
    tot = acs[ln - 1:ln, :]
    hg = SSD_HEADS // SSD_GROUPS
    s_all = s_ref[...]
    y_heads, s_heads = [], []
    for gi in range(SSD_GROUPS):
        bm = act[:, dw + gi * gn:dw + (gi + 1) * gn]
        cm = act[:, dw + SSD_GROUPS * gn + gi * gn:dw + SSD_GROUPS * gn + (gi + 1) * gn]
        cb = _bdot_nt(cm, bm)
        for hh in range(hg):
            h = gi * hg + hh
            sl = slice(h * hd, (h + 1) * hd)
            col = acs[:, h:h + 1]
            rowv = acs_t[h:h + 1, :]
            lmat = jnp.exp(jnp.where(tril, col - rowv, -jnp.inf))
            xh_h = xh[:, sl]
            xdt = xh_h * dt[:, h:h + 1]
            tot_h = tot[:, h:h + 1]
            st = s_all[h]
            y_h = _bdot(cb * lmat, xdt) + jnp.exp(col) * _bdot(cm, st)
            s_heads.append(jnp.exp(tot_h) * st + _bdot_tn(bm * jnp.exp(tot_h - col), xdt))
            y_heads.append(y_h + dsk_ref[:, sl] * xh_h)
    s_ref[...] = jnp.stack(s_heads, axis=0)
    y = jnp.concatenate(y_heads, axis=1) * _silu(z_ref[0])
    gw = dw // SSD_GROUPS
    for gi in range(SSD_GROUPS):
        yg = y[:, gi * gw:(gi + 1) * gw]
        yg = yg * lax.rsqrt(jnp.mean(yg * yg, axis=-1, keepdims=True) + EPS)
        o_ref[0, :, gi * gw:(gi + 1) * gw] = yg * nw_ref[:, gi * gw:(gi + 1) * gw]


def _ssd_mix(z, xbc, dtp, conv_w, conv_b, dt_bias, a_log, d_skip, norm_w):
    b, t, dw = z.shape
    ln = SSD_CHUNK
    xw = xbc.shape[2]
    pad = lambda vec: jnp.pad(vec, (0, LANES - vec.shape[0])).reshape(1, LANES)
    dsk = jnp.repeat(d_skip, SSD_HEAD).reshape(1, dw)
    prm = [conv_w, conv_b.reshape(1, xw), pad(dt_bias), pad(a_log), dsk, norm_w.reshape(1, dw)]
    full = lambda arr: pl.BlockSpec(arr.shape, lambda i, j: (0,) * arr.ndim)
    blk = lambda w: pl.BlockSpec((1, ln, w), lambda i, j: (i, j, 0))
    return pl.pallas_call(
        _ssd_kernel, grid=(b, t // ln),
        in_specs=[blk(dw), blk(xw),
                  pl.BlockSpec((1, SUBLANES, xw),
                               lambda i, j: (i, jnp.maximum(j * (ln // SUBLANES) - 1, 0), 0)),
                  blk(LANES)] + [full(x) for x in prm],
        out_specs=blk(dw), out_shape=jax.ShapeDtypeStruct((b, t, dw), F32),
        scratch_shapes=[pltpu.VMEM((SSD_HEADS, SSD_STATE, SSD_HEAD), F32)],
        compiler_params=_cparams(("parallel", "arbitrary")), name="ssd",
    )(z, xbc, xbc, dtp, *prm)


def _mix_out1_kernel(x_ref, yc_ref, yd_ref, gw_ref, gb_ref, wc_ref, wd_ref, gpost_ref, gpre_ref,
                     wrh_ref, wrl_ref, x1_ref, h_ref, idx_ref, gate_ref):
    yc = _gelu_tanh(yc_ref[...])
    yc = yc * _sigmoid(_bdot(yc, gw_ref[...]) + gb_ref[...])
    y = _bdot(yc, wc_ref[...]) + _bdot(yd_ref[...], wd_ref[...])
    x1 = x_ref[...] + _rms(y, gpost_ref[...])
    x1_ref[...] = x1
    h = _rms(x1, gpre_ref[...])
    h_ref[...] = h
    hh = h.astype(BF16)
    hl = (h - hh.astype(F32)).astype(BF16)
    wrh = wrh_ref[...]
    logits = (jnp.dot(hh, wrh, preferred_element_type=F32) + jnp.dot(hl, wrh, preferred_element_type=F32)
              + jnp.dot(hh, wrl_ref[...], preferred_element_type=F32))
    lane = _iota2(logits.shape, 1)
    lane_f = lane.astype(F32)
    logits = jnp.where(lane < MOE_EXPERTS, logits, -jnp.inf)
    m1 = jnp.max(logits, axis=-1, keepdims=True)
    i1 = jnp.min(jnp.where(logits == m1, lane_f, float(LANES)), axis=-1, keepdims=True)
    rest = jnp.where(lane_f == i1, -jnp.inf, logits)
    m2 = jnp.max(rest, axis=-1, keepdims=True)
    i2 = jnp.min(jnp.where(rest == m2, lane_f, float(LANES)), axis=-1, keepdims=True)
    e2 = jnp.exp(m2 - m1)
    g1 = 1.0 / (1.0 + e2)
    g2 = e2 / (1.0 + e2)
    idx_ref[...] = jnp.where(lane == 0, i1, jnp.where(lane == 1, i2, 0.0)).astype(jnp.int32)
    gate_ref[...] = jnp.where(lane == 0, g1, jnp.where(lane == 1, g2, 0.0))


def _mix_out1(x2, yc, yd, glu_w, glu_b, wc, wd, gpost, gpre, wr, tm=512):
    n, d = x2.shape
    wr_p = jnp.pad(wr, ((0, 0), (0, LANES - wr.shape[1])))
    wrh = wr_p.astype(BF16)
    wrl = (wr_p - wrh.astype(F32)).astype(BF16)
    row = lambda w: pl.BlockSpec((tm, w), lambda i: (i, 0))
    full = lambda arr: pl.BlockSpec(arr.shape, lambda i: (0,) * arr.ndim)
    prm = [glu_w, glu_b, wc, wd, gpost, gpre, wrh, wrl]
    return pl.pallas_call(
        _mix_out1_kernel, grid=(n // tm,),
        in_specs=[row(d), row(yc.shape[1]), row(yd.shape[1])] + [full(p) for p in prm],
        out_specs=[row(d), row(d), row(LANES), row(LANES)],
        out_shape=[jax.ShapeDtypeStruct((n, d), F32), jax.ShapeDtypeStruct((n, d), F32),
                   jax.ShapeDtypeStruct((n, LANES), jnp.int32), jax.ShapeDtypeStruct((n, LANES), F32)],
        compiler_params=_cparams(("parallel",)), name="mix_out1",
    )(x2, yc, yd, *prm)


GATHER_UNROLL = 8


def _gather_rows(n_rows, make_copy):
    def body(j, carry):
        for q in range(GATHER_UNROLL):
            make_copy(j * GATHER_UNROLL + q).start(priority=q % 2)
        return carry

    lax.fori_loop(0, n_rows // GATHER_UNROLL, body, 0)


def _moe_kernel(be_ref, tok_ref, nact_ref, h_hbm, wg_ref, wu_ref, wd_ref, o_ref, xs_ref, xb_ref,
                acc_ref, sem):
    i = pl.program_id(0)
    f = pl.program_id(1)
    tm = xs_ref.shape[1]
    nact = nact_ref[0]
    active = i < nact
    slot = lax.rem(i, 2)

    def gather(block, sl):
        def row_copy(r):
            tok = tok_ref[block * tm + r]
            return pltpu.make_async_copy(h_hbm.at[pl.ds(tok, 1)], xs_ref.at[sl, pl.ds(r, 1)], sem.at[sl])
        _gather_rows(tm, row_copy)

    @pl.when(jnp.logical_and(active, f == 0))
    def _():
        @pl.when(i == 0)
        def _():
            gather(0, 0)

        pltpu.make_async_copy(h_hbm.at[pl.ds(0, tm)], xs_ref.at[slot], sem.at[slot]).wait()

        @pl.when(i + 1 < nact)
        def _():
            gather(i + 1, 1 - slot)

        xb_ref[...] = xs_ref[slot].astype(BF16)
        acc_ref[...] = jnp.zeros_like(acc_ref)

    @pl.when(active)
    def _():
        x = xb_ref[...]
        gate = jnp.dot(x, wg_ref[0], preferred_element_type=F32)
        up = jnp.dot(x, wu_ref[0], preferred_element_type=F32)
        acc_ref[...] += jnp.dot((_silu(gate) * up).astype(BF16), wd_ref[0], preferred_element_type=F32)

    @pl.when(f == pl.num_programs(1) - 1)
    def _():
        o_ref[...] = jnp.where(active, acc_ref[...], 0.0)


def _moe_experts(h, block_expert, slot_tok, nact, wg, wu, wd, tf=896):
    n, d = h.shape
    tm = MOE_ROWS
    n_slots = slot_tok.shape[0]
    ff = wg.shape[2]
    grid_spec = pltpu.PrefetchScalarGridSpec(
        num_scalar_prefetch=3, grid=(n_slots // tm, ff // tf),
        in_specs=[pl.BlockSpec(memory_space=pl.ANY),
                  pl.BlockSpec((1, d, tf), lambda i, j, be, tok, na: (be[i], 0, j)),
                  pl.BlockSpec((1, d, tf), lambda i, j, be, tok, na: (be[i], 0, j)),
                  pl.BlockSpec((1, tf, d), lambda i, j, be, tok, na: (be[i], j, 0))],
        out_specs=pl.BlockSpec((tm, d), lambda i, j, be, tok, na: (i, 0)),
        scratch_shapes=[pltpu.VMEM((2, tm, d), F32), pltpu.VMEM((tm, d), BF16), pltpu.VMEM((tm, d), F32),
                        pltpu.SemaphoreType.DMA((2,))])
    return pl.pallas_call(
        _moe_kernel, grid_spec=grid_spec, out_shape=jax.ShapeDtypeStruct((n_slots, d), F32),
        compiler_params=pltpu.CompilerParams(dimension_semantics=("arbitrary", "arbitrary"),
                                             vmem_limit_bytes=VMEM_LIMIT, disable_bounds_checks=True),
        name="moe_experts",
    )(block_expert, slot_tok, nact, h, wg, wu, wd)


def _combine_kernel(pos_ref, ys_hbm, x_ref, gate_ref, gpost_ref, o_ref, yk_ref, sem):
    i = pl.program_id(0)
    tm = yk_ref.shape[2]
    slot = lax.rem(i, 2)

    def gather(tile, sl):
        for k in range(2):
            def row_copy(r, k=k):
                pos = pos_ref[2 * (tile * tm + r) + k]
                return pltpu.make_async_copy(ys_hbm.at[pl.ds(pos, 1)], yk_ref.at[sl, k, pl.ds(r, 1)],
                                             sem.at[sl])
            _gather_rows(tm, row_copy)

    @pl.when(i == 0)
    def _():
        gather(0, 0)

    for k in range(2):
        pltpu.make_async_copy(ys_hbm.at[pl.ds(0, tm)], yk_ref.at[slot, k], sem.at[slot]).wait()

    @pl.when(i + 1 < pl.num_programs(0))
    def _():
        gather(i + 1, 1 - slot)

    gates = gate_ref[...]
    y = gates[:, 0:1] * yk_ref[slot, 0] + gates[:, 1:2] * yk_ref[slot, 1]
    o_ref[...] = x_ref[...] + _rms(y, gpost_ref[...])


def _moe_combine(x1, ys, pos, gates, gpost, tm=256):
    n, d = x1.shape
    grid_spec = pltpu.PrefetchScalarGridSpec(
        num_scalar_prefetch=1, grid=(n // tm,),
        in_specs=[pl.BlockSpec(memory_space=pl.ANY),
                  pl.BlockSpec((tm, d), lambda i, pos: (i, 0)),
                  pl.BlockSpec((tm, LANES), lambda i, pos: (i, 0)),
                  pl.BlockSpec((1, d), lambda i, pos: (0, 0))],
        out_specs=pl.BlockSpec((tm, d), lambda i, pos: (i, 0)),
        scratch_shapes=[pltpu.VMEM((2, 2, tm, d), F32), pltpu.SemaphoreType.DMA((2,))])
    return pl.pallas_call(
        _combine_kernel, grid_spec=grid_spec, out_shape=jax.ShapeDtypeStruct((n, d), F32),
        compiler_params=pltpu.CompilerParams(dimension_semantics=("arbitrary",),
                                             vmem_limit_bytes=VMEM_LIMIT, disable_bounds_checks=True),
        name="moe_combine",
    )(pos, ys, x1, gates, gpost)


def _moe_plan(idx, n):
    tm = MOE_ROWS
    flat_e = idx[:, :2].reshape(-1)
    onehot = (flat_e[:, None] == jnp.arange(MOE_EXPERTS, dtype=jnp.int32)[None, :]).astype(jnp.int32)
    csum = jnp.cumsum(onehot, axis=0)
    counts = csum[-1]
    rank = jnp.sum((csum - onehot) * onehot, axis=1)
    padded = (counts + tm - 1) // tm * tm
    pend = jnp.cumsum(padded)
    pstart = pend - padded
    dest = (jnp.sum(onehot * pstart[None, :], axis=1) + rank).astype(jnp.int32)
    n_blocks = (2 * n) // tm + MOE_EXPERTS
    n_slots = n_blocks * tm
    flat_tok = jnp.arange(2 * n, dtype=jnp.int32) // 2
    slot_tok = jnp.zeros((n_slots,), jnp.int32).at[dest].set(flat_tok)
    block_start = jnp.arange(n_blocks, dtype=jnp.int32) * tm
    block_expert = jnp.minimum(jnp.searchsorted(pend, block_start, side='right'),
                               MOE_EXPERTS - 1).astype(jnp.int32)
    nact = (pend[-1] // tm).astype(jnp.int32).reshape(1)
    return block_expert, slot_tok, nact, dest


def kernel(x, l0_norm_pre_mix, l0_w_in, l0_rwkv_mu, l0_rwkv_w0, l0_rwkv_w2, l0_rwkv_a0, l0_rwkv_a2, l0_rwkv_g2, l0_rwkv_k_k, l0_rwkv_k_a, l0_rwkv_r_k, l0_rwkv_ln_w, l0_rwkv_ln_b, l0_gmlp_ln_w, l0_gmlp_ln_b, l0_gmlp_ws, l0_gmlp_bs, l0_w_out, l0_norm_post_mix, l0_norm_pre_ffn, l0_ffn_w_gate, l0_ffn_w_up, l0_ffn_w_down, l0_norm_post_ffn, l1_norm_pre_mix, l1_w_in, l1_s5_a_re, l1_s5_a_im, l1_s5_log_dt, l1_s5_b_re, l1_s5_b_im, l1_s5_c_re, l1_s5_c_im, l1_s5_d, l1_s5_glu_w, l1_s5_glu_b, l1_m2_conv_w, l1_m2_conv_b, l1_m2_dt_bias, l1_m2_a_log, l1_m2_d, l1_m2_norm_w, l1_w_out, l1_norm_post_mix, l1_norm_pre_ffn, l1_moe_router, l1_moe_w_gate, l1_moe_w_up, l1_moe_w_down, l1_norm_post_ffn):
    b, t, d = x.shape
    n = b * t
    x2 = x.reshape(n, d)
    row = lambda vec: vec.reshape(1, -1)

    aw = l0_rwkv_w0.shape[0]
    heads = aw // RWKV_HEAD
    lw_, la_, lg_ = l0_rwkv_w2.shape[0], l0_rwkv_a2.shape[0], l0_rwkv_g2.shape[0]
    a_in = 3 * aw + lw_ + la_ + lg_
    padc = lambda m, wdt: jnp.pad(m, ((0, 0), (0, LANES - wdt)))
    o = 3 * aw
    w_a = jnp.concatenate([l0_w_in[:, :o], padc(l0_w_in[:, o:o + lw_], lw_),
                           padc(l0_w_in[:, o + lw_:o + lw_ + la_], la_),
                           padc(l0_w_in[:, o + lw_ + la_:a_in], lg_)], axis=1).astype(BF16)
    w_b = l0_w_in[:, a_in:].astype(BF16)
    p_a, p_b = _norm_proj(x2, l0_norm_pre_mix, [w_a, w_b])
    padv = lambda vec, wdt: jnp.pad(vec, (0, LANES - wdt))
    mu = l0_rwkv_mu
    mu_p = jnp.concatenate([mu[:o], padv(mu[o:o + lw_], lw_), padv(mu[o + lw_:o + lw_ + la_], la_),
                            padv(mu[o + lw_ + la_:], lg_)])
    padr = lambda m: jnp.pad(m, ((0, LANES - m.shape[0]), (0, 0))).astype(BF16)
    hid = jnp.arange(aw, dtype=jnp.int32) // RWKV_HEAD
    gsum = (hid[:, None] == hid[None, :]).astype(BF16)
    rwkv_prm = [row(mu_p), row(l0_rwkv_w0), padr(l0_rwkv_w2), row(l0_rwkv_a0), padr(l0_rwkv_a2),
                padr(l0_rwkv_g2), row(l0_rwkv_k_k), row(l0_rwkv_k_a), row(l0_rwkv_r_k),
                row(l0_rwkv_ln_w), row(l0_rwkv_ln_b), gsum]
    ya = _rwkv_mix(p_a.reshape(b, t, -1), rwkv_prm, heads)
    yb = _gmlp_mix(p_b.reshape(b, t, -1), l0_gmlp_ln_w, l0_gmlp_ln_b, l0_gmlp_ws, l0_gmlp_bs)
    wo = l0_w_out.astype(BF16)
    x2 = _mix_out0(x2, ya.reshape(n, -1), yb.reshape(n, -1), wo[:aw], wo[aw:], row(l0_norm_post_mix))
    x2 = _ffn(x2, row(l0_norm_pre_ffn), l0_ffn_w_gate.astype(BF16), l0_ffn_w_up.astype(BF16),
              l0_ffn_w_down.astype(BF16), row(l0_norm_post_ffn))

    cw = l1_s5_d.shape[0]
    dw = l1_m2_norm_w.shape[0]
    xw = l1_m2_conv_w.shape[1]
    nh = l1_m2_dt_bias.shape[0]
    w1 = l1_w_in
    w_parts = [w1[:, :cw], w1[:, cw:cw + dw], w1[:, cw + dw:cw + dw + xw],
               padc(w1[:, cw + dw + xw:], nh)]
    u_c, z_d, xbc, dtp = _norm_proj(x2, l1_norm_pre_mix, [w.astype(BF16) for w in w_parts])
    s5_w = _s5_weights(l1_s5_a_re, l1_s5_a_im, l1_s5_log_dt, l1_s5_b_re, l1_s5_b_im, l1_s5_c_re,
                       l1_s5_c_im, l1_s5_d, S5_CHUNK)
    yc = _s5_core(u_c.reshape(b, t, cw), s5_w)
    yd = _ssd_mix(z_d.reshape(b, t, dw), xbc.reshape(b, t, xw), dtp.reshape(b, t, LANES),
                  l1_m2_conv_w, l1_m2_conv_b, l1_m2_dt_bias, l1_m2_a_log, l1_m2_d, l1_m2_norm_w)
    wo1 = l1_w_out.astype(BF16)
    x1, h, idx, gates = _mix_out1(x2, yc.reshape(n, cw), yd.reshape(n, dw), l1_s5_glu_w.astype(BF16),
                                  row(l1_s5_glu_b), wo1[:cw], wo1[cw:], row(l1_norm_post_mix),
                                  row(l1_norm_pre_ffn), l1_moe_router)
    block_expert, slot_tok, nact, dest = _moe_plan(idx, n)
    ys = _moe_experts(h, block_expert, slot_tok, nact, l1_moe_w_gate.astype(BF16),
                      l1_moe_w_up.astype(BF16), l1_moe_w_down.astype(BF16))
    out = _moe_combine(x1, ys, dest, gates, row(l1_norm_post_ffn))
    return out.reshape(b, t, d)
```

```python
import functools

import jax
import jax.numpy as jnp
from jax import lax
from jax.experimental import pallas as pl
from jax.experimental.pallas import tpu as pltpu

F32 = jnp.float32
BF16 = jnp.bfloat16

EPS = 1e-6
RWKV_GN_EPS = 64e-5
RWKV_HEAD = 64
RWKV_CHUNK = 64
GMLP_CHUNK = 128
GMLP_GROUPS = 4
S5_GROUP_CH = 16
S5_STATE = 64
S5_CHUNK = 64
SSD_HEAD = 64
SSD_HEADS = 8
SSD_GROUPS = 2
SSD_STATE = 128
SSD_CONV = 4
SSD_CHUNK = 128
MOE_EXPERTS = 8
MOE_ROWS = 512
MXU_TILE = 256
LANES = 128
SUBLANES = 8
VMEM_LIMIT = 56 * 1024 * 1024


def _cparams(sem):
    return pltpu.CompilerParams(dimension_semantics=sem, vmem_limit_bytes=VMEM_LIMIT)


def _bdot(a, b):
    return jnp.dot(a.astype(BF16), b.astype(BF16), preferred_element_type=F32)


def _bdot_nt(a, b):
    return lax.dot_general(a.astype(BF16), b.astype(BF16), (((1,), (1,)), ((), ())),
                           preferred_element_type=F32)


def _bdot_tn(a, b):
    return lax.dot_general(a.astype(BF16), b.astype(BF16), (((0,), (0,)), ((), ())),
                           preferred_element_type=F32)


def _split3(x):
    h = x.astype(BF16)
    r1 = x - h.astype(F32)
    m = r1.astype(BF16)
    l = (r1 - m.astype(F32)).astype(BF16)
    return h, m, l


def _dot_x_exact(x, e):
    h, m, l = _split3(x)
    e = e.astype(BF16)
    return (jnp.dot(h, e, preferred_element_type=F32) + jnp.dot(m, e, preferred_element_type=F32)
            + jnp.dot(l, e, preferred_element_type=F32))


def _dot_exact_x(e, x):
    h, m, l = _split3(x)
    e = e.astype(BF16)
    return (jnp.dot(e, h, preferred_element_type=F32) + jnp.dot(e, m, preferred_element_type=F32)
            + jnp.dot(e, l, preferred_element_type=F32))


def _rms(x, g):
    return x * lax.rsqrt(jnp.mean(x * x, axis=-1, keepdims=True) + EPS) * g


def _sigmoid(x):
    return 1.0 / (1.0 + jnp.exp(-x))


def _silu(x):
    return x * _sigmoid(x)


def _softplus(x):
    return jnp.maximum(x, 0.0) + jnp.log(1.0 + jnp.exp(-jnp.abs(x)))


def _gelu_tanh(x):
    return 0.5 * x * (1.0 + jnp.tanh(0.7978845608028654 * (x + 0.044715 * x * x * x)))


def _iota2(shape, dim):
    return lax.broadcasted_iota(jnp.int32, shape, dim)


def _norm_proj_kernel(n_out, x_ref, g_ref, *refs):
    w_refs = refs[:n_out]
    o_refs = refs[n_out:]
    xn = _rms(x_ref[...], g_ref[...]).astype(BF16)
    for w_ref, o_ref in zip(w_refs, o_refs):
        o_ref[...] = jnp.dot(xn, w_ref[...], preferred_element_type=F32)


def _norm_proj(x2, g, ws, tm=512):
    n, d = x2.shape
    in_specs = [pl.BlockSpec((tm, d), lambda i: (i, 0)), pl.BlockSpec((1, d), lambda i: (0, 0))]
    in_specs += [pl.BlockSpec(w.shape, lambda i: (0, 0)) for w in ws]
    out_specs = [pl.BlockSpec((tm, w.shape[1]), lambda i: (i, 0)) for w in ws]
    out_shape = [jax.ShapeDtypeStruct((n, w.shape[1]), F32) for w in ws]
    return pl.pallas_call(
        functools.partial(_norm_proj_kernel, len(ws)),
        grid=(n // tm,), in_specs=in_specs, out_specs=out_specs, out_shape=out_shape,
        compiler_params=_cparams(("parallel",)), name="norm_proj",
    )(x2, g.reshape(1, d), *ws)


def _rwkv_kernel(heads, nb, p_ref, pp_ref, mu_ref, w0_ref, w2_ref, a0_ref, a2_ref, g2_ref, kk_ref,
                 ka_ref, rk_ref, lnw_ref, lnb_ref, gs_ref, o_ref, z_ref):
    c = pl.program_id(1)
    ln = RWKV_CHUNK
    hd = RWKV_HEAD
    aw = heads * hd

    @pl.when(c == 0)
    def _():
        z_ref[...] = jnp.zeros_like(z_ref)

    tril_f = jnp.where(_iota2((ln, ln), 0) >= _iota2((ln, ln), 1), 1.0, 0.0)
    rows = _iota2((ln, 1), 0)
    gs_tile = gs_ref[...]

    def gs(x):
        nt = x.shape[1] // LANES
        stacked = jnp.concatenate([x[:, j * LANES:(j + 1) * LANES] for j in range(nt)], axis=0)
        red = _dot_x_exact(stacked, gs_tile)
        return jnp.concatenate([red[j * ln:(j + 1) * ln] for j in range(nt)], axis=1)

    per_b = []
    for bi in range(nb):
        p = p_ref[bi]
        prev = jnp.where(c == 0, 0.0, pp_ref[bi][SUBLANES - 1:SUBLANES, :])
        ps = jnp.where(rows == 0, prev, pltpu.roll(p, 1, axis=0))
        pm = p + (ps - p) * mu_ref[...]
        r = pm[:, 0:aw]
        k = pm[:, aw:2 * aw]
        v = pm[:, 2 * aw:3 * aw]
        xw = pm[:, 3 * aw:3 * aw + LANES]
        xa = pm[:, 3 * aw + LANES:3 * aw + 2 * LANES]
        xg = pm[:, 3 * aw + 2 * LANES:3 * aw + 3 * LANES]
        w = w0_ref[...] + _bdot(jnp.tanh(xw), w2_ref[...])
        w = -_softplus(-w) - 0.5
        lw = -jnp.exp(w)
        a = _sigmoid(a0_ref[...] + _bdot(xa, a2_ref[...]))
        g = _bdot(_sigmoid(xg), g2_ref[...])
        kk = k * kk_ref[...]
        kk = kk / jnp.maximum(jnp.sqrt(gs(kk * kk)), 1e-12)
        kmod = k * (1.0 + (a - 1.0) * ka_ref[...])
        bonus = gs(r * kmod * rk_ref[...])
        bvec = kk * a
        cs = _dot_exact_x(tril_f, lw)
        cs_last = cs[ln - 1:ln, :]
        encs = jnp.exp(-cs)
        dec_end = jnp.exp(cs_last - cs)
        per_b.append(dict(
            v=v, g=g, bonus=bonus, rt=r * jnp.exp(cs), kt=kmod * encs, bt=bvec * encs,
            at=-kk * jnp.exp(cs - lw), bh=bvec * dec_end, kh=kmod * dec_end, wl=jnp.exp(cs_last)))

    lane = _iota2((ln, LANES), 1)
    lane_in = jnp.where(lane >= hd, lane - hd, lane)
    trow = _iota2((ln, LANES), 0)
    left = lane < hd
    tril_p = lane_in <= trow
    stril_p = lane_in < trow
    eye_p = lane_in == trow
    eye_pf = jnp.where(eye_p, 1.0, 0.0)

    def bd(x):
        xb = x.astype(BF16)
        zero = jnp.zeros_like(xb)
        return jnp.concatenate([jnp.where(left, xb, zero), jnp.where(left, zero, xb)], axis=0)

    def dot(a, b):
        return jnp.dot(a.astype(BF16), b, preferred_element_type=F32)

    npair = heads // 2
    pairs = [(bi, j) for bi in range(nb) for j in range(npair)]
    cut = lambda name: [per_b[bi][name][:, j * LANES:(j + 1) * LANES] for bi, j in pairs]
    at, rt, bt, kt, vv, bh, kh, wl = (cut(n) for n in ("at", "rt", "bt", "kt", "v", "bh", "kh", "wl"))
    z_all = z_ref[...]
    zs = [z_all[bi, j] for bi, j in pairs]
    npr = range(len(pairs))
    lhs = [jnp.concatenate([at[i], rt[i]], axis=0).astype(BF16) for i in npr]
    ab = [lax.dot_general(lhs[i], bd(bt[i]), (((1,), (1,)), ((), ())), preferred_element_type=F32)
          for i in npr]
    ak = [lax.dot_general(lhs[i], bd(kt[i]), (((1,), (1,)), ((), ())), preferred_element_type=F32)
          for i in npr]
    nmat = [jnp.where(stril_p, ab[i][:ln], 0.0) for i in npr]
    tinv = [eye_pf + nmat[i] for i in npr]
    npow = [dot(nmat[i], bd(nmat[i])) for i in npr]
    for step in range(5):
        bdn = [bd(npow[i]) for i in npr]
        tinv = [tinv[i] + dot(tinv[i], bdn[i]) for i in npr]
        if step < 4:
            npow = [dot(npow[i], bdn[i]) for i in npr]
    bdv = [bd(vv[i]) for i in npr]
    bdz = [bd(zs[i]) for i in npr]
    xmat = [dot(jnp.concatenate([jnp.where(stril_p, ak[i][:ln], 0.0), at[i]], axis=1),
                jnp.concatenate([bdv[i], bdz[i]], axis=0)) for i in npr]
    u = [dot(tinv[i], bd(xmat[i])) for i in npr]
    ys_p = [dot(jnp.concatenate([rt[i], jnp.where(tril_p, ab[i][ln:], 0.0),
                                 jnp.where(tril_p, ak[i][ln:], 0.0)], axis=1),
                jnp.concatenate([bdz[i], bd(u[i]), bdv[i]], axis=0)) for i in npr]
    cross = [_bdot_tn(jnp.concatenate([bh[i], kh[i]], axis=0), jnp.concatenate([u[i], vv[i]], axis=0))
             for i in npr]
    z_new = []
    for i in npr:
        dg = jnp.where(eye_p, wl[i], 0.0)
        wl_i = jnp.sum(jnp.where(left, dg, 0.0), axis=1, keepdims=True)
        wl_j = jnp.sum(jnp.where(left, 0.0, dg), axis=1, keepdims=True)
        z_new.append(jnp.where(left, wl_i, wl_j) * zs[i] + jnp.where(left, cross[i][:ln], cross[i][ln:]))
    z_ref[...] = jnp.stack(z_new, axis=0).reshape(z_ref.shape)

    inv = 1.0 / hd
    for bi in range(nb):
        y = jnp.concatenate(ys_p[bi * npair:(bi + 1) * npair], axis=1)
        mean = gs(y) * inv
        d = y - mean
        var = gs(d * d) * inv
        yn = d * lax.rsqrt(var + RWKV_GN_EPS) * lnw_ref[...] + lnb_ref[...]
        o_ref[bi] = (yn + per_b[bi]["bonus"] * per_b[bi]["v"]) * per_b[bi]["g"]


def _rwkv_mix(p_a, prm, heads, nb=2):
    b, t, cin = p_a.shape
    aw = heads * RWKV_HEAD
    ln = RWKV_CHUNK
    nc = t // ln
    full = lambda arr: pl.BlockSpec(arr.shape, lambda i, j: (0,) * arr.ndim)
    in_specs = [pl.BlockSpec((nb, ln, cin), lambda i, j: (i, j, 0)),
                pl.BlockSpec((nb, SUBLANES, cin),
                             lambda i, j: (i, jnp.maximum(j * (ln // SUBLANES) - 1, 0), 0))]
    in_specs += [full(x) for x in prm]
    return pl.pallas_call(
        functools.partial(_rwkv_kernel, heads, nb),
        grid=(b // nb, nc), in_specs=in_specs,
        out_specs=pl.BlockSpec((nb, ln, aw), lambda i, j: (i, j, 0)),
        out_shape=jax.ShapeDtypeStruct((b, t, aw), F32),
        scratch_shapes=[pltpu.VMEM((nb, heads // 2, RWKV_HEAD, 2 * RWKV_HEAD), F32)],
        compiler_params=_cparams(("parallel", "arbitrary")), name="rwkv7",
    )(p_a, p_a, *prm)


def _gmlp_kernel(p_ref, lnw_ref, lnb_ref, ws_ref, bs_ref, o_ref):
    x = _gelu_tanh(p_ref[0])
    ln = GMLP_CHUNK
    bw = x.shape[1] // 2
    gd = bw // GMLP_GROUPS
    tril = _iota2((ln, ln), 0) >= _iota2((ln, ln), 1)
    for gi in range(GMLP_GROUPS):
        u = x[:, gi * gd:(gi + 1) * gd]
        v = x[:, bw + gi * gd:bw + (gi + 1) * gd]
        mean = jnp.mean(v, axis=-1, keepdims=True)
        d = v - mean
        var = jnp.mean(d * d, axis=-1, keepdims=True)
        vn = d * lax.rsqrt(var + EPS) * lnw_ref[gi:gi + 1, :] + lnb_ref[gi:gi + 1, :]
        s = _bdot(jnp.where(tril, ws_ref[gi], 0.0), vn) + bs_ref[gi]
        o_ref[0, :, gi * gd:(gi + 1) * gd] = u * s


def _gmlp_mix(p_b, ln_w, ln_b, ws, bs):
    b, t, cin = p_b.shape
    bw = cin // 2
    gd = bw // GMLP_GROUPS
    ln = GMLP_CHUNK
    bs_b = jnp.broadcast_to(bs[:, :, None], (GMLP_GROUPS, ln, gd))
    full = lambda arr: pl.BlockSpec(arr.shape, lambda i, j: (0,) * arr.ndim)
    return pl.pallas_call(
        _gmlp_kernel, grid=(b, t // ln),
        in_specs=[pl.BlockSpec((1, ln, cin), lambda i, j: (i, j, 0)),
                  full(ln_w), full(ln_b), full(ws), full(bs_b)],
        out_specs=pl.BlockSpec((1, ln, bw), lambda i, j: (i, j, 0)),
        out_shape=jax.ShapeDtypeStruct((b, t, bw), F32),
        compiler_params=_cparams(("parallel", "parallel")), name="gmlp",
    )(p_b, ln_w, ln_b, ws, bs_b)


def _mix_out0_kernel(x_ref, ya_ref, yb_ref, wa_ref, wb_ref, g_ref, o_ref):
    y = _bdot(ya_ref[...], wa_ref[...]) + _bdot(yb_ref[...], wb_ref[...])
    o_ref[...] = x_ref[...] + _rms(y, g_ref[...])


def _mix_out0(x2, ya, yb, wa, wb, g, tm=512):
    n, d = x2.shape
    row = lambda arr: pl.BlockSpec((tm, arr.shape[1]), lambda i: (i, 0))
    full = lambda arr: pl.BlockSpec(arr.shape, lambda i: (0,) * arr.ndim)
    return pl.pallas_call(
        _mix_out0_kernel, grid=(n // tm,),
        in_specs=[row(x2), row(ya), row(yb), full(wa), full(wb), full(g)],
        out_specs=row(x2), out_shape=jax.ShapeDtypeStruct((n, d), F32),
        compiler_params=_cparams(("parallel",)), name="mix_out0",
    )(x2, ya, yb, wa, wb, g)


def _ffn_kernel(fc, x_ref, gpre_ref, wg_ref, wu_ref, wd_ref, gpost_ref, o_ref):
    h = _rms(x_ref[...], gpre_ref[...]).astype(BF16)
    acc = None
    for c in range(wg_ref.shape[1] // fc):
        cols = slice(c * fc, (c + 1) * fc)
        gate = jnp.dot(h, wg_ref[:, cols], preferred_element_type=F32)
        up = jnp.dot(h, wu_ref[:, cols], preferred_element_type=F32)
        part = jnp.dot((_silu(gate) * up).astype(BF16), wd_ref[cols, :], preferred_element_type=F32)
        acc = part if acc is None else acc + part
    o_ref[...] = x_ref[...] + _rms(acc, gpost_ref[...])


def _ffn(x2, gpre, wg, wu, wd, gpost, tm=512, fc=MXU_TILE):
    n, d = x2.shape
    full = lambda arr: pl.BlockSpec(arr.shape, lambda i: (0,) * arr.ndim)
    return pl.pallas_call(
        functools.partial(_ffn_kernel, fc), grid=(n // tm,),
        in_specs=[pl.BlockSpec((tm, d), lambda i: (i, 0)), full(gpre), full(wg), full(wu), full(wd),
                  full(gpost)],
        out_specs=pl.BlockSpec((tm, d), lambda i: (i, 0)),
        out_shape=jax.ShapeDtypeStruct((n, d), F32),
        compiler_params=_cparams(("parallel",)), name="ffn",
    )(x2, gpre, wg, wu, wd, gpost)


def _s5_kernel(nc, nb, u_ref, tap_ref, wsr_ref, wsi_ref, wcr_ref, wci_ref, alr_ref, ali_ref, d_ref, o_ref,
               toep_ref):
    u = u_ref[0]
    taps = tap_ref[0]
    ch = taps.shape[0]
    lane = _iota2(taps.shape, 1)
    for s in range(taps.shape[1] // ch):
        blk = taps if s == 0 else jnp.where(lane >= ch * s, pltpu.roll(taps, ch * s, axis=1), 0.0)
        toep_ref[ch * s:ch * (s + 1), :] = blk.astype(BF16)
    y = jnp.dot(u, toep_ref[...], preferred_element_type=F32)
    xer = jnp.dot(u, wsr_ref[0], preferred_element_type=F32)
    xei = jnp.dot(u, wsi_ref[0], preferred_element_type=F32)
    alr = alr_ref[0]
    ali = ali_ref[0]
    cr = jnp.zeros((nb, xer.shape[1]), F32)
    ci = jnp.zeros((nb, xer.shape[1]), F32)
    prs, pis = [], []
    for c in range(nc):
        prs.append(cr)
        pis.append(ci)
        er = xer[c * nb:(c + 1) * nb]
        ei = xei[c * nb:(c + 1) * nb]
        cr, ci = alr * cr - ali * ci + er, alr * ci + ali * cr + ei
    pr = jnp.concatenate(prs, axis=0)
    pi = jnp.concatenate(pis, axis=0)
    y = y + _bdot(pr, wcr_ref[0]) + _bdot(pi, wci_ref[0])
    o_ref[0] = y + d_ref[0] * u.astype(F32)


def _s5_weights(a_re, a_im, log_dt, b_re, b_im, c_re, c_im, d_skip, ln):
    g, st = a_re.shape
    ch = b_re.shape[2]
    dt = jnp.exp(log_dt)[:, None]
    lr, li = a_re, a_im
    tau = jnp.arange(ln + 1, dtype=F32)[:, None, None]
    mag = jnp.exp(lr[None] * dt[None] * tau)
    pw_r = mag * jnp.cos(li[None] * dt[None] * tau)
    pw_i = mag * jnp.sin(li[None] * dt[None] * tau)
    ab_r, ab_i = pw_r[1], pw_i[1]
    nr, ni = ab_r - 1.0, ab_i
    den = lr * lr + li * li
    fr, fi = (nr * lr + ni * li) / den, (ni * lr - nr * li) / den
    bb_r = fr[..., None] * b_re - fi[..., None] * b_im
    bb_i = fr[..., None] * b_im + fi[..., None] * b_re
    cp_r = c_re[None] * pw_r[:ln, :, None, :] - c_im[None] * pw_i[:ln, :, None, :]
    cp_i = c_re[None] * pw_i[:ln, :, None, :] + c_im[None] * pw_r[:ln, :, None, :]
    hp = lax.Precision.HIGHEST
    taps = (jnp.einsum('tgcp,gpd->gdtc', cp_r, bb_r, precision=hp)
            - jnp.einsum('tgcp,gpd->gdtc', cp_i, bb_i, precision=hp))
    taps = taps.reshape(g, ch, ln * ch)
    rev_r, rev_i = pw_r[:ln][::-1], pw_i[:ln][::-1]
    ws_r = rev_r[..., None] * bb_r[None] - rev_i[..., None] * bb_i[None]
    ws_i = rev_r[..., None] * bb_i[None] + rev_i[..., None] * bb_r[None]
    ws_r = ws_r.transpose(1, 0, 3, 2).reshape(g, ln * ch, st)
    ws_i = ws_i.transpose(1, 0, 3, 2).reshape(g, ln * ch, st)
    q_r, q_i = pw_r[1:ln + 1], pw_i[1:ln + 1]
    wc_r = c_re[None] * q_r[:, :, None, :] - c_im[None] * q_i[:, :, None, :]
    wc_i = -(c_re[None] * q_i[:, :, None, :] + c_im[None] * q_r[:, :, None, :])
    wc_r = wc_r.transpose(1, 3, 0, 2).reshape(g, st, ln * ch)
    wc_i = wc_i.transpose(1, 3, 0, 2).reshape(g, st, ln * ch)
    al_r = pw_r[ln].reshape(g, 1, st)
    al_i = pw_i[ln].reshape(g, 1, st)
    d_t = jnp.tile(d_skip.reshape(g, 1, ch), (1, ln, 1)).reshape(g, 1, ln * ch)
    return (taps, ws_r.astype(BF16), ws_i.astype(BF16), wc_r.astype(BF16),
            wc_i.astype(BF16), al_r, al_i, d_t)


def _s5_core(u, weights):
    b, t, cw = u.shape
    ln, ch = S5_CHUNK, S5_GROUP_CH
    g = cw // ch
    nc = t // ln
    ug = u.astype(BF16).reshape(b, nc, ln, g, ch).transpose(3, 1, 0, 2, 4).reshape(g, nc * b, ln * ch)
    per_g = lambda arr: pl.BlockSpec((1,) + arr.shape[1:], lambda i: (i, 0, 0))
    yg = pl.pallas_call(
        functools.partial(_s5_kernel, nc, b), grid=(g,),
        in_specs=[per_g(ug)] + [per_g(w) for w in weights],
        out_specs=per_g(ug), out_shape=jax.ShapeDtypeStruct(ug.shape, F32),
        scratch_shapes=[pltpu.VMEM((ln * ch, ln * ch), BF16)],
        compiler_params=_cparams(("parallel",)), name="s5",
    )(ug, *weights)
    return yg.reshape(g, nc, b, ln, ch).transpose(2, 1, 3, 0, 4).reshape(b, t, cw)


def _ssd_kernel(z_ref, xbc_ref, xp_ref, dt_ref, cw_ref, cb_ref, dtb_ref, alog_ref, dsk_ref, nw_ref,
                o_ref, s_ref):
    c = pl.program_id(1)
    ln = SSD_CHUNK
    hd = SSD_HEAD
    dw = SSD_HEADS * hd
    gn = SSD_STATE

    @pl.when(c == 0)
    def _():
        s_ref[...] = jnp.zeros_like(s_ref)

    xbc = xbc_ref[0]
    prev = jnp.where(c == 0, 0.0, xp_ref[0])
    full = jnp.concatenate([prev, xbc], axis=0)
    conv = cb_ref[...]
    for j in range(SSD_CONV):
        off = SUBLANES - (SSD_CONV - 1) + j
        conv = conv + cw_ref[j:j + 1, :] * full[off:off + ln]
    act = _silu(conv)
    xh = act[:, :dw]
    dt = _softplus(dt_ref[0] + dtb_ref[...])
    adt = -jnp.exp(alog_ref[...]) * dt
    tril = _iota2((ln, ln), 0) >= _iota2((ln, ln), 1)
    acs = _dot_exact_x(jnp.where(tril, 1.0, 0.0), adt)
    acs_t = acs.T
    tot = acs[ln - 1:ln, :]
    hg = SSD_HEADS // SSD_GROUPS
    s_all = s_ref[...]
    y_heads, s_heads = [], []
    for gi in range(SSD_GROUPS):
        bm = act[:, dw + gi * gn:dw + (gi + 1) * gn]
        cm = act[:, dw + SSD_GROUPS * gn + gi * gn:dw + SSD_GROUPS * gn + (gi + 1) * gn]
        cb = _bdot_nt(cm, bm)
        for hh in range(hg):
            h = gi * hg + hh
            sl = slice(h * hd, (h + 1) * hd)
            col = acs[:, h:h + 1]
            rowv = acs_t[h:h + 1, :]
            lmat = jnp.exp(jnp.where(tril, col - rowv, -jnp.inf))
            xh_h = xh[:, sl]
            xdt = xh_h * dt[:, h:h + 1]
            tot_h = tot[:, h:h + 1]
            st = s_all[h]
            y_h = _bdot(cb * lmat, xdt) + jnp.exp(col) * _bdot(cm, st)
            s_heads.append(jnp.exp(tot_h) * st + _bdot_tn(bm * jnp.exp(tot_h - col), xdt))
            y_heads.append(y_h + dsk_ref[:, sl] * xh_h)
    s_ref[...] = jnp.stack(s_heads, axis=0)
    y = jnp.concatenate(y_heads, axis=1) * _silu(z_ref[0])
    gw = dw // SSD_GROUPS
    for gi in range(SSD_GROUPS):
        yg = y[:, gi * gw:(gi + 1) * gw]
        yg = yg * lax.rsqrt(jnp.mean(yg * yg, axis=-1, keepdims=True) + EPS)
        o_ref[0, :, gi * gw:(gi + 1) * gw] = yg * nw_ref[:, gi * gw:(gi + 1) * gw]


def _ssd_mix(z, xbc, dtp, conv_w, conv_b, dt_bias, a_log, d_skip, norm_w):
    b, t, dw = z.shape
    ln = SSD_CHUNK
    xw = xbc.shape[2]
    pad = lambda vec: jnp.pad(vec, (0, LANES - vec.shape[0])).reshape(1, LANES)
    dsk = jnp.repeat(d_skip, SSD_HEAD).reshape(1, dw)
    prm = [conv_w, conv_b.reshape(1, xw), pad(dt_bias), pad(a_log), dsk, norm_w.reshape(1, dw)]
    full = lambda arr: pl.BlockSpec(arr.shape, lambda i, j: (0,) * arr.ndim)
    blk = lambda w: pl.BlockSpec((1, ln, w), lambda i, j: (i, j, 0))
    return pl.pallas_call(
        _ssd_kernel, grid=(b, t // ln),
        in_specs=[blk(dw), blk(xw),
                  pl.BlockSpec((1, SUBLANES, xw),
                               lambda i, j: (i, jnp.maximum(j * (ln // SUBLANES) - 1, 0), 0)),
                  blk(LANES)] + [full(x) for x in prm],
        out_specs=blk(dw), out_shape=jax.ShapeDtypeStruct((b, t, dw), F32),
        scratch_shapes=[pltpu.VMEM((SSD_HEADS, SSD_STATE, SSD_HEAD), F32)],
        compiler_params=_cparams(("parallel", "arbitrary")), name="ssd",
    )(z, xbc, xbc, dtp, *prm)


def _mix_out1_kernel(x_ref, yc_ref, yd_ref, gw_ref, gb_ref, wc_ref, wd_ref, gpost_ref, gpre_ref,
                     wrh_ref, wrl_ref, x1_ref, h_ref, idx_ref, gate_ref):
    yc = _gelu_tanh(yc_ref[...])
    yc = yc * _sigmoid(_bdot(yc, gw_ref[...]) + gb_ref[...])
    y = _bdot(yc, wc_ref[...]) + _bdot(yd_ref[...], wd_ref[...])
    x1 = x_ref[...] + _rms(y, gpost_ref[...])
    x1_ref[...] = x1
    h = _rms(x1, gpre_ref[...])
    h_ref[...] = h
    hh = h.astype(BF16)
    hl = (h - hh.astype(F32)).astype(BF16)
    wrh = wrh_ref[...]
    logits = (jnp.dot(hh, wrh, preferred_element_type=F32) + jnp.dot(hl, wrh, preferred_element_type=F32)
              + jnp.dot(hh, wrl_ref[...], preferred_element_type=F32))
    lane = _iota2(logits.shape, 1)
    lane_f = lane.astype(F32)
    logits = jnp.where(lane < MOE_EXPERTS, logits, -jnp.inf)
    m1 = jnp.max(logits, axis=-1, keepdims=True)
    i1 = jnp.min(jnp.where(logits == m1, lane_f, float(LANES)), axis=-1, keepdims=True)
    rest = jnp.where(lane_f == i1, -jnp.inf, logits)
    m2 = jnp.max(rest, axis=-1, keepdims=True)
    i2 = jnp.min(jnp.where(rest == m2, lane_f, float(LANES)), axis=-1, keepdims=True)
    e2 = jnp.exp(m2 - m1)
    g1 = 1.0 / (1.0 + e2)
    g2 = e2 / (1.0 + e2)
    idx_ref[...] = jnp.where(lane == 0, i1, jnp.where(lane == 1, i2, 0.0)).astype(jnp.int32)
    gate_ref[...] = jnp.where(lane == 0, g1, jnp.where(lane == 1, g2, 0.0))


def _mix_out1(x2, yc, yd, glu_w, glu_b, wc, wd, gpost, gpre, wr, tm=512):
    n, d = x2.shape
    wr_p = jnp.pad(wr, ((0, 0), (0, LANES - wr.shape[1])))
    wrh = wr_p.astype(BF16)
    wrl = (wr_p - wrh.astype(F32)).astype(BF16)
    row = lambda w: pl.BlockSpec((tm, w), lambda i: (i, 0))
    full = lambda arr: pl.BlockSpec(arr.shape, lambda i: (0,) * arr.ndim)
    prm = [glu_w, glu_b, wc, wd, gpost, gpre, wrh, wrl]
    return pl.pallas_call(
        _mix_out1_kernel, grid=(n // tm,),
        in_specs=[row(d), row(yc.shape[1]), row(yd.shape[1])] + [full(p) for p in prm],
        out_specs=[row(d), row(d), row(LANES), row(LANES)],
        out_shape=[jax.ShapeDtypeStruct((n, d), F32), jax.ShapeDtypeStruct((n, d), F32),
                   jax.ShapeDtypeStruct((n, LANES), jnp.int32), jax.ShapeDtypeStruct((n, LANES), F32)],
        compiler_params=_cparams(("parallel",)), name="mix_out1",
    )(x2, yc, yd, *prm)


GATHER_UNROLL = 8


def _gather_rows(n_rows, make_copy):
    def body(j, carry):
        for q in range(GATHER_UNROLL):
            make_copy(j * GATHER_UNROLL + q).start(priority=q % 2)
        return carry

    lax.fori_loop(0, n_rows // GATHER_UNROLL, body, 0)


def _moe_kernel(be_ref, tok_ref, nact_ref, h_hbm, wg_ref, wu_ref, wd_ref, o_ref, xs_ref, xb_ref,
                acc_ref, sem):
    i = pl.program_id(0)
    f = pl.program_id(1)
    tm = xs_ref.shape[1]
    nact = nact_ref[0]
    active = i < nact
    slot = lax.rem(i, 2)

    def gather(block, sl):
        def row_copy(r):
            tok = tok_ref[block * tm + r]
            return pltpu.make_async_copy(h_hbm.at[pl.ds(tok, 1)], xs_ref.at[sl, pl.ds(r, 1)], sem.at[sl])
        _gather_rows(tm, row_copy)

    @pl.when(jnp.logical_and(active, f == 0))
    def _():
        @pl.when(i == 0)
        def _():
            gather(0, 0)

        pltpu.make_async_copy(h_hbm.at[pl.ds(0, tm)], xs_ref.at[slot], sem.at[slot]).wait()

        @pl.when(i + 1 < nact)
        def _():
            gather(i + 1, 1 - slot)

        xb_ref[...] = xs_ref[slot].astype(BF16)
        acc_ref[...] = jnp.zeros_like(acc_ref)

    @pl.when(active)
    def _():
        x = xb_ref[...]
        acc = acc_ref[...]
        for c in range(wg_ref.shape[2] // MXU_TILE):
            cols = slice(c * MXU_TILE, (c + 1) * MXU_TILE)
            gate = jnp.dot(x, wg_ref[0, :, cols], preferred_element_type=F32)
            up = jnp.dot(x, wu_ref[0, :, cols], preferred_element_type=F32)
            acc = acc + jnp.dot((_silu(gate) * up).astype(BF16), wd_ref[0, cols, :],
                                preferred_element_type=F32)
        acc_ref[...] = acc

    @pl.when(f == pl.num_programs(1) - 1)
    def _():
        o_ref[...] = jnp.where(active, acc_ref[...], 0.0)


def _moe_experts(h, block_expert, slot_tok, nact, wg, wu, wd, tf=7 * MXU_TILE):
    n, d = h.shape
    tm = MOE_ROWS
    n_slots = slot_tok.shape[0]
    ff = wg.shape[2]
    grid_spec = pltpu.PrefetchScalarGridSpec(
        num_scalar_prefetch=3, grid=(n_slots // tm, ff // tf),
        in_specs=[pl.BlockSpec(memory_space=pl.ANY),
                  pl.BlockSpec((1, d, tf), lambda i, j, be, tok, na: (be[i], 0, j)),
                  pl.BlockSpec((1, d, tf), lambda i, j, be, tok, na: (be[i], 0, j)),
                  pl.BlockSpec((1, tf, d), lambda i, j, be, tok, na: (be[i], j, 0))],
        out_specs=pl.BlockSpec((tm, d), lambda i, j, be, tok, na: (i, 0)),
        scratch_shapes=[pltpu.VMEM((2, tm, d), F32), pltpu.VMEM((tm, d), BF16), pltpu.VMEM((tm, d), F32),
                        pltpu.SemaphoreType.DMA((2,))])
    return pl.pallas_call(
        _moe_kernel, grid_spec=grid_spec, out_shape=jax.ShapeDtypeStruct((n_slots, d), F32),
        compiler_params=pltpu.CompilerParams(dimension_semantics=("arbitrary", "arbitrary"),
                                             vmem_limit_bytes=VMEM_LIMIT, disable_bounds_checks=True),
        name="moe_experts",
    )(block_expert, slot_tok, nact, h, wg, wu, wd)


def _combine_kernel(pos_ref, ys_hbm, x_ref, gate_ref, gpost_ref, o_ref, yk_ref, sem):
    i = pl.program_id(0)
    tm = yk_ref.shape[2]
    slot = lax.rem(i, 2)

    def gather(tile, sl):
        for k in range(2):
            def row_copy(r, k=k):
                pos = pos_ref[2 * (tile * tm + r) + k]
                return pltpu.make_async_copy(ys_hbm.at[pl.ds(pos, 1)], yk_ref.at[sl, k, pl.ds(r, 1)],
                                             sem.at[sl])
            _gather_rows(tm, row_copy)

    @pl.when(i == 0)
    def _():
        gather(0, 0)

    for k in range(2):
        pltpu.make_async_copy(ys_hbm.at[pl.ds(0, tm)], yk_ref.at[slot, k], sem.at[slot]).wait()

    @pl.when(i + 1 < pl.num_programs(0))
    def _():
        gather(i + 1, 1 - slot)

    gates = gate_ref[...]
    y = gates[:, 0:1] * yk_ref[slot, 0] + gates[:, 1:2] * yk_ref[slot, 1]
    o_ref[...] = x_ref[...] + _rms(y, gpost_ref[...])


def _moe_combine(x1, ys, pos, gates, gpost, tm=256):
    n, d = x1.shape
    grid_spec = pltpu.PrefetchScalarGridSpec(
        num_scalar_prefetch=1, grid=(n // tm,),
        in_specs=[pl.BlockSpec(memory_space=pl.ANY),
                  pl.BlockSpec((tm, d), lambda i, pos: (i, 0)),
                  pl.BlockSpec((tm, LANES), lambda i, pos: (i, 0)),
                  pl.BlockSpec((1, d), lambda i, pos: (0, 0))],
        out_specs=pl.BlockSpec((tm, d), lambda i, pos: (i, 0)),
        scratch_shapes=[pltpu.VMEM((2, 2, tm, d), F32), pltpu.SemaphoreType.DMA((2,))])
    return pl.pallas_call(
        _combine_kernel, grid_spec=grid_spec, out_shape=jax.ShapeDtypeStruct((n, d), F32),
        compiler_params=pltpu.CompilerParams(dimension_semantics=("arbitrary",),
                                             vmem_limit_bytes=VMEM_LIMIT, disable_bounds_checks=True),
        name="moe_combine",
    )(pos, ys, x1, gates, gpost)


def _moe_plan(idx, n):
    tm = MOE_ROWS
    flat_e = idx[:, :2].reshape(-1)
    onehot = (flat_e[:, None] == jnp.arange(MOE_EXPERTS, dtype=jnp.int32)[None, :]).astype(jnp.int32)
    csum = jnp.cumsum(onehot, axis=0)
    counts = csum[-1]
    rank = jnp.sum((csum - onehot) * onehot, axis=1)
    padded = (counts + tm - 1) // tm * tm
    pend = jnp.cumsum(padded)
    pstart = pend - padded
    dest = (jnp.sum(onehot * pstart[None, :], axis=1) + rank).astype(jnp.int32)
    n_blocks = (2 * n) // tm + MOE_EXPERTS
    n_slots = n_blocks * tm
    flat_tok = jnp.arange(2 * n, dtype=jnp.int32) // 2
    slot_tok = jnp.zeros((n_slots,), jnp.int32).at[dest].set(flat_tok)
    block_start = jnp.arange(n_blocks, dtype=jnp.int32) * tm
    block_expert = jnp.minimum(jnp.searchsorted(pend, block_start, side='right'),
                               MOE_EXPERTS - 1).astype(jnp.int32)
    nact = (pend[-1] // tm).astype(jnp.int32).reshape(1)
    return block_expert, slot_tok, nact, dest


def kernel(x, l0_norm_pre_mix, l0_w_in, l0_rwkv_mu, l0_rwkv_w0, l0_rwkv_w2, l0_rwkv_a0, l0_rwkv_a2, l0_rwkv_g2, l0_rwkv_k_k, l0_rwkv_k_a, l0_rwkv_r_k, l0_rwkv_ln_w, l0_rwkv_ln_b, l0_gmlp_ln_w, l0_gmlp_ln_b, l0_gmlp_ws, l0_gmlp_bs, l0_w_out, l0_norm_post_mix, l0_norm_pre_ffn, l0_ffn_w_gate, l0_ffn_w_up, l0_ffn_w_down, l0_norm_post_ffn, l1_norm_pre_mix, l1_w_in, l1_s5_a_re, l1_s5_a_im, l1_s5_log_dt, l1_s5_b_re, l1_s5_b_im, l1_s5_c_re, l1_s5_c_im, l1_s5_d, l1_s5_glu_w, l1_s5_glu_b, l1_m2_conv_w, l1_m2_conv_b, l1_m2_dt_bias, l1_m2_a_log, l1_m2_d, l1_m2_norm_w, l1_w_out, l1_norm_post_mix, l1_norm_pre_ffn, l1_moe_router, l1_moe_w_gate, l1_moe_w_up, l1_moe_w_down, l1_norm_post_ffn):
    b, t, d = x.shape
    n = b * t
    x2 = x.reshape(n, d)
    row = lambda vec: vec.reshape(1, -1)

    aw = l0_rwkv_w0.shape[0]
    heads = aw // RWKV_HEAD
    lw_, la_, lg_ = l0_rwkv_w2.shape[0], l0_rwkv_a2.shape[0], l0_rwkv_g2.shape[0]
    a_in = 3 * aw + lw_ + la_ + lg_
    padc = lambda m, wdt: jnp.pad(m, ((0, 0), (0, LANES - wdt)))
    o = 3 * aw
    w_a = jnp.concatenate([l0_w_in[:, :o], padc(l0_w_in[:, o:o + lw_], lw_),
                           padc(l0_w_in[:, o + lw_:o + lw_ + la_], la_),
                           padc(l0_w_in[:, o + lw_ + la_:a_in], lg_)], axis=1).astype(BF16)
    w_b = l0_w_in[:, a_in:].astype(BF16)
    p_a, p_b = _norm_proj(x2, l0_norm_pre_mix, [w_a, w_b])
    padv = lambda vec, wdt: jnp.pad(vec, (0, LANES - wdt))
    mu = l0_rwkv_mu
    mu_p = jnp.concatenate([mu[:o], padv(mu[o:o + lw_], lw_), padv(mu[o + lw_:o + lw_ + la_], la_),
                            padv(mu[o + lw_ + la_:], lg_)])
    padr = lambda m: jnp.pad(m, ((0, LANES - m.shape[0]), (0, 0))).astype(BF16)
    hid = jnp.arange(LANES, dtype=jnp.int32) // RWKV_HEAD
    gsum = (hid[:, None] == hid[None, :]).astype(BF16)
    rwkv_prm = [row(mu_p), row(l0_rwkv_w0), padr(l0_rwkv_w2), row(l0_rwkv_a0), padr(l0_rwkv_a2),
                padr(l0_rwkv_g2), row(l0_rwkv_k_k), row(l0_rwkv_k_a), row(l0_rwkv_r_k),
                row(l0_rwkv_ln_w), row(l0_rwkv_ln_b), gsum]
    ya = _rwkv_mix(p_a.reshape(b, t, -1), rwkv_prm, heads)
    yb = _gmlp_mix(p_b.reshape(b, t, -1), l0_gmlp_ln_w, l0_gmlp_ln_b, l0_gmlp_ws, l0_gmlp_bs)
    wo = l0_w_out.astype(BF16)
    x2 = _mix_out0(x2, ya.reshape(n, -1), yb.reshape(n, -1), wo[:aw], wo[aw:], row(l0_norm_post_mix))
    x2 = _ffn(x2, row(l0_norm_pre_ffn), l0_ffn_w_gate.astype(BF16), l0_ffn_w_up.astype(BF16),
              l0_ffn_w_down.astype(BF16), row(l0_norm_post_ffn))

    cw = l1_s5_d.shape[0]
    dw = l1_m2_norm_w.shape[0]
    xw = l1_m2_conv_w.shape[1]
    nh = l1_m2_dt_bias.shape[0]
    w1 = l1_w_in
    w_parts = [w1[:, :cw], w1[:, cw:cw + dw], w1[:, cw + dw:cw + dw + xw],
               padc(w1[:, cw + dw + xw:], nh)]
    u_c, z_d, xbc, dtp = _norm_proj(x2, l1_norm_pre_mix, [w.astype(BF16) for w in w_parts])
    s5_w = _s5_weights(l1_s5_a_re, l1_s5_a_im, l1_s5_log_dt, l1_s5_b_re, l1_s5_b_im, l1_s5_c_re,
                       l1_s5_c_im, l1_s5_d, S5_CHUNK)
    yc = _s5_core(u_c.reshape(b, t, cw), s5_w)
    yd = _ssd_mix(z_d.reshape(b, t, dw), xbc.reshape(b, t, xw), dtp.reshape(b, t, LANES),
                  l1_m2_conv_w, l1_m2_conv_b, l1_m2_dt_bias, l1_m2_a_log, l1_m2_d, l1_m2_norm_w)
    wo1 = l1_w_out.astype(BF16)
    x1, h, idx, gates = _mix_out1(x2, yc.reshape(n, cw), yd.reshape(n, dw), l1_s5_glu_w.astype(BF16),
                                  row(l1_s5_glu_b), wo1[:cw], wo1[cw:], row(l1_norm_post_mix),
                                  row(l1_norm_pre_ffn), l1_moe_router)
    block_expert, slot_tok, nact, dest = _moe_plan(idx, n)
    ys = _moe_experts(h, block_expert, slot_tok, nact, l1_moe_w_gate.astype(BF16),
                      l1_moe_w_up.astype(BF16), l1_moe_w_down.astype(BF16))
    out = _moe_combine(x1, ys, dest, gates, row(l1_norm_post_ffn))
    return out.reshape(b, t, d)
```

```python
import functools

import jax
import jax.numpy as jnp
from jax import lax
from jax.experimental import pallas as pl
from jax.experimental.pallas import tpu as pltpu

F32 = jnp.float32
BF16 = jnp.bfloat16

EPS = 1e-6
RWKV_GN_EPS = 64e-5
RWKV_HEAD = 64
RWKV_CHUNK = 64
GMLP_CHUNK = 128
GMLP_GROUPS = 4
S5_GROUP_CH = 16
S5_STATE = 64
S5_CHUNK = 64
SSD_HEAD = 64
SSD_HEADS = 8
SSD_GROUPS = 2
SSD_STATE = 128
SSD_CONV = 4
SSD_CHUNK = 128
MOE_EXPERTS = 8
MOE_ROWS = 512
MXU_TILE = 256
LANES = 128
SUBLANES = 8
VMEM_LIMIT = 56 * 1024 * 1024


def _cparams(sem):
    return pltpu.CompilerParams(dimension_semantics=sem, vmem_limit_bytes=VMEM_LIMIT)


def _bdot(a, b):
    return jnp.dot(a.astype(BF16), b.astype(BF16), preferred_element_type=F32)


def _bdot_nt(a, b):
    return lax.dot_general(a.astype(BF16), b.astype(BF16), (((1,), (1,)), ((), ())),
                           preferred_element_type=F32)


def _bdot_tn(a, b):
    return lax.dot_general(a.astype(BF16), b.astype(BF16), (((0,), (0,)), ((), ())),
                           preferred_element_type=F32)


def _split3(x):
    h = x.astype(BF16)
    r1 = x - h.astype(F32)
    m = r1.astype(BF16)
    l = (r1 - m.astype(F32)).astype(BF16)
    return h, m, l


def _dot_x_exact(x, e):
    h, m, l = _split3(x)
    e = e.astype(BF16)
    return (jnp.dot(h, e, preferred_element_type=F32) + jnp.dot(m, e, preferred_element_type=F32)
            + jnp.dot(l, e, preferred_element_type=F32))


def _dot_exact_x(e, x):
    h, m, l = _split3(x)
    e = e.astype(BF16)
    return (jnp.dot(e, h, preferred_element_type=F32) + jnp.dot(e, m, preferred_element_type=F32)
            + jnp.dot(e, l, preferred_element_type=F32))


def _rms(x, g):
    return x * lax.rsqrt(jnp.mean(x * x, axis=-1, keepdims=True) + EPS) * g


def _sigmoid(x):
    return 1.0 / (1.0 + jnp.exp(-x))


def _silu(x):
    return x * _sigmoid(x)


def _softplus(x):
    return jnp.maximum(x, 0.0) + jnp.log(1.0 + jnp.exp(-jnp.abs(x)))


def _gelu_tanh(x):
    return 0.5 * x * (1.0 + jnp.tanh(0.7978845608028654 * (x + 0.044715 * x * x * x)))


def _iota2(shape, dim):
    return lax.broadcasted_iota(jnp.int32, shape, dim)


def _norm_proj_kernel(n_out, x_ref, g_ref, *refs):
    w_refs = refs[:n_out]
    o_refs = refs[n_out:]
    xn = _rms(x_ref[...], g_ref[...]).astype(BF16)
    for w_ref, o_ref in zip(w_refs, o_refs):
        o_ref[...] = jnp.dot(xn, w_ref[...], preferred_element_type=F32)


def _norm_proj(x2, g, ws, tm=512):
    n, d = x2.shape
    in_specs = [pl.BlockSpec((tm, d), lambda i: (i, 0)), pl.BlockSpec((1, d), lambda i: (0, 0))]
    in_specs += [pl.BlockSpec(w.shape, lambda i: (0, 0)) for w in ws]
    out_specs = [pl.BlockSpec((tm, w.shape[1]), lambda i: (i, 0)) for w in ws]
    out_shape = [jax.ShapeDtypeStruct((n, w.shape[1]), F32) for w in ws]
    return pl.pallas_call(
        functools.partial(_norm_proj_kernel, len(ws)),
        grid=(n // tm,), in_specs=in_specs, out_specs=out_specs, out_shape=out_shape,
        compiler_params=_cparams(("parallel",)), name="norm_proj",
    )(x2, g.reshape(1, d), *ws)


def _rwkv_kernel(heads, nb, p_ref, pp_ref, mu_ref, w0_ref, w2_ref, a0_ref, a2_ref, g2_ref, kk_ref,
                 ka_ref, rk_ref, lnw_ref, lnb_ref, gs_ref, o_ref, z_ref):
    c = pl.program_id(1)
    ln = RWKV_CHUNK
    hd = RWKV_HEAD
    aw = heads * hd

    @pl.when(c == 0)
    def _():
        z_ref[...] = jnp.zeros_like(z_ref)

    tril_f = jnp.where(_iota2((ln, ln), 0) >= _iota2((ln, ln), 1), 1.0, 0.0)
    rows = _iota2((ln, 1), 0)
    gs_tile = gs_ref[...]

    def gs(x):
        nt = x.shape[1] // LANES
        stacked = jnp.concatenate([x[:, j * LANES:(j + 1) * LANES] for j in range(nt)], axis=0)
        red = _dot_x_exact(stacked, gs_tile)
        return jnp.concatenate([red[j * ln:(j + 1) * ln] for j in range(nt)], axis=1)

    per_b = []
    for bi in range(nb):
        p = p_ref[bi]
        prev = jnp.where(c == 0, 0.0, pp_ref[bi][SUBLANES - 1:SUBLANES, :])
        ps = jnp.where(rows == 0, prev, pltpu.roll(p, 1, axis=0))
        pm = p + (ps - p) * mu_ref[...]
        r = pm[:, 0:aw]
        k = pm[:, aw:2 * aw]
        v = pm[:, 2 * aw:3 * aw]
        xw = pm[:, 3 * aw:3 * aw + LANES]
        xa = pm[:, 3 * aw + LANES:3 * aw + 2 * LANES]
        xg = pm[:, 3 * aw + 2 * LANES:3 * aw + 3 * LANES]
        w = w0_ref[...] + _bdot(jnp.tanh(xw), w2_ref[...])
        w = -_softplus(-w) - 0.5
        lw = -jnp.exp(w)
        a = _sigmoid(a0_ref[...] + _bdot(xa, a2_ref[...]))
        g = _bdot(_sigmoid(xg), g2_ref[...])
        kk = k * kk_ref[...]
        kk = kk / jnp.maximum(jnp.sqrt(gs(kk * kk)), 1e-12)
        kmod = k * (1.0 + (a - 1.0) * ka_ref[...])
        bonus = gs(r * kmod * rk_ref[...])
        bvec = kk * a
        cs = _dot_exact_x(tril_f, lw)
        cs_last = cs[ln - 1:ln, :]
        encs = jnp.exp(-cs)
        dec_end = jnp.exp(cs_last - cs)
        per_b.append(dict(
            v=v, g=g, bonus=bonus, rt=r * jnp.exp(cs), kt=kmod * encs, bt=bvec * encs,
            at=-kk * jnp.exp(cs - lw), bh=bvec * dec_end, kh=kmod * dec_end, wl=jnp.exp(cs_last)))

    lane = _iota2((ln, LANES), 1)
    lane_in = jnp.where(lane >= hd, lane - hd, lane)
    trow = _iota2((ln, LANES), 0)
    left = lane < hd
    tril_p = lane_in <= trow
    stril_p = lane_in < trow
    eye_p = lane_in == trow
    eye_pf = jnp.where(eye_p, 1.0, 0.0)

    def bd(x):
        xb = x.astype(BF16)
        zero = jnp.zeros_like(xb)
        return jnp.concatenate([jnp.where(left, xb, zero), jnp.where(left, zero, xb)], axis=0)

    def dot(a, b):
        return jnp.dot(a.astype(BF16), b, preferred_element_type=F32)

    npair = heads // 2
    pairs = [(bi, j) for bi in range(nb) for j in range(npair)]
    cut = lambda name: [per_b[bi][name][:, j * LANES:(j + 1) * LANES] for bi, j in pairs]
    at, rt, bt, kt, vv, bh, kh, wl = (cut(n) for n in ("at", "rt", "bt", "kt", "v", "bh", "kh", "wl"))
    z_all = z_ref[...]
    zs = [z_all[bi, j] for bi, j in pairs]
    npr = range(len(pairs))
    lhs = [jnp.concatenate([at[i], rt[i]], axis=0).astype(BF16) for i in npr]
    ab = [lax.dot_general(lhs[i], bd(bt[i]), (((1,), (1,)), ((), ())), preferred_element_type=F32)
          for i in npr]
    ak = [lax.dot_general(lhs[i], bd(kt[i]), (((1,), (1,)), ((), ())), preferred_element_type=F32)
          for i in npr]
    nmat = [jnp.where(stril_p, ab[i][:ln], 0.0) for i in npr]
    tinv = [eye_pf + nmat[i] for i in npr]
    npow = [dot(nmat[i], bd(nmat[i])) for i in npr]
    for step in range(5):
        bdn = [bd(npow[i]) for i in npr]
        tinv = [tinv[i] + dot(tinv[i], bdn[i]) for i in npr]
        if step < 4:
            npow = [dot(npow[i], bdn[i]) for i in npr]
    bdv = [bd(vv[i]) for i in npr]
    bdz = [bd(zs[i]) for i in npr]
    xmat = [dot(jnp.concatenate([jnp.where(stril_p, ak[i][:ln], 0.0), at[i]], axis=1),
                jnp.concatenate([bdv[i], bdz[i]], axis=0)) for i in npr]
    u = [dot(tinv[i], bd(xmat[i])) for i in npr]
    ys_p = [dot(jnp.concatenate([rt[i], jnp.where(tril_p, ab[i][ln:], 0.0),
                                 jnp.where(tril_p, ak[i][ln:], 0.0)], axis=1),
                jnp.concatenate([bdz[i], bd(u[i]), bdv[i]], axis=0)) for i in npr]
    cross = [_bdot_tn(jnp.concatenate([bh[i], kh[i]], axis=0), jnp.concatenate([u[i], vv[i]], axis=0))
             for i in npr]
    z_new = []
    for i in npr:
        dg = jnp.where(eye_p, wl[i], 0.0)
        wl_i = jnp.sum(jnp.where(left, dg, 0.0), axis=1, keepdims=True)
        wl_j = jnp.sum(jnp.where(left, 0.0, dg), axis=1, keepdims=True)
        z_new.append(jnp.where(left, wl_i, wl_j) * zs[i] + jnp.where(left, cross[i][:ln], cross[i][ln:]))
    z_ref[...] = jnp.stack(z_new, axis=0).reshape(z_ref.shape)

    inv = 1.0 / hd
    for bi in range(nb):
        y = jnp.concatenate(ys_p[bi * npair:(bi + 1) * npair], axis=1)
        mean = gs(y) * inv
        d = y - mean
        var = gs(d * d) * inv
        yn = d * lax.rsqrt(var + RWKV_GN_EPS) * lnw_ref[...] + lnb_ref[...]
        o_ref[bi] = (yn + per_b[bi]["bonus"] * per_b[bi]["v"]) * per_b[bi]["g"]


def _rwkv_mix(p_a, prm, heads, nb=2):
    b, t, cin = p_a.shape
    aw = heads * RWKV_HEAD
    ln = RWKV_CHUNK
    nc = t // ln
    full = lambda arr: pl.BlockSpec(arr.shape, lambda i, j: (0,) * arr.ndim)
    in_specs = [pl.BlockSpec((nb, ln, cin), lambda i, j: (i, j, 0)),
                pl.BlockSpec((nb, SUBLANES, cin),
                             lambda i, j: (i, jnp.maximum(j * (ln // SUBLANES) - 1, 0), 0))]
    in_specs += [full(x) for x in prm]
    return pl.pallas_call(
        functools.partial(_rwkv_kernel, heads, nb),
        grid=(b // nb, nc), in_specs=in_specs,
        out_specs=pl.BlockSpec((nb, ln, aw), lambda i, j: (i, j, 0)),
        out_shape=jax.ShapeDtypeStruct((b, t, aw), F32),
        scratch_shapes=[pltpu.VMEM((nb, heads // 2, RWKV_HEAD, 2 * RWKV_HEAD), F32)],
        compiler_params=_cparams(("parallel", "arbitrary")), name="rwkv7",
    )(p_a, p_a, *prm)


def _gmlp_kernel(p_ref, lnw_ref, lnb_ref, ws_ref, bs_ref, o_ref):
    x = _gelu_tanh(p_ref[0])
    ln = GMLP_CHUNK
    bw = x.shape[1] // 2
    gd = bw // GMLP_GROUPS
    tril = _iota2((ln, ln), 0) >= _iota2((ln, ln), 1)
    for gi in range(GMLP_GROUPS):
        u = x[:, gi * gd:(gi + 1) * gd]
        v = x[:, bw + gi * gd:bw + (gi + 1) * gd]
        mean = jnp.mean(v, axis=-1, keepdims=True)
        d = v - mean
        var = jnp.mean(d * d, axis=-1, keepdims=True)
        vn = d * lax.rsqrt(var + EPS) * lnw_ref[gi:gi + 1, :] + lnb_ref[gi:gi + 1, :]
        s = _bdot(jnp.where(tril, ws_ref[gi], 0.0), vn) + bs_ref[gi]
        o_ref[0, :, gi * gd:(gi + 1) * gd] = u * s


def _gmlp_mix(p_b, ln_w, ln_b, ws, bs):
    b, t, cin = p_b.shape
    bw = cin // 2
    gd = bw // GMLP_GROUPS
    ln = GMLP_CHUNK
    bs_b = jnp.broadcast_to(bs[:, :, None], (GMLP_GROUPS, ln, gd))
    full = lambda arr: pl.BlockSpec(arr.shape, lambda i, j: (0,) * arr.ndim)
    return pl.pallas_call(
        _gmlp_kernel, grid=(b, t // ln),
        in_specs=[pl.BlockSpec((1, ln, cin), lambda i, j: (i, j, 0)),
                  full(ln_w), full(ln_b), full(ws), full(bs_b)],
        out_specs=pl.BlockSpec((1, ln, bw), lambda i, j: (i, j, 0)),
        out_shape=jax.ShapeDtypeStruct((b, t, bw), F32),
        compiler_params=_cparams(("parallel", "parallel")), name="gmlp",
    )(p_b, ln_w, ln_b, ws, bs_b)


def _mix_out0_kernel(x_ref, ya_ref, yb_ref, wa_ref, wb_ref, g_ref, o_ref):
    y = _bdot(ya_ref[...], wa_ref[...]) + _bdot(yb_ref[...], wb_ref[...])
    o_ref[...] = x_ref[...] + _rms(y, g_ref[...])


def _mix_out0(x2, ya, yb, wa, wb, g, tm=512):
    n, d = x2.shape
    row = lambda arr: pl.BlockSpec((tm, arr.shape[1]), lambda i: (i, 0))
    full = lambda arr: pl.BlockSpec(arr.shape, lambda i: (0,) * arr.ndim)
    return pl.pallas_call(
        _mix_out0_kernel, grid=(n // tm,),
        in_specs=[row(x2), row(ya), row(yb), full(wa), full(wb), full(g)],
        out_specs=row(x2), out_shape=jax.ShapeDtypeStruct((n, d), F32),
        compiler_params=_cparams(("parallel",)), name="mix_out0",
    )(x2, ya, yb, wa, wb, g)


def _ffn_kernel(fc, x_ref, gpre_ref, wg_ref, wu_ref, wd_ref, gpost_ref, o_ref):
    h = _rms(x_ref[...], gpre_ref[...]).astype(BF16)
    acc = None
    for c in range(wg_ref.shape[1] // fc):
        cols = slice(c * fc, (c + 1) * fc)
        gate = jnp.dot(h, wg_ref[:, cols], preferred_element_type=F32)
        up = jnp.dot(h, wu_ref[:, cols], preferred_element_type=F32)
        part = jnp.dot((_silu(gate) * up).astype(BF16), wd_ref[cols, :], preferred_element_type=F32)
        acc = part if acc is None else acc + part
    o_ref[...] = x_ref[...] + _rms(acc, gpost_ref[...])


def _ffn(x2, gpre, wg, wu, wd, gpost, tm=512, fc=MXU_TILE):
    n, d = x2.shape
    full = lambda arr: pl.BlockSpec(arr.shape, lambda i: (0,) * arr.ndim)
    return pl.pallas_call(
        functools.partial(_ffn_kernel, fc), grid=(n // tm,),
        in_specs=[pl.BlockSpec((tm, d), lambda i: (i, 0)), full(gpre), full(wg), full(wu), full(wd),
                  full(gpost)],
        out_specs=pl.BlockSpec((tm, d), lambda i: (i, 0)),
        out_shape=jax.ShapeDtypeStruct((n, d), F32),
        compiler_params=_cparams(("parallel",)), name="ffn",
    )(x2, gpre, wg, wu, wd, gpost)


def _s5_kernel(nc, nb, u_ref, tap_ref, wsr_ref, wsi_ref, wcr_ref, wci_ref, alr_ref, ali_ref, d_ref, o_ref,
               toep_ref):
    u = u_ref[0]
    taps = tap_ref[0]
    ch = taps.shape[0]
    lane = _iota2(taps.shape, 1)
    for s in range(taps.shape[1] // ch):
        blk = taps if s == 0 else jnp.where(lane >= ch * s, pltpu.roll(taps, ch * s, axis=1), 0.0)
        toep_ref[ch * s:ch * (s + 1), :] = blk.astype(BF16)
    y = jnp.dot(u, toep_ref[...], preferred_element_type=F32)
    xer = jnp.dot(u, wsr_ref[0], preferred_element_type=F32)
    xei = jnp.dot(u, wsi_ref[0], preferred_element_type=F32)
    alr = alr_ref[0]
    ali = ali_ref[0]
    cr = jnp.zeros((nb, xer.shape[1]), F32)
    ci = jnp.zeros((nb, xer.shape[1]), F32)
    prs, pis = [], []
    for c in range(nc):
        prs.append(cr)
        pis.append(ci)
        er = xer[c * nb:(c + 1) * nb]
        ei = xei[c * nb:(c + 1) * nb]
        cr, ci = alr * cr - ali * ci + er, alr * ci + ali * cr + ei
    pr = jnp.concatenate(prs, axis=0)
    pi = jnp.concatenate(pis, axis=0)
    y = y + _bdot(pr, wcr_ref[0]) + _bdot(pi, wci_ref[0])
    o_ref[0] = y + d_ref[0] * u.astype(F32)


def _s5_weights(a_re, a_im, log_dt, b_re, b_im, c_re, c_im, d_skip, ln):
    g, st = a_re.shape
    ch = b_re.shape[2]
    dt = jnp.exp(log_dt)[:, None]
    lr, li = a_re, a_im
    tau = jnp.arange(ln + 1, dtype=F32)[:, None, None]
    mag = jnp.exp(lr[None] * dt[None] * tau)
    pw_r = mag * jnp.cos(li[None] * dt[None] * tau)
    pw_i = mag * jnp.sin(li[None] * dt[None] * tau)
    ab_r, ab_i = pw_r[1], pw_i[1]
    nr, ni = ab_r - 1.0, ab_i
    den = lr * lr + li * li
    fr, fi = (nr * lr + ni * li) / den, (ni * lr - nr * li) / den
    bb_r = fr[..., None] * b_re - fi[..., None] * b_im
    bb_i = fr[..., None] * b_im + fi[..., None] * b_re
    cp_r = c_re[None] * pw_r[:ln, :, None, :] - c_im[None] * pw_i[:ln, :, None, :]
    cp_i = c_re[None] * pw_i[:ln, :, None, :] + c_im[None] * pw_r[:ln, :, None, :]
    hp = lax.Precision.HIGHEST
    taps = (jnp.einsum('tgcp,gpd->gdtc', cp_r, bb_r, precision=hp)
            - jnp.einsum('tgcp,gpd->gdtc', cp_i, bb_i, precision=hp))
    taps = taps.reshape(g, ch, ln * ch)
    rev_r, rev_i = pw_r[:ln][::-1], pw_i[:ln][::-1]
    ws_r = rev_r[..., None] * bb_r[None] - rev_i[..., None] * bb_i[None]
    ws_i = rev_r[..., None] * bb_i[None] + rev_i[..., None] * bb_r[None]
    ws_r = ws_r.transpose(1, 0, 3, 2).reshape(g, ln * ch, st)
    ws_i = ws_i.transpose(1, 0, 3, 2).reshape(g, ln * ch, st)
    q_r, q_i = pw_r[1:ln + 1], pw_i[1:ln + 1]
    wc_r = c_re[None] * q_r[:, :, None, :] - c_im[None] * q_i[:, :, None, :]
    wc_i = -(c_re[None] * q_i[:, :, None, :] + c_im[None] * q_r[:, :, None, :])
    wc_r = wc_r.transpose(1, 3, 0, 2).reshape(g, st, ln * ch)
    wc_i = wc_i.transpose(1, 3, 0, 2).reshape(g, st, ln * ch)
    al_r = pw_r[ln].reshape(g, 1, st)
    al_i = pw_i[ln].reshape(g, 1, st)
    d_t = jnp.tile(d_skip.reshape(g, 1, ch), (1, ln, 1)).reshape(g, 1, ln * ch)
    return (taps, ws_r.astype(BF16), ws_i.astype(BF16), wc_r.astype(BF16),
            wc_i.astype(BF16), al_r, al_i, d_t)


def _s5_core(u, weights):
    b, t, cw = u.shape
    ln, ch = S5_CHUNK, S5_GROUP_CH
    g = cw // ch
    nc = t // ln
    ug = u.astype(BF16).reshape(b, nc, ln, g, ch).transpose(3, 1, 0, 2, 4).reshape(g, nc * b, ln * ch)
    per_g = lambda arr: pl.BlockSpec((1,) + arr.shape[1:], lambda i: (i, 0, 0))
    yg = pl.pallas_call(
        functools.partial(_s5_kernel, nc, b), grid=(g,),
        in_specs=[per_g(ug)] + [per_g(w) for w in weights],
        out_specs=per_g(ug), out_shape=jax.ShapeDtypeStruct(ug.shape, F32),
        scratch_shapes=[pltpu.VMEM((ln * ch, ln * ch), BF16)],
        compiler_params=_cparams(("parallel",)), name="s5",
    )(ug, *weights)
    return yg.reshape(g, nc, b, ln, ch).transpose(2, 1, 3, 0, 4).reshape(b, t, cw)


def _ssd_kernel(z_ref, xbc_ref, xp_ref, dt_ref, cw_ref, cb_ref, dtb_ref, alog_ref, dsk_ref, nw_ref,
                o_ref, s_ref):
    c = pl.program_id(1)
    ln = SSD_CHUNK
    hd = SSD_HEAD
    dw = SSD_HEADS * hd
    gn = SSD_STATE

    @pl.when(c == 0)
    def _():
        s_ref[...] = jnp.zeros_like(s_ref)

    xbc = xbc_ref[0]
    prev = jnp.where(c == 0, 0.0, xp_ref[0])
    full = jnp.concatenate([prev, xbc], axis=0)
    conv = cb_ref[...]
    for j in range(SSD_CONV):
        off = SUBLANES - (SSD_CONV - 1) + j
        conv = conv + cw_ref[j:j + 1, :] * full[off:off + ln]
    act = _silu(conv)
    xh = act[:, :dw]
    dt = _softplus(dt_ref[0] + dtb_ref[...])
    adt = -jnp.exp(alog_ref[...]) * dt
    tril = _iota2((ln, ln), 0) >= _iota2((ln, ln), 1)
    acs = _dot_exact_x(jnp.where(tril, 1.0, 0.0), adt)
    acs_t = acs.T
    tot = acs[ln - 1:ln, :]
    hg = SSD_HEADS // SSD_GROUPS
    s_all = s_ref[...]
    y_heads, s_heads = [], []
    for gi in range(SSD_GROUPS):
        bm = act[:, dw + gi * gn:dw + (gi + 1) * gn]
        cm = act[:, dw + SSD_GROUPS * gn + gi * gn:dw + SSD_GROUPS * gn + (gi + 1) * gn]
        cb = _bdot_nt(cm, bm)
        for hh in range(hg):
            h = gi * hg + hh
            sl = slice(h * hd, (h + 1) * hd)
            col = acs[:, h:h + 1]
            rowv = acs_t[h:h + 1, :]
            lmat = jnp.exp(jnp.where(tril, col - rowv, -jnp.inf))
            xh_h = xh[:, sl]
            xdt = xh_h * dt[:, h:h + 1]
            tot_h = tot[:, h:h + 1]
            st = s_all[h]
            y_h = _bdot(cb * lmat, xdt) + jnp.exp(col) * _bdot(cm, st)
            s_heads.append(jnp.exp(tot_h) * st + _bdot_tn(bm * jnp.exp(tot_h - col), xdt))
            y_heads.append(y_h + dsk_ref[:, sl] * xh_h)
    s_ref[...] = jnp.stack(s_heads, axis=0)
    y = jnp.concatenate(y_heads, axis=1) * _silu(z_ref[0])
    gw = dw // SSD_GROUPS
    for gi in range(SSD_GROUPS):
        yg = y[:, gi * gw:(gi + 1) * gw]
        yg = yg * lax.rsqrt(jnp.mean(yg * yg, axis=-1, keepdims=True) + EPS)
        o_ref[0, :, gi * gw:(gi + 1) * gw] = yg * nw_ref[:, gi * gw:(gi + 1) * gw]


def _ssd_mix(z, xbc, dtp, conv_w, conv_b, dt_bias, a_log, d_skip, norm_w):
    b, t, dw = z.shape
    ln = SSD_CHUNK
    xw = xbc.shape[2]
    pad = lambda vec: jnp.pad(vec, (0, LANES - vec.shape[0])).reshape(1, LANES)
    dsk = jnp.repeat(d_skip, SSD_HEAD).reshape(1, dw)
    prm = [conv_w, conv_b.reshape(1, xw), pad(dt_bias), pad(a_log), dsk, norm_w.reshape(1, dw)]
    full = lambda arr: pl.BlockSpec(arr.shape, lambda i, j: (0,) * arr.ndim)
    blk = lambda w: pl.BlockSpec((1, ln, w), lambda i, j: (i, j, 0))
    return pl.pallas_call(
        _ssd_kernel, grid=(b, t // ln),
        in_specs=[blk(dw), blk(xw),
                  pl.BlockSpec((1, SUBLANES, xw),
                               lambda i, j: (i, jnp.maximum(j * (ln // SUBLANES) - 1, 0), 0)),
                  blk(LANES)] + [full(x) for x in prm],
        out_specs=blk(dw), out_shape=jax.ShapeDtypeStruct((b, t, dw), F32),
        scratch_shapes=[pltpu.VMEM((SSD_HEADS, SSD_STATE, SSD_HEAD), F32)],
        compiler_params=_cparams(("parallel", "arbitrary")), name="ssd",
    )(z, xbc, xbc, dtp, *prm)


def _mix_out1_kernel(x_ref, yc_ref, yd_ref, gw_ref, gb_ref, wc_ref, wd_ref, gpost_ref, gpre_ref,
                     wrh_ref, wrl_ref, x1_ref, h_ref, idx_ref, gate_ref):
    yc = _gelu_tanh(yc_ref[...])
    yc = yc * _sigmoid(_bdot(yc, gw_ref[...]) + gb_ref[...])
    y = _bdot(yc, wc_ref[...]) + _bdot(yd_ref[...], wd_ref[...])
    x1 = x_ref[...] + _rms(y, gpost_ref[...])
    x1_ref[...] = x1
    h = _rms(x1, gpre_ref[...])
    h_ref[...] = h
    hh = h.astype(BF16)
    hl = (h - hh.astype(F32)).astype(BF16)
    wrh = wrh_ref[...]
    logits = (jnp.dot(hh, wrh, preferred_element_type=F32) + jnp.dot(hl, wrh, preferred_element_type=F32)
              + jnp.dot(hh, wrl_ref[...], preferred_element_type=F32))
    lane = _iota2(logits.shape, 1)
    lane_f = lane.astype(F32)
    logits = jnp.where(lane < MOE_EXPERTS, logits, -jnp.inf)
    m1 = jnp.max(logits, axis=-1, keepdims=True)
    i1 = jnp.min(jnp.where(logits == m1, lane_f, float(LANES)), axis=-1, keepdims=True)
    rest = jnp.where(lane_f == i1, -jnp.inf, logits)
    m2 = jnp.max(rest, axis=-1, keepdims=True)
    i2 = jnp.min(jnp.where(rest == m2, lane_f, float(LANES)), axis=-1, keepdims=True)
    e2 = jnp.exp(m2 - m1)
    g1 = 1.0 / (1.0 + e2)
    g2 = e2 / (1.0 + e2)
    idx_ref[...] = jnp.where(lane == 0, i1, jnp.where(lane == 1, i2, 0.0)).astype(jnp.int32)
    gate_ref[...] = jnp.where(lane == 0, g1, jnp.where(lane == 1, g2, 0.0))


def _mix_out1(x2, yc, yd, glu_w, glu_b, wc, wd, gpost, gpre, wr, tm=512):
    n, d = x2.shape
    wr_p = jnp.pad(wr, ((0, 0), (0, LANES - wr.shape[1])))
    wrh = wr_p.astype(BF16)
    wrl = (wr_p - wrh.astype(F32)).astype(BF16)
    row = lambda w: pl.BlockSpec((tm, w), lambda i: (i, 0))
    full = lambda arr: pl.BlockSpec(arr.shape, lambda i: (0,) * arr.ndim)
    prm = [glu_w, glu_b, wc, wd, gpost, gpre, wrh, wrl]
    return pl.pallas_call(
        _mix_out1_kernel, grid=(n // tm,),
        in_specs=[row(d), row(yc.shape[1]), row(yd.shape[1])] + [full(p) for p in prm],
        out_specs=[row(d), row(d), row(LANES), row(LANES)],
        out_shape=[jax.ShapeDtypeStruct((n, d), F32), jax.ShapeDtypeStruct((n, d), F32),
                   jax.ShapeDtypeStruct((n, LANES), jnp.int32), jax.ShapeDtypeStruct((n, LANES), F32)],
        compiler_params=_cparams(("parallel",)), name="mix_out1",
    )(x2, yc, yd, *prm)


GATHER_UNROLL = 8


def _gather_rows(n_rows, make_copy):
    def body(j, carry):
        for q in range(GATHER_UNROLL):
            make_copy(j * GATHER_UNROLL + q).start(priority=q % 2)
        return carry

    lax.fori_loop(0, n_rows // GATHER_UNROLL, body, 0)


def _moe_kernel(nf, be_ref, tok_ref, dst_ref, nact_ref, h_hbm, wg_ref, wu_ref, wd_ref, y_hbm, buf_ref,
                xb_ref, gsem, ssem):
    i = pl.program_id(0)
    f = pl.program_id(1)
    n_blocks = pl.num_programs(0)
    tm = buf_ref.shape[1]
    nact = nact_ref[0]
    active = i < nact
    slot = lax.rem(i, 2)
    other = 1 - slot
    xs = lambda sl: buf_ref.at[sl]
    yb = lambda sl: buf_ref.at[2 + sl]

    def gather_copy(block, sl, r):
        tok = tok_ref[block * tm + r]
        return pltpu.make_async_copy(h_hbm.at[pl.ds(tok, 1)], buf_ref.at[sl, pl.ds(r, 1)], gsem.at[sl])

    def scatter_copy(block, sl, r):
        dst = dst_ref[block * tm + r]
        return pltpu.make_async_copy(buf_ref.at[2 + sl, pl.ds(r, 1)], y_hbm.at[pl.ds(dst, 1)], ssem.at[sl])

    def wait_rows(sem_slot_ref, buf):
        pltpu.make_async_copy(h_hbm.at[pl.ds(0, tm)], buf, sem_slot_ref).wait()

    @pl.when(f == 0)
    def _():
        @pl.when(i == 0)
        def _():
            _gather_rows(tm, functools.partial(gather_copy, 0, 0))
            buf_ref[3] = jnp.zeros(buf_ref.shape[1:], F32)
            n_real = y_hbm.shape[0] - 2 * tm
            for half in range(2):
                init = pltpu.make_async_copy(yb(1), y_hbm.at[pl.ds(n_real + half * tm, tm)], ssem.at[0])
                init.start()
                init.wait()

        @pl.when(i <= nact)
        def _():
            wait_rows(gsem.at[slot], xs(slot))

        @pl.when(jnp.logical_and(i >= 1, i <= nact))
        def _():
            wait_rows(ssem.at[slot], yb(slot))

        @pl.when(i == nact)
        def _():
            _gather_rows(tm, functools.partial(scatter_copy, i - 1, other))
            wait_rows(ssem.at[other], yb(other))

        @pl.when(active)
        def _():
            xb_ref[...] = buf_ref[slot].astype(BF16)
            buf_ref[2 + slot] = jnp.zeros(buf_ref.shape[1:], F32)

    @pl.when(active)
    def _():
        x = xb_ref[...]
        nch = wg_ref.shape[2] // MXU_TILE
        rows_f = tm // nf
        base = f * rows_f
        prev = jnp.where(i == 0, n_blocks - 1, i - 1)
        for c in range(nch):
            for r in range(rows_f * c // nch, rows_f * (c + 1) // nch):
                gather_copy(i + 1, other, base + r).start(priority=0)
                scatter_copy(prev, other, base + r).start(priority=1)
            cols = slice(c * MXU_TILE, (c + 1) * MXU_TILE)
            gate = jnp.dot(x, wg_ref[0, :, cols], preferred_element_type=F32)
            up = jnp.dot(x, wu_ref[0, :, cols], preferred_element_type=F32)
            buf_ref[2 + slot] += jnp.dot((_silu(gate) * up).astype(BF16), wd_ref[0, cols, :],
                                         preferred_element_type=F32)


def _moe_experts(h, block_expert, slot_tok, slot_dst, nact, n_rows, wg, wu, wd, tf=7 * MXU_TILE):
    n, d = h.shape
    tm = MOE_ROWS
    n_blocks = slot_tok.shape[0] // tm
    ff = wg.shape[2]
    nf = ff // tf
    wmap = lambda i, j, be, tok, dst, na: (be[i], 0, j)
    grid_spec = pltpu.PrefetchScalarGridSpec(
        num_scalar_prefetch=4, grid=(n_blocks, nf),
        in_specs=[pl.BlockSpec(memory_space=pl.ANY),
                  pl.BlockSpec((1, d, tf), wmap), pl.BlockSpec((1, d, tf), wmap),
                  pl.BlockSpec((1, tf, d), lambda i, j, be, tok, dst, na: (be[i], j, 0))],
        out_specs=pl.BlockSpec(memory_space=pl.ANY),
        scratch_shapes=[pltpu.VMEM((4, tm, d), F32), pltpu.VMEM((tm, d), BF16),
                        pltpu.SemaphoreType.DMA((2,)), pltpu.SemaphoreType.DMA((2,))])
    return pl.pallas_call(
        functools.partial(_moe_kernel, nf), grid_spec=grid_spec,
        out_shape=jax.ShapeDtypeStruct((n_rows, d), F32),
        compiler_params=pltpu.CompilerParams(dimension_semantics=("arbitrary", "arbitrary"),
                                             vmem_limit_bytes=VMEM_LIMIT, disable_bounds_checks=True),
        name="moe_experts",
    )(block_expert, slot_tok, slot_dst, nact, h, wg, wu, wd)


def _combine_kernel(x_ref, y_ref, gate_ref, gpost_ref, o_ref):
    d = x_ref.shape[1]
    gates = gate_ref[...]
    y = gates[:, 0:1] * y_ref[:, :d] + gates[:, 1:2] * y_ref[:, d:]
    o_ref[...] = x_ref[...] + _rms(y, gpost_ref[...])


def _moe_combine(x1, y, gates, gpost, tm=512):
    n, d = x1.shape
    y2 = y.reshape(y.shape[0] // 2, 2 * d)
    return pl.pallas_call(
        _combine_kernel, grid=(n // tm,),
        in_specs=[pl.BlockSpec((tm, d), lambda i: (i, 0)), pl.BlockSpec((tm, 2 * d), lambda i: (i, 0)),
                  pl.BlockSpec((tm, LANES), lambda i: (i, 0)), pl.BlockSpec((1, d), lambda i: (0, 0))],
        out_specs=pl.BlockSpec((tm, d), lambda i: (i, 0)),
        out_shape=jax.ShapeDtypeStruct((n, d), F32),
        compiler_params=_cparams(("parallel",)), name="moe_combine",
    )(x1, y2, gates, gpost)


def _moe_plan(idx, n):
    tm = MOE_ROWS
    flat_e = idx[:, :2].reshape(-1)
    onehot = (flat_e[:, None] == jnp.arange(MOE_EXPERTS, dtype=jnp.int32)[None, :]).astype(jnp.int32)
    csum = jnp.cumsum(onehot, axis=0)
    counts = csum[-1]
    rank = jnp.sum((csum - onehot) * onehot, axis=1)
    padded = (counts + tm - 1) // tm * tm
    pend = jnp.cumsum(padded)
    pstart = pend - padded
    dest = (jnp.sum(onehot * pstart[None, :], axis=1) + rank).astype(jnp.int32)
    n_blocks = (2 * n) // tm + MOE_EXPERTS + 1
    n_slots = n_blocks * tm
    slot_pair = jnp.full((n_slots,), -1, jnp.int32).at[dest].set(jnp.arange(2 * n, dtype=jnp.int32))
    real = slot_pair >= 0
    slot_tok = jnp.where(real, slot_pair // 2, 0)
    s_id = jnp.arange(n_slots, dtype=jnp.int32)
    slot_dst = jnp.where(real, slot_pair, 2 * n + (s_id // tm) % 2 * tm + s_id % tm)
    block_start = jnp.arange(n_blocks, dtype=jnp.int32) * tm
    block_expert = jnp.minimum(jnp.searchsorted(pend, block_start, side='right'),
                               MOE_EXPERTS - 1).astype(jnp.int32)
    nact = (pend[-1] // tm).astype(jnp.int32).reshape(1)
    return block_expert, slot_tok, slot_dst, nact, 2 * n + 2 * tm


def kernel(x, l0_norm_pre_mix, l0_w_in, l0_rwkv_mu, l0_rwkv_w0, l0_rwkv_w2, l0_rwkv_a0, l0_rwkv_a2, l0_rwkv_g2, l0_rwkv_k_k, l0_rwkv_k_a, l0_rwkv_r_k, l0_rwkv_ln_w, l0_rwkv_ln_b, l0_gmlp_ln_w, l0_gmlp_ln_b, l0_gmlp_ws, l0_gmlp_bs, l0_w_out, l0_norm_post_mix, l0_norm_pre_ffn, l0_ffn_w_gate, l0_ffn_w_up, l0_ffn_w_down, l0_norm_post_ffn, l1_norm_pre_mix, l1_w_in, l1_s5_a_re, l1_s5_a_im, l1_s5_log_dt, l1_s5_b_re, l1_s5_b_im, l1_s5_c_re, l1_s5_c_im, l1_s5_d, l1_s5_glu_w, l1_s5_glu_b, l1_m2_conv_w, l1_m2_conv_b, l1_m2_dt_bias, l1_m2_a_log, l1_m2_d, l1_m2_norm_w, l1_w_out, l1_norm_post_mix, l1_norm_pre_ffn, l1_moe_router, l1_moe_w_gate, l1_moe_w_up, l1_moe_w_down, l1_norm_post_ffn):
    b, t, d = x.shape
    n = b * t
    x2 = x.reshape(n, d)
    row = lambda vec: vec.reshape(1, -1)

    aw = l0_rwkv_w0.shape[0]
    heads = aw // RWKV_HEAD
    lw_, la_, lg_ = l0_rwkv_w2.shape[0], l0_rwkv_a2.shape[0], l0_rwkv_g2.shape[0]
    a_in = 3 * aw + lw_ + la_ + lg_
    padc = lambda m, wdt: jnp.pad(m, ((0, 0), (0, LANES - wdt)))
    o = 3 * aw
    w_a = jnp.concatenate([l0_w_in[:, :o], padc(l0_w_in[:, o:o + lw_], lw_),
                           padc(l0_w_in[:, o + lw_:o + lw_ + la_], la_),
                           padc(l0_w_in[:, o + lw_ + la_:a_in], lg_)], axis=1).astype(BF16)
    w_b = l0_w_in[:, a_in:].astype(BF16)
    p_a, p_b = _norm_proj(x2, l0_norm_pre_mix, [w_a, w_b])
    padv = lambda vec, wdt: jnp.pad(vec, (0, LANES - wdt))
    mu = l0_rwkv_mu
    mu_p = jnp.concatenate([mu[:o], padv(mu[o:o + lw_], lw_), padv(mu[o + lw_:o + lw_ + la_], la_),
                            padv(mu[o + lw_ + la_:], lg_)])
    padr = lambda m: jnp.pad(m, ((0, LANES - m.shape[0]), (0, 0))).astype(BF16)
    hid = jnp.arange(LANES, dtype=jnp.int32) // RWKV_HEAD
    gsum = (hid[:, None] == hid[None, :]).astype(BF16)
    rwkv_prm = [row(mu_p), row(l0_rwkv_w0), padr(l0_rwkv_w2), row(l0_rwkv_a0), padr(l0_rwkv_a2),
                padr(l0_rwkv_g2), row(l0_rwkv_k_k), row(l0_rwkv_k_a), row(l0_rwkv_r_k),
                row(l0_rwkv_ln_w), row(l0_rwkv_ln_b), gsum]
    ya = _rwkv_mix(p_a.reshape(b, t, -1), rwkv_prm, heads)
    yb = _gmlp_mix(p_b.reshape(b, t, -1), l0_gmlp_ln_w, l0_gmlp_ln_b, l0_gmlp_ws, l0_gmlp_bs)
    wo = l0_w_out.astype(BF16)
    x2 = _mix_out0(x2, ya.reshape(n, -1), yb.reshape(n, -1), wo[:aw], wo[aw:], row(l0_norm_post_mix))
    x2 = _ffn(x2, row(l0_norm_pre_ffn), l0_ffn_w_gate.astype(BF16), l0_ffn_w_up.astype(BF16),
              l0_ffn_w_down.astype(BF16), row(l0_norm_post_ffn))

    cw = l1_s5_d.shape[0]
    dw = l1_m2_norm_w.shape[0]
    xw = l1_m2_conv_w.shape[1]
    nh = l1_m2_dt_bias.shape[0]
    w1 = l1_w_in
    w_parts = [w1[:, :cw], w1[:, cw:cw + dw], w1[:, cw + dw:cw + dw + xw],
               padc(w1[:, cw + dw + xw:], nh)]
    u_c, z_d, xbc, dtp = _norm_proj(x2, l1_norm_pre_mix, [w.astype(BF16) for w in w_parts])
    s5_w = _s5_weights(l1_s5_a_re, l1_s5_a_im, l1_s5_log_dt, l1_s5_b_re, l1_s5_b_im, l1_s5_c_re,
                       l1_s5_c_im, l1_s5_d, S5_CHUNK)
    yc = _s5_core(u_c.reshape(b, t, cw), s5_w)
    yd = _ssd_mix(z_d.reshape(b, t, dw), xbc.reshape(b, t, xw), dtp.reshape(b, t, LANES),
                  l1_m2_conv_w, l1_m2_conv_b, l1_m2_dt_bias, l1_m2_a_log, l1_m2_d, l1_m2_norm_w)
    wo1 = l1_w_out.astype(BF16)
    x1, h, idx, gates = _mix_out1(x2, yc.reshape(n, cw), yd.reshape(n, dw), l1_s5_glu_w.astype(BF16),
                                  row(l1_s5_glu_b), wo1[:cw], wo1[cw:], row(l1_norm_post_mix),
                                  row(l1_norm_pre_ffn), l1_moe_router)
    block_expert, slot_tok, slot_dst, nact, n_rows = _moe_plan(idx, n)
    ys = _moe_experts(h, block_expert, slot_tok, slot_dst, nact, n_rows, l1_moe_w_gate.astype(BF16),
                      l1_moe_w_up.astype(BF16), l1_moe_w_down.astype(BF16))
    out = _moe_combine(x1, ys, gates, row(l1_norm_post_ffn))
    return out.reshape(b, t, d)
```

```python
import functools

import jax
import jax.numpy as jnp
from jax import lax
from jax.experimental import pallas as pl
from jax.experimental.pallas import tpu as pltpu

F32 = jnp.float32
BF16 = jnp.bfloat16

EPS = 1e-6
RWKV_GN_EPS = 64e-5
RWKV_HEAD = 64
RWKV_CHUNK = 64
GMLP_CHUNK = 128
GMLP_GROUPS = 4
S5_GROUP_CH = 16
S5_STATE = 64
S5_CHUNK = 64
SSD_HEAD = 64
SSD_HEADS = 8
SSD_GROUPS = 2
SSD_STATE = 128
SSD_CONV = 4
SSD_CHUNK = 128
MOE_EXPERTS = 8
MOE_ROWS = 1024
MXU_TILE = 256
LANES = 128
SUBLANES = 8
VMEM_LIMIT = 56 * 1024 * 1024


def _cparams(sem):
    return pltpu.CompilerParams(dimension_semantics=sem, vmem_limit_bytes=VMEM_LIMIT)


def _bdot(a, b):
    return jnp.dot(a.astype(BF16), b.astype(BF16), preferred_element_type=F32)


def _bdot_nt(a, b):
    return lax.dot_general(a.astype(BF16), b.astype(BF16), (((1,), (1,)), ((), ())),
                           preferred_element_type=F32)


def _bdot_tn(a, b):
    return lax.dot_general(a.astype(BF16), b.astype(BF16), (((0,), (0,)), ((), ())),
                           preferred_element_type=F32)


def _split3(x):
    h = x.astype(BF16)
    r1 = x - h.astype(F32)
    m = r1.astype(BF16)
    l = (r1 - m.astype(F32)).astype(BF16)
    return h, m, l


def _dot_x_exact(x, e):
    h, m, l = _split3(x)
    e = e.astype(BF16)
    return (jnp.dot(h, e, preferred_element_type=F32) + jnp.dot(m, e, preferred_element_type=F32)
            + jnp.dot(l, e, preferred_element_type=F32))


def _dot_exact_x(e, x):
    h, m, l = _split3(x)
    e = e.astype(BF16)
    return (jnp.dot(e, h, preferred_element_type=F32) + jnp.dot(e, m, preferred_element_type=F32)
            + jnp.dot(e, l, preferred_element_type=F32))


def _rms(x, g):
    return x * lax.rsqrt(jnp.mean(x * x, axis=-1, keepdims=True) + EPS) * g


def _sigmoid(x):
    return 1.0 / (1.0 + jnp.exp(-x))


def _silu(x):
    return x * _sigmoid(x)


def _softplus(x):
    return jnp.maximum(x, 0.0) + jnp.log(1.0 + jnp.exp(-jnp.abs(x)))


def _gelu_tanh(x):
    return 0.5 * x * (1.0 + jnp.tanh(0.7978845608028654 * (x + 0.044715 * x * x * x)))


def _iota2(shape, dim):
    return lax.broadcasted_iota(jnp.int32, shape, dim)


def _norm_proj_kernel(n_out, x_ref, g_ref, *refs):
    w_refs = refs[:n_out]
    o_refs = refs[n_out:]
    xn = _rms(x_ref[...], g_ref[...]).astype(BF16)
    for w_ref, o_ref in zip(w_refs, o_refs):
        o_ref[...] = jnp.dot(xn, w_ref[...], preferred_element_type=F32)


def _norm_proj(x2, g, ws, tm=512):
    n, d = x2.shape
    in_specs = [pl.BlockSpec((tm, d), lambda i: (i, 0)), pl.BlockSpec((1, d), lambda i: (0, 0))]
    in_specs += [pl.BlockSpec(w.shape, lambda i: (0, 0)) for w in ws]
    out_specs = [pl.BlockSpec((tm, w.shape[1]), lambda i: (i, 0)) for w in ws]
    out_shape = [jax.ShapeDtypeStruct((n, w.shape[1]), F32) for w in ws]
    return pl.pallas_call(
        functools.partial(_norm_proj_kernel, len(ws)),
        grid=(n // tm,), in_specs=in_specs, out_specs=out_specs, out_shape=out_shape,
        compiler_params=_cparams(("parallel",)), name="norm_proj",
    )(x2, g.reshape(1, d), *ws)


def _rwkv_kernel(heads, nb, p_ref, pp_ref, mu_ref, w0_ref, w2_ref, a0_ref, a2_ref, g2_ref, kk_ref,
                 ka_ref, rk_ref, lnw_ref, lnb_ref, gs_ref, o_ref, z_ref):
    c = pl.program_id(1)
    ln = RWKV_CHUNK
    hd = RWKV_HEAD
    aw = heads * hd

    @pl.when(c == 0)
    def _():
        z_ref[...] = jnp.zeros_like(z_ref)

    tril_f = jnp.where(_iota2((ln, ln), 0) >= _iota2((ln, ln), 1), 1.0, 0.0)
    rows = _iota2((ln, 1), 0)
    gs_tile = gs_ref[...]

    def gs(x):
        nt = x.shape[1] // LANES
        stacked = jnp.concatenate([x[:, j * LANES:(j + 1) * LANES] for j in range(nt)], axis=0)
        red = _dot_x_exact(stacked, gs_tile)
        return jnp.concatenate([red[j * ln:(j + 1) * ln] for j in range(nt)], axis=1)

    per_b = []
    for bi in range(nb):
        p = p_ref[bi]
        prev = jnp.where(c == 0, 0.0, pp_ref[bi][SUBLANES - 1:SUBLANES, :])
        ps = jnp.where(rows == 0, prev, pltpu.roll(p, 1, axis=0))
        pm = p + (ps - p) * mu_ref[...]
        r = pm[:, 0:aw]
        k = pm[:, aw:2 * aw]
        v = pm[:, 2 * aw:3 * aw]
        xw = pm[:, 3 * aw:3 * aw + LANES]
        xa = pm[:, 3 * aw + LANES:3 * aw + 2 * LANES]
        xg = pm[:, 3 * aw + 2 * LANES:3 * aw + 3 * LANES]
        w = w0_ref[...] + _bdot(jnp.tanh(xw), w2_ref[...])
        w = -_softplus(-w) - 0.5
        lw = -jnp.exp(w)
        a = _sigmoid(a0_ref[...] + _bdot(xa, a2_ref[...]))
        g = _bdot(_sigmoid(xg), g2_ref[...])
        kk = k * kk_ref[...]
        kk = kk / jnp.maximum(jnp.sqrt(gs(kk * kk)), 1e-12)
        kmod = k * (1.0 + (a - 1.0) * ka_ref[...])
        bonus = gs(r * kmod * rk_ref[...])
        bvec = kk * a
        cs = _dot_exact_x(tril_f, lw)
        cs_last = cs[ln - 1:ln, :]
        encs = jnp.exp(-cs)
        dec_end = jnp.exp(cs_last - cs)
        per_b.append(dict(
            v=v, g=g, bonus=bonus, rt=r * jnp.exp(cs), kt=kmod * encs, bt=bvec * encs,
            at=-kk * jnp.exp(cs - lw), bh=bvec * dec_end, kh=kmod * dec_end, wl=jnp.exp(cs_last)))

    lane = _iota2((ln, LANES), 1)
    lane_in = jnp.where(lane >= hd, lane - hd, lane)
    trow = _iota2((ln, LANES), 0)
    left = lane < hd
    tril_p = lane_in <= trow
    stril_p = lane_in < trow
    eye_p = lane_in == trow
    eye_pf = jnp.where(eye_p, 1.0, 0.0)

    def bd(x):
        xb = x.astype(BF16)
        zero = jnp.zeros_like(xb)
        return jnp.concatenate([jnp.where(left, xb, zero), jnp.where(left, zero, xb)], axis=0)

    def dot(a, b):
        return jnp.dot(a.astype(BF16), b, preferred_element_type=F32)

    npair = heads // 2
    pairs = [(bi, j) for bi in range(nb) for j in range(npair)]
    cut = lambda name: [per_b[bi][name][:, j * LANES:(j + 1) * LANES] for bi, j in pairs]
    at, rt, bt, kt, vv, bh, kh, wl = (cut(n) for n in ("at", "rt", "bt", "kt", "v", "bh", "kh", "wl"))
    z_all = z_ref[...]
    zs = [z_all[bi, j] for bi, j in pairs]
    npr = range(len(pairs))
    lhs = [jnp.concatenate([at[i], rt[i]], axis=0).astype(BF16) for i in npr]
    ab = [lax.dot_general(lhs[i], bd(bt[i]), (((1,), (1,)), ((), ())), preferred_element_type=F32)
          for i in npr]
    ak = [lax.dot_general(lhs[i], bd(kt[i]), (((1,), (1,)), ((), ())), preferred_element_type=F32)
          for i in npr]
    nmat = [jnp.where(stril_p, ab[i][:ln], 0.0) for i in npr]
    tinv = [eye_pf + nmat[i] for i in npr]
    npow = [dot(nmat[i], bd(nmat[i])) for i in npr]
    for step in range(5):
        bdn = [bd(npow[i]) for i in npr]
        tinv = [tinv[i] + dot(tinv[i], bdn[i]) for i in npr]
        if step < 4:
            npow = [dot(npow[i], bdn[i]) for i in npr]
    bdv = [bd(vv[i]) for i in npr]
    bdz = [bd(zs[i]) for i in npr]
    xmat = [dot(jnp.concatenate([jnp.where(stril_p, ak[i][:ln], 0.0), at[i]], axis=1),
                jnp.concatenate([bdv[i], bdz[i]], axis=0)) for i in npr]
    u = [dot(tinv[i], bd(xmat[i])) for i in npr]
    ys_p = [dot(jnp.concatenate([rt[i], jnp.where(tril_p, ab[i][ln:], 0.0),
                                 jnp.where(tril_p, ak[i][ln:], 0.0)], axis=1),
                jnp.concatenate([bdz[i], bd(u[i]), bdv[i]], axis=0)) for i in npr]
    cross = [_bdot_tn(jnp.concatenate([bh[i], kh[i]], axis=0), jnp.concatenate([u[i], vv[i]], axis=0))
             for i in npr]
    z_new = []
    for i in npr:
        dg = jnp.where(eye_p, wl[i], 0.0)
        wl_i = jnp.sum(jnp.where(left, dg, 0.0), axis=1, keepdims=True)
        wl_j = jnp.sum(jnp.where(left, 0.0, dg), axis=1, keepdims=True)
        z_new.append(jnp.where(left, wl_i, wl_j) * zs[i] + jnp.where(left, cross[i][:ln], cross[i][ln:]))
    z_ref[...] = jnp.stack(z_new, axis=0).reshape(z_ref.shape)

    inv = 1.0 / hd
    for bi in range(nb):
        y = jnp.concatenate(ys_p[bi * npair:(bi + 1) * npair], axis=1)
        mean = gs(y) * inv
        d = y - mean
        var = gs(d * d) * inv
        yn = d * lax.rsqrt(var + RWKV_GN_EPS) * lnw_ref[...] + lnb_ref[...]
        o_ref[bi] = (yn + per_b[bi]["bonus"] * per_b[bi]["v"]) * per_b[bi]["g"]


def _rwkv_mix(p_a, prm, heads, nb=2):
    b, t, cin = p_a.shape
    aw = heads * RWKV_HEAD
    ln = RWKV_CHUNK
    nc = t // ln
    full = lambda arr: pl.BlockSpec(arr.shape, lambda i, j: (0,) * arr.ndim)
    in_specs = [pl.BlockSpec((nb, ln, cin), lambda i, j: (i, j, 0)),
                pl.BlockSpec((nb, SUBLANES, cin),
                             lambda i, j: (i, jnp.maximum(j * (ln // SUBLANES) - 1, 0), 0))]
    in_specs += [full(x) for x in prm]
    return pl.pallas_call(
        functools.partial(_rwkv_kernel, heads, nb),
        grid=(b // nb, nc), in_specs=in_specs,
        out_specs=pl.BlockSpec((nb, ln, aw), lambda i, j: (i, j, 0)),
        out_shape=jax.ShapeDtypeStruct((b, t, aw), F32),
        scratch_shapes=[pltpu.VMEM((nb, heads // 2, RWKV_HEAD, 2 * RWKV_HEAD), F32)],
        compiler_params=_cparams(("parallel", "arbitrary")), name="rwkv7",
    )(p_a, p_a, *prm)


def _gmlp_kernel(p_ref, lnw_ref, lnb_ref, ws_ref, bs_ref, o_ref):
    x = _gelu_tanh(p_ref[0])
    ln = GMLP_CHUNK
    bw = x.shape[1] // 2
    gd = bw // GMLP_GROUPS
    tril = _iota2((ln, ln), 0) >= _iota2((ln, ln), 1)
    for gi in range(GMLP_GROUPS):
        u = x[:, gi * gd:(gi + 1) * gd]
        v = x[:, bw + gi * gd:bw + (gi + 1) * gd]
        mean = jnp.mean(v, axis=-1, keepdims=True)
        d = v - mean
        var = jnp.mean(d * d, axis=-1, keepdims=True)
        vn = d * lax.rsqrt(var + EPS) * lnw_ref[gi:gi + 1, :] + lnb_ref[gi:gi + 1, :]
        s = _bdot(jnp.where(tril, ws_ref[gi], 0.0), vn) + bs_ref[gi]
        o_ref[0, :, gi * gd:(gi + 1) * gd] = u * s


def _gmlp_mix(p_b, ln_w, ln_b, ws, bs):
    b, t, cin = p_b.shape
    bw = cin // 2
    gd = bw // GMLP_GROUPS
    ln = GMLP_CHUNK
    bs_b = jnp.broadcast_to(bs[:, :, None], (GMLP_GROUPS, ln, gd))
    full = lambda arr: pl.BlockSpec(arr.shape, lambda i, j: (0,) * arr.ndim)
    return pl.pallas_call(
        _gmlp_kernel, grid=(b, t // ln),
        in_specs=[pl.BlockSpec((1, ln, cin), lambda i, j: (i, j, 0)),
                  full(ln_w), full(ln_b), full(ws), full(bs_b)],
        out_specs=pl.BlockSpec((1, ln, bw), lambda i, j: (i, j, 0)),
        out_shape=jax.ShapeDtypeStruct((b, t, bw), F32),
        compiler_params=_cparams(("parallel", "parallel")), name="gmlp",
    )(p_b, ln_w, ln_b, ws, bs_b)


def _mix_out0_kernel(x_ref, ya_ref, yb_ref, wa_ref, wb_ref, g_ref, o_ref):
    y = _bdot(ya_ref[...], wa_ref[...]) + _bdot(yb_ref[...], wb_ref[...])
    o_ref[...] = x_ref[...] + _rms(y, g_ref[...])


def _mix_out0(x2, ya, yb, wa, wb, g, tm=512):
    n, d = x2.shape
    row = lambda arr: pl.BlockSpec((tm, arr.shape[1]), lambda i: (i, 0))
    full = lambda arr: pl.BlockSpec(arr.shape, lambda i: (0,) * arr.ndim)
    return pl.pallas_call(
        _mix_out0_kernel, grid=(n // tm,),
        in_specs=[row(x2), row(ya), row(yb), full(wa), full(wb), full(g)],
        out_specs=row(x2), out_shape=jax.ShapeDtypeStruct((n, d), F32),
        compiler_params=_cparams(("parallel",)), name="mix_out0",
    )(x2, ya, yb, wa, wb, g)


def _ffn_kernel(fc, x_ref, gpre_ref, wg_ref, wu_ref, wd_ref, gpost_ref, o_ref):
    h = _rms(x_ref[...], gpre_ref[...]).astype(BF16)
    acc = None
    for c in range(wg_ref.shape[1] // fc):
        cols = slice(c * fc, (c + 1) * fc)
        gate = jnp.dot(h, wg_ref[:, cols], preferred_element_type=F32)
        up = jnp.dot(h, wu_ref[:, cols], preferred_element_type=F32)
        part = jnp.dot((_silu(gate) * up).astype(BF16), wd_ref[cols, :], preferred_element_type=F32)
        acc = part if acc is None else acc + part
    o_ref[...] = x_ref[...] + _rms(acc, gpost_ref[...])


def _ffn(x2, gpre, wg, wu, wd, gpost, tm=512, fc=MXU_TILE):
    n, d = x2.shape
    full = lambda arr: pl.BlockSpec(arr.shape, lambda i: (0,) * arr.ndim)
    return pl.pallas_call(
        functools.partial(_ffn_kernel, fc), grid=(n // tm,),
        in_specs=[pl.BlockSpec((tm, d), lambda i: (i, 0)), full(gpre), full(wg), full(wu), full(wd),
                  full(gpost)],
        out_specs=pl.BlockSpec((tm, d), lambda i: (i, 0)),
        out_shape=jax.ShapeDtypeStruct((n, d), F32),
        compiler_params=_cparams(("parallel",)), name="ffn",
    )(x2, gpre, wg, wu, wd, gpost)


def _s5_kernel(nc, nb, u_ref, tap_ref, wsr_ref, wsi_ref, wcr_ref, wci_ref, alr_ref, ali_ref, d_ref, o_ref,
               toep_ref):
    u = u_ref[0]
    taps = tap_ref[0]
    ch = taps.shape[0]
    lane = _iota2(taps.shape, 1)
    for s in range(taps.shape[1] // ch):
        blk = taps if s == 0 else jnp.where(lane >= ch * s, pltpu.roll(taps, ch * s, axis=1), 0.0)
        toep_ref[ch * s:ch * (s + 1), :] = blk.astype(BF16)
    y = jnp.dot(u, toep_ref[...], preferred_element_type=F32)
    xer = jnp.dot(u, wsr_ref[0], preferred_element_type=F32)
    xei = jnp.dot(u, wsi_ref[0], preferred_element_type=F32)
    alr = alr_ref[0]
    ali = ali_ref[0]
    cr = jnp.zeros((nb, xer.shape[1]), F32)
    ci = jnp.zeros((nb, xer.shape[1]), F32)
    prs, pis = [], []
    for c in range(nc):
        prs.append(cr)
        pis.append(ci)
        er = xer[c * nb:(c + 1) * nb]
        ei = xei[c * nb:(c + 1) * nb]
        cr, ci = alr * cr - ali * ci + er, alr * ci + ali * cr + ei
    pr = jnp.concatenate(prs, axis=0)
    pi = jnp.concatenate(pis, axis=0)
    y = y + _bdot(pr, wcr_ref[0]) + _bdot(pi, wci_ref[0])
    o_ref[0] = y + d_ref[0] * u.astype(F32)


def _s5_weights(a_re, a_im, log_dt, b_re, b_im, c_re, c_im, d_skip, ln):
    g, st = a_re.shape
    ch = b_re.shape[2]
    dt = jnp.exp(log_dt)[:, None]
    lr, li = a_re, a_im
    tau = jnp.arange(ln + 1, dtype=F32)[:, None, None]
    mag = jnp.exp(lr[None] * dt[None] * tau)
    pw_r = mag * jnp.cos(li[None] * dt[None] * tau)
    pw_i = mag * jnp.sin(li[None] * dt[None] * tau)
    ab_r, ab_i = pw_r[1], pw_i[1]
    nr, ni = ab_r - 1.0, ab_i
    den = lr * lr + li * li
    fr, fi = (nr * lr + ni * li) / den, (ni * lr - nr * li) / den
    bb_r = fr[..., None] * b_re - fi[..., None] * b_im
    bb_i = fr[..., None] * b_im + fi[..., None] * b_re
    cp_r = c_re[None] * pw_r[:ln, :, None, :] - c_im[None] * pw_i[:ln, :, None, :]
    cp_i = c_re[None] * pw_i[:ln, :, None, :] + c_im[None] * pw_r[:ln, :, None, :]
    hp = lax.Precision.HIGHEST
    taps = (jnp.einsum('tgcp,gpd->gdtc', cp_r, bb_r, precision=hp)
            - jnp.einsum('tgcp,gpd->gdtc', cp_i, bb_i, precision=hp))
    taps = taps.reshape(g, ch, ln * ch)
    rev_r, rev_i = pw_r[:ln][::-1], pw_i[:ln][::-1]
    ws_r = rev_r[..., None] * bb_r[None] - rev_i[..., None] * bb_i[None]
    ws_i = rev_r[..., None] * bb_i[None] + rev_i[..., None] * bb_r[None]
    ws_r = ws_r.transpose(1, 0, 3, 2).reshape(g, ln * ch, st)
    ws_i = ws_i.transpose(1, 0, 3, 2).reshape(g, ln * ch, st)
    q_r, q_i = pw_r[1:ln + 1], pw_i[1:ln + 1]
    wc_r = c_re[None] * q_r[:, :, None, :] - c_im[None] * q_i[:, :, None, :]
    wc_i = -(c_re[None] * q_i[:, :, None, :] + c_im[None] * q_r[:, :, None, :])
    wc_r = wc_r.transpose(1, 3, 0, 2).reshape(g, st, ln * ch)
    wc_i = wc_i.transpose(1, 3, 0, 2).reshape(g, st, ln * ch)
    al_r = pw_r[ln].reshape(g, 1, st)
    al_i = pw_i[ln].reshape(g, 1, st)
    d_t = jnp.tile(d_skip.reshape(g, 1, ch), (1, ln, 1)).reshape(g, 1, ln * ch)
    return (taps, ws_r.astype(BF16), ws_i.astype(BF16), wc_r.astype(BF16),
            wc_i.astype(BF16), al_r, al_i, d_t)


def _s5_core(u, weights):
    b, t, cw = u.shape
    ln, ch = S5_CHUNK, S5_GROUP_CH
    g = cw // ch
    nc = t // ln
    ug = u.astype(BF16).reshape(b, nc, ln, g, ch).transpose(3, 1, 0, 2, 4).reshape(g, nc * b, ln * ch)
    per_g = lambda arr: pl.BlockSpec((1,) + arr.shape[1:], lambda i: (i, 0, 0))
    yg = pl.pallas_call(
        functools.partial(_s5_kernel, nc, b), grid=(g,),
        in_specs=[per_g(ug)] + [per_g(w) for w in weights],
        out_specs=per_g(ug), out_shape=jax.ShapeDtypeStruct(ug.shape, F32),
        scratch_shapes=[pltpu.VMEM((ln * ch, ln * ch), BF16)],
        compiler_params=_cparams(("parallel",)), name="s5",
    )(ug, *weights)
    return yg.reshape(g, nc, b, ln, ch).transpose(2, 1, 3, 0, 4).reshape(b, t, cw)


def _ssd_kernel(z_ref, xbc_ref, xp_ref, dt_ref, cw_ref, cb_ref, dtb_ref, alog_ref, dsk_ref, nw_ref,
                o_ref, s_ref):
    c = pl.program_id(1)
    ln = SSD_CHUNK
    hd = SSD_HEAD
    dw = SSD_HEADS * hd
    gn = SSD_STATE

    @pl.when(c == 0)
    def _():
        s_ref[...] = jnp.zeros_like(s_ref)

    xbc = xbc_ref[0]
    prev = jnp.where(c == 0, 0.0, xp_ref[0])
    full = jnp.concatenate([prev, xbc], axis=0)
    conv = cb_ref[...]
    for j in range(SSD_CONV):
        off = SUBLANES - (SSD_CONV - 1) + j
        conv = conv + cw_ref[j:j + 1, :] * full[off:off + ln]
    act = _silu(conv)
    xh = act[:, :dw]
    dt = _softplus(dt_ref[0] + dtb_ref[...])
    adt = -jnp.exp(alog_ref[...]) * dt
    tril = _iota2((ln, ln), 0) >= _iota2((ln, ln), 1)
    acs = _dot_exact_x(jnp.where(tril, 1.0, 0.0), adt)
    acs_t = acs.T
    tot = acs[ln - 1:ln, :]
    hg = SSD_HEADS // SSD_GROUPS
    s_all = s_ref[...]
    y_heads, s_heads = [], []
    for gi in range(SSD_GROUPS):
        bm = act[:, dw + gi * gn:dw + (gi + 1) * gn]
        cm = act[:, dw + SSD_GROUPS * gn + gi * gn:dw + SSD_GROUPS * gn + (gi + 1) * gn]
        cb = _bdot_nt(cm, bm)
        for hh in range(hg):
            h = gi * hg + hh
            sl = slice(h * hd, (h + 1) * hd)
            col = acs[:, h:h + 1]
            rowv = acs_t[h:h + 1, :]
            lmat = jnp.exp(jnp.where(tril, col - rowv, -jnp.inf))
            xh_h = xh[:, sl]
            xdt = xh_h * dt[:, h:h + 1]
            tot_h = tot[:, h:h + 1]
            st = s_all[h]
            y_h = _bdot(cb * lmat, xdt) + jnp.exp(col) * _bdot(cm, st)
            s_heads.append(jnp.exp(tot_h) * st + _bdot_tn(bm * jnp.exp(tot_h - col), xdt))
            y_heads.append(y_h + dsk_ref[:, sl] * xh_h)
    s_ref[...] = jnp.stack(s_heads, axis=0)
    y = jnp.concatenate(y_heads, axis=1) * _silu(z_ref[0])
    gw = dw // SSD_GROUPS
    for gi in range(SSD_GROUPS):
        yg = y[:, gi * gw:(gi + 1) * gw]
        yg = yg * lax.rsqrt(jnp.mean(yg * yg, axis=-1, keepdims=True) + EPS)
        o_ref[0, :, gi * gw:(gi + 1) * gw] = yg * nw_ref[:, gi * gw:(gi + 1) * gw]


def _ssd_mix(z, xbc, dtp, conv_w, conv_b, dt_bias, a_log, d_skip, norm_w):
    b, t, dw = z.shape
    ln = SSD_CHUNK
    xw = xbc.shape[2]
    pad = lambda vec: jnp.pad(vec, (0, LANES - vec.shape[0])).reshape(1, LANES)
    dsk = jnp.repeat(d_skip, SSD_HEAD).reshape(1, dw)
    prm = [conv_w, conv_b.reshape(1, xw), pad(dt_bias), pad(a_log), dsk, norm_w.reshape(1, dw)]
    full = lambda arr: pl.BlockSpec(arr.shape, lambda i, j: (0,) * arr.ndim)
    blk = lambda w: pl.BlockSpec((1, ln, w), lambda i, j: (i, j, 0))
    return pl.pallas_call(
        _ssd_kernel, grid=(b, t // ln),
        in_specs=[blk(dw), blk(xw),
                  pl.BlockSpec((1, SUBLANES, xw),
                               lambda i, j: (i, jnp.maximum(j * (ln // SUBLANES) - 1, 0), 0)),
                  blk(LANES)] + [full(x) for x in prm],
        out_specs=blk(dw), out_shape=jax.ShapeDtypeStruct((b, t, dw), F32),
        scratch_shapes=[pltpu.VMEM((SSD_HEADS, SSD_STATE, SSD_HEAD), F32)],
        compiler_params=_cparams(("parallel", "arbitrary")), name="ssd",
    )(z, xbc, xbc, dtp, *prm)


def _mix_out1_kernel(x_ref, yc_ref, yd_ref, gw_ref, gb_ref, wc_ref, wd_ref, gpost_ref, gpre_ref,
                     wrh_ref, wrl_ref, x1_ref, h_ref, idx_ref, gate_ref):
    yc = _gelu_tanh(yc_ref[...])
    yc = yc * _sigmoid(_bdot(yc, gw_ref[...]) + gb_ref[...])
    y = _bdot(yc, wc_ref[...]) + _bdot(yd_ref[...], wd_ref[...])
    x1 = x_ref[...] + _rms(y, gpost_ref[...])
    x1_ref[...] = x1
    h = _rms(x1, gpre_ref[...])
    h_ref[...] = h
    hh = h.astype(BF16)
    hl = (h - hh.astype(F32)).astype(BF16)
    wrh = wrh_ref[...]
    logits = (jnp.dot(hh, wrh, preferred_element_type=F32) + jnp.dot(hl, wrh, preferred_element_type=F32)
              + jnp.dot(hh, wrl_ref[...], preferred_element_type=F32))
    lane = _iota2(logits.shape, 1)
    lane_f = lane.astype(F32)
    logits = jnp.where(lane < MOE_EXPERTS, logits, -jnp.inf)
    m1 = jnp.max(logits, axis=-1, keepdims=True)
    i1 = jnp.min(jnp.where(logits == m1, lane_f, float(LANES)), axis=-1, keepdims=True)
    rest = jnp.where(lane_f == i1, -jnp.inf, logits)
    m2 = jnp.max(rest, axis=-1, keepdims=True)
    i2 = jnp.min(jnp.where(rest == m2, lane_f, float(LANES)), axis=-1, keepdims=True)
    e2 = jnp.exp(m2 - m1)
    g1 = 1.0 / (1.0 + e2)
    g2 = e2 / (1.0 + e2)
    idx_ref[...] = jnp.where(lane == 0, i1, jnp.where(lane == 1, i2, 0.0)).astype(jnp.int32)
    gate_ref[...] = jnp.where(lane == 0, g1, jnp.where(lane == 1, g2, 0.0))


def _mix_out1(x2, yc, yd, glu_w, glu_b, wc, wd, gpost, gpre, wr, tm=512):
    n, d = x2.shape
    wr_p = jnp.pad(wr, ((0, 0), (0, LANES - wr.shape[1])))
    wrh = wr_p.astype(BF16)
    wrl = (wr_p - wrh.astype(F32)).astype(BF16)
    row = lambda w: pl.BlockSpec((tm, w), lambda i: (i, 0))
    full = lambda arr: pl.BlockSpec(arr.shape, lambda i: (0,) * arr.ndim)
    prm = [glu_w, glu_b, wc, wd, gpost, gpre, wrh, wrl]
    return pl.pallas_call(
        _mix_out1_kernel, grid=(n // tm,),
        in_specs=[row(d), row(yc.shape[1]), row(yd.shape[1])] + [full(p) for p in prm],
        out_specs=[row(d), row(d), row(LANES), row(LANES)],
        out_shape=[jax.ShapeDtypeStruct((n, d), F32), jax.ShapeDtypeStruct((n, d), F32),
                   jax.ShapeDtypeStruct((n, LANES), jnp.int32), jax.ShapeDtypeStruct((n, LANES), F32)],
        compiler_params=_cparams(("parallel",)), name="mix_out1",
    )(x2, yc, yd, *prm)


GATHER_UNROLL = 8


def _gather_rows(n_rows, make_copy):
    def body(j, carry):
        for q in range(GATHER_UNROLL):
            make_copy(j * GATHER_UNROLL + q).start(priority=q % 2)
        return carry

    lax.fori_loop(0, n_rows // GATHER_UNROLL, body, 0)


def _moe_kernel(nf, be_ref, tok_ref, dst_ref, nact_ref, h_hbm, wg_ref, wu_ref, wd_ref, y_hbm, buf_ref,
                xb_ref, gsem, ssem):
    i = pl.program_id(0)
    f = pl.program_id(1)
    n_blocks = pl.num_programs(0)
    tm = buf_ref.shape[1]
    nact = nact_ref[0]
    active = i < nact
    slot = lax.rem(i, 2)
    other = 1 - slot
    xs = lambda sl: buf_ref.at[sl]
    yb = lambda sl: buf_ref.at[2 + sl]

    def gather_copy(block, sl, r):
        tok = tok_ref[block * tm + r]
        return pltpu.make_async_copy(h_hbm.at[pl.ds(tok, 1)], buf_ref.at[sl, pl.ds(r, 1)], gsem.at[sl])

    def scatter_copy(block, sl, r):
        dst = dst_ref[block * tm + r]
        return pltpu.make_async_copy(buf_ref.at[2 + sl, pl.ds(r, 1)], y_hbm.at[pl.ds(dst, 1)], ssem.at[sl])

    def wait_rows(sem_slot_ref, buf):
        pltpu.make_async_copy(h_hbm.at[pl.ds(0, tm)], buf, sem_slot_ref).wait()

    @pl.when(f == 0)
    def _():
        @pl.when(i == 0)
        def _():
            _gather_rows(tm, functools.partial(gather_copy, 0, 0))
            buf_ref[3] = jnp.zeros(buf_ref.shape[1:], F32)
            n_real = y_hbm.shape[0] - 2 * tm
            for half in range(2):
                init = pltpu.make_async_copy(yb(1), y_hbm.at[pl.ds(n_real + half * tm, tm)], ssem.at[0])
                init.start()
                init.wait()

        @pl.when(i <= nact)
        def _():
            wait_rows(gsem.at[slot], xs(slot))

        @pl.when(jnp.logical_and(i >= 1, i <= nact))
        def _():
            wait_rows(ssem.at[slot], yb(slot))

        @pl.when(i == nact)
        def _():
            _gather_rows(tm, functools.partial(scatter_copy, i - 1, other))
            wait_rows(ssem.at[other], yb(other))

        @pl.when(active)
        def _():
            xb_ref[...] = buf_ref[slot].astype(BF16)
            buf_ref[2 + slot] = jnp.zeros(buf_ref.shape[1:], F32)

    @pl.when(active)
    def _():
        x = xb_ref[...]
        nch = wg_ref.shape[2] // MXU_TILE
        rows_f = tm // nf
        base = f * rows_f
        prev = jnp.where(i == 0, n_blocks - 1, i - 1)
        for c in range(nch):
            for r in range(rows_f * c // nch, rows_f * (c + 1) // nch):
                gather_copy(i + 1, other, base + r).start(priority=0)
                scatter_copy(prev, other, base + r).start(priority=1)
            cols = slice(c * MXU_TILE, (c + 1) * MXU_TILE)
            gate = jnp.dot(x, wg_ref[0, :, cols], preferred_element_type=F32)
            up = jnp.dot(x, wu_ref[0, :, cols], preferred_element_type=F32)
            buf_ref[2 + slot] += jnp.dot((_silu(gate) * up).astype(BF16), wd_ref[0, cols, :],
                                         preferred_element_type=F32)


def _moe_experts(h, block_expert, slot_tok, slot_dst, nact, n_rows, wg, wu, wd, tf=7 * MXU_TILE):
    n, d = h.shape
    tm = MOE_ROWS
    n_blocks = slot_tok.shape[0] // tm
    ff = wg.shape[2]
    nf = ff // tf
    wmap = lambda i, j, be, tok, dst, na: (be[i], 0, j)
    grid_spec = pltpu.PrefetchScalarGridSpec(
        num_scalar_prefetch=4, grid=(n_blocks, nf),
        in_specs=[pl.BlockSpec(memory_space=pl.ANY),
                  pl.BlockSpec((1, d, tf), wmap), pl.BlockSpec((1, d, tf), wmap),
                  pl.BlockSpec((1, tf, d), lambda i, j, be, tok, dst, na: (be[i], j, 0))],
        out_specs=pl.BlockSpec(memory_space=pl.ANY),
        scratch_shapes=[pltpu.VMEM((4, tm, d), F32), pltpu.VMEM((tm, d), BF16),
                        pltpu.SemaphoreType.DMA((2,)), pltpu.SemaphoreType.DMA((2,))])
    return pl.pallas_call(
        functools.partial(_moe_kernel, nf), grid_spec=grid_spec,
        out_shape=jax.ShapeDtypeStruct((n_rows, d), F32),
        compiler_params=pltpu.CompilerParams(dimension_semantics=("arbitrary", "arbitrary"),
                                             vmem_limit_bytes=VMEM_LIMIT, disable_bounds_checks=True),
        name="moe_experts",
    )(block_expert, slot_tok, slot_dst, nact, h, wg, wu, wd)


def _combine_kernel(x_ref, y0_ref, y1_ref, gate_ref, gpost_ref, o_ref):
    gates = gate_ref[...]
    y = gates[:, 0:1] * y0_ref[...] + gates[:, 1:2] * y1_ref[...]
    o_ref[...] = x_ref[...] + _rms(y, gpost_ref[...])


def _moe_combine(x1, y, gates, gpost, tm=512):
    n, d = x1.shape
    nt = n // tm
    return pl.pallas_call(
        _combine_kernel, grid=(nt,),
        in_specs=[pl.BlockSpec((tm, d), lambda i: (i, 0)), pl.BlockSpec((tm, d), lambda i: (i, 0)),
                  pl.BlockSpec((tm, d), lambda i: (nt + i, 0)),
                  pl.BlockSpec((tm, LANES), lambda i: (i, 0)), pl.BlockSpec((1, d), lambda i: (0, 0))],
        out_specs=pl.BlockSpec((tm, d), lambda i: (i, 0)),
        out_shape=jax.ShapeDtypeStruct((n, d), F32),
        compiler_params=_cparams(("parallel",)), name="moe_combine",
    )(x1, y, y, gates, gpost)


def _moe_plan(idx, n):
    tm = MOE_ROWS
    flat_e = idx[:, :2].reshape(-1)
    onehot = (flat_e[:, None] == jnp.arange(MOE_EXPERTS, dtype=jnp.int32)[None, :]).astype(jnp.int32)
    csum = jnp.cumsum(onehot, axis=0)
    counts = csum[-1]
    rank = jnp.sum((csum - onehot) * onehot, axis=1)
    padded = (counts + tm - 1) // tm * tm
    pend = jnp.cumsum(padded)
    pstart = pend - padded
    dest = (jnp.sum(onehot * pstart[None, :], axis=1) + rank).astype(jnp.int32)
    n_blocks = (2 * n) // tm + MOE_EXPERTS + 1
    n_slots = n_blocks * tm
    slot_pair = jnp.full((n_slots,), -1, jnp.int32).at[dest].set(jnp.arange(2 * n, dtype=jnp.int32))
    real = slot_pair >= 0
    slot_tok = jnp.where(real, slot_pair // 2, 0)
    s_id = jnp.arange(n_slots, dtype=jnp.int32)
    slot_dst = jnp.where(real, slot_pair % 2 * n + slot_pair // 2, 2 * n + (s_id // tm) % 2 * tm + s_id % tm)
    block_start = jnp.arange(n_blocks, dtype=jnp.int32) * tm
    block_expert = jnp.minimum(jnp.searchsorted(pend, block_start, side='right'),
                               MOE_EXPERTS - 1).astype(jnp.int32)
    nact = (pend[-1] // tm).astype(jnp.int32).reshape(1)
    return block_expert, slot_tok, slot_dst, nact, 2 * n + 2 * tm


def kernel(x, l0_norm_pre_mix, l0_w_in, l0_rwkv_mu, l0_rwkv_w0, l0_rwkv_w2, l0_rwkv_a0, l0_rwkv_a2, l0_rwkv_g2, l0_rwkv_k_k, l0_rwkv_k_a, l0_rwkv_r_k, l0_rwkv_ln_w, l0_rwkv_ln_b, l0_gmlp_ln_w, l0_gmlp_ln_b, l0_gmlp_ws, l0_gmlp_bs, l0_w_out, l0_norm_post_mix, l0_norm_pre_ffn, l0_ffn_w_gate, l0_ffn_w_up, l0_ffn_w_down, l0_norm_post_ffn, l1_norm_pre_mix, l1_w_in, l1_s5_a_re, l1_s5_a_im, l1_s5_log_dt, l1_s5_b_re, l1_s5_b_im, l1_s5_c_re, l1_s5_c_im, l1_s5_d, l1_s5_glu_w, l1_s5_glu_b, l1_m2_conv_w, l1_m2_conv_b, l1_m2_dt_bias, l1_m2_a_log, l1_m2_d, l1_m2_norm_w, l1_w_out, l1_norm_post_mix, l1_norm_pre_ffn, l1_moe_router, l1_moe_w_gate, l1_moe_w_up, l1_moe_w_down, l1_norm_post_ffn):
    b, t, d = x.shape
    n = b * t
    x2 = x.reshape(n, d)
    row = lambda vec: vec.reshape(1, -1)

    aw = l0_rwkv_w0.shape[0]
    heads = aw // RWKV_HEAD
    lw_, la_, lg_ = l0_rwkv_w2.shape[0], l0_rwkv_a2.shape[0], l0_rwkv_g2.shape[0]
    a_in = 3 * aw + lw_ + la_ + lg_
    padc = lambda m, wdt: jnp.pad(m, ((0, 0), (0, LANES - wdt)))
    o = 3 * aw
    w_a = jnp.concatenate([l0_w_in[:, :o], padc(l0_w_in[:, o:o + lw_], lw_),
                           padc(l0_w_in[:, o + lw_:o + lw_ + la_], la_),
                           padc(l0_w_in[:, o + lw_ + la_:a_in], lg_)], axis=1).astype(BF16)
    w_b = l0_w_in[:, a_in:].astype(BF16)
    p_a, p_b = _norm_proj(x2, l0_norm_pre_mix, [w_a, w_b])
    padv = lambda vec, wdt: jnp.pad(vec, (0, LANES - wdt))
    mu = l0_rwkv_mu
    mu_p = jnp.concatenate([mu[:o], padv(mu[o:o + lw_], lw_), padv(mu[o + lw_:o + lw_ + la_], la_),
                            padv(mu[o + lw_ + la_:], lg_)])
    padr = lambda m: jnp.pad(m, ((0, LANES - m.shape[0]), (0, 0))).astype(BF16)
    hid = jnp.arange(LANES, dtype=jnp.int32) // RWKV_HEAD
    gsum = (hid[:, None] == hid[None, :]).astype(BF16)
    rwkv_prm = [row(mu_p), row(l0_rwkv_w0), padr(l0_rwkv_w2), row(l0_rwkv_a0), padr(l0_rwkv_a2),
                padr(l0_rwkv_g2), row(l0_rwkv_k_k), row(l0_rwkv_k_a), row(l0_rwkv_r_k),
                row(l0_rwkv_ln_w), row(l0_rwkv_ln_b), gsum]
    ya = _rwkv_mix(p_a.reshape(b, t, -1), rwkv_prm, heads)
    yb = _gmlp_mix(p_b.reshape(b, t, -1), l0_gmlp_ln_w, l0_gmlp_ln_b, l0_gmlp_ws, l0_gmlp_bs)
    wo = l0_w_out.astype(BF16)
    x2 = _mix_out0(x2, ya.reshape(n, -1), yb.reshape(n, -1), wo[:aw], wo[aw:], row(l0_norm_post_mix))
    x2 = _ffn(x2, row(l0_norm_pre_ffn), l0_ffn_w_gate.astype(BF16), l0_ffn_w_up.astype(BF16),
              l0_ffn_w_down.astype(BF16), row(l0_norm_post_ffn))

    cw = l1_s5_d.shape[0]
    dw = l1_m2_norm_w.shape[0]
    xw = l1_m2_conv_w.shape[1]
    nh = l1_m2_dt_bias.shape[0]
    w1 = l1_w_in
    w_parts = [w1[:, :cw], w1[:, cw:cw + dw], w1[:, cw + dw:cw + dw + xw],
               padc(w1[:, cw + dw + xw:], nh)]
    u_c, z_d, xbc, dtp = _norm_proj(x2, l1_norm_pre_mix, [w.astype(BF16) for w in w_parts])
    s5_w = _s5_weights(l1_s5_a_re, l1_s5_a_im, l1_s5_log_dt, l1_s5_b_re, l1_s5_b_im, l1_s5_c_re,
                       l1_s5_c_im, l1_s5_d, S5_CHUNK)
    yc = _s5_core(u_c.reshape(b, t, cw), s5_w)
    yd = _ssd_mix(z_d.reshape(b, t, dw), xbc.reshape(b, t, xw), dtp.reshape(b, t, LANES),
                  l1_m2_conv_w, l1_m2_conv_b, l1_m2_dt_bias, l1_m2_a_log, l1_m2_d, l1_m2_norm_w)
    wo1 = l1_w_out.astype(BF16)
    x1, h, idx, gates = _mix_out1(x2, yc.reshape(n, cw), yd.reshape(n, dw), l1_s5_glu_w.astype(BF16),
                                  row(l1_s5_glu_b), wo1[:cw], wo1[cw:], row(l1_norm_post_mix),
                                  row(l1_norm_pre_ffn), l1_moe_router)
    block_expert, slot_tok, slot_dst, nact, n_rows = _moe_plan(idx, n)
    ys = _moe_experts(h, block_expert, slot_tok, slot_dst, nact, n_rows, l1_moe_w_gate.astype(BF16),
                      l1_moe_w_up.astype(BF16), l1_moe_w_down.astype(BF16))
    out = _moe_combine(x1, ys, gates, row(l1_norm_post_ffn))
    return out.reshape(b, t, d)
```

```python
import functools

import jax
import jax.numpy as jnp
from jax import lax
from jax.experimental import pallas as pl
from jax.experimental.pallas import tpu as pltpu

F32 = jnp.float32
BF16 = jnp.bfloat16

EPS = 1e-6
RWKV_GN_EPS = 64e-5
RWKV_HEAD = 64
RWKV_CHUNK = 64
GMLP_CHUNK = 128
GMLP_GROUPS = 4
S5_GROUP_CH = 16
S5_STATE = 64
S5_CHUNK = 64
SSD_HEAD = 64
SSD_HEADS = 8
SSD_GROUPS = 2
SSD_STATE = 128
SSD_CONV = 4
SSD_CHUNK = 128
MOE_EXPERTS = 8
MOE_ROWS = 512
MXU_TILE = 256
LANES = 128
SUBLANES = 8
VMEM_LIMIT = 56 * 1024 * 1024


def _cparams(sem):
    return pltpu.CompilerParams(dimension_semantics=sem, vmem_limit_bytes=VMEM_LIMIT)


def _bdot(a, b):
    return jnp.dot(a.astype(BF16), b.astype(BF16), preferred_element_type=F32)


def _bdot_nt(a, b):
    return lax.dot_general(a.astype(BF16), b.astype(BF16), (((1,), (1,)), ((), ())),
                           preferred_element_type=F32)


def _bdot_tn(a, b):
    return lax.dot_general(a.astype(BF16), b.astype(BF16), (((0,), (0,)), ((), ())),
                           preferred_element_type=F32)


def _split3(x):
    h = x.astype(BF16)
    r1 = x - h.astype(F32)
    m = r1.astype(BF16)
    l = (r1 - m.astype(F32)).astype(BF16)
    return h, m, l


def _dot_x_exact(x, e):
    h, m, l = _split3(x)
    e = e.astype(BF16)
    return (jnp.dot(h, e, preferred_element_type=F32) + jnp.dot(m, e, preferred_element_type=F32)
            + jnp.dot(l, e, preferred_element_type=F32))


def _dot_exact_x(e, x):
    h, m, l = _split3(x)
    e = e.astype(BF16)
    return (jnp.dot(e, h, preferred_element_type=F32) + jnp.dot(e, m, preferred_element_type=F32)
            + jnp.dot(e, l, preferred_element_type=F32))


def _rms(x, g):
    return x * lax.rsqrt(jnp.mean(x * x, axis=-1, keepdims=True) + EPS) * g


def _sigmoid(x):
    return 1.0 / (1.0 + jnp.exp(-x))


def _silu(x):
    return x * _sigmoid(x)


def _softplus(x):
    return jnp.maximum(x, 0.0) + jnp.log(1.0 + jnp.exp(-jnp.abs(x)))


def _gelu_tanh(x):
    return 0.5 * x * (1.0 + jnp.tanh(0.7978845608028654 * (x + 0.044715 * x * x * x)))


def _iota2(shape, dim):
    return lax.broadcasted_iota(jnp.int32, shape, dim)


def _norm_proj_kernel(n_out, x_ref, g_ref, *refs):
    w_refs = refs[:n_out]
    o_refs = refs[n_out:]
    xn = _rms(x_ref[...], g_ref[...]).astype(BF16)
    for w_ref, o_ref in zip(w_refs, o_refs):
        o_ref[...] = jnp.dot(xn, w_ref[...], preferred_element_type=F32)


def _norm_proj(x2, g, ws, tm=512):
    n, d = x2.shape
    in_specs = [pl.BlockSpec((tm, d), lambda i: (i, 0)), pl.BlockSpec((1, d), lambda i: (0, 0))]
    in_specs += [pl.BlockSpec(w.shape, lambda i: (0, 0)) for w in ws]
    out_specs = [pl.BlockSpec((tm, w.shape[1]), lambda i: (i, 0)) for w in ws]
    out_shape = [jax.ShapeDtypeStruct((n, w.shape[1]), F32) for w in ws]
    return pl.pallas_call(
        functools.partial(_norm_proj_kernel, len(ws)),
        grid=(n // tm,), in_specs=in_specs, out_specs=out_specs, out_shape=out_shape,
        compiler_params=_cparams(("parallel",)), name="norm_proj",
    )(x2, g.reshape(1, d), *ws)


PRE_NAMES = ("v", "g", "bonus", "rt", "kt", "bt", "at", "bh", "kh")


def _rwkv_kernel(heads, nb, p_ref, pp_ref, mu_ref, w0_ref, w2_ref, a0_ref, a2_ref, g2_ref, kk_ref,
                 ka_ref, rk_ref, lnw_ref, lnb_ref, gs_ref, o_ref, z_ref, pre_ref, wl_ref):
    c = pl.program_id(1)
    ln = RWKV_CHUNK
    hd = RWKV_HEAD
    aw = heads * hd

    @pl.when(c == 0)
    def _():
        z_ref[...] = jnp.zeros_like(z_ref)
        pre_ref[...] = jnp.zeros_like(pre_ref)
        wl_ref[...] = jnp.zeros_like(wl_ref)

    tril_f = jnp.where(_iota2((ln, ln), 0) >= _iota2((ln, ln), 1), 1.0, 0.0)
    rows = _iota2((ln, 1), 0)
    gs_tile = gs_ref[...]

    def gs(x):
        nt = x.shape[1] // LANES
        stacked = jnp.concatenate([x[:, j * LANES:(j + 1) * LANES] for j in range(nt)], axis=0)
        red = _dot_x_exact(stacked, gs_tile)
        return jnp.concatenate([red[j * ln:(j + 1) * ln] for j in range(nt)], axis=1)

    pre_idx = {nm: idx for idx, nm in enumerate(PRE_NAMES)}
    pending = []

    def prep_steps():
        for bi in range(nb):
            p = p_ref[bi]
            prev = jnp.where(c == 0, 0.0, pp_ref[bi][SUBLANES - 1:SUBLANES, :])
            ps = jnp.where(rows == 0, prev, pltpu.roll(p, 1, axis=0))
            pm = p + (ps - p) * mu_ref[...]
            r = pm[:, 0:aw]
            k = pm[:, aw:2 * aw]
            v = pm[:, 2 * aw:3 * aw]
            xw = pm[:, 3 * aw:3 * aw + LANES]
            xa = pm[:, 3 * aw + LANES:3 * aw + 2 * LANES]
            xg = pm[:, 3 * aw + 2 * LANES:3 * aw + 3 * LANES]
            yield
            w = w0_ref[...] + _bdot(jnp.tanh(xw), w2_ref[...])
            a = _sigmoid(a0_ref[...] + _bdot(xa, a2_ref[...]))
            g = _bdot(_sigmoid(xg), g2_ref[...])
            yield
            w = -_softplus(-w) - 0.5
            lw = -jnp.exp(w)
            kk = k * kk_ref[...]
            kk_ss = gs(kk * kk)
            yield
            cs = _dot_exact_x(tril_f, lw)
            kk = kk / jnp.maximum(jnp.sqrt(kk_ss), 1e-12)
            kmod = k * (1.0 + (a - 1.0) * ka_ref[...])
            yield
            bonus = gs(r * kmod * rk_ref[...])
            bvec = kk * a
            cs_last = cs[ln - 1:ln, :]
            encs = jnp.exp(-cs)
            yield
            dec_end = jnp.exp(cs_last - cs)
            nxt = dict(v=v, g=g, bonus=bonus, rt=r * jnp.exp(cs), kt=kmod * encs, bt=bvec * encs,
                       at=-kk * jnp.exp(cs - lw), bh=bvec * dec_end, kh=kmod * dec_end)
            pending.append((bi, nxt, jnp.broadcast_to(jnp.exp(cs_last), (SUBLANES, aw))))
            yield

    prep = prep_steps()
    tick = lambda: next(prep, None)

    lane = _iota2((ln, LANES), 1)
    lane_in = jnp.where(lane >= hd, lane - hd, lane)
    trow = _iota2((ln, LANES), 0)
    left = lane < hd
    tril_p = lane_in <= trow
    stril_p = lane_in < trow
    eye_p = lane_in == trow
    eye_pf = jnp.where(eye_p, 1.0, 0.0)

    def bd(x):
        xb = x.astype(BF16)
        zero = jnp.zeros_like(xb)
        return jnp.concatenate([jnp.where(left, xb, zero), jnp.where(left, zero, xb)], axis=0)

    def dot(a, b):
        return jnp.dot(a.astype(BF16), b, preferred_element_type=F32)

    npair = heads // 2
    pairs = [(bi, j) for bi in range(nb) for j in range(npair)]

    class _Tiles:
        def __init__(self, name):
            self.idx = pre_idx[name]

        def __getitem__(self, i):
            bi, j = pairs[i]
            return pre_ref[self.idx, bi, :, j * LANES:(j + 1) * LANES]

    at, rt, bt, kt, vv, bh, kh = (_Tiles(n) for n in ("at", "rt", "bt", "kt", "v", "bh", "kh"))
    wl = [wl_ref[bi, 0:1, j * LANES:(j + 1) * LANES] for bi, j in pairs]
    z_all = z_ref[...]
    zs = [z_all[bi, j] for bi, j in pairs]
    npr = range(len(pairs))
    lhs = [jnp.concatenate([at[i], rt[i]], axis=0).astype(BF16) for i in npr]
    ab = [lax.dot_general(lhs[i], bd(bt[i]), (((1,), (1,)), ((), ())), preferred_element_type=F32)
          for i in npr]
    ak = [lax.dot_general(lhs[i], bd(kt[i]), (((1,), (1,)), ((), ())), preferred_element_type=F32)
          for i in npr]
    tick()
    nmat = [jnp.where(stril_p, ab[i][:ln], 0.0) for i in npr]
    tinv = [eye_pf + nmat[i] for i in npr]
    npow = [dot(nmat[i], bd(nmat[i])) for i in npr]
    tick()
    for step in range(5):
        bdn = [bd(npow[i]) for i in npr]
        tinv = [tinv[i] + dot(tinv[i], bdn[i]) for i in npr]
        if step < 4:
            npow = [dot(npow[i], bdn[i]) for i in npr]
        tick()
    bdv = [bd(vv[i]) for i in npr]
    bdz = [bd(zs[i]) for i in npr]
    xmat = [dot(jnp.concatenate([jnp.where(stril_p, ak[i][:ln], 0.0), at[i]], axis=1),
                jnp.concatenate([bdv[i], bdz[i]], axis=0)) for i in npr]
    tick()
    u = [dot(tinv[i], bd(xmat[i])) for i in npr]
    tick()
    ys_p = [dot(jnp.concatenate([rt[i], jnp.where(tril_p, ab[i][ln:], 0.0),
                                 jnp.where(tril_p, ak[i][ln:], 0.0)], axis=1),
                jnp.concatenate([bdz[i], bd(u[i]), bdv[i]], axis=0)) for i in npr]
    tick()
    cross = [_bdot_tn(jnp.concatenate([bh[i], kh[i]], axis=0), jnp.concatenate([u[i], vv[i]], axis=0))
             for i in npr]
    tick()
    z_new = []
    for i in npr:
        dg = jnp.where(eye_p, wl[i], 0.0)
        wl_i = jnp.sum(jnp.where(left, dg, 0.0), axis=1, keepdims=True)
        wl_j = jnp.sum(jnp.where(left, 0.0, dg), axis=1, keepdims=True)
        z_new.append(jnp.where(left, wl_i, wl_j) * zs[i] + jnp.where(left, cross[i][:ln], cross[i][ln:]))
    z_ref[...] = jnp.stack(z_new, axis=0).reshape(z_ref.shape)

    tick()
    inv = 1.0 / hd
    for bi in range(nb):
        y = jnp.concatenate(ys_p[bi * npair:(bi + 1) * npair], axis=1)
        mean = gs(y) * inv
        d = y - mean
        var = gs(d * d) * inv
        yn = d * lax.rsqrt(var + RWKV_GN_EPS) * lnw_ref[...] + lnb_ref[...]
        o_ref[bi] = ((yn + pre_ref[pre_idx["bonus"], bi] * pre_ref[pre_idx["v"], bi])
                     * pre_ref[pre_idx["g"], bi])
        tick()
    for _ in prep:
        pass
    for bi, nxt, wl_next in pending:
        for nm, idx in pre_idx.items():
            pre_ref[idx, bi] = nxt[nm]
        wl_ref[bi] = wl_next


def _rwkv_mix(p_a, prm, heads, nb=2):
    b, t, cin = p_a.shape
    aw = heads * RWKV_HEAD
    ln = RWKV_CHUNK
    nc = t // ln
    sub = ln // SUBLANES
    full = lambda arr: pl.BlockSpec(arr.shape, lambda i, j: (0,) * arr.ndim)
    in_specs = [pl.BlockSpec((nb, ln, cin), lambda i, j: (i, jnp.minimum(j, nc - 1), 0)),
                pl.BlockSpec((nb, SUBLANES, cin),
                             lambda i, j: (i, jnp.maximum(jnp.minimum(j, nc - 1) * sub - 1, 0), 0))]
    in_specs += [full(x) for x in prm]
    return pl.pallas_call(
        functools.partial(_rwkv_kernel, heads, nb),
        grid=(b // nb, nc + 1), in_specs=in_specs,
        out_specs=pl.BlockSpec((nb, ln, aw), lambda i, j: (i, jnp.maximum(j - 1, 0), 0)),
        out_shape=jax.ShapeDtypeStruct((b, t, aw), F32),
        scratch_shapes=[pltpu.VMEM((nb, heads // 2, RWKV_HEAD, 2 * RWKV_HEAD), F32),
                        pltpu.VMEM((len(PRE_NAMES), nb, ln, aw), F32),
                        pltpu.VMEM((nb, SUBLANES, aw), F32)],
        compiler_params=_cparams(("parallel", "arbitrary")), name="rwkv7",
    )(p_a, p_a, *prm)


def _gmlp_kernel(p_ref, lnw_ref, lnb_ref, ws_ref, bs_ref, o_ref):
    x = _gelu_tanh(p_ref[0])
    ln = GMLP_CHUNK
    bw = x.shape[1] // 2
    gd = bw // GMLP_GROUPS
    tril = _iota2((ln, ln), 0) >= _iota2((ln, ln), 1)
    for gi in range(GMLP_GROUPS):
        u = x[:, gi * gd:(gi + 1) * gd]
        v = x[:, bw + gi * gd:bw + (gi + 1) * gd]
        mean = jnp.mean(v, axis=-1, keepdims=True)
        d = v - mean
        var = jnp.mean(d * d, axis=-1, keepdims=True)
        vn = d * lax.rsqrt(var + EPS) * lnw_ref[gi:gi + 1, :] + lnb_ref[gi:gi + 1, :]
        s = _bdot(jnp.where(tril, ws_ref[gi], 0.0), vn) + bs_ref[gi]
        o_ref[0, :, gi * gd:(gi + 1) * gd] = u * s


def _gmlp_mix(p_b, ln_w, ln_b, ws, bs):
    b, t, cin = p_b.shape
    bw = cin // 2
    gd = bw // GMLP_GROUPS
    ln = GMLP_CHUNK
    bs_b = jnp.broadcast_to(bs[:, :, None], (GMLP_GROUPS, ln, gd))
    full = lambda arr: pl.BlockSpec(arr.shape, lambda i, j: (0,) * arr.ndim)
    return pl.pallas_call(
        _gmlp_kernel, grid=(b, t // ln),
        in_specs=[pl.BlockSpec((1, ln, cin), lambda i, j: (i, j, 0)),
                  full(ln_w), full(ln_b), full(ws), full(bs_b)],
        out_specs=pl.BlockSpec((1, ln, bw), lambda i, j: (i, j, 0)),
        out_shape=jax.ShapeDtypeStruct((b, t, bw), F32),
        compiler_params=_cparams(("parallel", "parallel")), name="gmlp",
    )(p_b, ln_w, ln_b, ws, bs_b)


def _mix_out0_kernel(x_ref, ya_ref, yb_ref, wa_ref, wb_ref, g_ref, o_ref):
    y = _bdot(ya_ref[...], wa_ref[...]) + _bdot(yb_ref[...], wb_ref[...])
    o_ref[...] = x_ref[...] + _rms(y, g_ref[...])


def _mix_out0(x2, ya, yb, wa, wb, g, tm=512):
    n, d = x2.shape
    row = lambda arr: pl.BlockSpec((tm, arr.shape[1]), lambda i: (i, 0))
    full = lambda arr: pl.BlockSpec(arr.shape, lambda i: (0,) * arr.ndim)
    return pl.pallas_call(
        _mix_out0_kernel, grid=(n // tm,),
        in_specs=[row(x2), row(ya), row(yb), full(wa), full(wb), full(g)],
        out_specs=row(x2), out_shape=jax.ShapeDtypeStruct((n, d), F32),
        compiler_params=_cparams(("parallel",)), name="mix_out0",
    )(x2, ya, yb, wa, wb, g)


def _ffn_kernel(fc, x_ref, gpre_ref, wg_ref, wu_ref, wd_ref, gpost_ref, o_ref):
    h = _rms(x_ref[...], gpre_ref[...]).astype(BF16)
    acc = None
    for c in range(wg_ref.shape[1] // fc):
        cols = slice(c * fc, (c + 1) * fc)
        gate = jnp.dot(h, wg_ref[:, cols], preferred_element_type=F32)
        up = jnp.dot(h, wu_ref[:, cols], preferred_element_type=F32)
        part = jnp.dot((_silu(gate) * up).astype(BF16), wd_ref[cols, :], preferred_element_type=F32)
        acc = part if acc is None else acc + part
    o_ref[...] = x_ref[...] + _rms(acc, gpost_ref[...])


def _ffn(x2, gpre, wg, wu, wd, gpost, tm=512, fc=MXU_TILE):
    n, d = x2.shape
    full = lambda arr: pl.BlockSpec(arr.shape, lambda i: (0,) * arr.ndim)
    return pl.pallas_call(
        functools.partial(_ffn_kernel, fc), grid=(n // tm,),
        in_specs=[pl.BlockSpec((tm, d), lambda i: (i, 0)), full(gpre), full(wg), full(wu), full(wd),
                  full(gpost)],
        out_specs=pl.BlockSpec((tm, d), lambda i: (i, 0)),
        out_shape=jax.ShapeDtypeStruct((n, d), F32),
        compiler_params=_cparams(("parallel",)), name="ffn",
    )(x2, gpre, wg, wu, wd, gpost)


def _s5_kernel(nc, nb, u_ref, tap_ref, wsr_ref, wsi_ref, wcr_ref, wci_ref, alr_ref, ali_ref, d_ref, o_ref,
               toep_ref):
    u = u_ref[0]
    taps = tap_ref[0]
    ch = taps.shape[0]
    lane = _iota2(taps.shape, 1)
    for s in range(taps.shape[1] // ch):
        blk = taps if s == 0 else jnp.where(lane >= ch * s, pltpu.roll(taps, ch * s, axis=1), 0.0)
        toep_ref[ch * s:ch * (s + 1), :] = blk.astype(BF16)
    y = jnp.dot(u, toep_ref[...], preferred_element_type=F32)
    xer = jnp.dot(u, wsr_ref[0], preferred_element_type=F32)
    xei = jnp.dot(u, wsi_ref[0], preferred_element_type=F32)
    alr = alr_ref[0]
    ali = ali_ref[0]
    cr = jnp.zeros((nb, xer.shape[1]), F32)
    ci = jnp.zeros((nb, xer.shape[1]), F32)
    prs, pis = [], []
    for c in range(nc):
        prs.append(cr)
        pis.append(ci)
        er = xer[c * nb:(c + 1) * nb]
        ei = xei[c * nb:(c + 1) * nb]
        cr, ci = alr * cr - ali * ci + er, alr * ci + ali * cr + ei
    pr = jnp.concatenate(prs, axis=0)
    pi = jnp.concatenate(pis, axis=0)
    y = y + _bdot(pr, wcr_ref[0]) + _bdot(pi, wci_ref[0])
    o_ref[0] = y + d_ref[0] * u.astype(F32)


def _s5_weights(a_re, a_im, log_dt, b_re, b_im, c_re, c_im, d_skip, ln):
    g, st = a_re.shape
    ch = b_re.shape[2]
    dt = jnp.exp(log_dt)[:, None]
    lr, li = a_re, a_im
    tau = jnp.arange(ln + 1, dtype=F32)[:, None, None]
    mag = jnp.exp(lr[None] * dt[None] * tau)
    pw_r = mag * jnp.cos(li[None] * dt[None] * tau)
    pw_i = mag * jnp.sin(li[None] * dt[None] * tau)
    ab_r, ab_i = pw_r[1], pw_i[1]
    nr, ni = ab_r - 1.0, ab_i
    den = lr * lr + li * li
    fr, fi = (nr * lr + ni * li) / den, (ni * lr - nr * li) / den
    bb_r = fr[..., None] * b_re - fi[..., None] * b_im
    bb_i = fr[..., None] * b_im + fi[..., None] * b_re
    cp_r = c_re[None] * pw_r[:ln, :, None, :] - c_im[None] * pw_i[:ln, :, None, :]
    cp_i = c_re[None] * pw_i[:ln, :, None, :] + c_im[None] * pw_r[:ln, :, None, :]
    hp = lax.Precision.HIGHEST
    taps = (jnp.einsum('tgcp,gpd->gdtc', cp_r, bb_r, precision=hp)
            - jnp.einsum('tgcp,gpd->gdtc', cp_i, bb_i, precision=hp))
    taps = taps.reshape(g, ch, ln * ch)
    rev_r, rev_i = pw_r[:ln][::-1], pw_i[:ln][::-1]
    ws_r = rev_r[..., None] * bb_r[None] - rev_i[..., None] * bb_i[None]
    ws_i = rev_r[..., None] * bb_i[None] + rev_i[..., None] * bb_r[None]
    ws_r = ws_r.transpose(1, 0, 3, 2).reshape(g, ln * ch, st)
    ws_i = ws_i.transpose(1, 0, 3, 2).reshape(g, ln * ch, st)
    q_r, q_i = pw_r[1:ln + 1], pw_i[1:ln + 1]
    wc_r = c_re[None] * q_r[:, :, None, :] - c_im[None] * q_i[:, :, None, :]
    wc_i = -(c_re[None] * q_i[:, :, None, :] + c_im[None] * q_r[:, :, None, :])
    wc_r = wc_r.transpose(1, 3, 0, 2).reshape(g, st, ln * ch)
    wc_i = wc_i.transpose(1, 3, 0, 2).reshape(g, st, ln * ch)
    al_r = pw_r[ln].reshape(g, 1, st)
    al_i = pw_i[ln].reshape(g, 1, st)
    d_t = jnp.tile(d_skip.reshape(g, 1, ch), (1, ln, 1)).reshape(g, 1, ln * ch)
    return (taps, ws_r.astype(BF16), ws_i.astype(BF16), wc_r.astype(BF16),
            wc_i.astype(BF16), al_r, al_i, d_t)


def _s5_core(u, weights):
    b, t, cw = u.shape
    ln, ch = S5_CHUNK, S5_GROUP_CH
    g = cw // ch
    nc = t // ln
    ug = u.astype(BF16).reshape(b, nc, ln, g, ch).transpose(3, 1, 0, 2, 4).reshape(g, nc * b, ln * ch)
    per_g = lambda arr: pl.BlockSpec((1,) + arr.shape[1:], lambda i: (i, 0, 0))
    yg = pl.pallas_call(
        functools.partial(_s5_kernel, nc, b), grid=(g,),
        in_specs=[per_g(ug)] + [per_g(w) for w in weights],
        out_specs=per_g(ug), out_shape=jax.ShapeDtypeStruct(ug.shape, F32),
        scratch_shapes=[pltpu.VMEM((ln * ch, ln * ch), BF16)],
        compiler_params=_cparams(("parallel",)), name="s5",
    )(ug, *weights)
    return yg.reshape(g, nc, b, ln, ch).transpose(2, 1, 3, 0, 4).reshape(b, t, cw)


def _ssd_kernel(z_ref, xbc_ref, xp_ref, dt_ref, cw_ref, cb_ref, dtb_ref, alog_ref, dsk_ref, nw_ref,
                o_ref, s_ref):
    c = pl.program_id(1)
    ln = SSD_CHUNK
    hd = SSD_HEAD
    dw = SSD_HEADS * hd
    gn = SSD_STATE

    @pl.when(c == 0)
    def _():
        s_ref[...] = jnp.zeros_like(s_ref)

    xbc = xbc_ref[0]
    prev = jnp.where(c == 0, 0.0, xp_ref[0])
    full = jnp.concatenate([prev, xbc], axis=0)
    conv = cb_ref[...]
    for j in range(SSD_CONV):
        off = SUBLANES - (SSD_CONV - 1) + j
        conv = conv + cw_ref[j:j + 1, :] * full[off:off + ln]
    act = _silu(conv)
    xh = act[:, :dw]
    dt = _softplus(dt_ref[0] + dtb_ref[...])
    adt = -jnp.exp(alog_ref[...]) * dt
    tril = _iota2((ln, ln), 0) >= _iota2((ln, ln), 1)
    acs = _dot_exact_x(jnp.where(tril, 1.0, 0.0), adt)
    acs_t = acs.T
    tot = acs[ln - 1:ln, :]
    hg = SSD_HEADS // SSD_GROUPS
    s_all = s_ref[...]
    y_heads, s_heads = [], []
    for gi in range(SSD_GROUPS):
        bm = act[:, dw + gi * gn:dw + (gi + 1) * gn]
        cm = act[:, dw + SSD_GROUPS * gn + gi * gn:dw + SSD_GROUPS * gn + (gi + 1) * gn]
        cb = _bdot_nt(cm, bm)
        for hh in range(hg):
            h = gi * hg + hh
            sl = slice(h * hd, (h + 1) * hd)
            col = acs[:, h:h + 1]
            rowv = acs_t[h:h + 1, :]
            lmat = jnp.exp(jnp.where(tril, col - rowv, -jnp.inf))
            xh_h = xh[:, sl]
            xdt = xh_h * dt[:, h:h + 1]
            tot_h = tot[:, h:h + 1]
            st = s_all[h]
            y_h = _bdot(cb * lmat, xdt) + jnp.exp(col) * _bdot(cm, st)
            s_heads.append(jnp.exp(tot_h) * st + _bdot_tn(bm * jnp.exp(tot_h - col), xdt))
            y_heads.append(y_h + dsk_ref[:, sl] * xh_h)
    s_ref[...] = jnp.stack(s_heads, axis=0)
    y = jnp.concatenate(y_heads, axis=1) * _silu(z_ref[0])
    gw = dw // SSD_GROUPS
    for gi in range(SSD_GROUPS):
        yg = y[:, gi * gw:(gi + 1) * gw]
        yg = yg * lax.rsqrt(jnp.mean(yg * yg, axis=-1, keepdims=True) + EPS)
        o_ref[0, :, gi * gw:(gi + 1) * gw] = yg * nw_ref[:, gi * gw:(gi + 1) * gw]


def _ssd_mix(z, xbc, dtp, conv_w, conv_b, dt_bias, a_log, d_skip, norm_w):
    b, t, dw = z.shape
    ln = SSD_CHUNK
    xw = xbc.shape[2]
    pad = lambda vec: jnp.pad(vec, (0, LANES - vec.shape[0])).reshape(1, LANES)
    dsk = jnp.repeat(d_skip, SSD_HEAD).reshape(1, dw)
    prm = [conv_w, conv_b.reshape(1, xw), pad(dt_bias), pad(a_log), dsk, norm_w.reshape(1, dw)]
    full = lambda arr: pl.BlockSpec(arr.shape, lambda i, j: (0,) * arr.ndim)
    blk = lambda w: pl.BlockSpec((1, ln, w), lambda i, j: (i, j, 0))
    return pl.pallas_call(
        _ssd_kernel, grid=(b, t // ln),
        in_specs=[blk(dw), blk(xw),
                  pl.BlockSpec((1, SUBLANES, xw),
                               lambda i, j: (i, jnp.maximum(j * (ln // SUBLANES) - 1, 0), 0)),
                  blk(LANES)] + [full(x) for x in prm],
        out_specs=blk(dw), out_shape=jax.ShapeDtypeStruct((b, t, dw), F32),
        scratch_shapes=[pltpu.VMEM((SSD_HEADS, SSD_STATE, SSD_HEAD), F32)],
        compiler_params=_cparams(("parallel", "arbitrary")), name="ssd",
    )(z, xbc, xbc, dtp, *prm)


def _mix_out1_kernel(x_ref, yc_ref, yd_ref, gw_ref, gb_ref, wc_ref, wd_ref, gpost_ref, gpre_ref,
                     wrh_ref, wrl_ref, x1_ref, h_ref, idx_ref, gate_ref):
    yc = _gelu_tanh(yc_ref[...])
    yc = yc * _sigmoid(_bdot(yc, gw_ref[...]) + gb_ref[...])
    y = _bdot(yc, wc_ref[...]) + _bdot(yd_ref[...], wd_ref[...])
    x1 = x_ref[...] + _rms(y, gpost_ref[...])
    x1_ref[...] = x1
    h = _rms(x1, gpre_ref[...])
    h_ref[...] = h
    hh = h.astype(BF16)
    hl = (h - hh.astype(F32)).astype(BF16)
    wrh = wrh_ref[...]
    logits = (jnp.dot(hh, wrh, preferred_element_type=F32) + jnp.dot(hl, wrh, preferred_element_type=F32)
              + jnp.dot(hh, wrl_ref[...], preferred_element_type=F32))
    lane = _iota2(logits.shape, 1)
    lane_f = lane.astype(F32)
    logits = jnp.where(lane < MOE_EXPERTS, logits, -jnp.inf)
    m1 = jnp.max(logits, axis=-1, keepdims=True)
    i1 = jnp.min(jnp.where(logits == m1, lane_f, float(LANES)), axis=-1, keepdims=True)
    rest = jnp.where(lane_f == i1, -jnp.inf, logits)
    m2 = jnp.max(rest, axis=-1, keepdims=True)
    i2 = jnp.min(jnp.where(rest == m2, lane_f, float(LANES)), axis=-1, keepdims=True)
    e2 = jnp.exp(m2 - m1)
    g1 = 1.0 / (1.0 + e2)
    g2 = e2 / (1.0 + e2)
    idx_ref[...] = jnp.where(lane == 0, i1, jnp.where(lane == 1, i2, 0.0)).astype(jnp.int32)
    gate_ref[...] = jnp.where(lane == 0, g1, jnp.where(lane == 1, g2, 0.0))


def _mix_out1(x2, yc, yd, glu_w, glu_b, wc, wd, gpost, gpre, wr, tm=512):
    n, d = x2.shape
    wr_p = jnp.pad(wr, ((0, 0), (0, LANES - wr.shape[1])))
    wrh = wr_p.astype(BF16)
    wrl = (wr_p - wrh.astype(F32)).astype(BF16)
    row = lambda w: pl.BlockSpec((tm, w), lambda i: (i, 0))
    full = lambda arr: pl.BlockSpec(arr.shape, lambda i: (0,) * arr.ndim)
    prm = [glu_w, glu_b, wc, wd, gpost, gpre, wrh, wrl]
    return pl.pallas_call(
        _mix_out1_kernel, grid=(n // tm,),
        in_specs=[row(d), row(yc.shape[1]), row(yd.shape[1])] + [full(p) for p in prm],
        out_specs=[row(d), row(d), row(LANES), row(LANES)],
        out_shape=[jax.ShapeDtypeStruct((n, d), F32), jax.ShapeDtypeStruct((n, d), F32),
                   jax.ShapeDtypeStruct((n, LANES), jnp.int32), jax.ShapeDtypeStruct((n, LANES), F32)],
        compiler_params=_cparams(("parallel",)), name="mix_out1",
    )(x2, yc, yd, *prm)


GATHER_UNROLL = 8


def _gather_rows(n_rows, make_copy):
    def body(j, carry):
        for q in range(GATHER_UNROLL):
            make_copy(j * GATHER_UNROLL + q).start(priority=q % 2)
        return carry

    lax.fori_loop(0, n_rows // GATHER_UNROLL, body, 0)


def _moe_kernel(nf, be_ref, tok_ref, dst_ref, nact_ref, h_hbm, wg_ref, wu_ref, wd_ref, y_hbm, buf_ref,
                xb_ref, gsem, ssem):
    i = pl.program_id(0)
    f = pl.program_id(1)
    n_blocks = pl.num_programs(0)
    tm = buf_ref.shape[1]
    nact = nact_ref[0]
    active = i < nact
    slot = lax.rem(i, 2)
    other = 1 - slot
    xs = lambda sl: buf_ref.at[sl]
    yb = lambda sl: buf_ref.at[2 + sl]

    def gather_copy(block, sl, r):
        tok = tok_ref[block * tm + r]
        return pltpu.make_async_copy(h_hbm.at[pl.ds(tok, 1)], buf_ref.at[sl, pl.ds(r, 1)], gsem.at[sl])

    def scatter_copy(block, sl, r):
        dst = dst_ref[block * tm + r]
        return pltpu.make_async_copy(buf_ref.at[2 + sl, pl.ds(r, 1)], y_hbm.at[pl.ds(dst, 1)], ssem.at[sl])

    def wait_rows(sem_slot_ref, buf):
        pltpu.make_async_copy(h_hbm.at[pl.ds(0, tm)], buf, sem_slot_ref).wait()

    @pl.when(f == 0)
    def _():
        @pl.when(i == 0)
        def _():
            _gather_rows(tm, functools.partial(gather_copy, 0, 0))
            buf_ref[3] = jnp.zeros(buf_ref.shape[1:], F32)
            n_real = y_hbm.shape[0] - 2 * tm
            for half in range(2):
                init = pltpu.make_async_copy(yb(1), y_hbm.at[pl.ds(n_real + half * tm, tm)], ssem.at[0])
                init.start()
                init.wait()

        @pl.when(i <= nact)
        def _():
            wait_rows(gsem.at[slot], xs(slot))

        @pl.when(jnp.logical_and(i >= 1, i <= nact))
        def _():
            wait_rows(ssem.at[slot], yb(slot))

        @pl.when(i == nact)
        def _():
            _gather_rows(tm, functools.partial(scatter_copy, i - 1, other))
            wait_rows(ssem.at[other], yb(other))

        @pl.when(active)
        def _():
            xb_ref[...] = buf_ref[slot].astype(BF16)
            buf_ref[2 + slot] = jnp.zeros(buf_ref.shape[1:], F32)

    @pl.when(active)
    def _():
        x = xb_ref[...]
        nch = wg_ref.shape[2] // MXU_TILE
        rows_f = tm // nf
        base = pl.multiple_of(f * rows_f, SUBLANES)
        prev = jnp.where(i == 0, n_blocks - 1, i - 1)
        for c in range(nch):
            for r in range(rows_f * c // nch, rows_f * (c + 1) // nch):
                gather_copy(i + 1, other, base + r).start(priority=1)
                scatter_copy(prev, other, base + r).start(priority=1)
            cols = slice(c * MXU_TILE, (c + 1) * MXU_TILE)
            gate = jnp.dot(x, wg_ref[0, :, cols], preferred_element_type=F32)
            up = jnp.dot(x, wu_ref[0, :, cols], preferred_element_type=F32)
            buf_ref[2 + slot] += jnp.dot((_silu(gate) * up).astype(BF16), wd_ref[0, cols, :],
                                         preferred_element_type=F32)


def _moe_experts(h, block_expert, slot_tok, slot_dst, nact, n_rows, wg, wu, wd, tf=7 * MXU_TILE):
    n, d = h.shape
    tm = MOE_ROWS
    n_blocks = slot_tok.shape[0] // tm
    ff = wg.shape[2]
    nf = ff // tf
    wmap = lambda i, j, be, tok, dst, na: (be[i], 0, j)
    grid_spec = pltpu.PrefetchScalarGridSpec(
        num_scalar_prefetch=4, grid=(n_blocks, nf),
        in_specs=[pl.BlockSpec(memory_space=pl.ANY),
                  pl.BlockSpec((1, d, tf), wmap), pl.BlockSpec((1, d, tf), wmap),
                  pl.BlockSpec((1, tf, d), lambda i, j, be, tok, dst, na: (be[i], j, 0))],
        out_specs=pl.BlockSpec(memory_space=pl.ANY),
        scratch_shapes=[pltpu.VMEM((4, tm, d), F32), pltpu.VMEM((tm, d), BF16),
                        pltpu.SemaphoreType.DMA((2,)), pltpu.SemaphoreType.DMA((2,))])
    return pl.pallas_call(
        functools.partial(_moe_kernel, nf), grid_spec=grid_spec,
        out_shape=jax.ShapeDtypeStruct((n_rows, d), F32),
        compiler_params=pltpu.CompilerParams(dimension_semantics=("arbitrary", "arbitrary"),
                                             vmem_limit_bytes=VMEM_LIMIT, disable_bounds_checks=True),
        name="moe_experts",
    )(block_expert, slot_tok, slot_dst, nact, h, wg, wu, wd)


def _combine_kernel(x_ref, y0_ref, y1_ref, gate_ref, gpost_ref, o_ref):
    gates = gate_ref[...]
    y = gates[:, 0:1] * y0_ref[...] + gates[:, 1:2] * y1_ref[...]
    o_ref[...] = x_ref[...] + _rms(y, gpost_ref[...])


def _moe_combine(x1, y, gates, gpost, tm=512):
    n, d = x1.shape
    nt = n // tm
    return pl.pallas_call(
        _combine_kernel, grid=(nt,),
        in_specs=[pl.BlockSpec((tm, d), lambda i: (i, 0)), pl.BlockSpec((tm, d), lambda i: (i, 0)),
                  pl.BlockSpec((tm, d), lambda i: (nt + i, 0)),
                  pl.BlockSpec((tm, LANES), lambda i: (i, 0)), pl.BlockSpec((1, d), lambda i: (0, 0))],
        out_specs=pl.BlockSpec((tm, d), lambda i: (i, 0)),
        out_shape=jax.ShapeDtypeStruct((n, d), F32),
        compiler_params=_cparams(("parallel",)), name="moe_combine",
    )(x1, y, y, gates, gpost)


def _moe_plan(idx, n):
    tm = MOE_ROWS
    flat_e = idx[:, :2].reshape(-1)
    onehot = (flat_e[:, None] == jnp.arange(MOE_EXPERTS, dtype=jnp.int32)[None, :]).astype(jnp.int32)
    csum = jnp.cumsum(onehot, axis=0)
    counts = csum[-1]
    rank = jnp.sum((csum - onehot) * onehot, axis=1)
    padded = (counts + tm - 1) // tm * tm
    pend = jnp.cumsum(padded)
    pstart = pend - padded
    dest = (jnp.sum(onehot * pstart[None, :], axis=1) + rank).astype(jnp.int32)
    n_blocks = (2 * n) // tm + MOE_EXPERTS + 1
    n_slots = n_blocks * tm
    slot_pair = jnp.full((n_slots,), -1, jnp.int32).at[dest].set(jnp.arange(2 * n, dtype=jnp.int32))
    real = slot_pair >= 0
    slot_tok = jnp.where(real, slot_pair // 2, 0)
    s_id = jnp.arange(n_slots, dtype=jnp.int32)
    slot_dst = jnp.where(real, slot_pair % 2 * n + slot_pair // 2, 2 * n + (s_id // tm) % 2 * tm + s_id % tm)
    block_start = jnp.arange(n_blocks, dtype=jnp.int32) * tm
    block_expert = jnp.minimum(jnp.searchsorted(pend, block_start, side='right'),
                               MOE_EXPERTS - 1).astype(jnp.int32)
    nact = (pend[-1] // tm).astype(jnp.int32).reshape(1)
    return block_expert, slot_tok, slot_dst, nact, 2 * n + 2 * tm


def kernel(x, l0_norm_pre_mix, l0_w_in, l0_rwkv_mu, l0_rwkv_w0, l0_rwkv_w2, l0_rwkv_a0, l0_rwkv_a2, l0_rwkv_g2, l0_rwkv_k_k, l0_rwkv_k_a, l0_rwkv_r_k, l0_rwkv_ln_w, l0_rwkv_ln_b, l0_gmlp_ln_w, l0_gmlp_ln_b, l0_gmlp_ws, l0_gmlp_bs, l0_w_out, l0_norm_post_mix, l0_norm_pre_ffn, l0_ffn_w_gate, l0_ffn_w_up, l0_ffn_w_down, l0_norm_post_ffn, l1_norm_pre_mix, l1_w_in, l1_s5_a_re, l1_s5_a_im, l1_s5_log_dt, l1_s5_b_re, l1_s5_b_im, l1_s5_c_re, l1_s5_c_im, l1_s5_d, l1_s5_glu_w, l1_s5_glu_b, l1_m2_conv_w, l1_m2_conv_b, l1_m2_dt_bias, l1_m2_a_log, l1_m2_d, l1_m2_norm_w, l1_w_out, l1_norm_post_mix, l1_norm_pre_ffn, l1_moe_router, l1_moe_w_gate, l1_moe_w_up, l1_moe_w_down, l1_norm_post_ffn):
    b, t, d = x.shape
    n = b * t
    x2 = x.reshape(n, d)
    row = lambda vec: vec.reshape(1, -1)

    aw = l0_rwkv_w0.shape[0]
    heads = aw // RWKV_HEAD
    lw_, la_, lg_ = l0_rwkv_w2.shape[0], l0_rwkv_a2.shape[0], l0_rwkv_g2.shape[0]
    a_in = 3 * aw + lw_ + la_ + lg_
    padc = lambda m, wdt: jnp.pad(m, ((0, 0), (0, LANES - wdt)))
    o = 3 * aw
    w_a = jnp.concatenate([l0_w_in[:, :o], padc(l0_w_in[:, o:o + lw_], lw_),
                           padc(l0_w_in[:, o + lw_:o + lw_ + la_], la_),
                           padc(l0_w_in[:, o + lw_ + la_:a_in], lg_)], axis=1).astype(BF16)
    w_b = l0_w_in[:, a_in:].astype(BF16)
    p_a, p_b = _norm_proj(x2, l0_norm_pre_mix, [w_a, w_b])
    padv = lambda vec, wdt: jnp.pad(vec, (0, LANES - wdt))
    mu = l0_rwkv_mu
    mu_p = jnp.concatenate([mu[:o], padv(mu[o:o + lw_], lw_), padv(mu[o + lw_:o + lw_ + la_], la_),
                            padv(mu[o + lw_ + la_:], lg_)])
    padr = lambda m: jnp.pad(m, ((0, LANES - m.shape[0]), (0, 0))).astype(BF16)
    hid = jnp.arange(LANES, dtype=jnp.int32) // RWKV_HEAD
    gsum = (hid[:, None] == hid[None, :]).astype(BF16)
    rwkv_prm = [row(mu_p), row(l0_rwkv_w0), padr(l0_rwkv_w2), row(l0_rwkv_a0), padr(l0_rwkv_a2),
                padr(l0_rwkv_g2), row(l0_rwkv_k_k), row(l0_rwkv_k_a), row(l0_rwkv_r_k),
                row(l0_rwkv_ln_w), row(l0_rwkv_ln_b), gsum]
    ya = _rwkv_mix(p_a.reshape(b, t, -1), rwkv_prm, heads)
    yb = _gmlp_mix(p_b.reshape(b, t, -1), l0_gmlp_ln_w, l0_gmlp_ln_b, l0_gmlp_ws, l0_gmlp_bs)
    wo = l0_w_out.astype(BF16)
    x2 = _mix_out0(x2, ya.reshape(n, -1), yb.reshape(n, -1), wo[:aw], wo[aw:], row(l0_norm_post_mix))
    x2 = _ffn(x2, row(l0_norm_pre_ffn), l0_ffn_w_gate.astype(BF16), l0_ffn_w_up.astype(BF16),
              l0_ffn_w_down.astype(BF16), row(l0_norm_post_ffn))

    cw = l1_s5_d.shape[0]
    dw = l1_m2_norm_w.shape[0]
    xw = l1_m2_conv_w.shape[1]
    nh = l1_m2_dt_bias.shape[0]
    w1 = l1_w_in
    w_parts = [w1[:, :cw], w1[:, cw:cw + dw], w1[:, cw + dw:cw + dw + xw],
               padc(w1[:, cw + dw + xw:], nh)]
    u_c, z_d, xbc, dtp = _norm_proj(x2, l1_norm_pre_mix, [w.astype(BF16) for w in w_parts])
    s5_w = _s5_weights(l1_s5_a_re, l1_s5_a_im, l1_s5_log_dt, l1_s5_b_re, l1_s5_b_im, l1_s5_c_re,
                       l1_s5_c_im, l1_s5_d, S5_CHUNK)
    yc = _s5_core(u_c.reshape(b, t, cw), s5_w)
    yd = _ssd_mix(z_d.reshape(b, t, dw), xbc.reshape(b, t, xw), dtp.reshape(b, t, LANES),
                  l1_m2_conv_w, l1_m2_conv_b, l1_m2_dt_bias, l1_m2_a_log, l1_m2_d, l1_m2_norm_w)
    wo1 = l1_w_out.astype(BF16)
    x1, h, idx, gates = _mix_out1(x2, yc.reshape(n, cw), yd.reshape(n, dw), l1_s5_glu_w.astype(BF16),
                                  row(l1_s5_glu_b), wo1[:cw], wo1[cw:], row(l1_norm_post_mix),
                                  row(l1_norm_pre_ffn), l1_moe_router)
    block_expert, slot_tok, slot_dst, nact, n_rows = _moe_plan(idx, n)
    ys = _moe_experts(h, block_expert, slot_tok, slot_dst, nact, n_rows, l1_moe_w_gate.astype(BF16),
                      l1_moe_w_up.astype(BF16), l1_moe_w_down.astype(BF16))
    out = _moe_combine(x1, ys, gates, row(l1_norm_post_ffn))
    return out.reshape(b, t, d)
```

```python
import functools

import jax
import jax.numpy as jnp
from jax import lax
from jax.experimental import pallas as pl
from jax.experimental.pallas import tpu as pltpu

F32 = jnp.float32
BF16 = jnp.bfloat16

EPS = 1e-6
RWKV_GN_EPS = 64e-5
RWKV_HEAD = 64
RWKV_CHUNK = 64
GMLP_CHUNK = 128
GMLP_GROUPS = 4
S5_GROUP_CH = 16
S5_STATE = 64
S5_CHUNK = 64
SSD_HEAD = 64
SSD_HEADS = 8
SSD_GROUPS = 2
SSD_STATE = 128
SSD_CONV = 4
SSD_CHUNK = 128
MOE_EXPERTS = 8
MOE_ROWS = 512
MXU_TILE = 256
LANES = 128
SUBLANES = 8
VMEM_LIMIT = 56 * 1024 * 1024


def _cparams(sem):
    return pltpu.CompilerParams(dimension_semantics=sem, vmem_limit_bytes=VMEM_LIMIT)


def _bdot(a, b):
    return jnp.dot(a.astype(BF16), b.astype(BF16), preferred_element_type=F32)


def _bdot_nt(a, b):
    return lax.dot_general(a.astype(BF16), b.astype(BF16), (((1,), (1,)), ((), ())),
                           preferred_element_type=F32)


def _bdot_tn(a, b):
    return lax.dot_general(a.astype(BF16), b.astype(BF16), (((0,), (0,)), ((), ())),
                           preferred_element_type=F32)


def _split3(x):
    h = x.astype(BF16)
    r1 = x - h.astype(F32)
    m = r1.astype(BF16)
    l = (r1 - m.astype(F32)).astype(BF16)
    return h, m, l


def _dot_x_exact(x, e):
    h, m, l = _split3(x)
    e = e.astype(BF16)
    return (jnp.dot(h, e, preferred_element_type=F32) + jnp.dot(m, e, preferred_element_type=F32)
            + jnp.dot(l, e, preferred_element_type=F32))


def _dot_exact_x(e, x):
    h, m, l = _split3(x)
    e = e.astype(BF16)
    return (jnp.dot(e, h, preferred_element_type=F32) + jnp.dot(e, m, preferred_element_type=F32)
            + jnp.dot(e, l, preferred_element_type=F32))


def _rms(x, g):
    return x * lax.rsqrt(jnp.mean(x * x, axis=-1, keepdims=True) + EPS) * g


def _sigmoid(x):
    return 1.0 / (1.0 + jnp.exp(-x))


def _silu(x):
    return x * _sigmoid(x)


def _softplus(x):
    return jnp.maximum(x, 0.0) + jnp.log(1.0 + jnp.exp(-jnp.abs(x)))


def _gelu_tanh(x):
    return 0.5 * x * (1.0 + jnp.tanh(0.7978845608028654 * (x + 0.044715 * x * x * x)))


def _iota2(shape, dim):
    return lax.broadcasted_iota(jnp.int32, shape, dim)


def _norm_proj_kernel(n_out, x_ref, g_ref, *refs):
    w_refs = refs[:n_out]
    o_refs = refs[n_out:]
    xn = _rms(x_ref[...], g_ref[...]).astype(BF16)
    for w_ref, o_ref in zip(w_refs, o_refs):
        o_ref[...] = jnp.dot(xn, w_ref[...], preferred_element_type=F32).astype(o_ref.dtype)


def _norm_proj(x2, g, ws, dtypes, tm=512):
    n, d = x2.shape
    in_specs = [pl.BlockSpec((tm, d), lambda i: (i, 0)), pl.BlockSpec((1, d), lambda i: (0, 0))]
    in_specs += [pl.BlockSpec(w.shape, lambda i: (0, 0)) for w in ws]
    out_specs = [pl.BlockSpec((tm, w.shape[1]), lambda i: (i, 0)) for w in ws]
    out_shape = [jax.ShapeDtypeStruct((n, w.shape[1]), dt) for w, dt in zip(ws, dtypes)]
    return pl.pallas_call(
        functools.partial(_norm_proj_kernel, len(ws)),
        grid=(n // tm,), in_specs=in_specs, out_specs=out_specs, out_shape=out_shape,
        compiler_params=_cparams(("parallel",)), name="norm_proj",
    )(x2, g.reshape(1, d), *ws)


PRE_NAMES = ("v", "g", "bonus", "rt", "kt", "bt", "at", "bh", "kh")


def _rwkv_kernel(heads, nb, p_ref, pp_ref, mu_ref, w0_ref, w2_ref, a0_ref, a2_ref, g2_ref, kk_ref,
                 ka_ref, rk_ref, lnw_ref, lnb_ref, gs_ref, o_ref, z_ref, pre_ref, wl_ref):
    c = pl.program_id(1)
    ln = RWKV_CHUNK
    hd = RWKV_HEAD
    aw = heads * hd

    @pl.when(c == 0)
    def _():
        z_ref[...] = jnp.zeros_like(z_ref)
        pre_ref[...] = jnp.zeros_like(pre_ref)
        wl_ref[...] = jnp.zeros_like(wl_ref)

    tril_f = jnp.where(_iota2((ln, ln), 0) >= _iota2((ln, ln), 1), 1.0, 0.0)
    rows = _iota2((ln, 1), 0)
    gs_tile = gs_ref[...]

    def gs(x):
        nt = x.shape[1] // LANES
        stacked = jnp.concatenate([x[:, j * LANES:(j + 1) * LANES] for j in range(nt)], axis=0)
        red = _dot_x_exact(stacked, gs_tile)
        return jnp.concatenate([red[j * ln:(j + 1) * ln] for j in range(nt)], axis=1)

    pre_idx = {nm: idx for idx, nm in enumerate(PRE_NAMES)}
    pending = []

    def prep_steps():
        for bi in range(nb):
            p = p_ref[bi]
            prev = jnp.where(c == 0, 0.0, pp_ref[bi][SUBLANES - 1:SUBLANES, :])
            ps = jnp.where(rows == 0, prev, pltpu.roll(p, 1, axis=0))
            pm = p + (ps - p) * mu_ref[...]
            r = pm[:, 0:aw]
            k = pm[:, aw:2 * aw]
            v = pm[:, 2 * aw:3 * aw]
            xw = pm[:, 3 * aw:3 * aw + LANES]
            xa = pm[:, 3 * aw + LANES:3 * aw + 2 * LANES]
            xg = pm[:, 3 * aw + 2 * LANES:3 * aw + 3 * LANES]
            yield
            w = w0_ref[...] + _bdot(jnp.tanh(xw), w2_ref[...])
            a = _sigmoid(a0_ref[...] + _bdot(xa, a2_ref[...]))
            g = _bdot(_sigmoid(xg), g2_ref[...])
            yield
            w = -_softplus(-w) - 0.5
            lw = -jnp.exp(w)
            kk = k * kk_ref[...]
            kk_ss = gs(kk * kk)
            yield
            cs = _dot_exact_x(tril_f, lw)
            kk = kk / jnp.maximum(jnp.sqrt(kk_ss), 1e-12)
            kmod = k * (1.0 + (a - 1.0) * ka_ref[...])
            yield
            bonus = gs(r * kmod * rk_ref[...])
            bvec = kk * a
            cs_last = cs[ln - 1:ln, :]
            encs = jnp.exp(-cs)
            yield
            dec_end = jnp.exp(cs_last - cs)
            nxt = dict(v=v, g=g, bonus=bonus, rt=r * jnp.exp(cs), kt=kmod * encs, bt=bvec * encs,
                       at=-kk * jnp.exp(cs - lw), bh=bvec * dec_end, kh=kmod * dec_end)
            pending.append((bi, nxt, jnp.broadcast_to(jnp.exp(cs_last), (SUBLANES, aw))))
            yield

    prep = prep_steps()
    tick = lambda: next(prep, None)

    lane = _iota2((ln, LANES), 1)
    lane_in = jnp.where(lane >= hd, lane - hd, lane)
    trow = _iota2((ln, LANES), 0)
    left = lane < hd
    tril_p = lane_in <= trow
    stril_p = lane_in < trow
    eye_p = lane_in == trow
    eye_pf = jnp.where(eye_p, 1.0, 0.0)

    def bd(x):
        xb = x.astype(BF16)
        zero = jnp.zeros_like(xb)
        return jnp.concatenate([jnp.where(left, xb, zero), jnp.where(left, zero, xb)], axis=0)

    def dot(a, b):
        return jnp.dot(a.astype(BF16), b, preferred_element_type=F32)

    npair = heads // 2
    pairs = [(bi, j) for bi in range(nb) for j in range(npair)]

    class _Tiles:
        def __init__(self, name):
            self.idx = pre_idx[name]

        def __getitem__(self, i):
            bi, j = pairs[i]
            return pre_ref[self.idx, bi, :, j * LANES:(j + 1) * LANES]

    at, rt, bt, kt, vv, bh, kh = (_Tiles(n) for n in ("at", "rt", "bt", "kt", "v", "bh", "kh"))
    wl = [wl_ref[bi, 0:1, j * LANES:(j + 1) * LANES] for bi, j in pairs]
    z_all = z_ref[...]
    zs = [z_all[bi, j] for bi, j in pairs]
    npr = range(len(pairs))
    lhs = [jnp.concatenate([at[i], rt[i]], axis=0).astype(BF16) for i in npr]
    ab = [lax.dot_general(lhs[i], bd(bt[i]), (((1,), (1,)), ((), ())), preferred_element_type=F32)
          for i in npr]
    ak = [lax.dot_general(lhs[i], bd(kt[i]), (((1,), (1,)), ((), ())), preferred_element_type=F32)
          for i in npr]
    tick()
    nmat = [jnp.where(stril_p, ab[i][:ln], 0.0) for i in npr]
    tinv = [eye_pf + nmat[i] for i in npr]
    npow = [dot(nmat[i], bd(nmat[i])) for i in npr]
    tick()
    for step in range(5):
        bdn = [bd(npow[i]) for i in npr]
        tinv = [tinv[i] + dot(tinv[i], bdn[i]) for i in npr]
        if step < 4:
            npow = [dot(npow[i], bdn[i]) for i in npr]
        tick()
    bdv = [bd(vv[i]) for i in npr]
    bdz = [bd(zs[i]) for i in npr]
    xmat = [dot(jnp.concatenate([jnp.where(stril_p, ak[i][:ln], 0.0), at[i]], axis=1),
                jnp.concatenate([bdv[i], bdz[i]], axis=0)) for i in npr]
    tick()
    u = [dot(tinv[i], bd(xmat[i])) for i in npr]
    tick()
    ys_p = [dot(jnp.concatenate([rt[i], jnp.where(tril_p, ab[i][ln:], 0.0),
                                 jnp.where(tril_p, ak[i][ln:], 0.0)], axis=1),
                jnp.concatenate([bdz[i], bd(u[i]), bdv[i]], axis=0)) for i in npr]
    tick()
    cross = [_bdot_tn(jnp.concatenate([bh[i], kh[i]], axis=0), jnp.concatenate([u[i], vv[i]], axis=0))
             for i in npr]
    tick()
    z_new = []
    for i in npr:
        dg = jnp.where(eye_p, wl[i], 0.0)
        wl_i = jnp.sum(jnp.where(left, dg, 0.0), axis=1, keepdims=True)
        wl_j = jnp.sum(jnp.where(left, 0.0, dg), axis=1, keepdims=True)
        z_new.append(jnp.where(left, wl_i, wl_j) * zs[i] + jnp.where(left, cross[i][:ln], cross[i][ln:]))
    z_ref[...] = jnp.stack(z_new, axis=0).reshape(z_ref.shape)

    tick()
    inv = 1.0 / hd
    for bi in range(nb):
        y = jnp.concatenate(ys_p[bi * npair:(bi + 1) * npair], axis=1)
        mean = gs(y) * inv
        d = y - mean
        var = gs(d * d) * inv
        yn = d * lax.rsqrt(var + RWKV_GN_EPS) * lnw_ref[...] + lnb_ref[...]
        o_ref[bi] = ((yn + pre_ref[pre_idx["bonus"], bi] * pre_ref[pre_idx["v"], bi])
                     * pre_ref[pre_idx["g"], bi]).astype(o_ref.dtype)
        tick()
    for _ in prep:
        pass
    for bi, nxt, wl_next in pending:
        for nm, idx in pre_idx.items():
            pre_ref[idx, bi] = nxt[nm]
        wl_ref[bi] = wl_next


def _rwkv_mix(p_a, prm, heads, nb=2):
    b, t, cin = p_a.shape
    aw = heads * RWKV_HEAD
    ln = RWKV_CHUNK
    nc = t // ln
    sub = ln // SUBLANES
    full = lambda arr: pl.BlockSpec(arr.shape, lambda i, j: (0,) * arr.ndim)
    in_specs = [pl.BlockSpec((nb, ln, cin), lambda i, j: (i, jnp.minimum(j, nc - 1), 0)),
                pl.BlockSpec((nb, SUBLANES, cin),
                             lambda i, j: (i, jnp.maximum(jnp.minimum(j, nc - 1) * sub - 1, 0), 0))]
    in_specs += [full(x) for x in prm]
    return pl.pallas_call(
        functools.partial(_rwkv_kernel, heads, nb),
        grid=(b // nb, nc + 1), in_specs=in_specs,
        out_specs=pl.BlockSpec((nb, ln, aw), lambda i, j: (i, jnp.maximum(j - 1, 0), 0)),
        out_shape=jax.ShapeDtypeStruct((b, t, aw), BF16),
        scratch_shapes=[pltpu.VMEM((nb, heads // 2, RWKV_HEAD, 2 * RWKV_HEAD), F32),
                        pltpu.VMEM((len(PRE_NAMES), nb, ln, aw), F32),
                        pltpu.VMEM((nb, SUBLANES, aw), F32)],
        compiler_params=_cparams(("parallel", "arbitrary")), name="rwkv7",
    )(p_a, p_a, *prm)


def _gmlp_kernel(p_ref, lnw_ref, lnb_ref, ws_ref, bs_ref, o_ref):
    ln = GMLP_CHUNK
    bw = p_ref.shape[2] // 2
    gd = bw // GMLP_GROUPS
    tril = _iota2((ln, ln), 0) >= _iota2((ln, ln), 1)
    ws_c = [jnp.where(tril, ws_ref[gi], 0.0).astype(BF16) for gi in range(GMLP_GROUPS)]
    for ci in range(p_ref.shape[1] // ln):
        x = _gelu_tanh(p_ref[0, ci * ln:(ci + 1) * ln, :].astype(F32))
        for gi in range(GMLP_GROUPS):
            u = x[:, gi * gd:(gi + 1) * gd]
            v = x[:, bw + gi * gd:bw + (gi + 1) * gd]
            mean = jnp.mean(v, axis=-1, keepdims=True)
            d = v - mean
            var = jnp.mean(d * d, axis=-1, keepdims=True)
            vn = d * lax.rsqrt(var + EPS) * lnw_ref[gi:gi + 1, :] + lnb_ref[gi:gi + 1, :]
            s = jnp.dot(ws_c[gi], vn.astype(BF16), preferred_element_type=F32) + bs_ref[gi]
            o_ref[0, ci * ln:(ci + 1) * ln, gi * gd:(gi + 1) * gd] = (u * s).astype(o_ref.dtype)


def _gmlp_mix(p_b, ln_w, ln_b, ws, bs, rows=512):
    b, t, cin = p_b.shape
    bw = cin // 2
    gd = bw // GMLP_GROUPS
    bs_b = jnp.broadcast_to(bs[:, :, None], (GMLP_GROUPS, GMLP_CHUNK, gd))
    full = lambda arr: pl.BlockSpec(arr.shape, lambda i, j: (0,) * arr.ndim)
    return pl.pallas_call(
        _gmlp_kernel, grid=(b, t // rows),
        in_specs=[pl.BlockSpec((1, rows, cin), lambda i, j: (i, j, 0)),
                  full(ln_w), full(ln_b), full(ws), full(bs_b)],
        out_specs=pl.BlockSpec((1, rows, bw), lambda i, j: (i, j, 0)),
        out_shape=jax.ShapeDtypeStruct((b, t, bw), BF16),
        compiler_params=_cparams(("parallel", "parallel")), name="gmlp",
    )(p_b, ln_w, ln_b, ws, bs_b)


def _mix_out0_kernel(x_ref, ya_ref, yb_ref, wa_ref, wb_ref, g_ref, o_ref):
    y = _bdot(ya_ref[...], wa_ref[...]) + _bdot(yb_ref[...], wb_ref[...])
    o_ref[...] = x_ref[...] + _rms(y, g_ref[...])


def _mix_out0(x2, ya, yb, wa, wb, g, tm=512):
    n, d = x2.shape
    row = lambda arr: pl.BlockSpec((tm, arr.shape[1]), lambda i: (i, 0))
    full = lambda arr: pl.BlockSpec(arr.shape, lambda i: (0,) * arr.ndim)
    return pl.pallas_call(
        _mix_out0_kernel, grid=(n // tm,),
        in_specs=[row(x2), row(ya), row(yb), full(wa), full(wb), full(g)],
        out_specs=row(x2), out_shape=jax.ShapeDtypeStruct((n, d), F32),
        compiler_params=_cparams(("parallel",)), name="mix_out0",
    )(x2, ya, yb, wa, wb, g)


def _ffn_kernel(fc, x_ref, gpre_ref, wg_ref, wu_ref, wd_ref, gpost_ref, o_ref):
    h = _rms(x_ref[...], gpre_ref[...]).astype(BF16)
    acc = None
    for c in range(wg_ref.shape[1] // fc):
        cols = slice(c * fc, (c + 1) * fc)
        gate = jnp.dot(h, wg_ref[:, cols], preferred_element_type=F32)
        up = jnp.dot(h, wu_ref[:, cols], preferred_element_type=F32)
        part = jnp.dot((_silu(gate) * up).astype(BF16), wd_ref[cols, :], preferred_element_type=F32)
        acc = part if acc is None else acc + part
    o_ref[...] = x_ref[...] + _rms(acc, gpost_ref[...])


def _ffn(x2, gpre, wg, wu, wd, gpost, tm=512, fc=MXU_TILE):
    n, d = x2.shape
    full = lambda arr: pl.BlockSpec(arr.shape, lambda i: (0,) * arr.ndim)
    return pl.pallas_call(
        functools.partial(_ffn_kernel, fc), grid=(n // tm,),
        in_specs=[pl.BlockSpec((tm, d), lambda i: (i, 0)), full(gpre), full(wg), full(wu), full(wd),
                  full(gpost)],
        out_specs=pl.BlockSpec((tm, d), lambda i: (i, 0)),
        out_shape=jax.ShapeDtypeStruct((n, d), F32),
        compiler_params=_cparams(("parallel",)), name="ffn",
    )(x2, gpre, wg, wu, wd, gpost)


def _s5_kernel(nc, nb, u_ref, tap_ref, wsr_ref, wsi_ref, wcr_ref, wci_ref, alr_ref, ali_ref, d_ref, o_ref,
               toep_ref):
    u = u_ref[0]
    taps = tap_ref[0]
    ch = taps.shape[0]
    lane = _iota2(taps.shape, 1)
    for s in range(taps.shape[1] // ch):
        blk = taps if s == 0 else jnp.where(lane >= ch * s, pltpu.roll(taps, ch * s, axis=1), 0.0)
        toep_ref[ch * s:ch * (s + 1), :] = blk.astype(BF16)
    y = jnp.dot(u, toep_ref[...], preferred_element_type=F32)
    xer = jnp.dot(u, wsr_ref[0], preferred_element_type=F32)
    xei = jnp.dot(u, wsi_ref[0], preferred_element_type=F32)
    alr = alr_ref[0]
    ali = ali_ref[0]
    cr = jnp.zeros((nb, xer.shape[1]), F32)
    ci = jnp.zeros((nb, xer.shape[1]), F32)
    prs, pis = [], []
    for c in range(nc):
        prs.append(cr)
        pis.append(ci)
        er = xer[c * nb:(c + 1) * nb]
        ei = xei[c * nb:(c + 1) * nb]
        cr, ci = alr * cr - ali * ci + er, alr * ci + ali * cr + ei
    pr = jnp.concatenate(prs, axis=0)
    pi = jnp.concatenate(pis, axis=0)
    y = y + _bdot(pr, wcr_ref[0]) + _bdot(pi, wci_ref[0])
    o_ref[0] = (y + d_ref[0] * u.astype(F32)).astype(o_ref.dtype)


def _s5_weights(a_re, a_im, log_dt, b_re, b_im, c_re, c_im, d_skip, ln):
    g, st = a_re.shape
    ch = b_re.shape[2]
    dt = jnp.exp(log_dt)[:, None]
    lr, li = a_re, a_im
    tau = jnp.arange(ln + 1, dtype=F32)[:, None, None]
    mag = jnp.exp(lr[None] * dt[None] * tau)
    pw_r = mag * jnp.cos(li[None] * dt[None] * tau)
    pw_i = mag * jnp.sin(li[None] * dt[None] * tau)
    ab_r, ab_i = pw_r[1], pw_i[1]
    nr, ni = ab_r - 1.0, ab_i
    den = lr * lr + li * li
    fr, fi = (nr * lr + ni * li) / den, (ni * lr - nr * li) / den
    bb_r = fr[..., None] * b_re - fi[..., None] * b_im
    bb_i = fr[..., None] * b_im + fi[..., None] * b_re
    cp_r = c_re[None] * pw_r[:ln, :, None, :] - c_im[None] * pw_i[:ln, :, None, :]
    cp_i = c_re[None] * pw_i[:ln, :, None, :] + c_im[None] * pw_r[:ln, :, None, :]
    hp = lax.Precision.HIGHEST
    taps = (jnp.einsum('tgcp,gpd->gdtc', cp_r, bb_r, precision=hp)
            - jnp.einsum('tgcp,gpd->gdtc', cp_i, bb_i, precision=hp))
    taps = taps.reshape(g, ch, ln * ch)
    rev_r, rev_i = pw_r[:ln][::-1], pw_i[:ln][::-1]
    ws_r = rev_r[..., None] * bb_r[None] - rev_i[..., None] * bb_i[None]
    ws_i = rev_r[..., None] * bb_i[None] + rev_i[..., None] * bb_r[None]
    ws_r = ws_r.transpose(1, 0, 3, 2).reshape(g, ln * ch, st)
    ws_i = ws_i.transpose(1, 0, 3, 2).reshape(g, ln * ch, st)
    q_r, q_i = pw_r[1:ln + 1], pw_i[1:ln + 1]
    wc_r = c_re[None] * q_r[:, :, None, :] - c_im[None] * q_i[:, :, None, :]
    wc_i = -(c_re[None] * q_i[:, :, None, :] + c_im[None] * q_r[:, :, None, :])
    wc_r = wc_r.transpose(1, 3, 0, 2).reshape(g, st, ln * ch)
    wc_i = wc_i.transpose(1, 3, 0, 2).reshape(g, st, ln * ch)
    al_r = pw_r[ln].reshape(g, 1, st)
    al_i = pw_i[ln].reshape(g, 1, st)
    d_t = jnp.tile(d_skip.reshape(g, 1, ch), (1, ln, 1)).reshape(g, 1, ln * ch)
    return (taps, ws_r.astype(BF16), ws_i.astype(BF16), wc_r.astype(BF16),
            wc_i.astype(BF16), al_r, al_i, d_t)


def _s5_core(u, weights):
    b, t, cw = u.shape
    ln, ch = S5_CHUNK, S5_GROUP_CH
    g = cw // ch
    nc = t // ln
    ug = u.astype(BF16).reshape(b, nc, ln, g, ch).transpose(3, 1, 0, 2, 4).reshape(g, nc * b, ln * ch)
    per_g = lambda arr: pl.BlockSpec((1,) + arr.shape[1:], lambda i: (i, 0, 0))
    yg = pl.pallas_call(
        functools.partial(_s5_kernel, nc, b), grid=(g,),
        in_specs=[per_g(ug)] + [per_g(w) for w in weights],
        out_specs=per_g(ug), out_shape=jax.ShapeDtypeStruct(ug.shape, BF16),
        scratch_shapes=[pltpu.VMEM((ln * ch, ln * ch), BF16)],
        compiler_params=_cparams(("parallel",)), name="s5",
    )(ug, *weights)
    return yg.reshape(g, nc, b, ln, ch).transpose(2, 1, 3, 0, 4).reshape(b, t, cw)


def _ssd_kernel(z_ref, xbc_ref, xp_ref, dt_ref, cw_ref, cb_ref, dtb_ref, alog_ref, dsk_ref, nw_ref,
                o_ref, s_ref):
    c = pl.program_id(1)
    ln = SSD_CHUNK
    hd = SSD_HEAD
    dw = SSD_HEADS * hd
    gn = SSD_STATE

    @pl.when(c == 0)
    def _():
        s_ref[...] = jnp.zeros_like(s_ref)

    xbc = xbc_ref[0].astype(F32)
    prev = jnp.where(c == 0, 0.0, xp_ref[0].astype(F32))
    full = jnp.concatenate([prev, xbc], axis=0)
    conv = cb_ref[...]
    for j in range(SSD_CONV):
        off = xp_ref.shape[1] - (SSD_CONV - 1) + j
        conv = conv + cw_ref[j:j + 1, :] * full[off:off + ln]
    act = _silu(conv)
    xh = act[:, :dw]
    dt = _softplus(dt_ref[0] + dtb_ref[...])
    adt = -jnp.exp(alog_ref[...]) * dt
    tril = _iota2((ln, ln), 0) >= _iota2((ln, ln), 1)
    acs = _dot_exact_x(jnp.where(tril, 1.0, 0.0), adt)
    acs_t = acs.T
    tot = acs[ln - 1:ln, :]
    hg = SSD_HEADS // SSD_GROUPS
    s_all = s_ref[...]
    y_heads, s_heads = [], []
    for gi in range(SSD_GROUPS):
        bm = act[:, dw + gi * gn:dw + (gi + 1) * gn]
        cm = act[:, dw + SSD_GROUPS * gn + gi * gn:dw + SSD_GROUPS * gn + (gi + 1) * gn]
        cb = _bdot_nt(cm, bm)
        for hh in range(hg):
            h = gi * hg + hh
            sl = slice(h * hd, (h + 1) * hd)
            col = acs[:, h:h + 1]
            rowv = acs_t[h:h + 1, :]
            lmat = jnp.exp(jnp.where(tril, col - rowv, -jnp.inf))
            xh_h = xh[:, sl]
            xdt = xh_h * dt[:, h:h + 1]
            tot_h = tot[:, h:h + 1]
            st = s_all[h]
            y_h = _bdot(cb * lmat, xdt) + jnp.exp(col) * _bdot(cm, st)
            s_heads.append(jnp.exp(tot_h) * st + _bdot_tn(bm * jnp.exp(tot_h - col), xdt))
            y_heads.append(y_h + dsk_ref[:, sl] * xh_h)
    s_ref[...] = jnp.stack(s_heads, axis=0)
    y = jnp.concatenate(y_heads, axis=1) * _silu(z_ref[0].astype(F32))
    gw = dw // SSD_GROUPS
    for gi in range(SSD_GROUPS):
        yg = y[:, gi * gw:(gi + 1) * gw]
        yg = yg * lax.rsqrt(jnp.mean(yg * yg, axis=-1, keepdims=True) + EPS)
        o_ref[0, :, gi * gw:(gi + 1) * gw] = (yg * nw_ref[:, gi * gw:(gi + 1) * gw]).astype(o_ref.dtype)


def _ssd_mix(z, xbc, dtp, conv_w, conv_b, dt_bias, a_log, d_skip, norm_w):
    b, t, dw = z.shape
    ln = SSD_CHUNK
    xw = xbc.shape[2]
    pad = lambda vec: jnp.pad(vec, (0, LANES - vec.shape[0])).reshape(1, LANES)
    dsk = jnp.repeat(d_skip, SSD_HEAD).reshape(1, dw)
    prm = [conv_w, conv_b.reshape(1, xw), pad(dt_bias), pad(a_log), dsk, norm_w.reshape(1, dw)]
    full = lambda arr: pl.BlockSpec(arr.shape, lambda i, j: (0,) * arr.ndim)
    blk = lambda w: pl.BlockSpec((1, ln, w), lambda i, j: (i, j, 0))
    halo = 2 * SUBLANES
    return pl.pallas_call(
        _ssd_kernel, grid=(b, t // ln),
        in_specs=[blk(dw), blk(xw),
                  pl.BlockSpec((1, halo, xw), lambda i, j: (i, jnp.maximum(j * (ln // halo) - 1, 0), 0)),
                  blk(LANES)] + [full(x) for x in prm],
        out_specs=blk(dw), out_shape=jax.ShapeDtypeStruct((b, t, dw), BF16),
        scratch_shapes=[pltpu.VMEM((SSD_HEADS, SSD_STATE, SSD_HEAD), F32)],
        compiler_params=_cparams(("parallel", "arbitrary")), name="ssd",
    )(z, xbc, xbc, dtp, *prm)


def _mix_out1_kernel(x_ref, yc_ref, yd_ref, gw_ref, gb_ref, wc_ref, wd_ref, gpost_ref, gpre_ref,
                     wrh_ref, wrl_ref, x1_ref, h_ref, idx_ref, gate_ref):
    yc = _gelu_tanh(yc_ref[...].astype(F32))
    yc = yc * _sigmoid(_bdot(yc, gw_ref[...]) + gb_ref[...])
    y = _bdot(yc, wc_ref[...]) + _bdot(yd_ref[...], wd_ref[...])
    x1 = x_ref[...] + _rms(y, gpost_ref[...])
    x1_ref[...] = x1
    h = _rms(x1, gpre_ref[...])
    h_ref[...] = h
    hh = h.astype(BF16)
    hl = (h - hh.astype(F32)).astype(BF16)
    wrh = wrh_ref[...]
    logits = (jnp.dot(hh, wrh, preferred_element_type=F32) + jnp.dot(hl, wrh, preferred_element_type=F32)
              + jnp.dot(hh, wrl_ref[...], preferred_element_type=F32))
    lane = _iota2(logits.shape, 1)
    lane_f = lane.astype(F32)
    logits = jnp.where(lane < MOE_EXPERTS, logits, -jnp.inf)
    m1 = jnp.max(logits, axis=-1, keepdims=True)
    i1 = jnp.min(jnp.where(logits == m1, lane_f, float(LANES)), axis=-1, keepdims=True)
    rest = jnp.where(lane_f == i1, -jnp.inf, logits)
    m2 = jnp.max(rest, axis=-1, keepdims=True)
    i2 = jnp.min(jnp.where(rest == m2, lane_f, float(LANES)), axis=-1, keepdims=True)
    e2 = jnp.exp(m2 - m1)
    g1 = 1.0 / (1.0 + e2)
    g2 = e2 / (1.0 + e2)
    idx_ref[...] = jnp.where(lane == 0, i1, jnp.where(lane == 1, i2, 0.0)).astype(jnp.int32)
    gate_ref[...] = jnp.where(lane == 0, g1, jnp.where(lane == 1, g2, 0.0))


def _mix_out1(x2, yc, yd, glu_w, glu_b, wc, wd, gpost, gpre, wr, tm=512):
    n, d = x2.shape
    wr_p = jnp.pad(wr, ((0, 0), (0, LANES - wr.shape[1])))
    wrh = wr_p.astype(BF16)
    wrl = (wr_p - wrh.astype(F32)).astype(BF16)
    row = lambda w: pl.BlockSpec((tm, w), lambda i: (i, 0))
    full = lambda arr: pl.BlockSpec(arr.shape, lambda i: (0,) * arr.ndim)
    prm = [glu_w, glu_b, wc, wd, gpost, gpre, wrh, wrl]
    return pl.pallas_call(
        _mix_out1_kernel, grid=(n // tm,),
        in_specs=[row(d), row(yc.shape[1]), row(yd.shape[1])] + [full(p) for p in prm],
        out_specs=[row(d), row(d), row(LANES), row(LANES)],
        out_shape=[jax.ShapeDtypeStruct((n, d), F32), jax.ShapeDtypeStruct((n, d), F32),
                   jax.ShapeDtypeStruct((n, LANES), jnp.int32), jax.ShapeDtypeStruct((n, LANES), F32)],
        compiler_params=_cparams(("parallel",)), name="mix_out1",
    )(x2, yc, yd, *prm)


GATHER_UNROLL = 8


def _gather_rows(n_rows, make_copy):
    def body(j, carry):
        for q in range(GATHER_UNROLL):
            make_copy(j * GATHER_UNROLL + q).start(priority=q % 2)
        return carry

    lax.fori_loop(0, n_rows // GATHER_UNROLL, body, 0)


def _moe_kernel(nf, be_ref, tok_ref, dst_ref, nact_ref, h_hbm, wg_ref, wu_ref, wd_ref, y_hbm, buf_ref,
                xb_ref, gsem, ssem):
    i = pl.program_id(0)
    f = pl.program_id(1)
    n_blocks = pl.num_programs(0)
    tm = buf_ref.shape[1]
    nact = nact_ref[0]
    active = i < nact
    slot = lax.rem(i, 2)
    other = 1 - slot
    xs = lambda sl: buf_ref.at[sl]
    yb = lambda sl: buf_ref.at[2 + sl]

    def gather_copy(block, sl, r):
        tok = tok_ref[block * tm + r]
        return pltpu.make_async_copy(h_hbm.at[pl.ds(tok, 1)], buf_ref.at[sl, pl.ds(r, 1)], gsem.at[sl])

    def scatter_copy(block, sl, r):
        dst = dst_ref[block * tm + r]
        return pltpu.make_async_copy(buf_ref.at[2 + sl, pl.ds(r, 1)], y_hbm.at[pl.ds(dst, 1)], ssem.at[sl])

    def wait_rows(sem_slot_ref, buf):
        pltpu.make_async_copy(h_hbm.at[pl.ds(0, tm)], buf, sem_slot_ref).wait()

    @pl.when(f == 0)
    def _():
        @pl.when(i == 0)
        def _():
            _gather_rows(tm, functools.partial(gather_copy, 0, 0))
            buf_ref[3] = jnp.zeros(buf_ref.shape[1:], F32)
            n_real = y_hbm.shape[0] - 2 * tm
            for half in range(2):
                init = pltpu.make_async_copy(yb(1), y_hbm.at[pl.ds(n_real + half * tm, tm)], ssem.at[0])
                init.start()
                init.wait()

        @pl.when(i <= nact)
        def _():
            wait_rows(gsem.at[slot], xs(slot))

        @pl.when(jnp.logical_and(i >= 1, i <= nact))
        def _():
            wait_rows(ssem.at[slot], yb(slot))

        @pl.when(i == nact)
        def _():
            _gather_rows(tm, functools.partial(scatter_copy, i - 1, other))
            wait_rows(ssem.at[other], yb(other))

        @pl.when(active)
        def _():
            xb_ref[...] = buf_ref[slot].astype(BF16)
            buf_ref[2 + slot] = jnp.zeros(buf_ref.shape[1:], F32)

    @pl.when(active)
    def _():
        x = xb_ref[...]
        nch = wg_ref.shape[2] // MXU_TILE
        rows_f = tm // nf
        base = pl.multiple_of(f * rows_f, SUBLANES)
        prev = jnp.where(i == 0, n_blocks - 1, i - 1)
        for c in range(nch):
            for r in range(rows_f * c // nch, rows_f * (c + 1) // nch):
                gather_copy(i + 1, other, base + r).start(priority=1)
                scatter_copy(prev, other, base + r).start(priority=1)
            cols = slice(c * MXU_TILE, (c + 1) * MXU_TILE)
            gate = jnp.dot(x, wg_ref[0, :, cols], preferred_element_type=F32)
            up = jnp.dot(x, wu_ref[0, :, cols], preferred_element_type=F32)
            buf_ref[2 + slot] += jnp.dot((_silu(gate) * up).astype(BF16), wd_ref[0, cols, :],
                                         preferred_element_type=F32)


def _moe_experts(h, block_expert, slot_tok, slot_dst, nact, n_rows, wg, wu, wd, tf=7 * MXU_TILE):
    n, d = h.shape
    tm = MOE_ROWS
    n_blocks = slot_tok.shape[0] // tm
    ff = wg.shape[2]
    nf = ff // tf
    wmap = lambda i, j, be, tok, dst, na: (be[i], 0, j)
    grid_spec = pltpu.PrefetchScalarGridSpec(
        num_scalar_prefetch=4, grid=(n_blocks, nf),
        in_specs=[pl.BlockSpec(memory_space=pl.ANY),
                  pl.BlockSpec((1, d, tf), wmap), pl.BlockSpec((1, d, tf), wmap),
                  pl.BlockSpec((1, tf, d), lambda i, j, be, tok, dst, na: (be[i], j, 0))],
        out_specs=pl.BlockSpec(memory_space=pl.ANY),
        scratch_shapes=[pltpu.VMEM((4, tm, d), F32), pltpu.VMEM((tm, d), BF16),
                        pltpu.SemaphoreType.DMA((2,)), pltpu.SemaphoreType.DMA((2,))])
    return pl.pallas_call(
        functools.partial(_moe_kernel, nf), grid_spec=grid_spec,
        out_shape=jax.ShapeDtypeStruct((n_rows, d), F32),
        compiler_params=pltpu.CompilerParams(dimension_semantics=("arbitrary", "arbitrary"),
                                             vmem_limit_bytes=VMEM_LIMIT, disable_bounds_checks=True),
        name="moe_experts",
    )(block_expert, slot_tok, slot_dst, nact, h, wg, wu, wd)


def _combine_kernel(x_ref, y0_ref, y1_ref, gate_ref, gpost_ref, o_ref):
    gates = gate_ref[...]
    y = gates[:, 0:1] * y0_ref[...] + gates[:, 1:2] * y1_ref[...]
    o_ref[...] = x_ref[...] + _rms(y, gpost_ref[...])


def _moe_combine(x1, y, gates, gpost, tm=512):
    n, d = x1.shape
    nt = n // tm
    return pl.pallas_call(
        _combine_kernel, grid=(nt,),
        in_specs=[pl.BlockSpec((tm, d), lambda i: (i, 0)), pl.BlockSpec((tm, d), lambda i: (i, 0)),
                  pl.BlockSpec((tm, d), lambda i: (nt + i, 0)),
                  pl.BlockSpec((tm, LANES), lambda i: (i, 0)), pl.BlockSpec((1, d), lambda i: (0, 0))],
        out_specs=pl.BlockSpec((tm, d), lambda i: (i, 0)),
        out_shape=jax.ShapeDtypeStruct((n, d), F32),
        compiler_params=_cparams(("parallel",)), name="moe_combine",
    )(x1, y, y, gates, gpost)


def _moe_plan(idx, n):
    tm = MOE_ROWS
    flat_e = idx[:, :2].reshape(-1)
    onehot = (flat_e[:, None] == jnp.arange(MOE_EXPERTS, dtype=jnp.int32)[None, :]).astype(jnp.int32)
    csum = jnp.cumsum(onehot, axis=0)
    counts = csum[-1]
    rank = jnp.sum((csum - onehot) * onehot, axis=1)
    padded = (counts + tm - 1) // tm * tm
    pend = jnp.cumsum(padded)
    pstart = pend - padded
    dest = (jnp.sum(onehot * pstart[None, :], axis=1) + rank).astype(jnp.int32)
    n_blocks = (2 * n) // tm + MOE_EXPERTS + 1
    n_slots = n_blocks * tm
    slot_pair = jnp.full((n_slots,), -1, jnp.int32).at[dest].set(jnp.arange(2 * n, dtype=jnp.int32))
    real = slot_pair >= 0
    slot_tok = jnp.where(real, slot_pair // 2, 0)
    s_id = jnp.arange(n_slots, dtype=jnp.int32)
    slot_dst = jnp.where(real, slot_pair % 2 * n + slot_pair // 2, 2 * n + (s_id // tm) % 2 * tm + s_id % tm)
    block_start = jnp.arange(n_blocks, dtype=jnp.int32) * tm
    block_expert = jnp.minimum(jnp.searchsorted(pend, block_start, side='right'),
                               MOE_EXPERTS - 1).astype(jnp.int32)
    nact = (pend[-1] // tm).astype(jnp.int32).reshape(1)
    return block_expert, slot_tok, slot_dst, nact, 2 * n + 2 * tm


def kernel(x, l0_norm_pre_mix, l0_w_in, l0_rwkv_mu, l0_rwkv_w0, l0_rwkv_w2, l0_rwkv_a0, l0_rwkv_a2, l0_rwkv_g2, l0_rwkv_k_k, l0_rwkv_k_a, l0_rwkv_r_k, l0_rwkv_ln_w, l0_rwkv_ln_b, l0_gmlp_ln_w, l0_gmlp_ln_b, l0_gmlp_ws, l0_gmlp_bs, l0_w_out, l0_norm_post_mix, l0_norm_pre_ffn, l0_ffn_w_gate, l0_ffn_w_up, l0_ffn_w_down, l0_norm_post_ffn, l1_norm_pre_mix, l1_w_in, l1_s5_a_re, l1_s5_a_im, l1_s5_log_dt, l1_s5_b_re, l1_s5_b_im, l1_s5_c_re, l1_s5_c_im, l1_s5_d, l1_s5_glu_w, l1_s5_glu_b, l1_m2_conv_w, l1_m2_conv_b, l1_m2_dt_bias, l1_m2_a_log, l1_m2_d, l1_m2_norm_w, l1_w_out, l1_norm_post_mix, l1_norm_pre_ffn, l1_moe_router, l1_moe_w_gate, l1_moe_w_up, l1_moe_w_down, l1_norm_post_ffn):
    b, t, d = x.shape
    n = b * t
    x2 = x.reshape(n, d)
    row = lambda vec: vec.reshape(1, -1)

    aw = l0_rwkv_w0.shape[0]
    heads = aw // RWKV_HEAD
    lw_, la_, lg_ = l0_rwkv_w2.shape[0], l0_rwkv_a2.shape[0], l0_rwkv_g2.shape[0]
    a_in = 3 * aw + lw_ + la_ + lg_
    padc = lambda m, wdt: jnp.pad(m, ((0, 0), (0, LANES - wdt)))
    o = 3 * aw
    w_a = jnp.concatenate([l0_w_in[:, :o], padc(l0_w_in[:, o:o + lw_], lw_),
                           padc(l0_w_in[:, o + lw_:o + lw_ + la_], la_),
                           padc(l0_w_in[:, o + lw_ + la_:a_in], lg_)], axis=1).astype(BF16)
    w_b = l0_w_in[:, a_in:].astype(BF16)
    p_a, p_b = _norm_proj(x2, l0_norm_pre_mix, [w_a, w_b], [F32, BF16])
    padv = lambda vec, wdt: jnp.pad(vec, (0, LANES - wdt))
    mu = l0_rwkv_mu
    mu_p = jnp.concatenate([mu[:o], padv(mu[o:o + lw_], lw_), padv(mu[o + lw_:o + lw_ + la_], la_),
                            padv(mu[o + lw_ + la_:], lg_)])
    padr = lambda m: jnp.pad(m, ((0, LANES - m.shape[0]), (0, 0))).astype(BF16)
    hid = jnp.arange(LANES, dtype=jnp.int32) // RWKV_HEAD
    gsum = (hid[:, None] == hid[None, :]).astype(BF16)
    rwkv_prm = [row(mu_p), row(l0_rwkv_w0), padr(l0_rwkv_w2), row(l0_rwkv_a0), padr(l0_rwkv_a2),
                padr(l0_rwkv_g2), row(l0_rwkv_k_k), row(l0_rwkv_k_a), row(l0_rwkv_r_k),
                row(l0_rwkv_ln_w), row(l0_rwkv_ln_b), gsum]
    ya = _rwkv_mix(p_a.reshape(b, t, -1), rwkv_prm, heads)
    yb = _gmlp_mix(p_b.reshape(b, t, -1), l0_gmlp_ln_w, l0_gmlp_ln_b, l0_gmlp_ws, l0_gmlp_bs)
    wo = l0_w_out.astype(BF16)
    x2 = _mix_out0(x2, ya.reshape(n, -1), yb.reshape(n, -1), wo[:aw], wo[aw:], row(l0_norm_post_mix))
    x2 = _ffn(x2, row(l0_norm_pre_ffn), l0_ffn_w_gate.astype(BF16), l0_ffn_w_up.astype(BF16),
              l0_ffn_w_down.astype(BF16), row(l0_norm_post_ffn))

    cw = l1_s5_d.shape[0]
    dw = l1_m2_norm_w.shape[0]
    xw = l1_m2_conv_w.shape[1]
    nh = l1_m2_dt_bias.shape[0]
    w1 = l1_w_in
    w_parts = [w1[:, :cw], w1[:, cw:cw + dw], w1[:, cw + dw:cw + dw + xw],
               padc(w1[:, cw + dw + xw:], nh)]
    u_c, z_d, xbc, dtp = _norm_proj(x2, l1_norm_pre_mix, [w.astype(BF16) for w in w_parts],
                                    [BF16, BF16, BF16, F32])
    s5_w = _s5_weights(l1_s5_a_re, l1_s5_a_im, l1_s5_log_dt, l1_s5_b_re, l1_s5_b_im, l1_s5_c_re,
                       l1_s5_c_im, l1_s5_d, S5_CHUNK)
    yc = _s5_core(u_c.reshape(b, t, cw), s5_w)
    yd = _ssd_mix(z_d.reshape(b, t, dw), xbc.reshape(b, t, xw), dtp.reshape(b, t, LANES),
                  l1_m2_conv_w, l1_m2_conv_b, l1_m2_dt_bias, l1_m2_a_log, l1_m2_d, l1_m2_norm_w)
    wo1 = l1_w_out.astype(BF16)
    x1, h, idx, gates = _mix_out1(x2, yc.reshape(n, cw), yd.reshape(n, dw), l1_s5_glu_w.astype(BF16),
                                  row(l1_s5_glu_b), wo1[:cw], wo1[cw:], row(l1_norm_post_mix),
                                  row(l1_norm_pre_ffn), l1_moe_router)
    block_expert, slot_tok, slot_dst, nact, n_rows = _moe_plan(idx, n)
    ys = _moe_experts(h, block_expert, slot_tok, slot_dst, nact, n_rows, l1_moe_w_gate.astype(BF16),
                      l1_moe_w_up.astype(BF16), l1_moe_w_down.astype(BF16))
    out = _moe_combine(x1, ys, gates, row(l1_norm_post_ffn))
    return out.reshape(b, t, d)
```

```python
import functools

import jax
import jax.numpy as jnp
from jax import lax
from jax.experimental import pallas as pl
from jax.experimental.pallas import tpu as pltpu

F32 = jnp.float32
BF16 = jnp.bfloat16

EPS = 1e-6
RWKV_GN_EPS = 64e-5
RWKV_HEAD = 64
RWKV_CHUNK = 64
GMLP_CHUNK = 128
GMLP_GROUPS = 4
S5_GROUP_CH = 16
S5_STATE = 64
S5_CHUNK = 64
SSD_HEAD = 64
SSD_HEADS = 8
SSD_GROUPS = 2
SSD_STATE = 128
SSD_CONV = 4
SSD_CHUNK = 128
MOE_EXPERTS = 8
MOE_ROWS = 512
MXU_TILE = 256
LANES = 128
SUBLANES = 8
VMEM_LIMIT = 56 * 1024 * 1024


def _cparams(sem):
    return pltpu.CompilerParams(dimension_semantics=sem, vmem_limit_bytes=VMEM_LIMIT)


def _bdot(a, b):
    return jnp.dot(a.astype(BF16), b.astype(BF16), preferred_element_type=F32)


def _bdot_nt(a, b):
    return lax.dot_general(a.astype(BF16), b.astype(BF16), (((1,), (1,)), ((), ())),
                           preferred_element_type=F32)


def _bdot_tn(a, b):
    return lax.dot_general(a.astype(BF16), b.astype(BF16), (((0,), (0,)), ((), ())),
                           preferred_element_type=F32)


def _split3(x):
    h = x.astype(BF16)
    r1 = x - h.astype(F32)
    m = r1.astype(BF16)
    l = (r1 - m.astype(F32)).astype(BF16)
    return h, m, l


def _dot_x_exact(x, e):
    h, m, l = _split3(x)
    e = e.astype(BF16)
    return (jnp.dot(h, e, preferred_element_type=F32) + jnp.dot(m, e, preferred_element_type=F32)
            + jnp.dot(l, e, preferred_element_type=F32))


def _dot_exact_x(e, x):
    h, m, l = _split3(x)
    e = e.astype(BF16)
    return (jnp.dot(e, h, preferred_element_type=F32) + jnp.dot(e, m, preferred_element_type=F32)
            + jnp.dot(e, l, preferred_element_type=F32))


def _rms(x, g):
    return x * lax.rsqrt(jnp.mean(x * x, axis=-1, keepdims=True) + EPS) * g


def _sigmoid(x):
    return 1.0 / (1.0 + jnp.exp(-x))


def _silu(x):
    return x * _sigmoid(x)


def _softplus(x):
    return jnp.maximum(x, 0.0) + jnp.log(1.0 + jnp.exp(-jnp.abs(x)))


def _gelu_tanh(x):
    return 0.5 * x * (1.0 + jnp.tanh(0.7978845608028654 * (x + 0.044715 * x * x * x)))


def _iota2(shape, dim):
    return lax.broadcasted_iota(jnp.int32, shape, dim)


def _norm_proj_kernel(n_out, x_ref, g_ref, *refs):
    w_refs = refs[:n_out]
    o_refs = refs[n_out:]
    xn = _rms(x_ref[...], g_ref[...]).astype(BF16)
    for w_ref, o_ref in zip(w_refs, o_refs):
        o_ref[...] = jnp.dot(xn, w_ref[...], preferred_element_type=F32).astype(o_ref.dtype)


def _norm_proj(x2, g, ws, dtypes, tm=512):
    n, d = x2.shape
    in_specs = [pl.BlockSpec((tm, d), lambda i: (i, 0)), pl.BlockSpec((1, d), lambda i: (0, 0))]
    in_specs += [pl.BlockSpec(w.shape, lambda i: (0, 0)) for w in ws]
    out_specs = [pl.BlockSpec((tm, w.shape[1]), lambda i: (i, 0)) for w in ws]
    out_shape = [jax.ShapeDtypeStruct((n, w.shape[1]), dt) for w, dt in zip(ws, dtypes)]
    return pl.pallas_call(
        functools.partial(_norm_proj_kernel, len(ws)),
        grid=(n // tm,), in_specs=in_specs, out_specs=out_specs, out_shape=out_shape,
        compiler_params=_cparams(("parallel",)), name="norm_proj",
    )(x2, g.reshape(1, d), *ws)


PRE_NAMES = ("v", "g", "bonus", "rt", "kt", "bt", "at", "bh", "kh")


def _rwkv_kernel(heads, nb, p_ref, pp_ref, mu_ref, w0_ref, w2_ref, a0_ref, a2_ref, g2_ref, kk_ref,
                 ka_ref, rk_ref, lnw_ref, lnb_ref, gs_ref, o_ref, z_ref, pre_ref, wl_ref):
    c = pl.program_id(1)
    ln = RWKV_CHUNK
    hd = RWKV_HEAD
    aw = heads * hd

    @pl.when(c == 0)
    def _():
        z_ref[...] = jnp.zeros_like(z_ref)
        pre_ref[...] = jnp.zeros_like(pre_ref)
        wl_ref[...] = jnp.zeros_like(wl_ref)

    tril_f = jnp.where(_iota2((ln, ln), 0) >= _iota2((ln, ln), 1), 1.0, 0.0)
    rows = _iota2((ln, 1), 0)
    gs_tile = gs_ref[...]

    def gs(x):
        nt = x.shape[1] // LANES
        stacked = jnp.concatenate([x[:, j * LANES:(j + 1) * LANES] for j in range(nt)], axis=0)
        red = _dot_x_exact(stacked, gs_tile)
        return jnp.concatenate([red[j * ln:(j + 1) * ln] for j in range(nt)], axis=1)

    pre_idx = {nm: idx for idx, nm in enumerate(PRE_NAMES)}
    pending = []

    def prep_steps():
        for bi in range(nb):
            p = p_ref[bi]
            prev = jnp.where(c == 0, 0.0, pp_ref[bi][SUBLANES - 1:SUBLANES, :])
            ps = jnp.where(rows == 0, prev, pltpu.roll(p, 1, axis=0))
            pm = p + (ps - p) * mu_ref[...]
            r = pm[:, 0:aw]
            k = pm[:, aw:2 * aw]
            v = pm[:, 2 * aw:3 * aw]
            xw = pm[:, 3 * aw:3 * aw + LANES]
            xa = pm[:, 3 * aw + LANES:3 * aw + 2 * LANES]
            xg = pm[:, 3 * aw + 2 * LANES:3 * aw + 3 * LANES]
            yield
            w = w0_ref[...] + _bdot(jnp.tanh(xw), w2_ref[...])
            a = _sigmoid(a0_ref[...] + _bdot(xa, a2_ref[...]))
            g = _bdot(_sigmoid(xg), g2_ref[...])
            yield
            w = -_softplus(-w) - 0.5
            lw = -jnp.exp(w)
            kk = k * kk_ref[...]
            kk_ss = gs(kk * kk)
            yield
            cs = _dot_exact_x(tril_f, lw)
            kk = kk / jnp.maximum(jnp.sqrt(kk_ss), 1e-12)
            kmod = k * (1.0 + (a - 1.0) * ka_ref[...])
            yield
            bonus = gs(r * kmod * rk_ref[...])
            bvec = kk * a
            cs_last = cs[ln - 1:ln, :]
            encs = jnp.exp(-cs)
            yield
            dec_end = jnp.exp(cs_last - cs)
            nxt = dict(v=v, g=g, bonus=bonus, rt=r * jnp.exp(cs), kt=kmod * encs, bt=bvec * encs,
                       at=-kk * jnp.exp(cs - lw), bh=bvec * dec_end, kh=kmod * dec_end)
            pending.append((bi, nxt, jnp.broadcast_to(jnp.exp(cs_last), (SUBLANES, aw))))
            yield

    prep = prep_steps()
    tick = lambda: next(prep, None)

    lane = _iota2((ln, LANES), 1)
    lane_in = jnp.where(lane >= hd, lane - hd, lane)
    trow = _iota2((ln, LANES), 0)
    left = lane < hd
    tril_p = lane_in <= trow
    stril_p = lane_in < trow
    eye_p = lane_in == trow
    eye_pf = jnp.where(eye_p, 1.0, 0.0)

    def bd(x):
        xb = x.astype(BF16)
        zero = jnp.zeros_like(xb)
        return jnp.concatenate([jnp.where(left, xb, zero), jnp.where(left, zero, xb)], axis=0)

    def dot(a, b):
        return jnp.dot(a.astype(BF16), b, preferred_element_type=F32)

    npair = heads // 2
    pairs = [(bi, j) for bi in range(nb) for j in range(npair)]

    class _Tiles:
        def __init__(self, name):
            self.idx = pre_idx[name]

        def __getitem__(self, i):
            bi, j = pairs[i]
            return pre_ref[self.idx, bi, :, j * LANES:(j + 1) * LANES]

    at, rt, bt, kt, vv, bh, kh = (_Tiles(n) for n in ("at", "rt", "bt", "kt", "v", "bh", "kh"))
    wl = [wl_ref[bi, 0:1, j * LANES:(j + 1) * LANES] for bi, j in pairs]
    z_all = z_ref[...]
    zs = [z_all[bi, j] for bi, j in pairs]
    npr = range(len(pairs))
    lhs = [jnp.concatenate([at[i], rt[i]], axis=0).astype(BF16) for i in npr]
    abk = [lax.dot_general(lhs[i], jnp.concatenate([bd(bt[i]), bd(kt[i])], axis=0),
                           (((1,), (1,)), ((), ())), preferred_element_type=F32) for i in npr]
    ab = [abk[i][:, :LANES] for i in npr]
    ak = [abk[i][:, LANES:] for i in npr]
    tick()
    nmat = [jnp.where(stril_p, ab[i][:ln], 0.0) for i in npr]
    tinv = [eye_pf + nmat[i] for i in npr]
    npow = [dot(nmat[i], bd(nmat[i])) for i in npr]
    tick()
    for step in range(5):
        bdn = [bd(npow[i]) for i in npr]
        if step < 4:
            both = [dot(jnp.concatenate([tinv[i], npow[i]], axis=0), bdn[i]) for i in npr]
            tinv = [tinv[i] + both[i][:ln] for i in npr]
            npow = [both[i][ln:] for i in npr]
        else:
            tinv = [tinv[i] + dot(tinv[i], bdn[i]) for i in npr]
        tick()
    bdv = [bd(vv[i]) for i in npr]
    bdz = [bd(zs[i]) for i in npr]
    xmat = [dot(jnp.concatenate([jnp.where(stril_p, ak[i][:ln], 0.0), at[i]], axis=1),
                jnp.concatenate([bdv[i], bdz[i]], axis=0)) for i in npr]
    tick()
    u = [dot(tinv[i], bd(xmat[i])) for i in npr]
    tick()
    ys_p = [dot(jnp.concatenate([rt[i], jnp.where(tril_p, ab[i][ln:], 0.0),
                                 jnp.where(tril_p, ak[i][ln:], 0.0)], axis=1),
                jnp.concatenate([bdz[i], bd(u[i]), bdv[i]], axis=0)) for i in npr]
    tick()
    cross = [_bdot_tn(jnp.concatenate([bh[i], kh[i]], axis=0), jnp.concatenate([u[i], vv[i]], axis=0))
             for i in npr]
    tick()
    z_new = []
    for i in npr:
        dg = jnp.where(eye_p, wl[i], 0.0)
        wl_i = jnp.sum(jnp.where(left, dg, 0.0), axis=1, keepdims=True)
        wl_j = jnp.sum(jnp.where(left, 0.0, dg), axis=1, keepdims=True)
        z_new.append(jnp.where(left, wl_i, wl_j) * zs[i] + jnp.where(left, cross[i][:ln], cross[i][ln:]))
    z_ref[...] = jnp.stack(z_new, axis=0).reshape(z_ref.shape)

    tick()
    inv = 1.0 / hd
    for bi in range(nb):
        y = jnp.concatenate(ys_p[bi * npair:(bi + 1) * npair], axis=1)
        mean = gs(y) * inv
        d = y - mean
        var = gs(d * d) * inv
        yn = d * lax.rsqrt(var + RWKV_GN_EPS) * lnw_ref[...] + lnb_ref[...]
        o_ref[bi] = ((yn + pre_ref[pre_idx["bonus"], bi] * pre_ref[pre_idx["v"], bi])
                     * pre_ref[pre_idx["g"], bi]).astype(o_ref.dtype)
        tick()
    for _ in prep:
        pass
    for bi, nxt, wl_next in pending:
        for nm, idx in pre_idx.items():
            pre_ref[idx, bi] = nxt[nm]
        wl_ref[bi] = wl_next


def _rwkv_mix(p_a, prm, heads, nb=2):
    b, t, cin = p_a.shape
    aw = heads * RWKV_HEAD
    ln = RWKV_CHUNK
    nc = t // ln
    sub = ln // SUBLANES
    full = lambda arr: pl.BlockSpec(arr.shape, lambda i, j: (0,) * arr.ndim)
    in_specs = [pl.BlockSpec((nb, ln, cin), lambda i, j: (i, jnp.minimum(j, nc - 1), 0)),
                pl.BlockSpec((nb, SUBLANES, cin),
                             lambda i, j: (i, jnp.maximum(jnp.minimum(j, nc - 1) * sub - 1, 0), 0))]
    in_specs += [full(x) for x in prm]
    return pl.pallas_call(
        functools.partial(_rwkv_kernel, heads, nb),
        grid=(b // nb, nc + 1), in_specs=in_specs,
        out_specs=pl.BlockSpec((nb, ln, aw), lambda i, j: (i, jnp.maximum(j - 1, 0), 0)),
        out_shape=jax.ShapeDtypeStruct((b, t, aw), BF16),
        scratch_shapes=[pltpu.VMEM((nb, heads // 2, RWKV_HEAD, 2 * RWKV_HEAD), F32),
                        pltpu.VMEM((len(PRE_NAMES), nb, ln, aw), F32),
                        pltpu.VMEM((nb, SUBLANES, aw), F32)],
        compiler_params=_cparams(("parallel", "arbitrary")), name="rwkv7",
    )(p_a, p_a, *prm)


def _gmlp_kernel(p_ref, lnw_ref, lnb_ref, ws_ref, bs_ref, o_ref):
    ln = GMLP_CHUNK
    bw = p_ref.shape[2] // 2
    gd = bw // GMLP_GROUPS
    tril = _iota2((ln, ln), 0) >= _iota2((ln, ln), 1)
    ws_c = [jnp.where(tril, ws_ref[gi], 0.0).astype(BF16) for gi in range(GMLP_GROUPS)]
    for ci in range(p_ref.shape[1] // ln):
        x = _gelu_tanh(p_ref[0, ci * ln:(ci + 1) * ln, :].astype(F32))
        for gi in range(GMLP_GROUPS):
            u = x[:, gi * gd:(gi + 1) * gd]
            v = x[:, bw + gi * gd:bw + (gi + 1) * gd]
            mean = jnp.mean(v, axis=-1, keepdims=True)
            d = v - mean
            var = jnp.mean(d * d, axis=-1, keepdims=True)
            vn = d * lax.rsqrt(var + EPS) * lnw_ref[gi:gi + 1, :] + lnb_ref[gi:gi + 1, :]
            s = jnp.dot(ws_c[gi], vn.astype(BF16), preferred_element_type=F32) + bs_ref[gi]
            o_ref[0, ci * ln:(ci + 1) * ln, gi * gd:(gi + 1) * gd] = (u * s).astype(o_ref.dtype)


def _gmlp_mix(p_b, ln_w, ln_b, ws, bs, rows=512):
    b, t, cin = p_b.shape
    bw = cin // 2
    gd = bw // GMLP_GROUPS
    bs_b = jnp.broadcast_to(bs[:, :, None], (GMLP_GROUPS, GMLP_CHUNK, gd))
    full = lambda arr: pl.BlockSpec(arr.shape, lambda i, j: (0,) * arr.ndim)
    return pl.pallas_call(
        _gmlp_kernel, grid=(b, t // rows),
        in_specs=[pl.BlockSpec((1, rows, cin), lambda i, j: (i, j, 0)),
                  full(ln_w), full(ln_b), full(ws), full(bs_b)],
        out_specs=pl.BlockSpec((1, rows, bw), lambda i, j: (i, j, 0)),
        out_shape=jax.ShapeDtypeStruct((b, t, bw), BF16),
        compiler_params=_cparams(("parallel", "parallel")), name="gmlp",
    )(p_b, ln_w, ln_b, ws, bs_b)


def _mix_out0_kernel(x_ref, ya_ref, yb_ref, wa_ref, wb_ref, g_ref, o_ref):
    y = _bdot(ya_ref[...], wa_ref[...]) + _bdot(yb_ref[...], wb_ref[...])
    o_ref[...] = x_ref[...] + _rms(y, g_ref[...])


def _mix_out0(x2, ya, yb, wa, wb, g, tm=512):
    n, d = x2.shape
    row = lambda arr: pl.BlockSpec((tm, arr.shape[1]), lambda i: (i, 0))
    full = lambda arr: pl.BlockSpec(arr.shape, lambda i: (0,) * arr.ndim)
    return pl.pallas_call(
        _mix_out0_kernel, grid=(n // tm,),
        in_specs=[row(x2), row(ya), row(yb), full(wa), full(wb), full(g)],
        out_specs=row(x2), out_shape=jax.ShapeDtypeStruct((n, d), F32),
        compiler_params=_cparams(("parallel",)), name="mix_out0",
    )(x2, ya, yb, wa, wb, g)


def _ffn_kernel(fc, x_ref, gpre_ref, wg_ref, wu_ref, wd_ref, gpost_ref, o_ref):
    h = _rms(x_ref[...], gpre_ref[...]).astype(BF16)
    acc = None
    for c in range(wg_ref.shape[1] // fc):
        cols = slice(c * fc, (c + 1) * fc)
        gate = jnp.dot(h, wg_ref[:, cols], preferred_element_type=F32)
        up = jnp.dot(h, wu_ref[:, cols], preferred_element_type=F32)
        part = jnp.dot((_silu(gate) * up).astype(BF16), wd_ref[cols, :], preferred_element_type=F32)
        acc = part if acc is None else acc + part
    o_ref[...] = x_ref[...] + _rms(acc, gpost_ref[...])


def _ffn(x2, gpre, wg, wu, wd, gpost, tm=512, fc=MXU_TILE):
    n, d = x2.shape
    full = lambda arr: pl.BlockSpec(arr.shape, lambda i: (0,) * arr.ndim)
    return pl.pallas_call(
        functools.partial(_ffn_kernel, fc), grid=(n // tm,),
        in_specs=[pl.BlockSpec((tm, d), lambda i: (i, 0)), full(gpre), full(wg), full(wu), full(wd),
                  full(gpost)],
        out_specs=pl.BlockSpec((tm, d), lambda i: (i, 0)),
        out_shape=jax.ShapeDtypeStruct((n, d), F32),
        compiler_params=_cparams(("parallel",)), name="ffn",
    )(x2, gpre, wg, wu, wd, gpost)


def _s5_kernel(nc, nb, u_ref, tap_ref, wsr_ref, wsi_ref, wcr_ref, wci_ref, alr_ref, ali_ref, d_ref, o_ref,
               toep_ref):
    u = u_ref[0]
    taps = tap_ref[0]
    ch = taps.shape[0]
    lane = _iota2(taps.shape, 1)
    for s in range(taps.shape[1] // ch):
        blk = taps if s == 0 else jnp.where(lane >= ch * s, pltpu.roll(taps, ch * s, axis=1), 0.0)
        toep_ref[ch * s:ch * (s + 1), :] = blk.astype(BF16)
    y = jnp.dot(u, toep_ref[...], preferred_element_type=F32)
    xer = jnp.dot(u, wsr_ref[0], preferred_element_type=F32)
    xei = jnp.dot(u, wsi_ref[0], preferred_element_type=F32)
    alr = alr_ref[0]
    ali = ali_ref[0]
    cr = jnp.zeros((nb, xer.shape[1]), F32)
    ci = jnp.zeros((nb, xer.shape[1]), F32)
    prs, pis = [], []
    for c in range(nc):
        prs.append(cr)
        pis.append(ci)
        er = xer[c * nb:(c + 1) * nb]
        ei = xei[c * nb:(c + 1) * nb]
        cr, ci = alr * cr - ali * ci + er, alr * ci + ali * cr + ei
    pr = jnp.concatenate(prs, axis=0)
    pi = jnp.concatenate(pis, axis=0)
    y = y + _bdot(pr, wcr_ref[0]) + _bdot(pi, wci_ref[0])
    o_ref[0] = (y + d_ref[0] * u.astype(F32)).astype(o_ref.dtype)


def _s5_weights(a_re, a_im, log_dt, b_re, b_im, c_re, c_im, d_skip, ln):
    g, st = a_re.shape
    ch = b_re.shape[2]
    dt = jnp.exp(log_dt)[:, None]
    lr, li = a_re, a_im
    tau = jnp.arange(ln + 1, dtype=F32)[:, None, None]
    mag = jnp.exp(lr[None] * dt[None] * tau)
    pw_r = mag * jnp.cos(li[None] * dt[None] * tau)
    pw_i = mag * jnp.sin(li[None] * dt[None] * tau)
    ab_r, ab_i = pw_r[1], pw_i[1]
    nr, ni = ab_r - 1.0, ab_i
    den = lr * lr + li * li
    fr, fi = (nr * lr + ni * li) / den, (ni * lr - nr * li) / den
    bb_r = fr[..., None] * b_re - fi[..., None] * b_im
    bb_i = fr[..., None] * b_im + fi[..., None] * b_re
    cp_r = c_re[None] * pw_r[:ln, :, None, :] - c_im[None] * pw_i[:ln, :, None, :]
    cp_i = c_re[None] * pw_i[:ln, :, None, :] + c_im[None] * pw_r[:ln, :, None, :]
    hp = lax.Precision.HIGHEST
    taps = (jnp.einsum('tgcp,gpd->gdtc', cp_r, bb_r, precision=hp)
            - jnp.einsum('tgcp,gpd->gdtc', cp_i, bb_i, precision=hp))
    taps = taps.reshape(g, ch, ln * ch)
    rev_r, rev_i = pw_r[:ln][::-1], pw_i[:ln][::-1]
    ws_r = rev_r[..., None] * bb_r[None] - rev_i[..., None] * bb_i[None]
    ws_i = rev_r[..., None] * bb_i[None] + rev_i[..., None] * bb_r[None]
    ws_r = ws_r.transpose(1, 0, 3, 2).reshape(g, ln * ch, st)
    ws_i = ws_i.transpose(1, 0, 3, 2).reshape(g, ln * ch, st)
    q_r, q_i = pw_r[1:ln + 1], pw_i[1:ln + 1]
    wc_r = c_re[None] * q_r[:, :, None, :] - c_im[None] * q_i[:, :, None, :]
    wc_i = -(c_re[None] * q_i[:, :, None, :] + c_im[None] * q_r[:, :, None, :])
    wc_r = wc_r.transpose(1, 3, 0, 2).reshape(g, st, ln * ch)
    wc_i = wc_i.transpose(1, 3, 0, 2).reshape(g, st, ln * ch)
    al_r = pw_r[ln].reshape(g, 1, st)
    al_i = pw_i[ln].reshape(g, 1, st)
    d_t = jnp.tile(d_skip.reshape(g, 1, ch), (1, ln, 1)).reshape(g, 1, ln * ch)
    return (taps, ws_r.astype(BF16), ws_i.astype(BF16), wc_r.astype(BF16),
            wc_i.astype(BF16), al_r, al_i, d_t)


def _s5_core(u, weights):
    b, t, cw = u.shape
    ln, ch = S5_CHUNK, S5_GROUP_CH
    g = cw // ch
    nc = t // ln
    ug = u.astype(BF16).reshape(b, nc, ln, g, ch).transpose(3, 1, 0, 2, 4).reshape(g, nc * b, ln * ch)
    per_g = lambda arr: pl.BlockSpec((1,) + arr.shape[1:], lambda i: (i, 0, 0))
    yg = pl.pallas_call(
        functools.partial(_s5_kernel, nc, b), grid=(g,),
        in_specs=[per_g(ug)] + [per_g(w) for w in weights],
        out_specs=per_g(ug), out_shape=jax.ShapeDtypeStruct(ug.shape, BF16),
        scratch_shapes=[pltpu.VMEM((ln * ch, ln * ch), BF16)],
        compiler_params=_cparams(("parallel",)), name="s5",
    )(ug, *weights)
    return yg.reshape(g, nc, b, ln, ch).transpose(2, 1, 3, 0, 4).reshape(b, t, cw)


def _ssd_kernel(z_ref, xbc_ref, xp_ref, dt_ref, cw_ref, cb_ref, dtb_ref, alog_ref, dsk_ref, nw_ref,
                o_ref, s_ref):
    c = pl.program_id(1)
    ln = SSD_CHUNK
    hd = SSD_HEAD
    dw = SSD_HEADS * hd
    gn = SSD_STATE

    @pl.when(c == 0)
    def _():
        s_ref[...] = jnp.zeros_like(s_ref)

    xbc = xbc_ref[0]
    halo = xp_ref.shape[1]
    prev = jnp.where(c == 0, jnp.zeros_like(xp_ref[0]), xp_ref[0])
    full = jnp.concatenate([prev, xbc], axis=0).astype(BF16)
    conv = cb_ref[...] + cw_ref[SSD_CONV - 1:SSD_CONV, :] * xbc.astype(F32)
    for j in range(SSD_CONV - 1):
        lag = SSD_CONV - 1 - j
        pick = _iota2((ln, halo + ln), 1) == _iota2((ln, halo + ln), 0) + (halo - lag)
        shifted = jnp.dot(jnp.where(pick, 1.0, 0.0).astype(BF16), full, preferred_element_type=F32)
        conv = conv + cw_ref[j:j + 1, :] * shifted
    act = _silu(conv)
    xh = act[:, :dw]
    dt = _softplus(dt_ref[0] + dtb_ref[...])
    adt = -jnp.exp(alog_ref[...]) * dt
    tril = _iota2((ln, ln), 0) >= _iota2((ln, ln), 1)
    acs = _dot_exact_x(jnp.where(tril, 1.0, 0.0), adt)
    acs_t = acs.T
    tot = acs[ln - 1:ln, :]
    hg = SSD_HEADS // SSD_GROUPS
    s_all = s_ref[...]
    y_heads, s_heads = [], []
    for gi in range(SSD_GROUPS):
        bm = act[:, dw + gi * gn:dw + (gi + 1) * gn]
        cm = act[:, dw + SSD_GROUPS * gn + gi * gn:dw + SSD_GROUPS * gn + (gi + 1) * gn]
        cb = _bdot_nt(cm, bm)
        for hh in range(hg):
            h = gi * hg + hh
            sl = slice(h * hd, (h + 1) * hd)
            col = acs[:, h:h + 1]
            rowv = acs_t[h:h + 1, :]
            lmat = jnp.exp(jnp.where(tril, col - rowv, -jnp.inf))
            xh_h = xh[:, sl]
            xdt = xh_h * dt[:, h:h + 1]
            tot_h = tot[:, h:h + 1]
            st = s_all[h]
            y_h = _bdot(cb * lmat, xdt) + jnp.exp(col) * _bdot(cm, st)
            s_heads.append(jnp.exp(tot_h) * st + _bdot_tn(bm * jnp.exp(tot_h - col), xdt))
            y_heads.append(y_h + dsk_ref[:, sl] * xh_h)
    s_ref[...] = jnp.stack(s_heads, axis=0)
    y = jnp.concatenate(y_heads, axis=1) * _silu(z_ref[0].astype(F32))
    gw = dw // SSD_GROUPS
    for gi in range(SSD_GROUPS):
        yg = y[:, gi * gw:(gi + 1) * gw]
        yg = yg * lax.rsqrt(jnp.mean(yg * yg, axis=-1, keepdims=True) + EPS)
        o_ref[0, :, gi * gw:(gi + 1) * gw] = (yg * nw_ref[:, gi * gw:(gi + 1) * gw]).astype(o_ref.dtype)


def _ssd_mix(z, xbc, dtp, conv_w, conv_b, dt_bias, a_log, d_skip, norm_w):
    b, t, dw = z.shape
    ln = SSD_CHUNK
    xw = xbc.shape[2]
    pad = lambda vec: jnp.pad(vec, (0, LANES - vec.shape[0])).reshape(1, LANES)
    dsk = jnp.repeat(d_skip, SSD_HEAD).reshape(1, dw)
    prm = [conv_w, conv_b.reshape(1, xw), pad(dt_bias), pad(a_log), dsk, norm_w.reshape(1, dw)]
    full = lambda arr: pl.BlockSpec(arr.shape, lambda i, j: (0,) * arr.ndim)
    blk = lambda w: pl.BlockSpec((1, ln, w), lambda i, j: (i, j, 0))
    halo = 2 * SUBLANES
    return pl.pallas_call(
        _ssd_kernel, grid=(b, t // ln),
        in_specs=[blk(dw), blk(xw),
                  pl.BlockSpec((1, halo, xw), lambda i, j: (i, jnp.maximum(j * (ln // halo) - 1, 0), 0)),
                  blk(LANES)] + [full(x) for x in prm],
        out_specs=blk(dw), out_shape=jax.ShapeDtypeStruct((b, t, dw), BF16),
        scratch_shapes=[pltpu.VMEM((SSD_HEADS, SSD_STATE, SSD_HEAD), F32)],
        compiler_params=_cparams(("parallel", "arbitrary")), name="ssd",
    )(z, xbc, xbc, dtp, *prm)


def _mix_out1_kernel(x_ref, yc_ref, yd_ref, gw_ref, gb_ref, wc_ref, wd_ref, gpost_ref, gpre_ref,
                     wrh_ref, wrl_ref, x1_ref, h_ref, idx_ref, gate_ref):
    yc = _gelu_tanh(yc_ref[...].astype(F32))
    yc = yc * _sigmoid(_bdot(yc, gw_ref[...]) + gb_ref[...])
    y = _bdot(yc, wc_ref[...]) + _bdot(yd_ref[...], wd_ref[...])
    x1 = x_ref[...] + _rms(y, gpost_ref[...])
    x1_ref[...] = x1
    h = _rms(x1, gpre_ref[...])
    h_ref[...] = h
    hh = h.astype(BF16)
    hl = (h - hh.astype(F32)).astype(BF16)
    wrh = wrh_ref[...]
    logits = (jnp.dot(hh, wrh, preferred_element_type=F32) + jnp.dot(hl, wrh, preferred_element_type=F32)
              + jnp.dot(hh, wrl_ref[...], preferred_element_type=F32))
    lane = _iota2(logits.shape, 1)
    lane_f = lane.astype(F32)
    logits = jnp.where(lane < MOE_EXPERTS, logits, -jnp.inf)
    m1 = jnp.max(logits, axis=-1, keepdims=True)
    i1 = jnp.min(jnp.where(logits == m1, lane_f, float(LANES)), axis=-1, keepdims=True)
    rest = jnp.where(lane_f == i1, -jnp.inf, logits)
    m2 = jnp.max(rest, axis=-1, keepdims=True)
    i2 = jnp.min(jnp.where(rest == m2, lane_f, float(LANES)), axis=-1, keepdims=True)
    e2 = jnp.exp(m2 - m1)
    g1 = 1.0 / (1.0 + e2)
    g2 = e2 / (1.0 + e2)
    idx_ref[...] = jnp.where(lane == 0, i1, jnp.where(lane == 1, i2, 0.0)).astype(jnp.int32)
    gate_ref[...] = jnp.where(lane == 0, g1, jnp.where(lane == 1, g2, 0.0))


def _mix_out1(x2, yc, yd, glu_w, glu_b, wc, wd, gpost, gpre, wr, tm=512):
    n, d = x2.shape
    wr_p = jnp.pad(wr, ((0, 0), (0, LANES - wr.shape[1])))
    wrh = wr_p.astype(BF16)
    wrl = (wr_p - wrh.astype(F32)).astype(BF16)
    row = lambda w: pl.BlockSpec((tm, w), lambda i: (i, 0))
    full = lambda arr: pl.BlockSpec(arr.shape, lambda i: (0,) * arr.ndim)
    prm = [glu_w, glu_b, wc, wd, gpost, gpre, wrh, wrl]
    return pl.pallas_call(
        _mix_out1_kernel, grid=(n // tm,),
        in_specs=[row(d), row(yc.shape[1]), row(yd.shape[1])] + [full(p) for p in prm],
        out_specs=[row(d), row(d), row(LANES), row(LANES)],
        out_shape=[jax.ShapeDtypeStruct((n, d), F32), jax.ShapeDtypeStruct((n, d), F32),
                   jax.ShapeDtypeStruct((n, LANES), jnp.int32), jax.ShapeDtypeStruct((n, LANES), F32)],
        compiler_params=_cparams(("parallel",)), name="mix_out1",
    )(x2, yc, yd, *prm)


GATHER_UNROLL = 8


def _gather_rows(n_rows, make_copy):
    def body(j, carry):
        for q in range(GATHER_UNROLL):
            make_copy(j * GATHER_UNROLL + q).start(priority=q % 2)
        return carry

    lax.fori_loop(0, n_rows // GATHER_UNROLL, body, 0)


def _moe_kernel(nf, be_ref, tok_ref, dst_ref, nact_ref, h_hbm, wg_ref, wu_ref, wd_ref, y_hbm, buf_ref,
                xb_ref, gsem, ssem):
    i = pl.program_id(0)
    f = pl.program_id(1)
    n_blocks = pl.num_programs(0)
    tm = buf_ref.shape[1]
    nact = nact_ref[0]
    active = i < nact
    slot = lax.rem(i, 2)
    other = 1 - slot
    xs = lambda sl: buf_ref.at[sl]
    yb = lambda sl: buf_ref.at[2 + sl]

    def gather_copy(block, sl, r):
        tok = tok_ref[block * tm + r]
        return pltpu.make_async_copy(h_hbm.at[pl.ds(tok, 1)], buf_ref.at[sl, pl.ds(r, 1)], gsem.at[sl])

    def scatter_copy(block, sl, r):
        dst = dst_ref[block * tm + r]
        return pltpu.make_async_copy(buf_ref.at[2 + sl, pl.ds(r, 1)], y_hbm.at[pl.ds(dst, 1)], ssem.at[sl])

    def wait_rows(sem_slot_ref, buf):
        pltpu.make_async_copy(h_hbm.at[pl.ds(0, tm)], buf, sem_slot_ref).wait()

    @pl.when(f == 0)
    def _():
        @pl.when(i == 0)
        def _():
            _gather_rows(tm, functools.partial(gather_copy, 0, 0))
            buf_ref[3] = jnp.zeros(buf_ref.shape[1:], F32)
            n_real = y_hbm.shape[0] - 2 * tm
            for half in range(2):
                init = pltpu.make_async_copy(yb(1), y_hbm.at[pl.ds(n_real + half * tm, tm)], ssem.at[0])
                init.start()
                init.wait()

        @pl.when(i <= nact)
        def _():
            wait_rows(gsem.at[slot], xs(slot))

        @pl.when(jnp.logical_and(i >= 1, i <= nact))
        def _():
            wait_rows(ssem.at[slot], yb(slot))

        @pl.when(i == nact)
        def _():
            _gather_rows(tm, functools.partial(scatter_copy, i - 1, other))
            wait_rows(ssem.at[other], yb(other))

        @pl.when(active)
        def _():
            xb_ref[...] = buf_ref[slot].astype(BF16)
            buf_ref[2 + slot] = jnp.zeros(buf_ref.shape[1:], F32)

    @pl.when(active)
    def _():
        x = xb_ref[...]
        nch = wg_ref.shape[2] // MXU_TILE
        rows_f = tm // nf
        base = pl.multiple_of(f * rows_f, SUBLANES)
        prev = jnp.where(i == 0, n_blocks - 1, i - 1)
        for c in range(nch):
            for r in range(rows_f * c // nch, rows_f * (c + 1) // nch):
                gather_copy(i + 1, other, base + r).start(priority=1)
                scatter_copy(prev, other, base + r).start(priority=1)
            cols = slice(c * MXU_TILE, (c + 1) * MXU_TILE)
            gate = jnp.dot(x, wg_ref[0, :, cols], preferred_element_type=F32)
            up = jnp.dot(x, wu_ref[0, :, cols], preferred_element_type=F32)
            buf_ref[2 + slot] += jnp.dot((_silu(gate) * up).astype(BF16), wd_ref[0, cols, :],
                                         preferred_element_type=F32)


def _moe_experts(h, block_expert, slot_tok, slot_dst, nact, n_rows, wg, wu, wd, tf=7 * MXU_TILE):
    n, d = h.shape
    tm = MOE_ROWS
    n_blocks = slot_tok.shape[0] // tm
    ff = wg.shape[2]
    nf = ff // tf
    wmap = lambda i, j, be, tok, dst, na: (be[i], 0, j)
    grid_spec = pltpu.PrefetchScalarGridSpec(
        num_scalar_prefetch=4, grid=(n_blocks, nf),
        in_specs=[pl.BlockSpec(memory_space=pl.ANY),
                  pl.BlockSpec((1, d, tf), wmap), pl.BlockSpec((1, d, tf), wmap),
                  pl.BlockSpec((1, tf, d), lambda i, j, be, tok, dst, na: (be[i], j, 0))],
        out_specs=pl.BlockSpec(memory_space=pl.ANY),
        scratch_shapes=[pltpu.VMEM((4, tm, d), F32), pltpu.VMEM((tm, d), BF16),
                        pltpu.SemaphoreType.DMA((2,)), pltpu.SemaphoreType.DMA((2,))])
    return pl.pallas_call(
        functools.partial(_moe_kernel, nf), grid_spec=grid_spec,
        out_shape=jax.ShapeDtypeStruct((n_rows, d), F32),
        compiler_params=pltpu.CompilerParams(dimension_semantics=("arbitrary", "arbitrary"),
                                             vmem_limit_bytes=VMEM_LIMIT, disable_bounds_checks=True),
        name="moe_experts",
    )(block_expert, slot_tok, slot_dst, nact, h, wg, wu, wd)


def _combine_kernel(x_ref, y0_ref, y1_ref, gate_ref, gpost_ref, o_ref):
    gates = gate_ref[...]
    y = gates[:, 0:1] * y0_ref[...] + gates[:, 1:2] * y1_ref[...]
    o_ref[...] = x_ref[...] + _rms(y, gpost_ref[...])


def _moe_combine(x1, y, gates, gpost, tm=512):
    n, d = x1.shape
    nt = n // tm
    return pl.pallas_call(
        _combine_kernel, grid=(nt,),
        in_specs=[pl.BlockSpec((tm, d), lambda i: (i, 0)), pl.BlockSpec((tm, d), lambda i: (i, 0)),
                  pl.BlockSpec((tm, d), lambda i: (nt + i, 0)),
                  pl.BlockSpec((tm, LANES), lambda i: (i, 0)), pl.BlockSpec((1, d), lambda i: (0, 0))],
        out_specs=pl.BlockSpec((tm, d), lambda i: (i, 0)),
        out_shape=jax.ShapeDtypeStruct((n, d), F32),
        compiler_params=_cparams(("parallel",)), name="moe_combine",
    )(x1, y, y, gates, gpost)


def _moe_plan(idx, n):
    tm = MOE_ROWS
    flat_e = idx[:, :2].reshape(-1)
    onehot = (flat_e[:, None] == jnp.arange(MOE_EXPERTS, dtype=jnp.int32)[None, :]).astype(jnp.int32)
    csum = jnp.cumsum(onehot, axis=0)
    counts = csum[-1]
    rank = jnp.sum((csum - onehot) * onehot, axis=1)
    padded = (counts + tm - 1) // tm * tm
    pend = jnp.cumsum(padded)
    pstart = pend - padded
    dest = (jnp.sum(onehot * pstart[None, :], axis=1) + rank).astype(jnp.int32)
    n_blocks = (2 * n) // tm + MOE_EXPERTS + 1
    n_slots = n_blocks * tm
    slot_pair = jnp.full((n_slots,), -1, jnp.int32).at[dest].set(
        jnp.arange(2 * n, dtype=jnp.int32), unique_indices=True)
    real = slot_pair >= 0
    slot_tok = jnp.where(real, slot_pair // 2, 0)
    s_id = jnp.arange(n_slots, dtype=jnp.int32)
    slot_dst = jnp.where(real, slot_pair % 2 * n + slot_pair // 2, 2 * n + (s_id // tm) % 2 * tm + s_id % tm)
    block_start = jnp.arange(n_blocks, dtype=jnp.int32) * tm
    block_expert = jnp.minimum(jnp.sum((block_start[:, None] >= pend[None, :]).astype(jnp.int32), axis=1),
                               MOE_EXPERTS - 1)
    nact = (pend[-1] // tm).astype(jnp.int32).reshape(1)
    return block_expert, slot_tok, slot_dst, nact, 2 * n + 2 * tm


def kernel(x, l0_norm_pre_mix, l0_w_in, l0_rwkv_mu, l0_rwkv_w0, l0_rwkv_w2, l0_rwkv_a0, l0_rwkv_a2, l0_rwkv_g2, l0_rwkv_k_k, l0_rwkv_k_a, l0_rwkv_r_k, l0_rwkv_ln_w, l0_rwkv_ln_b, l0_gmlp_ln_w, l0_gmlp_ln_b, l0_gmlp_ws, l0_gmlp_bs, l0_w_out, l0_norm_post_mix, l0_norm_pre_ffn, l0_ffn_w_gate, l0_ffn_w_up, l0_ffn_w_down, l0_norm_post_ffn, l1_norm_pre_mix, l1_w_in, l1_s5_a_re, l1_s5_a_im, l1_s5_log_dt, l1_s5_b_re, l1_s5_b_im, l1_s5_c_re, l1_s5_c_im, l1_s5_d, l1_s5_glu_w, l1_s5_glu_b, l1_m2_conv_w, l1_m2_conv_b, l1_m2_dt_bias, l1_m2_a_log, l1_m2_d, l1_m2_norm_w, l1_w_out, l1_norm_post_mix, l1_norm_pre_ffn, l1_moe_router, l1_moe_w_gate, l1_moe_w_up, l1_moe_w_down, l1_norm_post_ffn):
    b, t, d = x.shape
    n = b * t
    x2 = x.reshape(n, d)
    row = lambda vec: vec.reshape(1, -1)

    aw = l0_rwkv_w0.shape[0]
    heads = aw // RWKV_HEAD
    lw_, la_, lg_ = l0_rwkv_w2.shape[0], l0_rwkv_a2.shape[0], l0_rwkv_g2.shape[0]
    a_in = 3 * aw + lw_ + la_ + lg_
    padc = lambda m, wdt: jnp.pad(m, ((0, 0), (0, LANES - wdt)))
    o = 3 * aw
    w_a = jnp.concatenate([l0_w_in[:, :o], padc(l0_w_in[:, o:o + lw_], lw_),
                           padc(l0_w_in[:, o + lw_:o + lw_ + la_], la_),
                           padc(l0_w_in[:, o + lw_ + la_:a_in], lg_)], axis=1).astype(BF16)
    w_b = l0_w_in[:, a_in:].astype(BF16)
    p_a, p_b = _norm_proj(x2, l0_norm_pre_mix, [w_a, w_b], [F32, BF16])
    padv = lambda vec, wdt: jnp.pad(vec, (0, LANES - wdt))
    mu = l0_rwkv_mu
    mu_p = jnp.concatenate([mu[:o], padv(mu[o:o + lw_], lw_), padv(mu[o + lw_:o + lw_ + la_], la_),
                            padv(mu[o + lw_ + la_:], lg_)])
    padr = lambda m: jnp.pad(m, ((0, LANES - m.shape[0]), (0, 0))).astype(BF16)
    hid = jnp.arange(LANES, dtype=jnp.int32) // RWKV_HEAD
    gsum = (hid[:, None] == hid[None, :]).astype(BF16)
    rwkv_prm = [row(mu_p), row(l0_rwkv_w0), padr(l0_rwkv_w2), row(l0_rwkv_a0), padr(l0_rwkv_a2),
                padr(l0_rwkv_g2), row(l0_rwkv_k_k), row(l0_rwkv_k_a), row(l0_rwkv_r_k),
                row(l0_rwkv_ln_w), row(l0_rwkv_ln_b), gsum]
    ya = _rwkv_mix(p_a.reshape(b, t, -1), rwkv_prm, heads)
    yb = _gmlp_mix(p_b.reshape(b, t, -1), l0_gmlp_ln_w, l0_gmlp_ln_b, l0_gmlp_ws, l0_gmlp_bs)
    wo = l0_w_out.astype(BF16)
    x2 = _mix_out0(x2, ya.reshape(n, -1), yb.reshape(n, -1), wo[:aw], wo[aw:], row(l0_norm_post_mix))
    x2 = _ffn(x2, row(l0_norm_pre_ffn), l0_ffn_w_gate.astype(BF16), l0_ffn_w_up.astype(BF16),
              l0_ffn_w_down.astype(BF16), row(l0_norm_post_ffn))

    cw = l1_s5_d.shape[0]
    dw = l1_m2_norm_w.shape[0]
    xw = l1_m2_conv_w.shape[1]
    nh = l1_m2_dt_bias.shape[0]
    w1 = l1_w_in
    w_parts = [w1[:, :cw], w1[:, cw:cw + dw], w1[:, cw + dw:cw + dw + xw],
               padc(w1[:, cw + dw + xw:], nh)]
    u_c, z_d, xbc, dtp = _norm_proj(x2, l1_norm_pre_mix, [w.astype(BF16) for w in w_parts],
                                    [BF16, BF16, BF16, F32])
    s5_w = _s5_weights(l1_s5_a_re, l1_s5_a_im, l1_s5_log_dt, l1_s5_b_re, l1_s5_b_im, l1_s5_c_re,
                       l1_s5_c_im, l1_s5_d, S5_CHUNK)
    yc = _s5_core(u_c.reshape(b, t, cw), s5_w)
    yd = _ssd_mix(z_d.reshape(b, t, dw), xbc.reshape(b, t, xw), dtp.reshape(b, t, LANES),
                  l1_m2_conv_w, l1_m2_conv_b, l1_m2_dt_bias, l1_m2_a_log, l1_m2_d, l1_m2_norm_w)
    wo1 = l1_w_out.astype(BF16)
    x1, h, idx, gates = _mix_out1(x2, yc.reshape(n, cw), yd.reshape(n, dw), l1_s5_glu_w.astype(BF16),
                                  row(l1_s5_glu_b), wo1[:cw], wo1[cw:], row(l1_norm_post_mix),
                                  row(l1_norm_pre_ffn), l1_moe_router)
    block_expert, slot_tok, slot_dst, nact, n_rows = _moe_plan(idx, n)
    ys = _moe_experts(h, block_expert, slot_tok, slot_dst, nact, n_rows, l1_moe_w_gate.astype(BF16),
                      l1_moe_w_up.astype(BF16), l1_moe_w_down.astype(BF16))
    out = _moe_combine(x1, ys, gates, row(l1_norm_post_ffn))
    return out.reshape(b, t, d)
```

```python
import functools

import jax
import jax.numpy as jnp
from jax import lax
from jax.experimental import pallas as pl
from jax.experimental.pallas import tpu as pltpu

F32 = jnp.float32
BF16 = jnp.bfloat16

EPS = 1e-6
RWKV_GN_EPS = 64e-5
RWKV_HEAD = 64
RWKV_CHUNK = 64
GMLP_CHUNK = 128
GMLP_GROUPS = 4
S5_GROUP_CH = 16
S5_STATE = 64
S5_CHUNK = 64
SSD_HEAD = 64
SSD_HEADS = 8
SSD_GROUPS = 2
SSD_STATE = 128
SSD_CONV = 4
SSD_CHUNK = 128
MOE_EXPERTS = 8
MOE_ROWS = 512
MXU_TILE = 256
LANES = 128
SUBLANES = 8
VMEM_LIMIT = 56 * 1024 * 1024


def _cparams(sem):
    return pltpu.CompilerParams(dimension_semantics=sem, vmem_limit_bytes=VMEM_LIMIT)


def _bdot(a, b):
    return jnp.dot(a.astype(BF16), b.astype(BF16), preferred_element_type=F32)


def _bdot_nt(a, b):
    return lax.dot_general(a.astype(BF16), b.astype(BF16), (((1,), (1,)), ((), ())),
                           preferred_element_type=F32)


def _bdot_tn(a, b):
    return lax.dot_general(a.astype(BF16), b.astype(BF16), (((0,), (0,)), ((), ())),
                           preferred_element_type=F32)


def _split3(x):
    h = x.astype(BF16)
    r1 = x - h.astype(F32)
    m = r1.astype(BF16)
    l = (r1 - m.astype(F32)).astype(BF16)
    return h, m, l


def _dot_x_exact(x, e):
    h, m, l = _split3(x)
    e = e.astype(BF16)
    return (jnp.dot(h, e, preferred_element_type=F32) + jnp.dot(m, e, preferred_element_type=F32)
            + jnp.dot(l, e, preferred_element_type=F32))


def _dot_exact_x(e, x):
    h, m, l = _split3(x)
    e = e.astype(BF16)
    return (jnp.dot(e, h, preferred_element_type=F32) + jnp.dot(e, m, preferred_element_type=F32)
            + jnp.dot(e, l, preferred_element_type=F32))


def _rms(x, g):
    return x * lax.rsqrt(jnp.mean(x * x, axis=-1, keepdims=True) + EPS) * g


def _sigmoid(x):
    return 1.0 / (1.0 + jnp.exp(-x))


def _silu(x):
    return x * _sigmoid(x)


def _softplus(x):
    return jnp.maximum(x, 0.0) + jnp.log(1.0 + jnp.exp(-jnp.abs(x)))


def _gelu_tanh(x):
    return 0.5 * x * (1.0 + jnp.tanh(0.7978845608028654 * (x + 0.044715 * x * x * x)))


def _iota2(shape, dim):
    return lax.broadcasted_iota(jnp.int32, shape, dim)


def _norm_proj_kernel(n_out, x_ref, g_ref, *refs):
    w_refs = refs[:n_out]
    o_refs = refs[n_out:]
    xn = _rms(x_ref[...], g_ref[...]).astype(BF16)
    for w_ref, o_ref in zip(w_refs, o_refs):
        o_ref[...] = jnp.dot(xn, w_ref[...], preferred_element_type=F32).astype(o_ref.dtype)


def _norm_proj(x2, g, ws, dtypes, tm=512):
    n, d = x2.shape
    in_specs = [pl.BlockSpec((tm, d), lambda i: (i, 0)), pl.BlockSpec((1, d), lambda i: (0, 0))]
    in_specs += [pl.BlockSpec(w.shape, lambda i: (0, 0)) for w in ws]
    out_specs = [pl.BlockSpec((tm, w.shape[1]), lambda i: (i, 0)) for w in ws]
    out_shape = [jax.ShapeDtypeStruct((n, w.shape[1]), dt) for w, dt in zip(ws, dtypes)]
    return pl.pallas_call(
        functools.partial(_norm_proj_kernel, len(ws)),
        grid=(n // tm,), in_specs=in_specs, out_specs=out_specs, out_shape=out_shape,
        compiler_params=_cparams(("parallel",)), name="norm_proj",
    )(x2, g.reshape(1, d), *ws)


PRE_NAMES = ("v", "g", "bonus", "rt", "kt", "bt", "at", "bh", "kh")


def _rwkv_kernel(heads, nb, p_ref, pp_ref, mu_ref, w0_ref, w2_ref, a0_ref, a2_ref, g2_ref, kk_ref,
                 ka_ref, rk_ref, lnw_ref, lnb_ref, gs_ref, o_ref, z_ref, pre_ref, wl_ref):
    c = pl.program_id(1)
    ln = RWKV_CHUNK
    hd = RWKV_HEAD
    aw = heads * hd

    @pl.when(c == 0)
    def _():
        z_ref[...] = jnp.zeros_like(z_ref)
        pre_ref[...] = jnp.zeros_like(pre_ref)
        wl_ref[...] = jnp.zeros_like(wl_ref)

    tril_f = jnp.where(_iota2((ln, ln), 0) >= _iota2((ln, ln), 1), 1.0, 0.0)
    rows = _iota2((ln, 1), 0)
    gs_tile = gs_ref[...]

    def gs(x):
        nt = x.shape[1] // LANES
        stacked = jnp.concatenate([x[:, j * LANES:(j + 1) * LANES] for j in range(nt)], axis=0)
        red = _dot_x_exact(stacked, gs_tile)
        return jnp.concatenate([red[j * ln:(j + 1) * ln] for j in range(nt)], axis=1)

    pre_idx = {nm: idx for idx, nm in enumerate(PRE_NAMES)}
    pending = []

    def prep_steps():
        for bi in range(nb):
            p = p_ref[bi]
            prev = jnp.where(c == 0, 0.0, pp_ref[bi][SUBLANES - 1:SUBLANES, :])
            ps = jnp.where(rows == 0, prev, pltpu.roll(p, 1, axis=0))
            pm = p + (ps - p) * mu_ref[...]
            r = pm[:, 0:aw]
            k = pm[:, aw:2 * aw]
            v = pm[:, 2 * aw:3 * aw]
            xw = pm[:, 3 * aw:3 * aw + LANES]
            xa = pm[:, 3 * aw + LANES:3 * aw + 2 * LANES]
            xg = pm[:, 3 * aw + 2 * LANES:3 * aw + 3 * LANES]
            yield
            w = w0_ref[...] + _bdot(jnp.tanh(xw), w2_ref[...])
            a = _sigmoid(a0_ref[...] + _bdot(xa, a2_ref[...]))
            g = _bdot(_sigmoid(xg), g2_ref[...])
            yield
            w = -_softplus(-w) - 0.5
            lw = -jnp.exp(w)
            kk = k * kk_ref[...]
            kk_ss = gs(kk * kk)
            yield
            cs = _dot_exact_x(tril_f, lw)
            kk = kk / jnp.maximum(jnp.sqrt(kk_ss), 1e-12)
            kmod = k * (1.0 + (a - 1.0) * ka_ref[...])
            yield
            bonus = gs(r * kmod * rk_ref[...])
            bvec = kk * a
            cs_last = cs[ln - 1:ln, :]
            encs = jnp.exp(-cs)
            yield
            dec_end = jnp.exp(cs_last - cs)
            nxt = dict(v=v, g=g, bonus=bonus, rt=r * jnp.exp(cs), kt=kmod * encs, bt=bvec * encs,
                       at=-kk * jnp.exp(cs - lw), bh=bvec * dec_end, kh=kmod * dec_end)
            pending.append((bi, nxt, jnp.broadcast_to(jnp.exp(cs_last), (SUBLANES, aw))))
            yield

    prep = prep_steps()
    tick = lambda: next(prep, None)

    lane = _iota2((ln, LANES), 1)
    lane_in = jnp.where(lane >= hd, lane - hd, lane)
    trow = _iota2((ln, LANES), 0)
    left = lane < hd
    tril_p = lane_in <= trow
    stril_p = lane_in < trow
    eye_p = lane_in == trow
    eye_pf = jnp.where(eye_p, 1.0, 0.0)

    def bd(x):
        xb = x.astype(BF16)
        zero = jnp.zeros_like(xb)
        return jnp.concatenate([jnp.where(left, xb, zero), jnp.where(left, zero, xb)], axis=0)

    def dot(a, b):
        return jnp.dot(a.astype(BF16), b, preferred_element_type=F32)

    npair = heads // 2
    pairs = [(bi, j) for bi in range(nb) for j in range(npair)]

    class _Tiles:
        def __init__(self, name):
            self.idx = pre_idx[name]

        def __getitem__(self, i):
            bi, j = pairs[i]
            return pre_ref[self.idx, bi, :, j * LANES:(j + 1) * LANES]

    at, rt, bt, kt, vv, bh, kh = (_Tiles(n) for n in ("at", "rt", "bt", "kt", "v", "bh", "kh"))
    wl = [wl_ref[bi, 0:1, j * LANES:(j + 1) * LANES] for bi, j in pairs]
    z_all = z_ref[...]
    zs = [z_all[bi, j] for bi, j in pairs]
    npr = range(len(pairs))
    lhs = [jnp.concatenate([at[i], rt[i]], axis=0).astype(BF16) for i in npr]
    abk = [lax.dot_general(lhs[i], jnp.concatenate([bd(bt[i]), bd(kt[i])], axis=0),
                           (((1,), (1,)), ((), ())), preferred_element_type=F32) for i in npr]
    ab = [abk[i][:, :LANES] for i in npr]
    ak = [abk[i][:, LANES:] for i in npr]
    tick()
    nmat = [jnp.where(stril_p, ab[i][:ln], 0.0) for i in npr]
    tinv = [eye_pf + nmat[i] for i in npr]
    npow = [dot(nmat[i], bd(nmat[i])) for i in npr]
    tick()
    for step in range(5):
        bdn = [bd(npow[i]) for i in npr]
        if step < 4:
            both = [dot(jnp.concatenate([tinv[i], npow[i]], axis=0), bdn[i]) for i in npr]
            tinv = [tinv[i] + both[i][:ln] for i in npr]
            npow = [both[i][ln:] for i in npr]
        else:
            tinv = [tinv[i] + dot(tinv[i], bdn[i]) for i in npr]
        tick()
    bdv = [bd(vv[i]) for i in npr]
    bdz = [bd(zs[i]) for i in npr]
    xmat = [dot(jnp.concatenate([jnp.where(stril_p, ak[i][:ln], 0.0), at[i]], axis=1),
                jnp.concatenate([bdv[i], bdz[i]], axis=0)) for i in npr]
    tick()
    u = [dot(tinv[i], bd(xmat[i])) for i in npr]
    tick()
    ys_p = [dot(jnp.concatenate([rt[i], jnp.where(tril_p, ab[i][ln:], 0.0),
                                 jnp.where(tril_p, ak[i][ln:], 0.0)], axis=1),
                jnp.concatenate([bdz[i], bd(u[i]), bdv[i]], axis=0)) for i in npr]
    tick()
    cross = [_bdot_tn(jnp.concatenate([bh[i], kh[i]], axis=0), jnp.concatenate([u[i], vv[i]], axis=0))
             for i in npr]
    tick()
    z_new = []
    for i in npr:
        dg = jnp.where(eye_p, wl[i], 0.0)
        wl_i = jnp.sum(jnp.where(left, dg, 0.0), axis=1, keepdims=True)
        wl_j = jnp.sum(jnp.where(left, 0.0, dg), axis=1, keepdims=True)
        z_new.append(jnp.where(left, wl_i, wl_j) * zs[i] + jnp.where(left, cross[i][:ln], cross[i][ln:]))
    z_ref[...] = jnp.stack(z_new, axis=0).reshape(z_ref.shape)

    tick()
    inv = 1.0 / hd
    for bi in range(nb):
        y = jnp.concatenate(ys_p[bi * npair:(bi + 1) * npair], axis=1)
        mean = gs(y) * inv
        d = y - mean
        var = gs(d * d) * inv
        yn = d * lax.rsqrt(var + RWKV_GN_EPS) * lnw_ref[...] + lnb_ref[...]
        o_ref[bi] = ((yn + pre_ref[pre_idx["bonus"], bi] * pre_ref[pre_idx["v"], bi])
                     * pre_ref[pre_idx["g"], bi]).astype(o_ref.dtype)
        tick()
    for _ in prep:
        pass
    for bi, nxt, wl_next in pending:
        for nm, idx in pre_idx.items():
            pre_ref[idx, bi] = nxt[nm]
        wl_ref[bi] = wl_next


def _rwkv_mix(p_a, prm, heads, nb=2):
    b, t, cin = p_a.shape
    aw = heads * RWKV_HEAD
    ln = RWKV_CHUNK
    nc = t // ln
    sub = ln // SUBLANES
    full = lambda arr: pl.BlockSpec(arr.shape, lambda i, j: (0,) * arr.ndim)
    in_specs = [pl.BlockSpec((nb, ln, cin), lambda i, j: (i, jnp.minimum(j, nc - 1), 0)),
                pl.BlockSpec((nb, SUBLANES, cin),
                             lambda i, j: (i, jnp.maximum(jnp.minimum(j, nc - 1) * sub - 1, 0), 0))]
    in_specs += [full(x) for x in prm]
    return pl.pallas_call(
        functools.partial(_rwkv_kernel, heads, nb),
        grid=(b // nb, nc + 1), in_specs=in_specs,
        out_specs=pl.BlockSpec((nb, ln, aw), lambda i, j: (i, jnp.maximum(j - 1, 0), 0)),
        out_shape=jax.ShapeDtypeStruct((b, t, aw), BF16),
        scratch_shapes=[pltpu.VMEM((nb, heads // 2, RWKV_HEAD, 2 * RWKV_HEAD), F32),
                        pltpu.VMEM((len(PRE_NAMES), nb, ln, aw), F32),
                        pltpu.VMEM((nb, SUBLANES, aw), F32)],
        compiler_params=_cparams(("parallel", "arbitrary")), name="rwkv7",
    )(p_a, p_a, *prm)


def _gmlp_kernel(p_ref, lnw_ref, lnb_ref, ws_ref, bs_ref, o_ref):
    ln = GMLP_CHUNK
    bw = p_ref.shape[2] // 2
    gd = bw // GMLP_GROUPS
    tril = _iota2((ln, ln), 0) >= _iota2((ln, ln), 1)
    ws_c = [jnp.where(tril, ws_ref[gi], 0.0).astype(BF16) for gi in range(GMLP_GROUPS)]
    for ci in range(p_ref.shape[1] // ln):
        x = _gelu_tanh(p_ref[0, ci * ln:(ci + 1) * ln, :].astype(F32))
        for gi in range(GMLP_GROUPS):
            u = x[:, gi * gd:(gi + 1) * gd]
            v = x[:, bw + gi * gd:bw + (gi + 1) * gd]
            mean = jnp.mean(v, axis=-1, keepdims=True)
            d = v - mean
            var = jnp.mean(d * d, axis=-1, keepdims=True)
            vn = d * lax.rsqrt(var + EPS) * lnw_ref[gi:gi + 1, :] + lnb_ref[gi:gi + 1, :]
            s = jnp.dot(ws_c[gi], vn.astype(BF16), preferred_element_type=F32) + bs_ref[gi]
            o_ref[0, ci * ln:(ci + 1) * ln, gi * gd:(gi + 1) * gd] = (u * s).astype(o_ref.dtype)


def _gmlp_mix(p_b, ln_w, ln_b, ws, bs, rows=512):
    b, t, cin = p_b.shape
    bw = cin // 2
    gd = bw // GMLP_GROUPS
    bs_b = jnp.broadcast_to(bs[:, :, None], (GMLP_GROUPS, GMLP_CHUNK, gd))
    full = lambda arr: pl.BlockSpec(arr.shape, lambda i, j: (0,) * arr.ndim)
    return pl.pallas_call(
        _gmlp_kernel, grid=(b, t // rows),
        in_specs=[pl.BlockSpec((1, rows, cin), lambda i, j: (i, j, 0)),
                  full(ln_w), full(ln_b), full(ws), full(bs_b)],
        out_specs=pl.BlockSpec((1, rows, bw), lambda i, j: (i, j, 0)),
        out_shape=jax.ShapeDtypeStruct((b, t, bw), BF16),
        compiler_params=_cparams(("parallel", "parallel")), name="gmlp",
    )(p_b, ln_w, ln_b, ws, bs_b)


def _mix_out0_kernel(x_ref, ya_ref, yb_ref, wa_ref, wb_ref, g_ref, o_ref):
    y = _bdot(ya_ref[...], wa_ref[...]) + _bdot(yb_ref[...], wb_ref[...])
    o_ref[...] = x_ref[...] + _rms(y, g_ref[...])


def _mix_out0(x2, ya, yb, wa, wb, g, tm=512):
    n, d = x2.shape
    row = lambda arr: pl.BlockSpec((tm, arr.shape[1]), lambda i: (i, 0))
    full = lambda arr: pl.BlockSpec(arr.shape, lambda i: (0,) * arr.ndim)
    return pl.pallas_call(
        _mix_out0_kernel, grid=(n // tm,),
        in_specs=[row(x2), row(ya), row(yb), full(wa), full(wb), full(g)],
        out_specs=row(x2), out_shape=jax.ShapeDtypeStruct((n, d), F32),
        compiler_params=_cparams(("parallel",)), name="mix_out0",
    )(x2, ya, yb, wa, wb, g)


def _ffn_kernel(fc, x_ref, gpre_ref, wg_ref, wu_ref, wd_ref, gpost_ref, o_ref):
    h = _rms(x_ref[...], gpre_ref[...]).astype(BF16)
    acc = None
    for c in range(wg_ref.shape[1] // fc):
        cols = slice(c * fc, (c + 1) * fc)
        gate = jnp.dot(h, wg_ref[:, cols], preferred_element_type=F32)
        up = jnp.dot(h, wu_ref[:, cols], preferred_element_type=F32)
        part = jnp.dot((_silu(gate) * up).astype(BF16), wd_ref[cols, :], preferred_element_type=F32)
        acc = part if acc is None else acc + part
    o_ref[...] = x_ref[...] + _rms(acc, gpost_ref[...])


def _ffn(x2, gpre, wg, wu, wd, gpost, tm=512, fc=MXU_TILE):
    n, d = x2.shape
    full = lambda arr: pl.BlockSpec(arr.shape, lambda i: (0,) * arr.ndim)
    return pl.pallas_call(
        functools.partial(_ffn_kernel, fc), grid=(n // tm,),
        in_specs=[pl.BlockSpec((tm, d), lambda i: (i, 0)), full(gpre), full(wg), full(wu), full(wd),
                  full(gpost)],
        out_specs=pl.BlockSpec((tm, d), lambda i: (i, 0)),
        out_shape=jax.ShapeDtypeStruct((n, d), F32),
        compiler_params=_cparams(("parallel",)), name="ffn",
    )(x2, gpre, wg, wu, wd, gpost)


def _s5_kernel(nc, nb, u_ref, tap_ref, wsr_ref, wsi_ref, wcr_ref, wci_ref, alr_ref, ali_ref, d_ref, o_ref,
               toep_ref):
    u = u_ref[0]
    taps = tap_ref[0]
    ch = taps.shape[0]
    lane = _iota2(taps.shape, 1)
    for s in range(taps.shape[1] // ch):
        blk = taps if s == 0 else jnp.where(lane >= ch * s, pltpu.roll(taps, ch * s, axis=1), 0.0)
        toep_ref[ch * s:ch * (s + 1), :] = blk.astype(BF16)
    y = jnp.dot(u, toep_ref[...], preferred_element_type=F32)
    xer = jnp.dot(u, wsr_ref[0], preferred_element_type=F32)
    xei = jnp.dot(u, wsi_ref[0], preferred_element_type=F32)
    alr = alr_ref[0]
    ali = ali_ref[0]
    cr = jnp.zeros((nb, xer.shape[1]), F32)
    ci = jnp.zeros((nb, xer.shape[1]), F32)
    prs, pis = [], []
    for c in range(nc):
        prs.append(cr)
        pis.append(ci)
        er = xer[c * nb:(c + 1) * nb]
        ei = xei[c * nb:(c + 1) * nb]
        cr, ci = alr * cr - ali * ci + er, alr * ci + ali * cr + ei
    pr = jnp.concatenate(prs, axis=0)
    pi = jnp.concatenate(pis, axis=0)
    y = y + _bdot(pr, wcr_ref[0]) + _bdot(pi, wci_ref[0])
    o_ref[0] = (y + d_ref[0] * u.astype(F32)).astype(o_ref.dtype)


def _s5_weights(a_re, a_im, log_dt, b_re, b_im, c_re, c_im, d_skip, ln):
    g, st = a_re.shape
    ch = b_re.shape[2]
    dt = jnp.exp(log_dt)[:, None]
    lr, li = a_re, a_im
    tau = jnp.arange(ln + 1, dtype=F32)[:, None, None]
    mag = jnp.exp(lr[None] * dt[None] * tau)
    pw_r = mag * jnp.cos(li[None] * dt[None] * tau)
    pw_i = mag * jnp.sin(li[None] * dt[None] * tau)
    ab_r, ab_i = pw_r[1], pw_i[1]
    nr, ni = ab_r - 1.0, ab_i
    den = lr * lr + li * li
    fr, fi = (nr * lr + ni * li) / den, (ni * lr - nr * li) / den
    bb_r = fr[..., None] * b_re - fi[..., None] * b_im
    bb_i = fr[..., None] * b_im + fi[..., None] * b_re
    cp_r = c_re[None] * pw_r[:ln, :, None, :] - c_im[None] * pw_i[:ln, :, None, :]
    cp_i = c_re[None] * pw_i[:ln, :, None, :] + c_im[None] * pw_r[:ln, :, None, :]
    hp = lax.Precision.HIGHEST
    taps = (jnp.einsum('tgcp,gpd->gdtc', cp_r, bb_r, precision=hp)
            - jnp.einsum('tgcp,gpd->gdtc', cp_i, bb_i, precision=hp))
    taps = taps.reshape(g, ch, ln * ch)
    rev_r, rev_i = pw_r[:ln][::-1], pw_i[:ln][::-1]
    ws_r = rev_r[..., None] * bb_r[None] - rev_i[..., None] * bb_i[None]
    ws_i = rev_r[..., None] * bb_i[None] + rev_i[..., None] * bb_r[None]
    ws_r = ws_r.transpose(1, 0, 3, 2).reshape(g, ln * ch, st)
    ws_i = ws_i.transpose(1, 0, 3, 2).reshape(g, ln * ch, st)
    q_r, q_i = pw_r[1:ln + 1], pw_i[1:ln + 1]
    wc_r = c_re[None] * q_r[:, :, None, :] - c_im[None] * q_i[:, :, None, :]
    wc_i = -(c_re[None] * q_i[:, :, None, :] + c_im[None] * q_r[:, :, None, :])
    wc_r = wc_r.transpose(1, 3, 0, 2).reshape(g, st, ln * ch)
    wc_i = wc_i.transpose(1, 3, 0, 2).reshape(g, st, ln * ch)
    al_r = pw_r[ln].reshape(g, 1, st)
    al_i = pw_i[ln].reshape(g, 1, st)
    d_t = jnp.tile(d_skip.reshape(g, 1, ch), (1, ln, 1)).reshape(g, 1, ln * ch)
    return (taps, ws_r.astype(BF16), ws_i.astype(BF16), wc_r.astype(BF16),
            wc_i.astype(BF16), al_r, al_i, d_t)


def _s5_core(u, weights):
    b, t, cw = u.shape
    ln, ch = S5_CHUNK, S5_GROUP_CH
    g = cw // ch
    nc = t // ln
    ug = u.astype(BF16).reshape(b, nc, ln, g, ch).transpose(3, 1, 0, 2, 4).reshape(g, nc * b, ln * ch)
    per_g = lambda arr: pl.BlockSpec((1,) + arr.shape[1:], lambda i: (i, 0, 0))
    yg = pl.pallas_call(
        functools.partial(_s5_kernel, nc, b), grid=(g,),
        in_specs=[per_g(ug)] + [per_g(w) for w in weights],
        out_specs=per_g(ug), out_shape=jax.ShapeDtypeStruct(ug.shape, BF16),
        scratch_shapes=[pltpu.VMEM((ln * ch, ln * ch), BF16)],
        compiler_params=_cparams(("parallel",)), name="s5",
    )(ug, *weights)
    return yg.reshape(g, nc, b, ln, ch).transpose(2, 1, 3, 0, 4).reshape(b, t, cw)


def _ssd_kernel(z_ref, xbc_ref, xp_ref, dt_ref, cw_ref, cb_ref, dtb_ref, alog_ref, dsk_ref, nw_ref,
                o_ref, s_ref):
    c = pl.program_id(1)
    ln = SSD_CHUNK
    hd = SSD_HEAD
    dw = SSD_HEADS * hd
    gn = SSD_STATE

    @pl.when(c == 0)
    def _():
        s_ref[...] = jnp.zeros_like(s_ref)

    xbc = xbc_ref[0]
    halo = xp_ref.shape[1]
    prev = jnp.where(c == 0, jnp.zeros_like(xp_ref[0]), xp_ref[0])
    full = jnp.concatenate([prev, xbc], axis=0).astype(BF16)
    conv = cb_ref[...] + cw_ref[SSD_CONV - 1:SSD_CONV, :] * xbc.astype(F32)
    for j in range(SSD_CONV - 1):
        lag = SSD_CONV - 1 - j
        pick = _iota2((ln, halo + ln), 1) == _iota2((ln, halo + ln), 0) + (halo - lag)
        shifted = jnp.dot(jnp.where(pick, 1.0, 0.0).astype(BF16), full, preferred_element_type=F32)
        conv = conv + cw_ref[j:j + 1, :] * shifted
    act = _silu(conv)
    xh = act[:, :dw]
    dt = _softplus(dt_ref[0] + dtb_ref[...])
    adt = -jnp.exp(alog_ref[...]) * dt
    tril = _iota2((ln, ln), 0) >= _iota2((ln, ln), 1)
    acs = _dot_exact_x(jnp.where(tril, 1.0, 0.0), adt)
    acs_t = acs.T
    tot = acs[ln - 1:ln, :]
    hg = SSD_HEADS // SSD_GROUPS
    s_all = s_ref[...]
    y_heads, s_heads = [], []
    for gi in range(SSD_GROUPS):
        bm = act[:, dw + gi * gn:dw + (gi + 1) * gn]
        cm = act[:, dw + SSD_GROUPS * gn + gi * gn:dw + SSD_GROUPS * gn + (gi + 1) * gn]
        cb = _bdot_nt(cm, bm)
        for hh in range(hg):
            h = gi * hg + hh
            sl = slice(h * hd, (h + 1) * hd)
            col = acs[:, h:h + 1]
            rowv = acs_t[h:h + 1, :]
            lmat = jnp.exp(jnp.where(tril, col - rowv, -jnp.inf))
            xh_h = xh[:, sl]
            xdt = xh_h * dt[:, h:h + 1]
            tot_h = tot[:, h:h + 1]
            st = s_all[h]
            y_h = _bdot(cb * lmat, xdt) + jnp.exp(col) * _bdot(cm, st)
            s_heads.append(jnp.exp(tot_h) * st + _bdot_tn(bm * jnp.exp(tot_h - col), xdt))
            y_heads.append(y_h + dsk_ref[:, sl] * xh_h)
    s_ref[...] = jnp.stack(s_heads, axis=0)
    y = jnp.concatenate(y_heads, axis=1) * _silu(z_ref[0].astype(F32))
    gw = dw // SSD_GROUPS
    for gi in range(SSD_GROUPS):
        yg = y[:, gi * gw:(gi + 1) * gw]
        yg = yg * lax.rsqrt(jnp.mean(yg * yg, axis=-1, keepdims=True) + EPS)
        o_ref[0, :, gi * gw:(gi + 1) * gw] = (yg * nw_ref[:, gi * gw:(gi + 1) * gw]).astype(o_ref.dtype)


def _ssd_mix(z, xbc, dtp, conv_w, conv_b, dt_bias, a_log, d_skip, norm_w):
    b, t, dw = z.shape
    ln = SSD_CHUNK
    xw = xbc.shape[2]
    pad = lambda vec: jnp.pad(vec, (0, LANES - vec.shape[0])).reshape(1, LANES)
    dsk = jnp.repeat(d_skip, SSD_HEAD).reshape(1, dw)
    prm = [conv_w, conv_b.reshape(1, xw), pad(dt_bias), pad(a_log), dsk, norm_w.reshape(1, dw)]
    full = lambda arr: pl.BlockSpec(arr.shape, lambda i, j: (0,) * arr.ndim)
    blk = lambda w: pl.BlockSpec((1, ln, w), lambda i, j: (i, j, 0))
    halo = 2 * SUBLANES
    return pl.pallas_call(
        _ssd_kernel, grid=(b, t // ln),
        in_specs=[blk(dw), blk(xw),
                  pl.BlockSpec((1, halo, xw), lambda i, j: (i, jnp.maximum(j * (ln // halo) - 1, 0), 0)),
                  blk(LANES)] + [full(x) for x in prm],
        out_specs=blk(dw), out_shape=jax.ShapeDtypeStruct((b, t, dw), BF16),
        scratch_shapes=[pltpu.VMEM((SSD_HEADS, SSD_STATE, SSD_HEAD), F32)],
        compiler_params=_cparams(("parallel", "arbitrary")), name="ssd",
    )(z, xbc, xbc, dtp, *prm)


def _mix_out1_kernel(x_ref, yc_ref, yd_ref, gw_ref, gb_ref, wc_ref, wd_ref, gpost_ref, gpre_ref,
                     wrh_ref, wrl_ref, x1_ref, h_ref, idx_ref, gate_ref):
    yc = _gelu_tanh(yc_ref[...].astype(F32))
    yc = yc * _sigmoid(_bdot(yc, gw_ref[...]) + gb_ref[...])
    y = _bdot(yc, wc_ref[...]) + _bdot(yd_ref[...], wd_ref[...])
    x1 = x_ref[...] + _rms(y, gpost_ref[...])
    x1_ref[...] = x1
    h = _rms(x1, gpre_ref[...])
    h_ref[...] = h
    hh = h.astype(BF16)
    hl = (h - hh.astype(F32)).astype(BF16)
    wrh = wrh_ref[...]
    logits = (jnp.dot(hh, wrh, preferred_element_type=F32) + jnp.dot(hl, wrh, preferred_element_type=F32)
              + jnp.dot(hh, wrl_ref[...], preferred_element_type=F32))
    lane = _iota2(logits.shape, 1)
    lane_f = lane.astype(F32)
    logits = jnp.where(lane < MOE_EXPERTS, logits, -jnp.inf)
    m1 = jnp.max(logits, axis=-1, keepdims=True)
    i1 = jnp.min(jnp.where(logits == m1, lane_f, float(LANES)), axis=-1, keepdims=True)
    rest = jnp.where(lane_f == i1, -jnp.inf, logits)
    m2 = jnp.max(rest, axis=-1, keepdims=True)
    i2 = jnp.min(jnp.where(rest == m2, lane_f, float(LANES)), axis=-1, keepdims=True)
    e2 = jnp.exp(m2 - m1)
    g1 = 1.0 / (1.0 + e2)
    g2 = e2 / (1.0 + e2)
    idx_ref[...] = jnp.where(lane == 0, i1, jnp.where(lane == 1, i2, 0.0)).astype(jnp.int32)
    gate_ref[...] = jnp.where(lane == 0, g1, jnp.where(lane == 1, g2, 0.0))


def _mix_out1(x2, yc, yd, glu_w, glu_b, wc, wd, gpost, gpre, wr, tm=512):
    n, d = x2.shape
    wr_p = jnp.pad(wr, ((0, 0), (0, LANES - wr.shape[1])))
    wrh = wr_p.astype(BF16)
    wrl = (wr_p - wrh.astype(F32)).astype(BF16)
    row = lambda w: pl.BlockSpec((tm, w), lambda i: (i, 0))
    full = lambda arr: pl.BlockSpec(arr.shape, lambda i: (0,) * arr.ndim)
    prm = [glu_w, glu_b, wc, wd, gpost, gpre, wrh, wrl]
    return pl.pallas_call(
        _mix_out1_kernel, grid=(n // tm,),
        in_specs=[row(d), row(yc.shape[1]), row(yd.shape[1])] + [full(p) for p in prm],
        out_specs=[row(d), row(d), row(LANES), row(LANES)],
        out_shape=[jax.ShapeDtypeStruct((n, d), F32), jax.ShapeDtypeStruct((n, d), F32),
                   jax.ShapeDtypeStruct((n, LANES), jnp.int32), jax.ShapeDtypeStruct((n, LANES), F32)],
        compiler_params=_cparams(("parallel",)), name="mix_out1",
    )(x2, yc, yd, *prm)


GATHER_UNROLL = 8


def _gather_rows(n_rows, make_copy):
    def body(j, carry):
        for q in range(GATHER_UNROLL):
            make_copy(j * GATHER_UNROLL + q).start(priority=q % 2)
        return carry

    lax.fori_loop(0, n_rows // GATHER_UNROLL, body, 0)


def _moe_kernel(nf, be_ref, tok_ref, dst_ref, nact_ref, h_hbm, wg_ref, wu_ref, wd_ref, y_hbm, buf_ref,
                xb_ref, gsem, ssem):
    i = pl.program_id(0)
    f = pl.program_id(1)
    n_blocks = pl.num_programs(0)
    tm = buf_ref.shape[1]
    nact = nact_ref[0]
    active = i < nact
    slot = lax.rem(i, 2)
    other = 1 - slot
    xs = lambda sl: buf_ref.at[sl]
    yb = lambda sl: buf_ref.at[2 + sl]

    def gather_copy(block, sl, r):
        tok = tok_ref[block * tm + r]
        return pltpu.make_async_copy(h_hbm.at[pl.ds(tok, 1)], buf_ref.at[sl, pl.ds(r, 1)], gsem.at[sl])

    def scatter_copy(block, sl, r):
        dst = dst_ref[block * tm + r]
        return pltpu.make_async_copy(buf_ref.at[2 + sl, pl.ds(r, 1)], y_hbm.at[pl.ds(dst, 1)], ssem.at[sl])

    def wait_rows(sem_slot_ref, buf):
        pltpu.make_async_copy(h_hbm.at[pl.ds(0, tm)], buf, sem_slot_ref).wait()

    @pl.when(f == 0)
    def _():
        @pl.when(i == 0)
        def _():
            _gather_rows(tm, functools.partial(gather_copy, 0, 0))
            buf_ref[3] = jnp.zeros(buf_ref.shape[1:], F32)
            n_real = y_hbm.shape[0] - 2 * tm
            for half in range(2):
                init = pltpu.make_async_copy(yb(1), y_hbm.at[pl.ds(n_real + half * tm, tm)], ssem.at[0])
                init.start()
                init.wait()

        @pl.when(i <= nact)
        def _():
            wait_rows(gsem.at[slot], xs(slot))

        @pl.when(jnp.logical_and(i >= 1, i <= nact))
        def _():
            wait_rows(ssem.at[slot], yb(slot))

        @pl.when(i == nact)
        def _():
            _gather_rows(tm, functools.partial(scatter_copy, i - 1, other))
            wait_rows(ssem.at[other], yb(other))

        @pl.when(active)
        def _():
            xb_ref[...] = buf_ref[slot].astype(BF16)
            buf_ref[2 + slot] = jnp.zeros(buf_ref.shape[1:], F32)

    @pl.when(active)
    def _():
        x = xb_ref[...]
        nch = wg_ref.shape[2] // MXU_TILE
        rows_f = tm // nf
        base = pl.multiple_of(f * rows_f, SUBLANES)
        prev = jnp.where(i == 0, n_blocks - 1, i - 1)
        for c in range(nch):
            for r in range(rows_f * c // nch, rows_f * (c + 1) // nch):
                gather_copy(i + 1, other, base + r).start(priority=1)
                scatter_copy(prev, other, base + r).start(priority=1)
            cols = slice(c * MXU_TILE, (c + 1) * MXU_TILE)
            gate = jnp.dot(x, wg_ref[0, :, cols], preferred_element_type=F32)
            up = jnp.dot(x, wu_ref[0, :, cols], preferred_element_type=F32)
            buf_ref[2 + slot] += jnp.dot((_silu(gate) * up).astype(BF16), wd_ref[0, cols, :],
                                         preferred_element_type=F32)


def _moe_experts(h, block_expert, slot_tok, slot_dst, nact, n_rows, wg, wu, wd):
    n, d = h.shape
    tm = MOE_ROWS
    n_blocks = slot_tok.shape[0] // tm
    tf = wg.shape[2]
    nf = 1
    once = pl.Buffered(1)
    wmap = lambda i, j, be, tok, dst, na: (be[i], 0, j)
    grid_spec = pltpu.PrefetchScalarGridSpec(
        num_scalar_prefetch=4, grid=(n_blocks, nf),
        in_specs=[pl.BlockSpec(memory_space=pl.ANY),
                  pl.BlockSpec((1, d, tf), wmap, pipeline_mode=once),
                  pl.BlockSpec((1, d, tf), wmap, pipeline_mode=once),
                  pl.BlockSpec((1, tf, d), lambda i, j, be, tok, dst, na: (be[i], j, 0),
                               pipeline_mode=once)],
        out_specs=pl.BlockSpec(memory_space=pl.ANY),
        scratch_shapes=[pltpu.VMEM((4, tm, d), F32), pltpu.VMEM((tm, d), BF16),
                        pltpu.SemaphoreType.DMA((2,)), pltpu.SemaphoreType.DMA((2,))])
    return pl.pallas_call(
        functools.partial(_moe_kernel, nf), grid_spec=grid_spec,
        out_shape=jax.ShapeDtypeStruct((n_rows, d), F32),
        compiler_params=pltpu.CompilerParams(dimension_semantics=("arbitrary", "arbitrary"),
                                             vmem_limit_bytes=VMEM_LIMIT, disable_bounds_checks=True),
        name="moe_experts",
    )(block_expert, slot_tok, slot_dst, nact, h, wg, wu, wd)


def _combine_kernel(x_ref, y0_ref, y1_ref, gate_ref, gpost_ref, o_ref):
    gates = gate_ref[...]
    y = gates[:, 0:1] * y0_ref[...] + gates[:, 1:2] * y1_ref[...]
    o_ref[...] = x_ref[...] + _rms(y, gpost_ref[...])


def _moe_combine(x1, y, gates, gpost, tm=512):
    n, d = x1.shape
    nt = n // tm
    return pl.pallas_call(
        _combine_kernel, grid=(nt,),
        in_specs=[pl.BlockSpec((tm, d), lambda i: (i, 0)), pl.BlockSpec((tm, d), lambda i: (i, 0)),
                  pl.BlockSpec((tm, d), lambda i: (nt + i, 0)),
                  pl.BlockSpec((tm, LANES), lambda i: (i, 0)), pl.BlockSpec((1, d), lambda i: (0, 0))],
        out_specs=pl.BlockSpec((tm, d), lambda i: (i, 0)),
        out_shape=jax.ShapeDtypeStruct((n, d), F32),
        compiler_params=_cparams(("parallel",)), name="moe_combine",
    )(x1, y, y, gates, gpost)


def _moe_plan(idx, n):
    tm = MOE_ROWS
    flat_e = idx[:, :2].reshape(-1)
    onehot = (flat_e[:, None] == jnp.arange(MOE_EXPERTS, dtype=jnp.int32)[None, :]).astype(jnp.int32)
    csum = jnp.cumsum(onehot, axis=0)
    counts = csum[-1]
    rank = jnp.sum((csum - onehot) * onehot, axis=1)
    padded = (counts + tm - 1) // tm * tm
    pend = jnp.cumsum(padded)
    pstart = pend - padded
    dest = (jnp.sum(onehot * pstart[None, :], axis=1) + rank).astype(jnp.int32)
    n_blocks = (2 * n) // tm + MOE_EXPERTS + 1
    n_slots = n_blocks * tm
    slot_pair = jnp.full((n_slots,), -1, jnp.int32).at[dest].set(
        jnp.arange(2 * n, dtype=jnp.int32), unique_indices=True)
    real = slot_pair >= 0
    slot_tok = jnp.where(real, slot_pair // 2, 0)
    s_id = jnp.arange(n_slots, dtype=jnp.int32)
    slot_dst = jnp.where(real, slot_pair % 2 * n + slot_pair // 2, 2 * n + (s_id // tm) % 2 * tm + s_id % tm)
    block_start = jnp.arange(n_blocks, dtype=jnp.int32) * tm
    block_expert = jnp.minimum(jnp.sum((block_start[:, None] >= pend[None, :]).astype(jnp.int32), axis=1),
                               MOE_EXPERTS - 1)
    nact = (pend[-1] // tm).astype(jnp.int32).reshape(1)
    return block_expert, slot_tok, slot_dst, nact, 2 * n + 2 * tm


def kernel(x, l0_norm_pre_mix, l0_w_in, l0_rwkv_mu, l0_rwkv_w0, l0_rwkv_w2, l0_rwkv_a0, l0_rwkv_a2, l0_rwkv_g2, l0_rwkv_k_k, l0_rwkv_k_a, l0_rwkv_r_k, l0_rwkv_ln_w, l0_rwkv_ln_b, l0_gmlp_ln_w, l0_gmlp_ln_b, l0_gmlp_ws, l0_gmlp_bs, l0_w_out, l0_norm_post_mix, l0_norm_pre_ffn, l0_ffn_w_gate, l0_ffn_w_up, l0_ffn_w_down, l0_norm_post_ffn, l1_norm_pre_mix, l1_w_in, l1_s5_a_re, l1_s5_a_im, l1_s5_log_dt, l1_s5_b_re, l1_s5_b_im, l1_s5_c_re, l1_s5_c_im, l1_s5_d, l1_s5_glu_w, l1_s5_glu_b, l1_m2_conv_w, l1_m2_conv_b, l1_m2_dt_bias, l1_m2_a_log, l1_m2_d, l1_m2_norm_w, l1_w_out, l1_norm_post_mix, l1_norm_pre_ffn, l1_moe_router, l1_moe_w_gate, l1_moe_w_up, l1_moe_w_down, l1_norm_post_ffn):
    b, t, d = x.shape
    n = b * t
    x2 = x.reshape(n, d)
    row = lambda vec: vec.reshape(1, -1)

    aw = l0_rwkv_w0.shape[0]
    heads = aw // RWKV_HEAD
    lw_, la_, lg_ = l0_rwkv_w2.shape[0], l0_rwkv_a2.shape[0], l0_rwkv_g2.shape[0]
    a_in = 3 * aw + lw_ + la_ + lg_
    padc = lambda m, wdt: jnp.pad(m, ((0, 0), (0, LANES - wdt)))
    o = 3 * aw
    w_a = jnp.concatenate([l0_w_in[:, :o], padc(l0_w_in[:, o:o + lw_], lw_),
                           padc(l0_w_in[:, o + lw_:o + lw_ + la_], la_),
                           padc(l0_w_in[:, o + lw_ + la_:a_in], lg_)], axis=1).astype(BF16)
    w_b = l0_w_in[:, a_in:].astype(BF16)
    p_a, p_b = _norm_proj(x2, l0_norm_pre_mix, [w_a, w_b], [F32, BF16])
    padv = lambda vec, wdt: jnp.pad(vec, (0, LANES - wdt))
    mu = l0_rwkv_mu
    mu_p = jnp.concatenate([mu[:o], padv(mu[o:o + lw_], lw_), padv(mu[o + lw_:o + lw_ + la_], la_),
                            padv(mu[o + lw_ + la_:], lg_)])
    padr = lambda m: jnp.pad(m, ((0, LANES - m.shape[0]), (0, 0))).astype(BF16)
    hid = jnp.arange(LANES, dtype=jnp.int32) // RWKV_HEAD
    gsum = (hid[:, None] == hid[None, :]).astype(BF16)
    rwkv_prm = [row(mu_p), row(l0_rwkv_w0), padr(l0_rwkv_w2), row(l0_rwkv_a0), padr(l0_rwkv_a2),
                padr(l0_rwkv_g2), row(l0_rwkv_k_k), row(l0_rwkv_k_a), row(l0_rwkv_r_k),
                row(l0_rwkv_ln_w), row(l0_rwkv_ln_b), gsum]
    ya = _rwkv_mix(p_a.reshape(b, t, -1), rwkv_prm, heads)
    yb = _gmlp_mix(p_b.reshape(b, t, -1), l0_gmlp_ln_w, l0_gmlp_ln_b, l0_gmlp_ws, l0_gmlp_bs)
    wo = l0_w_out.astype(BF16)
    x2 = _mix_out0(x2, ya.reshape(n, -1), yb.reshape(n, -1), wo[:aw], wo[aw:], row(l0_norm_post_mix))
    x2 = _ffn(x2, row(l0_norm_pre_ffn), l0_ffn_w_gate.astype(BF16), l0_ffn_w_up.astype(BF16),
              l0_ffn_w_down.astype(BF16), row(l0_norm_post_ffn))

    cw = l1_s5_d.shape[0]
    dw = l1_m2_norm_w.shape[0]
    xw = l1_m2_conv_w.shape[1]
    nh = l1_m2_dt_bias.shape[0]
    w1 = l1_w_in
    w_parts = [w1[:, :cw], w1[:, cw:cw + dw], w1[:, cw + dw:cw + dw + xw],
               padc(w1[:, cw + dw + xw:], nh)]
    u_c, z_d, xbc, dtp = _norm_proj(x2, l1_norm_pre_mix, [w.astype(BF16) for w in w_parts],
                                    [BF16, BF16, BF16, F32])
    s5_w = _s5_weights(l1_s5_a_re, l1_s5_a_im, l1_s5_log_dt, l1_s5_b_re, l1_s5_b_im, l1_s5_c_re,
                       l1_s5_c_im, l1_s5_d, S5_CHUNK)
    yc = _s5_core(u_c.reshape(b, t, cw), s5_w)
    yd = _ssd_mix(z_d.reshape(b, t, dw), xbc.reshape(b, t, xw), dtp.reshape(b, t, LANES),
                  l1_m2_conv_w, l1_m2_conv_b, l1_m2_dt_bias, l1_m2_a_log, l1_m2_d, l1_m2_norm_w)
    wo1 = l1_w_out.astype(BF16)
    x1, h, idx, gates = _mix_out1(x2, yc.reshape(n, cw), yd.reshape(n, dw), l1_s5_glu_w.astype(BF16),
                                  row(l1_s5_glu_b), wo1[:cw], wo1[cw:], row(l1_norm_post_mix),
                                  row(l1_norm_pre_ffn), l1_moe_router)
    block_expert, slot_tok, slot_dst, nact, n_rows = _moe_plan(idx, n)
    ys = _moe_experts(h, block_expert, slot_tok, slot_dst, nact, n_rows, l1_moe_w_gate.astype(BF16),
                      l1_moe_w_up.astype(BF16), l1_moe_w_down.astype(BF16))
    out = _moe_combine(x1, ys, gates, row(l1_norm_post_ffn))
    return out.reshape(b, t, d)
```

```python
import functools

import jax
import jax.numpy as jnp
from jax import lax
from jax.experimental import pallas as pl
from jax.experimental.pallas import tpu as pltpu

F32 = jnp.float32
BF16 = jnp.bfloat16

EPS = 1e-6
RWKV_GN_EPS = 64e-5
RWKV_HEAD = 64
RWKV_CHUNK = 64
GMLP_CHUNK = 128
GMLP_GROUPS = 4
S5_GROUP_CH = 16
S5_STATE = 64
S5_CHUNK = 64
SSD_HEAD = 64
SSD_HEADS = 8
SSD_GROUPS = 2
SSD_STATE = 128
SSD_CONV = 4
SSD_CHUNK = 128
MOE_EXPERTS = 8
MOE_ROWS = 512
MXU_TILE = 256
LANES = 128
SUBLANES = 8
VMEM_LIMIT = 56 * 1024 * 1024


def _cparams(sem):
    return pltpu.CompilerParams(dimension_semantics=sem, vmem_limit_bytes=VMEM_LIMIT)


def _bdot(a, b):
    return jnp.dot(a.astype(BF16), b.astype(BF16), preferred_element_type=F32)


def _bdot_nt(a, b):
    return lax.dot_general(a.astype(BF16), b.astype(BF16), (((1,), (1,)), ((), ())),
                           preferred_element_type=F32)


def _bdot_tn(a, b):
    return lax.dot_general(a.astype(BF16), b.astype(BF16), (((0,), (0,)), ((), ())),
                           preferred_element_type=F32)


def _split3(x):
    h = x.astype(BF16)
    r1 = x - h.astype(F32)
    m = r1.astype(BF16)
    l = (r1 - m.astype(F32)).astype(BF16)
    return h, m, l


def _dot_x_exact(x, e):
    h, m, l = _split3(x)
    e = e.astype(BF16)
    return (jnp.dot(h, e, preferred_element_type=F32) + jnp.dot(m, e, preferred_element_type=F32)
            + jnp.dot(l, e, preferred_element_type=F32))


def _dot_exact_x(e, x):
    h, m, l = _split3(x)
    e = e.astype(BF16)
    return (jnp.dot(e, h, preferred_element_type=F32) + jnp.dot(e, m, preferred_element_type=F32)
            + jnp.dot(e, l, preferred_element_type=F32))


def _rms(x, g):
    return x * lax.rsqrt(jnp.mean(x * x, axis=-1, keepdims=True) + EPS) * g


def _sigmoid(x):
    return 1.0 / (1.0 + jnp.exp(-x))


def _silu(x):
    return x * _sigmoid(x)


def _softplus(x):
    return jnp.maximum(x, 0.0) + jnp.log(1.0 + jnp.exp(-jnp.abs(x)))


def _gelu_tanh(x):
    return 0.5 * x * (1.0 + jnp.tanh(0.7978845608028654 * (x + 0.044715 * x * x * x)))


def _iota2(shape, dim):
    return lax.broadcasted_iota(jnp.int32, shape, dim)


def _norm_proj_kernel(n_out, x_ref, g_ref, *refs):
    w_refs = refs[:n_out]
    o_refs = refs[n_out:]
    xn = _rms(x_ref[...], g_ref[...]).astype(BF16)
    for w_ref, o_ref in zip(w_refs, o_refs):
        o_ref[...] = jnp.dot(xn, w_ref[...], preferred_element_type=F32).astype(o_ref.dtype)


def _norm_proj(x2, g, ws, dtypes, tm=512):
    n, d = x2.shape
    in_specs = [pl.BlockSpec((tm, d), lambda i: (i, 0)), pl.BlockSpec((1, d), lambda i: (0, 0))]
    in_specs += [pl.BlockSpec(w.shape, lambda i: (0, 0)) for w in ws]
    out_specs = [pl.BlockSpec((tm, w.shape[1]), lambda i: (i, 0)) for w in ws]
    out_shape = [jax.ShapeDtypeStruct((n, w.shape[1]), dt) for w, dt in zip(ws, dtypes)]
    return pl.pallas_call(
        functools.partial(_norm_proj_kernel, len(ws)),
        grid=(n // tm,), in_specs=in_specs, out_specs=out_specs, out_shape=out_shape,
        compiler_params=_cparams(("parallel",)), name="norm_proj",
    )(x2, g.reshape(1, d), *ws)


PRE_NAMES = ("v", "g", "bonus", "rt", "kt", "bt", "at", "bh", "kh")


def _rwkv_kernel(heads, nb, p_ref, pp_ref, mu_ref, w0_ref, w2_ref, a0_ref, a2_ref, g2_ref, kk_ref,
                 ka_ref, rk_ref, lnw_ref, lnb_ref, gs_ref, o_ref, z_ref, pre_ref, wl_ref):
    c = pl.program_id(1)
    ln = RWKV_CHUNK
    hd = RWKV_HEAD
    aw = heads * hd

    @pl.when(c == 0)
    def _():
        z_ref[...] = jnp.zeros_like(z_ref)
        pre_ref[...] = jnp.zeros_like(pre_ref)
        wl_ref[...] = jnp.zeros_like(wl_ref)

    tril_f = jnp.where(_iota2((ln, ln), 0) >= _iota2((ln, ln), 1), 1.0, 0.0)
    rows = _iota2((ln, 1), 0)
    gs_tile = gs_ref[...]

    def gs(x):
        nt = x.shape[1] // LANES
        stacked = jnp.concatenate([x[:, j * LANES:(j + 1) * LANES] for j in range(nt)], axis=0)
        red = _dot_x_exact(stacked, gs_tile)
        return jnp.concatenate([red[j * ln:(j + 1) * ln] for j in range(nt)], axis=1)

    pre_idx = {nm: idx for idx, nm in enumerate(PRE_NAMES)}
    pending = []

    def prep_steps():
        for bi in range(nb):
            p = p_ref[bi]
            prev = jnp.where(c == 0, 0.0, pp_ref[bi][SUBLANES - 1:SUBLANES, :])
            ps = jnp.where(rows == 0, prev, pltpu.roll(p, 1, axis=0))
            pm = p + (ps - p) * mu_ref[...]
            r = pm[:, 0:aw]
            k = pm[:, aw:2 * aw]
            v = pm[:, 2 * aw:3 * aw]
            xw = pm[:, 3 * aw:3 * aw + LANES]
            xa = pm[:, 3 * aw + LANES:3 * aw + 2 * LANES]
            xg = pm[:, 3 * aw + 2 * LANES:3 * aw + 3 * LANES]
            yield
            w = w0_ref[...] + _bdot(jnp.tanh(xw), w2_ref[...])
            a = _sigmoid(a0_ref[...] + _bdot(xa, a2_ref[...]))
            g = _bdot(_sigmoid(xg), g2_ref[...])
            yield
            w = -_softplus(-w) - 0.5
            lw = -jnp.exp(w)
            kk = k * kk_ref[...]
            kk_ss = gs(kk * kk)
            yield
            cs = _dot_exact_x(tril_f, lw)
            kk = kk / jnp.maximum(jnp.sqrt(kk_ss), 1e-12)
            kmod = k * (1.0 + (a - 1.0) * ka_ref[...])
            yield
            bonus = gs(r * kmod * rk_ref[...])
            bvec = kk * a
            cs_last = cs[ln - 1:ln, :]
            encs = jnp.exp(-cs)
            yield
            dec_end = jnp.exp(cs_last - cs)
            nxt = dict(v=v, g=g, bonus=bonus, rt=r * jnp.exp(cs), kt=kmod * encs, bt=bvec * encs,
                       at=-kk * jnp.exp(cs - lw), bh=bvec * dec_end, kh=kmod * dec_end)
            pending.append((bi, nxt, jnp.broadcast_to(jnp.exp(cs_last), (SUBLANES, aw))))
            yield

    prep = prep_steps()
    tick = lambda: next(prep, None)

    lane = _iota2((ln, LANES), 1)
    lane_in = jnp.where(lane >= hd, lane - hd, lane)
    trow = _iota2((ln, LANES), 0)
    left = lane < hd
    tril_p = lane_in <= trow
    stril_p = lane_in < trow
    eye_p = lane_in == trow
    eye_pf = jnp.where(eye_p, 1.0, 0.0)

    def bd(x):
        xb = x.astype(BF16)
        zero = jnp.zeros_like(xb)
        return jnp.concatenate([jnp.where(left, xb, zero), jnp.where(left, zero, xb)], axis=0)

    def dot(a, b):
        return jnp.dot(a.astype(BF16), b, preferred_element_type=F32)

    npair = heads // 2
    pairs = [(bi, j) for bi in range(nb) for j in range(npair)]

    class _Tiles:
        def __init__(self, name):
            self.idx = pre_idx[name]

        def __getitem__(self, i):
            bi, j = pairs[i]
            return pre_ref[self.idx, bi, :, j * LANES:(j + 1) * LANES]

    at, rt, bt, kt, vv, bh, kh = (_Tiles(n) for n in ("at", "rt", "bt", "kt", "v", "bh", "kh"))
    wl = [wl_ref[bi, 0:1, j * LANES:(j + 1) * LANES] for bi, j in pairs]
    z_all = z_ref[...]
    zs = [z_all[bi, j] for bi, j in pairs]
    npr = range(len(pairs))
    lhs = [jnp.concatenate([at[i], rt[i]], axis=0).astype(BF16) for i in npr]
    abk = [lax.dot_general(lhs[i], jnp.concatenate([bd(bt[i]), bd(kt[i])], axis=0),
                           (((1,), (1,)), ((), ())), preferred_element_type=F32) for i in npr]
    ab = [abk[i][:, :LANES] for i in npr]
    ak = [abk[i][:, LANES:] for i in npr]
    tick()
    nmat = [jnp.where(stril_p, ab[i][:ln], 0.0) for i in npr]
    tinv = [eye_pf + nmat[i] for i in npr]
    npow = [dot(nmat[i], bd(nmat[i])) for i in npr]
    tick()
    for step in range(5):
        bdn = [bd(npow[i]) for i in npr]
        if step < 4:
            both = [dot(jnp.concatenate([tinv[i], npow[i]], axis=0), bdn[i]) for i in npr]
            tinv = [tinv[i] + both[i][:ln] for i in npr]
            npow = [both[i][ln:] for i in npr]
        else:
            tinv = [tinv[i] + dot(tinv[i], bdn[i]) for i in npr]
        tick()
    bdv = [bd(vv[i]) for i in npr]
    bdz = [bd(zs[i]) for i in npr]
    xmat = [dot(jnp.concatenate([jnp.where(stril_p, ak[i][:ln], 0.0), at[i]], axis=1),
                jnp.concatenate([bdv[i], bdz[i]], axis=0)) for i in npr]
    tick()
    u = [dot(tinv[i], bd(xmat[i])) for i in npr]
    tick()
    ys_p = [dot(jnp.concatenate([rt[i], jnp.where(tril_p, ab[i][ln:], 0.0),
                                 jnp.where(tril_p, ak[i][ln:], 0.0)], axis=1),
                jnp.concatenate([bdz[i], bd(u[i]), bdv[i]], axis=0)) for i in npr]
    tick()
    cross = [_bdot_tn(jnp.concatenate([bh[i], kh[i]], axis=0), jnp.concatenate([u[i], vv[i]], axis=0))
             for i in npr]
    tick()
    z_new = []
    for i in npr:
        dg = jnp.where(eye_p, wl[i], 0.0)
        wl_i = jnp.sum(jnp.where(left, dg, 0.0), axis=1, keepdims=True)
        wl_j = jnp.sum(jnp.where(left, 0.0, dg), axis=1, keepdims=True)
        z_new.append(jnp.where(left, wl_i, wl_j) * zs[i] + jnp.where(left, cross[i][:ln], cross[i][ln:]))
    z_ref[...] = jnp.stack(z_new, axis=0).reshape(z_ref.shape)

    tick()
    inv = 1.0 / hd
    for bi in range(nb):
        y = jnp.concatenate(ys_p[bi * npair:(bi + 1) * npair], axis=1)
        mean = gs(y) * inv
        d = y - mean
        var = gs(d * d) * inv
        yn = d * lax.rsqrt(var + RWKV_GN_EPS) * lnw_ref[...] + lnb_ref[...]
        o_ref[bi] = ((yn + pre_ref[pre_idx["bonus"], bi] * pre_ref[pre_idx["v"], bi])
                     * pre_ref[pre_idx["g"], bi]).astype(o_ref.dtype)
        tick()
    for _ in prep:
        pass
    for bi, nxt, wl_next in pending:
        for nm, idx in pre_idx.items():
            pre_ref[idx, bi] = nxt[nm]
        wl_ref[bi] = wl_next


def _rwkv_mix(p_a, prm, heads, nb=2):
    b, t, cin = p_a.shape
    aw = heads * RWKV_HEAD
    ln = RWKV_CHUNK
    nc = t // ln
    sub = ln // SUBLANES
    full = lambda arr: pl.BlockSpec(arr.shape, lambda i, j: (0,) * arr.ndim)
    in_specs = [pl.BlockSpec((nb, ln, cin), lambda i, j: (i, jnp.minimum(j, nc - 1), 0)),
                pl.BlockSpec((nb, SUBLANES, cin),
                             lambda i, j: (i, jnp.maximum(jnp.minimum(j, nc - 1) * sub - 1, 0), 0))]
    in_specs += [full(x) for x in prm]
    return pl.pallas_call(
        functools.partial(_rwkv_kernel, heads, nb),
        grid=(b // nb, nc + 1), in_specs=in_specs,
        out_specs=pl.BlockSpec((nb, ln, aw), lambda i, j: (i, jnp.maximum(j - 1, 0), 0)),
        out_shape=jax.ShapeDtypeStruct((b, t, aw), BF16),
        scratch_shapes=[pltpu.VMEM((nb, heads // 2, RWKV_HEAD, 2 * RWKV_HEAD), F32),
                        pltpu.VMEM((len(PRE_NAMES), nb, ln, aw), F32),
                        pltpu.VMEM((nb, SUBLANES, aw), F32)],
        compiler_params=_cparams(("parallel", "arbitrary")), name="rwkv7",
    )(p_a, p_a, *prm)


def _gmlp_kernel(p_ref, lnw_ref, lnb_ref, ws_ref, bs_ref, o_ref):
    ln = GMLP_CHUNK
    bw = p_ref.shape[2] // 2
    gd = bw // GMLP_GROUPS
    tril = _iota2((ln, ln), 0) >= _iota2((ln, ln), 1)
    ws_c = [jnp.where(tril, ws_ref[gi], 0.0).astype(BF16) for gi in range(GMLP_GROUPS)]
    for ci in range(p_ref.shape[1] // ln):
        x = _gelu_tanh(p_ref[0, ci * ln:(ci + 1) * ln, :].astype(F32))
        for gi in range(GMLP_GROUPS):
            u = x[:, gi * gd:(gi + 1) * gd]
            v = x[:, bw + gi * gd:bw + (gi + 1) * gd]
            mean = jnp.mean(v, axis=-1, keepdims=True)
            d = v - mean
            var = jnp.mean(d * d, axis=-1, keepdims=True)
            vn = d * lax.rsqrt(var + EPS) * lnw_ref[gi:gi + 1, :] + lnb_ref[gi:gi + 1, :]
            s = jnp.dot(ws_c[gi], vn.astype(BF16), preferred_element_type=F32) + bs_ref[gi]
            o_ref[0, ci * ln:(ci + 1) * ln, gi * gd:(gi + 1) * gd] = (u * s).astype(o_ref.dtype)


def _gmlp_mix(p_b, ln_w, ln_b, ws, bs, rows=512):
    b, t, cin = p_b.shape
    bw = cin // 2
    gd = bw // GMLP_GROUPS
    bs_b = jnp.broadcast_to(bs[:, :, None], (GMLP_GROUPS, GMLP_CHUNK, gd))
    full = lambda arr: pl.BlockSpec(arr.shape, lambda i, j: (0,) * arr.ndim)
    return pl.pallas_call(
        _gmlp_kernel, grid=(b, t // rows),
        in_specs=[pl.BlockSpec((1, rows, cin), lambda i, j: (i, j, 0)),
                  full(ln_w), full(ln_b), full(ws), full(bs_b)],
        out_specs=pl.BlockSpec((1, rows, bw), lambda i, j: (i, j, 0)),
        out_shape=jax.ShapeDtypeStruct((b, t, bw), BF16),
        compiler_params=_cparams(("parallel", "parallel")), name="gmlp",
    )(p_b, ln_w, ln_b, ws, bs_b)


def _mix_out0_kernel(x_ref, ya_ref, yb_ref, wa_ref, wb_ref, g_ref, o_ref):
    y = _bdot(ya_ref[...], wa_ref[...]) + _bdot(yb_ref[...], wb_ref[...])
    o_ref[...] = x_ref[...] + _rms(y, g_ref[...])


def _mix_out0(x2, ya, yb, wa, wb, g, tm=512):
    n, d = x2.shape
    row = lambda arr: pl.BlockSpec((tm, arr.shape[1]), lambda i: (i, 0))
    full = lambda arr: pl.BlockSpec(arr.shape, lambda i: (0,) * arr.ndim)
    return pl.pallas_call(
        _mix_out0_kernel, grid=(n // tm,),
        in_specs=[row(x2), row(ya), row(yb), full(wa), full(wb), full(g)],
        out_specs=row(x2), out_shape=jax.ShapeDtypeStruct((n, d), F32),
        compiler_params=_cparams(("parallel",)), name="mix_out0",
    )(x2, ya, yb, wa, wb, g)


def _ffn_kernel(fc, x_ref, gpre_ref, wg_ref, wu_ref, wd_ref, gpost_ref, o_ref):
    h = _rms(x_ref[...], gpre_ref[...]).astype(BF16)
    acc = None
    for c in range(wg_ref.shape[1] // fc):
        cols = slice(c * fc, (c + 1) * fc)
        gate = jnp.dot(h, wg_ref[:, cols], preferred_element_type=F32)
        up = jnp.dot(h, wu_ref[:, cols], preferred_element_type=F32)
        part = jnp.dot((_silu(gate) * up).astype(BF16), wd_ref[cols, :], preferred_element_type=F32)
        acc = part if acc is None else acc + part
    o_ref[...] = x_ref[...] + _rms(acc, gpost_ref[...])


def _ffn(x2, gpre, wg, wu, wd, gpost, tm=512, fc=MXU_TILE):
    n, d = x2.shape
    full = lambda arr: pl.BlockSpec(arr.shape, lambda i: (0,) * arr.ndim)
    return pl.pallas_call(
        functools.partial(_ffn_kernel, fc), grid=(n // tm,),
        in_specs=[pl.BlockSpec((tm, d), lambda i: (i, 0)), full(gpre), full(wg), full(wu), full(wd),
                  full(gpost)],
        out_specs=pl.BlockSpec((tm, d), lambda i: (i, 0)),
        out_shape=jax.ShapeDtypeStruct((n, d), F32),
        compiler_params=_cparams(("parallel",)), name="ffn",
    )(x2, gpre, wg, wu, wd, gpost)


def _s5_kernel(nc, nb, u_ref, tap_ref, wsr_ref, wsi_ref, wcr_ref, wci_ref, alr_ref, ali_ref, d_ref, o_ref,
               toep_ref):
    u = u_ref[0]
    taps = tap_ref[0]
    ch = taps.shape[0]
    lane = _iota2(taps.shape, 1)
    for s in range(taps.shape[1] // ch):
        blk = taps if s == 0 else jnp.where(lane >= ch * s, pltpu.roll(taps, ch * s, axis=1), 0.0)
        toep_ref[ch * s:ch * (s + 1), :] = blk.astype(BF16)
    y = jnp.dot(u, toep_ref[...], preferred_element_type=F32)
    xer = jnp.dot(u, wsr_ref[0], preferred_element_type=F32)
    xei = jnp.dot(u, wsi_ref[0], preferred_element_type=F32)
    alr = alr_ref[0]
    ali = ali_ref[0]
    cr = jnp.zeros((nb, xer.shape[1]), F32)
    ci = jnp.zeros((nb, xer.shape[1]), F32)
    prs, pis = [], []
    for c in range(nc):
        prs.append(cr)
        pis.append(ci)
        er = xer[c * nb:(c + 1) * nb]
        ei = xei[c * nb:(c + 1) * nb]
        cr, ci = alr * cr - ali * ci + er, alr * ci + ali * cr + ei
    pr = jnp.concatenate(prs, axis=0)
    pi = jnp.concatenate(pis, axis=0)
    y = y + _bdot(pr, wcr_ref[0]) + _bdot(pi, wci_ref[0])
    o_ref[0] = (y + d_ref[0] * u.astype(F32)).astype(o_ref.dtype)


def _s5_weights(a_re, a_im, log_dt, b_re, b_im, c_re, c_im, d_skip, ln):
    g, st = a_re.shape
    ch = b_re.shape[2]
    dt = jnp.exp(log_dt)[:, None]
    lr, li = a_re, a_im
    tau = jnp.arange(ln + 1, dtype=F32)[:, None, None]
    mag = jnp.exp(lr[None] * dt[None] * tau)
    pw_r = mag * jnp.cos(li[None] * dt[None] * tau)
    pw_i = mag * jnp.sin(li[None] * dt[None] * tau)
    ab_r, ab_i = pw_r[1], pw_i[1]
    nr, ni = ab_r - 1.0, ab_i
    den = lr * lr + li * li
    fr, fi = (nr * lr + ni * li) / den, (ni * lr - nr * li) / den
    bb_r = fr[..., None] * b_re - fi[..., None] * b_im
    bb_i = fr[..., None] * b_im + fi[..., None] * b_re
    cp_r = c_re[None] * pw_r[:ln, :, None, :] - c_im[None] * pw_i[:ln, :, None, :]
    cp_i = c_re[None] * pw_i[:ln, :, None, :] + c_im[None] * pw_r[:ln, :, None, :]
    hp = lax.Precision.HIGHEST
    taps = (jnp.einsum('tgcp,gpd->gdtc', cp_r, bb_r, precision=hp)
            - jnp.einsum('tgcp,gpd->gdtc', cp_i, bb_i, precision=hp))
    taps = taps.reshape(g, ch, ln * ch)
    rev_r, rev_i = pw_r[:ln][::-1], pw_i[:ln][::-1]
    ws_r = rev_r[..., None] * bb_r[None] - rev_i[..., None] * bb_i[None]
    ws_i = rev_r[..., None] * bb_i[None] + rev_i[..., None] * bb_r[None]
    ws_r = ws_r.transpose(1, 0, 3, 2).reshape(g, ln * ch, st)
    ws_i = ws_i.transpose(1, 0, 3, 2).reshape(g, ln * ch, st)
    q_r, q_i = pw_r[1:ln + 1], pw_i[1:ln + 1]
    wc_r = c_re[None] * q_r[:, :, None, :] - c_im[None] * q_i[:, :, None, :]
    wc_i = -(c_re[None] * q_i[:, :, None, :] + c_im[None] * q_r[:, :, None, :])
    wc_r = wc_r.transpose(1, 3, 0, 2).reshape(g, st, ln * ch)
    wc_i = wc_i.transpose(1, 3, 0, 2).reshape(g, st, ln * ch)
    al_r = pw_r[ln].reshape(g, 1, st)
    al_i = pw_i[ln].reshape(g, 1, st)
    d_t = jnp.tile(d_skip.reshape(g, 1, ch), (1, ln, 1)).reshape(g, 1, ln * ch)
    return (taps, ws_r.astype(BF16), ws_i.astype(BF16), wc_r.astype(BF16),
            wc_i.astype(BF16), al_r, al_i, d_t)


def _s5_core(u, weights):
    b, t, cw = u.shape
    ln, ch = S5_CHUNK, S5_GROUP_CH
    g = cw // ch
    nc = t // ln
    ug = u.astype(BF16).reshape(b, nc, ln, g, ch).transpose(3, 1, 0, 2, 4).reshape(g, nc * b, ln * ch)
    per_g = lambda arr: pl.BlockSpec((1,) + arr.shape[1:], lambda i: (i, 0, 0))
    yg = pl.pallas_call(
        functools.partial(_s5_kernel, nc, b), grid=(g,),
        in_specs=[per_g(ug)] + [per_g(w) for w in weights],
        out_specs=per_g(ug), out_shape=jax.ShapeDtypeStruct(ug.shape, BF16),
        scratch_shapes=[pltpu.VMEM((ln * ch, ln * ch), BF16)],
        compiler_params=_cparams(("parallel",)), name="s5",
    )(ug, *weights)
    return yg.reshape(g, nc, b, ln, ch).transpose(2, 1, 3, 0, 4).reshape(b, t, cw)


def _ssd_kernel(z_ref, xbc_ref, xp_ref, dt_ref, cw_ref, cb_ref, dtb_ref, alog_ref, dsk_ref, nw_ref,
                o_ref, s_ref):
    c = pl.program_id(1)
    ln = SSD_CHUNK
    hd = SSD_HEAD
    dw = SSD_HEADS * hd
    gn = SSD_STATE

    @pl.when(c == 0)
    def _():
        s_ref[...] = jnp.zeros_like(s_ref)

    xbc = xbc_ref[0]
    halo = xp_ref.shape[1]
    prev = jnp.where(c == 0, jnp.zeros_like(xp_ref[0]), xp_ref[0])
    full = jnp.concatenate([prev, xbc], axis=0).astype(BF16)
    conv = cb_ref[...] + cw_ref[SSD_CONV - 1:SSD_CONV, :] * xbc.astype(F32)
    for j in range(SSD_CONV - 1):
        lag = SSD_CONV - 1 - j
        pick = _iota2((ln, halo + ln), 1) == _iota2((ln, halo + ln), 0) + (halo - lag)
        shifted = jnp.dot(jnp.where(pick, 1.0, 0.0).astype(BF16), full, preferred_element_type=F32)
        conv = conv + cw_ref[j:j + 1, :] * shifted
    act = _silu(conv)
    xh = act[:, :dw]
    dt = _softplus(dt_ref[0] + dtb_ref[...])
    adt = -jnp.exp(alog_ref[...]) * dt
    tril = _iota2((ln, ln), 0) >= _iota2((ln, ln), 1)
    acs = _dot_exact_x(jnp.where(tril, 1.0, 0.0), adt)
    acs_t = acs.T
    tot = acs[ln - 1:ln, :]
    hg = SSD_HEADS // SSD_GROUPS
    s_all = s_ref[...]
    y_heads, s_heads = [], []
    for gi in range(SSD_GROUPS):
        bm = act[:, dw + gi * gn:dw + (gi + 1) * gn]
        cm = act[:, dw + SSD_GROUPS * gn + gi * gn:dw + SSD_GROUPS * gn + (gi + 1) * gn]
        cb = _bdot_nt(cm, bm)
        for hh in range(hg):
            h = gi * hg + hh
            sl = slice(h * hd, (h + 1) * hd)
            col = acs[:, h:h + 1]
            rowv = acs_t[h:h + 1, :]
            lmat = jnp.exp(jnp.where(tril, col - rowv, -jnp.inf))
            xh_h = xh[:, sl]
            xdt = xh_h * dt[:, h:h + 1]
            tot_h = tot[:, h:h + 1]
            st = s_all[h]
            y_h = _bdot(cb * lmat, xdt) + jnp.exp(col) * _bdot(cm, st)
            s_heads.append(jnp.exp(tot_h) * st + _bdot_tn(bm * jnp.exp(tot_h - col), xdt))
            y_heads.append(y_h + dsk_ref[:, sl] * xh_h)
    s_ref[...] = jnp.stack(s_heads, axis=0)
    y = jnp.concatenate(y_heads, axis=1) * _silu(z_ref[0].astype(F32))
    gw = dw // SSD_GROUPS
    for gi in range(SSD_GROUPS):
        yg = y[:, gi * gw:(gi + 1) * gw]
        yg = yg * lax.rsqrt(jnp.mean(yg * yg, axis=-1, keepdims=True) + EPS)
        o_ref[0, :, gi * gw:(gi + 1) * gw] = (yg * nw_ref[:, gi * gw:(gi + 1) * gw]).astype(o_ref.dtype)


def _ssd_mix(z, xbc, dtp, conv_w, conv_b, dt_bias, a_log, d_skip, norm_w):
    b, t, dw = z.shape
    ln = SSD_CHUNK
    xw = xbc.shape[2]
    pad = lambda vec: jnp.pad(vec, (0, LANES - vec.shape[0])).reshape(1, LANES)
    dsk = jnp.repeat(d_skip, SSD_HEAD).reshape(1, dw)
    prm = [conv_w, conv_b.reshape(1, xw), pad(dt_bias), pad(a_log), dsk, norm_w.reshape(1, dw)]
    full = lambda arr: pl.BlockSpec(arr.shape, lambda i, j: (0,) * arr.ndim)
    blk = lambda w: pl.BlockSpec((1, ln, w), lambda i, j: (i, j, 0))
    halo = 2 * SUBLANES
    return pl.pallas_call(
        _ssd_kernel, grid=(b, t // ln),
        in_specs=[blk(dw), blk(xw),
                  pl.BlockSpec((1, halo, xw), lambda i, j: (i, jnp.maximum(j * (ln // halo) - 1, 0), 0)),
                  blk(LANES)] + [full(x) for x in prm],
        out_specs=blk(dw), out_shape=jax.ShapeDtypeStruct((b, t, dw), BF16),
        scratch_shapes=[pltpu.VMEM((SSD_HEADS, SSD_STATE, SSD_HEAD), F32)],
        compiler_params=_cparams(("parallel", "arbitrary")), name="ssd",
    )(z, xbc, xbc, dtp, *prm)


def _mix_out1_kernel(x_ref, yc_ref, yd_ref, gw_ref, gb_ref, wc_ref, wd_ref, gpost_ref, gpre_ref,
                     wrh_ref, wrl_ref, x1_ref, h_ref, idx_ref, gate_ref):
    yc = _gelu_tanh(yc_ref[...].astype(F32))
    yc = yc * _sigmoid(_bdot(yc, gw_ref[...]) + gb_ref[...])
    y = _bdot(yc, wc_ref[...]) + _bdot(yd_ref[...], wd_ref[...])
    x1 = x_ref[...] + _rms(y, gpost_ref[...])
    x1_ref[...] = x1
    h = _rms(x1, gpre_ref[...])
    h_ref[...] = h
    hh = h.astype(BF16)
    hl = (h - hh.astype(F32)).astype(BF16)
    wrh = wrh_ref[...]
    logits = (jnp.dot(hh, wrh, preferred_element_type=F32) + jnp.dot(hl, wrh, preferred_element_type=F32)
              + jnp.dot(hh, wrl_ref[...], preferred_element_type=F32))
    lane = _iota2(logits.shape, 1)
    lane_f = lane.astype(F32)
    logits = jnp.where(lane < MOE_EXPERTS, logits, -jnp.inf)
    m1 = jnp.max(logits, axis=-1, keepdims=True)
    i1 = jnp.min(jnp.where(logits == m1, lane_f, float(LANES)), axis=-1, keepdims=True)
    rest = jnp.where(lane_f == i1, -jnp.inf, logits)
    m2 = jnp.max(rest, axis=-1, keepdims=True)
    i2 = jnp.min(jnp.where(rest == m2, lane_f, float(LANES)), axis=-1, keepdims=True)
    e2 = jnp.exp(m2 - m1)
    g1 = 1.0 / (1.0 + e2)
    g2 = e2 / (1.0 + e2)
    idx_ref[...] = jnp.where(lane == 0, i1, jnp.where(lane == 1, i2, 0.0)).astype(jnp.int32)
    gate_ref[...] = jnp.where(lane == 0, g1, jnp.where(lane == 1, g2, 0.0))


def _mix_out1(x2, yc, yd, glu_w, glu_b, wc, wd, gpost, gpre, wr, tm=512):
    n, d = x2.shape
    wr_p = jnp.pad(wr, ((0, 0), (0, LANES - wr.shape[1])))
    wrh = wr_p.astype(BF16)
    wrl = (wr_p - wrh.astype(F32)).astype(BF16)
    row = lambda w: pl.BlockSpec((tm, w), lambda i: (i, 0))
    full = lambda arr: pl.BlockSpec(arr.shape, lambda i: (0,) * arr.ndim)
    prm = [glu_w, glu_b, wc, wd, gpost, gpre, wrh, wrl]
    return pl.pallas_call(
        _mix_out1_kernel, grid=(n // tm,),
        in_specs=[row(d), row(yc.shape[1]), row(yd.shape[1])] + [full(p) for p in prm],
        out_specs=[row(d), row(d), row(LANES), row(LANES)],
        out_shape=[jax.ShapeDtypeStruct((n, d), F32), jax.ShapeDtypeStruct((n, d), F32),
                   jax.ShapeDtypeStruct((n, LANES), jnp.int32), jax.ShapeDtypeStruct((n, LANES), F32)],
        compiler_params=_cparams(("parallel",)), name="mix_out1",
    )(x2, yc, yd, *prm)


GATHER_UNROLL = 8


def _gather_rows(n_rows, make_copy):
    def body(j, carry):
        for q in range(GATHER_UNROLL):
            make_copy(j * GATHER_UNROLL + q).start(priority=q % 2)
        return carry

    lax.fori_loop(0, n_rows // GATHER_UNROLL, body, 0)


def _moe_kernel(nf, be_ref, tok_ref, dst_ref, nact_ref, h_hbm, wg_hbm, wu_hbm, wd_hbm, y_hbm, buf_ref,
                xb_ref, wg_ref, wu_ref, wd_ref, sa_ref, sb_ref, gsem, ssem, wsem):
    i = pl.program_id(0)
    f = pl.program_id(1)
    n_blocks = pl.num_programs(0)
    tm = buf_ref.shape[1]
    nact = nact_ref[0]
    active = i < nact
    slot = lax.rem(i, 2)
    other = 1 - slot
    xs = lambda sl: buf_ref.at[sl]
    yb = lambda sl: buf_ref.at[2 + sl]

    def gather_copy(block, sl, r):
        tok = tok_ref[block * tm + r]
        return pltpu.make_async_copy(h_hbm.at[pl.ds(tok, 1)], buf_ref.at[sl, pl.ds(r, 1)], gsem.at[sl])

    def scatter_copy(block, sl, r):
        dst = dst_ref[block * tm + r]
        return pltpu.make_async_copy(buf_ref.at[2 + sl, pl.ds(r, 1)], y_hbm.at[pl.ds(dst, 1)], ssem.at[sl])

    def wait_rows(sem_slot_ref, buf):
        pltpu.make_async_copy(h_hbm.at[pl.ds(0, tm)], buf, sem_slot_ref).wait()

    @pl.when(f == 0)
    def _():
        @pl.when(i == 0)
        def _():
            _gather_rows(tm, functools.partial(gather_copy, 0, 0))
            buf_ref[3] = jnp.zeros(buf_ref.shape[1:], F32)
            n_real = y_hbm.shape[0] - 2 * tm
            for half in range(2):
                init = pltpu.make_async_copy(yb(1), y_hbm.at[pl.ds(n_real + half * tm, tm)], ssem.at[0])
                init.start()
                init.wait()

        @pl.when(i <= nact)
        def _():
            wait_rows(gsem.at[slot], xs(slot))

        @pl.when(jnp.logical_and(i >= 1, i <= nact))
        def _():
            wait_rows(ssem.at[slot], yb(slot))

        @pl.when(i == nact)
        def _():
            _gather_rows(tm, functools.partial(scatter_copy, i - 1, other))
            wait_rows(ssem.at[other], yb(other))

        @pl.when(active)
        def _():
            xb_ref[...] = buf_ref[slot].astype(BF16)
            buf_ref[2 + slot] = jnp.zeros(buf_ref.shape[1:], F32)

        e = be_ref[i]
        e_before = be_ref[jnp.maximum(i - 1, 0)]

        @pl.when(jnp.logical_and(active, jnp.logical_or(i == 0, e != e_before)))
        def _():
            nch = wg_ref.shape[1] // MXU_TILE
            pieces = []
            for c in range(nch):
                cols = slice(c * MXU_TILE, (c + 1) * MXU_TILE)
                pieces.append((wg_hbm.at[e, :, cols], sa_ref, wg_ref, (slice(None), cols)))
                pieces.append((wu_hbm.at[e, :, cols], sa_ref, wu_ref, (slice(None), cols)))
                pieces.append((wd_hbm.at[e, cols, :], sb_ref, wd_ref, (cols, slice(None))))
            slots, seen = [], {id(sa_ref): 0, id(sb_ref): 0}
            for src, stage, _, _ in pieces:
                slots.append(seen[id(stage)] % 2)
                seen[id(stage)] += 1
            copies = [pltpu.make_async_copy(src, stage.at[sl], wsem.at[(0 if stage is sa_ref else 2) + sl])
                      for (src, stage, _, _), sl in zip(pieces, slots)]
            ahead = 3
            for p in range(min(ahead, len(copies))):
                copies[p].start()
            for p, (_, stage, dst, where) in enumerate(pieces):
                copies[p].wait()
                dst[where] = stage[slots[p]].astype(BF16)
                if p + ahead < len(copies):
                    copies[p + ahead].start()

    @pl.when(active)
    def _():
        x = xb_ref[...]
        nch = wg_ref.shape[1] // MXU_TILE
        rows_f = tm // nf
        base = pl.multiple_of(f * rows_f, SUBLANES)
        prev = jnp.where(i == 0, n_blocks - 1, i - 1)
        for c in range(nch):
            for r in range(rows_f * c // nch, rows_f * (c + 1) // nch):
                gather_copy(i + 1, other, base + r).start(priority=1)
                scatter_copy(prev, other, base + r).start(priority=1)
            cols = slice(c * MXU_TILE, (c + 1) * MXU_TILE)
            gate = jnp.dot(x, wg_ref[:, cols], preferred_element_type=F32)
            up = jnp.dot(x, wu_ref[:, cols], preferred_element_type=F32)
            buf_ref[2 + slot] += jnp.dot((_silu(gate) * up).astype(BF16), wd_ref[cols, :],
                                         preferred_element_type=F32)


def _moe_experts(h, block_expert, slot_tok, slot_dst, nact, n_rows, wg, wu, wd):
    n, d = h.shape
    tm = MOE_ROWS
    n_blocks = slot_tok.shape[0] // tm
    ff = wg.shape[2]
    nf = 1
    anywhere = pl.BlockSpec(memory_space=pl.ANY)
    grid_spec = pltpu.PrefetchScalarGridSpec(
        num_scalar_prefetch=4, grid=(n_blocks, nf),
        in_specs=[anywhere, anywhere, anywhere, anywhere],
        out_specs=pl.BlockSpec(memory_space=pl.ANY),
        scratch_shapes=[pltpu.VMEM((4, tm, d), F32), pltpu.VMEM((tm, d), BF16),
                        pltpu.VMEM((d, ff), BF16), pltpu.VMEM((d, ff), BF16), pltpu.VMEM((ff, d), BF16),
                        pltpu.VMEM((2, d, MXU_TILE), F32), pltpu.VMEM((2, MXU_TILE, d), F32),
                        pltpu.SemaphoreType.DMA((2,)), pltpu.SemaphoreType.DMA((2,)),
                        pltpu.SemaphoreType.DMA((4,))])
    return pl.pallas_call(
        functools.partial(_moe_kernel, nf), grid_spec=grid_spec,
        out_shape=jax.ShapeDtypeStruct((n_rows, d), F32),
        compiler_params=pltpu.CompilerParams(dimension_semantics=("arbitrary", "arbitrary"),
                                             vmem_limit_bytes=VMEM_LIMIT, disable_bounds_checks=True),
        name="moe_experts",
    )(block_expert, slot_tok, slot_dst, nact, h, wg, wu, wd)


def _combine_kernel(x_ref, y0_ref, y1_ref, gate_ref, gpost_ref, o_ref):
    gates = gate_ref[...]
    y = gates[:, 0:1] * y0_ref[...] + gates[:, 1:2] * y1_ref[...]
    o_ref[...] = x_ref[...] + _rms(y, gpost_ref[...])


def _moe_combine(x1, y, gates, gpost, tm=512):
    n, d = x1.shape
    nt = n // tm
    return pl.pallas_call(
        _combine_kernel, grid=(nt,),
        in_specs=[pl.BlockSpec((tm, d), lambda i: (i, 0)), pl.BlockSpec((tm, d), lambda i: (i, 0)),
                  pl.BlockSpec((tm, d), lambda i: (nt + i, 0)),
                  pl.BlockSpec((tm, LANES), lambda i: (i, 0)), pl.BlockSpec((1, d), lambda i: (0, 0))],
        out_specs=pl.BlockSpec((tm, d), lambda i: (i, 0)),
        out_shape=jax.ShapeDtypeStruct((n, d), F32),
        compiler_params=_cparams(("parallel",)), name="moe_combine",
    )(x1, y, y, gates, gpost)


def _moe_plan(idx, n):
    tm = MOE_ROWS
    flat_e = idx[:, :2].reshape(-1)
    onehot = (flat_e[:, None] == jnp.arange(MOE_EXPERTS, dtype=jnp.int32)[None, :]).astype(jnp.int32)
    csum = jnp.cumsum(onehot, axis=0)
    counts = csum[-1]
    rank = jnp.sum((csum - onehot) * onehot, axis=1)
    padded = (counts + tm - 1) // tm * tm
    pend = jnp.cumsum(padded)
    pstart = pend - padded
    dest = (jnp.sum(onehot * pstart[None, :], axis=1) + rank).astype(jnp.int32)
    n_blocks = (2 * n) // tm + MOE_EXPERTS + 1
    n_slots = n_blocks * tm
    slot_pair = jnp.full((n_slots,), -1, jnp.int32).at[dest].set(
        jnp.arange(2 * n, dtype=jnp.int32), unique_indices=True)
    real = slot_pair >= 0
    slot_tok = jnp.where(real, slot_pair // 2, 0)
    s_id = jnp.arange(n_slots, dtype=jnp.int32)
    slot_dst = jnp.where(real, slot_pair % 2 * n + slot_pair // 2, 2 * n + (s_id // tm) % 2 * tm + s_id % tm)
    block_start = jnp.arange(n_blocks, dtype=jnp.int32) * tm
    block_expert = jnp.minimum(jnp.sum((block_start[:, None] >= pend[None, :]).astype(jnp.int32), axis=1),
                               MOE_EXPERTS - 1)
    nact = (pend[-1] // tm).astype(jnp.int32).reshape(1)
    return block_expert, slot_tok, slot_dst, nact, 2 * n + 2 * tm


def kernel(x, l0_norm_pre_mix, l0_w_in, l0_rwkv_mu, l0_rwkv_w0, l0_rwkv_w2, l0_rwkv_a0, l0_rwkv_a2, l0_rwkv_g2, l0_rwkv_k_k, l0_rwkv_k_a, l0_rwkv_r_k, l0_rwkv_ln_w, l0_rwkv_ln_b, l0_gmlp_ln_w, l0_gmlp_ln_b, l0_gmlp_ws, l0_gmlp_bs, l0_w_out, l0_norm_post_mix, l0_norm_pre_ffn, l0_ffn_w_gate, l0_ffn_w_up, l0_ffn_w_down, l0_norm_post_ffn, l1_norm_pre_mix, l1_w_in, l1_s5_a_re, l1_s5_a_im, l1_s5_log_dt, l1_s5_b_re, l1_s5_b_im, l1_s5_c_re, l1_s5_c_im, l1_s5_d, l1_s5_glu_w, l1_s5_glu_b, l1_m2_conv_w, l1_m2_conv_b, l1_m2_dt_bias, l1_m2_a_log, l1_m2_d, l1_m2_norm_w, l1_w_out, l1_norm_post_mix, l1_norm_pre_ffn, l1_moe_router, l1_moe_w_gate, l1_moe_w_up, l1_moe_w_down, l1_norm_post_ffn):
    b, t, d = x.shape
    n = b * t
    x2 = x.reshape(n, d)
    row = lambda vec: vec.reshape(1, -1)

    aw = l0_rwkv_w0.shape[0]
    heads = aw // RWKV_HEAD
    lw_, la_, lg_ = l0_rwkv_w2.shape[0], l0_rwkv_a2.shape[0], l0_rwkv_g2.shape[0]
    a_in = 3 * aw + lw_ + la_ + lg_
    padc = lambda m, wdt: jnp.pad(m, ((0, 0), (0, LANES - wdt)))
    o = 3 * aw
    w_a = jnp.concatenate([l0_w_in[:, :o], padc(l0_w_in[:, o:o + lw_], lw_),
                           padc(l0_w_in[:, o + lw_:o + lw_ + la_], la_),
                           padc(l0_w_in[:, o + lw_ + la_:a_in], lg_)], axis=1).astype(BF16)
    w_b = l0_w_in[:, a_in:].astype(BF16)
    p_a, p_b = _norm_proj(x2, l0_norm_pre_mix, [w_a, w_b], [F32, BF16])
    padv = lambda vec, wdt: jnp.pad(vec, (0, LANES - wdt))
    mu = l0_rwkv_mu
    mu_p = jnp.concatenate([mu[:o], padv(mu[o:o + lw_], lw_), padv(mu[o + lw_:o + lw_ + la_], la_),
                            padv(mu[o + lw_ + la_:], lg_)])
    padr = lambda m: jnp.pad(m, ((0, LANES - m.shape[0]), (0, 0))).astype(BF16)
    hid = jnp.arange(LANES, dtype=jnp.int32) // RWKV_HEAD
    gsum = (hid[:, None] == hid[None, :]).astype(BF16)
    rwkv_prm = [row(mu_p), row(l0_rwkv_w0), padr(l0_rwkv_w2), row(l0_rwkv_a0), padr(l0_rwkv_a2),
                padr(l0_rwkv_g2), row(l0_rwkv_k_k), row(l0_rwkv_k_a), row(l0_rwkv_r_k),
                row(l0_rwkv_ln_w), row(l0_rwkv_ln_b), gsum]
    ya = _rwkv_mix(p_a.reshape(b, t, -1), rwkv_prm, heads)
    yb = _gmlp_mix(p_b.reshape(b, t, -1), l0_gmlp_ln_w, l0_gmlp_ln_b, l0_gmlp_ws, l0_gmlp_bs)
    wo = l0_w_out.astype(BF16)
    x2 = _mix_out0(x2, ya.reshape(n, -1), yb.reshape(n, -1), wo[:aw], wo[aw:], row(l0_norm_post_mix))
    x2 = _ffn(x2, row(l0_norm_pre_ffn), l0_ffn_w_gate.astype(BF16), l0_ffn_w_up.astype(BF16),
              l0_ffn_w_down.astype(BF16), row(l0_norm_post_ffn))

    cw = l1_s5_d.shape[0]
    dw = l1_m2_norm_w.shape[0]
    xw = l1_m2_conv_w.shape[1]
    nh = l1_m2_dt_bias.shape[0]
    w1 = l1_w_in
    w_parts = [w1[:, :cw], w1[:, cw:cw + dw], w1[:, cw + dw:cw + dw + xw],
               padc(w1[:, cw + dw + xw:], nh)]
    u_c, z_d, xbc, dtp = _norm_proj(x2, l1_norm_pre_mix, [w.astype(BF16) for w in w_parts],
                                    [BF16, BF16, BF16, F32])
    s5_w = _s5_weights(l1_s5_a_re, l1_s5_a_im, l1_s5_log_dt, l1_s5_b_re, l1_s5_b_im, l1_s5_c_re,
                       l1_s5_c_im, l1_s5_d, S5_CHUNK)
    yc = _s5_core(u_c.reshape(b, t, cw), s5_w)
    yd = _ssd_mix(z_d.reshape(b, t, dw), xbc.reshape(b, t, xw), dtp.reshape(b, t, LANES),
                  l1_m2_conv_w, l1_m2_conv_b, l1_m2_dt_bias, l1_m2_a_log, l1_m2_d, l1_m2_norm_w)
    wo1 = l1_w_out.astype(BF16)
    x1, h, idx, gates = _mix_out1(x2, yc.reshape(n, cw), yd.reshape(n, dw), l1_s5_glu_w.astype(BF16),
                                  row(l1_s5_glu_b), wo1[:cw], wo1[cw:], row(l1_norm_post_mix),
                                  row(l1_norm_pre_ffn), l1_moe_router)
    block_expert, slot_tok, slot_dst, nact, n_rows = _moe_plan(idx, n)
    ys = _moe_experts(h, block_expert, slot_tok, slot_dst, nact, n_rows, l1_moe_w_gate, l1_moe_w_up,
                      l1_moe_w_down)
    out = _moe_combine(x1, ys, gates, row(l1_norm_post_ffn))
    return out.reshape(b, t, d)
```

```python
import functools

import jax
import jax.numpy as jnp
from jax import lax
from jax.experimental import pallas as pl
from jax.experimental.pallas import tpu as pltpu

F32 = jnp.float32
BF16 = jnp.bfloat16

EPS = 1e-6
RWKV_GN_EPS = 64e-5
RWKV_HEAD = 64
RWKV_CHUNK = 64
GMLP_CHUNK = 128
GMLP_GROUPS = 4
S5_GROUP_CH = 16
S5_STATE = 64
S5_CHUNK = 64
SSD_HEAD = 64
SSD_HEADS = 8
SSD_GROUPS = 2
SSD_STATE = 128
SSD_CONV = 4
SSD_CHUNK = 128
MOE_EXPERTS = 8
MOE_ROWS = 512
MXU_TILE = 256
LANES = 128
SUBLANES = 8
VMEM_LIMIT = 56 * 1024 * 1024


def _cparams(sem):
    return pltpu.CompilerParams(dimension_semantics=sem, vmem_limit_bytes=VMEM_LIMIT)


def _bdot(a, b):
    return jnp.dot(a.astype(BF16), b.astype(BF16), preferred_element_type=F32)


def _bdot_nt(a, b):
    return lax.dot_general(a.astype(BF16), b.astype(BF16), (((1,), (1,)), ((), ())),
                           preferred_element_type=F32)


def _bdot_tn(a, b):
    return lax.dot_general(a.astype(BF16), b.astype(BF16), (((0,), (0,)), ((), ())),
                           preferred_element_type=F32)


def _split3(x):
    h = x.astype(BF16)
    r1 = x - h.astype(F32)
    m = r1.astype(BF16)
    l = (r1 - m.astype(F32)).astype(BF16)
    return h, m, l


def _dot_x_exact(x, e):
    h, m, l = _split3(x)
    e = e.astype(BF16)
    return (jnp.dot(h, e, preferred_element_type=F32) + jnp.dot(m, e, preferred_element_type=F32)
            + jnp.dot(l, e, preferred_element_type=F32))


def _dot_exact_x(e, x):
    h, m, l = _split3(x)
    e = e.astype(BF16)
    return (jnp.dot(e, h, preferred_element_type=F32) + jnp.dot(e, m, preferred_element_type=F32)
            + jnp.dot(e, l, preferred_element_type=F32))


def _rms(x, g):
    return x * lax.rsqrt(jnp.mean(x * x, axis=-1, keepdims=True) + EPS) * g


def _sigmoid(x):
    return 1.0 / (1.0 + jnp.exp(-x))


def _silu(x):
    return x * _sigmoid(x)


def _softplus(x):
    return jnp.maximum(x, 0.0) + jnp.log(1.0 + jnp.exp(-jnp.abs(x)))


def _gelu_tanh(x):
    return 0.5 * x * (1.0 + jnp.tanh(0.7978845608028654 * (x + 0.044715 * x * x * x)))


def _iota2(shape, dim):
    return lax.broadcasted_iota(jnp.int32, shape, dim)


def _norm_proj_kernel(n_out, x_ref, g_ref, *refs):
    w_refs = refs[:n_out]
    o_refs = refs[n_out:]
    xn = _rms(x_ref[...], g_ref[...]).astype(BF16)
    for w_ref, o_ref in zip(w_refs, o_refs):
        o_ref[...] = jnp.dot(xn, w_ref[...], preferred_element_type=F32).astype(o_ref.dtype)


def _norm_proj(x2, g, ws, dtypes, tm=512):
    n, d = x2.shape
    in_specs = [pl.BlockSpec((tm, d), lambda i: (i, 0)), pl.BlockSpec((1, d), lambda i: (0, 0))]
    in_specs += [pl.BlockSpec(w.shape, lambda i: (0, 0)) for w in ws]
    out_specs = [pl.BlockSpec((tm, w.shape[1]), lambda i: (i, 0)) for w in ws]
    out_shape = [jax.ShapeDtypeStruct((n, w.shape[1]), dt) for w, dt in zip(ws, dtypes)]
    return pl.pallas_call(
        functools.partial(_norm_proj_kernel, len(ws)),
        grid=(n // tm,), in_specs=in_specs, out_specs=out_specs, out_shape=out_shape,
        compiler_params=_cparams(("parallel",)), name="norm_proj",
    )(x2, g.reshape(1, d), *ws)


PRE_NAMES = ("v", "g", "bonus", "rt", "kt", "bt", "at", "bh", "kh")


def _rwkv_kernel(heads, nb, p_ref, pp_ref, mu_ref, w0_ref, w2_ref, a0_ref, a2_ref, g2_ref, kk_ref,
                 ka_ref, rk_ref, lnw_ref, lnb_ref, gs_ref, o_ref, z_ref, pre_ref, wl_ref):
    c = pl.program_id(1)
    ln = RWKV_CHUNK
    hd = RWKV_HEAD
    aw = heads * hd

    @pl.when(c == 0)
    def _():
        z_ref[...] = jnp.zeros_like(z_ref)
        pre_ref[...] = jnp.zeros_like(pre_ref)
        wl_ref[...] = jnp.zeros_like(wl_ref)

    tril_f = jnp.where(_iota2((ln, ln), 0) >= _iota2((ln, ln), 1), 1.0, 0.0)
    rows = _iota2((ln, 1), 0)
    gs_tile = gs_ref[...]

    def gs(x):
        nt = x.shape[1] // LANES
        stacked = jnp.concatenate([x[:, j * LANES:(j + 1) * LANES] for j in range(nt)], axis=0)
        red = _dot_x_exact(stacked, gs_tile)
        return jnp.concatenate([red[j * ln:(j + 1) * ln] for j in range(nt)], axis=1)

    pre_idx = {nm: idx for idx, nm in enumerate(PRE_NAMES)}
    pending = []

    def prep_steps():
        for bi in range(nb):
            p = p_ref[bi]
            prev = jnp.where(c == 0, 0.0, pp_ref[bi][SUBLANES - 1:SUBLANES, :])
            ps = jnp.where(rows == 0, prev, pltpu.roll(p, 1, axis=0))
            pm = p + (ps - p) * mu_ref[...]
            r = pm[:, 0:aw]
            k = pm[:, aw:2 * aw]
            v = pm[:, 2 * aw:3 * aw]
            xw = pm[:, 3 * aw:3 * aw + LANES]
            xa = pm[:, 3 * aw + LANES:3 * aw + 2 * LANES]
            xg = pm[:, 3 * aw + 2 * LANES:3 * aw + 3 * LANES]
            yield
            w = w0_ref[...] + _bdot(jnp.tanh(xw), w2_ref[...])
            a = _sigmoid(a0_ref[...] + _bdot(xa, a2_ref[...]))
            g = _bdot(_sigmoid(xg), g2_ref[...])
            yield
            w = -_softplus(-w) - 0.5
            lw = -jnp.exp(w)
            kk = k * kk_ref[...]
            kk_ss = gs(kk * kk)
            yield
            cs = _dot_exact_x(tril_f, lw)
            kk = kk / jnp.maximum(jnp.sqrt(kk_ss), 1e-12)
            kmod = k * (1.0 + (a - 1.0) * ka_ref[...])
            yield
            bonus = gs(r * kmod * rk_ref[...])
            bvec = kk * a
            cs_last = cs[ln - 1:ln, :]
            encs = jnp.exp(-cs)
            yield
            dec_end = jnp.exp(cs_last - cs)
            nxt = dict(v=v, g=g, bonus=bonus, rt=r * jnp.exp(cs), kt=kmod * encs, bt=bvec * encs,
                       at=-kk * jnp.exp(cs - lw), bh=bvec * dec_end, kh=kmod * dec_end)
            pending.append((bi, nxt, jnp.broadcast_to(jnp.exp(cs_last), (SUBLANES, aw))))
            yield

    prep = prep_steps()
    tick = lambda: next(prep, None)

    lane = _iota2((ln, LANES), 1)
    lane_in = jnp.where(lane >= hd, lane - hd, lane)
    trow = _iota2((ln, LANES), 0)
    left = lane < hd
    tril_p = lane_in <= trow
    stril_p = lane_in < trow
    eye_p = lane_in == trow
    eye_pf = jnp.where(eye_p, 1.0, 0.0)

    def bd(x):
        xb = x.astype(BF16)
        zero = jnp.zeros_like(xb)
        return jnp.concatenate([jnp.where(left, xb, zero), jnp.where(left, zero, xb)], axis=0)

    def dot(a, b):
        return jnp.dot(a.astype(BF16), b, preferred_element_type=F32)

    npair = heads // 2
    pairs = [(bi, j) for bi in range(nb) for j in range(npair)]

    class _Tiles:
        def __init__(self, name):
            self.idx = pre_idx[name]

        def __getitem__(self, i):
            bi, j = pairs[i]
            return pre_ref[self.idx, bi, :, j * LANES:(j + 1) * LANES]

    at, rt, bt, kt, vv, bh, kh = (_Tiles(n) for n in ("at", "rt", "bt", "kt", "v", "bh", "kh"))
    wl = [wl_ref[bi, 0:1, j * LANES:(j + 1) * LANES] for bi, j in pairs]
    z_all = z_ref[...]
    zs = [z_all[bi, j] for bi, j in pairs]
    npr = range(len(pairs))
    lhs = [jnp.concatenate([at[i], rt[i]], axis=0).astype(BF16) for i in npr]
    abk = [lax.dot_general(lhs[i], jnp.concatenate([bd(bt[i]), bd(kt[i])], axis=0),
                           (((1,), (1,)), ((), ())), preferred_element_type=F32) for i in npr]
    ab = [abk[i][:, :LANES] for i in npr]
    ak = [abk[i][:, LANES:] for i in npr]
    tick()
    nmat = [jnp.where(stril_p, ab[i][:ln], 0.0) for i in npr]
    tinv = [eye_pf + nmat[i] for i in npr]
    npow = [dot(nmat[i], bd(nmat[i])) for i in npr]
    tick()
    for step in range(5):
        bdn = [bd(npow[i]) for i in npr]
        if step < 4:
            both = [dot(jnp.concatenate([tinv[i], npow[i]], axis=0), bdn[i]) for i in npr]
            tinv = [tinv[i] + both[i][:ln] for i in npr]
            npow = [both[i][ln:] for i in npr]
        else:
            tinv = [tinv[i] + dot(tinv[i], bdn[i]) for i in npr]
        tick()
    bdv = [bd(vv[i]) for i in npr]
    bdz = [bd(zs[i]) for i in npr]
    xmat = [dot(jnp.concatenate([jnp.where(stril_p, ak[i][:ln], 0.0), at[i]], axis=1),
                jnp.concatenate([bdv[i], bdz[i]], axis=0)) for i in npr]
    tick()
    u = [dot(tinv[i], bd(xmat[i])) for i in npr]
    tick()
    ys_p = [dot(jnp.concatenate([rt[i], jnp.where(tril_p, ab[i][ln:], 0.0),
                                 jnp.where(tril_p, ak[i][ln:], 0.0)], axis=1),
                jnp.concatenate([bdz[i], bd(u[i]), bdv[i]], axis=0)) for i in npr]
    tick()
    cross = [_bdot_tn(jnp.concatenate([bh[i], kh[i]], axis=0), jnp.concatenate([u[i], vv[i]], axis=0))
             for i in npr]
    tick()
    z_new = []
    for i in npr:
        dg = jnp.where(eye_p, wl[i], 0.0)
        wl_i = jnp.sum(jnp.where(left, dg, 0.0), axis=1, keepdims=True)
        wl_j = jnp.sum(jnp.where(left, 0.0, dg), axis=1, keepdims=True)
        z_new.append(jnp.where(left, wl_i, wl_j) * zs[i] + jnp.where(left, cross[i][:ln], cross[i][ln:]))
    z_ref[...] = jnp.stack(z_new, axis=0).reshape(z_ref.shape)

    tick()
    inv = 1.0 / hd
    for bi in range(nb):
        y = jnp.concatenate(ys_p[bi * npair:(bi + 1) * npair], axis=1)
        mean = gs(y) * inv
        d = y - mean
        var = gs(d * d) * inv
        yn = d * lax.rsqrt(var + RWKV_GN_EPS) * lnw_ref[...] + lnb_ref[...]
        o_ref[bi] = ((yn + pre_ref[pre_idx["bonus"], bi] * pre_ref[pre_idx["v"], bi])
                     * pre_ref[pre_idx["g"], bi]).astype(o_ref.dtype)
        tick()
    for _ in prep:
        pass
    for bi, nxt, wl_next in pending:
        for nm, idx in pre_idx.items():
            pre_ref[idx, bi] = nxt[nm]
        wl_ref[bi] = wl_next


def _rwkv_mix(p_a, prm, heads, nb=2):
    b, t, cin = p_a.shape
    aw = heads * RWKV_HEAD
    ln = RWKV_CHUNK
    nc = t // ln
    sub = ln // SUBLANES
    full = lambda arr: pl.BlockSpec(arr.shape, lambda i, j: (0,) * arr.ndim)
    in_specs = [pl.BlockSpec((nb, ln, cin), lambda i, j: (i, jnp.minimum(j, nc - 1), 0)),
                pl.BlockSpec((nb, SUBLANES, cin),
                             lambda i, j: (i, jnp.maximum(jnp.minimum(j, nc - 1) * sub - 1, 0), 0))]
    in_specs += [full(x) for x in prm]
    return pl.pallas_call(
        functools.partial(_rwkv_kernel, heads, nb),
        grid=(b // nb, nc + 1), in_specs=in_specs,
        out_specs=pl.BlockSpec((nb, ln, aw), lambda i, j: (i, jnp.maximum(j - 1, 0), 0)),
        out_shape=jax.ShapeDtypeStruct((b, t, aw), BF16),
        scratch_shapes=[pltpu.VMEM((nb, heads // 2, RWKV_HEAD, 2 * RWKV_HEAD), F32),
                        pltpu.VMEM((len(PRE_NAMES), nb, ln, aw), F32),
                        pltpu.VMEM((nb, SUBLANES, aw), F32)],
        compiler_params=_cparams(("parallel", "arbitrary")), name="rwkv7",
    )(p_a, p_a, *prm)


def _gmlp_kernel(p_ref, lnw_ref, lnb_ref, ws_ref, bs_ref, o_ref):
    ln = GMLP_CHUNK
    bw = p_ref.shape[2] // 2
    gd = bw // GMLP_GROUPS
    tril = _iota2((ln, ln), 0) >= _iota2((ln, ln), 1)
    ws_c = [jnp.where(tril, ws_ref[gi], 0.0).astype(BF16) for gi in range(GMLP_GROUPS)]
    for ci in range(p_ref.shape[1] // ln):
        x = _gelu_tanh(p_ref[0, ci * ln:(ci + 1) * ln, :].astype(F32))
        for gi in range(GMLP_GROUPS):
            u = x[:, gi * gd:(gi + 1) * gd]
            v = x[:, bw + gi * gd:bw + (gi + 1) * gd]
            mean = jnp.mean(v, axis=-1, keepdims=True)
            d = v - mean
            var = jnp.mean(d * d, axis=-1, keepdims=True)
            vn = d * lax.rsqrt(var + EPS) * lnw_ref[gi:gi + 1, :] + lnb_ref[gi:gi + 1, :]
            s = jnp.dot(ws_c[gi], vn.astype(BF16), preferred_element_type=F32) + bs_ref[gi]
            o_ref[0, ci * ln:(ci + 1) * ln, gi * gd:(gi + 1) * gd] = (u * s).astype(o_ref.dtype)


def _gmlp_mix(p_b, ln_w, ln_b, ws, bs, rows=512):
    b, t, cin = p_b.shape
    bw = cin // 2
    gd = bw // GMLP_GROUPS
    bs_b = jnp.broadcast_to(bs[:, :, None], (GMLP_GROUPS, GMLP_CHUNK, gd))
    full = lambda arr: pl.BlockSpec(arr.shape, lambda i, j: (0,) * arr.ndim)
    return pl.pallas_call(
        _gmlp_kernel, grid=(b, t // rows),
        in_specs=[pl.BlockSpec((1, rows, cin), lambda i, j: (i, j, 0)),
                  full(ln_w), full(ln_b), full(ws), full(bs_b)],
        out_specs=pl.BlockSpec((1, rows, bw), lambda i, j: (i, j, 0)),
        out_shape=jax.ShapeDtypeStruct((b, t, bw), BF16),
        compiler_params=_cparams(("parallel", "parallel")), name="gmlp",
    )(p_b, ln_w, ln_b, ws, bs_b)


def _mix_out0_kernel(x_ref, ya_ref, yb_ref, wa_ref, wb_ref, g_ref, o_ref):
    y = _bdot(ya_ref[...], wa_ref[...]) + _bdot(yb_ref[...], wb_ref[...])
    o_ref[...] = x_ref[...] + _rms(y, g_ref[...])


def _mix_out0(x2, ya, yb, wa, wb, g, tm=512):
    n, d = x2.shape
    row = lambda arr: pl.BlockSpec((tm, arr.shape[1]), lambda i: (i, 0))
    full = lambda arr: pl.BlockSpec(arr.shape, lambda i: (0,) * arr.ndim)
    return pl.pallas_call(
        _mix_out0_kernel, grid=(n // tm,),
        in_specs=[row(x2), row(ya), row(yb), full(wa), full(wb), full(g)],
        out_specs=row(x2), out_shape=jax.ShapeDtypeStruct((n, d), F32),
        compiler_params=_cparams(("parallel",)), name="mix_out0",
    )(x2, ya, yb, wa, wb, g)


def _ffn_kernel(fc, x_ref, gpre_ref, wg_ref, wu_ref, wd_ref, gpost_ref, o_ref):
    h = _rms(x_ref[...], gpre_ref[...]).astype(BF16)
    acc = None
    for c in range(wg_ref.shape[1] // fc):
        cols = slice(c * fc, (c + 1) * fc)
        gate = jnp.dot(h, wg_ref[:, cols], preferred_element_type=F32)
        up = jnp.dot(h, wu_ref[:, cols], preferred_element_type=F32)
        part = jnp.dot((_silu(gate) * up).astype(BF16), wd_ref[cols, :], preferred_element_type=F32)
        acc = part if acc is None else acc + part
    o_ref[...] = x_ref[...] + _rms(acc, gpost_ref[...])


def _ffn(x2, gpre, wg, wu, wd, gpost, tm=512, fc=MXU_TILE):
    n, d = x2.shape
    full = lambda arr: pl.BlockSpec(arr.shape, lambda i: (0,) * arr.ndim)
    return pl.pallas_call(
        functools.partial(_ffn_kernel, fc), grid=(n // tm,),
        in_specs=[pl.BlockSpec((tm, d), lambda i: (i, 0)), full(gpre), full(wg), full(wu), full(wd),
                  full(gpost)],
        out_specs=pl.BlockSpec((tm, d), lambda i: (i, 0)),
        out_shape=jax.ShapeDtypeStruct((n, d), F32),
        compiler_params=_cparams(("parallel",)), name="ffn",
    )(x2, gpre, wg, wu, wd, gpost)


def _s5_kernel(nc, nb, u_ref, tap_ref, wsr_ref, wsi_ref, wcr_ref, wci_ref, alr_ref, ali_ref, d_ref, o_ref,
               toep_ref):
    u = u_ref[0]
    taps = tap_ref[0]
    ch = taps.shape[0]
    lane = _iota2(taps.shape, 1)
    for s in range(taps.shape[1] // ch):
        blk = taps if s == 0 else jnp.where(lane >= ch * s, pltpu.roll(taps, ch * s, axis=1), 0.0)
        toep_ref[ch * s:ch * (s + 1), :] = blk.astype(BF16)
    y = jnp.dot(u, toep_ref[...], preferred_element_type=F32)
    xer = jnp.dot(u, wsr_ref[0], preferred_element_type=F32)
    xei = jnp.dot(u, wsi_ref[0], preferred_element_type=F32)
    alr = alr_ref[0]
    ali = ali_ref[0]
    cr = jnp.zeros((nb, xer.shape[1]), F32)
    ci = jnp.zeros((nb, xer.shape[1]), F32)
    prs, pis = [], []
    for c in range(nc):
        prs.append(cr)
        pis.append(ci)
        er = xer[c * nb:(c + 1) * nb]
        ei = xei[c * nb:(c + 1) * nb]
        cr, ci = alr * cr - ali * ci + er, alr * ci + ali * cr + ei
    pr = jnp.concatenate(prs, axis=0)
    pi = jnp.concatenate(pis, axis=0)
    y = y + _bdot(pr, wcr_ref[0]) + _bdot(pi, wci_ref[0])
    o_ref[0] = (y + d_ref[0] * u.astype(F32)).astype(o_ref.dtype)


def _s5_weights(a_re, a_im, log_dt, b_re, b_im, c_re, c_im, d_skip, ln):
    g, st = a_re.shape
    ch = b_re.shape[2]
    dt = jnp.exp(log_dt)[:, None]
    lr, li = a_re, a_im
    tau = jnp.arange(ln + 1, dtype=F32)[:, None, None]
    mag = jnp.exp(lr[None] * dt[None] * tau)
    pw_r = mag * jnp.cos(li[None] * dt[None] * tau)
    pw_i = mag * jnp.sin(li[None] * dt[None] * tau)
    ab_r, ab_i = pw_r[1], pw_i[1]
    nr, ni = ab_r - 1.0, ab_i
    den = lr * lr + li * li
    fr, fi = (nr * lr + ni * li) / den, (ni * lr - nr * li) / den
    bb_r = fr[..., None] * b_re - fi[..., None] * b_im
    bb_i = fr[..., None] * b_im + fi[..., None] * b_re
    cp_r = c_re[None] * pw_r[:ln, :, None, :] - c_im[None] * pw_i[:ln, :, None, :]
    cp_i = c_re[None] * pw_i[:ln, :, None, :] + c_im[None] * pw_r[:ln, :, None, :]
    hp = lax.Precision.HIGHEST
    taps = (jnp.einsum('tgcp,gpd->gdtc', cp_r, bb_r, precision=hp)
            - jnp.einsum('tgcp,gpd->gdtc', cp_i, bb_i, precision=hp))
    taps = taps.reshape(g, ch, ln * ch)
    rev_r, rev_i = pw_r[:ln][::-1], pw_i[:ln][::-1]
    ws_r = rev_r[..., None] * bb_r[None] - rev_i[..., None] * bb_i[None]
    ws_i = rev_r[..., None] * bb_i[None] + rev_i[..., None] * bb_r[None]
    ws_r = ws_r.transpose(1, 0, 3, 2).reshape(g, ln * ch, st)
    ws_i = ws_i.transpose(1, 0, 3, 2).reshape(g, ln * ch, st)
    q_r, q_i = pw_r[1:ln + 1], pw_i[1:ln + 1]
    wc_r = c_re[None] * q_r[:, :, None, :] - c_im[None] * q_i[:, :, None, :]
    wc_i = -(c_re[None] * q_i[:, :, None, :] + c_im[None] * q_r[:, :, None, :])
    wc_r = wc_r.transpose(1, 3, 0, 2).reshape(g, st, ln * ch)
    wc_i = wc_i.transpose(1, 3, 0, 2).reshape(g, st, ln * ch)
    al_r = pw_r[ln].reshape(g, 1, st)
    al_i = pw_i[ln].reshape(g, 1, st)
    d_t = jnp.tile(d_skip.reshape(g, 1, ch), (1, ln, 1)).reshape(g, 1, ln * ch)
    return (taps, ws_r.astype(BF16), ws_i.astype(BF16), wc_r.astype(BF16),
            wc_i.astype(BF16), al_r, al_i, d_t)


def _s5_core(u, weights):
    b, t, cw = u.shape
    ln, ch = S5_CHUNK, S5_GROUP_CH
    g = cw // ch
    nc = t // ln
    ug = u.astype(BF16).reshape(b, nc, ln, g, ch).transpose(3, 1, 0, 2, 4).reshape(g, nc * b, ln * ch)
    per_g = lambda arr: pl.BlockSpec((1,) + arr.shape[1:], lambda i: (i, 0, 0))
    yg = pl.pallas_call(
        functools.partial(_s5_kernel, nc, b), grid=(g,),
        in_specs=[per_g(ug)] + [per_g(w) for w in weights],
        out_specs=per_g(ug), out_shape=jax.ShapeDtypeStruct(ug.shape, BF16),
        scratch_shapes=[pltpu.VMEM((ln * ch, ln * ch), BF16)],
        compiler_params=_cparams(("parallel",)), name="s5",
    )(ug, *weights)
    return yg.reshape(g, nc, b, ln, ch).transpose(2, 1, 3, 0, 4).reshape(b, t, cw)


def _ssd_kernel(z_ref, xbc_ref, xp_ref, dt_ref, cw_ref, cb_ref, dtb_ref, alog_ref, dsk_ref, nw_ref,
                o_ref, s_ref):
    c = pl.program_id(1)
    ln = SSD_CHUNK
    hd = SSD_HEAD
    dw = SSD_HEADS * hd
    gn = SSD_STATE

    @pl.when(c == 0)
    def _():
        s_ref[...] = jnp.zeros_like(s_ref)

    xbc = xbc_ref[0]
    halo = xp_ref.shape[1]
    prev = jnp.where(c == 0, jnp.zeros_like(xp_ref[0]), xp_ref[0])
    full = jnp.concatenate([prev, xbc], axis=0).astype(BF16)
    conv = cb_ref[...] + cw_ref[SSD_CONV - 1:SSD_CONV, :] * xbc.astype(F32)
    for j in range(SSD_CONV - 1):
        lag = SSD_CONV - 1 - j
        pick = _iota2((ln, halo + ln), 1) == _iota2((ln, halo + ln), 0) + (halo - lag)
        shifted = jnp.dot(jnp.where(pick, 1.0, 0.0).astype(BF16), full, preferred_element_type=F32)
        conv = conv + cw_ref[j:j + 1, :] * shifted
    act = _silu(conv)
    xh = act[:, :dw]
    dt = _softplus(dt_ref[0] + dtb_ref[...])
    adt = -jnp.exp(alog_ref[...]) * dt
    tril = _iota2((ln, ln), 0) >= _iota2((ln, ln), 1)
    acs = _dot_exact_x(jnp.where(tril, 1.0, 0.0), adt)
    acs_t = acs.T
    tot = acs[ln - 1:ln, :]
    hg = SSD_HEADS // SSD_GROUPS
    s_all = s_ref[...]
    y_heads, s_heads = [], []
    for gi in range(SSD_GROUPS):
        bm = act[:, dw + gi * gn:dw + (gi + 1) * gn]
        cm = act[:, dw + SSD_GROUPS * gn + gi * gn:dw + SSD_GROUPS * gn + (gi + 1) * gn]
        cb = _bdot_nt(cm, bm)
        for hh in range(hg):
            h = gi * hg + hh
            sl = slice(h * hd, (h + 1) * hd)
            col = acs[:, h:h + 1]
            rowv = acs_t[h:h + 1, :]
            lmat = jnp.exp(jnp.where(tril, col - rowv, -jnp.inf))
            xh_h = xh[:, sl]
            xdt = xh_h * dt[:, h:h + 1]
            tot_h = tot[:, h:h + 1]
            st = s_all[h]
            y_h = _bdot(cb * lmat, xdt) + jnp.exp(col) * _bdot(cm, st)
            s_heads.append(jnp.exp(tot_h) * st + _bdot_tn(bm * jnp.exp(tot_h - col), xdt))
            y_heads.append(y_h + dsk_ref[:, sl] * xh_h)
    s_ref[...] = jnp.stack(s_heads, axis=0)
    y = jnp.concatenate(y_heads, axis=1) * _silu(z_ref[0].astype(F32))
    gw = dw // SSD_GROUPS
    for gi in range(SSD_GROUPS):
        yg = y[:, gi * gw:(gi + 1) * gw]
        yg = yg * lax.rsqrt(jnp.mean(yg * yg, axis=-1, keepdims=True) + EPS)
        o_ref[0, :, gi * gw:(gi + 1) * gw] = (yg * nw_ref[:, gi * gw:(gi + 1) * gw]).astype(o_ref.dtype)


def _ssd_mix(z, xbc, dtp, conv_w, conv_b, dt_bias, a_log, d_skip, norm_w):
    b, t, dw = z.shape
    ln = SSD_CHUNK
    xw = xbc.shape[2]
    pad = lambda vec: jnp.pad(vec, (0, LANES - vec.shape[0])).reshape(1, LANES)
    dsk = jnp.repeat(d_skip, SSD_HEAD).reshape(1, dw)
    prm = [conv_w, conv_b.reshape(1, xw), pad(dt_bias), pad(a_log), dsk, norm_w.reshape(1, dw)]
    full = lambda arr: pl.BlockSpec(arr.shape, lambda i, j: (0,) * arr.ndim)
    blk = lambda w: pl.BlockSpec((1, ln, w), lambda i, j: (i, j, 0))
    halo = 2 * SUBLANES
    return pl.pallas_call(
        _ssd_kernel, grid=(b, t // ln),
        in_specs=[blk(dw), blk(xw),
                  pl.BlockSpec((1, halo, xw), lambda i, j: (i, jnp.maximum(j * (ln // halo) - 1, 0), 0)),
                  blk(LANES)] + [full(x) for x in prm],
        out_specs=blk(dw), out_shape=jax.ShapeDtypeStruct((b, t, dw), BF16),
        scratch_shapes=[pltpu.VMEM((SSD_HEADS, SSD_STATE, SSD_HEAD), F32)],
        compiler_params=_cparams(("parallel", "arbitrary")), name="ssd",
    )(z, xbc, xbc, dtp, *prm)


def _mix_out1_kernel(x_ref, yc_ref, yd_ref, gw_ref, gb_ref, wc_ref, wd_ref, gpost_ref, gpre_ref,
                     wrh_ref, wrl_ref, x1_ref, h_ref, idx_ref, gate_ref):
    yc = _gelu_tanh(yc_ref[...].astype(F32))
    yc = yc * _sigmoid(_bdot(yc, gw_ref[...]) + gb_ref[...])
    y = _bdot(yc, wc_ref[...]) + _bdot(yd_ref[...], wd_ref[...])
    x1 = x_ref[...] + _rms(y, gpost_ref[...])
    x1_ref[...] = x1
    h = _rms(x1, gpre_ref[...])
    h_ref[...] = h
    hh = h.astype(BF16)
    hl = (h - hh.astype(F32)).astype(BF16)
    wrh = wrh_ref[...]
    logits = (jnp.dot(hh, wrh, preferred_element_type=F32) + jnp.dot(hl, wrh, preferred_element_type=F32)
              + jnp.dot(hh, wrl_ref[...], preferred_element_type=F32))
    lane = _iota2(logits.shape, 1)
    lane_f = lane.astype(F32)
    logits = jnp.where(lane < MOE_EXPERTS, logits, -jnp.inf)
    m1 = jnp.max(logits, axis=-1, keepdims=True)
    i1 = jnp.min(jnp.where(logits == m1, lane_f, float(LANES)), axis=-1, keepdims=True)
    rest = jnp.where(lane_f == i1, -jnp.inf, logits)
    m2 = jnp.max(rest, axis=-1, keepdims=True)
    i2 = jnp.min(jnp.where(rest == m2, lane_f, float(LANES)), axis=-1, keepdims=True)
    e2 = jnp.exp(m2 - m1)
    g1 = 1.0 / (1.0 + e2)
    g2 = e2 / (1.0 + e2)
    idx_ref[...] = jnp.where(lane == 0, i1, jnp.where(lane == 1, i2, 0.0)).astype(jnp.int32)
    gate_ref[...] = jnp.where(lane == 0, g1, jnp.where(lane == 1, g2, 0.0))


def _mix_out1(x2, yc, yd, glu_w, glu_b, wc, wd, gpost, gpre, wr, tm=512):
    n, d = x2.shape
    wr_p = jnp.pad(wr, ((0, 0), (0, LANES - wr.shape[1])))
    wrh = wr_p.astype(BF16)
    wrl = (wr_p - wrh.astype(F32)).astype(BF16)
    row = lambda w: pl.BlockSpec((tm, w), lambda i: (i, 0))
    full = lambda arr: pl.BlockSpec(arr.shape, lambda i: (0,) * arr.ndim)
    prm = [glu_w, glu_b, wc, wd, gpost, gpre, wrh, wrl]
    return pl.pallas_call(
        _mix_out1_kernel, grid=(n // tm,),
        in_specs=[row(d), row(yc.shape[1]), row(yd.shape[1])] + [full(p) for p in prm],
        out_specs=[row(d), row(d), row(LANES), row(LANES)],
        out_shape=[jax.ShapeDtypeStruct((n, d), F32), jax.ShapeDtypeStruct((n, d), F32),
                   jax.ShapeDtypeStruct((n, LANES), jnp.int32), jax.ShapeDtypeStruct((n, LANES), F32)],
        compiler_params=_cparams(("parallel",)), name="mix_out1",
    )(x2, yc, yd, *prm)


GATHER_UNROLL = 8


def _gather_rows(n_rows, make_copy):
    def body(j, carry):
        for q in range(GATHER_UNROLL):
            make_copy(j * GATHER_UNROLL + q).start(priority=q % 2)
        return carry

    lax.fori_loop(0, n_rows // GATHER_UNROLL, body, 0)


def _moe_kernel(nf, be_ref, tok_ref, dst_ref, nact_ref, h_hbm, wg_hbm, wu_hbm, wd_hbm, y_hbm, buf_ref,
                xb_ref, wg_ref, wu_ref, wd_ref, sa_ref, sb_ref, gsem, ssem, wsem):
    i = pl.program_id(0)
    f = pl.program_id(1)
    n_blocks = pl.num_programs(0)
    tm = buf_ref.shape[1]
    nact = nact_ref[0]
    active = i < nact
    slot = lax.rem(i, 2)
    other = 1 - slot
    xs = lambda sl: buf_ref.at[sl]
    yb = lambda sl: buf_ref.at[2 + sl]

    def gather_copy(block, sl, r):
        tok = tok_ref[block * tm + r]
        return pltpu.make_async_copy(h_hbm.at[pl.ds(tok, 1)], buf_ref.at[sl, pl.ds(r, 1)], gsem.at[sl])

    def scatter_copy(block, sl, r):
        dst = dst_ref[block * tm + r]
        return pltpu.make_async_copy(buf_ref.at[2 + sl, pl.ds(r, 1)], y_hbm.at[pl.ds(dst, 1)], ssem.at[sl])

    def wait_rows(sem_slot_ref, buf):
        pltpu.make_async_copy(h_hbm.at[pl.ds(0, tm)], buf, sem_slot_ref).wait()

    @pl.when(f == 0)
    def _():
        @pl.when(i == 0)
        def _():
            _gather_rows(tm, functools.partial(gather_copy, 0, 0))
            buf_ref[3] = jnp.zeros(buf_ref.shape[1:], F32)
            n_real = y_hbm.shape[0] - 2 * tm
            for half in range(2):
                init = pltpu.make_async_copy(yb(1), y_hbm.at[pl.ds(n_real + half * tm, tm)], ssem.at[0])
                init.start()
                init.wait()

        @pl.when(i <= nact)
        def _():
            wait_rows(gsem.at[slot], xs(slot))

        @pl.when(jnp.logical_and(i >= 1, i <= nact))
        def _():
            wait_rows(ssem.at[slot], yb(slot))

        @pl.when(i == nact)
        def _():
            _gather_rows(tm, functools.partial(scatter_copy, i - 1, other))
            wait_rows(ssem.at[other], yb(other))

        @pl.when(active)
        def _():
            xb_ref[...] = buf_ref[slot].astype(BF16)
            buf_ref[2 + slot] = jnp.zeros(buf_ref.shape[1:], F32)

    e = be_ref[i]
    fresh = jnp.logical_or(i == 0, e != be_ref[jnp.maximum(i - 1, 0)])
    nch = wg_ref.shape[1] // MXU_TILE

    pieces = []
    for c in range(nch):
        cols = slice(c * MXU_TILE, (c + 1) * MXU_TILE)
        pieces.append((wg_hbm.at[e, :, cols], sa_ref, 2 * c, wg_ref, (slice(None), cols)))
        pieces.append((wu_hbm.at[e, :, cols], sa_ref, 2 * c + 1, wu_ref, (slice(None), cols)))
        pieces.append((wd_hbm.at[e, cols, :], sb_ref, c, wd_ref, (cols, slice(None))))
    ahead = 2 * 3

    def piece_copy(p):
        src, stage, k, _, _ = pieces[p]
        sl = k % stage.shape[0]
        return pltpu.make_async_copy(src, stage.at[sl], wsem.at[(0 if stage is sa_ref else sa_ref.shape[0]) + sl])

    def compute(load_weights):
        x = xb_ref[...]
        rows_f = tm // nf
        base = pl.multiple_of(f * rows_f, SUBLANES)
        prev = jnp.where(i == 0, n_blocks - 1, i - 1)
        if load_weights:
            for p in range(ahead):
                piece_copy(p).start()
        for c in range(nch):
            if load_weights:
                for p in range(3 * c, 3 * c + 3):
                    _, stage, k, dst, where = pieces[p]
                    piece_copy(p).wait()
                    dst[where] = stage[k % stage.shape[0]].astype(BF16)
                    if p + ahead < len(pieces):
                        piece_copy(p + ahead).start()
            for r in range(rows_f * c // nch, rows_f * (c + 1) // nch):
                gather_copy(i + 1, other, base + r).start(priority=1)
                scatter_copy(prev, other, base + r).start(priority=1)
            cols = slice(c * MXU_TILE, (c + 1) * MXU_TILE)
            gate = jnp.dot(x, wg_ref[:, cols], preferred_element_type=F32)
            up = jnp.dot(x, wu_ref[:, cols], preferred_element_type=F32)
            buf_ref[2 + slot] += jnp.dot((_silu(gate) * up).astype(BF16), wd_ref[cols, :],
                                         preferred_element_type=F32)

    @pl.when(jnp.logical_and(active, fresh))
    def _():
        compute(True)

    @pl.when(jnp.logical_and(active, jnp.logical_not(fresh)))
    def _():
        compute(False)


def _moe_experts(h, block_expert, slot_tok, slot_dst, nact, n_rows, wg, wu, wd):
    n, d = h.shape
    tm = MOE_ROWS
    n_blocks = slot_tok.shape[0] // tm
    ff = wg.shape[2]
    nf = 1
    anywhere = pl.BlockSpec(memory_space=pl.ANY)
    grid_spec = pltpu.PrefetchScalarGridSpec(
        num_scalar_prefetch=4, grid=(n_blocks, nf),
        in_specs=[anywhere, anywhere, anywhere, anywhere],
        out_specs=pl.BlockSpec(memory_space=pl.ANY),
        scratch_shapes=[pltpu.VMEM((4, tm, d), F32), pltpu.VMEM((tm, d), BF16),
                        pltpu.VMEM((d, ff), BF16), pltpu.VMEM((d, ff), BF16), pltpu.VMEM((ff, d), BF16),
                        pltpu.VMEM((4, d, MXU_TILE), F32), pltpu.VMEM((2, MXU_TILE, d), F32),
                        pltpu.SemaphoreType.DMA((2,)), pltpu.SemaphoreType.DMA((2,)),
                        pltpu.SemaphoreType.DMA((6,))])
    return pl.pallas_call(
        functools.partial(_moe_kernel, nf), grid_spec=grid_spec,
        out_shape=jax.ShapeDtypeStruct((n_rows, d), F32),
        compiler_params=pltpu.CompilerParams(dimension_semantics=("arbitrary", "arbitrary"),
                                             vmem_limit_bytes=VMEM_LIMIT, disable_bounds_checks=True),
        name="moe_experts",
    )(block_expert, slot_tok, slot_dst, nact, h, wg, wu, wd)


def _combine_kernel(x_ref, y0_ref, y1_ref, gate_ref, gpost_ref, o_ref):
    gates = gate_ref[...]
    y = gates[:, 0:1] * y0_ref[...] + gates[:, 1:2] * y1_ref[...]
    o_ref[...] = x_ref[...] + _rms(y, gpost_ref[...])


def _moe_combine(x1, y, gates, gpost, tm=512):
    n, d = x1.shape
    nt = n // tm
    return pl.pallas_call(
        _combine_kernel, grid=(nt,),
        in_specs=[pl.BlockSpec((tm, d), lambda i: (i, 0)), pl.BlockSpec((tm, d), lambda i: (i, 0)),
                  pl.BlockSpec((tm, d), lambda i: (nt + i, 0)),
                  pl.BlockSpec((tm, LANES), lambda i: (i, 0)), pl.BlockSpec((1, d), lambda i: (0, 0))],
        out_specs=pl.BlockSpec((tm, d), lambda i: (i, 0)),
        out_shape=jax.ShapeDtypeStruct((n, d), F32),
        compiler_params=_cparams(("parallel",)), name="moe_combine",
    )(x1, y, y, gates, gpost)


def _moe_plan(idx, n):
    tm = MOE_ROWS
    flat_e = idx[:, :2].reshape(-1)
    onehot = (flat_e[:, None] == jnp.arange(MOE_EXPERTS, dtype=jnp.int32)[None, :]).astype(jnp.int32)
    csum = jnp.cumsum(onehot, axis=0)
    counts = csum[-1]
    rank = jnp.sum((csum - onehot) * onehot, axis=1)
    padded = (counts + tm - 1) // tm * tm
    pend = jnp.cumsum(padded)
    pstart = pend - padded
    dest = (jnp.sum(onehot * pstart[None, :], axis=1) + rank).astype(jnp.int32)
    n_blocks = (2 * n) // tm + MOE_EXPERTS + 1
    n_slots = n_blocks * tm
    slot_pair = jnp.full((n_slots,), -1, jnp.int32).at[dest].set(
        jnp.arange(2 * n, dtype=jnp.int32), unique_indices=True)
    real = slot_pair >= 0
    slot_tok = jnp.where(real, slot_pair // 2, 0)
    s_id = jnp.arange(n_slots, dtype=jnp.int32)
    slot_dst = jnp.where(real, slot_pair % 2 * n + slot_pair // 2, 2 * n + (s_id // tm) % 2 * tm + s_id % tm)
    block_start = jnp.arange(n_blocks, dtype=jnp.int32) * tm
    block_expert = jnp.minimum(jnp.sum((block_start[:, None] >= pend[None, :]).astype(jnp.int32), axis=1),
                               MOE_EXPERTS - 1)
    nact = (pend[-1] // tm).astype(jnp.int32).reshape(1)
    return block_expert, slot_tok, slot_dst, nact, 2 * n + 2 * tm


def kernel(x, l0_norm_pre_mix, l0_w_in, l0_rwkv_mu, l0_rwkv_w0, l0_rwkv_w2, l0_rwkv_a0, l0_rwkv_a2, l0_rwkv_g2, l0_rwkv_k_k, l0_rwkv_k_a, l0_rwkv_r_k, l0_rwkv_ln_w, l0_rwkv_ln_b, l0_gmlp_ln_w, l0_gmlp_ln_b, l0_gmlp_ws, l0_gmlp_bs, l0_w_out, l0_norm_post_mix, l0_norm_pre_ffn, l0_ffn_w_gate, l0_ffn_w_up, l0_ffn_w_down, l0_norm_post_ffn, l1_norm_pre_mix, l1_w_in, l1_s5_a_re, l1_s5_a_im, l1_s5_log_dt, l1_s5_b_re, l1_s5_b_im, l1_s5_c_re, l1_s5_c_im, l1_s5_d, l1_s5_glu_w, l1_s5_glu_b, l1_m2_conv_w, l1_m2_conv_b, l1_m2_dt_bias, l1_m2_a_log, l1_m2_d, l1_m2_norm_w, l1_w_out, l1_norm_post_mix, l1_norm_pre_ffn, l1_moe_router, l1_moe_w_gate, l1_moe_w_up, l1_moe_w_down, l1_norm_post_ffn):
    b, t, d = x.shape
    n = b * t
    x2 = x.reshape(n, d)
    row = lambda vec: vec.reshape(1, -1)

    aw = l0_rwkv_w0.shape[0]
    heads = aw // RWKV_HEAD
    lw_, la_, lg_ = l0_rwkv_w2.shape[0], l0_rwkv_a2.shape[0], l0_rwkv_g2.shape[0]
    a_in = 3 * aw + lw_ + la_ + lg_
    padc = lambda m, wdt: jnp.pad(m, ((0, 0), (0, LANES - wdt)))
    o = 3 * aw
    w_a = jnp.concatenate([l0_w_in[:, :o], padc(l0_w_in[:, o:o + lw_], lw_),
                           padc(l0_w_in[:, o + lw_:o + lw_ + la_], la_),
                           padc(l0_w_in[:, o + lw_ + la_:a_in], lg_)], axis=1).astype(BF16)
    w_b = l0_w_in[:, a_in:].astype(BF16)
    p_a, p_b = _norm_proj(x2, l0_norm_pre_mix, [w_a, w_b], [F32, BF16])
    padv = lambda vec, wdt: jnp.pad(vec, (0, LANES - wdt))
    mu = l0_rwkv_mu
    mu_p = jnp.concatenate([mu[:o], padv(mu[o:o + lw_], lw_), padv(mu[o + lw_:o + lw_ + la_], la_),
                            padv(mu[o + lw_ + la_:], lg_)])
    padr = lambda m: jnp.pad(m, ((0, LANES - m.shape[0]), (0, 0))).astype(BF16)
    hid = jnp.arange(LANES, dtype=jnp.int32) // RWKV_HEAD
    gsum = (hid[:, None] == hid[None, :]).astype(BF16)
    rwkv_prm = [row(mu_p), row(l0_rwkv_w0), padr(l0_rwkv_w2), row(l0_rwkv_a0), padr(l0_rwkv_a2),
                padr(l0_rwkv_g2), row(l0_rwkv_k_k), row(l0_rwkv_k_a), row(l0_rwkv_r_k),
                row(l0_rwkv_ln_w), row(l0_rwkv_ln_b), gsum]
    ya = _rwkv_mix(p_a.reshape(b, t, -1), rwkv_prm, heads)
    yb = _gmlp_mix(p_b.reshape(b, t, -1), l0_gmlp_ln_w, l0_gmlp_ln_b, l0_gmlp_ws, l0_gmlp_bs)
    wo = l0_w_out.astype(BF16)
    x2 = _mix_out0(x2, ya.reshape(n, -1), yb.reshape(n, -1), wo[:aw], wo[aw:], row(l0_norm_post_mix))
    x2 = _ffn(x2, row(l0_norm_pre_ffn), l0_ffn_w_gate.astype(BF16), l0_ffn_w_up.astype(BF16),
              l0_ffn_w_down.astype(BF16), row(l0_norm_post_ffn))

    cw = l1_s5_d.shape[0]
    dw = l1_m2_norm_w.shape[0]
    xw = l1_m2_conv_w.shape[1]
    nh = l1_m2_dt_bias.shape[0]
    w1 = l1_w_in
    w_parts = [w1[:, :cw], w1[:, cw:cw + dw], w1[:, cw + dw:cw + dw + xw],
               padc(w1[:, cw + dw + xw:], nh)]
    u_c, z_d, xbc, dtp = _norm_proj(x2, l1_norm_pre_mix, [w.astype(BF16) for w in w_parts],
                                    [BF16, BF16, BF16, F32])
    s5_w = _s5_weights(l1_s5_a_re, l1_s5_a_im, l1_s5_log_dt, l1_s5_b_re, l1_s5_b_im, l1_s5_c_re,
                       l1_s5_c_im, l1_s5_d, S5_CHUNK)
    yc = _s5_core(u_c.reshape(b, t, cw), s5_w)
    yd = _ssd_mix(z_d.reshape(b, t, dw), xbc.reshape(b, t, xw), dtp.reshape(b, t, LANES),
                  l1_m2_conv_w, l1_m2_conv_b, l1_m2_dt_bias, l1_m2_a_log, l1_m2_d, l1_m2_norm_w)
    wo1 = l1_w_out.astype(BF16)
    x1, h, idx, gates = _mix_out1(x2, yc.reshape(n, cw), yd.reshape(n, dw), l1_s5_glu_w.astype(BF16),
                                  row(l1_s5_glu_b), wo1[:cw], wo1[cw:], row(l1_norm_post_mix),
                                  row(l1_norm_pre_ffn), l1_moe_router)
    block_expert, slot_tok, slot_dst, nact, n_rows = _moe_plan(idx, n)
    ys = _moe_experts(h, block_expert, slot_tok, slot_dst, nact, n_rows, l1_moe_w_gate, l1_moe_w_up,
                      l1_moe_w_down)
    out = _moe_combine(x1, ys, gates, row(l1_norm_post_ffn))
    return out.reshape(b, t, d)
```

```python
import functools

import jax
import jax.numpy as jnp
from jax import lax
from jax.experimental import pallas as pl
from jax.experimental.pallas import tpu as pltpu

F32 = jnp.float32
BF16 = jnp.bfloat16

EPS = 1e-6
RWKV_GN_EPS = 64e-5
RWKV_HEAD = 64
RWKV_CHUNK = 64
GMLP_CHUNK = 128
GMLP_GROUPS = 4
S5_GROUP_CH = 16
S5_STATE = 64
S5_CHUNK = 64
SSD_HEAD = 64
SSD_HEADS = 8
SSD_GROUPS = 2
SSD_STATE = 128
SSD_CONV = 4
SSD_CHUNK = 128
MOE_EXPERTS = 8
MOE_ROWS = 512
MXU_TILE = 256
LANES = 128
SUBLANES = 8
VMEM_LIMIT = 56 * 1024 * 1024


def _cparams(sem):
    return pltpu.CompilerParams(dimension_semantics=sem, vmem_limit_bytes=VMEM_LIMIT)


def _bdot(a, b):
    return jnp.dot(a.astype(BF16), b.astype(BF16), preferred_element_type=F32)


def _bdot_nt(a, b):
    return lax.dot_general(a.astype(BF16), b.astype(BF16), (((1,), (1,)), ((), ())),
                           preferred_element_type=F32)


def _bdot_tn(a, b):
    return lax.dot_general(a.astype(BF16), b.astype(BF16), (((0,), (0,)), ((), ())),
                           preferred_element_type=F32)


def _split3(x):
    h = x.astype(BF16)
    r1 = x - h.astype(F32)
    m = r1.astype(BF16)
    l = (r1 - m.astype(F32)).astype(BF16)
    return h, m, l


def _dot_x_exact(x, e):
    h, m, l = _split3(x)
    e = e.astype(BF16)
    return (jnp.dot(h, e, preferred_element_type=F32) + jnp.dot(m, e, preferred_element_type=F32)
            + jnp.dot(l, e, preferred_element_type=F32))


def _dot_exact_x(e, x):
    h, m, l = _split3(x)
    e = e.astype(BF16)
    return (jnp.dot(e, h, preferred_element_type=F32) + jnp.dot(e, m, preferred_element_type=F32)
            + jnp.dot(e, l, preferred_element_type=F32))


def _rms(x, g):
    return x * lax.rsqrt(jnp.mean(x * x, axis=-1, keepdims=True) + EPS) * g


def _sigmoid(x):
    return 1.0 / (1.0 + jnp.exp(-x))


def _silu(x):
    return x * _sigmoid(x)


def _softplus(x):
    return jnp.maximum(x, 0.0) + jnp.log(1.0 + jnp.exp(-jnp.abs(x)))


def _gelu_tanh(x):
    return 0.5 * x * (1.0 + jnp.tanh(0.7978845608028654 * (x + 0.044715 * x * x * x)))


def _iota2(shape, dim):
    return lax.broadcasted_iota(jnp.int32, shape, dim)


def _norm_proj_kernel(n_out, n_t, x_ref, g_ref, *refs):
    w_refs = refs[:n_out]
    wt_refs = refs[n_out:n_out + n_t]
    o_refs = refs[n_out + n_t:2 * n_out + n_t]
    ot_refs = refs[2 * n_out + n_t:]
    xn = _rms(x_ref[...], g_ref[...]).astype(BF16)
    for w_ref, o_ref in zip(w_refs, o_refs):
        o_ref[...] = jnp.dot(xn, w_ref[...], preferred_element_type=F32).astype(o_ref.dtype)
    for wt_ref, ot_ref in zip(wt_refs, ot_refs):
        ot_ref[...] = lax.dot_general(wt_ref[...], xn, (((1,), (1,)), ((), ())),
                                      preferred_element_type=F32).astype(ot_ref.dtype)


def _norm_proj(x2, g, ws, dtypes, wts=(), tdtypes=(), tm=512):
    n, d = x2.shape
    in_specs = [pl.BlockSpec((tm, d), lambda i: (i, 0)), pl.BlockSpec((1, d), lambda i: (0, 0))]
    in_specs += [pl.BlockSpec(w.shape, lambda i: (0, 0)) for w in list(ws) + list(wts)]
    out_specs = [pl.BlockSpec((tm, w.shape[1]), lambda i: (i, 0)) for w in ws]
    out_specs += [pl.BlockSpec((wt.shape[0], tm), lambda i: (0, i)) for wt in wts]
    out_shape = [jax.ShapeDtypeStruct((n, w.shape[1]), dt) for w, dt in zip(ws, dtypes)]
    out_shape += [jax.ShapeDtypeStruct((wt.shape[0], n), dt) for wt, dt in zip(wts, tdtypes)]
    return pl.pallas_call(
        functools.partial(_norm_proj_kernel, len(ws), len(wts)),
        grid=(n // tm,), in_specs=in_specs, out_specs=out_specs, out_shape=out_shape,
        compiler_params=_cparams(("parallel",)), name="norm_proj",
    )(x2, g.reshape(1, d), *ws, *wts)


PRE_NAMES = ("v", "g", "bonus", "rt", "kt", "bt", "at", "bh", "kh")


def _rwkv_kernel(heads, nb, p_ref, pp_ref, mu_ref, w0_ref, w2_ref, a0_ref, a2_ref, g2_ref, kk_ref,
                 ka_ref, rk_ref, lnw_ref, lnb_ref, gs_ref, o_ref, z_ref, pre_ref, wl_ref):
    c = pl.program_id(1)
    ln = RWKV_CHUNK
    hd = RWKV_HEAD
    aw = heads * hd

    @pl.when(c == 0)
    def _():
        z_ref[...] = jnp.zeros_like(z_ref)
        pre_ref[...] = jnp.zeros_like(pre_ref)
        wl_ref[...] = jnp.zeros_like(wl_ref)

    tril_f = jnp.where(_iota2((ln, ln), 0) >= _iota2((ln, ln), 1), 1.0, 0.0)
    rows = _iota2((ln, 1), 0)
    gs_tile = gs_ref[...]

    def gs(x):
        nt = x.shape[1] // LANES
        stacked = jnp.concatenate([x[:, j * LANES:(j + 1) * LANES] for j in range(nt)], axis=0)
        red = _dot_x_exact(stacked, gs_tile)
        return jnp.concatenate([red[j * ln:(j + 1) * ln] for j in range(nt)], axis=1)

    pre_idx = {nm: idx for idx, nm in enumerate(PRE_NAMES)}
    pending = []

    def prep_steps():
        for bi in range(nb):
            p = p_ref[bi]
            prev = jnp.where(c == 0, 0.0, pp_ref[bi][SUBLANES - 1:SUBLANES, :])
            ps = jnp.where(rows == 0, prev, pltpu.roll(p, 1, axis=0))
            pm = p + (ps - p) * mu_ref[...]
            r = pm[:, 0:aw]
            k = pm[:, aw:2 * aw]
            v = pm[:, 2 * aw:3 * aw]
            xw = pm[:, 3 * aw:3 * aw + LANES]
            xa = pm[:, 3 * aw + LANES:3 * aw + 2 * LANES]
            xg = pm[:, 3 * aw + 2 * LANES:3 * aw + 3 * LANES]
            yield
            w = w0_ref[...] + _bdot(jnp.tanh(xw), w2_ref[...])
            a = _sigmoid(a0_ref[...] + _bdot(xa, a2_ref[...]))
            g = _bdot(_sigmoid(xg), g2_ref[...])
            yield
            w = -_softplus(-w) - 0.5
            lw = -jnp.exp(w)
            kk = k * kk_ref[...]
            kk_ss = gs(kk * kk)
            yield
            cs = _dot_exact_x(tril_f, lw)
            kk = kk / jnp.maximum(jnp.sqrt(kk_ss), 1e-12)
            kmod = k * (1.0 + (a - 1.0) * ka_ref[...])
            yield
            bonus = gs(r * kmod * rk_ref[...])
            bvec = kk * a
            cs_last = cs[ln - 1:ln, :]
            encs = jnp.exp(-cs)
            yield
            dec_end = jnp.exp(cs_last - cs)
            nxt = dict(v=v, g=g, bonus=bonus, rt=r * jnp.exp(cs), kt=kmod * encs, bt=bvec * encs,
                       at=-kk * jnp.exp(cs - lw), bh=bvec * dec_end, kh=kmod * dec_end)
            pending.append((bi, nxt, jnp.broadcast_to(jnp.exp(cs_last), (SUBLANES, aw))))
            yield

    prep = prep_steps()
    tick = lambda: next(prep, None)

    lane = _iota2((ln, LANES), 1)
    lane_in = jnp.where(lane >= hd, lane - hd, lane)
    trow = _iota2((ln, LANES), 0)
    left = lane < hd
    tril_p = lane_in <= trow
    stril_p = lane_in < trow
    eye_p = lane_in == trow
    eye_pf = jnp.where(eye_p, 1.0, 0.0)

    def bd(x):
        xb = x.astype(BF16)
        zero = jnp.zeros_like(xb)
        return jnp.concatenate([jnp.where(left, xb, zero), jnp.where(left, zero, xb)], axis=0)

    def dot(a, b):
        return jnp.dot(a.astype(BF16), b, preferred_element_type=F32)

    npair = heads // 2
    pairs = [(bi, j) for bi in range(nb) for j in range(npair)]

    class _Tiles:
        def __init__(self, name):
            self.idx = pre_idx[name]

        def __getitem__(self, i):
            bi, j = pairs[i]
            return pre_ref[self.idx, bi, :, j * LANES:(j + 1) * LANES]

    at, rt, bt, kt, vv, bh, kh = (_Tiles(n) for n in ("at", "rt", "bt", "kt", "v", "bh", "kh"))
    wl = [wl_ref[bi, 0:1, j * LANES:(j + 1) * LANES] for bi, j in pairs]
    z_all = z_ref[...]
    zs = [z_all[bi, j] for bi, j in pairs]
    npr = range(len(pairs))
    lhs = [jnp.concatenate([at[i], rt[i]], axis=0).astype(BF16) for i in npr]
    abk = [lax.dot_general(lhs[i], jnp.concatenate([bd(bt[i]), bd(kt[i])], axis=0),
                           (((1,), (1,)), ((), ())), preferred_element_type=F32) for i in npr]
    ab = [abk[i][:, :LANES] for i in npr]
    ak = [abk[i][:, LANES:] for i in npr]
    tick()
    nmat = [jnp.where(stril_p, ab[i][:ln], 0.0) for i in npr]
    tinv = [eye_pf + nmat[i] for i in npr]
    npow = [dot(nmat[i], bd(nmat[i])) for i in npr]
    tick()
    for step in range(5):
        bdn = [bd(npow[i]) for i in npr]
        if step < 4:
            both = [dot(jnp.concatenate([tinv[i], npow[i]], axis=0), bdn[i]) for i in npr]
            tinv = [tinv[i] + both[i][:ln] for i in npr]
            npow = [both[i][ln:] for i in npr]
        else:
            tinv = [tinv[i] + dot(tinv[i], bdn[i]) for i in npr]
        tick()
    bdv = [bd(vv[i]) for i in npr]
    bdz = [bd(zs[i]) for i in npr]
    xmat = [dot(jnp.concatenate([jnp.where(stril_p, ak[i][:ln], 0.0), at[i]], axis=1),
                jnp.concatenate([bdv[i], bdz[i]], axis=0)) for i in npr]
    tick()
    u = [dot(tinv[i], bd(xmat[i])) for i in npr]
    tick()
    ys_p = [dot(jnp.concatenate([rt[i], jnp.where(tril_p, ab[i][ln:], 0.0),
                                 jnp.where(tril_p, ak[i][ln:], 0.0)], axis=1),
                jnp.concatenate([bdz[i], bd(u[i]), bdv[i]], axis=0)) for i in npr]
    tick()
    cross = [_bdot_tn(jnp.concatenate([bh[i], kh[i]], axis=0), jnp.concatenate([u[i], vv[i]], axis=0))
             for i in npr]
    tick()
    z_new = []
    for i in npr:
        dg = jnp.where(eye_p, wl[i], 0.0)
        wl_i = jnp.sum(jnp.where(left, dg, 0.0), axis=1, keepdims=True)
        wl_j = jnp.sum(jnp.where(left, 0.0, dg), axis=1, keepdims=True)
        z_new.append(jnp.where(left, wl_i, wl_j) * zs[i] + jnp.where(left, cross[i][:ln], cross[i][ln:]))
    z_ref[...] = jnp.stack(z_new, axis=0).reshape(z_ref.shape)

    tick()
    inv = 1.0 / hd
    for bi in range(nb):
        y = jnp.concatenate(ys_p[bi * npair:(bi + 1) * npair], axis=1)
        mean = gs(y) * inv
        d = y - mean
        var = gs(d * d) * inv
        yn = d * lax.rsqrt(var + RWKV_GN_EPS) * lnw_ref[...] + lnb_ref[...]
        o_ref[bi] = ((yn + pre_ref[pre_idx["bonus"], bi] * pre_ref[pre_idx["v"], bi])
                     * pre_ref[pre_idx["g"], bi]).astype(o_ref.dtype)
        tick()
    for _ in prep:
        pass
    for bi, nxt, wl_next in pending:
        for nm, idx in pre_idx.items():
            pre_ref[idx, bi] = nxt[nm]
        wl_ref[bi] = wl_next


def _rwkv_mix(p_a, prm, heads, nb=4):
    b, t, cin = p_a.shape
    aw = heads * RWKV_HEAD
    ln = RWKV_CHUNK
    nc = t // ln
    sub = ln // SUBLANES
    full = lambda arr: pl.BlockSpec(arr.shape, lambda i, j: (0,) * arr.ndim)
    in_specs = [pl.BlockSpec((nb, ln, cin), lambda i, j: (i, jnp.minimum(j, nc - 1), 0)),
                pl.BlockSpec((nb, SUBLANES, cin),
                             lambda i, j: (i, jnp.maximum(jnp.minimum(j, nc - 1) * sub - 1, 0), 0))]
    in_specs += [full(x) for x in prm]
    return pl.pallas_call(
        functools.partial(_rwkv_kernel, heads, nb),
        grid=(b // nb, nc + 1), in_specs=in_specs,
        out_specs=pl.BlockSpec((nb, ln, aw), lambda i, j: (i, jnp.maximum(j - 1, 0), 0)),
        out_shape=jax.ShapeDtypeStruct((b, t, aw), BF16),
        scratch_shapes=[pltpu.VMEM((nb, heads // 2, RWKV_HEAD, 2 * RWKV_HEAD), F32),
                        pltpu.VMEM((len(PRE_NAMES), nb, ln, aw), F32),
                        pltpu.VMEM((nb, SUBLANES, aw), F32)],
        compiler_params=_cparams(("parallel", "arbitrary")), name="rwkv7",
    )(p_a, p_a, *prm)


def _gmlp_kernel(p_ref, lnw_ref, lnb_ref, ws_ref, bs_ref, o_ref):
    ln = GMLP_CHUNK
    bw = p_ref.shape[2] // 2
    gd = bw // GMLP_GROUPS
    tril = _iota2((ln, ln), 0) >= _iota2((ln, ln), 1)
    ws_c = [jnp.where(tril, ws_ref[gi], 0.0).astype(BF16) for gi in range(GMLP_GROUPS)]
    for ci in range(p_ref.shape[1] // ln):
        x = _gelu_tanh(p_ref[0, ci * ln:(ci + 1) * ln, :].astype(F32))
        for gi in range(GMLP_GROUPS):
            u = x[:, gi * gd:(gi + 1) * gd]
            v = x[:, bw + gi * gd:bw + (gi + 1) * gd]
            mean = jnp.mean(v, axis=-1, keepdims=True)
            d = v - mean
            var = jnp.mean(d * d, axis=-1, keepdims=True)
            vn = d * lax.rsqrt(var + EPS) * lnw_ref[gi:gi + 1, :] + lnb_ref[gi:gi + 1, :]
            s = jnp.dot(ws_c[gi], vn.astype(BF16), preferred_element_type=F32) + bs_ref[gi]
            o_ref[0, ci * ln:(ci + 1) * ln, gi * gd:(gi + 1) * gd] = (u * s).astype(o_ref.dtype)


def _gmlp_mix(p_b, ln_w, ln_b, ws, bs, rows=512):
    b, t, cin = p_b.shape
    bw = cin // 2
    gd = bw // GMLP_GROUPS
    bs_b = jnp.broadcast_to(bs[:, :, None], (GMLP_GROUPS, GMLP_CHUNK, gd))
    full = lambda arr: pl.BlockSpec(arr.shape, lambda i, j: (0,) * arr.ndim)
    return pl.pallas_call(
        _gmlp_kernel, grid=(b, t // rows),
        in_specs=[pl.BlockSpec((1, rows, cin), lambda i, j: (i, j, 0)),
                  full(ln_w), full(ln_b), full(ws), full(bs_b)],
        out_specs=pl.BlockSpec((1, rows, bw), lambda i, j: (i, j, 0)),
        out_shape=jax.ShapeDtypeStruct((b, t, bw), BF16),
        compiler_params=_cparams(("parallel", "parallel")), name="gmlp",
    )(p_b, ln_w, ln_b, ws, bs_b)


def _mix_out0_kernel(x_ref, ya_ref, yb_ref, wa_ref, wb_ref, g_ref, o_ref):
    y = _bdot(ya_ref[...], wa_ref[...]) + _bdot(yb_ref[...], wb_ref[...])
    o_ref[...] = x_ref[...] + _rms(y, g_ref[...])


def _mix_out0(x2, ya, yb, wa, wb, g, tm=512):
    n, d = x2.shape
    row = lambda arr: pl.BlockSpec((tm, arr.shape[1]), lambda i: (i, 0))
    full = lambda arr: pl.BlockSpec(arr.shape, lambda i: (0,) * arr.ndim)
    return pl.pallas_call(
        _mix_out0_kernel, grid=(n // tm,),
        in_specs=[row(x2), row(ya), row(yb), full(wa), full(wb), full(g)],
        out_specs=row(x2), out_shape=jax.ShapeDtypeStruct((n, d), F32),
        compiler_params=_cparams(("parallel",)), name="mix_out0",
    )(x2, ya, yb, wa, wb, g)


def _ffn_kernel(fc, x_ref, gpre_ref, wg_ref, wu_ref, wd_ref, gpost_ref, o_ref):
    h = _rms(x_ref[...], gpre_ref[...]).astype(BF16)
    acc = None
    for c in range(wg_ref.shape[1] // fc):
        cols = slice(c * fc, (c + 1) * fc)
        gate = jnp.dot(h, wg_ref[:, cols], preferred_element_type=F32)
        up = jnp.dot(h, wu_ref[:, cols], preferred_element_type=F32)
        part = jnp.dot((_silu(gate) * up).astype(BF16), wd_ref[cols, :], preferred_element_type=F32)
        acc = part if acc is None else acc + part
    o_ref[...] = x_ref[...] + _rms(acc, gpost_ref[...])


def _ffn(x2, gpre, wg, wu, wd, gpost, tm=512, fc=MXU_TILE):
    n, d = x2.shape
    full = lambda arr: pl.BlockSpec(arr.shape, lambda i: (0,) * arr.ndim)
    return pl.pallas_call(
        functools.partial(_ffn_kernel, fc), grid=(n // tm,),
        in_specs=[pl.BlockSpec((tm, d), lambda i: (i, 0)), full(gpre), full(wg), full(wu), full(wd),
                  full(gpost)],
        out_specs=pl.BlockSpec((tm, d), lambda i: (i, 0)),
        out_shape=jax.ShapeDtypeStruct((n, d), F32),
        compiler_params=_cparams(("parallel",)), name="ffn",
    )(x2, gpre, wg, wu, wd, gpost)


def _s5_kernel(nc, nb, u_ref, tap_ref, wsr_ref, wsi_ref, wcr_ref, wci_ref, alr_ref, ali_ref, d_ref, o_ref,
               toep_ref):
    u = u_ref[0]
    taps = tap_ref[0]
    ch, width = taps.shape
    ln = width // ch
    keep = (_iota2((ln, width), 1) & (ln - 1)) >= _iota2((ln, width), 0)
    for cin in range(ch):
        src = jnp.broadcast_to(taps[cin:cin + 1, :], (ln, width))
        blk = jnp.where(keep, pltpu.roll(src, 0, 1, stride=1, stride_axis=0), 0.0)
        toep_ref[cin * ln:(cin + 1) * ln, :] = blk.astype(BF16)
    y = jnp.dot(u, toep_ref[...], preferred_element_type=F32)
    xer = jnp.dot(u, wsr_ref[0], preferred_element_type=F32)
    xei = jnp.dot(u, wsi_ref[0], preferred_element_type=F32)
    alr = alr_ref[0]
    ali = ali_ref[0]
    cr = jnp.zeros((nb, xer.shape[1]), F32)
    ci = jnp.zeros((nb, xer.shape[1]), F32)
    prs, pis = [], []
    for c in range(nc):
        prs.append(cr)
        pis.append(ci)
        er = xer[c * nb:(c + 1) * nb]
        ei = xei[c * nb:(c + 1) * nb]
        cr, ci = alr * cr - ali * ci + er, alr * ci + ali * cr + ei
    pr = jnp.concatenate(prs, axis=0)
    pi = jnp.concatenate(pis, axis=0)
    y = y + _bdot(pr, wcr_ref[0]) + _bdot(pi, wci_ref[0])
    o_ref[0] = (y + d_ref[0] * u.astype(F32)).astype(o_ref.dtype)


def _s5_weights(a_re, a_im, log_dt, b_re, b_im, c_re, c_im, d_skip, ln):
    g, st = a_re.shape
    ch = b_re.shape[2]
    dt = jnp.exp(log_dt)[:, None]
    lr, li = a_re, a_im
    tau = jnp.arange(ln + 1, dtype=F32)[:, None, None]
    mag = jnp.exp(lr[None] * dt[None] * tau)
    pw_r = mag * jnp.cos(li[None] * dt[None] * tau)
    pw_i = mag * jnp.sin(li[None] * dt[None] * tau)
    ab_r, ab_i = pw_r[1], pw_i[1]
    nr, ni = ab_r - 1.0, ab_i
    den = lr * lr + li * li
    fr, fi = (nr * lr + ni * li) / den, (ni * lr - nr * li) / den
    bb_r = fr[..., None] * b_re - fi[..., None] * b_im
    bb_i = fr[..., None] * b_im + fi[..., None] * b_re
    cp_r = c_re[None] * pw_r[:ln, :, None, :] - c_im[None] * pw_i[:ln, :, None, :]
    cp_i = c_re[None] * pw_i[:ln, :, None, :] + c_im[None] * pw_r[:ln, :, None, :]
    hp = lax.Precision.HIGHEST
    taps = (jnp.einsum('tgcp,gpd->gdct', cp_r, bb_r, precision=hp)
            - jnp.einsum('tgcp,gpd->gdct', cp_i, bb_i, precision=hp))
    taps = taps.reshape(g, ch, ch * ln)
    rev_r, rev_i = pw_r[:ln][::-1], pw_i[:ln][::-1]
    ws_r = rev_r[..., None] * bb_r[None] - rev_i[..., None] * bb_i[None]
    ws_i = rev_r[..., None] * bb_i[None] + rev_i[..., None] * bb_r[None]
    ws_r = ws_r.transpose(1, 3, 0, 2).reshape(g, ch * ln, st)
    ws_i = ws_i.transpose(1, 3, 0, 2).reshape(g, ch * ln, st)
    q_r, q_i = pw_r[1:ln + 1], pw_i[1:ln + 1]
    wc_r = c_re[None] * q_r[:, :, None, :] - c_im[None] * q_i[:, :, None, :]
    wc_i = -(c_re[None] * q_i[:, :, None, :] + c_im[None] * q_r[:, :, None, :])
    wc_r = wc_r.transpose(1, 3, 2, 0).reshape(g, st, ch * ln)
    wc_i = wc_i.transpose(1, 3, 2, 0).reshape(g, st, ch * ln)
    al_r = pw_r[ln].reshape(g, 1, st)
    al_i = pw_i[ln].reshape(g, 1, st)
    d_t = jnp.repeat(d_skip.reshape(g, ch), ln, axis=1).reshape(g, 1, ch * ln)
    return (taps, ws_r.astype(BF16), ws_i.astype(BF16), wc_r.astype(BF16),
            wc_i.astype(BF16), al_r, al_i, d_t)


def _s5_core(ut, weights, b, t):
    cw = ut.shape[0]
    ln, ch = S5_CHUNK, S5_GROUP_CH
    g = cw // ch
    nc = t // ln
    ug = ut.reshape(g, ch, b, nc, ln).transpose(0, 3, 2, 1, 4).reshape(g, nc * b, ch * ln)
    per_g = lambda arr: pl.BlockSpec((1,) + arr.shape[1:], lambda i: (i, 0, 0))
    yg = pl.pallas_call(
        functools.partial(_s5_kernel, nc, b), grid=(g,),
        in_specs=[per_g(ug)] + [per_g(w) for w in weights],
        out_specs=per_g(ug), out_shape=jax.ShapeDtypeStruct(ug.shape, BF16),
        scratch_shapes=[pltpu.VMEM((ln * ch, ln * ch), BF16)],
        compiler_params=_cparams(("parallel",)), name="s5",
    )(ug, *weights)
    return yg.reshape(g, nc, b, ch, ln).transpose(0, 3, 2, 1, 4).reshape(cw, b * t)


def _ssd_kernel(z_ref, xbc_ref, xp_ref, dt_ref, cw_ref, cb_ref, dtb_ref, alog_ref, dsk_ref, nw_ref,
                o_ref, s_ref):
    c = pl.program_id(1)
    ln = SSD_CHUNK
    hd = SSD_HEAD
    dw = SSD_HEADS * hd
    gn = SSD_STATE

    @pl.when(c == 0)
    def _():
        s_ref[...] = jnp.zeros_like(s_ref)

    xbc = xbc_ref[0]
    halo = xp_ref.shape[1]
    prev = jnp.where(c == 0, jnp.zeros_like(xp_ref[0]), xp_ref[0])
    full = jnp.concatenate([prev, xbc], axis=0).astype(BF16)
    conv = cb_ref[...] + cw_ref[SSD_CONV - 1:SSD_CONV, :] * xbc.astype(F32)
    for j in range(SSD_CONV - 1):
        lag = SSD_CONV - 1 - j
        pick = _iota2((ln, halo + ln), 1) == _iota2((ln, halo + ln), 0) + (halo - lag)
        shifted = jnp.dot(jnp.where(pick, 1.0, 0.0).astype(BF16), full, preferred_element_type=F32)
        conv = conv + cw_ref[j:j + 1, :] * shifted
    act = _silu(conv)
    xh = act[:, :dw]
    dt = _softplus(dt_ref[0] + dtb_ref[...])
    adt = -jnp.exp(alog_ref[...]) * dt
    tril = _iota2((ln, ln), 0) >= _iota2((ln, ln), 1)
    acs = _dot_exact_x(jnp.where(tril, 1.0, 0.0), adt)
    acs_t = acs.T
    tot = acs[ln - 1:ln, :]
    hg = SSD_HEADS // SSD_GROUPS
    s_all = s_ref[...]
    y_heads, s_heads = [], []
    for gi in range(SSD_GROUPS):
        bm = act[:, dw + gi * gn:dw + (gi + 1) * gn]
        cm = act[:, dw + SSD_GROUPS * gn + gi * gn:dw + SSD_GROUPS * gn + (gi + 1) * gn]
        cb = _bdot_nt(cm, bm)
        for hh in range(hg):
            h = gi * hg + hh
            sl = slice(h * hd, (h + 1) * hd)
            col = acs[:, h:h + 1]
            rowv = acs_t[h:h + 1, :]
            lmat = jnp.exp(jnp.where(tril, col - rowv, -jnp.inf))
            xh_h = xh[:, sl]
            xdt = xh_h * dt[:, h:h + 1]
            tot_h = tot[:, h:h + 1]
            st = s_all[h]
            y_h = _bdot(cb * lmat, xdt) + jnp.exp(col) * _bdot(cm, st)
            s_heads.append(jnp.exp(tot_h) * st + _bdot_tn(bm * jnp.exp(tot_h - col), xdt))
            y_heads.append(y_h + dsk_ref[:, sl] * xh_h)
    s_ref[...] = jnp.stack(s_heads, axis=0)
    y = jnp.concatenate(y_heads, axis=1) * _silu(z_ref[0].astype(F32))
    gw = dw // SSD_GROUPS
    for gi in range(SSD_GROUPS):
        yg = y[:, gi * gw:(gi + 1) * gw]
        yg = yg * lax.rsqrt(jnp.mean(yg * yg, axis=-1, keepdims=True) + EPS)
        o_ref[0, :, gi * gw:(gi + 1) * gw] = (yg * nw_ref[:, gi * gw:(gi + 1) * gw]).astype(o_ref.dtype)


def _ssd_mix(z, xbc, dtp, conv_w, conv_b, dt_bias, a_log, d_skip, norm_w):
    b, t, dw = z.shape
    ln = SSD_CHUNK
    xw = xbc.shape[2]
    pad = lambda vec: jnp.pad(vec, (0, LANES - vec.shape[0])).reshape(1, LANES)
    dsk = jnp.repeat(d_skip, SSD_HEAD).reshape(1, dw)
    prm = [conv_w, conv_b.reshape(1, xw), pad(dt_bias), pad(a_log), dsk, norm_w.reshape(1, dw)]
    full = lambda arr: pl.BlockSpec(arr.shape, lambda i, j: (0,) * arr.ndim)
    blk = lambda w: pl.BlockSpec((1, ln, w), lambda i, j: (i, j, 0))
    halo = 2 * SUBLANES
    return pl.pallas_call(
        _ssd_kernel, grid=(b, t // ln),
        in_specs=[blk(dw), blk(xw),
                  pl.BlockSpec((1, halo, xw), lambda i, j: (i, jnp.maximum(j * (ln // halo) - 1, 0), 0)),
                  blk(LANES)] + [full(x) for x in prm],
        out_specs=blk(dw), out_shape=jax.ShapeDtypeStruct((b, t, dw), BF16),
        scratch_shapes=[pltpu.VMEM((SSD_HEADS, SSD_STATE, SSD_HEAD), F32)],
        compiler_params=_cparams(("parallel", "arbitrary")), name="ssd",
    )(z, xbc, xbc, dtp, *prm)


def _mix_out1_kernel(x_ref, yc_ref, yd_ref, gw_ref, gb_ref, wc_ref, wd_ref, gpost_ref, gpre_ref,
                     wrh_ref, wrl_ref, x1_ref, h_ref, idx_ref, gate_ref):
    yc = _gelu_tanh(yc_ref[...].astype(F32))
    yc = yc * _sigmoid(jnp.dot(gw_ref[...], yc.astype(BF16), preferred_element_type=F32) + gb_ref[...])
    y = _bdot_tn(yc, wc_ref[...]) + _bdot(yd_ref[...], wd_ref[...])
    x1 = x_ref[...] + _rms(y, gpost_ref[...])
    x1_ref[...] = x1
    h = _rms(x1, gpre_ref[...])
    h_ref[...] = h
    hh = h.astype(BF16)
    hl = (h - hh.astype(F32)).astype(BF16)
    wrh = wrh_ref[...]
    logits = (jnp.dot(hh, wrh, preferred_element_type=F32) + jnp.dot(hl, wrh, preferred_element_type=F32)
              + jnp.dot(hh, wrl_ref[...], preferred_element_type=F32))
    lane = _iota2(logits.shape, 1)
    lane_f = lane.astype(F32)
    logits = jnp.where(lane < MOE_EXPERTS, logits, -jnp.inf)
    m1 = jnp.max(logits, axis=-1, keepdims=True)
    i1 = jnp.min(jnp.where(logits == m1, lane_f, float(LANES)), axis=-1, keepdims=True)
    rest = jnp.where(lane_f == i1, -jnp.inf, logits)
    m2 = jnp.max(rest, axis=-1, keepdims=True)
    i2 = jnp.min(jnp.where(rest == m2, lane_f, float(LANES)), axis=-1, keepdims=True)
    e2 = jnp.exp(m2 - m1)
    g1 = 1.0 / (1.0 + e2)
    g2 = e2 / (1.0 + e2)
    idx_ref[...] = jnp.where(lane == 0, i1, jnp.where(lane == 1, i2, 0.0)).astype(jnp.int32)
    gate_ref[...] = jnp.where(lane == 0, g1, jnp.where(lane == 1, g2, 0.0))


def _mix_out1(x2, yct, yd, glu_wt, glu_b, wc, wd, gpost, gpre, wr, tm=512):
    n, d = x2.shape
    wr_p = jnp.pad(wr, ((0, 0), (0, LANES - wr.shape[1])))
    wrh = wr_p.astype(BF16)
    wrl = (wr_p - wrh.astype(F32)).astype(BF16)
    row = lambda w: pl.BlockSpec((tm, w), lambda i: (i, 0))
    full = lambda arr: pl.BlockSpec(arr.shape, lambda i: (0,) * arr.ndim)
    prm = [glu_wt, glu_b, wc, wd, gpost, gpre, wrh, wrl]
    return pl.pallas_call(
        _mix_out1_kernel, grid=(n // tm,),
        in_specs=[row(d), pl.BlockSpec((yct.shape[0], tm), lambda i: (0, i)), row(yd.shape[1])]
        + [full(p) for p in prm],
        out_specs=[row(d), row(d), row(LANES), row(LANES)],
        out_shape=[jax.ShapeDtypeStruct((n, d), F32), jax.ShapeDtypeStruct((n, d), F32),
                   jax.ShapeDtypeStruct((n, LANES), jnp.int32), jax.ShapeDtypeStruct((n, LANES), F32)],
        compiler_params=_cparams(("parallel",)), name="mix_out1",
    )(x2, yct, yd, *prm)


GATHER_UNROLL = 8


def _gather_rows(n_rows, make_copy):
    def body(j, carry):
        for q in range(GATHER_UNROLL):
            make_copy(j * GATHER_UNROLL + q).start(priority=q % 2)
        return carry

    lax.fori_loop(0, n_rows // GATHER_UNROLL, body, 0)


def _moe_kernel(nf, be_ref, tok_ref, dst_ref, nact_ref, h_hbm, wg_hbm, wu_hbm, wd_hbm, y_hbm, buf_ref,
                xb_ref, wg_ref, wu_ref, wd_ref, sa_ref, sb_ref, gsem, ssem, wsem):
    i = pl.program_id(0)
    f = pl.program_id(1)
    n_blocks = pl.num_programs(0)
    tm = buf_ref.shape[1]
    nact = nact_ref[0]
    active = i < nact
    slot = lax.rem(i, 2)
    other = 1 - slot
    xs = lambda sl: buf_ref.at[sl]
    yb = lambda sl: buf_ref.at[2 + sl]

    def gather_copy(block, sl, r):
        tok = tok_ref[block * tm + r]
        return pltpu.make_async_copy(h_hbm.at[pl.ds(tok, 1)], buf_ref.at[sl, pl.ds(r, 1)], gsem.at[sl])

    def scatter_copy(block, sl, r):
        dst = dst_ref[block * tm + r]
        return pltpu.make_async_copy(buf_ref.at[2 + sl, pl.ds(r, 1)], y_hbm.at[pl.ds(dst, 1)], ssem.at[sl])

    def wait_rows(sem_slot_ref, buf):
        pltpu.make_async_copy(h_hbm.at[pl.ds(0, tm)], buf, sem_slot_ref).wait()

    @pl.when(f == 0)
    def _():
        @pl.when(i == 0)
        def _():
            _gather_rows(tm, functools.partial(gather_copy, 0, 0))
            buf_ref[3] = jnp.zeros(buf_ref.shape[1:], F32)
            n_real = y_hbm.shape[0] - 2 * tm
            for half in range(2):
                init = pltpu.make_async_copy(yb(1), y_hbm.at[pl.ds(n_real + half * tm, tm)], ssem.at[0])
                init.start()
                init.wait()

        @pl.when(i <= nact)
        def _():
            wait_rows(gsem.at[slot], xs(slot))

        @pl.when(jnp.logical_and(i >= 1, i <= nact))
        def _():
            wait_rows(ssem.at[slot], yb(slot))

        @pl.when(i == nact)
        def _():
            _gather_rows(tm, functools.partial(scatter_copy, i - 1, other))
            wait_rows(ssem.at[other], yb(other))

        @pl.when(active)
        def _():
            xb_ref[...] = buf_ref[slot].astype(BF16)
            buf_ref[2 + slot] = jnp.zeros(buf_ref.shape[1:], F32)

    e = be_ref[i]
    fresh = jnp.logical_or(i == 0, e != be_ref[jnp.maximum(i - 1, 0)])
    nch = wg_ref.shape[1] // MXU_TILE

    pieces = []
    for c in range(nch):
        cols = slice(c * MXU_TILE, (c + 1) * MXU_TILE)
        pieces.append((wg_hbm.at[e, :, cols], sa_ref, 2 * c, wg_ref, (slice(None), cols)))
        pieces.append((wu_hbm.at[e, :, cols], sa_ref, 2 * c + 1, wu_ref, (slice(None), cols)))
        pieces.append((wd_hbm.at[e, cols, :], sb_ref, c, wd_ref, (cols, slice(None))))
    ahead = 2 * 3

    def piece_copy(p):
        src, stage, k, _, _ = pieces[p]
        sl = k % stage.shape[0]
        return pltpu.make_async_copy(src, stage.at[sl], wsem.at[(0 if stage is sa_ref else sa_ref.shape[0]) + sl])

    def compute(load_weights):
        x = xb_ref[...]
        rows_f = tm // nf
        base = pl.multiple_of(f * rows_f, SUBLANES)
        prev = jnp.where(i == 0, n_blocks - 1, i - 1)
        if load_weights:
            for p in range(ahead):
                piece_copy(p).start()
        for c in range(nch):
            if load_weights:
                for p in range(3 * c, 3 * c + 3):
                    _, stage, k, dst, where = pieces[p]
                    piece_copy(p).wait()
                    dst[where] = stage[k % stage.shape[0]].astype(BF16)
                    if p + ahead < len(pieces):
                        piece_copy(p + ahead).start()
            for r in range(rows_f * c // nch, rows_f * (c + 1) // nch):
                gather_copy(i + 1, other, base + r).start(priority=1)
                scatter_copy(prev, other, base + r).start(priority=1)
            cols = slice(c * MXU_TILE, (c + 1) * MXU_TILE)
            gate = jnp.dot(x, wg_ref[:, cols], preferred_element_type=F32)
            up = jnp.dot(x, wu_ref[:, cols], preferred_element_type=F32)
            buf_ref[2 + slot] += jnp.dot((_silu(gate) * up).astype(BF16), wd_ref[cols, :],
                                         preferred_element_type=F32)

    @pl.when(jnp.logical_and(active, fresh))
    def _():
        compute(True)

    @pl.when(jnp.logical_and(active, jnp.logical_not(fresh)))
    def _():
        compute(False)


def _moe_experts(h, block_expert, slot_tok, slot_dst, nact, n_rows, wg, wu, wd):
    n, d = h.shape
    tm = MOE_ROWS
    n_blocks = slot_tok.shape[0] // tm
    ff = wg.shape[2]
    nf = 1
    anywhere = pl.BlockSpec(memory_space=pl.ANY)
    grid_spec = pltpu.PrefetchScalarGridSpec(
        num_scalar_prefetch=4, grid=(n_blocks, nf),
        in_specs=[anywhere, anywhere, anywhere, anywhere],
        out_specs=pl.BlockSpec(memory_space=pl.ANY),
        scratch_shapes=[pltpu.VMEM((4, tm, d), F32), pltpu.VMEM((tm, d), BF16),
                        pltpu.VMEM((d, ff), BF16), pltpu.VMEM((d, ff), BF16), pltpu.VMEM((ff, d), BF16),
                        pltpu.VMEM((4, d, MXU_TILE), F32), pltpu.VMEM((2, MXU_TILE, d), F32),
                        pltpu.SemaphoreType.DMA((2,)), pltpu.SemaphoreType.DMA((2,)),
                        pltpu.SemaphoreType.DMA((6,))])
    return pl.pallas_call(
        functools.partial(_moe_kernel, nf), grid_spec=grid_spec,
        out_shape=jax.ShapeDtypeStruct((n_rows, d), F32),
        compiler_params=pltpu.CompilerParams(dimension_semantics=("arbitrary", "arbitrary"),
                                             vmem_limit_bytes=VMEM_LIMIT, disable_bounds_checks=True),
        name="moe_experts",
    )(block_expert, slot_tok, slot_dst, nact, h, wg, wu, wd)


def _combine_kernel(x_ref, y0_ref, y1_ref, gate_ref, gpost_ref, o_ref):
    gates = gate_ref[...]
    y = gates[:, 0:1] * y0_ref[...] + gates[:, 1:2] * y1_ref[...]
    o_ref[...] = x_ref[...] + _rms(y, gpost_ref[...])


def _moe_combine(x1, y, gates, gpost, tm=512):
    n, d = x1.shape
    nt = n // tm
    return pl.pallas_call(
        _combine_kernel, grid=(nt,),
        in_specs=[pl.BlockSpec((tm, d), lambda i: (i, 0)), pl.BlockSpec((tm, d), lambda i: (i, 0)),
                  pl.BlockSpec((tm, d), lambda i: (nt + i, 0)),
                  pl.BlockSpec((tm, LANES), lambda i: (i, 0)), pl.BlockSpec((1, d), lambda i: (0, 0))],
        out_specs=pl.BlockSpec((tm, d), lambda i: (i, 0)),
        out_shape=jax.ShapeDtypeStruct((n, d), F32),
        compiler_params=_cparams(("parallel",)), name="moe_combine",
    )(x1, y, y, gates, gpost)


def _moe_plan(idx, n):
    tm = MOE_ROWS
    flat_e = idx[:, :2].reshape(-1)
    onehot = (flat_e[:, None] == jnp.arange(MOE_EXPERTS, dtype=jnp.int32)[None, :]).astype(jnp.int32)
    csum = jnp.cumsum(onehot, axis=0)
    counts = csum[-1]
    rank = jnp.sum((csum - onehot) * onehot, axis=1)
    padded = (counts + tm - 1) // tm * tm
    pend = jnp.cumsum(padded)
    pstart = pend - padded
    dest = (jnp.sum(onehot * pstart[None, :], axis=1) + rank).astype(jnp.int32)
    n_blocks = (2 * n) // tm + MOE_EXPERTS + 1
    n_slots = n_blocks * tm
    slot_pair = jnp.full((n_slots,), -1, jnp.int32).at[dest].set(
        jnp.arange(2 * n, dtype=jnp.int32), unique_indices=True)
    real = slot_pair >= 0
    slot_tok = jnp.where(real, slot_pair // 2, 0)
    s_id = jnp.arange(n_slots, dtype=jnp.int32)
    slot_dst = jnp.where(real, slot_pair % 2 * n + slot_pair // 2, 2 * n + (s_id // tm) % 2 * tm + s_id % tm)
    block_start = jnp.arange(n_blocks, dtype=jnp.int32) * tm
    block_expert = jnp.minimum(jnp.sum((block_start[:, None] >= pend[None, :]).astype(jnp.int32), axis=1),
                               MOE_EXPERTS - 1)
    nact = (pend[-1] // tm).astype(jnp.int32).reshape(1)
    return block_expert, slot_tok, slot_dst, nact, 2 * n + 2 * tm


def kernel(x, l0_norm_pre_mix, l0_w_in, l0_rwkv_mu, l0_rwkv_w0, l0_rwkv_w2, l0_rwkv_a0, l0_rwkv_a2, l0_rwkv_g2, l0_rwkv_k_k, l0_rwkv_k_a, l0_rwkv_r_k, l0_rwkv_ln_w, l0_rwkv_ln_b, l0_gmlp_ln_w, l0_gmlp_ln_b, l0_gmlp_ws, l0_gmlp_bs, l0_w_out, l0_norm_post_mix, l0_norm_pre_ffn, l0_ffn_w_gate, l0_ffn_w_up, l0_ffn_w_down, l0_norm_post_ffn, l1_norm_pre_mix, l1_w_in, l1_s5_a_re, l1_s5_a_im, l1_s5_log_dt, l1_s5_b_re, l1_s5_b_im, l1_s5_c_re, l1_s5_c_im, l1_s5_d, l1_s5_glu_w, l1_s5_glu_b, l1_m2_conv_w, l1_m2_conv_b, l1_m2_dt_bias, l1_m2_a_log, l1_m2_d, l1_m2_norm_w, l1_w_out, l1_norm_post_mix, l1_norm_pre_ffn, l1_moe_router, l1_moe_w_gate, l1_moe_w_up, l1_moe_w_down, l1_norm_post_ffn):
    b, t, d = x.shape
    n = b * t
    x2 = x.reshape(n, d)
    row = lambda vec: vec.reshape(1, -1)

    aw = l0_rwkv_w0.shape[0]
    heads = aw // RWKV_HEAD
    lw_, la_, lg_ = l0_rwkv_w2.shape[0], l0_rwkv_a2.shape[0], l0_rwkv_g2.shape[0]
    a_in = 3 * aw + lw_ + la_ + lg_
    padc = lambda m, wdt: jnp.pad(m, ((0, 0), (0, LANES - wdt)))
    o = 3 * aw
    w_a = jnp.concatenate([l0_w_in[:, :o], padc(l0_w_in[:, o:o + lw_], lw_),
                           padc(l0_w_in[:, o + lw_:o + lw_ + la_], la_),
                           padc(l0_w_in[:, o + lw_ + la_:a_in], lg_)], axis=1).astype(BF16)
    w_b = l0_w_in[:, a_in:].astype(BF16)
    p_a, p_b = _norm_proj(x2, l0_norm_pre_mix, [w_a, w_b], [F32, BF16])
    padv = lambda vec, wdt: jnp.pad(vec, (0, LANES - wdt))
    mu = l0_rwkv_mu
    mu_p = jnp.concatenate([mu[:o], padv(mu[o:o + lw_], lw_), padv(mu[o + lw_:o + lw_ + la_], la_),
                            padv(mu[o + lw_ + la_:], lg_)])
    padr = lambda m: jnp.pad(m, ((0, LANES - m.shape[0]), (0, 0))).astype(BF16)
    hid = jnp.arange(LANES, dtype=jnp.int32) // RWKV_HEAD
    gsum = (hid[:, None] == hid[None, :]).astype(BF16)
    rwkv_prm = [row(mu_p), row(l0_rwkv_w0), padr(l0_rwkv_w2), row(l0_rwkv_a0), padr(l0_rwkv_a2),
                padr(l0_rwkv_g2), row(l0_rwkv_k_k), row(l0_rwkv_k_a), row(l0_rwkv_r_k),
                row(l0_rwkv_ln_w), row(l0_rwkv_ln_b), gsum]
    ya = _rwkv_mix(p_a.reshape(b, t, -1), rwkv_prm, heads)
    yb = _gmlp_mix(p_b.reshape(b, t, -1), l0_gmlp_ln_w, l0_gmlp_ln_b, l0_gmlp_ws, l0_gmlp_bs)
    wo = l0_w_out.astype(BF16)
    x2 = _mix_out0(x2, ya.reshape(n, -1), yb.reshape(n, -1), wo[:aw], wo[aw:], row(l0_norm_post_mix))
    x2 = _ffn(x2, row(l0_norm_pre_ffn), l0_ffn_w_gate.astype(BF16), l0_ffn_w_up.astype(BF16),
              l0_ffn_w_down.astype(BF16), row(l0_norm_post_ffn))

    cw = l1_s5_d.shape[0]
    dw = l1_m2_norm_w.shape[0]
    xw = l1_m2_conv_w.shape[1]
    nh = l1_m2_dt_bias.shape[0]
    w1 = l1_w_in
    w_parts = [w1[:, cw:cw + dw], w1[:, cw + dw:cw + dw + xw], padc(w1[:, cw + dw + xw:], nh)]
    z_d, xbc, dtp, u_t = _norm_proj(x2, l1_norm_pre_mix, [w.astype(BF16) for w in w_parts],
                                    [BF16, BF16, F32], [w1[:, :cw].T.astype(BF16)], [BF16])
    s5_w = _s5_weights(l1_s5_a_re, l1_s5_a_im, l1_s5_log_dt, l1_s5_b_re, l1_s5_b_im, l1_s5_c_re,
                       l1_s5_c_im, l1_s5_d, S5_CHUNK)
    yct = _s5_core(u_t, s5_w, b, t)
    yd = _ssd_mix(z_d.reshape(b, t, dw), xbc.reshape(b, t, xw), dtp.reshape(b, t, LANES),
                  l1_m2_conv_w, l1_m2_conv_b, l1_m2_dt_bias, l1_m2_a_log, l1_m2_d, l1_m2_norm_w)
    wo1 = l1_w_out.astype(BF16)
    x1, h, idx, gates = _mix_out1(x2, yct, yd.reshape(n, dw), l1_s5_glu_w.T.astype(BF16),
                                  l1_s5_glu_b.reshape(cw, 1), wo1[:cw], wo1[cw:], row(l1_norm_post_mix),
                                  row(l1_norm_pre_ffn), l1_moe_router)
    block_expert, slot_tok, slot_dst, nact, n_rows = _moe_plan(idx, n)
    ys = _moe_experts(h, block_expert, slot_tok, slot_dst, nact, n_rows, l1_moe_w_gate, l1_moe_w_up,
                      l1_moe_w_down)
    out = _moe_combine(x1, ys, gates, row(l1_norm_post_ffn))
    return out.reshape(b, t, d)
```

```python
import functools

import jax
import jax.numpy as jnp
from jax import lax
from jax.experimental import pallas as pl
from jax.experimental.pallas import tpu as pltpu

F32 = jnp.float32
BF16 = jnp.bfloat16

EPS = 1e-6
RWKV_GN_EPS = 64e-5
RWKV_HEAD = 64
RWKV_CHUNK = 64
GMLP_CHUNK = 128
GMLP_GROUPS = 4
S5_GROUP_CH = 16
S5_STATE = 64
S5_CHUNK = 128
SSD_HEAD = 64
SSD_HEADS = 8
SSD_GROUPS = 2
SSD_STATE = 128
SSD_CONV = 4
SSD_CHUNK = 128
MOE_EXPERTS = 8
MOE_ROWS = 512
MXU_TILE = 256
LANES = 128
SUBLANES = 8
VMEM_LIMIT = 56 * 1024 * 1024


def _cparams(sem):
    return pltpu.CompilerParams(dimension_semantics=sem, vmem_limit_bytes=VMEM_LIMIT)


def _bdot(a, b):
    return jnp.dot(a.astype(BF16), b.astype(BF16), preferred_element_type=F32)


def _bdot_nt(a, b):
    return lax.dot_general(a.astype(BF16), b.astype(BF16), (((1,), (1,)), ((), ())),
                           preferred_element_type=F32)


def _bdot_tn(a, b):
    return lax.dot_general(a.astype(BF16), b.astype(BF16), (((0,), (0,)), ((), ())),
                           preferred_element_type=F32)


def _split3(x):
    h = x.astype(BF16)
    r1 = x - h.astype(F32)
    m = r1.astype(BF16)
    l = (r1 - m.astype(F32)).astype(BF16)
    return h, m, l


def _dot_x_exact(x, e):
    h, m, l = _split3(x)
    e = e.astype(BF16)
    return (jnp.dot(h, e, preferred_element_type=F32) + jnp.dot(m, e, preferred_element_type=F32)
            + jnp.dot(l, e, preferred_element_type=F32))


def _dot_exact_x(e, x):
    h, m, l = _split3(x)
    e = e.astype(BF16)
    return (jnp.dot(e, h, preferred_element_type=F32) + jnp.dot(e, m, preferred_element_type=F32)
            + jnp.dot(e, l, preferred_element_type=F32))


def _rms(x, g):
    return x * lax.rsqrt(jnp.mean(x * x, axis=-1, keepdims=True) + EPS) * g


def _sigmoid(x):
    return 1.0 / (1.0 + jnp.exp(-x))


def _silu(x):
    return x * _sigmoid(x)


def _softplus(x):
    return jnp.maximum(x, 0.0) + jnp.log(1.0 + jnp.exp(-jnp.abs(x)))


def _gelu_tanh(x):
    return 0.5 * x * (1.0 + jnp.tanh(0.7978845608028654 * (x + 0.044715 * x * x * x)))


def _iota2(shape, dim):
    return lax.broadcasted_iota(jnp.int32, shape, dim)


def _norm_proj_kernel(n_out, n_t, x_ref, g_ref, *refs):
    w_refs = refs[:n_out]
    wt_refs = refs[n_out:n_out + n_t]
    o_refs = refs[n_out + n_t:2 * n_out + n_t]
    ot_refs = refs[2 * n_out + n_t:]
    xn = _rms(x_ref[...], g_ref[...]).astype(BF16)
    for w_ref, o_ref in zip(w_refs, o_refs):
        o_ref[...] = jnp.dot(xn, w_ref[...], preferred_element_type=F32).astype(o_ref.dtype)
    for wt_ref, ot_ref in zip(wt_refs, ot_refs):
        yt = lax.dot_general(wt_ref[...], xn, (((1,), (1,)), ((), ())),
                             preferred_element_type=F32).astype(ot_ref.dtype)
        ng, nchunk, _, ch, ln = ot_ref.shape
        for cl in range(nchunk):
            ot_ref[:, cl, 0, :, :] = yt[:, cl * ln:(cl + 1) * ln].reshape(ng, ch, ln)


def _norm_proj(x2, g, ws, dtypes, wts=(), tdtypes=(), seq=None, ch=S5_GROUP_CH, ln=S5_CHUNK, tm=512):
    n, d = x2.shape
    per_seq = (seq or tm) // tm
    in_specs = [pl.BlockSpec((tm, d), lambda i: (i, 0)), pl.BlockSpec((1, d), lambda i: (0, 0))]
    in_specs += [pl.BlockSpec(w.shape, lambda i: (0, 0)) for w in list(ws) + list(wts)]
    out_specs = [pl.BlockSpec((tm, w.shape[1]), lambda i: (i, 0)) for w in ws]
    out_specs += [pl.BlockSpec((wt.shape[0] // ch, tm // ln, 1, ch, ln),
                               lambda i: (0, i % per_seq, i // per_seq, 0, 0)) for wt in wts]
    out_shape = [jax.ShapeDtypeStruct((n, w.shape[1]), dt) for w, dt in zip(ws, dtypes)]
    out_shape += [jax.ShapeDtypeStruct((wt.shape[0] // ch, seq // ln, n // seq, ch, ln), dt)
                  for wt, dt in zip(wts, tdtypes)]
    return pl.pallas_call(
        functools.partial(_norm_proj_kernel, len(ws), len(wts)),
        grid=(n // tm,), in_specs=in_specs, out_specs=out_specs, out_shape=out_shape,
        compiler_params=_cparams(("parallel",)), name="norm_proj",
    )(x2, g.reshape(1, d), *ws, *wts)


PRE_NAMES = ("v", "g", "bonus", "rt", "kt", "bt", "at", "bh", "kh")


def _rwkv_kernel(heads, nb, p_ref, pp_ref, mu_ref, w0_ref, w2_ref, a0_ref, a2_ref, g2_ref, kk_ref,
                 ka_ref, rk_ref, lnw_ref, lnb_ref, gs_ref, o_ref, z_ref, pre_ref, wl_ref):
    c = pl.program_id(1)
    ln = RWKV_CHUNK
    hd = RWKV_HEAD
    aw = heads * hd

    @pl.when(c == 0)
    def _():
        z_ref[...] = jnp.zeros_like(z_ref)
        pre_ref[...] = jnp.zeros_like(pre_ref)
        wl_ref[...] = jnp.zeros_like(wl_ref)

    tril_f = jnp.where(_iota2((ln, ln), 0) >= _iota2((ln, ln), 1), 1.0, 0.0)
    rows = _iota2((ln, 1), 0)
    gs_tile = gs_ref[...]

    def gs(x):
        nt = x.shape[1] // LANES
        stacked = jnp.concatenate([x[:, j * LANES:(j + 1) * LANES] for j in range(nt)], axis=0)
        red = _dot_x_exact(stacked, gs_tile)
        return jnp.concatenate([red[j * ln:(j + 1) * ln] for j in range(nt)], axis=1)

    pre_idx = {nm: idx for idx, nm in enumerate(PRE_NAMES)}
    pending = []

    def prep_steps():
        for bi in range(nb):
            p = p_ref[bi]
            prev = jnp.where(c == 0, 0.0, pp_ref[bi][SUBLANES - 1:SUBLANES, :])
            ps = jnp.where(rows == 0, prev, pltpu.roll(p, 1, axis=0))
            pm = p + (ps - p) * mu_ref[...]
            r = pm[:, 0:aw]
            k = pm[:, aw:2 * aw]
            v = pm[:, 2 * aw:3 * aw]
            xw = pm[:, 3 * aw:3 * aw + LANES]
            xa = pm[:, 3 * aw + LANES:3 * aw + 2 * LANES]
            xg = pm[:, 3 * aw + 2 * LANES:3 * aw + 3 * LANES]
            yield
            w = w0_ref[...] + _bdot(jnp.tanh(xw), w2_ref[...])
            a = _sigmoid(a0_ref[...] + _bdot(xa, a2_ref[...]))
            g = _bdot(_sigmoid(xg), g2_ref[...])
            yield
            w = -_softplus(-w) - 0.5
            lw = -jnp.exp(w)
            kk = k * kk_ref[...]
            kk_ss = gs(kk * kk)
            yield
            cs = _dot_exact_x(tril_f, lw)
            kk = kk / jnp.maximum(jnp.sqrt(kk_ss), 1e-12)
            kmod = k * (1.0 + (a - 1.0) * ka_ref[...])
            yield
            bonus = gs(r * kmod * rk_ref[...])
            bvec = kk * a
            cs_last = cs[ln - 1:ln, :]
            encs = jnp.exp(-cs)
            yield
            dec_end = jnp.exp(cs_last - cs)
            nxt = dict(v=v, g=g, bonus=bonus, rt=r * jnp.exp(cs), kt=kmod * encs, bt=bvec * encs,
                       at=-kk * jnp.exp(cs - lw), bh=bvec * dec_end, kh=kmod * dec_end)
            pending.append((bi, nxt, jnp.broadcast_to(jnp.exp(cs_last), (SUBLANES, aw))))
            yield

    prep = prep_steps()
    tick = lambda: next(prep, None)

    lane = _iota2((ln, LANES), 1)
    lane_in = jnp.where(lane >= hd, lane - hd, lane)
    trow = _iota2((ln, LANES), 0)
    left = lane < hd
    tril_p = lane_in <= trow
    stril_p = lane_in < trow
    eye_p = lane_in == trow
    eye_pf = jnp.where(eye_p, 1.0, 0.0)

    def bd(x):
        xb = x.astype(BF16)
        zero = jnp.zeros_like(xb)
        return jnp.concatenate([jnp.where(left, xb, zero), jnp.where(left, zero, xb)], axis=0)

    def dot(a, b):
        return jnp.dot(a.astype(BF16), b, preferred_element_type=F32)

    npair = heads // 2
    pairs = [(bi, j) for bi in range(nb) for j in range(npair)]

    class _Tiles:
        def __init__(self, name):
            self.idx = pre_idx[name]

        def __getitem__(self, i):
            bi, j = pairs[i]
            return pre_ref[self.idx, bi, :, j * LANES:(j + 1) * LANES]

    at, rt, bt, kt, vv, bh, kh = (_Tiles(n) for n in ("at", "rt", "bt", "kt", "v", "bh", "kh"))
    wl = [wl_ref[bi, 0:1, j * LANES:(j + 1) * LANES] for bi, j in pairs]
    z_all = z_ref[...]
    zs = [z_all[bi, j] for bi, j in pairs]
    npr = range(len(pairs))
    lhs = [jnp.concatenate([at[i], rt[i]], axis=0).astype(BF16) for i in npr]
    abk = [lax.dot_general(lhs[i], jnp.concatenate([bd(bt[i]), bd(kt[i])], axis=0),
                           (((1,), (1,)), ((), ())), preferred_element_type=F32) for i in npr]
    ab = [abk[i][:, :LANES] for i in npr]
    ak = [abk[i][:, LANES:] for i in npr]
    tick()
    nmat = [jnp.where(stril_p, ab[i][:ln], 0.0) for i in npr]
    tinv = [eye_pf + nmat[i] for i in npr]
    npow = [dot(nmat[i], bd(nmat[i])) for i in npr]
    tick()
    for step in range(5):
        bdn = [bd(npow[i]) for i in npr]
        if step < 4:
            both = [dot(jnp.concatenate([tinv[i], npow[i]], axis=0), bdn[i]) for i in npr]
            tinv = [tinv[i] + both[i][:ln] for i in npr]
            npow = [both[i][ln:] for i in npr]
        else:
            tinv = [tinv[i] + dot(tinv[i], bdn[i]) for i in npr]
        tick()
    bdv = [bd(vv[i]) for i in npr]
    bdz = [bd(zs[i]) for i in npr]
    xmat = [dot(jnp.concatenate([jnp.where(stril_p, ak[i][:ln], 0.0), at[i]], axis=1),
                jnp.concatenate([bdv[i], bdz[i]], axis=0)) for i in npr]
    tick()
    u = [dot(tinv[i], bd(xmat[i])) for i in npr]
    tick()
    ys_p = [dot(jnp.concatenate([rt[i], jnp.where(tril_p, ab[i][ln:], 0.0),
                                 jnp.where(tril_p, ak[i][ln:], 0.0)], axis=1),
                jnp.concatenate([bdz[i], bd(u[i]), bdv[i]], axis=0)) for i in npr]
    tick()
    cross = [_bdot_tn(jnp.concatenate([bh[i], kh[i]], axis=0), jnp.concatenate([u[i], vv[i]], axis=0))
             for i in npr]
    tick()
    z_new = []
    for i in npr:
        dg = jnp.where(eye_p, wl[i], 0.0)
        wl_i = jnp.sum(jnp.where(left, dg, 0.0), axis=1, keepdims=True)
        wl_j = jnp.sum(jnp.where(left, 0.0, dg), axis=1, keepdims=True)
        z_new.append(jnp.where(left, wl_i, wl_j) * zs[i] + jnp.where(left, cross[i][:ln], cross[i][ln:]))
    z_ref[...] = jnp.stack(z_new, axis=0).reshape(z_ref.shape)

    tick()
    inv = 1.0 / hd
    for bi in range(nb):
        y = jnp.concatenate(ys_p[bi * npair:(bi + 1) * npair], axis=1)
        mean = gs(y) * inv
        d = y - mean
        var = gs(d * d) * inv
        yn = d * lax.rsqrt(var + RWKV_GN_EPS) * lnw_ref[...] + lnb_ref[...]
        o_ref[bi] = ((yn + pre_ref[pre_idx["bonus"], bi] * pre_ref[pre_idx["v"], bi])
                     * pre_ref[pre_idx["g"], bi]).astype(o_ref.dtype)
        tick()
    for _ in prep:
        pass
    for bi, nxt, wl_next in pending:
        for nm, idx in pre_idx.items():
            pre_ref[idx, bi] = nxt[nm]
        wl_ref[bi] = wl_next


def _rwkv_mix(p_a, prm, heads, nb=4):
    b, t, cin = p_a.shape
    aw = heads * RWKV_HEAD
    ln = RWKV_CHUNK
    nc = t // ln
    sub = ln // SUBLANES
    full = lambda arr: pl.BlockSpec(arr.shape, lambda i, j: (0,) * arr.ndim)
    in_specs = [pl.BlockSpec((nb, ln, cin), lambda i, j: (i, jnp.minimum(j, nc - 1), 0)),
                pl.BlockSpec((nb, SUBLANES, cin),
                             lambda i, j: (i, jnp.maximum(jnp.minimum(j, nc - 1) * sub - 1, 0), 0))]
    in_specs += [full(x) for x in prm]
    return pl.pallas_call(
        functools.partial(_rwkv_kernel, heads, nb),
        grid=(b // nb, nc + 1), in_specs=in_specs,
        out_specs=pl.BlockSpec((nb, ln, aw), lambda i, j: (i, jnp.maximum(j - 1, 0), 0)),
        out_shape=jax.ShapeDtypeStruct((b, t, aw), BF16),
        scratch_shapes=[pltpu.VMEM((nb, heads // 2, RWKV_HEAD, 2 * RWKV_HEAD), F32),
                        pltpu.VMEM((len(PRE_NAMES), nb, ln, aw), F32),
                        pltpu.VMEM((nb, SUBLANES, aw), F32)],
        compiler_params=_cparams(("parallel", "arbitrary")), name="rwkv7",
    )(p_a, p_a, *prm)


def _gmlp_kernel(p_ref, lnw_ref, lnb_ref, ws_ref, bs_ref, o_ref):
    ln = GMLP_CHUNK
    bw = p_ref.shape[2] // 2
    gd = bw // GMLP_GROUPS
    tril = _iota2((ln, ln), 0) >= _iota2((ln, ln), 1)
    ws_c = [jnp.where(tril, ws_ref[gi], 0.0).astype(BF16) for gi in range(GMLP_GROUPS)]
    for ci in range(p_ref.shape[1] // ln):
        x = _gelu_tanh(p_ref[0, ci * ln:(ci + 1) * ln, :].astype(F32))
        for gi in range(GMLP_GROUPS):
            u = x[:, gi * gd:(gi + 1) * gd]
            v = x[:, bw + gi * gd:bw + (gi + 1) * gd]
            mean = jnp.mean(v, axis=-1, keepdims=True)
            d = v - mean
            var = jnp.mean(d * d, axis=-1, keepdims=True)
            vn = d * lax.rsqrt(var + EPS) * lnw_ref[gi:gi + 1, :] + lnb_ref[gi:gi + 1, :]
            s = jnp.dot(ws_c[gi], vn.astype(BF16), preferred_element_type=F32) + bs_ref[gi]
            o_ref[0, ci * ln:(ci + 1) * ln, gi * gd:(gi + 1) * gd] = (u * s).astype(o_ref.dtype)


def _gmlp_mix(p_b, ln_w, ln_b, ws, bs, rows=512):
    b, t, cin = p_b.shape
    bw = cin // 2
    gd = bw // GMLP_GROUPS
    bs_b = jnp.broadcast_to(bs[:, :, None], (GMLP_GROUPS, GMLP_CHUNK, gd))
    full = lambda arr: pl.BlockSpec(arr.shape, lambda i, j: (0,) * arr.ndim)
    return pl.pallas_call(
        _gmlp_kernel, grid=(b, t // rows),
        in_specs=[pl.BlockSpec((1, rows, cin), lambda i, j: (i, j, 0)),
                  full(ln_w), full(ln_b), full(ws), full(bs_b)],
        out_specs=pl.BlockSpec((1, rows, bw), lambda i, j: (i, j, 0)),
        out_shape=jax.ShapeDtypeStruct((b, t, bw), BF16),
        compiler_params=_cparams(("parallel", "parallel")), name="gmlp",
    )(p_b, ln_w, ln_b, ws, bs_b)


def _mix_out0_kernel(x_ref, ya_ref, yb_ref, wa_ref, wb_ref, g_ref, o_ref):
    y = _bdot(ya_ref[...], wa_ref[...]) + _bdot(yb_ref[...], wb_ref[...])
    o_ref[...] = x_ref[...] + _rms(y, g_ref[...])


def _mix_out0(x2, ya, yb, wa, wb, g, tm=512):
    n, d = x2.shape
    row = lambda arr: pl.BlockSpec((tm, arr.shape[1]), lambda i: (i, 0))
    full = lambda arr: pl.BlockSpec(arr.shape, lambda i: (0,) * arr.ndim)
    return pl.pallas_call(
        _mix_out0_kernel, grid=(n // tm,),
        in_specs=[row(x2), row(ya), row(yb), full(wa), full(wb), full(g)],
        out_specs=row(x2), out_shape=jax.ShapeDtypeStruct((n, d), F32),
        compiler_params=_cparams(("parallel",)), name="mix_out0",
    )(x2, ya, yb, wa, wb, g)


def _ffn_kernel(fc, x_ref, gpre_ref, wg_ref, wu_ref, wd_ref, gpost_ref, o_ref):
    h = _rms(x_ref[...], gpre_ref[...]).astype(BF16)
    acc = None
    for c in range(wg_ref.shape[1] // fc):
        cols = slice(c * fc, (c + 1) * fc)
        gate = jnp.dot(h, wg_ref[:, cols], preferred_element_type=F32)
        up = jnp.dot(h, wu_ref[:, cols], preferred_element_type=F32)
        part = jnp.dot((_silu(gate) * up).astype(BF16), wd_ref[cols, :], preferred_element_type=F32)
        acc = part if acc is None else acc + part
    o_ref[...] = x_ref[...] + _rms(acc, gpost_ref[...])


def _ffn(x2, gpre, wg, wu, wd, gpost, tm=512, fc=MXU_TILE):
    n, d = x2.shape
    full = lambda arr: pl.BlockSpec(arr.shape, lambda i: (0,) * arr.ndim)
    return pl.pallas_call(
        functools.partial(_ffn_kernel, fc), grid=(n // tm,),
        in_specs=[pl.BlockSpec((tm, d), lambda i: (i, 0)), full(gpre), full(wg), full(wu), full(wd),
                  full(gpost)],
        out_specs=pl.BlockSpec((tm, d), lambda i: (i, 0)),
        out_shape=jax.ShapeDtypeStruct((n, d), F32),
        compiler_params=_cparams(("parallel",)), name="ffn",
    )(x2, gpre, wg, wu, wd, gpost)


def _s5_kernel(nc, nb, u_ref, tap_ref, wsr_ref, wsi_ref, wcr_ref, wci_ref, alr_ref, ali_ref, d_ref, o_ref,
               toep_ref):
    ch, ln = u_ref.shape[3], u_ref.shape[4]
    u = u_ref[0].reshape(nc * nb, ch * ln)
    taps = tap_ref[0]
    width = ch * ln
    keep = (_iota2((ln, width), 1) & (ln - 1)) >= _iota2((ln, width), 0)
    for cin in range(ch):
        src = jnp.broadcast_to(taps[cin:cin + 1, :], (ln, width))
        blk = jnp.where(keep, pltpu.roll(src, 0, 1, stride=1, stride_axis=0), 0.0)
        toep_ref[cin * ln:(cin + 1) * ln, :] = blk.astype(BF16)
    y = jnp.dot(u, toep_ref[...], preferred_element_type=F32)
    xer = jnp.dot(u, wsr_ref[0], preferred_element_type=F32)
    xei = jnp.dot(u, wsi_ref[0], preferred_element_type=F32)
    alr = alr_ref[0]
    ali = ali_ref[0]
    cr = jnp.zeros((nb, xer.shape[1]), F32)
    ci = jnp.zeros((nb, xer.shape[1]), F32)
    prs, pis = [], []
    for c in range(nc):
        prs.append(cr)
        pis.append(ci)
        er = xer[c * nb:(c + 1) * nb]
        ei = xei[c * nb:(c + 1) * nb]
        cr, ci = alr * cr - ali * ci + er, alr * ci + ali * cr + ei
    pr = jnp.concatenate(prs, axis=0)
    pi = jnp.concatenate(pis, axis=0)
    y = y + _bdot(pr, wcr_ref[0]) + _bdot(pi, wci_ref[0])
    o_ref[0] = (y + d_ref[0] * u.astype(F32)).astype(o_ref.dtype).reshape(nc, nb, ch, ln)


def _s5_weights(a_re, a_im, log_dt, b_re, b_im, c_re, c_im, d_skip, ln):
    g, st = a_re.shape
    ch = b_re.shape[2]
    dt = jnp.exp(log_dt)[:, None]
    lr, li = a_re, a_im
    tau = jnp.arange(ln + 1, dtype=F32)[:, None, None]
    mag = jnp.exp(lr[None] * dt[None] * tau)
    pw_r = mag * jnp.cos(li[None] * dt[None] * tau)
    pw_i = mag * jnp.sin(li[None] * dt[None] * tau)
    ab_r, ab_i = pw_r[1], pw_i[1]
    nr, ni = ab_r - 1.0, ab_i
    den = lr * lr + li * li
    fr, fi = (nr * lr + ni * li) / den, (ni * lr - nr * li) / den
    bb_r = fr[..., None] * b_re - fi[..., None] * b_im
    bb_i = fr[..., None] * b_im + fi[..., None] * b_re
    cp_r = c_re[None] * pw_r[:ln, :, None, :] - c_im[None] * pw_i[:ln, :, None, :]
    cp_i = c_re[None] * pw_i[:ln, :, None, :] + c_im[None] * pw_r[:ln, :, None, :]
    hp = lax.Precision.HIGHEST
    taps = (jnp.einsum('tgcp,gpd->gdct', cp_r, bb_r, precision=hp)
            - jnp.einsum('tgcp,gpd->gdct', cp_i, bb_i, precision=hp))
    taps = taps.reshape(g, ch, ch * ln)
    rev_r, rev_i = pw_r[:ln][::-1], pw_i[:ln][::-1]
    ws_r = rev_r[..., None] * bb_r[None] - rev_i[..., None] * bb_i[None]
    ws_i = rev_r[..., None] * bb_i[None] + rev_i[..., None] * bb_r[None]
    ws_r = ws_r.transpose(1, 3, 0, 2).reshape(g, ch * ln, st)
    ws_i = ws_i.transpose(1, 3, 0, 2).reshape(g, ch * ln, st)
    q_r, q_i = pw_r[1:ln + 1], pw_i[1:ln + 1]
    wc_r = c_re[None] * q_r[:, :, None, :] - c_im[None] * q_i[:, :, None, :]
    wc_i = -(c_re[None] * q_i[:, :, None, :] + c_im[None] * q_r[:, :, None, :])
    wc_r = wc_r.transpose(1, 3, 2, 0).reshape(g, st, ch * ln)
    wc_i = wc_i.transpose(1, 3, 2, 0).reshape(g, st, ch * ln)
    al_r = pw_r[ln].reshape(g, 1, st)
    al_i = pw_i[ln].reshape(g, 1, st)
    d_t = jnp.repeat(d_skip.reshape(g, ch), ln, axis=1).reshape(g, 1, ch * ln)
    return (taps, ws_r.astype(BF16), ws_i.astype(BF16), wc_r.astype(BF16),
            wc_i.astype(BF16), al_r, al_i, d_t)


def _s5_core(u5, weights):
    g, nc, b, ch, ln = u5.shape
    blk = pl.BlockSpec((1, nc, b, ch, ln), lambda i: (i, 0, 0, 0, 0))
    per_g = lambda arr: pl.BlockSpec((1,) + arr.shape[1:], lambda i: (i, 0, 0))
    return pl.pallas_call(
        functools.partial(_s5_kernel, nc, b), grid=(g,),
        in_specs=[blk] + [per_g(w) for w in weights],
        out_specs=blk, out_shape=jax.ShapeDtypeStruct(u5.shape, BF16),
        scratch_shapes=[pltpu.VMEM((ln * ch, ln * ch), BF16)],
        compiler_params=_cparams(("parallel",)), name="s5",
    )(u5, *weights)


def _ssd_kernel(z_ref, xbc_ref, xp_ref, dt_ref, cw_ref, cb_ref, dtb_ref, alog_ref, dsk_ref, nw_ref,
                o_ref, s_ref):
    c = pl.program_id(1)
    ln = SSD_CHUNK
    hd = SSD_HEAD
    dw = SSD_HEADS * hd
    gn = SSD_STATE

    @pl.when(c == 0)
    def _():
        s_ref[...] = jnp.zeros_like(s_ref)

    xbc = xbc_ref[0]
    halo = xp_ref.shape[1]
    prev = jnp.where(c == 0, jnp.zeros_like(xp_ref[0]), xp_ref[0])
    full = jnp.concatenate([prev, xbc], axis=0).astype(BF16)
    conv = cb_ref[...] + cw_ref[SSD_CONV - 1:SSD_CONV, :] * xbc.astype(F32)
    for j in range(SSD_CONV - 1):
        lag = SSD_CONV - 1 - j
        pick = _iota2((ln, halo + ln), 1) == _iota2((ln, halo + ln), 0) + (halo - lag)
        shifted = jnp.dot(jnp.where(pick, 1.0, 0.0).astype(BF16), full, preferred_element_type=F32)
        conv = conv + cw_ref[j:j + 1, :] * shifted
    act = _silu(conv)
    xh = act[:, :dw]
    dt = _softplus(dt_ref[0] + dtb_ref[...])
    adt = -jnp.exp(alog_ref[...]) * dt
    tril = _iota2((ln, ln), 0) >= _iota2((ln, ln), 1)
    acs = _dot_exact_x(jnp.where(tril, 1.0, 0.0), adt)
    acs_t = acs.T
    tot = acs[ln - 1:ln, :]
    hg = SSD_HEADS // SSD_GROUPS
    s_all = s_ref[...]
    y_heads, s_heads = [], []
    for gi in range(SSD_GROUPS):
        bm = act[:, dw + gi * gn:dw + (gi + 1) * gn]
        cm = act[:, dw + SSD_GROUPS * gn + gi * gn:dw + SSD_GROUPS * gn + (gi + 1) * gn]
        cb = _bdot_nt(cm, bm)
        for hh in range(hg):
            h = gi * hg + hh
            sl = slice(h * hd, (h + 1) * hd)
            col = acs[:, h:h + 1]
            rowv = acs_t[h:h + 1, :]
            lmat = jnp.exp(jnp.where(tril, col - rowv, -jnp.inf))
            xh_h = xh[:, sl]
            xdt = xh_h * dt[:, h:h + 1]
            tot_h = tot[:, h:h + 1]
            st = s_all[h]
            y_h = _bdot(cb * lmat, xdt) + jnp.exp(col) * _bdot(cm, st)
            s_heads.append(jnp.exp(tot_h) * st + _bdot_tn(bm * jnp.exp(tot_h - col), xdt))
            y_heads.append(y_h + dsk_ref[:, sl] * xh_h)
    s_ref[...] = jnp.stack(s_heads, axis=0)
    y = jnp.concatenate(y_heads, axis=1) * _silu(z_ref[0].astype(F32))
    gw = dw // SSD_GROUPS
    for gi in range(SSD_GROUPS):
        yg = y[:, gi * gw:(gi + 1) * gw]
        yg = yg * lax.rsqrt(jnp.mean(yg * yg, axis=-1, keepdims=True) + EPS)
        o_ref[0, :, gi * gw:(gi + 1) * gw] = (yg * nw_ref[:, gi * gw:(gi + 1) * gw]).astype(o_ref.dtype)


def _ssd_mix(z, xbc, dtp, conv_w, conv_b, dt_bias, a_log, d_skip, norm_w):
    b, t, dw = z.shape
    ln = SSD_CHUNK
    xw = xbc.shape[2]
    pad = lambda vec: jnp.pad(vec, (0, LANES - vec.shape[0])).reshape(1, LANES)
    dsk = jnp.repeat(d_skip, SSD_HEAD).reshape(1, dw)
    prm = [conv_w, conv_b.reshape(1, xw), pad(dt_bias), pad(a_log), dsk, norm_w.reshape(1, dw)]
    full = lambda arr: pl.BlockSpec(arr.shape, lambda i, j: (0,) * arr.ndim)
    blk = lambda w: pl.BlockSpec((1, ln, w), lambda i, j: (i, j, 0))
    halo = 2 * SUBLANES
    return pl.pallas_call(
        _ssd_kernel, grid=(b, t // ln),
        in_specs=[blk(dw), blk(xw),
                  pl.BlockSpec((1, halo, xw), lambda i, j: (i, jnp.maximum(j * (ln // halo) - 1, 0), 0)),
                  blk(LANES)] + [full(x) for x in prm],
        out_specs=blk(dw), out_shape=jax.ShapeDtypeStruct((b, t, dw), BF16),
        scratch_shapes=[pltpu.VMEM((SSD_HEADS, SSD_STATE, SSD_HEAD), F32)],
        compiler_params=_cparams(("parallel", "arbitrary")), name="ssd",
    )(z, xbc, xbc, dtp, *prm)


def _mix_out1_kernel(x_ref, yc_ref, yd_ref, gw_ref, gb_ref, wc_ref, wd_ref, gpost_ref, gpre_ref,
                     wrh_ref, wrl_ref, x1_ref, h_ref, idx_ref, gate_ref):
    ng, nchunk, _, ch, ln = yc_ref.shape
    yc = jnp.concatenate([yc_ref[:, cl, 0, :, :].reshape(ng * ch, ln) for cl in range(nchunk)], axis=1)
    yc = _gelu_tanh(yc.astype(F32))
    yc = yc * _sigmoid(jnp.dot(gw_ref[...], yc.astype(BF16), preferred_element_type=F32) + gb_ref[...])
    y = _bdot_tn(yc, wc_ref[...]) + _bdot(yd_ref[...], wd_ref[...])
    x1 = x_ref[...] + _rms(y, gpost_ref[...])
    x1_ref[...] = x1
    h = _rms(x1, gpre_ref[...])
    h_ref[...] = h
    hh = h.astype(BF16)
    hl = (h - hh.astype(F32)).astype(BF16)
    wrh = wrh_ref[...]
    logits = (jnp.dot(hh, wrh, preferred_element_type=F32) + jnp.dot(hl, wrh, preferred_element_type=F32)
              + jnp.dot(hh, wrl_ref[...], preferred_element_type=F32))
    lane = _iota2(logits.shape, 1)
    lane_f = lane.astype(F32)
    logits = jnp.where(lane < MOE_EXPERTS, logits, -jnp.inf)
    m1 = jnp.max(logits, axis=-1, keepdims=True)
    i1 = jnp.min(jnp.where(logits == m1, lane_f, float(LANES)), axis=-1, keepdims=True)
    rest = jnp.where(lane_f == i1, -jnp.inf, logits)
    m2 = jnp.max(rest, axis=-1, keepdims=True)
    i2 = jnp.min(jnp.where(rest == m2, lane_f, float(LANES)), axis=-1, keepdims=True)
    e2 = jnp.exp(m2 - m1)
    g1 = 1.0 / (1.0 + e2)
    g2 = e2 / (1.0 + e2)
    idx_ref[...] = jnp.where(lane == 0, i1, jnp.where(lane == 1, i2, 0.0)).astype(jnp.int32)
    gate_ref[...] = jnp.where(lane == 0, g1, jnp.where(lane == 1, g2, 0.0))


def _mix_out1(x2, y5, yd, glu_wt, glu_b, wc, wd, gpost, gpre, wr, tm=512):
    n, d = x2.shape
    ng, nc, _, ch, ln = y5.shape
    per_seq = nc * ln // tm
    wr_p = jnp.pad(wr, ((0, 0), (0, LANES - wr.shape[1])))
    wrh = wr_p.astype(BF16)
    wrl = (wr_p - wrh.astype(F32)).astype(BF16)
    row = lambda w: pl.BlockSpec((tm, w), lambda i: (i, 0))
    full = lambda arr: pl.BlockSpec(arr.shape, lambda i: (0,) * arr.ndim)
    prm = [glu_wt, glu_b, wc, wd, gpost, gpre, wrh, wrl]
    return pl.pallas_call(
        _mix_out1_kernel, grid=(n // tm,),
        in_specs=[row(d), pl.BlockSpec((ng, tm // ln, 1, ch, ln),
                                       lambda i: (0, i % per_seq, i // per_seq, 0, 0)),
                  row(yd.shape[1])] + [full(p) for p in prm],
        out_specs=[row(d), row(d), row(LANES), row(LANES)],
        out_shape=[jax.ShapeDtypeStruct((n, d), F32), jax.ShapeDtypeStruct((n, d), F32),
                   jax.ShapeDtypeStruct((n, LANES), jnp.int32), jax.ShapeDtypeStruct((n, LANES), F32)],
        compiler_params=_cparams(("parallel",)), name="mix_out1",
    )(x2, y5, yd, *prm)


GATHER_UNROLL = 8


def _gather_rows(n_rows, make_copy):
    def body(j, carry):
        for q in range(GATHER_UNROLL):
            make_copy(j * GATHER_UNROLL + q).start(priority=q % 2)
        return carry

    lax.fori_loop(0, n_rows // GATHER_UNROLL, body, 0)


def _moe_kernel(nf, be_ref, tok_ref, dst_ref, nact_ref, h_hbm, wg_hbm, wu_hbm, wd_hbm, y_hbm, buf_ref,
                xb_ref, wg_ref, wu_ref, wd_ref, sa_ref, sb_ref, gsem, ssem, wsem):
    i = pl.program_id(0)
    f = pl.program_id(1)
    n_blocks = pl.num_programs(0)
    tm = buf_ref.shape[1]
    nact = nact_ref[0]
    active = i < nact
    slot = lax.rem(i, 2)
    other = 1 - slot
    xs = lambda sl: buf_ref.at[sl]
    yb = lambda sl: buf_ref.at[2 + sl]

    def gather_copy(block, sl, r):
        tok = tok_ref[block * tm + r]
        return pltpu.make_async_copy(h_hbm.at[pl.ds(tok, 1)], buf_ref.at[sl, pl.ds(r, 1)], gsem.at[sl])

    def scatter_copy(block, sl, r):
        dst = dst_ref[block * tm + r]
        return pltpu.make_async_copy(buf_ref.at[2 + sl, pl.ds(r, 1)], y_hbm.at[pl.ds(dst, 1)], ssem.at[sl])

    def wait_rows(sem_slot_ref, buf):
        pltpu.make_async_copy(h_hbm.at[pl.ds(0, tm)], buf, sem_slot_ref).wait()

    @pl.when(f == 0)
    def _():
        @pl.when(i == 0)
        def _():
            _gather_rows(tm, functools.partial(gather_copy, 0, 0))
            buf_ref[3] = jnp.zeros(buf_ref.shape[1:], F32)
            n_real = y_hbm.shape[0] - 2 * tm
            for half in range(2):
                init = pltpu.make_async_copy(yb(1), y_hbm.at[pl.ds(n_real + half * tm, tm)], ssem.at[0])
                init.start()
                init.wait()

        @pl.when(i <= nact)
        def _():
            wait_rows(gsem.at[slot], xs(slot))

        @pl.when(jnp.logical_and(i >= 1, i <= nact))
        def _():
            wait_rows(ssem.at[slot], yb(slot))

        @pl.when(i == nact)
        def _():
            _gather_rows(tm, functools.partial(scatter_copy, i - 1, other))
            wait_rows(ssem.at[other], yb(other))

        @pl.when(active)
        def _():
            xb_ref[...] = buf_ref[slot].astype(BF16)
            buf_ref[2 + slot] = jnp.zeros(buf_ref.shape[1:], F32)

    e = be_ref[i]
    fresh = jnp.logical_or(i == 0, e != be_ref[jnp.maximum(i - 1, 0)])
    nch = wg_ref.shape[1] // MXU_TILE

    pieces = []
    for c in range(nch):
        cols = slice(c * MXU_TILE, (c + 1) * MXU_TILE)
        pieces.append((wg_hbm.at[e, :, cols], sa_ref, 2 * c, wg_ref, (slice(None), cols)))
        pieces.append((wu_hbm.at[e, :, cols], sa_ref, 2 * c + 1, wu_ref, (slice(None), cols)))
        pieces.append((wd_hbm.at[e, cols, :], sb_ref, c, wd_ref, (cols, slice(None))))
    ahead = 2 * 3

    def piece_copy(p):
        src, stage, k, _, _ = pieces[p]
        sl = k % stage.shape[0]
        return pltpu.make_async_copy(src, stage.at[sl], wsem.at[(0 if stage is sa_ref else sa_ref.shape[0]) + sl])

    def compute(load_weights):
        x = xb_ref[...]
        rows_f = tm // nf
        base = pl.multiple_of(f * rows_f, SUBLANES)
        prev = jnp.where(i == 0, n_blocks - 1, i - 1)
        if load_weights:
            for p in range(ahead):
                piece_copy(p).start()
        for c in range(nch):
            if load_weights:
                for p in range(3 * c, 3 * c + 3):
                    _, stage, k, dst, where = pieces[p]
                    piece_copy(p).wait()
                    dst[where] = stage[k % stage.shape[0]].astype(BF16)
                    if p + ahead < len(pieces):
                        piece_copy(p + ahead).start()
            for r in range(rows_f * c // nch, rows_f * (c + 1) // nch):
                gather_copy(i + 1, other, base + r).start(priority=1)
                scatter_copy(prev, other, base + r).start(priority=1)
            cols = slice(c * MXU_TILE, (c + 1) * MXU_TILE)
            gate = jnp.dot(x, wg_ref[:, cols], preferred_element_type=F32)
            up = jnp.dot(x, wu_ref[:, cols], preferred_element_type=F32)
            buf_ref[2 + slot] += jnp.dot((_silu(gate) * up).astype(BF16), wd_ref[cols, :],
                                         preferred_element_type=F32)

    @pl.when(jnp.logical_and(active, fresh))
    def _():
        compute(True)

    @pl.when(jnp.logical_and(active, jnp.logical_not(fresh)))
    def _():
        compute(False)


def _moe_experts(h, block_expert, slot_tok, slot_dst, nact, n_rows, wg, wu, wd):
    n, d = h.shape
    tm = MOE_ROWS
    n_blocks = slot_tok.shape[0] // tm
    ff = wg.shape[2]
    nf = 1
    anywhere = pl.BlockSpec(memory_space=pl.ANY)
    grid_spec = pltpu.PrefetchScalarGridSpec(
        num_scalar_prefetch=4, grid=(n_blocks, nf),
        in_specs=[anywhere, anywhere, anywhere, anywhere],
        out_specs=pl.BlockSpec(memory_space=pl.ANY),
        scratch_shapes=[pltpu.VMEM((4, tm, d), F32), pltpu.VMEM((tm, d), BF16),
                        pltpu.VMEM((d, ff), BF16), pltpu.VMEM((d, ff), BF16), pltpu.VMEM((ff, d), BF16),
                        pltpu.VMEM((4, d, MXU_TILE), F32), pltpu.VMEM((2, MXU_TILE, d), F32),
                        pltpu.SemaphoreType.DMA((2,)), pltpu.SemaphoreType.DMA((2,)),
                        pltpu.SemaphoreType.DMA((6,))])
    return pl.pallas_call(
        functools.partial(_moe_kernel, nf), grid_spec=grid_spec,
        out_shape=jax.ShapeDtypeStruct((n_rows, d), F32),
        compiler_params=pltpu.CompilerParams(dimension_semantics=("arbitrary", "arbitrary"),
                                             vmem_limit_bytes=VMEM_LIMIT, disable_bounds_checks=True),
        name="moe_experts",
    )(block_expert, slot_tok, slot_dst, nact, h, wg, wu, wd)


def _combine_kernel(x_ref, y0_ref, y1_ref, gate_ref, gpost_ref, o_ref):
    gates = gate_ref[...]
    y = gates[:, 0:1] * y0_ref[...] + gates[:, 1:2] * y1_ref[...]
    o_ref[...] = x_ref[...] + _rms(y, gpost_ref[...])


def _moe_combine(x1, y, gates, gpost, tm=512):
    n, d = x1.shape
    nt = n // tm
    return pl.pallas_call(
        _combine_kernel, grid=(nt,),
        in_specs=[pl.BlockSpec((tm, d), lambda i: (i, 0)), pl.BlockSpec((tm, d), lambda i: (i, 0)),
                  pl.BlockSpec((tm, d), lambda i: (nt + i, 0)),
                  pl.BlockSpec((tm, LANES), lambda i: (i, 0)), pl.BlockSpec((1, d), lambda i: (0, 0))],
        out_specs=pl.BlockSpec((tm, d), lambda i: (i, 0)),
        out_shape=jax.ShapeDtypeStruct((n, d), F32),
        compiler_params=_cparams(("parallel",)), name="moe_combine",
    )(x1, y, y, gates, gpost)


def _moe_plan(idx, n):
    tm = MOE_ROWS
    flat_e = idx[:, :2].reshape(-1)
    onehot = (flat_e[:, None] == jnp.arange(MOE_EXPERTS, dtype=jnp.int32)[None, :]).astype(jnp.int32)
    csum = jnp.cumsum(onehot, axis=0)
    counts = csum[-1]
    rank = jnp.sum((csum - onehot) * onehot, axis=1)
    padded = (counts + tm - 1) // tm * tm
    pend = jnp.cumsum(padded)
    pstart = pend - padded
    dest = (jnp.sum(onehot * pstart[None, :], axis=1) + rank).astype(jnp.int32)
    n_blocks = (2 * n) // tm + MOE_EXPERTS + 1
    n_slots = n_blocks * tm
    slot_pair = jnp.full((n_slots,), -1, jnp.int32).at[dest].set(
        jnp.arange(2 * n, dtype=jnp.int32), unique_indices=True)
    real = slot_pair >= 0
    slot_tok = jnp.where(real, slot_pair // 2, 0)
    s_id = jnp.arange(n_slots, dtype=jnp.int32)
    slot_dst = jnp.where(real, slot_pair % 2 * n + slot_pair // 2, 2 * n + (s_id // tm) % 2 * tm + s_id % tm)
    block_start = jnp.arange(n_blocks, dtype=jnp.int32) * tm
    block_expert = jnp.minimum(jnp.sum((block_start[:, None] >= pend[None, :]).astype(jnp.int32), axis=1),
                               MOE_EXPERTS - 1)
    nact = (pend[-1] // tm).astype(jnp.int32).reshape(1)
    return block_expert, slot_tok, slot_dst, nact, 2 * n + 2 * tm


def kernel(x, l0_norm_pre_mix, l0_w_in, l0_rwkv_mu, l0_rwkv_w0, l0_rwkv_w2, l0_rwkv_a0, l0_rwkv_a2, l0_rwkv_g2, l0_rwkv_k_k, l0_rwkv_k_a, l0_rwkv_r_k, l0_rwkv_ln_w, l0_rwkv_ln_b, l0_gmlp_ln_w, l0_gmlp_ln_b, l0_gmlp_ws, l0_gmlp_bs, l0_w_out, l0_norm_post_mix, l0_norm_pre_ffn, l0_ffn_w_gate, l0_ffn_w_up, l0_ffn_w_down, l0_norm_post_ffn, l1_norm_pre_mix, l1_w_in, l1_s5_a_re, l1_s5_a_im, l1_s5_log_dt, l1_s5_b_re, l1_s5_b_im, l1_s5_c_re, l1_s5_c_im, l1_s5_d, l1_s5_glu_w, l1_s5_glu_b, l1_m2_conv_w, l1_m2_conv_b, l1_m2_dt_bias, l1_m2_a_log, l1_m2_d, l1_m2_norm_w, l1_w_out, l1_norm_post_mix, l1_norm_pre_ffn, l1_moe_router, l1_moe_w_gate, l1_moe_w_up, l1_moe_w_down, l1_norm_post_ffn):
    b, t, d = x.shape
    n = b * t
    x2 = x.reshape(n, d)
    row = lambda vec: vec.reshape(1, -1)

    aw = l0_rwkv_w0.shape[0]
    heads = aw // RWKV_HEAD
    lw_, la_, lg_ = l0_rwkv_w2.shape[0], l0_rwkv_a2.shape[0], l0_rwkv_g2.shape[0]
    a_in = 3 * aw + lw_ + la_ + lg_
    padc = lambda m, wdt: jnp.pad(m, ((0, 0), (0, LANES - wdt)))
    o = 3 * aw
    w_a = jnp.concatenate([l0_w_in[:, :o], padc(l0_w_in[:, o:o + lw_], lw_),
                           padc(l0_w_in[:, o + lw_:o + lw_ + la_], la_),
                           padc(l0_w_in[:, o + lw_ + la_:a_in], lg_)], axis=1).astype(BF16)
    w_b = l0_w_in[:, a_in:].astype(BF16)
    p_a, p_b = _norm_proj(x2, l0_norm_pre_mix, [w_a, w_b], [F32, BF16])
    padv = lambda vec, wdt: jnp.pad(vec, (0, LANES - wdt))
    mu = l0_rwkv_mu
    mu_p = jnp.concatenate([mu[:o], padv(mu[o:o + lw_], lw_), padv(mu[o + lw_:o + lw_ + la_], la_),
                            padv(mu[o + lw_ + la_:], lg_)])
    padr = lambda m: jnp.pad(m, ((0, LANES - m.shape[0]), (0, 0))).astype(BF16)
    hid = jnp.arange(LANES, dtype=jnp.int32) // RWKV_HEAD
    gsum = (hid[:, None] == hid[None, :]).astype(BF16)
    rwkv_prm = [row(mu_p), row(l0_rwkv_w0), padr(l0_rwkv_w2), row(l0_rwkv_a0), padr(l0_rwkv_a2),
                padr(l0_rwkv_g2), row(l0_rwkv_k_k), row(l0_rwkv_k_a), row(l0_rwkv_r_k),
                row(l0_rwkv_ln_w), row(l0_rwkv_ln_b), gsum]
    ya = _rwkv_mix(p_a.reshape(b, t, -1), rwkv_prm, heads)
    yb = _gmlp_mix(p_b.reshape(b, t, -1), l0_gmlp_ln_w, l0_gmlp_ln_b, l0_gmlp_ws, l0_gmlp_bs)
    wo = l0_w_out.astype(BF16)
    x2 = _mix_out0(x2, ya.reshape(n, -1), yb.reshape(n, -1), wo[:aw], wo[aw:], row(l0_norm_post_mix))
    x2 = _ffn(x2, row(l0_norm_pre_ffn), l0_ffn_w_gate.astype(BF16), l0_ffn_w_up.astype(BF16),
              l0_ffn_w_down.astype(BF16), row(l0_norm_post_ffn))

    cw = l1_s5_d.shape[0]
    dw = l1_m2_norm_w.shape[0]
    xw = l1_m2_conv_w.shape[1]
    nh = l1_m2_dt_bias.shape[0]
    w1 = l1_w_in
    w_parts = [w1[:, cw:cw + dw], w1[:, cw + dw:cw + dw + xw], padc(w1[:, cw + dw + xw:], nh)]
    z_d, xbc, dtp, u5 = _norm_proj(x2, l1_norm_pre_mix, [w.astype(BF16) for w in w_parts],
                                    [BF16, BF16, F32], [w1[:, :cw].T.astype(BF16)], [BF16], seq=t)
    s5_w = _s5_weights(l1_s5_a_re, l1_s5_a_im, l1_s5_log_dt, l1_s5_b_re, l1_s5_b_im, l1_s5_c_re,
                       l1_s5_c_im, l1_s5_d, S5_CHUNK)
    y5 = _s5_core(u5, s5_w)
    yd = _ssd_mix(z_d.reshape(b, t, dw), xbc.reshape(b, t, xw), dtp.reshape(b, t, LANES),
                  l1_m2_conv_w, l1_m2_conv_b, l1_m2_dt_bias, l1_m2_a_log, l1_m2_d, l1_m2_norm_w)
    wo1 = l1_w_out.astype(BF16)
    x1, h, idx, gates = _mix_out1(x2, y5, yd.reshape(n, dw), l1_s5_glu_w.T.astype(BF16),
                                  l1_s5_glu_b.reshape(cw, 1), wo1[:cw], wo1[cw:], row(l1_norm_post_mix),
                                  row(l1_norm_pre_ffn), l1_moe_router)
    block_expert, slot_tok, slot_dst, nact, n_rows = _moe_plan(idx, n)
    ys = _moe_experts(h, block_expert, slot_tok, slot_dst, nact, n_rows, l1_moe_w_gate, l1_moe_w_up,
                      l1_moe_w_down)
    out = _moe_combine(x1, ys, gates, row(l1_norm_post_ffn))
    return out.reshape(b, t, d)
```

```python
import functools

import jax
import jax.numpy as jnp
from jax import lax
from jax.experimental import pallas as pl
from jax.experimental.pallas import tpu as pltpu

F32 = jnp.float32
BF16 = jnp.bfloat16

EPS = 1e-6
RWKV_GN_EPS = 64e-5
RWKV_HEAD = 64
RWKV_CHUNK = 64
GMLP_CHUNK = 128
GMLP_GROUPS = 4
S5_GROUP_CH = 16
S5_STATE = 64
S5_CHUNK = 128
SSD_HEAD = 64
SSD_HEADS = 8
SSD_GROUPS = 2
SSD_STATE = 128
SSD_CONV = 4
SSD_CHUNK = 128
MOE_EXPERTS = 8
MOE_ROWS = 512
MXU_TILE = 256
LANES = 128
SUBLANES = 8
VMEM_LIMIT = 56 * 1024 * 1024


def _cparams(sem):
    return pltpu.CompilerParams(dimension_semantics=sem, vmem_limit_bytes=VMEM_LIMIT)


def _bdot(a, b):
    return jnp.dot(a.astype(BF16), b.astype(BF16), preferred_element_type=F32)


def _bdot_nt(a, b):
    return lax.dot_general(a.astype(BF16), b.astype(BF16), (((1,), (1,)), ((), ())),
                           preferred_element_type=F32)


def _bdot_tn(a, b):
    return lax.dot_general(a.astype(BF16), b.astype(BF16), (((0,), (0,)), ((), ())),
                           preferred_element_type=F32)


def _split3(x):
    h = x.astype(BF16)
    r1 = x - h.astype(F32)
    m = r1.astype(BF16)
    l = (r1 - m.astype(F32)).astype(BF16)
    return h, m, l


def _dot_x_exact(x, e):
    h, m, l = _split3(x)
    e = e.astype(BF16)
    return (jnp.dot(h, e, preferred_element_type=F32) + jnp.dot(m, e, preferred_element_type=F32)
            + jnp.dot(l, e, preferred_element_type=F32))


def _dot_exact_x(e, x):
    h, m, l = _split3(x)
    e = e.astype(BF16)
    return (jnp.dot(e, h, preferred_element_type=F32) + jnp.dot(e, m, preferred_element_type=F32)
            + jnp.dot(e, l, preferred_element_type=F32))


def _rms(x, g):
    return x * lax.rsqrt(jnp.mean(x * x, axis=-1, keepdims=True) + EPS) * g


def _sigmoid(x):
    return 1.0 / (1.0 + jnp.exp(-x))


def _silu(x):
    return x * _sigmoid(x)


def _softplus(x):
    return jnp.maximum(x, 0.0) + jnp.log(1.0 + jnp.exp(-jnp.abs(x)))


def _gelu_tanh(x):
    return 0.5 * x * (1.0 + jnp.tanh(0.7978845608028654 * (x + 0.044715 * x * x * x)))


def _iota2(shape, dim):
    return lax.broadcasted_iota(jnp.int32, shape, dim)


def _norm_proj_kernel(n_out, n_t, x_ref, g_ref, *refs):
    w_refs = refs[:n_out]
    wt_refs = refs[n_out:n_out + n_t]
    o_refs = refs[n_out + n_t:2 * n_out + n_t]
    ot_refs = refs[2 * n_out + n_t:]
    xn = _rms(x_ref[...], g_ref[...]).astype(BF16)
    for w_ref, o_ref in zip(w_refs, o_refs):
        o_ref[...] = jnp.dot(xn, w_ref[...], preferred_element_type=F32).astype(o_ref.dtype)
    for wt_ref, ot_ref in zip(wt_refs, ot_refs):
        yt = lax.dot_general(wt_ref[...], xn, (((1,), (1,)), ((), ())),
                             preferred_element_type=F32).astype(ot_ref.dtype)
        ng, nchunk, _, ch, ln = ot_ref.shape
        for cl in range(nchunk):
            ot_ref[:, cl, 0, :, :] = yt[:, cl * ln:(cl + 1) * ln].reshape(ng, ch, ln)


def _norm_proj(x2, g, ws, dtypes, wts=(), tdtypes=(), seq=None, ch=S5_GROUP_CH, ln=S5_CHUNK, tm=512):
    n, d = x2.shape
    per_seq = (seq or tm) // tm
    in_specs = [pl.BlockSpec((tm, d), lambda i: (i, 0)), pl.BlockSpec((1, d), lambda i: (0, 0))]
    in_specs += [pl.BlockSpec(w.shape, lambda i: (0, 0)) for w in list(ws) + list(wts)]
    out_specs = [pl.BlockSpec((tm, w.shape[1]), lambda i: (i, 0)) for w in ws]
    out_specs += [pl.BlockSpec((wt.shape[0] // ch, tm // ln, 1, ch, ln),
                               lambda i: (0, i % per_seq, i // per_seq, 0, 0)) for wt in wts]
    out_shape = [jax.ShapeDtypeStruct((n, w.shape[1]), dt) for w, dt in zip(ws, dtypes)]
    out_shape += [jax.ShapeDtypeStruct((wt.shape[0] // ch, seq // ln, n // seq, ch, ln), dt)
                  for wt, dt in zip(wts, tdtypes)]
    return pl.pallas_call(
        functools.partial(_norm_proj_kernel, len(ws), len(wts)),
        grid=(n // tm,), in_specs=in_specs, out_specs=out_specs, out_shape=out_shape,
        compiler_params=_cparams(("parallel",)), name="norm_proj",
    )(x2, g.reshape(1, d), *ws, *wts)


PRE_NAMES = ("v", "g", "bonus", "rt", "kt", "bt", "at", "bh", "kh")


def _rwkv_kernel(heads, nb, p_ref, pp_ref, mu_ref, w0_ref, w2_ref, a0_ref, a2_ref, g2_ref, kk_ref,
                 ka_ref, rk_ref, lnw_ref, lnb_ref, gs_ref, o_ref, z_ref, pre_ref, wl_ref):
    c = pl.program_id(1)
    ln = RWKV_CHUNK
    hd = RWKV_HEAD
    aw = heads * hd

    @pl.when(c == 0)
    def _():
        z_ref[...] = jnp.zeros_like(z_ref)
        pre_ref[...] = jnp.zeros_like(pre_ref)
        wl_ref[...] = jnp.zeros_like(wl_ref)

    tril_f = jnp.where(_iota2((ln, ln), 0) >= _iota2((ln, ln), 1), 1.0, 0.0)
    rows = _iota2((ln, 1), 0)
    gs_tile = gs_ref[...]

    def gs(x):
        nt = x.shape[1] // LANES
        stacked = jnp.concatenate([x[:, j * LANES:(j + 1) * LANES] for j in range(nt)], axis=0)
        red = _dot_x_exact(stacked, gs_tile)
        return jnp.concatenate([red[j * ln:(j + 1) * ln] for j in range(nt)], axis=1)

    pre_idx = {nm: idx for idx, nm in enumerate(PRE_NAMES)}
    pending = []

    def prep_steps():
        for bi in range(nb):
            p = p_ref[bi]
            prev = jnp.where(c == 0, 0.0, pp_ref[bi][SUBLANES - 1:SUBLANES, :])
            ps = jnp.where(rows == 0, prev, pltpu.roll(p, 1, axis=0))
            pm = p + (ps - p) * mu_ref[...]
            r = pm[:, 0:aw]
            k = pm[:, aw:2 * aw]
            v = pm[:, 2 * aw:3 * aw]
            xw = pm[:, 3 * aw:3 * aw + LANES]
            xa = pm[:, 3 * aw + LANES:3 * aw + 2 * LANES]
            xg = pm[:, 3 * aw + 2 * LANES:3 * aw + 3 * LANES]
            yield
            w = w0_ref[...] + _bdot(jnp.tanh(xw), w2_ref[...])
            a = _sigmoid(a0_ref[...] + _bdot(xa, a2_ref[...]))
            g = _bdot(_sigmoid(xg), g2_ref[...])
            yield
            w = -_softplus(-w) - 0.5
            lw = -jnp.exp(w)
            kk = k * kk_ref[...]
            kk_ss = gs(kk * kk)
            yield
            cs = _dot_exact_x(tril_f, lw)
            kk = kk / jnp.maximum(jnp.sqrt(kk_ss), 1e-12)
            kmod = k * (1.0 + (a - 1.0) * ka_ref[...])
            yield
            bonus = gs(r * kmod * rk_ref[...])
            bvec = kk * a
            cs_last = cs[ln - 1:ln, :]
            encs = jnp.exp(-cs)
            yield
            dec_end = jnp.exp(cs_last - cs)
            nxt = dict(v=v, g=g, bonus=bonus, rt=r * jnp.exp(cs), kt=kmod * encs, bt=bvec * encs,
                       at=-kk * jnp.exp(cs - lw), bh=bvec * dec_end, kh=kmod * dec_end)
            pending.append((bi, nxt, jnp.broadcast_to(jnp.exp(cs_last), (SUBLANES, aw))))
            yield

    prep = prep_steps()
    tick = lambda: next(prep, None)

    lane = _iota2((ln, LANES), 1)
    lane_in = jnp.where(lane >= hd, lane - hd, lane)
    trow = _iota2((ln, LANES), 0)
    left = lane < hd
    tril_p = lane_in <= trow
    stril_p = lane_in < trow
    eye_p = lane_in == trow
    eye_pf = jnp.where(eye_p, 1.0, 0.0)

    def bd(x):
        xb = x.astype(BF16)
        zero = jnp.zeros_like(xb)
        return jnp.concatenate([jnp.where(left, xb, zero), jnp.where(left, zero, xb)], axis=0)

    def dot(a, b):
        return jnp.dot(a.astype(BF16), b, preferred_element_type=F32)

    npair = heads // 2
    pairs = [(bi, j) for bi in range(nb) for j in range(npair)]

    class _Tiles:
        def __init__(self, name):
            self.idx = pre_idx[name]

        def __getitem__(self, i):
            bi, j = pairs[i]
            return pre_ref[self.idx, bi, :, j * LANES:(j + 1) * LANES]

    at, rt, bt, kt, vv, bh, kh = (_Tiles(n) for n in ("at", "rt", "bt", "kt", "v", "bh", "kh"))
    wl = [wl_ref[bi, 0:1, j * LANES:(j + 1) * LANES] for bi, j in pairs]
    z_all = z_ref[...]
    zs = [z_all[bi, j] for bi, j in pairs]
    npr = range(len(pairs))
    lhs = [jnp.concatenate([at[i], rt[i]], axis=0).astype(BF16) for i in npr]
    abk = [lax.dot_general(lhs[i], jnp.concatenate([bd(bt[i]), bd(kt[i])], axis=0),
                           (((1,), (1,)), ((), ())), preferred_element_type=F32) for i in npr]
    ab = [abk[i][:, :LANES] for i in npr]
    ak = [abk[i][:, LANES:] for i in npr]
    tick()
    nmat = [jnp.where(stril_p, ab[i][:ln], 0.0) for i in npr]
    tinv = [eye_pf + nmat[i] for i in npr]
    npow = [dot(nmat[i], bd(nmat[i])) for i in npr]
    tick()
    for step in range(5):
        bdn = [bd(npow[i]) for i in npr]
        if step < 4:
            both = [dot(jnp.concatenate([tinv[i], npow[i]], axis=0), bdn[i]) for i in npr]
            tinv = [tinv[i] + both[i][:ln] for i in npr]
            npow = [both[i][ln:] for i in npr]
        else:
            tinv = [tinv[i] + dot(tinv[i], bdn[i]) for i in npr]
        tick()
    bdv = [bd(vv[i]) for i in npr]
    bdz = [bd(zs[i]) for i in npr]
    xmat = [dot(jnp.concatenate([jnp.where(stril_p, ak[i][:ln], 0.0), at[i]], axis=1),
                jnp.concatenate([bdv[i], bdz[i]], axis=0)) for i in npr]
    tick()
    u = [dot(tinv[i], bd(xmat[i])) for i in npr]
    tick()
    ys_p = [dot(jnp.concatenate([rt[i], jnp.where(tril_p, ab[i][ln:], 0.0),
                                 jnp.where(tril_p, ak[i][ln:], 0.0)], axis=1),
                jnp.concatenate([bdz[i], bd(u[i]), bdv[i]], axis=0)) for i in npr]
    tick()
    cross = [_bdot_tn(jnp.concatenate([bh[i], kh[i]], axis=0), jnp.concatenate([u[i], vv[i]], axis=0))
             for i in npr]
    tick()
    z_new = []
    for i in npr:
        dg = jnp.where(eye_p, wl[i], 0.0)
        wl_i = jnp.sum(jnp.where(left, dg, 0.0), axis=1, keepdims=True)
        wl_j = jnp.sum(jnp.where(left, 0.0, dg), axis=1, keepdims=True)
        z_new.append(jnp.where(left, wl_i, wl_j) * zs[i] + jnp.where(left, cross[i][:ln], cross[i][ln:]))
    z_ref[...] = jnp.stack(z_new, axis=0).reshape(z_ref.shape)

    tick()
    inv = 1.0 / hd
    for bi in range(nb):
        y = jnp.concatenate(ys_p[bi * npair:(bi + 1) * npair], axis=1)
        mean = gs(y) * inv
        d = y - mean
        var = gs(d * d) * inv
        yn = d * lax.rsqrt(var + RWKV_GN_EPS) * lnw_ref[...] + lnb_ref[...]
        o_ref[bi] = ((yn + pre_ref[pre_idx["bonus"], bi] * pre_ref[pre_idx["v"], bi])
                     * pre_ref[pre_idx["g"], bi]).astype(o_ref.dtype)
        tick()
    for _ in prep:
        pass
    for bi, nxt, wl_next in pending:
        for nm, idx in pre_idx.items():
            pre_ref[idx, bi] = nxt[nm]
        wl_ref[bi] = wl_next


def _rwkv_mix(p_a, prm, heads, nb=4):
    b, t, cin = p_a.shape
    aw = heads * RWKV_HEAD
    ln = RWKV_CHUNK
    nc = t // ln
    sub = ln // SUBLANES
    full = lambda arr: pl.BlockSpec(arr.shape, lambda i, j: (0,) * arr.ndim)
    in_specs = [pl.BlockSpec((nb, ln, cin), lambda i, j: (i, jnp.minimum(j, nc - 1), 0)),
                pl.BlockSpec((nb, SUBLANES, cin),
                             lambda i, j: (i, jnp.maximum(jnp.minimum(j, nc - 1) * sub - 1, 0), 0))]
    in_specs += [full(x) for x in prm]
    return pl.pallas_call(
        functools.partial(_rwkv_kernel, heads, nb),
        grid=(b // nb, nc + 1), in_specs=in_specs,
        out_specs=pl.BlockSpec((nb, ln, aw), lambda i, j: (i, jnp.maximum(j - 1, 0), 0)),
        out_shape=jax.ShapeDtypeStruct((b, t, aw), BF16),
        scratch_shapes=[pltpu.VMEM((nb, heads // 2, RWKV_HEAD, 2 * RWKV_HEAD), F32),
                        pltpu.VMEM((len(PRE_NAMES), nb, ln, aw), F32),
                        pltpu.VMEM((nb, SUBLANES, aw), F32)],
        compiler_params=_cparams(("parallel", "arbitrary")), name="rwkv7",
    )(p_a, p_a, *prm)


def _gmlp_kernel(p_ref, lnw_ref, lnb_ref, ws_ref, bs_ref, o_ref):
    ln = GMLP_CHUNK
    bw = p_ref.shape[2] // 2
    gd = bw // GMLP_GROUPS
    tril = _iota2((ln, ln), 0) >= _iota2((ln, ln), 1)
    ws_c = [jnp.where(tril, ws_ref[gi], 0.0).astype(BF16) for gi in range(GMLP_GROUPS)]
    for ci in range(p_ref.shape[1] // ln):
        x = _gelu_tanh(p_ref[0, ci * ln:(ci + 1) * ln, :].astype(F32))
        for gi in range(GMLP_GROUPS):
            u = x[:, gi * gd:(gi + 1) * gd]
            v = x[:, bw + gi * gd:bw + (gi + 1) * gd]
            mean = jnp.mean(v, axis=-1, keepdims=True)
            d = v - mean
            var = jnp.mean(d * d, axis=-1, keepdims=True)
            vn = d * lax.rsqrt(var + EPS) * lnw_ref[gi:gi + 1, :] + lnb_ref[gi:gi + 1, :]
            s = jnp.dot(ws_c[gi], vn.astype(BF16), preferred_element_type=F32) + bs_ref[gi]
            o_ref[0, ci * ln:(ci + 1) * ln, gi * gd:(gi + 1) * gd] = (u * s).astype(o_ref.dtype)


def _gmlp_mix(p_b, ln_w, ln_b, ws, bs, rows=512):
    b, t, cin = p_b.shape
    bw = cin // 2
    gd = bw // GMLP_GROUPS
    bs_b = jnp.broadcast_to(bs[:, :, None], (GMLP_GROUPS, GMLP_CHUNK, gd))
    full = lambda arr: pl.BlockSpec(arr.shape, lambda i, j: (0,) * arr.ndim)
    return pl.pallas_call(
        _gmlp_kernel, grid=(b, t // rows),
        in_specs=[pl.BlockSpec((1, rows, cin), lambda i, j: (i, j, 0)),
                  full(ln_w), full(ln_b), full(ws), full(bs_b)],
        out_specs=pl.BlockSpec((1, rows, bw), lambda i, j: (i, j, 0)),
        out_shape=jax.ShapeDtypeStruct((b, t, bw), BF16),
        compiler_params=_cparams(("parallel", "parallel")), name="gmlp",
    )(p_b, ln_w, ln_b, ws, bs_b)


def _mix_out0_kernel(x_ref, ya_ref, yb_ref, wa_ref, wb_ref, g_ref, o_ref):
    y = _bdot(ya_ref[...], wa_ref[...]) + _bdot(yb_ref[...], wb_ref[...])
    o_ref[...] = x_ref[...] + _rms(y, g_ref[...])


def _mix_out0(x2, ya, yb, wa, wb, g, tm=512):
    n, d = x2.shape
    row = lambda arr: pl.BlockSpec((tm, arr.shape[1]), lambda i: (i, 0))
    full = lambda arr: pl.BlockSpec(arr.shape, lambda i: (0,) * arr.ndim)
    return pl.pallas_call(
        _mix_out0_kernel, grid=(n // tm,),
        in_specs=[row(x2), row(ya), row(yb), full(wa), full(wb), full(g)],
        out_specs=row(x2), out_shape=jax.ShapeDtypeStruct((n, d), F32),
        compiler_params=_cparams(("parallel",)), name="mix_out0",
    )(x2, ya, yb, wa, wb, g)


def _ffn_kernel(fc, x_ref, gpre_ref, wg_ref, wu_ref, wd_ref, gpost_ref, o_ref):
    h = _rms(x_ref[...], gpre_ref[...]).astype(BF16)
    acc = None
    for c in range(wg_ref.shape[1] // fc):
        cols = slice(c * fc, (c + 1) * fc)
        gate = jnp.dot(h, wg_ref[:, cols], preferred_element_type=F32)
        up = jnp.dot(h, wu_ref[:, cols], preferred_element_type=F32)
        part = jnp.dot((_silu(gate) * up).astype(BF16), wd_ref[cols, :], preferred_element_type=F32)
        acc = part if acc is None else acc + part
    o_ref[...] = x_ref[...] + _rms(acc, gpost_ref[...])


def _ffn(x2, gpre, wg, wu, wd, gpost, tm=512, fc=MXU_TILE):
    n, d = x2.shape
    full = lambda arr: pl.BlockSpec(arr.shape, lambda i: (0,) * arr.ndim)
    return pl.pallas_call(
        functools.partial(_ffn_kernel, fc), grid=(n // tm,),
        in_specs=[pl.BlockSpec((tm, d), lambda i: (i, 0)), full(gpre), full(wg), full(wu), full(wd),
                  full(gpost)],
        out_specs=pl.BlockSpec((tm, d), lambda i: (i, 0)),
        out_shape=jax.ShapeDtypeStruct((n, d), F32),
        compiler_params=_cparams(("parallel",)), name="ffn",
    )(x2, gpre, wg, wu, wd, gpost)


def _s5_kernel(nc, nb, u_ref, tap_ref, wsr_ref, wsi_ref, wcr_ref, wci_ref, alr_ref, ali_ref, d_ref, o_ref,
               toep_ref):
    ch, ln = u_ref.shape[3], u_ref.shape[4]
    u = u_ref[0].reshape(nc * nb, ch * ln)
    taps = tap_ref[0]
    width = ch * ln
    keep = (_iota2((ln, width), 1) & (ln - 1)) >= _iota2((ln, width), 0)
    for cin in range(ch):
        src = jnp.broadcast_to(taps[cin:cin + 1, :], (ln, width))
        blk = jnp.where(keep, pltpu.roll(src, 0, 1, stride=1, stride_axis=0), 0.0)
        toep_ref[cin * ln:(cin + 1) * ln, :] = blk.astype(BF16)
    y = jnp.dot(u, toep_ref[...], preferred_element_type=F32)
    xer = jnp.dot(u, wsr_ref[0], preferred_element_type=F32)
    xei = jnp.dot(u, wsi_ref[0], preferred_element_type=F32)
    alr = alr_ref[0]
    ali = ali_ref[0]
    cr = jnp.zeros((nb, xer.shape[1]), F32)
    ci = jnp.zeros((nb, xer.shape[1]), F32)
    prs, pis = [], []
    for c in range(nc):
        prs.append(cr)
        pis.append(ci)
        er = xer[c * nb:(c + 1) * nb]
        ei = xei[c * nb:(c + 1) * nb]
        cr, ci = alr * cr - ali * ci + er, alr * ci + ali * cr + ei
    pr = jnp.concatenate(prs, axis=0)
    pi = jnp.concatenate(pis, axis=0)
    y = y + _bdot(pr, wcr_ref[0]) + _bdot(pi, wci_ref[0])
    o_ref[0] = (y + d_ref[0] * u.astype(F32)).astype(o_ref.dtype).reshape(nc, nb, ch, ln)


def _s5_weights(a_re, a_im, log_dt, b_re, b_im, c_re, c_im, d_skip, ln):
    g, st = a_re.shape
    ch = b_re.shape[2]
    dt = jnp.exp(log_dt)[:, None]
    lr, li = a_re, a_im
    tau = jnp.arange(ln + 1, dtype=F32)[:, None, None]
    mag = jnp.exp(lr[None] * dt[None] * tau)
    pw_r = mag * jnp.cos(li[None] * dt[None] * tau)
    pw_i = mag * jnp.sin(li[None] * dt[None] * tau)
    ab_r, ab_i = pw_r[1], pw_i[1]
    nr, ni = ab_r - 1.0, ab_i
    den = lr * lr + li * li
    fr, fi = (nr * lr + ni * li) / den, (ni * lr - nr * li) / den
    bb_r = fr[..., None] * b_re - fi[..., None] * b_im
    bb_i = fr[..., None] * b_im + fi[..., None] * b_re
    cp_r = c_re[None] * pw_r[:ln, :, None, :] - c_im[None] * pw_i[:ln, :, None, :]
    cp_i = c_re[None] * pw_i[:ln, :, None, :] + c_im[None] * pw_r[:ln, :, None, :]
    hp = lax.Precision.HIGHEST
    taps = (jnp.einsum('tgcp,gpd->gdct', cp_r, bb_r, precision=hp)
            - jnp.einsum('tgcp,gpd->gdct', cp_i, bb_i, precision=hp))
    taps = taps.reshape(g, ch, ch * ln)
    rev_r, rev_i = pw_r[:ln][::-1], pw_i[:ln][::-1]
    ws_r = rev_r[..., None] * bb_r[None] - rev_i[..., None] * bb_i[None]
    ws_i = rev_r[..., None] * bb_i[None] + rev_i[..., None] * bb_r[None]
    ws_r = ws_r.transpose(1, 3, 0, 2).reshape(g, ch * ln, st)
    ws_i = ws_i.transpose(1, 3, 0, 2).reshape(g, ch * ln, st)
    q_r, q_i = pw_r[1:ln + 1], pw_i[1:ln + 1]
    wc_r = c_re[None] * q_r[:, :, None, :] - c_im[None] * q_i[:, :, None, :]
    wc_i = -(c_re[None] * q_i[:, :, None, :] + c_im[None] * q_r[:, :, None, :])
    wc_r = wc_r.transpose(1, 3, 2, 0).reshape(g, st, ch * ln)
    wc_i = wc_i.transpose(1, 3, 2, 0).reshape(g, st, ch * ln)
    al_r = pw_r[ln].reshape(g, 1, st)
    al_i = pw_i[ln].reshape(g, 1, st)
    d_t = jnp.repeat(d_skip.reshape(g, ch), ln, axis=1).reshape(g, 1, ch * ln)
    return (taps, ws_r.astype(BF16), ws_i.astype(BF16), wc_r.astype(BF16),
            wc_i.astype(BF16), al_r, al_i, d_t)


def _s5_core(u5, weights):
    g, nc, b, ch, ln = u5.shape
    blk = pl.BlockSpec((1, nc, b, ch, ln), lambda i: (i, 0, 0, 0, 0))
    per_g = lambda arr: pl.BlockSpec((1,) + arr.shape[1:], lambda i: (i, 0, 0))
    return pl.pallas_call(
        functools.partial(_s5_kernel, nc, b), grid=(g,),
        in_specs=[blk] + [per_g(w) for w in weights],
        out_specs=blk, out_shape=jax.ShapeDtypeStruct(u5.shape, BF16),
        scratch_shapes=[pltpu.VMEM((ln * ch, ln * ch), BF16)],
        compiler_params=_cparams(("parallel",)), name="s5",
    )(u5, *weights)


def _ssd_kernel(z_ref, xbc_ref, xp_ref, dt_ref, cw_ref, cb_ref, dtb_ref, alog_ref, dsk_ref, nw_ref,
                o_ref, s_ref):
    c = pl.program_id(1)
    ln = SSD_CHUNK
    hd = SSD_HEAD
    dw = SSD_HEADS * hd
    gn = SSD_STATE

    @pl.when(c == 0)
    def _():
        s_ref[...] = jnp.zeros_like(s_ref)

    xbc = xbc_ref[0]
    halo = xp_ref.shape[1]
    prev = jnp.where(c == 0, jnp.zeros_like(xp_ref[0]), xp_ref[0])
    full = jnp.concatenate([prev, xbc], axis=0).astype(BF16)
    conv = cb_ref[...] + cw_ref[SSD_CONV - 1:SSD_CONV, :] * xbc.astype(F32)
    for j in range(SSD_CONV - 1):
        lag = SSD_CONV - 1 - j
        pick = _iota2((ln, halo + ln), 1) == _iota2((ln, halo + ln), 0) + (halo - lag)
        shifted = jnp.dot(jnp.where(pick, 1.0, 0.0).astype(BF16), full, preferred_element_type=F32)
        conv = conv + cw_ref[j:j + 1, :] * shifted
    act = _silu(conv)
    xh = act[:, :dw]
    dt = _softplus(dt_ref[0] + dtb_ref[...])
    adt = -jnp.exp(alog_ref[...]) * dt
    tril = _iota2((ln, ln), 0) >= _iota2((ln, ln), 1)
    acs = _dot_exact_x(jnp.where(tril, 1.0, 0.0), adt)
    acs_t = acs.T
    tot = acs[ln - 1:ln, :]
    hg = SSD_HEADS // SSD_GROUPS
    s_all = s_ref[...]
    y_heads, s_heads = [], []
    for gi in range(SSD_GROUPS):
        bm = act[:, dw + gi * gn:dw + (gi + 1) * gn]
        cm = act[:, dw + SSD_GROUPS * gn + gi * gn:dw + SSD_GROUPS * gn + (gi + 1) * gn]
        cb = _bdot_nt(cm, bm)
        for hh in range(hg):
            h = gi * hg + hh
            sl = slice(h * hd, (h + 1) * hd)
            col = acs[:, h:h + 1]
            rowv = acs_t[h:h + 1, :]
            lmat = jnp.exp(jnp.where(tril, col - rowv, -jnp.inf))
            xh_h = xh[:, sl]
            xdt = xh_h * dt[:, h:h + 1]
            tot_h = tot[:, h:h + 1]
            st = s_all[h]
            y_h = _bdot(cb * lmat, xdt) + jnp.exp(col) * _bdot(cm, st)
            s_heads.append(jnp.exp(tot_h) * st + _bdot_tn(bm * jnp.exp(tot_h - col), xdt))
            y_heads.append(y_h + dsk_ref[:, sl] * xh_h)
    s_ref[...] = jnp.stack(s_heads, axis=0)
    y = jnp.concatenate(y_heads, axis=1) * _silu(z_ref[0].astype(F32))
    gw = dw // SSD_GROUPS
    for gi in range(SSD_GROUPS):
        yg = y[:, gi * gw:(gi + 1) * gw]
        yg = yg * lax.rsqrt(jnp.mean(yg * yg, axis=-1, keepdims=True) + EPS)
        o_ref[0, :, gi * gw:(gi + 1) * gw] = (yg * nw_ref[:, gi * gw:(gi + 1) * gw]).astype(o_ref.dtype)


def _ssd_mix(z, xbc, dtp, conv_w, conv_b, dt_bias, a_log, d_skip, norm_w):
    b, t, dw = z.shape
    ln = SSD_CHUNK
    xw = xbc.shape[2]
    pad = lambda vec: jnp.pad(vec, (0, LANES - vec.shape[0])).reshape(1, LANES)
    dsk = jnp.repeat(d_skip, SSD_HEAD).reshape(1, dw)
    prm = [conv_w, conv_b.reshape(1, xw), pad(dt_bias), pad(a_log), dsk, norm_w.reshape(1, dw)]
    full = lambda arr: pl.BlockSpec(arr.shape, lambda i, j: (0,) * arr.ndim)
    blk = lambda w: pl.BlockSpec((1, ln, w), lambda i, j: (i, j, 0))
    halo = 2 * SUBLANES
    return pl.pallas_call(
        _ssd_kernel, grid=(b, t // ln),
        in_specs=[blk(dw), blk(xw),
                  pl.BlockSpec((1, halo, xw), lambda i, j: (i, jnp.maximum(j * (ln // halo) - 1, 0), 0)),
                  blk(LANES)] + [full(x) for x in prm],
        out_specs=blk(dw), out_shape=jax.ShapeDtypeStruct((b, t, dw), BF16),
        scratch_shapes=[pltpu.VMEM((SSD_HEADS, SSD_STATE, SSD_HEAD), F32)],
        compiler_params=_cparams(("parallel", "arbitrary")), name="ssd",
    )(z, xbc, xbc, dtp, *prm)


def _mix_out1_kernel(x_ref, yc_ref, yd_ref, gw_ref, gb_ref, wc_ref, wd_ref, gpost_ref, gpre_ref,
                     wrh_ref, wrl_ref, x1_ref, h_ref, idx_ref, gate_ref):
    ng, nchunk, _, ch, ln = yc_ref.shape
    yc = jnp.concatenate([yc_ref[:, cl, 0, :, :].reshape(ng * ch, ln) for cl in range(nchunk)], axis=1)
    yc = _gelu_tanh(yc.astype(F32))
    yc = yc * _sigmoid(jnp.dot(gw_ref[...], yc.astype(BF16), preferred_element_type=F32) + gb_ref[...])
    y = _bdot_tn(yc, wc_ref[...]) + _bdot(yd_ref[...], wd_ref[...])
    x1 = x_ref[...] + _rms(y, gpost_ref[...])
    x1_ref[...] = x1
    h = _rms(x1, gpre_ref[...])
    h_ref[...] = h
    hh = h.astype(BF16)
    hl = (h - hh.astype(F32)).astype(BF16)
    wrh = wrh_ref[...]
    logits = (jnp.dot(hh, wrh, preferred_element_type=F32) + jnp.dot(hl, wrh, preferred_element_type=F32)
              + jnp.dot(hh, wrl_ref[...], preferred_element_type=F32))
    lane = _iota2(logits.shape, 1)
    lane_f = lane.astype(F32)
    logits = jnp.where(lane < MOE_EXPERTS, logits, -jnp.inf)
    m1 = jnp.max(logits, axis=-1, keepdims=True)
    i1 = jnp.min(jnp.where(logits == m1, lane_f, float(LANES)), axis=-1, keepdims=True)
    rest = jnp.where(lane_f == i1, -jnp.inf, logits)
    m2 = jnp.max(rest, axis=-1, keepdims=True)
    i2 = jnp.min(jnp.where(rest == m2, lane_f, float(LANES)), axis=-1, keepdims=True)
    e2 = jnp.exp(m2 - m1)
    g1 = 1.0 / (1.0 + e2)
    g2 = e2 / (1.0 + e2)
    idx_ref[...] = jnp.where(lane == 0, i1, jnp.where(lane == 1, i2, 0.0)).astype(jnp.int32)
    gate_ref[...] = jnp.where(lane == 0, g1, jnp.where(lane == 1, g2, 0.0))


def _mix_out1(x2, y5, yd, glu_wt, glu_b, wc, wd, gpost, gpre, wr, tm=512):
    n, d = x2.shape
    ng, nc, _, ch, ln = y5.shape
    per_seq = nc * ln // tm
    wr_p = jnp.pad(wr, ((0, 0), (0, LANES - wr.shape[1])))
    wrh = wr_p.astype(BF16)
    wrl = (wr_p - wrh.astype(F32)).astype(BF16)
    row = lambda w: pl.BlockSpec((tm, w), lambda i: (i, 0))
    full = lambda arr: pl.BlockSpec(arr.shape, lambda i: (0,) * arr.ndim)
    prm = [glu_wt, glu_b, wc, wd, gpost, gpre, wrh, wrl]
    return pl.pallas_call(
        _mix_out1_kernel, grid=(n // tm,),
        in_specs=[row(d), pl.BlockSpec((ng, tm // ln, 1, ch, ln),
                                       lambda i: (0, i % per_seq, i // per_seq, 0, 0)),
                  row(yd.shape[1])] + [full(p) for p in prm],
        out_specs=[row(d), row(d), row(LANES), row(LANES)],
        out_shape=[jax.ShapeDtypeStruct((n, d), F32), jax.ShapeDtypeStruct((n, d), F32),
                   jax.ShapeDtypeStruct((n, LANES), jnp.int32), jax.ShapeDtypeStruct((n, LANES), F32)],
        compiler_params=_cparams(("parallel",)), name="mix_out1",
    )(x2, y5, yd, *prm)


GATHER_UNROLL = 8


def _gather_rows(n_rows, make_copy):
    def body(j, carry):
        for q in range(GATHER_UNROLL):
            make_copy(j * GATHER_UNROLL + q).start(priority=q % 2)
        return carry

    lax.fori_loop(0, n_rows // GATHER_UNROLL, body, 0)


def _moe_kernel(be_ref, tok_ref, dst_ref, nact_ref, h_hbm, wg_hbm, wu_hbm, wd_hbm, y_hbm, buf_ref,
                xb_ref, wg_ref, wu_ref, wd_ref, sa_ref, sb_ref, gsem, ssem, wsem):
    i = pl.program_id(0)
    n_blocks = pl.num_programs(0)
    tm = buf_ref.shape[1]
    nact = nact_ref[0]
    active = i < nact
    slot = lax.rem(i, 2)
    other = 1 - slot
    xs = lambda sl: buf_ref.at[sl]
    yb = lambda sl: buf_ref.at[2 + sl]

    def gather_copy(block, sl, r):
        tok = tok_ref[block * tm + r]
        return pltpu.make_async_copy(h_hbm.at[pl.ds(tok, 1)], buf_ref.at[sl, pl.ds(r, 1)], gsem.at[sl])

    def scatter_copy(block, sl, r):
        dst = dst_ref[block * tm + r]
        return pltpu.make_async_copy(buf_ref.at[2 + sl, pl.ds(r, 1)], y_hbm.at[pl.ds(dst, 1)], ssem.at[sl])

    def wait_rows(sem_slot_ref, buf):
        pltpu.make_async_copy(h_hbm.at[pl.ds(0, tm)], buf, sem_slot_ref).wait()

    @pl.when(i == 0)
    def _():
        _gather_rows(tm, functools.partial(gather_copy, 0, 0))
        buf_ref[3] = jnp.zeros(buf_ref.shape[1:], F32)
        n_real = y_hbm.shape[0] - 2 * tm
        for half in range(2):
            init = pltpu.make_async_copy(yb(1), y_hbm.at[pl.ds(n_real + half * tm, tm)], ssem.at[0])
            init.start()
            init.wait()

    @pl.when(i <= nact)
    def _():
        wait_rows(gsem.at[slot], xs(slot))

    @pl.when(jnp.logical_and(i >= 1, i <= nact))
    def _():
        wait_rows(ssem.at[slot], yb(slot))

    @pl.when(i == nact)
    def _():
        _gather_rows(tm, functools.partial(scatter_copy, i - 1, other))
        wait_rows(ssem.at[other], yb(other))

    @pl.when(active)
    def _():
        xb_ref[...] = buf_ref[slot].astype(BF16)
        buf_ref[2 + slot] = jnp.zeros(buf_ref.shape[1:], F32)

    e = be_ref[i]
    fresh = jnp.logical_or(i == 0, e != be_ref[jnp.maximum(i - 1, 0)])
    nch = wg_ref.shape[1] // MXU_TILE

    pieces = []
    for c in range(nch):
        cols = slice(c * MXU_TILE, (c + 1) * MXU_TILE)
        pieces.append((wg_hbm.at[e, :, cols], sa_ref, 2 * c, wg_ref, (slice(None), cols)))
        pieces.append((wu_hbm.at[e, :, cols], sa_ref, 2 * c + 1, wu_ref, (slice(None), cols)))
        pieces.append((wd_hbm.at[e, cols, :], sb_ref, c, wd_ref, (cols, slice(None))))
    ahead = 2 * 3

    def piece_copy(p):
        src, stage, k, _, _ = pieces[p]
        sl = k % stage.shape[0]
        return pltpu.make_async_copy(src, stage.at[sl], wsem.at[(0 if stage is sa_ref else sa_ref.shape[0]) + sl])

    def compute(load_weights):
        x = xb_ref[...]
        prev = jnp.where(i == 0, n_blocks - 1, i - 1)
        if load_weights:
            for p in range(ahead):
                piece_copy(p).start()
        for c in range(nch):
            if load_weights:
                for p in range(3 * c, 3 * c + 3):
                    _, stage, k, dst, where = pieces[p]
                    piece_copy(p).wait()
                    dst[where] = stage[k % stage.shape[0]].astype(BF16)
                    if p + ahead < len(pieces):
                        piece_copy(p + ahead).start()
            lo, hi = tm * c // nch, tm * (c + 1) // nch
            cuts = [lo, lo + (hi - lo) // 3, lo + 2 * (hi - lo) // 3, hi]

            def row_copies(part):
                for r in range(cuts[part], cuts[part + 1]):
                    gather_copy(i + 1, other, r).start(priority=1)
                    scatter_copy(prev, other, r).start(priority=1)

            cols = slice(c * MXU_TILE, (c + 1) * MXU_TILE)
            row_copies(0)
            gate = jnp.dot(x, wg_ref[:, cols], preferred_element_type=F32)
            buf_ref[4, 0:SUBLANES, 0:LANES] = gate[0:SUBLANES, 0:LANES]
            row_copies(1)
            up = jnp.dot(x, wu_ref[:, cols], preferred_element_type=F32)
            buf_ref[4, SUBLANES:2 * SUBLANES, 0:LANES] = up[0:SUBLANES, 0:LANES]
            row_copies(2)
            buf_ref[2 + slot] += jnp.dot((_silu(gate) * up).astype(BF16), wd_ref[cols, :],
                                         preferred_element_type=F32)

    @pl.when(jnp.logical_and(active, fresh))
    def _():
        compute(True)

    @pl.when(jnp.logical_and(active, jnp.logical_not(fresh)))
    def _():
        compute(False)


def _moe_experts(h, block_expert, slot_tok, slot_dst, nact, n_rows, wg, wu, wd):
    n, d = h.shape
    tm = MOE_ROWS
    n_blocks = slot_tok.shape[0] // tm
    ff = wg.shape[2]
    anywhere = pl.BlockSpec(memory_space=pl.ANY)
    grid_spec = pltpu.PrefetchScalarGridSpec(
        num_scalar_prefetch=4, grid=(n_blocks,),
        in_specs=[anywhere, anywhere, anywhere, anywhere],
        out_specs=pl.BlockSpec(memory_space=pl.ANY),
        scratch_shapes=[pltpu.VMEM((5, tm, d), F32), pltpu.VMEM((tm, d), BF16),
                        pltpu.VMEM((d, ff), BF16), pltpu.VMEM((d, ff), BF16), pltpu.VMEM((ff, d), BF16),
                        pltpu.VMEM((4, d, MXU_TILE), F32), pltpu.VMEM((2, MXU_TILE, d), F32),
                        pltpu.SemaphoreType.DMA((2,)), pltpu.SemaphoreType.DMA((2,)),
                        pltpu.SemaphoreType.DMA((6,))])
    return pl.pallas_call(
        _moe_kernel, grid_spec=grid_spec,
        out_shape=jax.ShapeDtypeStruct((n_rows, d), F32),
        compiler_params=pltpu.CompilerParams(dimension_semantics=("arbitrary",),
                                             vmem_limit_bytes=VMEM_LIMIT, disable_bounds_checks=True),
        name="moe_experts",
    )(block_expert, slot_tok, slot_dst, nact, h, wg, wu, wd)


def _combine_kernel(x_ref, y0_ref, y1_ref, gate_ref, gpost_ref, o_ref):
    gates = gate_ref[...]
    y = gates[:, 0:1] * y0_ref[...] + gates[:, 1:2] * y1_ref[...]
    o_ref[...] = x_ref[...] + _rms(y, gpost_ref[...])


def _moe_combine(x1, y, gates, gpost, tm=512):
    n, d = x1.shape
    nt = n // tm
    return pl.pallas_call(
        _combine_kernel, grid=(nt,),
        in_specs=[pl.BlockSpec((tm, d), lambda i: (i, 0)), pl.BlockSpec((tm, d), lambda i: (i, 0)),
                  pl.BlockSpec((tm, d), lambda i: (nt + i, 0)),
                  pl.BlockSpec((tm, LANES), lambda i: (i, 0)), pl.BlockSpec((1, d), lambda i: (0, 0))],
        out_specs=pl.BlockSpec((tm, d), lambda i: (i, 0)),
        out_shape=jax.ShapeDtypeStruct((n, d), F32),
        compiler_params=_cparams(("parallel",)), name="moe_combine",
    )(x1, y, y, gates, gpost)


def _moe_plan(idx, n):
    tm = MOE_ROWS
    flat_e = idx[:, :2].reshape(-1)
    onehot = (flat_e[:, None] == jnp.arange(MOE_EXPERTS, dtype=jnp.int32)[None, :]).astype(jnp.int32)
    csum = jnp.cumsum(onehot, axis=0)
    counts = csum[-1]
    rank = jnp.sum((csum - onehot) * onehot, axis=1)
    padded = (counts + tm - 1) // tm * tm
    pend = jnp.cumsum(padded)
    pstart = pend - padded
    dest = (jnp.sum(onehot * pstart[None, :], axis=1) + rank).astype(jnp.int32)
    n_blocks = (2 * n) // tm + MOE_EXPERTS + 1
    n_slots = n_blocks * tm
    slot_pair = jnp.full((n_slots,), -1, jnp.int32).at[dest].set(
        jnp.arange(2 * n, dtype=jnp.int32), unique_indices=True)
    real = slot_pair >= 0
    slot_tok = jnp.where(real, slot_pair // 2, 0)
    s_id = jnp.arange(n_slots, dtype=jnp.int32)
    slot_dst = jnp.where(real, slot_pair % 2 * n + slot_pair // 2, 2 * n + (s_id // tm) % 2 * tm + s_id % tm)
    block_start = jnp.arange(n_blocks, dtype=jnp.int32) * tm
    block_expert = jnp.minimum(jnp.sum((block_start[:, None] >= pend[None, :]).astype(jnp.int32), axis=1),
                               MOE_EXPERTS - 1)
    nact = (pend[-1] // tm).astype(jnp.int32).reshape(1)
    return block_expert, slot_tok, slot_dst, nact, 2 * n + 2 * tm


def kernel(x, l0_norm_pre_mix, l0_w_in, l0_rwkv_mu, l0_rwkv_w0, l0_rwkv_w2, l0_rwkv_a0, l0_rwkv_a2, l0_rwkv_g2, l0_rwkv_k_k, l0_rwkv_k_a, l0_rwkv_r_k, l0_rwkv_ln_w, l0_rwkv_ln_b, l0_gmlp_ln_w, l0_gmlp_ln_b, l0_gmlp_ws, l0_gmlp_bs, l0_w_out, l0_norm_post_mix, l0_norm_pre_ffn, l0_ffn_w_gate, l0_ffn_w_up, l0_ffn_w_down, l0_norm_post_ffn, l1_norm_pre_mix, l1_w_in, l1_s5_a_re, l1_s5_a_im, l1_s5_log_dt, l1_s5_b_re, l1_s5_b_im, l1_s5_c_re, l1_s5_c_im, l1_s5_d, l1_s5_glu_w, l1_s5_glu_b, l1_m2_conv_w, l1_m2_conv_b, l1_m2_dt_bias, l1_m2_a_log, l1_m2_d, l1_m2_norm_w, l1_w_out, l1_norm_post_mix, l1_norm_pre_ffn, l1_moe_router, l1_moe_w_gate, l1_moe_w_up, l1_moe_w_down, l1_norm_post_ffn):
    b, t, d = x.shape
    n = b * t
    x2 = x.reshape(n, d)
    row = lambda vec: vec.reshape(1, -1)

    aw = l0_rwkv_w0.shape[0]
    heads = aw // RWKV_HEAD
    lw_, la_, lg_ = l0_rwkv_w2.shape[0], l0_rwkv_a2.shape[0], l0_rwkv_g2.shape[0]
    a_in = 3 * aw + lw_ + la_ + lg_
    padc = lambda m, wdt: jnp.pad(m, ((0, 0), (0, LANES - wdt)))
    o = 3 * aw
    w_a = jnp.concatenate([l0_w_in[:, :o], padc(l0_w_in[:, o:o + lw_], lw_),
                           padc(l0_w_in[:, o + lw_:o + lw_ + la_], la_),
                           padc(l0_w_in[:, o + lw_ + la_:a_in], lg_)], axis=1).astype(BF16)
    w_b = l0_w_in[:, a_in:].astype(BF16)
    p_a, p_b = _norm_proj(x2, l0_norm_pre_mix, [w_a, w_b], [F32, BF16])
    padv = lambda vec, wdt: jnp.pad(vec, (0, LANES - wdt))
    mu = l0_rwkv_mu
    mu_p = jnp.concatenate([mu[:o], padv(mu[o:o + lw_], lw_), padv(mu[o + lw_:o + lw_ + la_], la_),
                            padv(mu[o + lw_ + la_:], lg_)])
    padr = lambda m: jnp.pad(m, ((0, LANES - m.shape[0]), (0, 0))).astype(BF16)
    hid = jnp.arange(LANES, dtype=jnp.int32) // RWKV_HEAD
    gsum = (hid[:, None] == hid[None, :]).astype(BF16)
    rwkv_prm = [row(mu_p), row(l0_rwkv_w0), padr(l0_rwkv_w2), row(l0_rwkv_a0), padr(l0_rwkv_a2),
                padr(l0_rwkv_g2), row(l0_rwkv_k_k), row(l0_rwkv_k_a), row(l0_rwkv_r_k),
                row(l0_rwkv_ln_w), row(l0_rwkv_ln_b), gsum]
    ya = _rwkv_mix(p_a.reshape(b, t, -1), rwkv_prm, heads)
    yb = _gmlp_mix(p_b.reshape(b, t, -1), l0_gmlp_ln_w, l0_gmlp_ln_b, l0_gmlp_ws, l0_gmlp_bs)
    wo = l0_w_out.astype(BF16)
    x2 = _mix_out0(x2, ya.reshape(n, -1), yb.reshape(n, -1), wo[:aw], wo[aw:], row(l0_norm_post_mix))
    x2 = _ffn(x2, row(l0_norm_pre_ffn), l0_ffn_w_gate.astype(BF16), l0_ffn_w_up.astype(BF16),
              l0_ffn_w_down.astype(BF16), row(l0_norm_post_ffn))

    cw = l1_s5_d.shape[0]
    dw = l1_m2_norm_w.shape[0]
    xw = l1_m2_conv_w.shape[1]
    nh = l1_m2_dt_bias.shape[0]
    w1 = l1_w_in
    w_parts = [w1[:, cw:cw + dw], w1[:, cw + dw:cw + dw + xw], padc(w1[:, cw + dw + xw:], nh)]
    z_d, xbc, dtp, u5 = _norm_proj(x2, l1_norm_pre_mix, [w.astype(BF16) for w in w_parts],
                                    [BF16, BF16, F32], [w1[:, :cw].T.astype(BF16)], [BF16], seq=t)
    s5_w = _s5_weights(l1_s5_a_re, l1_s5_a_im, l1_s5_log_dt, l1_s5_b_re, l1_s5_b_im, l1_s5_c_re,
                       l1_s5_c_im, l1_s5_d, S5_CHUNK)
    y5 = _s5_core(u5, s5_w)
    yd = _ssd_mix(z_d.reshape(b, t, dw), xbc.reshape(b, t, xw), dtp.reshape(b, t, LANES),
                  l1_m2_conv_w, l1_m2_conv_b, l1_m2_dt_bias, l1_m2_a_log, l1_m2_d, l1_m2_norm_w)
    wo1 = l1_w_out.astype(BF16)
    x1, h, idx, gates = _mix_out1(x2, y5, yd.reshape(n, dw), l1_s5_glu_w.T.astype(BF16),
                                  l1_s5_glu_b.reshape(cw, 1), wo1[:cw], wo1[cw:], row(l1_norm_post_mix),
                                  row(l1_norm_pre_ffn), l1_moe_router)
    block_expert, slot_tok, slot_dst, nact, n_rows = _moe_plan(idx, n)
    ys = _moe_experts(h, block_expert, slot_tok, slot_dst, nact, n_rows, l1_moe_w_gate, l1_moe_w_up,
                      l1_moe_w_down)
    out = _moe_combine(x1, ys, gates, row(l1_norm_post_ffn))
    return out.reshape(b, t, d)
```

```python
import functools

import jax
import jax.numpy as jnp
from jax import lax
from jax.experimental import pallas as pl
from jax.experimental.pallas import tpu as pltpu

F32 = jnp.float32
BF16 = jnp.bfloat16

EPS = 1e-6
RWKV_GN_EPS = 64e-5
RWKV_HEAD = 64
RWKV_CHUNK = 64
GMLP_CHUNK = 128
GMLP_GROUPS = 4
S5_GROUP_CH = 16
S5_STATE = 64
S5_CHUNK = 128
SSD_HEAD = 64
SSD_HEADS = 8
SSD_GROUPS = 2
SSD_STATE = 128
SSD_CONV = 4
SSD_CHUNK = 128
MOE_EXPERTS = 8
MOE_ROWS = 512
MXU_TILE = 256
LANES = 128
SUBLANES = 8
VMEM_LIMIT = 56 * 1024 * 1024


def _cparams(sem):
    return pltpu.CompilerParams(dimension_semantics=sem, vmem_limit_bytes=VMEM_LIMIT)


def _bdot(a, b):
    return jnp.dot(a.astype(BF16), b.astype(BF16), preferred_element_type=F32)


def _bdot_nt(a, b):
    return lax.dot_general(a.astype(BF16), b.astype(BF16), (((1,), (1,)), ((), ())),
                           preferred_element_type=F32)


def _bdot_tn(a, b):
    return lax.dot_general(a.astype(BF16), b.astype(BF16), (((0,), (0,)), ((), ())),
                           preferred_element_type=F32)


def _split3(x):
    h = x.astype(BF16)
    r1 = x - h.astype(F32)
    m = r1.astype(BF16)
    l = (r1 - m.astype(F32)).astype(BF16)
    return h, m, l


def _dot_x_exact(x, e):
    h, m, l = _split3(x)
    e = e.astype(BF16)
    return (jnp.dot(h, e, preferred_element_type=F32) + jnp.dot(m, e, preferred_element_type=F32)
            + jnp.dot(l, e, preferred_element_type=F32))


def _dot_exact_x(e, x):
    h, m, l = _split3(x)
    e = e.astype(BF16)
    return (jnp.dot(e, h, preferred_element_type=F32) + jnp.dot(e, m, preferred_element_type=F32)
            + jnp.dot(e, l, preferred_element_type=F32))


def _rms(x, g):
    return x * lax.rsqrt(jnp.mean(x * x, axis=-1, keepdims=True) + EPS) * g


def _sigmoid(x):
    return 1.0 / (1.0 + jnp.exp(-x))


def _silu(x):
    return x * _sigmoid(x)


def _softplus(x):
    return jnp.maximum(x, 0.0) + jnp.log(1.0 + jnp.exp(-jnp.abs(x)))


def _gelu_tanh(x):
    return 0.5 * x * (1.0 + jnp.tanh(0.7978845608028654 * (x + 0.044715 * x * x * x)))


def _iota2(shape, dim):
    return lax.broadcasted_iota(jnp.int32, shape, dim)


def _project(xn, w_refs, wt_refs, o_refs, ot_refs):
    for w_ref, o_ref in zip(w_refs, o_refs):
        o_ref[...] = jnp.dot(xn, w_ref[...], preferred_element_type=F32).astype(o_ref.dtype)
    for wt_ref, ot_ref in zip(wt_refs, ot_refs):
        yt = lax.dot_general(wt_ref[...], xn, (((1,), (1,)), ((), ())),
                             preferred_element_type=F32).astype(ot_ref.dtype)
        ng, nchunk, _, ch, ln = ot_ref.shape
        for cl in range(nchunk):
            ot_ref[:, cl, 0, :, :] = yt[:, cl * ln:(cl + 1) * ln].reshape(ng, ch, ln)


def _norm_proj_kernel(n_out, n_t, x_ref, g_ref, *refs):
    w_refs = refs[:n_out]
    wt_refs = refs[n_out:n_out + n_t]
    o_refs = refs[n_out + n_t:2 * n_out + n_t]
    ot_refs = refs[2 * n_out + n_t:]
    _project(_rms(x_ref[...], g_ref[...]).astype(BF16), w_refs, wt_refs, o_refs, ot_refs)


def _proj_specs(n, d, tm, ws, dtypes, wts, tdtypes, seq, ch, ln, once=False):
    per_seq = (seq or tm) // tm
    mode = dict(pipeline_mode=pl.Buffered(1)) if once else {}
    w_specs = [pl.BlockSpec(w.shape, lambda i: (0, 0), **mode) for w in list(ws) + list(wts)]
    out_specs = [pl.BlockSpec((tm, w.shape[1]), lambda i: (i, 0)) for w in ws]
    out_specs += [pl.BlockSpec((wt.shape[0] // ch, tm // ln, 1, ch, ln),
                               lambda i: (0, i % per_seq, i // per_seq, 0, 0)) for wt in wts]
    out_shape = [jax.ShapeDtypeStruct((n, w.shape[1]), dt) for w, dt in zip(ws, dtypes)]
    out_shape += [jax.ShapeDtypeStruct((wt.shape[0] // ch, seq // ln, n // seq, ch, ln), dt)
                  for wt, dt in zip(wts, tdtypes)]
    return w_specs, out_specs, out_shape


def _norm_proj(x2, g, ws, dtypes, wts=(), tdtypes=(), seq=None, ch=S5_GROUP_CH, ln=S5_CHUNK, tm=512):
    n, d = x2.shape
    w_specs, out_specs, out_shape = _proj_specs(n, d, tm, ws, dtypes, wts, tdtypes, seq, ch, ln)
    in_specs = [pl.BlockSpec((tm, d), lambda i: (i, 0)), pl.BlockSpec((1, d), lambda i: (0, 0))] + w_specs
    return pl.pallas_call(
        functools.partial(_norm_proj_kernel, len(ws), len(wts)),
        grid=(n // tm,), in_specs=in_specs, out_specs=out_specs, out_shape=out_shape,
        compiler_params=_cparams(("parallel",)), name="norm_proj",
    )(x2, g.reshape(1, d), *ws, *wts)


PRE_NAMES = ("v", "g", "bonus", "rt", "kt", "bt", "at", "bh", "kh")


def _rwkv_kernel(heads, nb, p_ref, pp_ref, mu_ref, w0_ref, w2_ref, a0_ref, a2_ref, g2_ref, kk_ref,
                 ka_ref, rk_ref, lnw_ref, lnb_ref, gs_ref, o_ref, z_ref, pre_ref, wl_ref):
    c = pl.program_id(1)
    ln = RWKV_CHUNK
    hd = RWKV_HEAD
    aw = heads * hd

    @pl.when(c == 0)
    def _():
        z_ref[...] = jnp.zeros_like(z_ref)
        pre_ref[...] = jnp.zeros_like(pre_ref)
        wl_ref[...] = jnp.zeros_like(wl_ref)

    tril_f = jnp.where(_iota2((ln, ln), 0) >= _iota2((ln, ln), 1), 1.0, 0.0)
    rows = _iota2((ln, 1), 0)
    gs_tile = gs_ref[...]

    def gs(x):
        nt = x.shape[1] // LANES
        stacked = jnp.concatenate([x[:, j * LANES:(j + 1) * LANES] for j in range(nt)], axis=0)
        red = _dot_x_exact(stacked, gs_tile)
        return jnp.concatenate([red[j * ln:(j + 1) * ln] for j in range(nt)], axis=1)

    pre_idx = {nm: idx for idx, nm in enumerate(PRE_NAMES)}
    pending = []

    def prep_steps():
        for bi in range(nb):
            p = p_ref[bi]
            prev = jnp.where(c == 0, 0.0, pp_ref[bi][SUBLANES - 1:SUBLANES, :])
            ps = jnp.where(rows == 0, prev, pltpu.roll(p, 1, axis=0))
            pm = p + (ps - p) * mu_ref[...]
            r = pm[:, 0:aw]
            k = pm[:, aw:2 * aw]
            v = pm[:, 2 * aw:3 * aw]
            xw = pm[:, 3 * aw:3 * aw + LANES]
            xa = pm[:, 3 * aw + LANES:3 * aw + 2 * LANES]
            xg = pm[:, 3 * aw + 2 * LANES:3 * aw + 3 * LANES]
            yield
            w = w0_ref[...] + _bdot(jnp.tanh(xw), w2_ref[...])
            a = _sigmoid(a0_ref[...] + _bdot(xa, a2_ref[...]))
            g = _bdot(_sigmoid(xg), g2_ref[...])
            yield
            w = -_softplus(-w) - 0.5
            lw = -jnp.exp(w)
            kk = k * kk_ref[...]
            kk_ss = gs(kk * kk)
            yield
            cs = _dot_exact_x(tril_f, lw)
            kk = kk / jnp.maximum(jnp.sqrt(kk_ss), 1e-12)
            kmod = k * (1.0 + (a - 1.0) * ka_ref[...])
            yield
            bonus = gs(r * kmod * rk_ref[...])
            bvec = kk * a
            cs_last = cs[ln - 1:ln, :]
            encs = jnp.exp(-cs)
            yield
            dec_end = jnp.exp(cs_last - cs)
            nxt = dict(v=v, g=g, bonus=bonus, rt=r * jnp.exp(cs), kt=kmod * encs, bt=bvec * encs,
                       at=-kk * jnp.exp(cs - lw), bh=bvec * dec_end, kh=kmod * dec_end)
            pending.append((bi, nxt, jnp.broadcast_to(jnp.exp(cs_last), (SUBLANES, aw))))
            yield

    prep = prep_steps()
    tick = lambda: next(prep, None)

    lane = _iota2((ln, LANES), 1)
    lane_in = jnp.where(lane >= hd, lane - hd, lane)
    trow = _iota2((ln, LANES), 0)
    left = lane < hd
    tril_p = lane_in <= trow
    stril_p = lane_in < trow
    eye_p = lane_in == trow
    eye_pf = jnp.where(eye_p, 1.0, 0.0)

    def bd(x):
        xb = x.astype(BF16)
        zero = jnp.zeros_like(xb)
        return jnp.concatenate([jnp.where(left, xb, zero), jnp.where(left, zero, xb)], axis=0)

    def dot(a, b):
        return jnp.dot(a.astype(BF16), b, preferred_element_type=F32)

    npair = heads // 2
    pairs = [(bi, j) for bi in range(nb) for j in range(npair)]

    class _Tiles:
        def __init__(self, name):
            self.idx = pre_idx[name]

        def __getitem__(self, i):
            bi, j = pairs[i]
            return pre_ref[self.idx, bi, :, j * LANES:(j + 1) * LANES]

    at, rt, bt, kt, vv, bh, kh = (_Tiles(n) for n in ("at", "rt", "bt", "kt", "v", "bh", "kh"))
    wl = [wl_ref[bi, 0:1, j * LANES:(j + 1) * LANES] for bi, j in pairs]
    z_all = z_ref[...]
    zs = [z_all[bi, j] for bi, j in pairs]
    npr = range(len(pairs))
    lhs = [jnp.concatenate([at[i], rt[i]], axis=0).astype(BF16) for i in npr]
    abk = [lax.dot_general(lhs[i], jnp.concatenate([bd(bt[i]), bd(kt[i])], axis=0),
                           (((1,), (1,)), ((), ())), preferred_element_type=F32) for i in npr]
    ab = [abk[i][:, :LANES] for i in npr]
    ak = [abk[i][:, LANES:] for i in npr]
    tick()
    nmat = [jnp.where(stril_p, ab[i][:ln], 0.0) for i in npr]
    tinv = [eye_pf + nmat[i] for i in npr]
    npow = [dot(nmat[i], bd(nmat[i])) for i in npr]
    tick()
    for step in range(5):
        bdn = [bd(npow[i]) for i in npr]
        if step < 4:
            both = [dot(jnp.concatenate([tinv[i], npow[i]], axis=0), bdn[i]) for i in npr]
            tinv = [tinv[i] + both[i][:ln] for i in npr]
            npow = [both[i][ln:] for i in npr]
        else:
            tinv = [tinv[i] + dot(tinv[i], bdn[i]) for i in npr]
        tick()
    bdv = [bd(vv[i]) for i in npr]
    bdz = [bd(zs[i]) for i in npr]
    xmat = [dot(jnp.concatenate([jnp.where(stril_p, ak[i][:ln], 0.0), at[i]], axis=1),
                jnp.concatenate([bdv[i], bdz[i]], axis=0)) for i in npr]
    tick()
    u = [dot(tinv[i], bd(xmat[i])) for i in npr]
    tick()
    ys_p = [dot(jnp.concatenate([rt[i], jnp.where(tril_p, ab[i][ln:], 0.0),
                                 jnp.where(tril_p, ak[i][ln:], 0.0)], axis=1),
                jnp.concatenate([bdz[i], bd(u[i]), bdv[i]], axis=0)) for i in npr]
    tick()
    cross = [_bdot_tn(jnp.concatenate([bh[i], kh[i]], axis=0), jnp.concatenate([u[i], vv[i]], axis=0))
             for i in npr]
    tick()
    z_new = []
    for i in npr:
        dg = jnp.where(eye_p, wl[i], 0.0)
        wl_i = jnp.sum(jnp.where(left, dg, 0.0), axis=1, keepdims=True)
        wl_j = jnp.sum(jnp.where(left, 0.0, dg), axis=1, keepdims=True)
        z_new.append(jnp.where(left, wl_i, wl_j) * zs[i] + jnp.where(left, cross[i][:ln], cross[i][ln:]))
    z_ref[...] = jnp.stack(z_new, axis=0).reshape(z_ref.shape)

    tick()
    inv = 1.0 / hd
    for bi in range(nb):
        y = jnp.concatenate(ys_p[bi * npair:(bi + 1) * npair], axis=1)
        mean = gs(y) * inv
        d = y - mean
        var = gs(d * d) * inv
        yn = d * lax.rsqrt(var + RWKV_GN_EPS) * lnw_ref[...] + lnb_ref[...]
        o_ref[bi] = ((yn + pre_ref[pre_idx["bonus"], bi] * pre_ref[pre_idx["v"], bi])
                     * pre_ref[pre_idx["g"], bi]).astype(o_ref.dtype)
        tick()
    for _ in prep:
        pass
    for bi, nxt, wl_next in pending:
        for nm, idx in pre_idx.items():
            pre_ref[idx, bi] = nxt[nm]
        wl_ref[bi] = wl_next


def _rwkv_mix(p_a, prm, heads, nb=4):
    b, t, cin = p_a.shape
    aw = heads * RWKV_HEAD
    ln = RWKV_CHUNK
    nc = t // ln
    sub = ln // SUBLANES
    full = lambda arr: pl.BlockSpec(arr.shape, lambda i, j: (0,) * arr.ndim)
    in_specs = [pl.BlockSpec((nb, ln, cin), lambda i, j: (i, jnp.minimum(j, nc - 1), 0)),
                pl.BlockSpec((nb, SUBLANES, cin),
                             lambda i, j: (i, jnp.maximum(jnp.minimum(j, nc - 1) * sub - 1, 0), 0))]
    in_specs += [full(x) for x in prm]
    return pl.pallas_call(
        functools.partial(_rwkv_kernel, heads, nb),
        grid=(b // nb, nc + 1), in_specs=in_specs,
        out_specs=pl.BlockSpec((nb, ln, aw), lambda i, j: (i, jnp.maximum(j - 1, 0), 0)),
        out_shape=jax.ShapeDtypeStruct((b, t, aw), BF16),
        scratch_shapes=[pltpu.VMEM((nb, heads // 2, RWKV_HEAD, 2 * RWKV_HEAD), F32),
                        pltpu.VMEM((len(PRE_NAMES), nb, ln, aw), F32),
                        pltpu.VMEM((nb, SUBLANES, aw), F32)],
        compiler_params=_cparams(("parallel", "arbitrary")), name="rwkv7",
    )(p_a, p_a, *prm)


def _gmlp_kernel(p_ref, lnw_ref, lnb_ref, ws_ref, bs_ref, o_ref):
    ln = GMLP_CHUNK
    bw = p_ref.shape[2] // 2
    gd = bw // GMLP_GROUPS
    tril = _iota2((ln, ln), 0) >= _iota2((ln, ln), 1)
    ws_c = [jnp.where(tril, ws_ref[gi], 0.0).astype(BF16) for gi in range(GMLP_GROUPS)]
    for ci in range(p_ref.shape[1] // ln):
        x = _gelu_tanh(p_ref[0, ci * ln:(ci + 1) * ln, :].astype(F32))
        for gi in range(GMLP_GROUPS):
            u = x[:, gi * gd:(gi + 1) * gd]
            v = x[:, bw + gi * gd:bw + (gi + 1) * gd]
            mean = jnp.mean(v, axis=-1, keepdims=True)
            d = v - mean
            var = jnp.mean(d * d, axis=-1, keepdims=True)
            vn = d * lax.rsqrt(var + EPS) * lnw_ref[gi:gi + 1, :] + lnb_ref[gi:gi + 1, :]
            s = jnp.dot(ws_c[gi], vn.astype(BF16), preferred_element_type=F32) + bs_ref[gi]
            o_ref[0, ci * ln:(ci + 1) * ln, gi * gd:(gi + 1) * gd] = (u * s).astype(o_ref.dtype)


def _gmlp_mix(p_b, ln_w, ln_b, ws, bs, rows=512):
    b, t, cin = p_b.shape
    bw = cin // 2
    gd = bw // GMLP_GROUPS
    bs_b = jnp.broadcast_to(bs[:, :, None], (GMLP_GROUPS, GMLP_CHUNK, gd))
    full = lambda arr: pl.BlockSpec(arr.shape, lambda i, j: (0,) * arr.ndim)
    return pl.pallas_call(
        _gmlp_kernel, grid=(b, t // rows),
        in_specs=[pl.BlockSpec((1, rows, cin), lambda i, j: (i, j, 0)),
                  full(ln_w), full(ln_b), full(ws), full(bs_b)],
        out_specs=pl.BlockSpec((1, rows, bw), lambda i, j: (i, j, 0)),
        out_shape=jax.ShapeDtypeStruct((b, t, bw), BF16),
        compiler_params=_cparams(("parallel", "parallel")), name="gmlp",
    )(p_b, ln_w, ln_b, ws, bs_b)


def _mix_ffn_kernel(fc, n_out, n_t, x_ref, ya_ref, yb_ref, wa_ref, wb_ref, gmix_ref, gpre_ref, wg_ref, wu_ref,
                    wd_ref, gpost_ref, gnext_ref, *refs):
    w_refs = refs[:n_out]
    wt_refs = refs[n_out:n_out + n_t]
    o_ref = refs[n_out + n_t]
    o_refs = refs[n_out + n_t + 1:2 * n_out + n_t + 1]
    ot_refs = refs[2 * n_out + n_t + 1:]
    y = _bdot(ya_ref[...], wa_ref[...]) + _bdot(yb_ref[...], wb_ref[...])
    x1 = x_ref[...] + _rms(y, gmix_ref[...])
    h = _rms(x1, gpre_ref[...]).astype(BF16)
    acc = None
    for c in range(wg_ref.shape[1] // fc):
        cols = slice(c * fc, (c + 1) * fc)
        gate = jnp.dot(h, wg_ref[:, cols], preferred_element_type=F32)
        up = jnp.dot(h, wu_ref[:, cols], preferred_element_type=F32)
        part = jnp.dot((_silu(gate) * up).astype(BF16), wd_ref[cols, :], preferred_element_type=F32)
        acc = part if acc is None else acc + part
    x2 = x1 + _rms(acc, gpost_ref[...])
    o_ref[...] = x2
    _project(_rms(x2, gnext_ref[...]).astype(BF16), w_refs, wt_refs, o_refs, ot_refs)


def _mix_ffn(x2, ya, yb, wa, wb, gmix, gpre, wg, wu, wd, gpost, gnext, ws, dtypes, wts, tdtypes, seq,
             ch=S5_GROUP_CH, ln=S5_CHUNK, tm=512, fc=MXU_TILE):
    n, d = x2.shape
    row = lambda arr: pl.BlockSpec((tm, arr.shape[1]), lambda i: (i, 0))
    once = lambda arr: pl.BlockSpec(arr.shape, lambda i: (0,) * arr.ndim, pipeline_mode=pl.Buffered(1))
    w_specs, out_specs, out_shape = _proj_specs(n, d, tm, ws, dtypes, wts, tdtypes, seq, ch, ln, once=True)
    return pl.pallas_call(
        functools.partial(_mix_ffn_kernel, fc, len(ws), len(wts)), grid=(n // tm,),
        in_specs=[row(x2), row(ya), row(yb), once(wa), once(wb), once(gmix), once(gpre), once(wg),
                  once(wu), once(wd), once(gpost), once(gnext)] + w_specs,
        out_specs=[row(x2)] + out_specs,
        out_shape=[jax.ShapeDtypeStruct((n, d), F32)] + out_shape,
        compiler_params=_cparams(("parallel",)), name="mix_ffn",
    )(x2, ya, yb, wa, wb, gmix, gpre, wg, wu, wd, gpost, gnext, *ws, *wts)


def _s5_kernel(nc, nb, u_ref, tap_ref, wsr_ref, wsi_ref, wcr_ref, wci_ref, alr_ref, ali_ref, d_ref, o_ref,
               toep_ref):
    ch, ln = u_ref.shape[3], u_ref.shape[4]
    u = u_ref[0].reshape(nc * nb, ch * ln)
    taps = tap_ref[0]
    width = ch * ln
    keep = (_iota2((ln, width), 1) & (ln - 1)) >= _iota2((ln, width), 0)
    for cin in range(ch):
        src = jnp.broadcast_to(taps[cin:cin + 1, :], (ln, width))
        blk = jnp.where(keep, pltpu.roll(src, 0, 1, stride=1, stride_axis=0), 0.0)
        toep_ref[cin * ln:(cin + 1) * ln, :] = blk.astype(BF16)
    y = jnp.dot(u, toep_ref[...], preferred_element_type=F32)
    xer = jnp.dot(u, wsr_ref[0], preferred_element_type=F32)
    xei = jnp.dot(u, wsi_ref[0], preferred_element_type=F32)
    alr = alr_ref[0]
    ali = ali_ref[0]
    cr = jnp.zeros((nb, xer.shape[1]), F32)
    ci = jnp.zeros((nb, xer.shape[1]), F32)
    prs, pis = [], []
    for c in range(nc):
        prs.append(cr)
        pis.append(ci)
        er = xer[c * nb:(c + 1) * nb]
        ei = xei[c * nb:(c + 1) * nb]
        cr, ci = alr * cr - ali * ci + er, alr * ci + ali * cr + ei
    pr = jnp.concatenate(prs, axis=0)
    pi = jnp.concatenate(pis, axis=0)
    y = y + _bdot(pr, wcr_ref[0]) + _bdot(pi, wci_ref[0])
    o_ref[0] = (y + d_ref[0] * u.astype(F32)).astype(o_ref.dtype).reshape(nc, nb, ch, ln)


def _s5_weights(a_re, a_im, log_dt, b_re, b_im, c_re, c_im, d_skip, ln):
    g, st = a_re.shape
    ch = b_re.shape[2]
    dt = jnp.exp(log_dt)[:, None]
    lr, li = a_re, a_im
    tau = jnp.arange(ln + 1, dtype=F32)[:, None, None]
    mag = jnp.exp(lr[None] * dt[None] * tau)
    pw_r = mag * jnp.cos(li[None] * dt[None] * tau)
    pw_i = mag * jnp.sin(li[None] * dt[None] * tau)
    ab_r, ab_i = pw_r[1], pw_i[1]
    nr, ni = ab_r - 1.0, ab_i
    den = lr * lr + li * li
    fr, fi = (nr * lr + ni * li) / den, (ni * lr - nr * li) / den
    bb_r = fr[..., None] * b_re - fi[..., None] * b_im
    bb_i = fr[..., None] * b_im + fi[..., None] * b_re
    cp_r = c_re[None] * pw_r[:ln, :, None, :] - c_im[None] * pw_i[:ln, :, None, :]
    cp_i = c_re[None] * pw_i[:ln, :, None, :] + c_im[None] * pw_r[:ln, :, None, :]
    hp = lax.Precision.HIGHEST
    taps = (jnp.einsum('tgcp,gpd->gdct', cp_r, bb_r, precision=hp)
            - jnp.einsum('tgcp,gpd->gdct', cp_i, bb_i, precision=hp))
    taps = taps.reshape(g, ch, ch * ln)
    rev_r, rev_i = pw_r[:ln][::-1], pw_i[:ln][::-1]
    ws_r = rev_r[..., None] * bb_r[None] - rev_i[..., None] * bb_i[None]
    ws_i = rev_r[..., None] * bb_i[None] + rev_i[..., None] * bb_r[None]
    ws_r = ws_r.transpose(1, 3, 0, 2).reshape(g, ch * ln, st)
    ws_i = ws_i.transpose(1, 3, 0, 2).reshape(g, ch * ln, st)
    q_r, q_i = pw_r[1:ln + 1], pw_i[1:ln + 1]
    wc_r = c_re[None] * q_r[:, :, None, :] - c_im[None] * q_i[:, :, None, :]
    wc_i = -(c_re[None] * q_i[:, :, None, :] + c_im[None] * q_r[:, :, None, :])
    wc_r = wc_r.transpose(1, 3, 2, 0).reshape(g, st, ch * ln)
    wc_i = wc_i.transpose(1, 3, 2, 0).reshape(g, st, ch * ln)
    al_r = pw_r[ln].reshape(g, 1, st)
    al_i = pw_i[ln].reshape(g, 1, st)
    d_t = jnp.repeat(d_skip.reshape(g, ch), ln, axis=1).reshape(g, 1, ch * ln)
    return (taps, ws_r.astype(BF16), ws_i.astype(BF16), wc_r.astype(BF16),
            wc_i.astype(BF16), al_r, al_i, d_t)


def _s5_core(u5, weights):
    g, nc, b, ch, ln = u5.shape
    blk = pl.BlockSpec((1, nc, b, ch, ln), lambda i: (i, 0, 0, 0, 0))
    per_g = lambda arr: pl.BlockSpec((1,) + arr.shape[1:], lambda i: (i, 0, 0))
    return pl.pallas_call(
        functools.partial(_s5_kernel, nc, b), grid=(g,),
        in_specs=[blk] + [per_g(w) for w in weights],
        out_specs=blk, out_shape=jax.ShapeDtypeStruct(u5.shape, BF16),
        scratch_shapes=[pltpu.VMEM((ln * ch, ln * ch), BF16)],
        compiler_params=_cparams(("parallel",)), name="s5",
    )(u5, *weights)


def _ssd_kernel(z_ref, xbc_ref, xp_ref, dt_ref, cw_ref, cb_ref, dtb_ref, alog_ref, dsk_ref, nw_ref,
                o_ref, s_ref):
    c = pl.program_id(1)
    ln = SSD_CHUNK
    hd = SSD_HEAD
    dw = SSD_HEADS * hd
    gn = SSD_STATE

    @pl.when(c == 0)
    def _():
        s_ref[...] = jnp.zeros_like(s_ref)

    xbc = xbc_ref[0]
    halo = xp_ref.shape[1]
    prev = jnp.where(c == 0, jnp.zeros_like(xp_ref[0]), xp_ref[0])
    full = jnp.concatenate([prev, xbc], axis=0).astype(BF16)
    conv = cb_ref[...] + cw_ref[SSD_CONV - 1:SSD_CONV, :] * xbc.astype(F32)
    for j in range(SSD_CONV - 1):
        lag = SSD_CONV - 1 - j
        pick = _iota2((ln, halo + ln), 1) == _iota2((ln, halo + ln), 0) + (halo - lag)
        shifted = jnp.dot(jnp.where(pick, 1.0, 0.0).astype(BF16), full, preferred_element_type=F32)
        conv = conv + cw_ref[j:j + 1, :] * shifted
    act = _silu(conv)
    xh = act[:, :dw]
    dt = _softplus(dt_ref[0] + dtb_ref[...])
    adt = -jnp.exp(alog_ref[...]) * dt
    tril = _iota2((ln, ln), 0) >= _iota2((ln, ln), 1)
    acs = _dot_exact_x(jnp.where(tril, 1.0, 0.0), adt)
    acs_t = acs.T
    tot = acs[ln - 1:ln, :]
    hg = SSD_HEADS // SSD_GROUPS
    s_all = s_ref[...]
    y_heads, s_heads = [], []
    for gi in range(SSD_GROUPS):
        bm = act[:, dw + gi * gn:dw + (gi + 1) * gn]
        cm = act[:, dw + SSD_GROUPS * gn + gi * gn:dw + SSD_GROUPS * gn + (gi + 1) * gn]
        cb = _bdot_nt(cm, bm)
        for hh in range(hg):
            h = gi * hg + hh
            sl = slice(h * hd, (h + 1) * hd)
            col = acs[:, h:h + 1]
            rowv = acs_t[h:h + 1, :]
            lmat = jnp.exp(jnp.where(tril, col - rowv, -jnp.inf))
            xh_h = xh[:, sl]
            xdt = xh_h * dt[:, h:h + 1]
            tot_h = tot[:, h:h + 1]
            st = s_all[h]
            y_h = _bdot(cb * lmat, xdt) + jnp.exp(col) * _bdot(cm, st)
            s_heads.append(jnp.exp(tot_h) * st + _bdot_tn(bm * jnp.exp(tot_h - col), xdt))
            y_heads.append(y_h + dsk_ref[:, sl] * xh_h)
    s_ref[...] = jnp.stack(s_heads, axis=0)
    y = jnp.concatenate(y_heads, axis=1) * _silu(z_ref[0].astype(F32))
    gw = dw // SSD_GROUPS
    for gi in range(SSD_GROUPS):
        yg = y[:, gi * gw:(gi + 1) * gw]
        yg = yg * lax.rsqrt(jnp.mean(yg * yg, axis=-1, keepdims=True) + EPS)
        o_ref[0, :, gi * gw:(gi + 1) * gw] = (yg * nw_ref[:, gi * gw:(gi + 1) * gw]).astype(o_ref.dtype)


def _ssd_mix(z, xbc, dtp, conv_w, conv_b, dt_bias, a_log, d_skip, norm_w):
    b, t, dw = z.shape
    ln = SSD_CHUNK
    xw = xbc.shape[2]
    pad = lambda vec: jnp.pad(vec, (0, LANES - vec.shape[0])).reshape(1, LANES)
    dsk = jnp.repeat(d_skip, SSD_HEAD).reshape(1, dw)
    prm = [conv_w, conv_b.reshape(1, xw), pad(dt_bias), pad(a_log), dsk, norm_w.reshape(1, dw)]
    full = lambda arr: pl.BlockSpec(arr.shape, lambda i, j: (0,) * arr.ndim)
    blk = lambda w: pl.BlockSpec((1, ln, w), lambda i, j: (i, j, 0))
    halo = 2 * SUBLANES
    return pl.pallas_call(
        _ssd_kernel, grid=(b, t // ln),
        in_specs=[blk(dw), blk(xw),
                  pl.BlockSpec((1, halo, xw), lambda i, j: (i, jnp.maximum(j * (ln // halo) - 1, 0), 0)),
                  blk(LANES)] + [full(x) for x in prm],
        out_specs=blk(dw), out_shape=jax.ShapeDtypeStruct((b, t, dw), BF16),
        scratch_shapes=[pltpu.VMEM((SSD_HEADS, SSD_STATE, SSD_HEAD), F32)],
        compiler_params=_cparams(("parallel", "arbitrary")), name="ssd",
    )(z, xbc, xbc, dtp, *prm)


def _mix_out1_kernel(x_ref, yc_ref, yd_ref, gw_ref, gb_ref, wc_ref, wd_ref, gpost_ref, gpre_ref,
                     wrh_ref, wrl_ref, x1_ref, h_ref, idx_ref, gate_ref):
    ng, nchunk, _, ch, ln = yc_ref.shape
    yc = jnp.concatenate([yc_ref[:, cl, 0, :, :].reshape(ng * ch, ln) for cl in range(nchunk)], axis=1)
    yc = _gelu_tanh(yc.astype(F32))
    yc = yc * _sigmoid(jnp.dot(gw_ref[...], yc.astype(BF16), preferred_element_type=F32) + gb_ref[...])
    y = _bdot_tn(yc, wc_ref[...]) + _bdot(yd_ref[...], wd_ref[...])
    x1 = x_ref[...] + _rms(y, gpost_ref[...])
    x1_ref[...] = x1
    h = _rms(x1, gpre_ref[...])
    h_ref[...] = h
    hh = h.astype(BF16)
    hl = (h - hh.astype(F32)).astype(BF16)
    wrh = wrh_ref[...]
    logits = (jnp.dot(hh, wrh, preferred_element_type=F32) + jnp.dot(hl, wrh, preferred_element_type=F32)
              + jnp.dot(hh, wrl_ref[...], preferred_element_type=F32))
    lane = _iota2(logits.shape, 1)
    lane_f = lane.astype(F32)
    logits = jnp.where(lane < MOE_EXPERTS, logits, -jnp.inf)
    m1 = jnp.max(logits, axis=-1, keepdims=True)
    i1 = jnp.min(jnp.where(logits == m1, lane_f, float(LANES)), axis=-1, keepdims=True)
    rest = jnp.where(lane_f == i1, -jnp.inf, logits)
    m2 = jnp.max(rest, axis=-1, keepdims=True)
    i2 = jnp.min(jnp.where(rest == m2, lane_f, float(LANES)), axis=-1, keepdims=True)
    e2 = jnp.exp(m2 - m1)
    g1 = 1.0 / (1.0 + e2)
    g2 = e2 / (1.0 + e2)
    idx_ref[...] = jnp.where(lane == 0, i1, jnp.where(lane == 1, i2, 0.0)).astype(jnp.int32)
    gate_ref[...] = jnp.where(lane == 0, g1, jnp.where(lane == 1, g2, 0.0))


def _mix_out1(x2, y5, yd, glu_wt, glu_b, wc, wd, gpost, gpre, wr, tm=512):
    n, d = x2.shape
    ng, nc, _, ch, ln = y5.shape
    per_seq = nc * ln // tm
    wr_p = jnp.pad(wr, ((0, 0), (0, LANES - wr.shape[1])))
    wrh = wr_p.astype(BF16)
    wrl = (wr_p - wrh.astype(F32)).astype(BF16)
    row = lambda w: pl.BlockSpec((tm, w), lambda i: (i, 0))
    full = lambda arr: pl.BlockSpec(arr.shape, lambda i: (0,) * arr.ndim)
    prm = [glu_wt, glu_b, wc, wd, gpost, gpre, wrh, wrl]
    return pl.pallas_call(
        _mix_out1_kernel, grid=(n // tm,),
        in_specs=[row(d), pl.BlockSpec((ng, tm // ln, 1, ch, ln),
                                       lambda i: (0, i % per_seq, i // per_seq, 0, 0)),
                  row(yd.shape[1])] + [full(p) for p in prm],
        out_specs=[row(d), row(d), row(LANES), row(LANES)],
        out_shape=[jax.ShapeDtypeStruct((n, d), F32), jax.ShapeDtypeStruct((n, d), F32),
                   jax.ShapeDtypeStruct((n, LANES), jnp.int32), jax.ShapeDtypeStruct((n, LANES), F32)],
        compiler_params=_cparams(("parallel",)), name="mix_out1",
    )(x2, y5, yd, *prm)


GATHER_UNROLL = 8


def _gather_rows(n_rows, make_copy):
    def body(j, carry):
        for q in range(GATHER_UNROLL):
            make_copy(j * GATHER_UNROLL + q).start(priority=q % 2)
        return carry

    lax.fori_loop(0, n_rows // GATHER_UNROLL, body, 0)


def _moe_kernel(be_ref, tok_ref, dst_ref, nact_ref, h_hbm, wg_hbm, wu_hbm, wd_hbm, y_hbm, buf_ref,
                xb_ref, wg_ref, wu_ref, wd_ref, sa_ref, sb_ref, gsem, ssem, wsem):
    i = pl.program_id(0)
    n_blocks = pl.num_programs(0)
    tm = buf_ref.shape[1]
    nact = nact_ref[0]
    active = i < nact
    slot = lax.rem(i, 2)
    other = 1 - slot
    xs = lambda sl: buf_ref.at[sl]
    yb = lambda sl: buf_ref.at[2 + sl]

    def gather_copy(block, sl, r):
        tok = tok_ref[block * tm + r]
        return pltpu.make_async_copy(h_hbm.at[pl.ds(tok, 1)], buf_ref.at[sl, pl.ds(r, 1)], gsem.at[sl])

    def scatter_copy(block, sl, r):
        dst = dst_ref[block * tm + r]
        return pltpu.make_async_copy(buf_ref.at[2 + sl, pl.ds(r, 1)], y_hbm.at[pl.ds(dst, 1)], ssem.at[sl])

    def wait_rows(sem_slot_ref, buf):
        pltpu.make_async_copy(h_hbm.at[pl.ds(0, tm)], buf, sem_slot_ref).wait()

    @pl.when(i == 0)
    def _():
        _gather_rows(tm, functools.partial(gather_copy, 0, 0))
        buf_ref[3] = jnp.zeros(buf_ref.shape[1:], F32)
        n_real = y_hbm.shape[0] - 2 * tm
        for half in range(2):
            init = pltpu.make_async_copy(yb(1), y_hbm.at[pl.ds(n_real + half * tm, tm)], ssem.at[0])
            init.start()
            init.wait()

    @pl.when(i <= nact)
    def _():
        wait_rows(gsem.at[slot], xs(slot))

    @pl.when(jnp.logical_and(i >= 1, i <= nact))
    def _():
        wait_rows(ssem.at[slot], yb(slot))

    @pl.when(i == nact)
    def _():
        _gather_rows(tm, functools.partial(scatter_copy, i - 1, other))
        wait_rows(ssem.at[other], yb(other))

    @pl.when(active)
    def _():
        xb_ref[...] = buf_ref[slot].astype(BF16)

    e = be_ref[i]
    fresh = jnp.logical_or(i == 0, e != be_ref[jnp.maximum(i - 1, 0)])
    nch = wg_ref.shape[1] // MXU_TILE

    pieces = []
    for c in range(nch):
        cols = slice(c * MXU_TILE, (c + 1) * MXU_TILE)
        pieces.append((wg_hbm.at[e, :, cols], sa_ref, 2 * c, wg_ref, (slice(None), cols)))
        pieces.append((wu_hbm.at[e, :, cols], sa_ref, 2 * c + 1, wu_ref, (slice(None), cols)))
        pieces.append((wd_hbm.at[e, cols, :], sb_ref, c, wd_ref, (cols, slice(None))))
    ahead = 2 * 3

    def piece_copy(p):
        src, stage, k, _, _ = pieces[p]
        sl = k % stage.shape[0]
        return pltpu.make_async_copy(src, stage.at[sl], wsem.at[(0 if stage is sa_ref else sa_ref.shape[0]) + sl])

    def compute(load_weights):
        x = xb_ref[...]
        prev = jnp.where(i == 0, n_blocks - 1, i - 1)
        if load_weights:
            for p in range(ahead):
                piece_copy(p).start()
        for c in range(nch):
            if load_weights:
                for p in range(3 * c, 3 * c + 3):
                    _, stage, k, dst, where = pieces[p]
                    piece_copy(p).wait()
                    dst[where] = stage[k % stage.shape[0]].astype(BF16)
                    if p + ahead < len(pieces):
                        piece_copy(p + ahead).start()
            lo, hi = tm * c // nch, tm * (c + 1) // nch
            cuts = [lo, lo + (hi - lo) // 3, lo + 2 * (hi - lo) // 3, hi]

            def row_copies(part):
                for r in range(cuts[part], cuts[part + 1]):
                    gather_copy(i + 1, other, r).start(priority=1)
                    scatter_copy(prev, other, r).start(priority=1)

            cols = slice(c * MXU_TILE, (c + 1) * MXU_TILE)
            row_copies(0)
            gate = jnp.dot(x, wg_ref[:, cols], preferred_element_type=F32)
            buf_ref[4, 0:SUBLANES, 0:LANES] = gate[0:SUBLANES, 0:LANES]
            row_copies(1)
            up = jnp.dot(x, wu_ref[:, cols], preferred_element_type=F32)
            buf_ref[4, SUBLANES:2 * SUBLANES, 0:LANES] = up[0:SUBLANES, 0:LANES]
            row_copies(2)
            part = jnp.dot((_silu(gate) * up).astype(BF16), wd_ref[cols, :], preferred_element_type=F32)
            if c == 0:
                buf_ref[2 + slot] = part
            else:
                buf_ref[2 + slot] += part

    @pl.when(jnp.logical_and(active, fresh))
    def _():
        compute(True)

    @pl.when(jnp.logical_and(active, jnp.logical_not(fresh)))
    def _():
        compute(False)


def _moe_experts(h, block_expert, slot_tok, slot_dst, nact, n_rows, wg, wu, wd):
    n, d = h.shape
    tm = MOE_ROWS
    n_blocks = slot_tok.shape[0] // tm
    ff = wg.shape[2]
    anywhere = pl.BlockSpec(memory_space=pl.ANY)
    grid_spec = pltpu.PrefetchScalarGridSpec(
        num_scalar_prefetch=4, grid=(n_blocks,),
        in_specs=[anywhere, anywhere, anywhere, anywhere],
        out_specs=pl.BlockSpec(memory_space=pl.ANY),
        scratch_shapes=[pltpu.VMEM((5, tm, d), F32), pltpu.VMEM((tm, d), BF16),
                        pltpu.VMEM((d, ff), BF16), pltpu.VMEM((d, ff), BF16), pltpu.VMEM((ff, d), BF16),
                        pltpu.VMEM((4, d, MXU_TILE), F32), pltpu.VMEM((2, MXU_TILE, d), F32),
                        pltpu.SemaphoreType.DMA((2,)), pltpu.SemaphoreType.DMA((2,)),
                        pltpu.SemaphoreType.DMA((6,))])
    return pl.pallas_call(
        _moe_kernel, grid_spec=grid_spec,
        out_shape=jax.ShapeDtypeStruct((n_rows, d), F32),
        compiler_params=pltpu.CompilerParams(dimension_semantics=("arbitrary",),
                                             vmem_limit_bytes=VMEM_LIMIT, disable_bounds_checks=True),
        name="moe_experts",
    )(block_expert, slot_tok, slot_dst, nact, h, wg, wu, wd)


def _combine_kernel(x_ref, y0_ref, y1_ref, gate_ref, gpost_ref, o_ref):
    gates = gate_ref[...]
    y = gates[:, 0:1] * y0_ref[...] + gates[:, 1:2] * y1_ref[...]
    o_ref[...] = x_ref[...] + _rms(y, gpost_ref[...])


def _moe_combine(x1, y, gates, gpost, tm=512):
    n, d = x1.shape
    nt = n // tm
    return pl.pallas_call(
        _combine_kernel, grid=(nt,),
        in_specs=[pl.BlockSpec((tm, d), lambda i: (i, 0)), pl.BlockSpec((tm, d), lambda i: (i, 0)),
                  pl.BlockSpec((tm, d), lambda i: (nt + i, 0)),
                  pl.BlockSpec((tm, LANES), lambda i: (i, 0)), pl.BlockSpec((1, d), lambda i: (0, 0))],
        out_specs=pl.BlockSpec((tm, d), lambda i: (i, 0)),
        out_shape=jax.ShapeDtypeStruct((n, d), F32),
        compiler_params=_cparams(("parallel",)), name="moe_combine",
    )(x1, y, y, gates, gpost)


def _moe_plan(idx, n):
    tm = MOE_ROWS
    flat_e = idx[:, :2].reshape(-1)
    onehot = (flat_e[:, None] == jnp.arange(MOE_EXPERTS, dtype=jnp.int32)[None, :]).astype(jnp.int32)
    csum = jnp.cumsum(onehot, axis=0)
    counts = csum[-1]
    rank = jnp.sum((csum - onehot) * onehot, axis=1)
    padded = (counts + tm - 1) // tm * tm
    pend = jnp.cumsum(padded)
    pstart = pend - padded
    dest = (jnp.sum(onehot * pstart[None, :], axis=1) + rank).astype(jnp.int32)
    n_blocks = (2 * n) // tm + MOE_EXPERTS + 1
    n_slots = n_blocks * tm
    slot_pair = jnp.full((n_slots,), -1, jnp.int32).at[dest].set(
        jnp.arange(2 * n, dtype=jnp.int32), unique_indices=True)
    real = slot_pair >= 0
    slot_tok = jnp.where(real, slot_pair // 2, 0)
    s_id = jnp.arange(n_slots, dtype=jnp.int32)
    slot_dst = jnp.where(real, slot_pair % 2 * n + slot_pair // 2, 2 * n + (s_id // tm) % 2 * tm + s_id % tm)
    block_start = jnp.arange(n_blocks, dtype=jnp.int32) * tm
    block_expert = jnp.minimum(jnp.sum((block_start[:, None] >= pend[None, :]).astype(jnp.int32), axis=1),
                               MOE_EXPERTS - 1)
    nact = (pend[-1] // tm).astype(jnp.int32).reshape(1)
    return block_expert, slot_tok, slot_dst, nact, 2 * n + 2 * tm


def kernel(x, l0_norm_pre_mix, l0_w_in, l0_rwkv_mu, l0_rwkv_w0, l0_rwkv_w2, l0_rwkv_a0, l0_rwkv_a2, l0_rwkv_g2, l0_rwkv_k_k, l0_rwkv_k_a, l0_rwkv_r_k, l0_rwkv_ln_w, l0_rwkv_ln_b, l0_gmlp_ln_w, l0_gmlp_ln_b, l0_gmlp_ws, l0_gmlp_bs, l0_w_out, l0_norm_post_mix, l0_norm_pre_ffn, l0_ffn_w_gate, l0_ffn_w_up, l0_ffn_w_down, l0_norm_post_ffn, l1_norm_pre_mix, l1_w_in, l1_s5_a_re, l1_s5_a_im, l1_s5_log_dt, l1_s5_b_re, l1_s5_b_im, l1_s5_c_re, l1_s5_c_im, l1_s5_d, l1_s5_glu_w, l1_s5_glu_b, l1_m2_conv_w, l1_m2_conv_b, l1_m2_dt_bias, l1_m2_a_log, l1_m2_d, l1_m2_norm_w, l1_w_out, l1_norm_post_mix, l1_norm_pre_ffn, l1_moe_router, l1_moe_w_gate, l1_moe_w_up, l1_moe_w_down, l1_norm_post_ffn):
    b, t, d = x.shape
    n = b * t
    x2 = x.reshape(n, d)
    row = lambda vec: vec.reshape(1, -1)

    aw = l0_rwkv_w0.shape[0]
    heads = aw // RWKV_HEAD
    lw_, la_, lg_ = l0_rwkv_w2.shape[0], l0_rwkv_a2.shape[0], l0_rwkv_g2.shape[0]
    a_in = 3 * aw + lw_ + la_ + lg_
    padc = lambda m, wdt: jnp.pad(m, ((0, 0), (0, LANES - wdt)))
    o = 3 * aw
    w_a = jnp.concatenate([l0_w_in[:, :o], padc(l0_w_in[:, o:o + lw_], lw_),
                           padc(l0_w_in[:, o + lw_:o + lw_ + la_], la_),
                           padc(l0_w_in[:, o + lw_ + la_:a_in], lg_)], axis=1).astype(BF16)
    w_b = l0_w_in[:, a_in:].astype(BF16)
    p_a, p_b = _norm_proj(x2, l0_norm_pre_mix, [w_a, w_b], [F32, BF16])
    padv = lambda vec, wdt: jnp.pad(vec, (0, LANES - wdt))
    mu = l0_rwkv_mu
    mu_p = jnp.concatenate([mu[:o], padv(mu[o:o + lw_], lw_), padv(mu[o + lw_:o + lw_ + la_], la_),
                            padv(mu[o + lw_ + la_:], lg_)])
    padr = lambda m: jnp.pad(m, ((0, LANES - m.shape[0]), (0, 0))).astype(BF16)
    hid = jnp.arange(LANES, dtype=jnp.int32) // RWKV_HEAD
    gsum = (hid[:, None] == hid[None, :]).astype(BF16)
    rwkv_prm = [row(mu_p), row(l0_rwkv_w0), padr(l0_rwkv_w2), row(l0_rwkv_a0), padr(l0_rwkv_a2),
                padr(l0_rwkv_g2), row(l0_rwkv_k_k), row(l0_rwkv_k_a), row(l0_rwkv_r_k),
                row(l0_rwkv_ln_w), row(l0_rwkv_ln_b), gsum]
    ya = _rwkv_mix(p_a.reshape(b, t, -1), rwkv_prm, heads)
    yb = _gmlp_mix(p_b.reshape(b, t, -1), l0_gmlp_ln_w, l0_gmlp_ln_b, l0_gmlp_ws, l0_gmlp_bs)
    wo = l0_w_out.astype(BF16)
    cw = l1_s5_d.shape[0]
    dw = l1_m2_norm_w.shape[0]
    xw = l1_m2_conv_w.shape[1]
    nh = l1_m2_dt_bias.shape[0]
    w1 = l1_w_in
    w_parts = [w1[:, cw:cw + dw], w1[:, cw + dw:cw + dw + xw], padc(w1[:, cw + dw + xw:], nh)]
    x2, z_d, xbc, dtp, u5 = _mix_ffn(
        x2, ya.reshape(n, -1), yb.reshape(n, -1), wo[:aw], wo[aw:], row(l0_norm_post_mix),
        row(l0_norm_pre_ffn), l0_ffn_w_gate.astype(BF16), l0_ffn_w_up.astype(BF16),
        l0_ffn_w_down.astype(BF16), row(l0_norm_post_ffn), row(l1_norm_pre_mix),
        [w.astype(BF16) for w in w_parts], [BF16, BF16, F32], [w1[:, :cw].T.astype(BF16)], [BF16], seq=t)

    s5_w = _s5_weights(l1_s5_a_re, l1_s5_a_im, l1_s5_log_dt, l1_s5_b_re, l1_s5_b_im, l1_s5_c_re,
                       l1_s5_c_im, l1_s5_d, S5_CHUNK)
    y5 = _s5_core(u5, s5_w)
    yd = _ssd_mix(z_d.reshape(b, t, dw), xbc.reshape(b, t, xw), dtp.reshape(b, t, LANES),
                  l1_m2_conv_w, l1_m2_conv_b, l1_m2_dt_bias, l1_m2_a_log, l1_m2_d, l1_m2_norm_w)
    wo1 = l1_w_out.astype(BF16)
    x1, h, idx, gates = _mix_out1(x2, y5, yd.reshape(n, dw), l1_s5_glu_w.T.astype(BF16),
                                  l1_s5_glu_b.reshape(cw, 1), wo1[:cw], wo1[cw:], row(l1_norm_post_mix),
                                  row(l1_norm_pre_ffn), l1_moe_router)
    block_expert, slot_tok, slot_dst, nact, n_rows = _moe_plan(idx, n)
    ys = _moe_experts(h, block_expert, slot_tok, slot_dst, nact, n_rows, l1_moe_w_gate, l1_moe_w_up,
                      l1_moe_w_down)
    out = _moe_combine(x1, ys, gates, row(l1_norm_post_ffn))
    return out.reshape(b, t, d)
```

```python
import functools

import jax
import jax.numpy as jnp
from jax import lax
from jax.experimental import pallas as pl
from jax.experimental.pallas import tpu as pltpu

F32 = jnp.float32
BF16 = jnp.bfloat16

EPS = 1e-6
RWKV_GN_EPS = 64e-5
RWKV_HEAD = 64
RWKV_CHUNK = 64
GMLP_CHUNK = 128
GMLP_GROUPS = 4
S5_GROUP_CH = 16
S5_STATE = 64
S5_CHUNK = 128
SSD_HEAD = 64
SSD_HEADS = 8
SSD_GROUPS = 2
SSD_STATE = 128
SSD_CONV = 4
SSD_CHUNK = 128
MOE_EXPERTS = 8
MOE_ROWS = 512
MXU_TILE = 256
LANES = 128
SUBLANES = 8
VMEM_LIMIT = 56 * 1024 * 1024


def _cparams(sem):
    return pltpu.CompilerParams(dimension_semantics=sem, vmem_limit_bytes=VMEM_LIMIT)


def _bdot(a, b):
    return jnp.dot(a.astype(BF16), b.astype(BF16), preferred_element_type=F32)


def _bdot_nt(a, b):
    return lax.dot_general(a.astype(BF16), b.astype(BF16), (((1,), (1,)), ((), ())),
                           preferred_element_type=F32)


def _bdot_tn(a, b):
    return lax.dot_general(a.astype(BF16), b.astype(BF16), (((0,), (0,)), ((), ())),
                           preferred_element_type=F32)


def _split3(x):
    h = x.astype(BF16)
    r1 = x - h.astype(F32)
    m = r1.astype(BF16)
    l = (r1 - m.astype(F32)).astype(BF16)
    return h, m, l


def _dot_x_exact(x, e):
    h, m, l = _split3(x)
    e = e.astype(BF16)
    return (jnp.dot(h, e, preferred_element_type=F32) + jnp.dot(m, e, preferred_element_type=F32)
            + jnp.dot(l, e, preferred_element_type=F32))


def _dot_exact_x(e, x):
    h, m, l = _split3(x)
    e = e.astype(BF16)
    return (jnp.dot(e, h, preferred_element_type=F32) + jnp.dot(e, m, preferred_element_type=F32)
            + jnp.dot(e, l, preferred_element_type=F32))


def _rms(x, g):
    return x * lax.rsqrt(jnp.mean(x * x, axis=-1, keepdims=True) + EPS) * g


def _sigmoid(x):
    return 1.0 / (1.0 + jnp.exp(-x))


def _silu(x):
    return x * _sigmoid(x)


def _softplus(x):
    return jnp.maximum(x, 0.0) + jnp.log(1.0 + jnp.exp(-jnp.abs(x)))


def _gelu_tanh(x):
    return 0.5 * x * (1.0 + jnp.tanh(0.7978845608028654 * (x + 0.044715 * x * x * x)))


def _iota2(shape, dim):
    return lax.broadcasted_iota(jnp.int32, shape, dim)


def _project(xn, w_refs, wt_refs, o_refs, ot_refs):
    for w_ref, o_ref in zip(w_refs, o_refs):
        o_ref[...] = jnp.dot(xn, w_ref[...], preferred_element_type=F32).astype(o_ref.dtype)
    for wt_ref, ot_ref in zip(wt_refs, ot_refs):
        yt = lax.dot_general(wt_ref[...], xn, (((1,), (1,)), ((), ())),
                             preferred_element_type=F32).astype(ot_ref.dtype)
        ng, nchunk, _, ch, ln = ot_ref.shape
        for cl in range(nchunk):
            ot_ref[:, cl, 0, :, :] = yt[:, cl * ln:(cl + 1) * ln].reshape(ng, ch, ln)


def _norm_proj_kernel(n_out, n_t, x_ref, g_ref, *refs):
    w_refs = refs[:n_out]
    wt_refs = refs[n_out:n_out + n_t]
    o_refs = refs[n_out + n_t:2 * n_out + n_t]
    ot_refs = refs[2 * n_out + n_t:]
    _project(_rms(x_ref[...], g_ref[...]).astype(BF16), w_refs, wt_refs, o_refs, ot_refs)


def _proj_specs(n, d, tm, ws, dtypes, wts, tdtypes, seq, ch, ln, once=False):
    per_seq = (seq or tm) // tm
    mode = dict(pipeline_mode=pl.Buffered(1)) if once else {}
    w_specs = [pl.BlockSpec(w.shape, lambda i: (0, 0), **mode) for w in list(ws) + list(wts)]
    out_specs = [pl.BlockSpec((tm, w.shape[1]), lambda i: (i, 0)) for w in ws]
    out_specs += [pl.BlockSpec((wt.shape[0] // ch, tm // ln, 1, ch, ln),
                               lambda i: (0, i % per_seq, i // per_seq, 0, 0)) for wt in wts]
    out_shape = [jax.ShapeDtypeStruct((n, w.shape[1]), dt) for w, dt in zip(ws, dtypes)]
    out_shape += [jax.ShapeDtypeStruct((wt.shape[0] // ch, seq // ln, n // seq, ch, ln), dt)
                  for wt, dt in zip(wts, tdtypes)]
    return w_specs, out_specs, out_shape


def _norm_proj(x2, g, ws, dtypes, wts=(), tdtypes=(), seq=None, ch=S5_GROUP_CH, ln=S5_CHUNK, tm=512):
    n, d = x2.shape
    w_specs, out_specs, out_shape = _proj_specs(n, d, tm, ws, dtypes, wts, tdtypes, seq, ch, ln)
    in_specs = [pl.BlockSpec((tm, d), lambda i: (i, 0)), pl.BlockSpec((1, d), lambda i: (0, 0))] + w_specs
    return pl.pallas_call(
        functools.partial(_norm_proj_kernel, len(ws), len(wts)),
        grid=(n // tm,), in_specs=in_specs, out_specs=out_specs, out_shape=out_shape,
        compiler_params=_cparams(("parallel",)), name="norm_proj",
    )(x2, g.reshape(1, d), *ws, *wts)


PRE_NAMES = ("v", "g", "bonus", "rt", "kt", "bt", "at", "bh", "kh")


def _rwkv_kernel(heads, nb, p_ref, pp_ref, mu_ref, w0_ref, w2_ref, a0_ref, a2_ref, g2_ref, kk_ref,
                 ka_ref, rk_ref, lnw_ref, lnb_ref, gs_ref, o_ref, z_ref, pre_ref, wl_ref):
    c = pl.program_id(1)
    ln = RWKV_CHUNK
    hd = RWKV_HEAD
    aw = heads * hd

    @pl.when(c == 0)
    def _():
        z_ref[...] = jnp.zeros_like(z_ref)
        pre_ref[...] = jnp.zeros_like(pre_ref)
        wl_ref[...] = jnp.zeros_like(wl_ref)

    tril_f = jnp.where(_iota2((ln, ln), 0) >= _iota2((ln, ln), 1), 1.0, 0.0)
    rows = _iota2((ln, 1), 0)
    gs_tile = gs_ref[...]

    def gs(x):
        nt = x.shape[1] // LANES
        stacked = jnp.concatenate([x[:, j * LANES:(j + 1) * LANES] for j in range(nt)], axis=0)
        red = _dot_x_exact(stacked, gs_tile)
        return jnp.concatenate([red[j * ln:(j + 1) * ln] for j in range(nt)], axis=1)

    pre_idx = {nm: idx for idx, nm in enumerate(PRE_NAMES)}
    pending = []

    def prep_steps():
        for bi in range(nb):
            p = p_ref[bi]
            prev = jnp.where(c == 0, 0.0, pp_ref[bi][SUBLANES - 1:SUBLANES, :])
            ps = jnp.where(rows == 0, prev, pltpu.roll(p, 1, axis=0))
            pm = p + (ps - p) * mu_ref[...]
            r = pm[:, 0:aw]
            k = pm[:, aw:2 * aw]
            v = pm[:, 2 * aw:3 * aw]
            xw = pm[:, 3 * aw:3 * aw + LANES]
            xa = pm[:, 3 * aw + LANES:3 * aw + 2 * LANES]
            xg = pm[:, 3 * aw + 2 * LANES:3 * aw + 3 * LANES]
            yield
            w = w0_ref[...] + _bdot(jnp.tanh(xw), w2_ref[...])
            a = _sigmoid(a0_ref[...] + _bdot(xa, a2_ref[...]))
            g = _bdot(_sigmoid(xg), g2_ref[...])
            yield
            w = -_softplus(-w) - 0.5
            lw = -jnp.exp(w)
            kk = k * kk_ref[...]
            kk_ss = gs(kk * kk)
            yield
            cs = _dot_exact_x(tril_f, lw)
            kk = kk / jnp.maximum(jnp.sqrt(kk_ss), 1e-12)
            kmod = k * (1.0 + (a - 1.0) * ka_ref[...])
            yield
            bonus = gs(r * kmod * rk_ref[...])
            bvec = kk * a
            cs_last = cs[ln - 1:ln, :]
            encs = jnp.exp(-cs)
            yield
            dec_end = jnp.exp(cs_last - cs)
            nxt = dict(v=v, g=g, bonus=bonus, rt=r * jnp.exp(cs), kt=kmod * encs, bt=bvec * encs,
                       at=-kk * jnp.exp(cs - lw), bh=bvec * dec_end, kh=kmod * dec_end)
            pending.append((bi, nxt, jnp.broadcast_to(jnp.exp(cs_last), (SUBLANES, aw))))
            yield

    prep = prep_steps()
    tick = lambda: next(prep, None)

    lane = _iota2((ln, LANES), 1)
    lane_in = jnp.where(lane >= hd, lane - hd, lane)
    trow = _iota2((ln, LANES), 0)
    left = lane < hd
    tril_p = lane_in <= trow
    stril_p = lane_in < trow
    eye_p = lane_in == trow
    eye_pf = jnp.where(eye_p, 1.0, 0.0)

    def bd(x):
        xb = x.astype(BF16)
        zero = jnp.zeros_like(xb)
        return jnp.concatenate([jnp.where(left, xb, zero), jnp.where(left, zero, xb)], axis=0)

    def dot(a, b):
        return jnp.dot(a.astype(BF16), b, preferred_element_type=F32)

    npair = heads // 2
    pairs = [(bi, j) for bi in range(nb) for j in range(npair)]

    class _Tiles:
        def __init__(self, name):
            self.idx = pre_idx[name]

        def __getitem__(self, i):
            bi, j = pairs[i]
            return pre_ref[self.idx, bi, :, j * LANES:(j + 1) * LANES]

    at, rt, bt, kt, vv, bh, kh = (_Tiles(n) for n in ("at", "rt", "bt", "kt", "v", "bh", "kh"))
    wl = [wl_ref[bi, 0:1, j * LANES:(j + 1) * LANES] for bi, j in pairs]
    z_all = z_ref[...]
    zs = [z_all[bi, j] for bi, j in pairs]
    npr = range(len(pairs))
    lhs = [jnp.concatenate([at[i], rt[i]], axis=0).astype(BF16) for i in npr]
    abk = [lax.dot_general(lhs[i], jnp.concatenate([bd(bt[i]), bd(kt[i])], axis=0),
                           (((1,), (1,)), ((), ())), preferred_element_type=F32) for i in npr]
    ab = [abk[i][:, :LANES] for i in npr]
    ak = [abk[i][:, LANES:] for i in npr]
    tick()
    nmat = [jnp.where(stril_p, ab[i][:ln], 0.0) for i in npr]
    tinv = [eye_pf + nmat[i] for i in npr]
    npow = [dot(nmat[i], bd(nmat[i])) for i in npr]
    tick()
    for step in range(5):
        bdn = [bd(npow[i]) for i in npr]
        if step < 4:
            both = [dot(jnp.concatenate([tinv[i], npow[i]], axis=0), bdn[i]) for i in npr]
            tinv = [tinv[i] + both[i][:ln] for i in npr]
            npow = [both[i][ln:] for i in npr]
        else:
            tinv = [tinv[i] + dot(tinv[i], bdn[i]) for i in npr]
        tick()
    bdv = [bd(vv[i]) for i in npr]
    bdz = [bd(zs[i]) for i in npr]
    xmat = [dot(jnp.concatenate([jnp.where(stril_p, ak[i][:ln], 0.0), at[i]], axis=1),
                jnp.concatenate([bdv[i], bdz[i]], axis=0)) for i in npr]
    tick()
    u = [dot(tinv[i], bd(xmat[i])) for i in npr]
    tick()
    ys_p = [dot(jnp.concatenate([rt[i], jnp.where(tril_p, ab[i][ln:], 0.0),
                                 jnp.where(tril_p, ak[i][ln:], 0.0)], axis=1),
                jnp.concatenate([bdz[i], bd(u[i]), bdv[i]], axis=0)) for i in npr]
    tick()
    cross = [_bdot_tn(jnp.concatenate([bh[i], kh[i]], axis=0), jnp.concatenate([u[i], vv[i]], axis=0))
             for i in npr]
    tick()
    z_new = []
    for i in npr:
        dg = jnp.where(eye_p, wl[i], 0.0)
        wl_i = jnp.sum(jnp.where(left, dg, 0.0), axis=1, keepdims=True)
        wl_j = jnp.sum(jnp.where(left, 0.0, dg), axis=1, keepdims=True)
        z_new.append(jnp.where(left, wl_i, wl_j) * zs[i] + jnp.where(left, cross[i][:ln], cross[i][ln:]))
    z_ref[...] = jnp.stack(z_new, axis=0).reshape(z_ref.shape)

    tick()
    inv = 1.0 / hd
    for bi in range(nb):
        y = jnp.concatenate(ys_p[bi * npair:(bi + 1) * npair], axis=1)
        mean = gs(y) * inv
        d = y - mean
        var = gs(d * d) * inv
        yn = d * lax.rsqrt(var + RWKV_GN_EPS) * lnw_ref[...] + lnb_ref[...]
        o_ref[bi] = ((yn + pre_ref[pre_idx["bonus"], bi] * pre_ref[pre_idx["v"], bi])
                     * pre_ref[pre_idx["g"], bi]).astype(o_ref.dtype)
        tick()
    for _ in prep:
        pass
    for bi, nxt, wl_next in pending:
        for nm, idx in pre_idx.items():
            pre_ref[idx, bi] = nxt[nm]
        wl_ref[bi] = wl_next


def _rwkv_mix(p_a, prm, heads, nb=4):
    b, t, cin = p_a.shape
    aw = heads * RWKV_HEAD
    ln = RWKV_CHUNK
    nc = t // ln
    sub = ln // SUBLANES
    full = lambda arr: pl.BlockSpec(arr.shape, lambda i, j: (0,) * arr.ndim)
    in_specs = [pl.BlockSpec((nb, ln, cin), lambda i, j: (i, jnp.minimum(j, nc - 1), 0)),
                pl.BlockSpec((nb, SUBLANES, cin),
                             lambda i, j: (i, jnp.maximum(jnp.minimum(j, nc - 1) * sub - 1, 0), 0))]
    in_specs += [full(x) for x in prm]
    return pl.pallas_call(
        functools.partial(_rwkv_kernel, heads, nb),
        grid=(b // nb, nc + 1), in_specs=in_specs,
        out_specs=pl.BlockSpec((nb, ln, aw), lambda i, j: (i, jnp.maximum(j - 1, 0), 0)),
        out_shape=jax.ShapeDtypeStruct((b, t, aw), BF16),
        scratch_shapes=[pltpu.VMEM((nb, heads // 2, RWKV_HEAD, 2 * RWKV_HEAD), F32),
                        pltpu.VMEM((len(PRE_NAMES), nb, ln, aw), F32),
                        pltpu.VMEM((nb, SUBLANES, aw), F32)],
        compiler_params=_cparams(("parallel", "arbitrary")), name="rwkv7",
    )(p_a, p_a, *prm)


def _gmlp_kernel(p_ref, lnw_ref, lnb_ref, ws_ref, bs_ref, o_ref):
    ln = GMLP_CHUNK
    bw = p_ref.shape[2] // 2
    gd = bw // GMLP_GROUPS
    tril = _iota2((ln, ln), 0) >= _iota2((ln, ln), 1)
    ws_c = [jnp.where(tril, ws_ref[gi], 0.0).astype(BF16) for gi in range(GMLP_GROUPS)]
    for ci in range(p_ref.shape[1] // ln):
        x = _gelu_tanh(p_ref[0, ci * ln:(ci + 1) * ln, :].astype(F32))
        for gi in range(GMLP_GROUPS):
            u = x[:, gi * gd:(gi + 1) * gd]
            v = x[:, bw + gi * gd:bw + (gi + 1) * gd]
            mean = jnp.mean(v, axis=-1, keepdims=True)
            d = v - mean
            var = jnp.mean(d * d, axis=-1, keepdims=True)
            vn = d * lax.rsqrt(var + EPS) * lnw_ref[gi:gi + 1, :] + lnb_ref[gi:gi + 1, :]
            s = jnp.dot(ws_c[gi], vn.astype(BF16), preferred_element_type=F32) + bs_ref[gi]
            o_ref[0, ci * ln:(ci + 1) * ln, gi * gd:(gi + 1) * gd] = (u * s).astype(o_ref.dtype)


def _gmlp_mix(p_b, ln_w, ln_b, ws, bs, rows=512):
    b, t, cin = p_b.shape
    bw = cin // 2
    gd = bw // GMLP_GROUPS
    bs_b = jnp.broadcast_to(bs[:, :, None], (GMLP_GROUPS, GMLP_CHUNK, gd))
    full = lambda arr: pl.BlockSpec(arr.shape, lambda i, j: (0,) * arr.ndim)
    return pl.pallas_call(
        _gmlp_kernel, grid=(b, t // rows),
        in_specs=[pl.BlockSpec((1, rows, cin), lambda i, j: (i, j, 0)),
                  full(ln_w), full(ln_b), full(ws), full(bs_b)],
        out_specs=pl.BlockSpec((1, rows, bw), lambda i, j: (i, j, 0)),
        out_shape=jax.ShapeDtypeStruct((b, t, bw), BF16),
        compiler_params=_cparams(("parallel", "parallel")), name="gmlp",
    )(p_b, ln_w, ln_b, ws, bs_b)


def _mix_ffn_kernel(fc, n_out, n_t, x_ref, ya_ref, yb_ref, wa_ref, wb_ref, gmix_ref, gpre_ref, wg_ref, wu_ref,
                    wd_ref, gpost_ref, gnext_ref, *refs):
    w_refs = refs[:n_out]
    wt_refs = refs[n_out:n_out + n_t]
    o_ref = refs[n_out + n_t]
    o_refs = refs[n_out + n_t + 1:2 * n_out + n_t + 1]
    ot_refs = refs[2 * n_out + n_t + 1:]
    y = _bdot(ya_ref[...], wa_ref[...]) + _bdot(yb_ref[...], wb_ref[...])
    x1 = x_ref[...] + _rms(y, gmix_ref[...])
    h = _rms(x1, gpre_ref[...]).astype(BF16)
    acc = None
    for c in range(wg_ref.shape[1] // fc):
        cols = slice(c * fc, (c + 1) * fc)
        gate = jnp.dot(h, wg_ref[:, cols], preferred_element_type=F32)
        up = jnp.dot(h, wu_ref[:, cols], preferred_element_type=F32)
        part = jnp.dot((_silu(gate) * up).astype(BF16), wd_ref[cols, :], preferred_element_type=F32)
        acc = part if acc is None else acc + part
    x2 = x1 + _rms(acc, gpost_ref[...])
    o_ref[...] = x2
    _project(_rms(x2, gnext_ref[...]).astype(BF16), w_refs, wt_refs, o_refs, ot_refs)


def _mix_ffn(x2, ya, yb, wa, wb, gmix, gpre, wg, wu, wd, gpost, gnext, ws, dtypes, wts, tdtypes, seq,
             ch=S5_GROUP_CH, ln=S5_CHUNK, tm=512, fc=MXU_TILE):
    n, d = x2.shape
    row = lambda arr: pl.BlockSpec((tm, arr.shape[1]), lambda i: (i, 0))
    once = lambda arr: pl.BlockSpec(arr.shape, lambda i: (0,) * arr.ndim, pipeline_mode=pl.Buffered(1))
    w_specs, out_specs, out_shape = _proj_specs(n, d, tm, ws, dtypes, wts, tdtypes, seq, ch, ln, once=True)
    return pl.pallas_call(
        functools.partial(_mix_ffn_kernel, fc, len(ws), len(wts)), grid=(n // tm,),
        in_specs=[row(x2), row(ya), row(yb), once(wa), once(wb), once(gmix), once(gpre), once(wg),
                  once(wu), once(wd), once(gpost), once(gnext)] + w_specs,
        out_specs=[row(x2)] + out_specs,
        out_shape=[jax.ShapeDtypeStruct((n, d), F32)] + out_shape,
        compiler_params=_cparams(("parallel",)), name="mix_ffn",
    )(x2, ya, yb, wa, wb, gmix, gpre, wg, wu, wd, gpost, gnext, *ws, *wts)


def _s5_kernel(nc, nb, u_ref, tap_ref, wsr_ref, wsi_ref, wcr_ref, wci_ref, alr_ref, ali_ref, d_ref, o_ref,
               toep_ref):
    ch, ln = u_ref.shape[3], u_ref.shape[4]
    u = u_ref[0].reshape(nc * nb, ch * ln)
    taps = tap_ref[0]
    width = ch * ln
    keep = (_iota2((ln, width), 1) & (ln - 1)) >= _iota2((ln, width), 0)
    for cin in range(ch):
        src = jnp.broadcast_to(taps[cin:cin + 1, :], (ln, width))
        blk = jnp.where(keep, pltpu.roll(src, 0, 1, stride=1, stride_axis=0), 0.0)
        toep_ref[cin * ln:(cin + 1) * ln, :] = blk.astype(BF16)
    y = jnp.dot(u, toep_ref[...], preferred_element_type=F32)
    xer = jnp.dot(u, wsr_ref[0], preferred_element_type=F32)
    xei = jnp.dot(u, wsi_ref[0], preferred_element_type=F32)
    alr = alr_ref[0]
    ali = ali_ref[0]
    cr = jnp.zeros((nb, xer.shape[1]), F32)
    ci = jnp.zeros((nb, xer.shape[1]), F32)
    prs, pis = [], []
    for c in range(nc):
        prs.append(cr)
        pis.append(ci)
        er = xer[c * nb:(c + 1) * nb]
        ei = xei[c * nb:(c + 1) * nb]
        cr, ci = alr * cr - ali * ci + er, alr * ci + ali * cr + ei
    pr = jnp.concatenate(prs, axis=0)
    pi = jnp.concatenate(pis, axis=0)
    y = y + _bdot(pr, wcr_ref[0]) + _bdot(pi, wci_ref[0])
    o_ref[0] = (y + d_ref[0] * u.astype(F32)).astype(o_ref.dtype).reshape(nc, nb, ch, ln)


def _s5_weights(a_re, a_im, log_dt, b_re, b_im, c_re, c_im, d_skip, ln):
    g, st = a_re.shape
    ch = b_re.shape[2]
    dt = jnp.exp(log_dt)[:, None]
    lr, li = a_re, a_im
    tau = jnp.arange(ln + 1, dtype=F32)[:, None, None]
    mag = jnp.exp(lr[None] * dt[None] * tau)
    pw_r = mag * jnp.cos(li[None] * dt[None] * tau)
    pw_i = mag * jnp.sin(li[None] * dt[None] * tau)
    ab_r, ab_i = pw_r[1], pw_i[1]
    nr, ni = ab_r - 1.0, ab_i
    den = lr * lr + li * li
    fr, fi = (nr * lr + ni * li) / den, (ni * lr - nr * li) / den
    bb_r = fr[..., None] * b_re - fi[..., None] * b_im
    bb_i = fr[..., None] * b_im + fi[..., None] * b_re
    cp_r = c_re[None] * pw_r[:ln, :, None, :] - c_im[None] * pw_i[:ln, :, None, :]
    cp_i = c_re[None] * pw_i[:ln, :, None, :] + c_im[None] * pw_r[:ln, :, None, :]
    hp = lax.Precision.HIGHEST
    taps = (jnp.einsum('tgcp,gpd->gdct', cp_r, bb_r, precision=hp)
            - jnp.einsum('tgcp,gpd->gdct', cp_i, bb_i, precision=hp))
    taps = taps.reshape(g, ch, ch * ln)
    rev_r, rev_i = pw_r[:ln][::-1], pw_i[:ln][::-1]
    ws_r = rev_r[..., None] * bb_r[None] - rev_i[..., None] * bb_i[None]
    ws_i = rev_r[..., None] * bb_i[None] + rev_i[..., None] * bb_r[None]
    ws_r = ws_r.transpose(1, 3, 0, 2).reshape(g, ch * ln, st)
    ws_i = ws_i.transpose(1, 3, 0, 2).reshape(g, ch * ln, st)
    q_r, q_i = pw_r[1:ln + 1], pw_i[1:ln + 1]
    wc_r = c_re[None] * q_r[:, :, None, :] - c_im[None] * q_i[:, :, None, :]
    wc_i = -(c_re[None] * q_i[:, :, None, :] + c_im[None] * q_r[:, :, None, :])
    wc_r = wc_r.transpose(1, 3, 2, 0).reshape(g, st, ch * ln)
    wc_i = wc_i.transpose(1, 3, 2, 0).reshape(g, st, ch * ln)
    al_r = pw_r[ln].reshape(g, 1, st)
    al_i = pw_i[ln].reshape(g, 1, st)
    d_t = jnp.repeat(d_skip.reshape(g, ch), ln, axis=1).reshape(g, 1, ch * ln)
    return (taps, ws_r.astype(BF16), ws_i.astype(BF16), wc_r.astype(BF16),
            wc_i.astype(BF16), al_r, al_i, d_t)


def _s5_core(u5, weights):
    g, nc, b, ch, ln = u5.shape
    blk = pl.BlockSpec((1, nc, b, ch, ln), lambda i: (i, 0, 0, 0, 0))
    per_g = lambda arr: pl.BlockSpec((1,) + arr.shape[1:], lambda i: (i, 0, 0))
    return pl.pallas_call(
        functools.partial(_s5_kernel, nc, b), grid=(g,),
        in_specs=[blk] + [per_g(w) for w in weights],
        out_specs=blk, out_shape=jax.ShapeDtypeStruct(u5.shape, BF16),
        scratch_shapes=[pltpu.VMEM((ln * ch, ln * ch), BF16)],
        compiler_params=_cparams(("parallel",)), name="s5",
    )(u5, *weights)


def _ssd_kernel(z_ref, xbc_ref, xp_ref, dt_ref, cw_ref, cb_ref, dtb_ref, alog_ref, dsk_ref, nw_ref,
                o_ref, s_ref):
    c = pl.program_id(1)
    ln = SSD_CHUNK
    hd = SSD_HEAD
    dw = SSD_HEADS * hd
    gn = SSD_STATE

    @pl.when(c == 0)
    def _():
        s_ref[...] = jnp.zeros_like(s_ref)

    xbc = xbc_ref[0]
    halo = xp_ref.shape[1]
    prev = jnp.where(c == 0, jnp.zeros_like(xp_ref[0]), xp_ref[0])
    full = jnp.concatenate([prev, xbc], axis=0).astype(BF16)
    conv = cb_ref[...] + cw_ref[SSD_CONV - 1:SSD_CONV, :] * xbc.astype(F32)
    for j in range(SSD_CONV - 1):
        lag = SSD_CONV - 1 - j
        pick = _iota2((ln, halo + ln), 1) == _iota2((ln, halo + ln), 0) + (halo - lag)
        shifted = jnp.dot(jnp.where(pick, 1.0, 0.0).astype(BF16), full, preferred_element_type=F32)
        conv = conv + cw_ref[j:j + 1, :] * shifted
    act = _silu(conv)
    xh = act[:, :dw]
    dt = _softplus(dt_ref[0] + dtb_ref[...])
    adt = -jnp.exp(alog_ref[...]) * dt
    tril = _iota2((ln, ln), 0) >= _iota2((ln, ln), 1)
    acs = _dot_exact_x(jnp.where(tril, 1.0, 0.0), adt)
    acs_t = acs.T
    tot = acs[ln - 1:ln, :]
    hg = SSD_HEADS // SSD_GROUPS
    s_all = s_ref[...]
    hs = range(SSD_HEADS)
    bm = [act[:, dw + gi * gn:dw + (gi + 1) * gn] for gi in range(SSD_GROUPS)]
    cm = [act[:, dw + SSD_GROUPS * gn + gi * gn:dw + SSD_GROUPS * gn + (gi + 1) * gn]
          for gi in range(SSD_GROUPS)]
    cb = [_bdot_nt(cm[gi], bm[gi]) for gi in range(SSD_GROUPS)]
    col = [acs[:, h:h + 1] for h in hs]
    tot_h = [tot[:, h:h + 1] for h in hs]
    xh_h = [xh[:, h * hd:(h + 1) * hd] for h in hs]
    xdt = [xh_h[h] * dt[:, h:h + 1] for h in hs]
    lcb = [cb[h // hg] * jnp.exp(jnp.where(tril, col[h] - acs_t[h:h + 1, :], -jnp.inf)) for h in hs]
    bdec = [bm[h // hg] * jnp.exp(tot_h[h] - col[h]) for h in hs]
    y_in = [_bdot(lcb[h], xdt[h]) for h in hs]
    y_st = [_bdot(cm[h // hg], s_all[h]) for h in hs]
    s_up = [_bdot_tn(bdec[h], xdt[h]) for h in hs]
    y_heads = [y_in[h] + jnp.exp(col[h]) * y_st[h] + dsk_ref[:, h * hd:(h + 1) * hd] * xh_h[h] for h in hs]
    s_ref[...] = jnp.stack([jnp.exp(tot_h[h]) * s_all[h] + s_up[h] for h in hs], axis=0)
    y = jnp.concatenate(y_heads, axis=1) * _silu(z_ref[0].astype(F32))
    gw = dw // SSD_GROUPS
    for gi in range(SSD_GROUPS):
        yg = y[:, gi * gw:(gi + 1) * gw]
        yg = yg * lax.rsqrt(jnp.mean(yg * yg, axis=-1, keepdims=True) + EPS)
        o_ref[0, :, gi * gw:(gi + 1) * gw] = (yg * nw_ref[:, gi * gw:(gi + 1) * gw]).astype(o_ref.dtype)


def _ssd_mix(z, xbc, dtp, conv_w, conv_b, dt_bias, a_log, d_skip, norm_w):
    b, t, dw = z.shape
    ln = SSD_CHUNK
    xw = xbc.shape[2]
    pad = lambda vec: jnp.pad(vec, (0, LANES - vec.shape[0])).reshape(1, LANES)
    dsk = jnp.repeat(d_skip, SSD_HEAD).reshape(1, dw)
    prm = [conv_w, conv_b.reshape(1, xw), pad(dt_bias), pad(a_log), dsk, norm_w.reshape(1, dw)]
    full = lambda arr: pl.BlockSpec(arr.shape, lambda i, j: (0,) * arr.ndim)
    blk = lambda w: pl.BlockSpec((1, ln, w), lambda i, j: (i, j, 0))
    halo = 2 * SUBLANES
    return pl.pallas_call(
        _ssd_kernel, grid=(b, t // ln),
        in_specs=[blk(dw), blk(xw),
                  pl.BlockSpec((1, halo, xw), lambda i, j: (i, jnp.maximum(j * (ln // halo) - 1, 0), 0)),
                  blk(LANES)] + [full(x) for x in prm],
        out_specs=blk(dw), out_shape=jax.ShapeDtypeStruct((b, t, dw), BF16),
        scratch_shapes=[pltpu.VMEM((SSD_HEADS, SSD_STATE, SSD_HEAD), F32)],
        compiler_params=_cparams(("parallel", "arbitrary")), name="ssd",
    )(z, xbc, xbc, dtp, *prm)


def _mix_out1_kernel(x_ref, yc_ref, yd_ref, gw_ref, gb_ref, wc_ref, wd_ref, gpost_ref, gpre_ref,
                     wrh_ref, wrl_ref, x1_ref, h_ref, idx_ref, gate_ref):
    ng, nchunk, _, ch, ln = yc_ref.shape
    per = nchunk // 2
    grp = range(2)
    rows = [slice(k * per * ln, (k + 1) * per * ln) for k in grp]
    yc = [_gelu_tanh(jnp.concatenate([yc_ref[:, k * per + cl, 0, :, :].reshape(ng * ch, ln)
                                      for cl in range(per)], axis=1).astype(F32)) for k in grp]
    zt = [jnp.dot(gw_ref[...], yc[k].astype(BF16), preferred_element_type=F32) for k in grp]
    yc = [yc[k] * _sigmoid(zt[k] + gb_ref[...]) for k in grp]
    y = [_bdot_tn(yc[k], wc_ref[...]) + _bdot(yd_ref[rows[k], :], wd_ref[...]) for k in grp]
    x1 = [x_ref[rows[k], :] + _rms(y[k], gpost_ref[...]) for k in grp]
    h = [_rms(x1[k], gpre_ref[...]) for k in grp]
    for k in grp:
        x1_ref[rows[k], :] = x1[k]
        h_ref[rows[k], :] = h[k]
    hh = [h[k].astype(BF16) for k in grp]
    hl = [(h[k] - hh[k].astype(F32)).astype(BF16) for k in grp]
    wrh = wrh_ref[...]
    logits = [jnp.dot(hh[k], wrh, preferred_element_type=F32) + jnp.dot(hl[k], wrh, preferred_element_type=F32)
              + jnp.dot(hh[k], wrl_ref[...], preferred_element_type=F32) for k in grp]
    lane = _iota2(logits[0].shape, 1)
    lane_f = lane.astype(F32)
    for k in grp:
        lg = jnp.where(lane < MOE_EXPERTS, logits[k], -jnp.inf)
        m1 = jnp.max(lg, axis=-1, keepdims=True)
        i1 = jnp.min(jnp.where(lg == m1, lane_f, float(LANES)), axis=-1, keepdims=True)
        rest = jnp.where(lane_f == i1, -jnp.inf, lg)
        m2 = jnp.max(rest, axis=-1, keepdims=True)
        i2 = jnp.min(jnp.where(rest == m2, lane_f, float(LANES)), axis=-1, keepdims=True)
        e2 = jnp.exp(m2 - m1)
        g1 = 1.0 / (1.0 + e2)
        g2 = e2 / (1.0 + e2)
        idx_ref[rows[k], :] = jnp.where(lane == 0, i1, jnp.where(lane == 1, i2, 0.0)).astype(jnp.int32)
        gate_ref[rows[k], :] = jnp.where(lane == 0, g1, jnp.where(lane == 1, g2, 0.0))


def _mix_out1(x2, y5, yd, glu_wt, glu_b, wc, wd, gpost, gpre, wr, tm=512):
    n, d = x2.shape
    ng, nc, _, ch, ln = y5.shape
    per_seq = nc * ln // tm
    wr_p = jnp.pad(wr, ((0, 0), (0, LANES - wr.shape[1])))
    wrh = wr_p.astype(BF16)
    wrl = (wr_p - wrh.astype(F32)).astype(BF16)
    row = lambda w: pl.BlockSpec((tm, w), lambda i: (i, 0))
    full = lambda arr: pl.BlockSpec(arr.shape, lambda i: (0,) * arr.ndim)
    prm = [glu_wt, glu_b, wc, wd, gpost, gpre, wrh, wrl]
    return pl.pallas_call(
        _mix_out1_kernel, grid=(n // tm,),
        in_specs=[row(d), pl.BlockSpec((ng, tm // ln, 1, ch, ln),
                                       lambda i: (0, i % per_seq, i // per_seq, 0, 0)),
                  row(yd.shape[1])] + [full(p) for p in prm],
        out_specs=[row(d), row(d), row(LANES), row(LANES)],
        out_shape=[jax.ShapeDtypeStruct((n, d), F32), jax.ShapeDtypeStruct((n, d), F32),
                   jax.ShapeDtypeStruct((n, LANES), jnp.int32), jax.ShapeDtypeStruct((n, LANES), F32)],
        compiler_params=_cparams(("parallel",)), name="mix_out1",
    )(x2, y5, yd, *prm)


GATHER_UNROLL = 8


def _gather_rows(n_rows, make_copy):
    def body(j, carry):
        for q in range(GATHER_UNROLL):
            make_copy(j * GATHER_UNROLL + q).start(priority=q % 2)
        return carry

    lax.fori_loop(0, n_rows // GATHER_UNROLL, body, 0)


def _moe_kernel(be_ref, tok_ref, dst_ref, nact_ref, h_hbm, wg_hbm, wu_hbm, wd_hbm, y_hbm, buf_ref,
                xb_ref, wg_ref, wu_ref, wd_ref, sa_ref, sb_ref, gsem, ssem, wsem):
    i = pl.program_id(0)
    n_blocks = pl.num_programs(0)
    tm = buf_ref.shape[1]
    nact = nact_ref[0]
    active = i < nact
    slot = lax.rem(i, 2)
    other = 1 - slot
    xs = lambda sl: buf_ref.at[sl]
    yb = lambda sl: buf_ref.at[2 + sl]

    def gather_copy(block, sl, r):
        tok = tok_ref[block * tm + r]
        return pltpu.make_async_copy(h_hbm.at[pl.ds(tok, 1)], buf_ref.at[sl, pl.ds(r, 1)], gsem.at[sl])

    def scatter_copy(block, sl, r):
        dst = dst_ref[block * tm + r]
        return pltpu.make_async_copy(buf_ref.at[2 + sl, pl.ds(r, 1)], y_hbm.at[pl.ds(dst, 1)], ssem.at[sl])

    def wait_rows(sem_slot_ref, buf):
        pltpu.make_async_copy(h_hbm.at[pl.ds(0, tm)], buf, sem_slot_ref).wait()

    @pl.when(i == 0)
    def _():
        _gather_rows(tm, functools.partial(gather_copy, 0, 0))
        buf_ref[3] = jnp.zeros(buf_ref.shape[1:], F32)
        n_real = y_hbm.shape[0] - 2 * tm
        for half in range(2):
            init = pltpu.make_async_copy(yb(1), y_hbm.at[pl.ds(n_real + half * tm, tm)], ssem.at[0])
            init.start()
            init.wait()

    @pl.when(i <= nact)
    def _():
        wait_rows(gsem.at[slot], xs(slot))

    @pl.when(jnp.logical_and(i >= 1, i <= nact))
    def _():
        wait_rows(ssem.at[slot], yb(slot))

    @pl.when(i == nact)
    def _():
        _gather_rows(tm, functools.partial(scatter_copy, i - 1, other))
        wait_rows(ssem.at[other], yb(other))

    @pl.when(active)
    def _():
        xb_ref[...] = buf_ref[slot].astype(BF16)

    e = be_ref[i]
    fresh = jnp.logical_or(i == 0, e != be_ref[jnp.maximum(i - 1, 0)])
    nch = wg_ref.shape[1] // MXU_TILE

    pieces = []
    for c in range(nch):
        cols = slice(c * MXU_TILE, (c + 1) * MXU_TILE)
        pieces.append((wg_hbm.at[e, :, cols], sa_ref, 2 * c, wg_ref, (slice(None), cols)))
        pieces.append((wu_hbm.at[e, :, cols], sa_ref, 2 * c + 1, wu_ref, (slice(None), cols)))
        pieces.append((wd_hbm.at[e, cols, :], sb_ref, c, wd_ref, (cols, slice(None))))
    ahead = 2 * 3

    def piece_copy(p):
        src, stage, k, _, _ = pieces[p]
        sl = k % stage.shape[0]
        return pltpu.make_async_copy(src, stage.at[sl], wsem.at[(0 if stage is sa_ref else sa_ref.shape[0]) + sl])

    def compute(load_weights):
        x = xb_ref[...]
        prev = jnp.where(i == 0, n_blocks - 1, i - 1)
        if load_weights:
            for p in range(ahead):
                piece_copy(p).start()
        for c in range(nch):
            if load_weights:
                for p in range(3 * c, 3 * c + 3):
                    _, stage, k, dst, where = pieces[p]
                    piece_copy(p).wait()
                    dst[where] = stage[k % stage.shape[0]].astype(BF16)
                    if p + ahead < len(pieces):
                        piece_copy(p + ahead).start()
            lo, hi = tm * c // nch, tm * (c + 1) // nch
            cuts = [lo, lo + (hi - lo) // 3, lo + 2 * (hi - lo) // 3, hi]

            def row_copies(part):
                for r in range(cuts[part], cuts[part + 1]):
                    gather_copy(i + 1, other, r).start(priority=1)
                    scatter_copy(prev, other, r).start(priority=1)

            cols = slice(c * MXU_TILE, (c + 1) * MXU_TILE)
            row_copies(0)
            gate = jnp.dot(x, wg_ref[:, cols], preferred_element_type=F32)
            buf_ref[4, 0:SUBLANES, 0:LANES] = gate[0:SUBLANES, 0:LANES]
            row_copies(1)
            up = jnp.dot(x, wu_ref[:, cols], preferred_element_type=F32)
            buf_ref[4, SUBLANES:2 * SUBLANES, 0:LANES] = up[0:SUBLANES, 0:LANES]
            row_copies(2)
            part = jnp.dot((_silu(gate) * up).astype(BF16), wd_ref[cols, :], preferred_element_type=F32)
            if c == 0:
                buf_ref[2 + slot] = part
            else:
                buf_ref[2 + slot] += part

    @pl.when(jnp.logical_and(active, fresh))
    def _():
        compute(True)

    @pl.when(jnp.logical_and(active, jnp.logical_not(fresh)))
    def _():
        compute(False)


def _moe_experts(h, block_expert, slot_tok, slot_dst, nact, n_rows, wg, wu, wd):
    n, d = h.shape
    tm = MOE_ROWS
    n_blocks = slot_tok.shape[0] // tm
    ff = wg.shape[2]
    anywhere = pl.BlockSpec(memory_space=pl.ANY)
    grid_spec = pltpu.PrefetchScalarGridSpec(
        num_scalar_prefetch=4, grid=(n_blocks,),
        in_specs=[anywhere, anywhere, anywhere, anywhere],
        out_specs=pl.BlockSpec(memory_space=pl.ANY),
        scratch_shapes=[pltpu.VMEM((5, tm, d), F32), pltpu.VMEM((tm, d), BF16),
                        pltpu.VMEM((d, ff), BF16), pltpu.VMEM((d, ff), BF16), pltpu.VMEM((ff, d), BF16),
                        pltpu.VMEM((4, d, MXU_TILE), F32), pltpu.VMEM((2, MXU_TILE, d), F32),
                        pltpu.SemaphoreType.DMA((2,)), pltpu.SemaphoreType.DMA((2,)),
                        pltpu.SemaphoreType.DMA((6,))])
    return pl.pallas_call(
        _moe_kernel, grid_spec=grid_spec,
        out_shape=jax.ShapeDtypeStruct((n_rows, d), F32),
        compiler_params=pltpu.CompilerParams(dimension_semantics=("arbitrary",),
                                             vmem_limit_bytes=VMEM_LIMIT, disable_bounds_checks=True),
        name="moe_experts",
    )(block_expert, slot_tok, slot_dst, nact, h, wg, wu, wd)


def _combine_kernel(x_ref, y0_ref, y1_ref, gate_ref, gpost_ref, o_ref):
    gates = gate_ref[...]
    y = gates[:, 0:1] * y0_ref[...] + gates[:, 1:2] * y1_ref[...]
    o_ref[...] = x_ref[...] + _rms(y, gpost_ref[...])


def _moe_combine(x1, y, gates, gpost, tm=512):
    n, d = x1.shape
    nt = n // tm
    return pl.pallas_call(
        _combine_kernel, grid=(nt,),
        in_specs=[pl.BlockSpec((tm, d), lambda i: (i, 0)), pl.BlockSpec((tm, d), lambda i: (i, 0)),
                  pl.BlockSpec((tm, d), lambda i: (nt + i, 0)),
                  pl.BlockSpec((tm, LANES), lambda i: (i, 0)), pl.BlockSpec((1, d), lambda i: (0, 0))],
        out_specs=pl.BlockSpec((tm, d), lambda i: (i, 0)),
        out_shape=jax.ShapeDtypeStruct((n, d), F32),
        compiler_params=_cparams(("parallel",)), name="moe_combine",
    )(x1, y, y, gates, gpost)


def _moe_plan(idx, n):
    tm = MOE_ROWS
    flat_e = idx[:, :2].reshape(-1)
    onehot = (flat_e[:, None] == jnp.arange(MOE_EXPERTS, dtype=jnp.int32)[None, :]).astype(jnp.int32)
    csum = jnp.cumsum(onehot, axis=0)
    counts = csum[-1]
    rank = jnp.sum((csum - onehot) * onehot, axis=1)
    padded = (counts + tm - 1) // tm * tm
    pend = jnp.cumsum(padded)
    pstart = pend - padded
    dest = (jnp.sum(onehot * pstart[None, :], axis=1) + rank).astype(jnp.int32)
    n_blocks = (2 * n) // tm + MOE_EXPERTS + 1
    n_slots = n_blocks * tm
    slot_pair = jnp.full((n_slots,), -1, jnp.int32).at[dest].set(
        jnp.arange(2 * n, dtype=jnp.int32), unique_indices=True)
    real = slot_pair >= 0
    slot_tok = jnp.where(real, slot_pair // 2, 0)
    s_id = jnp.arange(n_slots, dtype=jnp.int32)
    slot_dst = jnp.where(real, slot_pair % 2 * n + slot_pair // 2, 2 * n + (s_id // tm) % 2 * tm + s_id % tm)
    block_start = jnp.arange(n_blocks, dtype=jnp.int32) * tm
    block_expert = jnp.minimum(jnp.sum((block_start[:, None] >= pend[None, :]).astype(jnp.int32), axis=1),
                               MOE_EXPERTS - 1)
    nact = (pend[-1] // tm).astype(jnp.int32).reshape(1)
    return block_expert, slot_tok, slot_dst, nact, 2 * n + 2 * tm


def kernel(x, l0_norm_pre_mix, l0_w_in, l0_rwkv_mu, l0_rwkv_w0, l0_rwkv_w2, l0_rwkv_a0, l0_rwkv_a2, l0_rwkv_g2, l0_rwkv_k_k, l0_rwkv_k_a, l0_rwkv_r_k, l0_rwkv_ln_w, l0_rwkv_ln_b, l0_gmlp_ln_w, l0_gmlp_ln_b, l0_gmlp_ws, l0_gmlp_bs, l0_w_out, l0_norm_post_mix, l0_norm_pre_ffn, l0_ffn_w_gate, l0_ffn_w_up, l0_ffn_w_down, l0_norm_post_ffn, l1_norm_pre_mix, l1_w_in, l1_s5_a_re, l1_s5_a_im, l1_s5_log_dt, l1_s5_b_re, l1_s5_b_im, l1_s5_c_re, l1_s5_c_im, l1_s5_d, l1_s5_glu_w, l1_s5_glu_b, l1_m2_conv_w, l1_m2_conv_b, l1_m2_dt_bias, l1_m2_a_log, l1_m2_d, l1_m2_norm_w, l1_w_out, l1_norm_post_mix, l1_norm_pre_ffn, l1_moe_router, l1_moe_w_gate, l1_moe_w_up, l1_moe_w_down, l1_norm_post_ffn):
    b, t, d = x.shape
    n = b * t
    x2 = x.reshape(n, d)
    row = lambda vec: vec.reshape(1, -1)

    aw = l0_rwkv_w0.shape[0]
    heads = aw // RWKV_HEAD
    lw_, la_, lg_ = l0_rwkv_w2.shape[0], l0_rwkv_a2.shape[0], l0_rwkv_g2.shape[0]
    a_in = 3 * aw + lw_ + la_ + lg_
    padc = lambda m, wdt: jnp.pad(m, ((0, 0), (0, LANES - wdt)))
    o = 3 * aw
    w_a = jnp.concatenate([l0_w_in[:, :o], padc(l0_w_in[:, o:o + lw_], lw_),
                           padc(l0_w_in[:, o + lw_:o + lw_ + la_], la_),
                           padc(l0_w_in[:, o + lw_ + la_:a_in], lg_)], axis=1).astype(BF16)
    w_b = l0_w_in[:, a_in:].astype(BF16)
    p_a, p_b = _norm_proj(x2, l0_norm_pre_mix, [w_a, w_b], [F32, BF16])
    padv = lambda vec, wdt: jnp.pad(vec, (0, LANES - wdt))
    mu = l0_rwkv_mu
    mu_p = jnp.concatenate([mu[:o], padv(mu[o:o + lw_], lw_), padv(mu[o + lw_:o + lw_ + la_], la_),
                            padv(mu[o + lw_ + la_:], lg_)])
    padr = lambda m: jnp.pad(m, ((0, LANES - m.shape[0]), (0, 0))).astype(BF16)
    hid = jnp.arange(LANES, dtype=jnp.int32) // RWKV_HEAD
    gsum = (hid[:, None] == hid[None, :]).astype(BF16)
    rwkv_prm = [row(mu_p), row(l0_rwkv_w0), padr(l0_rwkv_w2), row(l0_rwkv_a0), padr(l0_rwkv_a2),
                padr(l0_rwkv_g2), row(l0_rwkv_k_k), row(l0_rwkv_k_a), row(l0_rwkv_r_k),
                row(l0_rwkv_ln_w), row(l0_rwkv_ln_b), gsum]
    ya = _rwkv_mix(p_a.reshape(b, t, -1), rwkv_prm, heads)
    yb = _gmlp_mix(p_b.reshape(b, t, -1), l0_gmlp_ln_w, l0_gmlp_ln_b, l0_gmlp_ws, l0_gmlp_bs)
    wo = l0_w_out.astype(BF16)
    cw = l1_s5_d.shape[0]
    dw = l1_m2_norm_w.shape[0]
    xw = l1_m2_conv_w.shape[1]
    nh = l1_m2_dt_bias.shape[0]
    w1 = l1_w_in
    w_parts = [w1[:, cw:cw + dw], w1[:, cw + dw:cw + dw + xw], padc(w1[:, cw + dw + xw:], nh)]
    x2, z_d, xbc, dtp, u5 = _mix_ffn(
        x2, ya.reshape(n, -1), yb.reshape(n, -1), wo[:aw], wo[aw:], row(l0_norm_post_mix),
        row(l0_norm_pre_ffn), l0_ffn_w_gate.astype(BF16), l0_ffn_w_up.astype(BF16),
        l0_ffn_w_down.astype(BF16), row(l0_norm_post_ffn), row(l1_norm_pre_mix),
        [w.astype(BF16) for w in w_parts], [BF16, BF16, F32], [w1[:, :cw].T.astype(BF16)], [BF16], seq=t)

    s5_w = _s5_weights(l1_s5_a_re, l1_s5_a_im, l1_s5_log_dt, l1_s5_b_re, l1_s5_b_im, l1_s5_c_re,
                       l1_s5_c_im, l1_s5_d, S5_CHUNK)
    y5 = _s5_core(u5, s5_w)
    yd = _ssd_mix(z_d.reshape(b, t, dw), xbc.reshape(b, t, xw), dtp.reshape(b, t, LANES),
                  l1_m2_conv_w, l1_m2_conv_b, l1_m2_dt_bias, l1_m2_a_log, l1_m2_d, l1_m2_norm_w)
    wo1 = l1_w_out.astype(BF16)
    x1, h, idx, gates = _mix_out1(x2, y5, yd.reshape(n, dw), l1_s5_glu_w.T.astype(BF16),
                                  l1_s5_glu_b.reshape(cw, 1), wo1[:cw], wo1[cw:], row(l1_norm_post_mix),
                                  row(l1_norm_pre_ffn), l1_moe_router)
    block_expert, slot_tok, slot_dst, nact, n_rows = _moe_plan(idx, n)
    ys = _moe_experts(h, block_expert, slot_tok, slot_dst, nact, n_rows, l1_moe_w_gate, l1_moe_w_up,
                      l1_moe_w_down)
    out = _moe_combine(x1, ys, gates, row(l1_norm_post_ffn))
    return out.reshape(b, t, d)
```

```python
import functools

import jax
import jax.numpy as jnp
from jax import lax
from jax.experimental import pallas as pl
from jax.experimental.pallas import tpu as pltpu

F32 = jnp.float32
BF16 = jnp.bfloat16

EPS = 1e-6
RWKV_GN_EPS = 64e-5
RWKV_HEAD = 64
RWKV_CHUNK = 64
GMLP_CHUNK = 128
GMLP_GROUPS = 4
S5_GROUP_CH = 16
S5_STATE = 64
S5_CHUNK = 128
SSD_HEAD = 64
SSD_HEADS = 8
SSD_GROUPS = 2
SSD_STATE = 128
SSD_CONV = 4
SSD_CHUNK = 128
MOE_EXPERTS = 8
MOE_ROWS = 512
MXU_TILE = 256
LANES = 128
SUBLANES = 8
VMEM_LIMIT = 56 * 1024 * 1024


def _cparams(sem):
    return pltpu.CompilerParams(dimension_semantics=sem, vmem_limit_bytes=VMEM_LIMIT)


def _bdot(a, b):
    return jnp.dot(a.astype(BF16), b.astype(BF16), preferred_element_type=F32)


def _bdot_nt(a, b):
    return lax.dot_general(a.astype(BF16), b.astype(BF16), (((1,), (1,)), ((), ())),
                           preferred_element_type=F32)


def _bdot_tn(a, b):
    return lax.dot_general(a.astype(BF16), b.astype(BF16), (((0,), (0,)), ((), ())),
                           preferred_element_type=F32)


def _split3(x):
    h = x.astype(BF16)
    r1 = x - h.astype(F32)
    m = r1.astype(BF16)
    l = (r1 - m.astype(F32)).astype(BF16)
    return h, m, l


def _dot_x_exact(x, e):
    h, m, l = _split3(x)
    e = e.astype(BF16)
    return (jnp.dot(h, e, preferred_element_type=F32) + jnp.dot(m, e, preferred_element_type=F32)
            + jnp.dot(l, e, preferred_element_type=F32))


def _dot_exact_x(e, x):
    h, m, l = _split3(x)
    e = e.astype(BF16)
    return (jnp.dot(e, h, preferred_element_type=F32) + jnp.dot(e, m, preferred_element_type=F32)
            + jnp.dot(e, l, preferred_element_type=F32))


def _rms(x, g):
    return x * lax.rsqrt(jnp.mean(x * x, axis=-1, keepdims=True) + EPS) * g


def _sigmoid(x):
    return 1.0 / (1.0 + jnp.exp(-x))


def _silu(x):
    return x * _sigmoid(x)


def _softplus(x):
    return jnp.maximum(x, 0.0) + jnp.log(1.0 + jnp.exp(-jnp.abs(x)))


def _gelu_tanh(x):
    return 0.5 * x * (1.0 + jnp.tanh(0.7978845608028654 * (x + 0.044715 * x * x * x)))


def _iota2(shape, dim):
    return lax.broadcasted_iota(jnp.int32, shape, dim)


def _project(xn, w_refs, wt_refs, o_refs, ot_refs):
    for w_ref, o_ref in zip(w_refs, o_refs):
        o_ref[...] = jnp.dot(xn, w_ref[...], preferred_element_type=F32).astype(o_ref.dtype)
    for wt_ref, ot_ref in zip(wt_refs, ot_refs):
        yt = lax.dot_general(wt_ref[...], xn, (((1,), (1,)), ((), ())),
                             preferred_element_type=F32).astype(ot_ref.dtype)
        ng, nchunk, _, ch, ln = ot_ref.shape
        for cl in range(nchunk):
            ot_ref[:, cl, 0, :, :] = yt[:, cl * ln:(cl + 1) * ln].reshape(ng, ch, ln)


def _norm_proj_kernel(n_out, n_t, x_ref, g_ref, *refs):
    w_refs = refs[:n_out]
    wt_refs = refs[n_out:n_out + n_t]
    o_refs = refs[n_out + n_t:2 * n_out + n_t]
    ot_refs = refs[2 * n_out + n_t:]
    _project(_rms(x_ref[...], g_ref[...]).astype(BF16), w_refs, wt_refs, o_refs, ot_refs)


def _proj_specs(n, d, tm, ws, dtypes, wts, tdtypes, seq, ch, ln, once=False):
    per_seq = (seq or tm) // tm
    mode = dict(pipeline_mode=pl.Buffered(1)) if once else {}
    w_specs = [pl.BlockSpec(w.shape, lambda i: (0, 0), **mode) for w in list(ws) + list(wts)]
    out_specs = [pl.BlockSpec((tm, w.shape[1]), lambda i: (i, 0)) for w in ws]
    out_specs += [pl.BlockSpec((wt.shape[0] // ch, tm // ln, 1, ch, ln),
                               lambda i: (0, i % per_seq, i // per_seq, 0, 0)) for wt in wts]
    out_shape = [jax.ShapeDtypeStruct((n, w.shape[1]), dt) for w, dt in zip(ws, dtypes)]
    out_shape += [jax.ShapeDtypeStruct((wt.shape[0] // ch, seq // ln, n // seq, ch, ln), dt)
                  for wt, dt in zip(wts, tdtypes)]
    return w_specs, out_specs, out_shape


def _norm_proj(x2, g, ws, dtypes, wts=(), tdtypes=(), seq=None, ch=S5_GROUP_CH, ln=S5_CHUNK, tm=512):
    n, d = x2.shape
    w_specs, out_specs, out_shape = _proj_specs(n, d, tm, ws, dtypes, wts, tdtypes, seq, ch, ln)
    in_specs = [pl.BlockSpec((tm, d), lambda i: (i, 0)), pl.BlockSpec((1, d), lambda i: (0, 0))] + w_specs
    return pl.pallas_call(
        functools.partial(_norm_proj_kernel, len(ws), len(wts)),
        grid=(n // tm,), in_specs=in_specs, out_specs=out_specs, out_shape=out_shape,
        compiler_params=_cparams(("parallel",)), name="norm_proj",
    )(x2, g.reshape(1, d), *ws, *wts)


PRE_NAMES = ("v", "g", "bonus", "rt", "kt", "bt", "at", "bh", "kh")


def _rwkv_kernel(heads, nb, p_ref, pp_ref, mu_ref, w0_ref, w2_ref, a0_ref, a2_ref, g2_ref, kk_ref,
                 ka_ref, rk_ref, lnw_ref, lnb_ref, gs_ref, o_ref, z_ref, pre_ref, wl_ref):
    c = pl.program_id(1)
    ln = RWKV_CHUNK
    hd = RWKV_HEAD
    aw = heads * hd

    @pl.when(c == 0)
    def _():
        z_ref[...] = jnp.zeros_like(z_ref)
        pre_ref[...] = jnp.zeros_like(pre_ref)
        wl_ref[...] = jnp.zeros_like(wl_ref)

    tril_f = jnp.where(_iota2((ln, ln), 0) >= _iota2((ln, ln), 1), 1.0, 0.0)
    rows = _iota2((ln, 1), 0)
    gs_tile = gs_ref[...]

    def gs(x):
        nt = x.shape[1] // LANES
        stacked = jnp.concatenate([x[:, j * LANES:(j + 1) * LANES] for j in range(nt)], axis=0)
        red = _dot_x_exact(stacked, gs_tile)
        return jnp.concatenate([red[j * ln:(j + 1) * ln] for j in range(nt)], axis=1)

    pre_idx = {nm: idx for idx, nm in enumerate(PRE_NAMES)}
    pending = []

    def prep_steps():
        for bi in range(nb):
            p = p_ref[bi]
            prev = jnp.where(c == 0, 0.0, pp_ref[bi][SUBLANES - 1:SUBLANES, :])
            ps = jnp.where(rows == 0, prev, pltpu.roll(p, 1, axis=0))
            pm = p + (ps - p) * mu_ref[...]
            r = pm[:, 0:aw]
            k = pm[:, aw:2 * aw]
            v = pm[:, 2 * aw:3 * aw]
            xw = pm[:, 3 * aw:3 * aw + LANES]
            xa = pm[:, 3 * aw + LANES:3 * aw + 2 * LANES]
            xg = pm[:, 3 * aw + 2 * LANES:3 * aw + 3 * LANES]
            yield
            w = w0_ref[...] + _bdot(jnp.tanh(xw), w2_ref[...])
            a = _sigmoid(a0_ref[...] + _bdot(xa, a2_ref[...]))
            g = _bdot(_sigmoid(xg), g2_ref[...])
            yield
            w = -_softplus(-w) - 0.5
            lw = -jnp.exp(w)
            kk = k * kk_ref[...]
            kk_ss = gs(kk * kk)
            yield
            cs = _dot_exact_x(tril_f, lw)
            kk = kk / jnp.maximum(jnp.sqrt(kk_ss), 1e-12)
            kmod = k * (1.0 + (a - 1.0) * ka_ref[...])
            yield
            bonus = gs(r * kmod * rk_ref[...])
            bvec = kk * a
            cs_last = cs[ln - 1:ln, :]
            encs = jnp.exp(-cs)
            yield
            dec_end = jnp.exp(cs_last - cs)
            nxt = dict(v=v, g=g, bonus=bonus, rt=r * jnp.exp(cs), kt=kmod * encs, bt=bvec * encs,
                       at=-kk * jnp.exp(cs - lw), bh=bvec * dec_end, kh=kmod * dec_end)
            pending.append((bi, nxt, jnp.broadcast_to(jnp.exp(cs_last), (SUBLANES, aw))))
            yield

    prep = prep_steps()
    tick = lambda: next(prep, None)

    lane = _iota2((ln, LANES), 1)
    lane_in = jnp.where(lane >= hd, lane - hd, lane)
    trow = _iota2((ln, LANES), 0)
    left = lane < hd
    tril_p = lane_in <= trow
    stril_p = lane_in < trow
    eye_p = lane_in == trow
    eye_pf = jnp.where(eye_p, 1.0, 0.0)

    def bd(x):
        xb = x.astype(BF16)
        zero = jnp.zeros_like(xb)
        return jnp.concatenate([jnp.where(left, xb, zero), jnp.where(left, zero, xb)], axis=0)

    def dot(a, b):
        return jnp.dot(a.astype(BF16), b, preferred_element_type=F32)

    npair = heads // 2
    pairs = [(bi, j) for bi in range(nb) for j in range(npair)]

    class _Tiles:
        def __init__(self, name):
            self.idx = pre_idx[name]

        def __getitem__(self, i):
            bi, j = pairs[i]
            return pre_ref[self.idx, bi, :, j * LANES:(j + 1) * LANES]

    at, rt, bt, kt, vv, bh, kh = (_Tiles(n) for n in ("at", "rt", "bt", "kt", "v", "bh", "kh"))
    wl = [wl_ref[bi, 0:1, j * LANES:(j + 1) * LANES] for bi, j in pairs]
    z_all = z_ref[...]
    zs = [z_all[bi, j] for bi, j in pairs]
    npr = range(len(pairs))
    lhs = [jnp.concatenate([at[i], rt[i]], axis=0).astype(BF16) for i in npr]
    abk = [lax.dot_general(lhs[i], jnp.concatenate([bd(bt[i]), bd(kt[i])], axis=0),
                           (((1,), (1,)), ((), ())), preferred_element_type=F32) for i in npr]
    ab = [abk[i][:, :LANES] for i in npr]
    ak = [abk[i][:, LANES:] for i in npr]
    tick()
    nmat = [jnp.where(stril_p, ab[i][:ln], 0.0) for i in npr]
    tinv = [eye_pf + nmat[i] for i in npr]
    npow = [dot(nmat[i], bd(nmat[i])) for i in npr]
    tick()
    for step in range(5):
        bdn = [bd(npow[i]) for i in npr]
        if step < 4:
            both = [dot(jnp.concatenate([tinv[i], npow[i]], axis=0), bdn[i]) for i in npr]
            tinv = [tinv[i] + both[i][:ln] for i in npr]
            npow = [both[i][ln:] for i in npr]
        else:
            tinv = [tinv[i] + dot(tinv[i], bdn[i]) for i in npr]
        tick()
    bdv = [bd(vv[i]) for i in npr]
    bdz = [bd(zs[i]) for i in npr]
    xmat = [dot(jnp.concatenate([jnp.where(stril_p, ak[i][:ln], 0.0), at[i]], axis=1),
                jnp.concatenate([bdv[i], bdz[i]], axis=0)) for i in npr]
    tick()
    u = [dot(tinv[i], bd(xmat[i])) for i in npr]
    tick()
    ys_p = [dot(jnp.concatenate([rt[i], jnp.where(tril_p, ab[i][ln:], 0.0),
                                 jnp.where(tril_p, ak[i][ln:], 0.0)], axis=1),
                jnp.concatenate([bdz[i], bd(u[i]), bdv[i]], axis=0)) for i in npr]
    tick()
    cross = [_bdot_tn(jnp.concatenate([bh[i], kh[i]], axis=0), jnp.concatenate([u[i], vv[i]], axis=0))
             for i in npr]
    tick()
    z_new = []
    for i in npr:
        dg = jnp.where(eye_p, wl[i], 0.0)
        wl_i = jnp.sum(jnp.where(left, dg, 0.0), axis=1, keepdims=True)
        wl_j = jnp.sum(jnp.where(left, 0.0, dg), axis=1, keepdims=True)
        z_new.append(jnp.where(left, wl_i, wl_j) * zs[i] + jnp.where(left, cross[i][:ln], cross[i][ln:]))
    z_ref[...] = jnp.stack(z_new, axis=0).reshape(z_ref.shape)

    tick()
    inv = 1.0 / hd
    for bi in range(nb):
        y = jnp.concatenate(ys_p[bi * npair:(bi + 1) * npair], axis=1)
        mean = gs(y) * inv
        d = y - mean
        var = gs(d * d) * inv
        yn = d * lax.rsqrt(var + RWKV_GN_EPS) * lnw_ref[...] + lnb_ref[...]
        o_ref[bi] = ((yn + pre_ref[pre_idx["bonus"], bi] * pre_ref[pre_idx["v"], bi])
                     * pre_ref[pre_idx["g"], bi]).astype(o_ref.dtype)
        tick()
    for _ in prep:
        pass
    for bi, nxt, wl_next in pending:
        for nm, idx in pre_idx.items():
            pre_ref[idx, bi] = nxt[nm]
        wl_ref[bi] = wl_next


def _rwkv_mix(p_a, prm, heads, nb=4):
    b, t, cin = p_a.shape
    aw = heads * RWKV_HEAD
    ln = RWKV_CHUNK
    nc = t // ln
    sub = ln // SUBLANES
    full = lambda arr: pl.BlockSpec(arr.shape, lambda i, j: (0,) * arr.ndim)
    in_specs = [pl.BlockSpec((nb, ln, cin), lambda i, j: (i, jnp.minimum(j, nc - 1), 0)),
                pl.BlockSpec((nb, SUBLANES, cin),
                             lambda i, j: (i, jnp.maximum(jnp.minimum(j, nc - 1) * sub - 1, 0), 0))]
    in_specs += [full(x) for x in prm]
    return pl.pallas_call(
        functools.partial(_rwkv_kernel, heads, nb),
        grid=(b // nb, nc + 1), in_specs=in_specs,
        out_specs=pl.BlockSpec((nb, ln, aw), lambda i, j: (i, jnp.maximum(j - 1, 0), 0)),
        out_shape=jax.ShapeDtypeStruct((b, t, aw), BF16),
        scratch_shapes=[pltpu.VMEM((nb, heads // 2, RWKV_HEAD, 2 * RWKV_HEAD), F32),
                        pltpu.VMEM((len(PRE_NAMES), nb, ln, aw), F32),
                        pltpu.VMEM((nb, SUBLANES, aw), F32)],
        compiler_params=_cparams(("parallel", "arbitrary")), name="rwkv7",
    )(p_a, p_a, *prm)


def _gmlp_kernel(p_ref, lnw_ref, lnb_ref, ws_ref, bs_ref, o_ref):
    ln = GMLP_CHUNK
    bw = p_ref.shape[2] // 2
    gd = bw // GMLP_GROUPS
    tril = _iota2((ln, ln), 0) >= _iota2((ln, ln), 1)
    ws_c = [jnp.where(tril, ws_ref[gi], 0.0).astype(BF16) for gi in range(GMLP_GROUPS)]
    for ci in range(p_ref.shape[1] // ln):
        x = _gelu_tanh(p_ref[0, ci * ln:(ci + 1) * ln, :].astype(F32))
        for gi in range(GMLP_GROUPS):
            u = x[:, gi * gd:(gi + 1) * gd]
            v = x[:, bw + gi * gd:bw + (gi + 1) * gd]
            mean = jnp.mean(v, axis=-1, keepdims=True)
            d = v - mean
            var = jnp.mean(d * d, axis=-1, keepdims=True)
            vn = d * lax.rsqrt(var + EPS) * lnw_ref[gi:gi + 1, :] + lnb_ref[gi:gi + 1, :]
            s = jnp.dot(ws_c[gi], vn.astype(BF16), preferred_element_type=F32) + bs_ref[gi]
            o_ref[0, ci * ln:(ci + 1) * ln, gi * gd:(gi + 1) * gd] = (u * s).astype(o_ref.dtype)


def _gmlp_mix(p_b, ln_w, ln_b, ws, bs, rows=512):
    b, t, cin = p_b.shape
    bw = cin // 2
    gd = bw // GMLP_GROUPS
    bs_b = jnp.broadcast_to(bs[:, :, None], (GMLP_GROUPS, GMLP_CHUNK, gd))
    full = lambda arr: pl.BlockSpec(arr.shape, lambda i, j: (0,) * arr.ndim)
    return pl.pallas_call(
        _gmlp_kernel, grid=(b, t // rows),
        in_specs=[pl.BlockSpec((1, rows, cin), lambda i, j: (i, j, 0)),
                  full(ln_w), full(ln_b), full(ws), full(bs_b)],
        out_specs=pl.BlockSpec((1, rows, bw), lambda i, j: (i, j, 0)),
        out_shape=jax.ShapeDtypeStruct((b, t, bw), BF16),
        compiler_params=_cparams(("parallel", "parallel")), name="gmlp",
    )(p_b, ln_w, ln_b, ws, bs_b)


def _mix_ffn_kernel(fc, n_out, n_t, x_ref, ya_ref, yb_ref, wa_ref, wb_ref, gmix_ref, gpre_ref, wg_ref, wu_ref,
                    wd_ref, gpost_ref, gnext_ref, *refs):
    w_refs = refs[:n_out]
    wt_refs = refs[n_out:n_out + n_t]
    o_ref = refs[n_out + n_t]
    o_refs = refs[n_out + n_t + 1:2 * n_out + n_t + 1]
    ot_refs = refs[2 * n_out + n_t + 1:]
    y = _bdot(ya_ref[...], wa_ref[...]) + _bdot(yb_ref[...], wb_ref[...])
    x1 = x_ref[...] + _rms(y, gmix_ref[...])
    h = _rms(x1, gpre_ref[...]).astype(BF16)
    acc = None
    for c in range(wg_ref.shape[1] // fc):
        cols = slice(c * fc, (c + 1) * fc)
        gate = jnp.dot(h, wg_ref[:, cols], preferred_element_type=F32)
        up = jnp.dot(h, wu_ref[:, cols], preferred_element_type=F32)
        part = jnp.dot((_silu(gate) * up).astype(BF16), wd_ref[cols, :], preferred_element_type=F32)
        acc = part if acc is None else acc + part
    x2 = x1 + _rms(acc, gpost_ref[...])
    o_ref[...] = x2
    _project(_rms(x2, gnext_ref[...]).astype(BF16), w_refs, wt_refs, o_refs, ot_refs)


def _mix_ffn(x2, ya, yb, wa, wb, gmix, gpre, wg, wu, wd, gpost, gnext, ws, dtypes, wts, tdtypes, seq,
             ch=S5_GROUP_CH, ln=S5_CHUNK, tm=512, fc=MXU_TILE):
    n, d = x2.shape
    row = lambda arr: pl.BlockSpec((tm, arr.shape[1]), lambda i: (i, 0))
    once = lambda arr: pl.BlockSpec(arr.shape, lambda i: (0,) * arr.ndim, pipeline_mode=pl.Buffered(1))
    w_specs, out_specs, out_shape = _proj_specs(n, d, tm, ws, dtypes, wts, tdtypes, seq, ch, ln, once=True)
    return pl.pallas_call(
        functools.partial(_mix_ffn_kernel, fc, len(ws), len(wts)), grid=(n // tm,),
        in_specs=[row(x2), row(ya), row(yb), once(wa), once(wb), once(gmix), once(gpre), once(wg),
                  once(wu), once(wd), once(gpost), once(gnext)] + w_specs,
        out_specs=[row(x2)] + out_specs,
        out_shape=[jax.ShapeDtypeStruct((n, d), F32)] + out_shape,
        compiler_params=_cparams(("parallel",)), name="mix_ffn",
    )(x2, ya, yb, wa, wb, gmix, gpre, wg, wu, wd, gpost, gnext, *ws, *wts)


def _s5_kernel(nc, nb, u_ref, tap_ref, wsr_ref, wsi_ref, wcr_ref, wci_ref, alr_ref, ali_ref, d_ref, o_ref,
               toep_ref):
    ch, ln = u_ref.shape[3], u_ref.shape[4]
    u = u_ref[0].reshape(nc * nb, ch * ln)
    taps = tap_ref[0]
    width = ch * ln
    keep = (_iota2((ln, width), 1) & (ln - 1)) >= _iota2((ln, width), 0)
    for cin in range(ch):
        src = jnp.broadcast_to(taps[cin:cin + 1, :], (ln, width))
        blk = jnp.where(keep, pltpu.roll(src, 0, 1, stride=1, stride_axis=0), 0.0)
        toep_ref[cin * ln:(cin + 1) * ln, :] = blk.astype(BF16)
    y = jnp.dot(u, toep_ref[...], preferred_element_type=F32)
    xer = jnp.dot(u, wsr_ref[0], preferred_element_type=F32)
    xei = jnp.dot(u, wsi_ref[0], preferred_element_type=F32)
    alr = alr_ref[0]
    ali = ali_ref[0]
    cr = jnp.zeros((nb, xer.shape[1]), F32)
    ci = jnp.zeros((nb, xer.shape[1]), F32)
    prs, pis = [], []
    for c in range(nc):
        prs.append(cr)
        pis.append(ci)
        er = xer[c * nb:(c + 1) * nb]
        ei = xei[c * nb:(c + 1) * nb]
        cr, ci = alr * cr - ali * ci + er, alr * ci + ali * cr + ei
    pr = jnp.concatenate(prs, axis=0)
    pi = jnp.concatenate(pis, axis=0)
    y = y + _bdot(pr, wcr_ref[0]) + _bdot(pi, wci_ref[0])
    o_ref[0] = (y + d_ref[0] * u.astype(F32)).astype(o_ref.dtype).reshape(nc, nb, ch, ln)


def _s5_weights(a_re, a_im, log_dt, b_re, b_im, c_re, c_im, d_skip, ln):
    g, st = a_re.shape
    ch = b_re.shape[2]
    dt = jnp.exp(log_dt)[:, None]
    lr, li = a_re, a_im
    tau = jnp.arange(ln + 1, dtype=F32)[:, None, None]
    mag = jnp.exp(lr[None] * dt[None] * tau)
    pw_r = mag * jnp.cos(li[None] * dt[None] * tau)
    pw_i = mag * jnp.sin(li[None] * dt[None] * tau)
    ab_r, ab_i = pw_r[1], pw_i[1]
    nr, ni = ab_r - 1.0, ab_i
    den = lr * lr + li * li
    fr, fi = (nr * lr + ni * li) / den, (ni * lr - nr * li) / den
    bb_r = fr[..., None] * b_re - fi[..., None] * b_im
    bb_i = fr[..., None] * b_im + fi[..., None] * b_re
    cp_r = c_re[None] * pw_r[:ln, :, None, :] - c_im[None] * pw_i[:ln, :, None, :]
    cp_i = c_re[None] * pw_i[:ln, :, None, :] + c_im[None] * pw_r[:ln, :, None, :]
    hp = lax.Precision.HIGHEST
    taps = (jnp.einsum('tgcp,gpd->gdct', cp_r, bb_r, precision=hp)
            - jnp.einsum('tgcp,gpd->gdct', cp_i, bb_i, precision=hp))
    taps = taps.reshape(g, ch, ch * ln)
    rev_r, rev_i = pw_r[:ln][::-1], pw_i[:ln][::-1]
    ws_r = rev_r[..., None] * bb_r[None] - rev_i[..., None] * bb_i[None]
    ws_i = rev_r[..., None] * bb_i[None] + rev_i[..., None] * bb_r[None]
    ws_r = ws_r.transpose(1, 3, 0, 2).reshape(g, ch * ln, st)
    ws_i = ws_i.transpose(1, 3, 0, 2).reshape(g, ch * ln, st)
    q_r, q_i = pw_r[1:ln + 1], pw_i[1:ln + 1]
    wc_r = c_re[None] * q_r[:, :, None, :] - c_im[None] * q_i[:, :, None, :]
    wc_i = -(c_re[None] * q_i[:, :, None, :] + c_im[None] * q_r[:, :, None, :])
    wc_r = wc_r.transpose(1, 3, 2, 0).reshape(g, st, ch * ln)
    wc_i = wc_i.transpose(1, 3, 2, 0).reshape(g, st, ch * ln)
    al_r = pw_r[ln].reshape(g, 1, st)
    al_i = pw_i[ln].reshape(g, 1, st)
    d_t = jnp.repeat(d_skip.reshape(g, ch), ln, axis=1).reshape(g, 1, ch * ln)
    return (taps, ws_r.astype(BF16), ws_i.astype(BF16), wc_r.astype(BF16),
            wc_i.astype(BF16), al_r, al_i, d_t)


def _s5_core(u5, weights):
    g, nc, b, ch, ln = u5.shape
    blk = pl.BlockSpec((1, nc, b, ch, ln), lambda i: (i, 0, 0, 0, 0))
    per_g = lambda arr: pl.BlockSpec((1,) + arr.shape[1:], lambda i: (i, 0, 0))
    return pl.pallas_call(
        functools.partial(_s5_kernel, nc, b), grid=(g,),
        in_specs=[blk] + [per_g(w) for w in weights],
        out_specs=blk, out_shape=jax.ShapeDtypeStruct(u5.shape, BF16),
        scratch_shapes=[pltpu.VMEM((ln * ch, ln * ch), BF16)],
        compiler_params=_cparams(("parallel",)), name="s5",
    )(u5, *weights)


def _ssd_kernel(z_ref, xbc_ref, xp_ref, dt_ref, cw_ref, cb_ref, dtb_ref, alog_ref, dsk_ref, nw_ref,
                o_ref, s_ref):
    c = pl.program_id(1)
    ln = SSD_CHUNK
    hd = SSD_HEAD
    dw = SSD_HEADS * hd
    gn = SSD_STATE

    @pl.when(c == 0)
    def _():
        s_ref[...] = jnp.zeros_like(s_ref)

    xbc = xbc_ref[0]
    halo = xp_ref.shape[1]
    prev = jnp.where(c == 0, jnp.zeros_like(xp_ref[0]), xp_ref[0])
    full = jnp.concatenate([prev, xbc], axis=0).astype(BF16)
    conv = cb_ref[...] + cw_ref[SSD_CONV - 1:SSD_CONV, :] * xbc.astype(F32)
    for j in range(SSD_CONV - 1):
        lag = SSD_CONV - 1 - j
        pick = _iota2((ln, halo + ln), 1) == _iota2((ln, halo + ln), 0) + (halo - lag)
        shifted = jnp.dot(jnp.where(pick, 1.0, 0.0).astype(BF16), full, preferred_element_type=F32)
        conv = conv + cw_ref[j:j + 1, :] * shifted
    act = _silu(conv)
    xh = act[:, :dw]
    dt = _softplus(dt_ref[0] + dtb_ref[...])
    adt = -jnp.exp(alog_ref[...]) * dt
    tril = _iota2((ln, ln), 0) >= _iota2((ln, ln), 1)
    acs = _dot_exact_x(jnp.where(tril, 1.0, 0.0), adt)
    acs_t = acs.T
    tot = acs[ln - 1:ln, :]
    hg = SSD_HEADS // SSD_GROUPS
    s_all = s_ref[...]
    hs = range(SSD_HEADS)
    bm = [act[:, dw + gi * gn:dw + (gi + 1) * gn] for gi in range(SSD_GROUPS)]
    cm = [act[:, dw + SSD_GROUPS * gn + gi * gn:dw + SSD_GROUPS * gn + (gi + 1) * gn]
          for gi in range(SSD_GROUPS)]
    cb = [_bdot_nt(cm[gi], bm[gi]) for gi in range(SSD_GROUPS)]
    col = [acs[:, h:h + 1] for h in hs]
    tot_h = [tot[:, h:h + 1] for h in hs]
    xh_h = [xh[:, h * hd:(h + 1) * hd] for h in hs]
    xdt = [xh_h[h] * dt[:, h:h + 1] for h in hs]
    lcb = [cb[h // hg] * jnp.exp(jnp.where(tril, col[h] - acs_t[h:h + 1, :], -jnp.inf)) for h in hs]
    bdec = [bm[h // hg] * jnp.exp(tot_h[h] - col[h]) for h in hs]
    y_in = [_bdot(lcb[h], xdt[h]) for h in hs]
    y_st = [_bdot(cm[h // hg], s_all[h]) for h in hs]
    s_up = [_bdot_tn(bdec[h], xdt[h]) for h in hs]
    y_heads = [y_in[h] + jnp.exp(col[h]) * y_st[h] + dsk_ref[:, h * hd:(h + 1) * hd] * xh_h[h] for h in hs]
    s_ref[...] = jnp.stack([jnp.exp(tot_h[h]) * s_all[h] + s_up[h] for h in hs], axis=0)
    y = jnp.concatenate(y_heads, axis=1) * _silu(z_ref[0].astype(F32))
    gw = dw // SSD_GROUPS
    for gi in range(SSD_GROUPS):
        yg = y[:, gi * gw:(gi + 1) * gw]
        yg = yg * lax.rsqrt(jnp.mean(yg * yg, axis=-1, keepdims=True) + EPS)
        o_ref[0, :, gi * gw:(gi + 1) * gw] = (yg * nw_ref[:, gi * gw:(gi + 1) * gw]).astype(o_ref.dtype)


def _ssd_mix(z, xbc, dtp, conv_w, conv_b, dt_bias, a_log, d_skip, norm_w):
    b, t, dw = z.shape
    ln = SSD_CHUNK
    xw = xbc.shape[2]
    pad = lambda vec: jnp.pad(vec, (0, LANES - vec.shape[0])).reshape(1, LANES)
    dsk = jnp.repeat(d_skip, SSD_HEAD).reshape(1, dw)
    prm = [conv_w, conv_b.reshape(1, xw), pad(dt_bias), pad(a_log), dsk, norm_w.reshape(1, dw)]
    full = lambda arr: pl.BlockSpec(arr.shape, lambda i, j: (0,) * arr.ndim)
    blk = lambda w: pl.BlockSpec((1, ln, w), lambda i, j: (i, j, 0))
    halo = 2 * SUBLANES
    return pl.pallas_call(
        _ssd_kernel, grid=(b, t // ln),
        in_specs=[blk(dw), blk(xw),
                  pl.BlockSpec((1, halo, xw), lambda i, j: (i, jnp.maximum(j * (ln // halo) - 1, 0), 0)),
                  blk(LANES)] + [full(x) for x in prm],
        out_specs=blk(dw), out_shape=jax.ShapeDtypeStruct((b, t, dw), BF16),
        scratch_shapes=[pltpu.VMEM((SSD_HEADS, SSD_STATE, SSD_HEAD), F32)],
        compiler_params=_cparams(("parallel", "arbitrary")), name="ssd",
    )(z, xbc, xbc, dtp, *prm)


def _mix_out1_kernel(x_ref, yc_ref, yd_ref, gw_ref, gb_ref, wc_ref, wd_ref, gpost_ref, gpre_ref,
                     wrh_ref, wrl_ref, x1_ref, h_ref, idx_ref, gate_ref):
    ng, nchunk, _, ch, ln = yc_ref.shape
    per = nchunk // 2
    grp = range(2)
    rows = [slice(k * per * ln, (k + 1) * per * ln) for k in grp]
    yc = [_gelu_tanh(jnp.concatenate([yc_ref[:, k * per + cl, 0, :, :].reshape(ng * ch, ln)
                                      for cl in range(per)], axis=1).astype(F32)) for k in grp]
    zt = [jnp.dot(gw_ref[...], yc[k].astype(BF16), preferred_element_type=F32) for k in grp]
    yc = [yc[k] * _sigmoid(zt[k] + gb_ref[...]) for k in grp]
    y = [_bdot_tn(yc[k], wc_ref[...]) + _bdot(yd_ref[rows[k], :], wd_ref[...]) for k in grp]
    x1 = [x_ref[rows[k], :] + _rms(y[k], gpost_ref[...]) for k in grp]
    h = [_rms(x1[k], gpre_ref[...]) for k in grp]
    for k in grp:
        x1_ref[rows[k], :] = x1[k]
        h_ref[rows[k], :] = h[k]
    hh = [h[k].astype(BF16) for k in grp]
    hl = [(h[k] - hh[k].astype(F32)).astype(BF16) for k in grp]
    wrh = wrh_ref[...]
    logits = [jnp.dot(hh[k], wrh, preferred_element_type=F32) + jnp.dot(hl[k], wrh, preferred_element_type=F32)
              + jnp.dot(hh[k], wrl_ref[...], preferred_element_type=F32) for k in grp]
    lane = _iota2(logits[0].shape, 1)
    lane_f = lane.astype(F32)
    for k in grp:
        lg = jnp.where(lane < MOE_EXPERTS, logits[k], -jnp.inf)
        m1 = jnp.max(lg, axis=-1, keepdims=True)
        i1 = jnp.min(jnp.where(lg == m1, lane_f, float(LANES)), axis=-1, keepdims=True)
        rest = jnp.where(lane_f == i1, -jnp.inf, lg)
        m2 = jnp.max(rest, axis=-1, keepdims=True)
        i2 = jnp.min(jnp.where(rest == m2, lane_f, float(LANES)), axis=-1, keepdims=True)
        e2 = jnp.exp(m2 - m1)
        g1 = 1.0 / (1.0 + e2)
        g2 = e2 / (1.0 + e2)
        idx_ref[rows[k], :] = jnp.where(lane == 0, i1, jnp.where(lane == 1, i2, 0.0)).astype(jnp.int32)
        gate_ref[rows[k], :] = jnp.where(lane == 0, g1, jnp.where(lane == 1, g2, 0.0))


def _mix_out1(x2, y5, yd, glu_wt, glu_b, wc, wd, gpost, gpre, wr, tm=512):
    n, d = x2.shape
    ng, nc, _, ch, ln = y5.shape
    per_seq = nc * ln // tm
    wr_p = jnp.pad(wr, ((0, 0), (0, LANES - wr.shape[1])))
    wrh = wr_p.astype(BF16)
    wrl = (wr_p - wrh.astype(F32)).astype(BF16)
    row = lambda w: pl.BlockSpec((tm, w), lambda i: (i, 0))
    full = lambda arr: pl.BlockSpec(arr.shape, lambda i: (0,) * arr.ndim)
    prm = [glu_wt, glu_b, wc, wd, gpost, gpre, wrh, wrl]
    return pl.pallas_call(
        _mix_out1_kernel, grid=(n // tm,),
        in_specs=[row(d), pl.BlockSpec((ng, tm // ln, 1, ch, ln),
                                       lambda i: (0, i % per_seq, i // per_seq, 0, 0)),
                  row(yd.shape[1])] + [full(p) for p in prm],
        out_specs=[row(d), row(d), row(LANES), row(LANES)],
        out_shape=[jax.ShapeDtypeStruct((n, d), F32), jax.ShapeDtypeStruct((n, d), F32),
                   jax.ShapeDtypeStruct((n, LANES), jnp.int32), jax.ShapeDtypeStruct((n, LANES), F32)],
        compiler_params=_cparams(("parallel",)), name="mix_out1",
    )(x2, y5, yd, *prm)


GATHER_UNROLL = 8


def _gather_rows(n_rows, make_copy):
    def body(j, carry):
        for q in range(GATHER_UNROLL):
            make_copy(j * GATHER_UNROLL + q).start(priority=q % 2)
        return carry

    lax.fori_loop(0, n_rows // GATHER_UNROLL, body, 0)


def _moe_kernel(be_ref, dest_ref, nact_ref, h_hbm, wg_hbm, wu_hbm, wd_hbm, y_hbm, buf_ref,
                wg_ref, wu_ref, wd_ref, sa_ref, sb_ref, tok_ref, dst_ref, gsem, ssem, wsem):
    i = pl.program_id(0)
    n_blocks = pl.num_programs(0)
    tm = buf_ref.shape[1]
    nact = nact_ref[0]
    active = i < nact
    slot = lax.rem(i, 2)
    other = 1 - slot
    xs = lambda sl: buf_ref.at[sl]
    yb = lambda sl: buf_ref.at[2 + sl]

    def gather_copy(block, sl, r):
        tok = tok_ref[block * tm + r]
        return pltpu.make_async_copy(h_hbm.at[pl.ds(tok, 1)], buf_ref.at[sl, pl.ds(r, 1)], gsem.at[sl])

    def scatter_copy(block, sl, r):
        dst = dst_ref[block * tm + r]
        return pltpu.make_async_copy(buf_ref.at[2 + sl, pl.ds(r, 1)], y_hbm.at[pl.ds(dst, 1)], ssem.at[sl])

    def wait_rows(sem_slot_ref, buf):
        pltpu.make_async_copy(h_hbm.at[pl.ds(0, tm)], buf, sem_slot_ref).wait()

    @pl.when(i == 0)
    def _():
        n_pairs = dest_ref.shape[0]
        n_tok = n_pairs // 2
        n_real = y_hbm.shape[0] - 2 * tm

        def fill_block(blk, carry):
            base = n_real + lax.rem(blk, 2) * tm

            def fill(m, carry2):
                for q in range(GATHER_UNROLL):
                    r = m * GATHER_UNROLL + q
                    tok_ref[blk * tm + r] = 0
                    dst_ref[blk * tm + r] = base + r
                return carry2

            last_of_expert = be_ref[blk] != be_ref[jnp.minimum(blk + 1, n_blocks - 1)]

            @pl.when(jnp.logical_or(blk >= nact - 1, last_of_expert))
            def _():
                lax.fori_loop(0, tm // GATHER_UNROLL, fill, 0)

            return carry

        lax.fori_loop(0, n_blocks, fill_block, 0)

        def place(m, carry):
            for q in range(GATHER_UNROLL):
                s = dest_ref[m * GATHER_UNROLL + q]
                tok = m * (GATHER_UNROLL // 2) + q // 2
                tok_ref[s] = tok
                dst_ref[s] = (q % 2) * n_tok + tok
            return carry

        lax.fori_loop(0, n_pairs // GATHER_UNROLL, place, 0)
        _gather_rows(tm, functools.partial(gather_copy, 0, 0))
        buf_ref[3] = jnp.zeros(buf_ref.shape[1:], F32)
        n_real = y_hbm.shape[0] - 2 * tm
        for half in range(2):
            init = pltpu.make_async_copy(yb(1), y_hbm.at[pl.ds(n_real + half * tm, tm)], ssem.at[0])
            init.start()
            init.wait()

    @pl.when(i <= nact)
    def _():
        wait_rows(gsem.at[slot], xs(slot))

    @pl.when(jnp.logical_and(i >= 1, i <= nact))
    def _():
        wait_rows(ssem.at[slot], yb(slot))

    @pl.when(i == nact)
    def _():
        _gather_rows(tm, functools.partial(scatter_copy, i - 1, other))
        wait_rows(ssem.at[other], yb(other))

    e = be_ref[i]
    fresh = jnp.logical_or(i == 0, e != be_ref[jnp.maximum(i - 1, 0)])
    nch = wg_ref.shape[1] // MXU_TILE

    pieces = []
    for c in range(nch):
        cols = slice(c * MXU_TILE, (c + 1) * MXU_TILE)
        pieces.append((wg_hbm.at[e, :, cols], sa_ref, 2 * c, wg_ref, (slice(None), cols)))
        pieces.append((wu_hbm.at[e, :, cols], sa_ref, 2 * c + 1, wu_ref, (slice(None), cols)))
        pieces.append((wd_hbm.at[e, cols, :], sb_ref, c, wd_ref, (cols, slice(None))))
    ahead = 2 * 3

    def piece_copy(p):
        src, stage, k, _, _ = pieces[p]
        sl = k % stage.shape[0]
        return pltpu.make_async_copy(src, stage.at[sl], wsem.at[(0 if stage is sa_ref else sa_ref.shape[0]) + sl])

    def compute(load_weights):
        x = buf_ref[slot].astype(BF16)
        prev = jnp.where(i == 0, n_blocks - 1, i - 1)
        if load_weights:
            for p in range(ahead):
                piece_copy(p).start()
        for c in range(nch):
            if load_weights:
                for p in range(3 * c, 3 * c + 3):
                    _, stage, k, dst, where = pieces[p]
                    piece_copy(p).wait()
                    dst[where] = stage[k % stage.shape[0]].astype(BF16)
                    if p + ahead < len(pieces):
                        piece_copy(p + ahead).start()
            lo, hi = tm * c // nch, tm * (c + 1) // nch
            cuts = [lo, lo + (hi - lo) // 3, lo + 2 * (hi - lo) // 3, hi]

            def row_copies(part):
                for r in range(cuts[part], cuts[part + 1]):
                    gather_copy(i + 1, other, r).start(priority=1)
                    scatter_copy(prev, other, r).start(priority=1)

            cols = slice(c * MXU_TILE, (c + 1) * MXU_TILE)
            row_copies(0)
            gate = jnp.dot(x, wg_ref[:, cols], preferred_element_type=F32)
            buf_ref[4, 0:SUBLANES, 0:LANES] = gate[0:SUBLANES, 0:LANES]
            row_copies(1)
            up = jnp.dot(x, wu_ref[:, cols], preferred_element_type=F32)
            buf_ref[4, SUBLANES:2 * SUBLANES, 0:LANES] = up[0:SUBLANES, 0:LANES]
            row_copies(2)
            part = jnp.dot((_silu(gate) * up).astype(BF16), wd_ref[cols, :], preferred_element_type=F32)
            if c == 0:
                buf_ref[2 + slot] = part
            else:
                buf_ref[2 + slot] += part

    @pl.when(jnp.logical_and(active, fresh))
    def _():
        compute(True)

    @pl.when(jnp.logical_and(active, jnp.logical_not(fresh)))
    def _():
        compute(False)


def _moe_experts(h, block_expert, dest, nact, n_blocks, wg, wu, wd):
    n, d = h.shape
    tm = MOE_ROWS
    ff = wg.shape[2]
    anywhere = pl.BlockSpec(memory_space=pl.ANY)
    grid_spec = pltpu.PrefetchScalarGridSpec(
        num_scalar_prefetch=3, grid=(n_blocks,),
        in_specs=[anywhere, anywhere, anywhere, anywhere],
        out_specs=pl.BlockSpec(memory_space=pl.ANY),
        scratch_shapes=[pltpu.VMEM((5, tm, d), F32),
                        pltpu.VMEM((d, ff), BF16), pltpu.VMEM((d, ff), BF16), pltpu.VMEM((ff, d), BF16),
                        pltpu.VMEM((4, d, MXU_TILE), F32), pltpu.VMEM((2, MXU_TILE, d), F32),
                        pltpu.SMEM((n_blocks * tm,), jnp.int32), pltpu.SMEM((n_blocks * tm,), jnp.int32),
                        pltpu.SemaphoreType.DMA((2,)), pltpu.SemaphoreType.DMA((2,)),
                        pltpu.SemaphoreType.DMA((6,))])
    return pl.pallas_call(
        _moe_kernel, grid_spec=grid_spec,
        out_shape=jax.ShapeDtypeStruct((2 * n + 2 * tm, d), F32),
        compiler_params=pltpu.CompilerParams(dimension_semantics=("arbitrary",),
                                             vmem_limit_bytes=VMEM_LIMIT, disable_bounds_checks=True),
        name="moe_experts",
    )(block_expert, dest, nact, h, wg, wu, wd)


def _combine_kernel(x_ref, y0_ref, y1_ref, gate_ref, gpost_ref, o_ref):
    gates = gate_ref[...]
    y = gates[:, 0:1] * y0_ref[...] + gates[:, 1:2] * y1_ref[...]
    o_ref[...] = x_ref[...] + _rms(y, gpost_ref[...])


def _moe_combine(x1, y, gates, gpost, tm=512):
    n, d = x1.shape
    nt = n // tm
    return pl.pallas_call(
        _combine_kernel, grid=(nt,),
        in_specs=[pl.BlockSpec((tm, d), lambda i: (i, 0)), pl.BlockSpec((tm, d), lambda i: (i, 0)),
                  pl.BlockSpec((tm, d), lambda i: (nt + i, 0)),
                  pl.BlockSpec((tm, LANES), lambda i: (i, 0)), pl.BlockSpec((1, d), lambda i: (0, 0))],
        out_specs=pl.BlockSpec((tm, d), lambda i: (i, 0)),
        out_shape=jax.ShapeDtypeStruct((n, d), F32),
        compiler_params=_cparams(("parallel",)), name="moe_combine",
    )(x1, y, y, gates, gpost)


def _moe_plan(idx, n):
    tm = MOE_ROWS
    flat_e = idx[:, :2].reshape(-1)
    onehot = (flat_e[:, None] == jnp.arange(MOE_EXPERTS, dtype=jnp.int32)[None, :]).astype(jnp.int32)
    csum = jnp.cumsum(onehot, axis=0)
    counts = csum[-1]
    rank = jnp.sum((csum - onehot) * onehot, axis=1)
    padded = (counts + tm - 1) // tm * tm
    pend = jnp.cumsum(padded)
    pstart = pend - padded
    dest = (jnp.sum(onehot * pstart[None, :], axis=1) + rank).astype(jnp.int32)
    n_blocks = (2 * n) // tm + MOE_EXPERTS + 1
    block_start = jnp.arange(n_blocks, dtype=jnp.int32) * tm
    block_expert = jnp.minimum(jnp.sum((block_start[:, None] >= pend[None, :]).astype(jnp.int32), axis=1),
                               MOE_EXPERTS - 1)
    nact = (pend[-1] // tm).astype(jnp.int32).reshape(1)
    return block_expert, dest, nact, n_blocks


def kernel(x, l0_norm_pre_mix, l0_w_in, l0_rwkv_mu, l0_rwkv_w0, l0_rwkv_w2, l0_rwkv_a0, l0_rwkv_a2, l0_rwkv_g2, l0_rwkv_k_k, l0_rwkv_k_a, l0_rwkv_r_k, l0_rwkv_ln_w, l0_rwkv_ln_b, l0_gmlp_ln_w, l0_gmlp_ln_b, l0_gmlp_ws, l0_gmlp_bs, l0_w_out, l0_norm_post_mix, l0_norm_pre_ffn, l0_ffn_w_gate, l0_ffn_w_up, l0_ffn_w_down, l0_norm_post_ffn, l1_norm_pre_mix, l1_w_in, l1_s5_a_re, l1_s5_a_im, l1_s5_log_dt, l1_s5_b_re, l1_s5_b_im, l1_s5_c_re, l1_s5_c_im, l1_s5_d, l1_s5_glu_w, l1_s5_glu_b, l1_m2_conv_w, l1_m2_conv_b, l1_m2_dt_bias, l1_m2_a_log, l1_m2_d, l1_m2_norm_w, l1_w_out, l1_norm_post_mix, l1_norm_pre_ffn, l1_moe_router, l1_moe_w_gate, l1_moe_w_up, l1_moe_w_down, l1_norm_post_ffn):
    b, t, d = x.shape
    n = b * t
    x2 = x.reshape(n, d)
    row = lambda vec: vec.reshape(1, -1)

    aw = l0_rwkv_w0.shape[0]
    heads = aw // RWKV_HEAD
    lw_, la_, lg_ = l0_rwkv_w2.shape[0], l0_rwkv_a2.shape[0], l0_rwkv_g2.shape[0]
    a_in = 3 * aw + lw_ + la_ + lg_
    padc = lambda m, wdt: jnp.pad(m, ((0, 0), (0, LANES - wdt)))
    o = 3 * aw
    w_a = jnp.concatenate([l0_w_in[:, :o], padc(l0_w_in[:, o:o + lw_], lw_),
                           padc(l0_w_in[:, o + lw_:o + lw_ + la_], la_),
                           padc(l0_w_in[:, o + lw_ + la_:a_in], lg_)], axis=1).astype(BF16)
    w_b = l0_w_in[:, a_in:].astype(BF16)
    p_a, p_b = _norm_proj(x2, l0_norm_pre_mix, [w_a, w_b], [F32, BF16])
    padv = lambda vec, wdt: jnp.pad(vec, (0, LANES - wdt))
    mu = l0_rwkv_mu
    mu_p = jnp.concatenate([mu[:o], padv(mu[o:o + lw_], lw_), padv(mu[o + lw_:o + lw_ + la_], la_),
                            padv(mu[o + lw_ + la_:], lg_)])
    padr = lambda m: jnp.pad(m, ((0, LANES - m.shape[0]), (0, 0))).astype(BF16)
    hid = jnp.arange(LANES, dtype=jnp.int32) // RWKV_HEAD
    gsum = (hid[:, None] == hid[None, :]).astype(BF16)
    rwkv_prm = [row(mu_p), row(l0_rwkv_w0), padr(l0_rwkv_w2), row(l0_rwkv_a0), padr(l0_rwkv_a2),
                padr(l0_rwkv_g2), row(l0_rwkv_k_k), row(l0_rwkv_k_a), row(l0_rwkv_r_k),
                row(l0_rwkv_ln_w), row(l0_rwkv_ln_b), gsum]
    ya = _rwkv_mix(p_a.reshape(b, t, -1), rwkv_prm, heads)
    yb = _gmlp_mix(p_b.reshape(b, t, -1), l0_gmlp_ln_w, l0_gmlp_ln_b, l0_gmlp_ws, l0_gmlp_bs)
    wo = l0_w_out.astype(BF16)
    cw = l1_s5_d.shape[0]
    dw = l1_m2_norm_w.shape[0]
    xw = l1_m2_conv_w.shape[1]
    nh = l1_m2_dt_bias.shape[0]
    w1 = l1_w_in
    w_parts = [w1[:, cw:cw + dw], w1[:, cw + dw:cw + dw + xw], padc(w1[:, cw + dw + xw:], nh)]
    x2, z_d, xbc, dtp, u5 = _mix_ffn(
        x2, ya.reshape(n, -1), yb.reshape(n, -1), wo[:aw], wo[aw:], row(l0_norm_post_mix),
        row(l0_norm_pre_ffn), l0_ffn_w_gate.astype(BF16), l0_ffn_w_up.astype(BF16),
        l0_ffn_w_down.astype(BF16), row(l0_norm_post_ffn), row(l1_norm_pre_mix),
        [w.astype(BF16) for w in w_parts], [BF16, BF16, F32], [w1[:, :cw].T.astype(BF16)], [BF16], seq=t)

    s5_w = _s5_weights(l1_s5_a_re, l1_s5_a_im, l1_s5_log_dt, l1_s5_b_re, l1_s5_b_im, l1_s5_c_re,
                       l1_s5_c_im, l1_s5_d, S5_CHUNK)
    y5 = _s5_core(u5, s5_w)
    yd = _ssd_mix(z_d.reshape(b, t, dw), xbc.reshape(b, t, xw), dtp.reshape(b, t, LANES),
                  l1_m2_conv_w, l1_m2_conv_b, l1_m2_dt_bias, l1_m2_a_log, l1_m2_d, l1_m2_norm_w)
    wo1 = l1_w_out.astype(BF16)
    x1, h, idx, gates = _mix_out1(x2, y5, yd.reshape(n, dw), l1_s5_glu_w.T.astype(BF16),
                                  l1_s5_glu_b.reshape(cw, 1), wo1[:cw], wo1[cw:], row(l1_norm_post_mix),
                                  row(l1_norm_pre_ffn), l1_moe_router)
    block_expert, dest, nact, n_blocks = _moe_plan(idx, n)
    ys = _moe_experts(h, block_expert, dest, nact, n_blocks, l1_moe_w_gate, l1_moe_w_up, l1_moe_w_down)
    out = _moe_combine(x1, ys, gates, row(l1_norm_post_ffn))
    return out.reshape(b, t, d)
```

```python
import functools

import jax
import jax.numpy as jnp
from jax import lax
from jax.experimental import pallas as pl
from jax.experimental.pallas import tpu as pltpu

F32 = jnp.float32
BF16 = jnp.bfloat16

EPS = 1e-6
RWKV_GN_EPS = 64e-5
RWKV_HEAD = 64
RWKV_CHUNK = 64
GMLP_CHUNK = 128
GMLP_GROUPS = 4
S5_GROUP_CH = 16
S5_STATE = 64
S5_CHUNK = 128
SSD_HEAD = 64
SSD_HEADS = 8
SSD_GROUPS = 2
SSD_STATE = 128
SSD_CONV = 4
SSD_CHUNK = 128
MOE_EXPERTS = 8
MOE_ROWS = 512
MXU_TILE = 256
LANES = 128
SUBLANES = 8
VMEM_LIMIT = 56 * 1024 * 1024


def _cparams(sem):
    return pltpu.CompilerParams(dimension_semantics=sem, vmem_limit_bytes=VMEM_LIMIT)


def _bdot(a, b):
    return jnp.dot(a.astype(BF16), b.astype(BF16), preferred_element_type=F32)


def _bdot_nt(a, b):
    return lax.dot_general(a.astype(BF16), b.astype(BF16), (((1,), (1,)), ((), ())),
                           preferred_element_type=F32)


def _bdot_tn(a, b):
    return lax.dot_general(a.astype(BF16), b.astype(BF16), (((0,), (0,)), ((), ())),
                           preferred_element_type=F32)


def _split3(x):
    h = x.astype(BF16)
    r1 = x - h.astype(F32)
    m = r1.astype(BF16)
    l = (r1 - m.astype(F32)).astype(BF16)
    return h, m, l


def _dot_x_exact(x, e):
    h, m, l = _split3(x)
    e = e.astype(BF16)
    return (jnp.dot(h, e, preferred_element_type=F32) + jnp.dot(m, e, preferred_element_type=F32)
            + jnp.dot(l, e, preferred_element_type=F32))


def _dot_exact_x(e, x):
    h, m, l = _split3(x)
    e = e.astype(BF16)
    return (jnp.dot(e, h, preferred_element_type=F32) + jnp.dot(e, m, preferred_element_type=F32)
            + jnp.dot(e, l, preferred_element_type=F32))


def _rms(x, g):
    return x * lax.rsqrt(jnp.mean(x * x, axis=-1, keepdims=True) + EPS) * g


def _sigmoid(x):
    return 1.0 / (1.0 + jnp.exp(-x))


def _silu(x):
    return x * _sigmoid(x)


def _softplus(x):
    return jnp.maximum(x, 0.0) + jnp.log(1.0 + jnp.exp(-jnp.abs(x)))


def _gelu_tanh(x):
    return 0.5 * x * (1.0 + jnp.tanh(0.7978845608028654 * (x + 0.044715 * x * x * x)))


def _iota2(shape, dim):
    return lax.broadcasted_iota(jnp.int32, shape, dim)


def _project(xn, w_refs, wt_refs, o_refs, ot_refs):
    for w_ref, o_ref in zip(w_refs, o_refs):
        o_ref[...] = jnp.dot(xn, w_ref[...], preferred_element_type=F32).astype(o_ref.dtype)
    for wt_ref, ot_ref in zip(wt_refs, ot_refs):
        yt = lax.dot_general(wt_ref[...], xn, (((1,), (1,)), ((), ())),
                             preferred_element_type=F32).astype(ot_ref.dtype)
        ng, nchunk, _, ch, ln = ot_ref.shape
        for cl in range(nchunk):
            ot_ref[:, cl, 0, :, :] = yt[:, cl * ln:(cl + 1) * ln].reshape(ng, ch, ln)


def _norm_proj_kernel(n_out, n_t, x_ref, g_ref, *refs):
    w_refs = refs[:n_out]
    wt_refs = refs[n_out:n_out + n_t]
    o_refs = refs[n_out + n_t:2 * n_out + n_t]
    ot_refs = refs[2 * n_out + n_t:]
    _project(_rms(x_ref[...], g_ref[...]).astype(BF16), w_refs, wt_refs, o_refs, ot_refs)


def _proj_specs(n, d, tm, ws, dtypes, wts, tdtypes, seq, ch, ln, once=False):
    per_seq = (seq or tm) // tm
    mode = dict(pipeline_mode=pl.Buffered(1)) if once else {}
    w_specs = [pl.BlockSpec(w.shape, lambda i: (0, 0), **mode) for w in list(ws) + list(wts)]
    out_specs = [pl.BlockSpec((tm, w.shape[1]), lambda i: (i, 0)) for w in ws]
    out_specs += [pl.BlockSpec((wt.shape[0] // ch, tm // ln, 1, ch, ln),
                               lambda i: (0, i % per_seq, i // per_seq, 0, 0)) for wt in wts]
    out_shape = [jax.ShapeDtypeStruct((n, w.shape[1]), dt) for w, dt in zip(ws, dtypes)]
    out_shape += [jax.ShapeDtypeStruct((wt.shape[0] // ch, seq // ln, n // seq, ch, ln), dt)
                  for wt, dt in zip(wts, tdtypes)]
    return w_specs, out_specs, out_shape


def _norm_proj(x2, g, ws, dtypes, wts=(), tdtypes=(), seq=None, ch=S5_GROUP_CH, ln=S5_CHUNK, tm=512):
    n, d = x2.shape
    w_specs, out_specs, out_shape = _proj_specs(n, d, tm, ws, dtypes, wts, tdtypes, seq, ch, ln)
    in_specs = [pl.BlockSpec((tm, d), lambda i: (i, 0)), pl.BlockSpec((1, d), lambda i: (0, 0))] + w_specs
    return pl.pallas_call(
        functools.partial(_norm_proj_kernel, len(ws), len(wts)),
        grid=(n // tm,), in_specs=in_specs, out_specs=out_specs, out_shape=out_shape,
        compiler_params=_cparams(("parallel",)), name="norm_proj",
    )(x2, g.reshape(1, d), *ws, *wts)


PRE_NAMES = ("v", "g", "bonus", "rt", "kt", "bt", "at", "bh", "kh")


def _rwkv_kernel(heads, nb, p_ref, pp_ref, mu_ref, w0_ref, w2_ref, a0_ref, a2_ref, g2_ref, kk_ref,
                 ka_ref, rk_ref, lnw_ref, lnb_ref, gs_ref, o_ref, z_ref, pre_ref, wl_ref):
    c = pl.program_id(1)
    ln = RWKV_CHUNK
    hd = RWKV_HEAD
    aw = heads * hd

    @pl.when(c == 0)
    def _():
        z_ref[...] = jnp.zeros_like(z_ref)
        pre_ref[...] = jnp.zeros_like(pre_ref)
        wl_ref[...] = jnp.zeros_like(wl_ref)

    tril_f = jnp.where(_iota2((ln, ln), 0) >= _iota2((ln, ln), 1), 1.0, 0.0)
    rows = _iota2((ln, 1), 0)
    gs_tile = gs_ref[...]

    def gs(x):
        nt = x.shape[1] // LANES
        stacked = jnp.concatenate([x[:, j * LANES:(j + 1) * LANES] for j in range(nt)], axis=0)
        red = _dot_x_exact(stacked, gs_tile)
        return jnp.concatenate([red[j * ln:(j + 1) * ln] for j in range(nt)], axis=1)

    pre_idx = {nm: idx for idx, nm in enumerate(PRE_NAMES)}
    pending = []

    def prep_steps():
        for bi in range(nb):
            p = p_ref[bi]
            prev = jnp.where(c == 0, 0.0, pp_ref[bi][SUBLANES - 1:SUBLANES, :])
            ps = jnp.where(rows == 0, prev, pltpu.roll(p, 1, axis=0))
            pm = p + (ps - p) * mu_ref[...]
            r = pm[:, 0:aw]
            k = pm[:, aw:2 * aw]
            v = pm[:, 2 * aw:3 * aw]
            xw = pm[:, 3 * aw:3 * aw + LANES]
            xa = pm[:, 3 * aw + LANES:3 * aw + 2 * LANES]
            xg = pm[:, 3 * aw + 2 * LANES:3 * aw + 3 * LANES]
            yield
            w = w0_ref[...] + _bdot(jnp.tanh(xw), w2_ref[...])
            a = _sigmoid(a0_ref[...] + _bdot(xa, a2_ref[...]))
            g = _bdot(_sigmoid(xg), g2_ref[...])
            yield
            w = -_softplus(-w) - 0.5
            lw = -jnp.exp(w)
            kk = k * kk_ref[...]
            kk_ss = gs(kk * kk)
            yield
            cs = _dot_exact_x(tril_f, lw)
            kk = kk / jnp.maximum(jnp.sqrt(kk_ss), 1e-12)
            kmod = k * (1.0 + (a - 1.0) * ka_ref[...])
            yield
            bonus = gs(r * kmod * rk_ref[...])
            bvec = kk * a
            cs_last = cs[ln - 1:ln, :]
            encs = jnp.exp(-cs)
            yield
            dec_end = jnp.exp(cs_last - cs)
            nxt = dict(v=v, g=g, bonus=bonus, rt=r * jnp.exp(cs), kt=kmod * encs, bt=bvec * encs,
                       at=-kk * jnp.exp(cs - lw), bh=bvec * dec_end, kh=kmod * dec_end)
            pending.append((bi, nxt, jnp.broadcast_to(jnp.exp(cs_last), (SUBLANES, aw))))
            yield

    prep = prep_steps()
    tick = lambda: next(prep, None)

    lane = _iota2((ln, LANES), 1)
    lane_in = jnp.where(lane >= hd, lane - hd, lane)
    trow = _iota2((ln, LANES), 0)
    left = lane < hd
    tril_p = lane_in <= trow
    stril_p = lane_in < trow
    eye_p = lane_in == trow
    eye_pf = jnp.where(eye_p, 1.0, 0.0)

    def bd(x):
        xb = x.astype(BF16)
        zero = jnp.zeros_like(xb)
        return jnp.concatenate([jnp.where(left, xb, zero), jnp.where(left, zero, xb)], axis=0)

    def dot(a, b):
        return jnp.dot(a.astype(BF16), b, preferred_element_type=F32)

    npair = heads // 2
    pairs = [(bi, j) for bi in range(nb) for j in range(npair)]

    class _Tiles:
        def __init__(self, name):
            self.idx = pre_idx[name]

        def __getitem__(self, i):
            bi, j = pairs[i]
            return pre_ref[self.idx, bi, :, j * LANES:(j + 1) * LANES]

    at, rt, bt, kt, vv, bh, kh = (_Tiles(n) for n in ("at", "rt", "bt", "kt", "v", "bh", "kh"))
    wl = [wl_ref[bi, 0:1, j * LANES:(j + 1) * LANES] for bi, j in pairs]
    z_all = z_ref[...]
    zs = [z_all[bi, j] for bi, j in pairs]
    npr = range(len(pairs))
    lhs = [jnp.concatenate([at[i], rt[i]], axis=0).astype(BF16) for i in npr]
    abk = [lax.dot_general(lhs[i], jnp.concatenate([bd(bt[i]), bd(kt[i])], axis=0),
                           (((1,), (1,)), ((), ())), preferred_element_type=F32) for i in npr]
    ab = [abk[i][:, :LANES] for i in npr]
    ak = [abk[i][:, LANES:] for i in npr]
    tick()
    nmat = [jnp.where(stril_p, ab[i][:ln], 0.0) for i in npr]
    tinv = [eye_pf + nmat[i] for i in npr]
    npow = [dot(nmat[i], bd(nmat[i])) for i in npr]
    tick()
    for step in range(5):
        bdn = [bd(npow[i]) for i in npr]
        if step < 4:
            both = [dot(jnp.concatenate([tinv[i], npow[i]], axis=0), bdn[i]) for i in npr]
            tinv = [tinv[i] + both[i][:ln] for i in npr]
            npow = [both[i][ln:] for i in npr]
        else:
            tinv = [tinv[i] + dot(tinv[i], bdn[i]) for i in npr]
        tick()
    bdv = [bd(vv[i]) for i in npr]
    bdz = [bd(zs[i]) for i in npr]
    xmat = [dot(jnp.concatenate([jnp.where(stril_p, ak[i][:ln], 0.0), at[i]], axis=1),
                jnp.concatenate([bdv[i], bdz[i]], axis=0)) for i in npr]
    tick()
    u = [dot(tinv[i], bd(xmat[i])) for i in npr]
    tick()
    ys_p = [dot(jnp.concatenate([rt[i], jnp.where(tril_p, ab[i][ln:], 0.0),
                                 jnp.where(tril_p, ak[i][ln:], 0.0)], axis=1),
                jnp.concatenate([bdz[i], bd(u[i]), bdv[i]], axis=0)) for i in npr]
    tick()
    cross = [_bdot_tn(jnp.concatenate([bh[i], kh[i]], axis=0), jnp.concatenate([u[i], vv[i]], axis=0))
             for i in npr]
    tick()
    z_new = []
    for i in npr:
        dg = jnp.where(eye_p, wl[i], 0.0)
        wl_i = jnp.sum(jnp.where(left, dg, 0.0), axis=1, keepdims=True)
        wl_j = jnp.sum(jnp.where(left, 0.0, dg), axis=1, keepdims=True)
        z_new.append(jnp.where(left, wl_i, wl_j) * zs[i] + jnp.where(left, cross[i][:ln], cross[i][ln:]))
    z_ref[...] = jnp.stack(z_new, axis=0).reshape(z_ref.shape)

    tick()
    inv = 1.0 / hd
    for bi in range(nb):
        y = jnp.concatenate(ys_p[bi * npair:(bi + 1) * npair], axis=1)
        mean = gs(y) * inv
        d = y - mean
        var = gs(d * d) * inv
        yn = d * lax.rsqrt(var + RWKV_GN_EPS) * lnw_ref[...] + lnb_ref[...]
        o_ref[bi] = ((yn + pre_ref[pre_idx["bonus"], bi] * pre_ref[pre_idx["v"], bi])
                     * pre_ref[pre_idx["g"], bi]).astype(o_ref.dtype)
        tick()
    for _ in prep:
        pass
    for bi, nxt, wl_next in pending:
        for nm, idx in pre_idx.items():
            pre_ref[idx, bi] = nxt[nm]
        wl_ref[bi] = wl_next


def _rwkv_mix(p_a, prm, heads, nb=4):
    b, t, cin = p_a.shape
    aw = heads * RWKV_HEAD
    ln = RWKV_CHUNK
    nc = t // ln
    sub = ln // SUBLANES
    full = lambda arr: pl.BlockSpec(arr.shape, lambda i, j: (0,) * arr.ndim)
    in_specs = [pl.BlockSpec((nb, ln, cin), lambda i, j: (i, jnp.minimum(j, nc - 1), 0)),
                pl.BlockSpec((nb, SUBLANES, cin),
                             lambda i, j: (i, jnp.maximum(jnp.minimum(j, nc - 1) * sub - 1, 0), 0))]
    in_specs += [full(x) for x in prm]
    return pl.pallas_call(
        functools.partial(_rwkv_kernel, heads, nb),
        grid=(b // nb, nc + 1), in_specs=in_specs,
        out_specs=pl.BlockSpec((nb, ln, aw), lambda i, j: (i, jnp.maximum(j - 1, 0), 0)),
        out_shape=jax.ShapeDtypeStruct((b, t, aw), BF16),
        scratch_shapes=[pltpu.VMEM((nb, heads // 2, RWKV_HEAD, 2 * RWKV_HEAD), F32),
                        pltpu.VMEM((len(PRE_NAMES), nb, ln, aw), F32),
                        pltpu.VMEM((nb, SUBLANES, aw), F32)],
        compiler_params=_cparams(("parallel", "arbitrary")), name="rwkv7",
    )(p_a, p_a, *prm)


def _gmlp_kernel(p_ref, lnw_ref, lnb_ref, ws_ref, bs_ref, o_ref):
    ln = GMLP_CHUNK
    bw = p_ref.shape[2] // 2
    gd = bw // GMLP_GROUPS
    tril = _iota2((ln, ln), 0) >= _iota2((ln, ln), 1)
    ws_c = [jnp.where(tril, ws_ref[gi], 0.0).astype(BF16) for gi in range(GMLP_GROUPS)]
    for ci in range(p_ref.shape[1] // ln):
        x = _gelu_tanh(p_ref[0, ci * ln:(ci + 1) * ln, :].astype(F32))
        for gi in range(GMLP_GROUPS):
            u = x[:, gi * gd:(gi + 1) * gd]
            v = x[:, bw + gi * gd:bw + (gi + 1) * gd]
            mean = jnp.mean(v, axis=-1, keepdims=True)
            d = v - mean
            var = jnp.mean(d * d, axis=-1, keepdims=True)
            vn = d * lax.rsqrt(var + EPS) * lnw_ref[gi:gi + 1, :] + lnb_ref[gi:gi + 1, :]
            s = jnp.dot(ws_c[gi], vn.astype(BF16), preferred_element_type=F32) + bs_ref[gi]
            o_ref[0, ci * ln:(ci + 1) * ln, gi * gd:(gi + 1) * gd] = (u * s).astype(o_ref.dtype)


def _gmlp_mix(p_b, ln_w, ln_b, ws, bs, rows=512):
    b, t, cin = p_b.shape
    bw = cin // 2
    gd = bw // GMLP_GROUPS
    bs_b = jnp.broadcast_to(bs[:, :, None], (GMLP_GROUPS, GMLP_CHUNK, gd))
    full = lambda arr: pl.BlockSpec(arr.shape, lambda i, j: (0,) * arr.ndim)
    return pl.pallas_call(
        _gmlp_kernel, grid=(b, t // rows),
        in_specs=[pl.BlockSpec((1, rows, cin), lambda i, j: (i, j, 0)),
                  full(ln_w), full(ln_b), full(ws), full(bs_b)],
        out_specs=pl.BlockSpec((1, rows, bw), lambda i, j: (i, j, 0)),
        out_shape=jax.ShapeDtypeStruct((b, t, bw), BF16),
        compiler_params=_cparams(("parallel", "parallel")), name="gmlp",
    )(p_b, ln_w, ln_b, ws, bs_b)


def _mix_ffn_kernel(fc, n_out, n_t, x_ref, ya_ref, yb_ref, wa_ref, wb_ref, gmix_ref, gpre_ref, wg_ref, wu_ref,
                    wd_ref, gpost_ref, gnext_ref, *refs):
    w_refs = refs[:n_out]
    wt_refs = refs[n_out:n_out + n_t]
    o_ref = refs[n_out + n_t]
    o_refs = refs[n_out + n_t + 1:2 * n_out + n_t + 1]
    ot_refs = refs[2 * n_out + n_t + 1:]
    y = _bdot(ya_ref[...], wa_ref[...]) + _bdot(yb_ref[...], wb_ref[...])
    x1 = x_ref[...] + _rms(y, gmix_ref[...])
    h = _rms(x1, gpre_ref[...]).astype(BF16)
    acc = None
    for c in range(wg_ref.shape[1] // fc):
        cols = slice(c * fc, (c + 1) * fc)
        gate = jnp.dot(h, wg_ref[:, cols], preferred_element_type=F32)
        up = jnp.dot(h, wu_ref[:, cols], preferred_element_type=F32)
        part = jnp.dot((_silu(gate) * up).astype(BF16), wd_ref[cols, :], preferred_element_type=F32)
        acc = part if acc is None else acc + part
    x2 = x1 + _rms(acc, gpost_ref[...])
    o_ref[...] = x2
    _project(_rms(x2, gnext_ref[...]).astype(BF16), w_refs, wt_refs, o_refs, ot_refs)


def _mix_ffn(x2, ya, yb, wa, wb, gmix, gpre, wg, wu, wd, gpost, gnext, ws, dtypes, wts, tdtypes, seq,
             ch=S5_GROUP_CH, ln=S5_CHUNK, tm=512, fc=MXU_TILE):
    n, d = x2.shape
    row = lambda arr: pl.BlockSpec((tm, arr.shape[1]), lambda i: (i, 0))
    once = lambda arr: pl.BlockSpec(arr.shape, lambda i: (0,) * arr.ndim, pipeline_mode=pl.Buffered(1))
    w_specs, out_specs, out_shape = _proj_specs(n, d, tm, ws, dtypes, wts, tdtypes, seq, ch, ln, once=True)
    return pl.pallas_call(
        functools.partial(_mix_ffn_kernel, fc, len(ws), len(wts)), grid=(n // tm,),
        in_specs=[row(x2), row(ya), row(yb), once(wa), once(wb), once(gmix), once(gpre), once(wg),
                  once(wu), once(wd), once(gpost), once(gnext)] + w_specs,
        out_specs=[row(x2)] + out_specs,
        out_shape=[jax.ShapeDtypeStruct((n, d), F32)] + out_shape,
        compiler_params=_cparams(("parallel",)), name="mix_ffn",
    )(x2, ya, yb, wa, wb, gmix, gpre, wg, wu, wd, gpost, gnext, *ws, *wts)


def _s5_kernel(nc, nb, u_ref, tap_ref, wsr_ref, wsi_ref, wcr_ref, wci_ref, alr_ref, ali_ref, d_ref, o_ref,
               toep_ref):
    ch, ln = u_ref.shape[3], u_ref.shape[4]
    u = u_ref[0].reshape(nc * nb, ch * ln)
    taps = tap_ref[0]
    width = ch * ln
    keep = (_iota2((ln, width), 1) & (ln - 1)) >= _iota2((ln, width), 0)
    for cin in range(ch):
        src = jnp.broadcast_to(taps[cin:cin + 1, :], (ln, width))
        blk = jnp.where(keep, pltpu.roll(src, 0, 1, stride=1, stride_axis=0), 0.0)
        toep_ref[cin * ln:(cin + 1) * ln, :] = blk.astype(BF16)
    y = jnp.dot(u, toep_ref[...], preferred_element_type=F32)
    xer = jnp.dot(u, wsr_ref[0], preferred_element_type=F32)
    xei = jnp.dot(u, wsi_ref[0], preferred_element_type=F32)
    alr = alr_ref[0]
    ali = ali_ref[0]
    cr = jnp.zeros((nb, xer.shape[1]), F32)
    ci = jnp.zeros((nb, xer.shape[1]), F32)
    prs, pis = [], []
    for c in range(nc):
        prs.append(cr)
        pis.append(ci)
        er = xer[c * nb:(c + 1) * nb]
        ei = xei[c * nb:(c + 1) * nb]
        cr, ci = alr * cr - ali * ci + er, alr * ci + ali * cr + ei
    pr = jnp.concatenate(prs, axis=0)
    pi = jnp.concatenate(pis, axis=0)
    y = y + _bdot(pr, wcr_ref[0]) + _bdot(pi, wci_ref[0])
    o_ref[0] = (y + d_ref[0] * u.astype(F32)).astype(o_ref.dtype).reshape(nc, nb, ch, ln)


def _s5_weights(a_re, a_im, log_dt, b_re, b_im, c_re, c_im, d_skip, ln):
    g, st = a_re.shape
    ch = b_re.shape[2]
    dt = jnp.exp(log_dt)[:, None]
    lr, li = a_re, a_im
    tau = jnp.arange(ln + 1, dtype=F32)[:, None, None]
    mag = jnp.exp(lr[None] * dt[None] * tau)
    pw_r = mag * jnp.cos(li[None] * dt[None] * tau)
    pw_i = mag * jnp.sin(li[None] * dt[None] * tau)
    ab_r, ab_i = pw_r[1], pw_i[1]
    nr, ni = ab_r - 1.0, ab_i
    den = lr * lr + li * li
    fr, fi = (nr * lr + ni * li) / den, (ni * lr - nr * li) / den
    bb_r = fr[..., None] * b_re - fi[..., None] * b_im
    bb_i = fr[..., None] * b_im + fi[..., None] * b_re
    cp_r = c_re[None] * pw_r[:ln, :, None, :] - c_im[None] * pw_i[:ln, :, None, :]
    cp_i = c_re[None] * pw_i[:ln, :, None, :] + c_im[None] * pw_r[:ln, :, None, :]
    hp = lax.Precision.HIGHEST
    taps = (jnp.einsum('tgcp,gpd->gdct', cp_r, bb_r, precision=hp)
            - jnp.einsum('tgcp,gpd->gdct', cp_i, bb_i, precision=hp))
    taps = taps.reshape(g, ch, ch * ln)
    rev_r, rev_i = pw_r[:ln][::-1], pw_i[:ln][::-1]
    ws_r = rev_r[..., None] * bb_r[None] - rev_i[..., None] * bb_i[None]
    ws_i = rev_r[..., None] * bb_i[None] + rev_i[..., None] * bb_r[None]
    ws_r = ws_r.transpose(1, 3, 0, 2).reshape(g, ch * ln, st)
    ws_i = ws_i.transpose(1, 3, 0, 2).reshape(g, ch * ln, st)
    q_r, q_i = pw_r[1:ln + 1], pw_i[1:ln + 1]
    wc_r = c_re[None] * q_r[:, :, None, :] - c_im[None] * q_i[:, :, None, :]
    wc_i = -(c_re[None] * q_i[:, :, None, :] + c_im[None] * q_r[:, :, None, :])
    wc_r = wc_r.transpose(1, 3, 2, 0).reshape(g, st, ch * ln)
    wc_i = wc_i.transpose(1, 3, 2, 0).reshape(g, st, ch * ln)
    al_r = pw_r[ln].reshape(g, 1, st)
    al_i = pw_i[ln].reshape(g, 1, st)
    d_t = jnp.repeat(d_skip.reshape(g, ch), ln, axis=1).reshape(g, 1, ch * ln)
    return (taps, ws_r.astype(BF16), ws_i.astype(BF16), wc_r.astype(BF16),
            wc_i.astype(BF16), al_r, al_i, d_t)


def _s5_core(u5, weights):
    g, nc, b, ch, ln = u5.shape
    blk = pl.BlockSpec((1, nc, b, ch, ln), lambda i: (i, 0, 0, 0, 0))
    per_g = lambda arr: pl.BlockSpec((1,) + arr.shape[1:], lambda i: (i, 0, 0))
    return pl.pallas_call(
        functools.partial(_s5_kernel, nc, b), grid=(g,),
        in_specs=[blk] + [per_g(w) for w in weights],
        out_specs=blk, out_shape=jax.ShapeDtypeStruct(u5.shape, BF16),
        scratch_shapes=[pltpu.VMEM((ln * ch, ln * ch), BF16)],
        compiler_params=_cparams(("parallel",)), name="s5",
    )(u5, *weights)


def _ssd_kernel(z_ref, xbc_ref, xp_ref, dt_ref, cw_ref, cb_ref, dtb_ref, alog_ref, dsk_ref, nw_ref,
                o_ref, s_ref):
    c = pl.program_id(1)
    ln = SSD_CHUNK
    hd = SSD_HEAD
    dw = SSD_HEADS * hd
    gn = SSD_STATE

    @pl.when(c == 0)
    def _():
        s_ref[...] = jnp.zeros_like(s_ref)

    xbc = xbc_ref[0]
    halo = xp_ref.shape[1]
    prev = jnp.where(c == 0, jnp.zeros_like(xp_ref[0]), xp_ref[0])
    full = jnp.concatenate([prev, xbc], axis=0).astype(BF16)
    conv = cb_ref[...] + cw_ref[SSD_CONV - 1:SSD_CONV, :] * xbc.astype(F32)
    for j in range(SSD_CONV - 1):
        lag = SSD_CONV - 1 - j
        pick = _iota2((ln, halo + ln), 1) == _iota2((ln, halo + ln), 0) + (halo - lag)
        shifted = jnp.dot(jnp.where(pick, 1.0, 0.0).astype(BF16), full, preferred_element_type=F32)
        conv = conv + cw_ref[j:j + 1, :] * shifted
    act = _silu(conv)
    xh = act[:, :dw]
    dt = _softplus(dt_ref[0] + dtb_ref[...])
    adt = -jnp.exp(alog_ref[...]) * dt
    tril = _iota2((ln, ln), 0) >= _iota2((ln, ln), 1)
    acs = _dot_exact_x(jnp.where(tril, 1.0, 0.0), adt)
    acs_t = acs.T
    tot = acs[ln - 1:ln, :]
    hg = SSD_HEADS // SSD_GROUPS
    s_all = s_ref[...]
    hs = range(SSD_HEADS)
    bm = [act[:, dw + gi * gn:dw + (gi + 1) * gn] for gi in range(SSD_GROUPS)]
    cm = [act[:, dw + SSD_GROUPS * gn + gi * gn:dw + SSD_GROUPS * gn + (gi + 1) * gn]
          for gi in range(SSD_GROUPS)]
    cb = [_bdot_nt(cm[gi], bm[gi]) for gi in range(SSD_GROUPS)]
    col = [acs[:, h:h + 1] for h in hs]
    tot_h = [tot[:, h:h + 1] for h in hs]
    xh_h = [xh[:, h * hd:(h + 1) * hd] for h in hs]
    xdt = [xh_h[h] * dt[:, h:h + 1] for h in hs]
    lcb = [cb[h // hg] * jnp.exp(jnp.where(tril, col[h] - acs_t[h:h + 1, :], -jnp.inf)) for h in hs]
    bdec = [bm[h // hg] * jnp.exp(tot_h[h] - col[h]) for h in hs]
    y_in = [_bdot(lcb[h], xdt[h]) for h in hs]
    y_st = [_bdot(cm[h // hg], s_all[h]) for h in hs]
    s_up = [_bdot_tn(bdec[h], xdt[h]) for h in hs]
    y_heads = [y_in[h] + jnp.exp(col[h]) * y_st[h] + dsk_ref[:, h * hd:(h + 1) * hd] * xh_h[h] for h in hs]
    s_ref[...] = jnp.stack([jnp.exp(tot_h[h]) * s_all[h] + s_up[h] for h in hs], axis=0)
    y = jnp.concatenate(y_heads, axis=1) * _silu(z_ref[0].astype(F32))
    gw = dw // SSD_GROUPS
    for gi in range(SSD_GROUPS):
        yg = y[:, gi * gw:(gi + 1) * gw]
        yg = yg * lax.rsqrt(jnp.mean(yg * yg, axis=-1, keepdims=True) + EPS)
        o_ref[0, :, gi * gw:(gi + 1) * gw] = (yg * nw_ref[:, gi * gw:(gi + 1) * gw]).astype(o_ref.dtype)


def _ssd_mix(z, xbc, dtp, conv_w, conv_b, dt_bias, a_log, d_skip, norm_w):
    b, t, dw = z.shape
    ln = SSD_CHUNK
    xw = xbc.shape[2]
    pad = lambda vec: jnp.pad(vec, (0, LANES - vec.shape[0])).reshape(1, LANES)
    dsk = jnp.repeat(d_skip, SSD_HEAD).reshape(1, dw)
    prm = [conv_w, conv_b.reshape(1, xw), pad(dt_bias), pad(a_log), dsk, norm_w.reshape(1, dw)]
    full = lambda arr: pl.BlockSpec(arr.shape, lambda i, j: (0,) * arr.ndim)
    blk = lambda w: pl.BlockSpec((1, ln, w), lambda i, j: (i, j, 0))
    halo = 2 * SUBLANES
    return pl.pallas_call(
        _ssd_kernel, grid=(b, t // ln),
        in_specs=[blk(dw), blk(xw),
                  pl.BlockSpec((1, halo, xw), lambda i, j: (i, jnp.maximum(j * (ln // halo) - 1, 0), 0)),
                  blk(LANES)] + [full(x) for x in prm],
        out_specs=blk(dw), out_shape=jax.ShapeDtypeStruct((b, t, dw), BF16),
        scratch_shapes=[pltpu.VMEM((SSD_HEADS, SSD_STATE, SSD_HEAD), F32)],
        compiler_params=_cparams(("parallel", "arbitrary")), name="ssd",
    )(z, xbc, xbc, dtp, *prm)


def _mix_out1_kernel(x_ref, yc_ref, yd_ref, gw_ref, gb_ref, wc_ref, wd_ref, gpost_ref, gpre_ref,
                     wrh_ref, wrl_ref, x1_ref, h_ref, idx_ref, gate_ref):
    ng, nchunk, _, ch, ln = yc_ref.shape
    per = nchunk // 2
    grp = range(2)
    rows = [slice(k * per * ln, (k + 1) * per * ln) for k in grp]
    yc = [_gelu_tanh(jnp.concatenate([yc_ref[:, k * per + cl, 0, :, :].reshape(ng * ch, ln)
                                      for cl in range(per)], axis=1).astype(F32)) for k in grp]
    zt = [jnp.dot(gw_ref[...], yc[k].astype(BF16), preferred_element_type=F32) for k in grp]
    yc = [yc[k] * _sigmoid(zt[k] + gb_ref[...]) for k in grp]
    y = [_bdot_tn(yc[k], wc_ref[...]) + _bdot(yd_ref[rows[k], :], wd_ref[...]) for k in grp]
    x1 = [x_ref[rows[k], :] + _rms(y[k], gpost_ref[...]) for k in grp]
    h = [_rms(x1[k], gpre_ref[...]) for k in grp]
    for k in grp:
        x1_ref[rows[k], :] = x1[k]
        h_ref[rows[k], :] = h[k]
    hh = [h[k].astype(BF16) for k in grp]
    hl = [(h[k] - hh[k].astype(F32)).astype(BF16) for k in grp]
    wrh = wrh_ref[...]
    logits = [jnp.dot(hh[k], wrh, preferred_element_type=F32) + jnp.dot(hl[k], wrh, preferred_element_type=F32)
              + jnp.dot(hh[k], wrl_ref[...], preferred_element_type=F32) for k in grp]
    lane = _iota2(logits[0].shape, 1)
    lane_f = lane.astype(F32)
    for k in grp:
        lg = jnp.where(lane < MOE_EXPERTS, logits[k], -jnp.inf)
        m1 = jnp.max(lg, axis=-1, keepdims=True)
        i1 = jnp.min(jnp.where(lg == m1, lane_f, float(LANES)), axis=-1, keepdims=True)
        rest = jnp.where(lane_f == i1, -jnp.inf, lg)
        m2 = jnp.max(rest, axis=-1, keepdims=True)
        i2 = jnp.min(jnp.where(rest == m2, lane_f, float(LANES)), axis=-1, keepdims=True)
        e2 = jnp.exp(m2 - m1)
        g1 = 1.0 / (1.0 + e2)
        g2 = e2 / (1.0 + e2)
        idx_ref[rows[k], :] = jnp.where(lane == 0, i1, jnp.where(lane == 1, i2, 0.0)).astype(jnp.int32)
        gate_ref[rows[k], :] = jnp.where(lane == 0, g1, jnp.where(lane == 1, g2, 0.0))


def _mix_out1(x2, y5, yd, glu_wt, glu_b, wc, wd, gpost, gpre, wr, tm=512):
    n, d = x2.shape
    ng, nc, _, ch, ln = y5.shape
    per_seq = nc * ln // tm
    wr_p = jnp.pad(wr, ((0, 0), (0, LANES - wr.shape[1])))
    wrh = wr_p.astype(BF16)
    wrl = (wr_p - wrh.astype(F32)).astype(BF16)
    row = lambda w: pl.BlockSpec((tm, w), lambda i: (i, 0))
    full = lambda arr: pl.BlockSpec(arr.shape, lambda i: (0,) * arr.ndim)
    prm = [glu_wt, glu_b, wc, wd, gpost, gpre, wrh, wrl]
    return pl.pallas_call(
        _mix_out1_kernel, grid=(n // tm,),
        in_specs=[row(d), pl.BlockSpec((ng, tm // ln, 1, ch, ln),
                                       lambda i: (0, i % per_seq, i // per_seq, 0, 0)),
                  row(yd.shape[1])] + [full(p) for p in prm],
        out_specs=[row(d), row(d), row(LANES), row(LANES)],
        out_shape=[jax.ShapeDtypeStruct((n, d), F32), jax.ShapeDtypeStruct((n, d), F32),
                   jax.ShapeDtypeStruct((n, LANES), jnp.int32), jax.ShapeDtypeStruct((n, LANES), F32)],
        compiler_params=_cparams(("parallel",)), name="mix_out1",
    )(x2, y5, yd, *prm)


GATHER_UNROLL = 8


def _gather_rows(n_rows, make_copy):
    def body(j, carry):
        for q in range(GATHER_UNROLL):
            make_copy(j * GATHER_UNROLL + q).start(priority=q % 2)
        return carry

    lax.fori_loop(0, n_rows // GATHER_UNROLL, body, 0)


def _moe_kernel(be_ref, dest_ref, nact_ref, h_hbm, wg_hbm, wu_hbm, wd_hbm, y_hbm, buf_ref,
                xb_ref, wg_ref, wu_ref, wd_ref, sa_ref, sb_ref, tok_ref, dst_ref, gsem, ssem, wsem):
    i = pl.program_id(0)
    n_blocks = pl.num_programs(0)
    tm = buf_ref.shape[1]
    nact = nact_ref[0]
    active = i < nact
    slot = lax.rem(i, 2)
    other = 1 - slot
    xs = lambda sl: buf_ref.at[sl]
    yb = lambda sl: buf_ref.at[2 + sl]

    def gather_copy(block, sl, r):
        tok = tok_ref[block * tm + r]
        return pltpu.make_async_copy(h_hbm.at[pl.ds(tok, 1)], buf_ref.at[sl, pl.ds(r, 1)], gsem.at[sl])

    def scatter_copy(block, sl, r):
        dst = dst_ref[block * tm + r]
        return pltpu.make_async_copy(buf_ref.at[2 + sl, pl.ds(r, 1)], y_hbm.at[pl.ds(dst, 1)], ssem.at[sl])

    def wait_rows(sem_slot_ref, buf):
        pltpu.make_async_copy(h_hbm.at[pl.ds(0, tm)], buf, sem_slot_ref).wait()

    @pl.when(i == 0)
    def _():
        n_pairs = dest_ref.shape[0]
        n_tok = n_pairs // 2
        n_real = y_hbm.shape[0] - 2 * tm

        def fill_block(blk, carry):
            base = n_real + lax.rem(blk, 2) * tm

            def fill(m, carry2):
                for q in range(GATHER_UNROLL):
                    r = m * GATHER_UNROLL + q
                    tok_ref[blk * tm + r] = 0
                    dst_ref[blk * tm + r] = base + r
                return carry2

            last_of_expert = be_ref[blk] != be_ref[jnp.minimum(blk + 1, n_blocks - 1)]

            @pl.when(jnp.logical_or(blk >= nact - 1, last_of_expert))
            def _():
                lax.fori_loop(0, tm // GATHER_UNROLL, fill, 0)

            return carry

        lax.fori_loop(0, n_blocks, fill_block, 0)

        def place(m, carry):
            for q in range(GATHER_UNROLL):
                s = dest_ref[m * GATHER_UNROLL + q]
                tok = m * (GATHER_UNROLL // 2) + q // 2
                tok_ref[s] = tok
                dst_ref[s] = (q % 2) * n_tok + tok
            return carry

        lax.fori_loop(0, nact_ref[1], place, 0)
        _gather_rows(tm, functools.partial(gather_copy, 0, 0))
        buf_ref[3] = jnp.zeros(buf_ref.shape[1:], F32)
        n_real = y_hbm.shape[0] - 2 * tm
        for half in range(2):
            init = pltpu.make_async_copy(yb(1), y_hbm.at[pl.ds(n_real + half * tm, tm)], ssem.at[0])
            init.start()
            init.wait()

    @pl.when(i <= nact)
    def _():
        wait_rows(gsem.at[slot], xs(slot))

    @pl.when(jnp.logical_and(i >= 1, i <= nact))
    def _():
        wait_rows(ssem.at[slot], yb(slot))

    @pl.when(i == nact)
    def _():
        _gather_rows(tm, functools.partial(scatter_copy, i - 1, other))
        wait_rows(ssem.at[other], yb(other))

    @pl.when(active)
    def _():
        xb_ref[...] = buf_ref[slot].astype(BF16)

    e = be_ref[i]
    fresh = jnp.logical_or(i == 0, e != be_ref[jnp.maximum(i - 1, 0)])
    nch = wg_ref.shape[1] // MXU_TILE

    pieces = []
    for c in range(nch):
        cols = slice(c * MXU_TILE, (c + 1) * MXU_TILE)
        pieces.append((wg_hbm.at[e, :, cols], sa_ref, 2 * c, wg_ref, (slice(None), cols)))
        pieces.append((wu_hbm.at[e, :, cols], sa_ref, 2 * c + 1, wu_ref, (slice(None), cols)))
        pieces.append((wd_hbm.at[e, cols, :], sb_ref, c, wd_ref, (cols, slice(None))))
    ahead = 2 * 3

    def piece_copy(p):
        src, stage, k, _, _ = pieces[p]
        sl = k % stage.shape[0]
        return pltpu.make_async_copy(src, stage.at[sl], wsem.at[(0 if stage is sa_ref else sa_ref.shape[0]) + sl])

    def compute(load_weights):
        x = xb_ref[...]
        prev = jnp.where(i == 0, n_blocks - 1, i - 1)
        if load_weights:
            for p in range(ahead):
                piece_copy(p).start()
        for c in range(nch):
            if load_weights:
                for p in range(3 * c, 3 * c + 3):
                    _, stage, k, dst, where = pieces[p]
                    piece_copy(p).wait()
                    dst[where] = stage[k % stage.shape[0]].astype(BF16)
                    if p + ahead < len(pieces):
                        piece_copy(p + ahead).start()
            lo, hi = tm * c // nch, tm * (c + 1) // nch
            cuts = [lo, lo + (hi - lo) // 3, lo + 2 * (hi - lo) // 3, hi]

            def row_copies(part):
                for r in range(cuts[part], cuts[part + 1]):
                    gather_copy(i + 1, other, r).start(priority=1)
                    scatter_copy(prev, other, r).start(priority=1)

            cols = slice(c * MXU_TILE, (c + 1) * MXU_TILE)
            row_copies(0)
            gate = jnp.dot(x, wg_ref[:, cols], preferred_element_type=F32)
            buf_ref[4, 0:SUBLANES, 0:LANES] = gate[0:SUBLANES, 0:LANES]
            row_copies(1)
            up = jnp.dot(x, wu_ref[:, cols], preferred_element_type=F32)
            buf_ref[4, SUBLANES:2 * SUBLANES, 0:LANES] = up[0:SUBLANES, 0:LANES]
            row_copies(2)
            part = jnp.dot((_silu(gate) * up).astype(BF16), wd_ref[cols, :], preferred_element_type=F32)
            if c == 0:
                buf_ref[2 + slot] = part
            else:
                buf_ref[2 + slot] += part

    @pl.when(jnp.logical_and(active, fresh))
    def _():
        compute(True)

    @pl.when(jnp.logical_and(active, jnp.logical_not(fresh)))
    def _():
        compute(False)


def _moe_experts(h, block_expert, dest, nact, n_blocks, wg, wu, wd):
    n, d = h.shape
    tm = MOE_ROWS
    ff = wg.shape[2]
    anywhere = pl.BlockSpec(memory_space=pl.ANY)
    grid_spec = pltpu.PrefetchScalarGridSpec(
        num_scalar_prefetch=3, grid=(n_blocks,),
        in_specs=[anywhere, anywhere, anywhere, anywhere],
        out_specs=pl.BlockSpec(memory_space=pl.ANY),
        scratch_shapes=[pltpu.VMEM((5, tm, d), F32), pltpu.VMEM((tm, d), BF16),
                        pltpu.VMEM((d, ff), BF16), pltpu.VMEM((d, ff), BF16), pltpu.VMEM((ff, d), BF16),
                        pltpu.VMEM((4, d, MXU_TILE), F32), pltpu.VMEM((2, MXU_TILE, d), F32),
                        pltpu.SMEM((n_blocks * tm,), jnp.int32), pltpu.SMEM((n_blocks * tm,), jnp.int32),
                        pltpu.SemaphoreType.DMA((2,)), pltpu.SemaphoreType.DMA((2,)),
                        pltpu.SemaphoreType.DMA((6,))])
    return pl.pallas_call(
        _moe_kernel, grid_spec=grid_spec,
        out_shape=jax.ShapeDtypeStruct((2 * n + 2 * tm, d), F32),
        compiler_params=pltpu.CompilerParams(dimension_semantics=("arbitrary",),
                                             vmem_limit_bytes=VMEM_LIMIT, disable_bounds_checks=True),
        name="moe_experts",
    )(block_expert, dest, nact, h, wg, wu, wd)


def _combine_kernel(x_ref, y0_ref, y1_ref, gate_ref, gpost_ref, o_ref):
    gates = gate_ref[...]
    y = gates[:, 0:1] * y0_ref[...] + gates[:, 1:2] * y1_ref[...]
    o_ref[...] = x_ref[...] + _rms(y, gpost_ref[...])


def _moe_combine(x1, y, gates, gpost, tm=512):
    n, d = x1.shape
    nt = n // tm
    return pl.pallas_call(
        _combine_kernel, grid=(nt,),
        in_specs=[pl.BlockSpec((tm, d), lambda i: (i, 0)), pl.BlockSpec((tm, d), lambda i: (i, 0)),
                  pl.BlockSpec((tm, d), lambda i: (nt + i, 0)),
                  pl.BlockSpec((tm, LANES), lambda i: (i, 0)), pl.BlockSpec((1, d), lambda i: (0, 0))],
        out_specs=pl.BlockSpec((tm, d), lambda i: (i, 0)),
        out_shape=jax.ShapeDtypeStruct((n, d), F32),
        compiler_params=_cparams(("parallel",)), name="moe_combine",
    )(x1, y, y, gates, gpost)


def _moe_plan(idx, n):
    tm = MOE_ROWS
    flat_e = idx[:, :2].reshape(-1)
    onehot = (flat_e[:, None] == jnp.arange(MOE_EXPERTS, dtype=jnp.int32)[None, :]).astype(jnp.int32)
    csum = jnp.cumsum(onehot, axis=0)
    counts = csum[-1]
    rank = jnp.sum((csum - onehot) * onehot, axis=1)
    padded = (counts + tm - 1) // tm * tm
    pend = jnp.cumsum(padded)
    pstart = pend - padded
    dest = (jnp.sum(onehot * pstart[None, :], axis=1) + rank).astype(jnp.int32)
    n_blocks = (2 * n) // tm + MOE_EXPERTS + 1
    block_start = jnp.arange(n_blocks, dtype=jnp.int32) * tm
    block_expert = jnp.minimum(jnp.sum((block_start[:, None] >= pend[None, :]).astype(jnp.int32), axis=1),
                               MOE_EXPERTS - 1)
    nact = jnp.stack([(pend[-1] // tm).astype(jnp.int32), jnp.int32(2 * n // GATHER_UNROLL)])
    return block_expert, dest, nact, n_blocks


def kernel(x, l0_norm_pre_mix, l0_w_in, l0_rwkv_mu, l0_rwkv_w0, l0_rwkv_w2, l0_rwkv_a0, l0_rwkv_a2, l0_rwkv_g2, l0_rwkv_k_k, l0_rwkv_k_a, l0_rwkv_r_k, l0_rwkv_ln_w, l0_rwkv_ln_b, l0_gmlp_ln_w, l0_gmlp_ln_b, l0_gmlp_ws, l0_gmlp_bs, l0_w_out, l0_norm_post_mix, l0_norm_pre_ffn, l0_ffn_w_gate, l0_ffn_w_up, l0_ffn_w_down, l0_norm_post_ffn, l1_norm_pre_mix, l1_w_in, l1_s5_a_re, l1_s5_a_im, l1_s5_log_dt, l1_s5_b_re, l1_s5_b_im, l1_s5_c_re, l1_s5_c_im, l1_s5_d, l1_s5_glu_w, l1_s5_glu_b, l1_m2_conv_w, l1_m2_conv_b, l1_m2_dt_bias, l1_m2_a_log, l1_m2_d, l1_m2_norm_w, l1_w_out, l1_norm_post_mix, l1_norm_pre_ffn, l1_moe_router, l1_moe_w_gate, l1_moe_w_up, l1_moe_w_down, l1_norm_post_ffn):
    b, t, d = x.shape
    n = b * t
    x2 = x.reshape(n, d)
    row = lambda vec: vec.reshape(1, -1)

    aw = l0_rwkv_w0.shape[0]
    heads = aw // RWKV_HEAD
    lw_, la_, lg_ = l0_rwkv_w2.shape[0], l0_rwkv_a2.shape[0], l0_rwkv_g2.shape[0]
    a_in = 3 * aw + lw_ + la_ + lg_
    padc = lambda m, wdt: jnp.pad(m, ((0, 0), (0, LANES - wdt)))
    o = 3 * aw
    w_a = jnp.concatenate([l0_w_in[:, :o], padc(l0_w_in[:, o:o + lw_], lw_),
                           padc(l0_w_in[:, o + lw_:o + lw_ + la_], la_),
                           padc(l0_w_in[:, o + lw_ + la_:a_in], lg_)], axis=1).astype(BF16)
    w_b = l0_w_in[:, a_in:].astype(BF16)
    p_a, p_b = _norm_proj(x2, l0_norm_pre_mix, [w_a, w_b], [F32, BF16])
    padv = lambda vec, wdt: jnp.pad(vec, (0, LANES - wdt))
    mu = l0_rwkv_mu
    mu_p = jnp.concatenate([mu[:o], padv(mu[o:o + lw_], lw_), padv(mu[o + lw_:o + lw_ + la_], la_),
                            padv(mu[o + lw_ + la_:], lg_)])
    padr = lambda m: jnp.pad(m, ((0, LANES - m.shape[0]), (0, 0))).astype(BF16)
    hid = jnp.arange(LANES, dtype=jnp.int32) // RWKV_HEAD
    gsum = (hid[:, None] == hid[None, :]).astype(BF16)
    rwkv_prm = [row(mu_p), row(l0_rwkv_w0), padr(l0_rwkv_w2), row(l0_rwkv_a0), padr(l0_rwkv_a2),
                padr(l0_rwkv_g2), row(l0_rwkv_k_k), row(l0_rwkv_k_a), row(l0_rwkv_r_k),
                row(l0_rwkv_ln_w), row(l0_rwkv_ln_b), gsum]
    ya = _rwkv_mix(p_a.reshape(b, t, -1), rwkv_prm, heads)
    yb = _gmlp_mix(p_b.reshape(b, t, -1), l0_gmlp_ln_w, l0_gmlp_ln_b, l0_gmlp_ws, l0_gmlp_bs)
    wo = l0_w_out.astype(BF16)
    cw = l1_s5_d.shape[0]
    dw = l1_m2_norm_w.shape[0]
    xw = l1_m2_conv_w.shape[1]
    nh = l1_m2_dt_bias.shape[0]
    w1 = l1_w_in
    w_parts = [w1[:, cw:cw + dw], w1[:, cw + dw:cw + dw + xw], padc(w1[:, cw + dw + xw:], nh)]
    x2, z_d, xbc, dtp, u5 = _mix_ffn(
        x2, ya.reshape(n, -1), yb.reshape(n, -1), wo[:aw], wo[aw:], row(l0_norm_post_mix),
        row(l0_norm_pre_ffn), l0_ffn_w_gate.astype(BF16), l0_ffn_w_up.astype(BF16),
        l0_ffn_w_down.astype(BF16), row(l0_norm_post_ffn), row(l1_norm_pre_mix),
        [w.astype(BF16) for w in w_parts], [BF16, BF16, F32], [w1[:, :cw].T.astype(BF16)], [BF16], seq=t)

    s5_w = _s5_weights(l1_s5_a_re, l1_s5_a_im, l1_s5_log_dt, l1_s5_b_re, l1_s5_b_im, l1_s5_c_re,
                       l1_s5_c_im, l1_s5_d, S5_CHUNK)
    y5 = _s5_core(u5, s5_w)
    yd = _ssd_mix(z_d.reshape(b, t, dw), xbc.reshape(b, t, xw), dtp.reshape(b, t, LANES),
                  l1_m2_conv_w, l1_m2_conv_b, l1_m2_dt_bias, l1_m2_a_log, l1_m2_d, l1_m2_norm_w)
    wo1 = l1_w_out.astype(BF16)
    x1, h, idx, gates = _mix_out1(x2, y5, yd.reshape(n, dw), l1_s5_glu_w.T.astype(BF16),
                                  l1_s5_glu_b.reshape(cw, 1), wo1[:cw], wo1[cw:], row(l1_norm_post_mix),
                                  row(l1_norm_pre_ffn), l1_moe_router)
    block_expert, dest, nact, n_blocks = _moe_plan(idx, n)
    ys = _moe_experts(h, block_expert, dest, nact, n_blocks, l1_moe_w_gate, l1_moe_w_up, l1_moe_w_down)
    out = _moe_combine(x1, ys, gates, row(l1_norm_post_ffn))
    return out.reshape(b, t, d)
```

```python
import functools

import jax
import jax.numpy as jnp
from jax import lax
from jax.experimental import pallas as pl
from jax.experimental.pallas import tpu as pltpu

F32 = jnp.float32
BF16 = jnp.bfloat16

EPS = 1e-6
RWKV_GN_EPS = 64e-5
RWKV_HEAD = 64
RWKV_CHUNK = 64
GMLP_CHUNK = 128
GMLP_GROUPS = 4
S5_GROUP_CH = 16
S5_STATE = 64
S5_CHUNK = 128
SSD_HEAD = 64
SSD_HEADS = 8
SSD_GROUPS = 2
SSD_STATE = 128
SSD_CONV = 4
SSD_CHUNK = 128
MOE_EXPERTS = 8
MOE_ROWS = 512
MXU_TILE = 256
LANES = 128
SUBLANES = 8
VMEM_LIMIT = 56 * 1024 * 1024


def _cparams(sem):
    return pltpu.CompilerParams(dimension_semantics=sem, vmem_limit_bytes=VMEM_LIMIT)


def _bdot(a, b):
    return jnp.dot(a.astype(BF16), b.astype(BF16), preferred_element_type=F32)


def _bdot_nt(a, b):
    return lax.dot_general(a.astype(BF16), b.astype(BF16), (((1,), (1,)), ((), ())),
                           preferred_element_type=F32)


def _bdot_tn(a, b):
    return lax.dot_general(a.astype(BF16), b.astype(BF16), (((0,), (0,)), ((), ())),
                           preferred_element_type=F32)


def _split3(x):
    h = x.astype(BF16)
    r1 = x - h.astype(F32)
    m = r1.astype(BF16)
    l = (r1 - m.astype(F32)).astype(BF16)
    return h, m, l


def _dot_x_exact(x, e):
    h, m, l = _split3(x)
    e = e.astype(BF16)
    return (jnp.dot(h, e, preferred_element_type=F32) + jnp.dot(m, e, preferred_element_type=F32)
            + jnp.dot(l, e, preferred_element_type=F32))


def _dot_exact_x(e, x):
    h, m, l = _split3(x)
    e = e.astype(BF16)
    return (jnp.dot(e, h, preferred_element_type=F32) + jnp.dot(e, m, preferred_element_type=F32)
            + jnp.dot(e, l, preferred_element_type=F32))


def _rms(x, g):
    return x * lax.rsqrt(jnp.mean(x * x, axis=-1, keepdims=True) + EPS) * g


def _sigmoid(x):
    return 1.0 / (1.0 + jnp.exp(-x))


def _silu(x):
    return x * _sigmoid(x)


def _softplus(x):
    return jnp.maximum(x, 0.0) + jnp.log(1.0 + jnp.exp(-jnp.abs(x)))


def _gelu_tanh(x):
    return 0.5 * x * (1.0 + jnp.tanh(0.7978845608028654 * (x + 0.044715 * x * x * x)))


def _iota2(shape, dim):
    return lax.broadcasted_iota(jnp.int32, shape, dim)


def _project(xn, w_refs, wt_refs, o_refs, ot_refs):
    for w_ref, o_ref in zip(w_refs, o_refs):
        o_ref[...] = jnp.dot(xn, w_ref[...], preferred_element_type=F32).astype(o_ref.dtype)
    for wt_ref, ot_ref in zip(wt_refs, ot_refs):
        yt = lax.dot_general(wt_ref[...], xn, (((1,), (1,)), ((), ())),
                             preferred_element_type=F32).astype(ot_ref.dtype)
        ng, nchunk, _, ch, ln = ot_ref.shape
        for cl in range(nchunk):
            ot_ref[:, cl, 0, :, :] = yt[:, cl * ln:(cl + 1) * ln].reshape(ng, ch, ln)


def _norm_proj_kernel(n_out, n_t, x_ref, g_ref, *refs):
    w_refs = refs[:n_out]
    wt_refs = refs[n_out:n_out + n_t]
    o_refs = refs[n_out + n_t:2 * n_out + n_t]
    ot_refs = refs[2 * n_out + n_t:]
    _project(_rms(x_ref[...], g_ref[...]).astype(BF16), w_refs, wt_refs, o_refs, ot_refs)


def _proj_specs(n, d, tm, ws, dtypes, wts, tdtypes, seq, ch, ln, once=False):
    per_seq = (seq or tm) // tm
    mode = dict(pipeline_mode=pl.Buffered(1)) if once else {}
    w_specs = [pl.BlockSpec(w.shape, lambda i: (0, 0), **mode) for w in list(ws) + list(wts)]
    out_specs = [pl.BlockSpec((tm, w.shape[1]), lambda i: (i, 0)) for w in ws]
    out_specs += [pl.BlockSpec((wt.shape[0] // ch, tm // ln, 1, ch, ln),
                               lambda i: (0, i % per_seq, i // per_seq, 0, 0)) for wt in wts]
    out_shape = [jax.ShapeDtypeStruct((n, w.shape[1]), dt) for w, dt in zip(ws, dtypes)]
    out_shape += [jax.ShapeDtypeStruct((wt.shape[0] // ch, seq // ln, n // seq, ch, ln), dt)
                  for wt, dt in zip(wts, tdtypes)]
    return w_specs, out_specs, out_shape


def _norm_proj(x2, g, ws, dtypes, wts=(), tdtypes=(), seq=None, ch=S5_GROUP_CH, ln=S5_CHUNK, tm=512):
    n, d = x2.shape
    w_specs, out_specs, out_shape = _proj_specs(n, d, tm, ws, dtypes, wts, tdtypes, seq, ch, ln)
    in_specs = [pl.BlockSpec((tm, d), lambda i: (i, 0)), pl.BlockSpec((1, d), lambda i: (0, 0))] + w_specs
    return pl.pallas_call(
        functools.partial(_norm_proj_kernel, len(ws), len(wts)),
        grid=(n // tm,), in_specs=in_specs, out_specs=out_specs, out_shape=out_shape,
        compiler_params=_cparams(("parallel",)), name="norm_proj",
    )(x2, g.reshape(1, d), *ws, *wts)


PRE_NAMES = ("v", "g", "bonus", "rt", "kt", "bt", "at", "bh", "kh")


def _rwkv_kernel(heads, nb, p_ref, pp_ref, mu_ref, w0_ref, w2_ref, a0_ref, a2_ref, g2_ref, kk_ref,
                 ka_ref, rk_ref, lnw_ref, lnb_ref, gs_ref, o_ref, z_ref, pre_ref, wl_ref):
    c = pl.program_id(1)
    ln = RWKV_CHUNK
    hd = RWKV_HEAD
    aw = heads * hd

    @pl.when(c == 0)
    def _():
        z_ref[...] = jnp.zeros_like(z_ref)
        pre_ref[...] = jnp.zeros_like(pre_ref)
        wl_ref[...] = jnp.zeros_like(wl_ref)

    tril_f = jnp.where(_iota2((ln, ln), 0) >= _iota2((ln, ln), 1), 1.0, 0.0)
    rows = _iota2((ln, 1), 0)
    gs_tile = gs_ref[...]

    def gs(x):
        nt = x.shape[1] // LANES
        stacked = jnp.concatenate([x[:, j * LANES:(j + 1) * LANES] for j in range(nt)], axis=0)
        red = _dot_x_exact(stacked, gs_tile)
        return jnp.concatenate([red[j * ln:(j + 1) * ln] for j in range(nt)], axis=1)

    pre_idx = {nm: idx for idx, nm in enumerate(PRE_NAMES)}
    pending = []

    def prep_steps():
        for bi in range(nb):
            p = p_ref[bi]
            prev = jnp.where(c == 0, 0.0, pp_ref[bi][SUBLANES - 1:SUBLANES, :])
            ps = jnp.where(rows == 0, prev, pltpu.roll(p, 1, axis=0))
            pm = p + (ps - p) * mu_ref[...]
            r = pm[:, 0:aw]
            k = pm[:, aw:2 * aw]
            v = pm[:, 2 * aw:3 * aw]
            xw = pm[:, 3 * aw:3 * aw + LANES]
            xa = pm[:, 3 * aw + LANES:3 * aw + 2 * LANES]
            xg = pm[:, 3 * aw + 2 * LANES:3 * aw + 3 * LANES]
            yield
            w = w0_ref[...] + _bdot(jnp.tanh(xw), w2_ref[...])
            a = _sigmoid(a0_ref[...] + _bdot(xa, a2_ref[...]))
            g = _bdot(_sigmoid(xg), g2_ref[...])
            yield
            w = -_softplus(-w) - 0.5
            lw = -jnp.exp(w)
            kk = k * kk_ref[...]
            kk_ss = gs(kk * kk)
            yield
            cs = _dot_exact_x(tril_f, lw)
            kk = kk / jnp.maximum(jnp.sqrt(kk_ss), 1e-12)
            kmod = k * (1.0 + (a - 1.0) * ka_ref[...])
            yield
            bonus = gs(r * kmod * rk_ref[...])
            bvec = kk * a
            cs_last = cs[ln - 1:ln, :]
            encs = jnp.exp(-cs)
            yield
            dec_end = jnp.exp(cs_last - cs)
            nxt = dict(v=v, g=g, bonus=bonus, rt=r * jnp.exp(cs), kt=kmod * encs, bt=bvec * encs,
                       at=-kk * jnp.exp(cs - lw), bh=bvec * dec_end, kh=kmod * dec_end)
            pending.append((bi, nxt, jnp.broadcast_to(jnp.exp(cs_last), (SUBLANES, aw))))
            yield

    prep = prep_steps()
    tick = lambda: next(prep, None)

    lane = _iota2((ln, LANES), 1)
    lane_in = jnp.where(lane >= hd, lane - hd, lane)
    trow = _iota2((ln, LANES), 0)
    left = lane < hd
    tril_p = lane_in <= trow
    stril_p = lane_in < trow
    eye_p = lane_in == trow
    eye_pf = jnp.where(eye_p, 1.0, 0.0)

    def bd(x):
        xb = x.astype(BF16)
        zero = jnp.zeros_like(xb)
        return jnp.concatenate([jnp.where(left, xb, zero), jnp.where(left, zero, xb)], axis=0)

    def dot(a, b):
        return jnp.dot(a.astype(BF16), b, preferred_element_type=F32)

    npair = heads // 2
    pairs = [(bi, j) for bi in range(nb) for j in range(npair)]

    class _Tiles:
        def __init__(self, name):
            self.idx = pre_idx[name]

        def __getitem__(self, i):
            bi, j = pairs[i]
            return pre_ref[self.idx, bi, :, j * LANES:(j + 1) * LANES]

    at, rt, bt, kt, vv, bh, kh = (_Tiles(n) for n in ("at", "rt", "bt", "kt", "v", "bh", "kh"))
    wl = [wl_ref[bi, 0:1, j * LANES:(j + 1) * LANES] for bi, j in pairs]
    z_all = z_ref[...]
    zs = [z_all[bi, j] for bi, j in pairs]
    npr = range(len(pairs))
    lhs = [jnp.concatenate([at[i], rt[i]], axis=0).astype(BF16) for i in npr]
    abk = [lax.dot_general(lhs[i], jnp.concatenate([bd(bt[i]), bd(kt[i])], axis=0),
                           (((1,), (1,)), ((), ())), preferred_element_type=F32) for i in npr]
    ab = [abk[i][:, :LANES] for i in npr]
    ak = [abk[i][:, LANES:] for i in npr]
    tick()
    nmat = [jnp.where(stril_p, ab[i][:ln], 0.0) for i in npr]
    tinv = [eye_pf + nmat[i] for i in npr]
    npow = [dot(nmat[i], bd(nmat[i])) for i in npr]
    tick()
    for step in range(5):
        bdn = [bd(npow[i]) for i in npr]
        if step < 4:
            both = [dot(jnp.concatenate([tinv[i], npow[i]], axis=0), bdn[i]) for i in npr]
            tinv = [tinv[i] + both[i][:ln] for i in npr]
            npow = [both[i][ln:] for i in npr]
        else:
            tinv = [tinv[i] + dot(tinv[i], bdn[i]) for i in npr]
        tick()
    bdv = [bd(vv[i]) for i in npr]
    bdz = [bd(zs[i]) for i in npr]
    xmat = [dot(jnp.concatenate([jnp.where(stril_p, ak[i][:ln], 0.0), at[i]], axis=1),
                jnp.concatenate([bdv[i], bdz[i]], axis=0)) for i in npr]
    tick()
    u = [dot(tinv[i], bd(xmat[i])) for i in npr]
    tick()
    ys_p = [dot(jnp.concatenate([rt[i], jnp.where(tril_p, ab[i][ln:], 0.0),
                                 jnp.where(tril_p, ak[i][ln:], 0.0)], axis=1),
                jnp.concatenate([bdz[i], bd(u[i]), bdv[i]], axis=0)) for i in npr]
    tick()
    cross = [_bdot_tn(jnp.concatenate([bh[i], kh[i]], axis=0), jnp.concatenate([u[i], vv[i]], axis=0))
             for i in npr]
    tick()
    z_new = []
    for i in npr:
        dg = jnp.where(eye_p, wl[i], 0.0)
        wl_i = jnp.sum(jnp.where(left, dg, 0.0), axis=1, keepdims=True)
        wl_j = jnp.sum(jnp.where(left, 0.0, dg), axis=1, keepdims=True)
        z_new.append(jnp.where(left, wl_i, wl_j) * zs[i] + jnp.where(left, cross[i][:ln], cross[i][ln:]))
    z_ref[...] = jnp.stack(z_new, axis=0).reshape(z_ref.shape)

    tick()
    inv = 1.0 / hd
    for bi in range(nb):
        y = jnp.concatenate(ys_p[bi * npair:(bi + 1) * npair], axis=1)
        mean = gs(y) * inv
        d = y - mean
        var = gs(d * d) * inv
        yn = d * lax.rsqrt(var + RWKV_GN_EPS) * lnw_ref[...] + lnb_ref[...]
        o_ref[bi] = ((yn + pre_ref[pre_idx["bonus"], bi] * pre_ref[pre_idx["v"], bi])
                     * pre_ref[pre_idx["g"], bi]).astype(o_ref.dtype)
        tick()
    for _ in prep:
        pass
    for bi, nxt, wl_next in pending:
        for nm, idx in pre_idx.items():
            pre_ref[idx, bi] = nxt[nm]
        wl_ref[bi] = wl_next


def _rwkv_mix(p_a, prm, heads, nb=4):
    b, t, cin = p_a.shape
    aw = heads * RWKV_HEAD
    ln = RWKV_CHUNK
    nc = t // ln
    sub = ln // SUBLANES
    full = lambda arr: pl.BlockSpec(arr.shape, lambda i, j: (0,) * arr.ndim)
    in_specs = [pl.BlockSpec((nb, ln, cin), lambda i, j: (i, jnp.minimum(j, nc - 1), 0)),
                pl.BlockSpec((nb, SUBLANES, cin),
                             lambda i, j: (i, jnp.maximum(jnp.minimum(j, nc - 1) * sub - 1, 0), 0))]
    in_specs += [full(x) for x in prm]
    return pl.pallas_call(
        functools.partial(_rwkv_kernel, heads, nb),
        grid=(b // nb, nc + 1), in_specs=in_specs,
        out_specs=pl.BlockSpec((nb, ln, aw), lambda i, j: (i, jnp.maximum(j - 1, 0), 0)),
        out_shape=jax.ShapeDtypeStruct((b, t, aw), BF16),
        scratch_shapes=[pltpu.VMEM((nb, heads // 2, RWKV_HEAD, 2 * RWKV_HEAD), F32),
                        pltpu.VMEM((len(PRE_NAMES), nb, ln, aw), F32),
                        pltpu.VMEM((nb, SUBLANES, aw), F32)],
        compiler_params=_cparams(("parallel", "arbitrary")), name="rwkv7",
    )(p_a, p_a, *prm)


def _gmlp_kernel(p_ref, lnw_ref, lnb_ref, ws_ref, bs_ref, o_ref):
    ln = GMLP_CHUNK
    bw = p_ref.shape[2] // 2
    gd = bw // GMLP_GROUPS
    tril = _iota2((ln, ln), 0) >= _iota2((ln, ln), 1)
    ws_c = [jnp.where(tril, ws_ref[gi], 0.0).astype(BF16) for gi in range(GMLP_GROUPS)]
    for ci in range(p_ref.shape[1] // ln):
        x = _gelu_tanh(p_ref[0, ci * ln:(ci + 1) * ln, :].astype(F32))
        for gi in range(GMLP_GROUPS):
            u = x[:, gi * gd:(gi + 1) * gd]
            v = x[:, bw + gi * gd:bw + (gi + 1) * gd]
            mean = jnp.mean(v, axis=-1, keepdims=True)
            d = v - mean
            var = jnp.mean(d * d, axis=-1, keepdims=True)
            vn = d * lax.rsqrt(var + EPS) * lnw_ref[gi:gi + 1, :] + lnb_ref[gi:gi + 1, :]
            s = jnp.dot(ws_c[gi], vn.astype(BF16), preferred_element_type=F32) + bs_ref[gi]
            o_ref[0, ci * ln:(ci + 1) * ln, gi * gd:(gi + 1) * gd] = (u * s).astype(o_ref.dtype)


def _gmlp_mix(p_b, ln_w, ln_b, ws, bs, rows=512):
    b, t, cin = p_b.shape
    bw = cin // 2
    gd = bw // GMLP_GROUPS
    bs_b = jnp.broadcast_to(bs[:, :, None], (GMLP_GROUPS, GMLP_CHUNK, gd))
    full = lambda arr: pl.BlockSpec(arr.shape, lambda i, j: (0,) * arr.ndim)
    return pl.pallas_call(
        _gmlp_kernel, grid=(b, t // rows),
        in_specs=[pl.BlockSpec((1, rows, cin), lambda i, j: (i, j, 0)),
                  full(ln_w), full(ln_b), full(ws), full(bs_b)],
        out_specs=pl.BlockSpec((1, rows, bw), lambda i, j: (i, j, 0)),
        out_shape=jax.ShapeDtypeStruct((b, t, bw), BF16),
        compiler_params=_cparams(("parallel", "parallel")), name="gmlp",
    )(p_b, ln_w, ln_b, ws, bs_b)


def _mix_ffn_kernel(fc, n_out, n_t, x_ref, ya_ref, yb_ref, wa_ref, wb_ref, gmix_ref, gpre_ref, wg_ref, wu_ref,
                    wd_ref, gpost_ref, gnext_ref, *refs):
    w_refs = refs[:n_out]
    wt_refs = refs[n_out:n_out + n_t]
    o_ref = refs[n_out + n_t]
    o_refs = refs[n_out + n_t + 1:2 * n_out + n_t + 1]
    ot_refs = refs[2 * n_out + n_t + 1:]
    y = _bdot(ya_ref[...], wa_ref[...]) + _bdot(yb_ref[...], wb_ref[...])
    x1 = x_ref[...] + _rms(y, gmix_ref[...])
    h = _rms(x1, gpre_ref[...]).astype(BF16)
    acc = None
    for c in range(wg_ref.shape[1] // fc):
        cols = slice(c * fc, (c + 1) * fc)
        gate = jnp.dot(h, wg_ref[:, cols], preferred_element_type=F32)
        up = jnp.dot(h, wu_ref[:, cols], preferred_element_type=F32)
        part = jnp.dot((_silu(gate) * up).astype(BF16), wd_ref[cols, :], preferred_element_type=F32)
        acc = part if acc is None else acc + part
    x2 = x1 + _rms(acc, gpost_ref[...])
    o_ref[...] = x2
    _project(_rms(x2, gnext_ref[...]).astype(BF16), w_refs, wt_refs, o_refs, ot_refs)


def _mix_ffn(x2, ya, yb, wa, wb, gmix, gpre, wg, wu, wd, gpost, gnext, ws, dtypes, wts, tdtypes, seq,
             ch=S5_GROUP_CH, ln=S5_CHUNK, tm=512, fc=MXU_TILE):
    n, d = x2.shape
    row = lambda arr: pl.BlockSpec((tm, arr.shape[1]), lambda i: (i, 0))
    once = lambda arr: pl.BlockSpec(arr.shape, lambda i: (0,) * arr.ndim, pipeline_mode=pl.Buffered(1))
    w_specs, out_specs, out_shape = _proj_specs(n, d, tm, ws, dtypes, wts, tdtypes, seq, ch, ln, once=True)
    return pl.pallas_call(
        functools.partial(_mix_ffn_kernel, fc, len(ws), len(wts)), grid=(n // tm,),
        in_specs=[row(x2), row(ya), row(yb), once(wa), once(wb), once(gmix), once(gpre), once(wg),
                  once(wu), once(wd), once(gpost), once(gnext)] + w_specs,
        out_specs=[row(x2)] + out_specs,
        out_shape=[jax.ShapeDtypeStruct((n, d), F32)] + out_shape,
        compiler_params=_cparams(("parallel",)), name="mix_ffn",
    )(x2, ya, yb, wa, wb, gmix, gpre, wg, wu, wd, gpost, gnext, *ws, *wts)


def _s5_kernel(nc, nb, u_ref, tap_ref, wsr_ref, wsi_ref, wcr_ref, wci_ref, alr_ref, ali_ref, d_ref, o_ref,
               toep_ref):
    ch, ln = u_ref.shape[3], u_ref.shape[4]
    u = u_ref[0].reshape(nc * nb, ch * ln)
    taps = tap_ref[0]
    width = ch * ln
    keep = (_iota2((ln, width), 1) & (ln - 1)) >= _iota2((ln, width), 0)
    for cin in range(ch):
        src = jnp.broadcast_to(taps[cin:cin + 1, :], (ln, width))
        blk = jnp.where(keep, pltpu.roll(src, 0, 1, stride=1, stride_axis=0), 0.0)
        toep_ref[cin * ln:(cin + 1) * ln, :] = blk.astype(BF16)
    y = jnp.dot(u, toep_ref[...], preferred_element_type=F32)
    xer = jnp.dot(u, wsr_ref[0], preferred_element_type=F32)
    xei = jnp.dot(u, wsi_ref[0], preferred_element_type=F32)
    alr = alr_ref[0]
    ali = ali_ref[0]
    cr = jnp.zeros((nb, xer.shape[1]), F32)
    ci = jnp.zeros((nb, xer.shape[1]), F32)
    prs, pis = [], []
    for c in range(nc):
        prs.append(cr)
        pis.append(ci)
        er = xer[c * nb:(c + 1) * nb]
        ei = xei[c * nb:(c + 1) * nb]
        cr, ci = alr * cr - ali * ci + er, alr * ci + ali * cr + ei
    pr = jnp.concatenate(prs, axis=0)
    pi = jnp.concatenate(pis, axis=0)
    y = y + _bdot(pr, wcr_ref[0]) + _bdot(pi, wci_ref[0])
    o_ref[0] = (y + d_ref[0] * u.astype(F32)).astype(o_ref.dtype).reshape(nc, nb, ch, ln)


def _s5_weights(a_re, a_im, log_dt, b_re, b_im, c_re, c_im, d_skip, ln):
    g, st = a_re.shape
    ch = b_re.shape[2]
    dt = jnp.exp(log_dt)[:, None]
    lr, li = a_re, a_im
    tau = jnp.arange(ln + 1, dtype=F32)[:, None, None]
    mag = jnp.exp(lr[None] * dt[None] * tau)
    pw_r = mag * jnp.cos(li[None] * dt[None] * tau)
    pw_i = mag * jnp.sin(li[None] * dt[None] * tau)
    ab_r, ab_i = pw_r[1], pw_i[1]
    nr, ni = ab_r - 1.0, ab_i
    den = lr * lr + li * li
    fr, fi = (nr * lr + ni * li) / den, (ni * lr - nr * li) / den
    bb_r = fr[..., None] * b_re - fi[..., None] * b_im
    bb_i = fr[..., None] * b_im + fi[..., None] * b_re
    cp_r = c_re[None] * pw_r[:ln, :, None, :] - c_im[None] * pw_i[:ln, :, None, :]
    cp_i = c_re[None] * pw_i[:ln, :, None, :] + c_im[None] * pw_r[:ln, :, None, :]
    hp = lax.Precision.HIGHEST
    taps = (jnp.einsum('tgcp,gpd->gdct', cp_r, bb_r, precision=hp)
            - jnp.einsum('tgcp,gpd->gdct', cp_i, bb_i, precision=hp))
    taps = taps.reshape(g, ch, ch * ln)
    rev_r, rev_i = pw_r[:ln][::-1], pw_i[:ln][::-1]
    ws_r = rev_r[..., None] * bb_r[None] - rev_i[..., None] * bb_i[None]
    ws_i = rev_r[..., None] * bb_i[None] + rev_i[..., None] * bb_r[None]
    ws_r = ws_r.transpose(1, 3, 0, 2).reshape(g, ch * ln, st)
    ws_i = ws_i.transpose(1, 3, 0, 2).reshape(g, ch * ln, st)
    q_r, q_i = pw_r[1:ln + 1], pw_i[1:ln + 1]
    wc_r = c_re[None] * q_r[:, :, None, :] - c_im[None] * q_i[:, :, None, :]
    wc_i = -(c_re[None] * q_i[:, :, None, :] + c_im[None] * q_r[:, :, None, :])
    wc_r = wc_r.transpose(1, 3, 2, 0).reshape(g, st, ch * ln)
    wc_i = wc_i.transpose(1, 3, 2, 0).reshape(g, st, ch * ln)
    al_r = pw_r[ln].reshape(g, 1, st)
    al_i = pw_i[ln].reshape(g, 1, st)
    d_t = jnp.repeat(d_skip.reshape(g, ch), ln, axis=1).reshape(g, 1, ch * ln)
    return (taps, ws_r.astype(BF16), ws_i.astype(BF16), wc_r.astype(BF16),
            wc_i.astype(BF16), al_r, al_i, d_t)


def _s5_core(u5, weights):
    g, nc, b, ch, ln = u5.shape
    blk = pl.BlockSpec((1, nc, b, ch, ln), lambda i: (i, 0, 0, 0, 0))
    per_g = lambda arr: pl.BlockSpec((1,) + arr.shape[1:], lambda i: (i, 0, 0))
    return pl.pallas_call(
        functools.partial(_s5_kernel, nc, b), grid=(g,),
        in_specs=[blk] + [per_g(w) for w in weights],
        out_specs=blk, out_shape=jax.ShapeDtypeStruct(u5.shape, BF16),
        scratch_shapes=[pltpu.VMEM((ln * ch, ln * ch), BF16)],
        compiler_params=_cparams(("parallel",)), name="s5",
    )(u5, *weights)


def _ssd_kernel(nb, z_ref, xbc_ref, xp_ref, dt_ref, cw_ref, cb_ref, dtb_ref, alog_ref, dsk_ref, nw_ref,
                o_ref, s_ref):
    c = pl.program_id(1)
    ln = SSD_CHUNK
    hd = SSD_HEAD
    dw = SSD_HEADS * hd
    gn = SSD_STATE

    @pl.when(c == 0)
    def _():
        s_ref[...] = jnp.zeros_like(s_ref)

    halo = xp_ref.shape[1]
    tril = _iota2((ln, ln), 0) >= _iota2((ln, ln), 1)
    tril_f = jnp.where(tril, 1.0, 0.0)
    fronts = []
    for bi in range(nb):
        xbc = xbc_ref[bi]
        prev = jnp.where(c == 0, jnp.zeros_like(xp_ref[bi]), xp_ref[bi])
        full = jnp.concatenate([prev, xbc], axis=0).astype(BF16)
        conv = cb_ref[...] + cw_ref[SSD_CONV - 1:SSD_CONV, :] * xbc.astype(F32)
        for j in range(SSD_CONV - 1):
            lag = SSD_CONV - 1 - j
            pick = _iota2((ln, halo + ln), 1) == _iota2((ln, halo + ln), 0) + (halo - lag)
            shifted = jnp.dot(jnp.where(pick, 1.0, 0.0).astype(BF16), full, preferred_element_type=F32)
            conv = conv + cw_ref[j:j + 1, :] * shifted
        act = _silu(conv)
        dt = _softplus(dt_ref[bi] + dtb_ref[...])
        acs = _dot_exact_x(tril_f, -jnp.exp(alog_ref[...]) * dt)
        fronts.append(dict(act=act, dt=dt, acs=acs, acs_t=acs.T, tot=acs[ln - 1:ln, :]))
    hg = SSD_HEADS // SSD_GROUPS
    s_all = s_ref[...]
    chains = [(bi, h) for bi in range(nb) for h in range(SSD_HEADS)]
    ks = range(len(chains))
    grp = lambda name, off, bi, gi: fronts[bi]["act"][:, dw + off + gi * gn:dw + off + (gi + 1) * gn]
    bm = {(bi, gi): grp("b", 0, bi, gi) for bi in range(nb) for gi in range(SSD_GROUPS)}
    cm = {(bi, gi): grp("c", SSD_GROUPS * gn, bi, gi) for bi in range(nb) for gi in range(SSD_GROUPS)}
    cb = {key: _bdot_nt(cm[key], bm[key]) for key in bm}
    col = [fronts[bi]["acs"][:, h:h + 1] for bi, h in chains]
    tot_h = [fronts[bi]["tot"][:, h:h + 1] for bi, h in chains]
    xh_h = [fronts[bi]["act"][:, h * hd:(h + 1) * hd] for bi, h in chains]
    xdt = [xh_h[k] * fronts[bi]["dt"][:, h:h + 1] for k, (bi, h) in enumerate(chains)]
    lcb = [cb[(bi, h // hg)] * jnp.exp(jnp.where(tril, col[k] - fronts[bi]["acs_t"][h:h + 1, :], -jnp.inf))
           for k, (bi, h) in enumerate(chains)]
    bdec = [bm[(bi, h // hg)] * jnp.exp(tot_h[k] - col[k]) for k, (bi, h) in enumerate(chains)]
    y_in = [_bdot(lcb[k], xdt[k]) for k in ks]
    y_st = [_bdot(cm[(bi, h // hg)], s_all[bi, h]) for bi, h in chains]
    s_up = [_bdot_tn(bdec[k], xdt[k]) for k in ks]
    y_heads = [y_in[k] + jnp.exp(col[k]) * y_st[k] + dsk_ref[:, h * hd:(h + 1) * hd] * xh_h[k]
               for k, (bi, h) in enumerate(chains)]
    s_ref[...] = jnp.stack([jnp.exp(tot_h[k]) * s_all[bi, h] + s_up[k] for k, (bi, h) in enumerate(chains)],
                           axis=0).reshape(s_ref.shape)
    gw = dw // SSD_GROUPS
    for bi in range(nb):
        y = jnp.concatenate(y_heads[bi * SSD_HEADS:(bi + 1) * SSD_HEADS], axis=1) * _silu(z_ref[bi].astype(F32))
        for gi in range(SSD_GROUPS):
            yg = y[:, gi * gw:(gi + 1) * gw]
            yg = yg * lax.rsqrt(jnp.mean(yg * yg, axis=-1, keepdims=True) + EPS)
            o_ref[bi, :, gi * gw:(gi + 1) * gw] = (yg * nw_ref[:, gi * gw:(gi + 1) * gw]).astype(o_ref.dtype)


def _ssd_mix(z, xbc, dtp, conv_w, conv_b, dt_bias, a_log, d_skip, norm_w, nb=2):
    b, t, dw = z.shape
    ln = SSD_CHUNK
    xw = xbc.shape[2]
    pad = lambda vec: jnp.pad(vec, (0, LANES - vec.shape[0])).reshape(1, LANES)
    dsk = jnp.repeat(d_skip, SSD_HEAD).reshape(1, dw)
    prm = [conv_w, conv_b.reshape(1, xw), pad(dt_bias), pad(a_log), dsk, norm_w.reshape(1, dw)]
    full = lambda arr: pl.BlockSpec(arr.shape, lambda i, j: (0,) * arr.ndim)
    blk = lambda w: pl.BlockSpec((nb, ln, w), lambda i, j: (i, j, 0))
    halo = 2 * SUBLANES
    return pl.pallas_call(
        functools.partial(_ssd_kernel, nb), grid=(b // nb, t // ln),
        in_specs=[blk(dw), blk(xw),
                  pl.BlockSpec((nb, halo, xw), lambda i, j: (i, jnp.maximum(j * (ln // halo) - 1, 0), 0)),
                  blk(LANES)] + [full(x) for x in prm],
        out_specs=blk(dw), out_shape=jax.ShapeDtypeStruct((b, t, dw), BF16),
        scratch_shapes=[pltpu.VMEM((nb, SSD_HEADS, SSD_STATE, SSD_HEAD), F32)],
        compiler_params=_cparams(("parallel", "arbitrary")), name="ssd",
    )(z, xbc, xbc, dtp, *prm)


def _mix_out1_kernel(x_ref, yc_ref, yd_ref, gw_ref, gb_ref, wc_ref, wd_ref, gpost_ref, gpre_ref,
                     wrh_ref, wrl_ref, x1_ref, h_ref, idx_ref, gate_ref):
    ng, nchunk, _, ch, ln = yc_ref.shape
    per = nchunk // 2
    grp = range(2)
    rows = [slice(k * per * ln, (k + 1) * per * ln) for k in grp]
    yc = [_gelu_tanh(jnp.concatenate([yc_ref[:, k * per + cl, 0, :, :].reshape(ng * ch, ln)
                                      for cl in range(per)], axis=1).astype(F32)) for k in grp]
    zt = [jnp.dot(gw_ref[...], yc[k].astype(BF16), preferred_element_type=F32) for k in grp]
    yc = [yc[k] * _sigmoid(zt[k] + gb_ref[...]) for k in grp]
    y = [_bdot_tn(yc[k], wc_ref[...]) + _bdot(yd_ref[rows[k], :], wd_ref[...]) for k in grp]
    x1 = [x_ref[rows[k], :] + _rms(y[k], gpost_ref[...]) for k in grp]
    h = [_rms(x1[k], gpre_ref[...]) for k in grp]
    for k in grp:
        x1_ref[rows[k], :] = x1[k]
        h_ref[rows[k], :] = h[k]
    hh = [h[k].astype(BF16) for k in grp]
    hl = [(h[k] - hh[k].astype(F32)).astype(BF16) for k in grp]
    wrh = wrh_ref[...]
    logits = [jnp.dot(hh[k], wrh, preferred_element_type=F32) + jnp.dot(hl[k], wrh, preferred_element_type=F32)
              + jnp.dot(hh[k], wrl_ref[...], preferred_element_type=F32) for k in grp]
    lane = _iota2(logits[0].shape, 1)
    lane_f = lane.astype(F32)
    for k in grp:
        lg = jnp.where(lane < MOE_EXPERTS, logits[k], -jnp.inf)
        m1 = jnp.max(lg, axis=-1, keepdims=True)
        i1 = jnp.min(jnp.where(lg == m1, lane_f, float(LANES)), axis=-1, keepdims=True)
        rest = jnp.where(lane_f == i1, -jnp.inf, lg)
        m2 = jnp.max(rest, axis=-1, keepdims=True)
        i2 = jnp.min(jnp.where(rest == m2, lane_f, float(LANES)), axis=-1, keepdims=True)
        e2 = jnp.exp(m2 - m1)
        g1 = 1.0 / (1.0 + e2)
        g2 = e2 / (1.0 + e2)
        idx_ref[rows[k], :] = jnp.where(lane == 0, i1, jnp.where(lane == 1, i2, 0.0)).astype(jnp.int32)
        gate_ref[rows[k], :] = jnp.where(lane == 0, g1, jnp.where(lane == 1, g2, 0.0))


def _mix_out1(x2, y5, yd, glu_wt, glu_b, wc, wd, gpost, gpre, wr, tm=512):
    n, d = x2.shape
    ng, nc, _, ch, ln = y5.shape
    per_seq = nc * ln // tm
    wr_p = jnp.pad(wr, ((0, 0), (0, LANES - wr.shape[1])))
    wrh = wr_p.astype(BF16)
    wrl = (wr_p - wrh.astype(F32)).astype(BF16)
    row = lambda w: pl.BlockSpec((tm, w), lambda i: (i, 0))
    full = lambda arr: pl.BlockSpec(arr.shape, lambda i: (0,) * arr.ndim)
    prm = [glu_wt, glu_b, wc, wd, gpost, gpre, wrh, wrl]
    return pl.pallas_call(
        _mix_out1_kernel, grid=(n // tm,),
        in_specs=[row(d), pl.BlockSpec((ng, tm // ln, 1, ch, ln),
                                       lambda i: (0, i % per_seq, i // per_seq, 0, 0)),
                  row(yd.shape[1])] + [full(p) for p in prm],
        out_specs=[row(d), row(d), row(LANES), row(LANES)],
        out_shape=[jax.ShapeDtypeStruct((n, d), F32), jax.ShapeDtypeStruct((n, d), F32),
                   jax.ShapeDtypeStruct((n, LANES), jnp.int32), jax.ShapeDtypeStruct((n, LANES), F32)],
        compiler_params=_cparams(("parallel",)), name="mix_out1",
    )(x2, y5, yd, *prm)


GATHER_UNROLL = 8


def _gather_rows(n_rows, make_copy):
    def body(j, carry):
        for q in range(GATHER_UNROLL):
            make_copy(j * GATHER_UNROLL + q).start(priority=q % 2)
        return carry

    lax.fori_loop(0, n_rows // GATHER_UNROLL, body, 0)


def _moe_kernel(be_ref, dest_ref, nact_ref, h_hbm, wg_hbm, wu_hbm, wd_hbm, y_hbm, buf_ref,
                xb_ref, wg_ref, wu_ref, wd_ref, sa_ref, sb_ref, tok_ref, dst_ref, gsem, ssem, wsem):
    i = pl.program_id(0)
    n_blocks = pl.num_programs(0)
    tm = buf_ref.shape[1]
    nact = nact_ref[0]
    active = i < nact
    slot = lax.rem(i, 2)
    other = 1 - slot
    xs = lambda sl: buf_ref.at[sl]
    yb = lambda sl: buf_ref.at[2 + sl]

    def gather_copy(block, sl, r):
        tok = tok_ref[block * tm + r]
        return pltpu.make_async_copy(h_hbm.at[pl.ds(tok, 1)], buf_ref.at[sl, pl.ds(r, 1)], gsem.at[sl])

    def scatter_copy(block, sl, r):
        dst = dst_ref[block * tm + r]
        return pltpu.make_async_copy(buf_ref.at[2 + sl, pl.ds(r, 1)], y_hbm.at[pl.ds(dst, 1)], ssem.at[sl])

    def wait_rows(sem_slot_ref, buf):
        pltpu.make_async_copy(h_hbm.at[pl.ds(0, tm)], buf, sem_slot_ref).wait()

    @pl.when(i == 0)
    def _():
        n_pairs = dest_ref.shape[0]
        n_tok = n_pairs // 2
        n_real = y_hbm.shape[0] - 2 * tm

        def fill_block(blk, carry):
            base = n_real + lax.rem(blk, 2) * tm

            def fill(m, carry2):
                for q in range(GATHER_UNROLL):
                    r = m * GATHER_UNROLL + q
                    tok_ref[blk * tm + r] = 0
                    dst_ref[blk * tm + r] = base + r
                return carry2

            last_of_expert = be_ref[blk] != be_ref[jnp.minimum(blk + 1, n_blocks - 1)]

            @pl.when(jnp.logical_or(blk >= nact - 1, last_of_expert))
            def _():
                lax.fori_loop(0, tm // GATHER_UNROLL, fill, 0)

            return carry

        lax.fori_loop(0, n_blocks, fill_block, 0)

        def place(m, carry):
            for q in range(GATHER_UNROLL):
                s = dest_ref[m * GATHER_UNROLL + q]
                tok = m * (GATHER_UNROLL // 2) + q // 2
                tok_ref[s] = tok
                dst_ref[s] = (q % 2) * n_tok + tok
            return carry

        lax.fori_loop(0, nact_ref[1], place, 0)
        _gather_rows(tm, functools.partial(gather_copy, 0, 0))
        buf_ref[3] = jnp.zeros(buf_ref.shape[1:], F32)
        n_real = y_hbm.shape[0] - 2 * tm
        for half in range(2):
            init = pltpu.make_async_copy(yb(1), y_hbm.at[pl.ds(n_real + half * tm, tm)], ssem.at[0])
            init.start()
            init.wait()

    @pl.when(i <= nact)
    def _():
        wait_rows(gsem.at[slot], xs(slot))

    @pl.when(jnp.logical_and(i >= 1, i <= nact))
    def _():
        wait_rows(ssem.at[slot], yb(slot))

    @pl.when(i == nact)
    def _():
        _gather_rows(tm, functools.partial(scatter_copy, i - 1, other))
        wait_rows(ssem.at[other], yb(other))

    @pl.when(active)
    def _():
        xb_ref[...] = buf_ref[slot].astype(BF16)

    e = be_ref[i]
    fresh = jnp.logical_or(i == 0, e != be_ref[jnp.maximum(i - 1, 0)])
    nch = wg_ref.shape[1] // MXU_TILE

    pieces = []
    for c in range(nch):
        cols = slice(c * MXU_TILE, (c + 1) * MXU_TILE)
        pieces.append((wg_hbm.at[e, :, cols], sa_ref, 2 * c, wg_ref, (slice(None), cols)))
        pieces.append((wu_hbm.at[e, :, cols], sa_ref, 2 * c + 1, wu_ref, (slice(None), cols)))
        pieces.append((wd_hbm.at[e, cols, :], sb_ref, c, wd_ref, (cols, slice(None))))
    ahead = 2 * 3

    def piece_copy(p):
        src, stage, k, _, _ = pieces[p]
        sl = k % stage.shape[0]
        return pltpu.make_async_copy(src, stage.at[sl], wsem.at[(0 if stage is sa_ref else sa_ref.shape[0]) + sl])

    def compute(load_weights):
        x = xb_ref[...]
        prev = jnp.where(i == 0, n_blocks - 1, i - 1)
        if load_weights:
            for p in range(ahead):
                piece_copy(p).start()
        for c in range(nch):
            if load_weights:
                for p in range(3 * c, 3 * c + 3):
                    _, stage, k, dst, where = pieces[p]
                    piece_copy(p).wait()
                    dst[where] = stage[k % stage.shape[0]].astype(BF16)
                    if p + ahead < len(pieces):
                        piece_copy(p + ahead).start()
            lo, hi = tm * c // nch, tm * (c + 1) // nch
            cuts = [lo, lo + (hi - lo) // 3, lo + 2 * (hi - lo) // 3, hi]

            def row_copies(part):
                for r in range(cuts[part], cuts[part + 1]):
                    gather_copy(i + 1, other, r).start(priority=1)
                    scatter_copy(prev, other, r).start(priority=1)

            cols = slice(c * MXU_TILE, (c + 1) * MXU_TILE)
            row_copies(0)
            gate = jnp.dot(x, wg_ref[:, cols], preferred_element_type=F32)
            buf_ref[4, 0:SUBLANES, 0:LANES] = gate[0:SUBLANES, 0:LANES]
            row_copies(1)
            up = jnp.dot(x, wu_ref[:, cols], preferred_element_type=F32)
            buf_ref[4, SUBLANES:2 * SUBLANES, 0:LANES] = up[0:SUBLANES, 0:LANES]
            row_copies(2)
            part = jnp.dot((_silu(gate) * up).astype(BF16), wd_ref[cols, :], preferred_element_type=F32)
            if c == 0:
                buf_ref[2 + slot] = part
            else:
                buf_ref[2 + slot] += part

    @pl.when(jnp.logical_and(active, fresh))
    def _():
        compute(True)

    @pl.when(jnp.logical_and(active, jnp.logical_not(fresh)))
    def _():
        compute(False)


def _moe_experts(h, block_expert, dest, nact, n_blocks, wg, wu, wd):
    n, d = h.shape
    tm = MOE_ROWS
    ff = wg.shape[2]
    anywhere = pl.BlockSpec(memory_space=pl.ANY)
    grid_spec = pltpu.PrefetchScalarGridSpec(
        num_scalar_prefetch=3, grid=(n_blocks,),
        in_specs=[anywhere, anywhere, anywhere, anywhere],
        out_specs=pl.BlockSpec(memory_space=pl.ANY),
        scratch_shapes=[pltpu.VMEM((5, tm, d), F32), pltpu.VMEM((tm, d), BF16),
                        pltpu.VMEM((d, ff), BF16), pltpu.VMEM((d, ff), BF16), pltpu.VMEM((ff, d), BF16),
                        pltpu.VMEM((4, d, MXU_TILE), F32), pltpu.VMEM((2, MXU_TILE, d), F32),
                        pltpu.SMEM((n_blocks * tm,), jnp.int32), pltpu.SMEM((n_blocks * tm,), jnp.int32),
                        pltpu.SemaphoreType.DMA((2,)), pltpu.SemaphoreType.DMA((2,)),
                        pltpu.SemaphoreType.DMA((6,))])
    return pl.pallas_call(
        _moe_kernel, grid_spec=grid_spec,
        out_shape=jax.ShapeDtypeStruct((2 * n + 2 * tm, d), F32),
        compiler_params=pltpu.CompilerParams(dimension_semantics=("arbitrary",),
                                             vmem_limit_bytes=VMEM_LIMIT, disable_bounds_checks=True),
        name="moe_experts",
    )(block_expert, dest, nact, h, wg, wu, wd)


def _combine_kernel(x_ref, y0_ref, y1_ref, gate_ref, gpost_ref, o_ref):
    gates = gate_ref[...]
    y = gates[:, 0:1] * y0_ref[...] + gates[:, 1:2] * y1_ref[...]
    o_ref[...] = x_ref[...] + _rms(y, gpost_ref[...])


def _moe_combine(x1, y, gates, gpost, tm=512):
    n, d = x1.shape
    nt = n // tm
    return pl.pallas_call(
        _combine_kernel, grid=(nt,),
        in_specs=[pl.BlockSpec((tm, d), lambda i: (i, 0)), pl.BlockSpec((tm, d), lambda i: (i, 0)),
                  pl.BlockSpec((tm, d), lambda i: (nt + i, 0)),
                  pl.BlockSpec((tm, LANES), lambda i: (i, 0)), pl.BlockSpec((1, d), lambda i: (0, 0))],
        out_specs=pl.BlockSpec((tm, d), lambda i: (i, 0)),
        out_shape=jax.ShapeDtypeStruct((n, d), F32),
        compiler_params=_cparams(("parallel",)), name="moe_combine",
    )(x1, y, y, gates, gpost)


def _moe_plan(idx, n):
    tm = MOE_ROWS
    flat_e = idx[:, :2].reshape(-1)
    onehot = (flat_e[:, None] == jnp.arange(MOE_EXPERTS, dtype=jnp.int32)[None, :]).astype(jnp.int32)
    csum = jnp.cumsum(onehot, axis=0)
    counts = csum[-1]
    rank = jnp.sum((csum - onehot) * onehot, axis=1)
    padded = (counts + tm - 1) // tm * tm
    pend = jnp.cumsum(padded)
    pstart = pend - padded
    dest = (jnp.sum(onehot * pstart[None, :], axis=1) + rank).astype(jnp.int32)
    n_blocks = (2 * n) // tm + MOE_EXPERTS + 1
    block_start = jnp.arange(n_blocks, dtype=jnp.int32) * tm
    block_expert = jnp.minimum(jnp.sum((block_start[:, None] >= pend[None, :]).astype(jnp.int32), axis=1),
                               MOE_EXPERTS - 1)
    nact = jnp.stack([(pend[-1] // tm).astype(jnp.int32), jnp.int32(2 * n // GATHER_UNROLL)])
    return block_expert, dest, nact, n_blocks


def kernel(x, l0_norm_pre_mix, l0_w_in, l0_rwkv_mu, l0_rwkv_w0, l0_rwkv_w2, l0_rwkv_a0, l0_rwkv_a2, l0_rwkv_g2, l0_rwkv_k_k, l0_rwkv_k_a, l0_rwkv_r_k, l0_rwkv_ln_w, l0_rwkv_ln_b, l0_gmlp_ln_w, l0_gmlp_ln_b, l0_gmlp_ws, l0_gmlp_bs, l0_w_out, l0_norm_post_mix, l0_norm_pre_ffn, l0_ffn_w_gate, l0_ffn_w_up, l0_ffn_w_down, l0_norm_post_ffn, l1_norm_pre_mix, l1_w_in, l1_s5_a_re, l1_s5_a_im, l1_s5_log_dt, l1_s5_b_re, l1_s5_b_im, l1_s5_c_re, l1_s5_c_im, l1_s5_d, l1_s5_glu_w, l1_s5_glu_b, l1_m2_conv_w, l1_m2_conv_b, l1_m2_dt_bias, l1_m2_a_log, l1_m2_d, l1_m2_norm_w, l1_w_out, l1_norm_post_mix, l1_norm_pre_ffn, l1_moe_router, l1_moe_w_gate, l1_moe_w_up, l1_moe_w_down, l1_norm_post_ffn):
    b, t, d = x.shape
    n = b * t
    x2 = x.reshape(n, d)
    row = lambda vec: vec.reshape(1, -1)

    aw = l0_rwkv_w0.shape[0]
    heads = aw // RWKV_HEAD
    lw_, la_, lg_ = l0_rwkv_w2.shape[0], l0_rwkv_a2.shape[0], l0_rwkv_g2.shape[0]
    a_in = 3 * aw + lw_ + la_ + lg_
    padc = lambda m, wdt: jnp.pad(m, ((0, 0), (0, LANES - wdt)))
    o = 3 * aw
    w_a = jnp.concatenate([l0_w_in[:, :o], padc(l0_w_in[:, o:o + lw_], lw_),
                           padc(l0_w_in[:, o + lw_:o + lw_ + la_], la_),
                           padc(l0_w_in[:, o + lw_ + la_:a_in], lg_)], axis=1).astype(BF16)
    w_b = l0_w_in[:, a_in:].astype(BF16)
    p_a, p_b = _norm_proj(x2, l0_norm_pre_mix, [w_a, w_b], [F32, BF16])
    padv = lambda vec, wdt: jnp.pad(vec, (0, LANES - wdt))
    mu = l0_rwkv_mu
    mu_p = jnp.concatenate([mu[:o], padv(mu[o:o + lw_], lw_), padv(mu[o + lw_:o + lw_ + la_], la_),
                            padv(mu[o + lw_ + la_:], lg_)])
    padr = lambda m: jnp.pad(m, ((0, LANES - m.shape[0]), (0, 0))).astype(BF16)
    hid = jnp.arange(LANES, dtype=jnp.int32) // RWKV_HEAD
    gsum = (hid[:, None] == hid[None, :]).astype(BF16)
    rwkv_prm = [row(mu_p), row(l0_rwkv_w0), padr(l0_rwkv_w2), row(l0_rwkv_a0), padr(l0_rwkv_a2),
                padr(l0_rwkv_g2), row(l0_rwkv_k_k), row(l0_rwkv_k_a), row(l0_rwkv_r_k),
                row(l0_rwkv_ln_w), row(l0_rwkv_ln_b), gsum]
    ya = _rwkv_mix(p_a.reshape(b, t, -1), rwkv_prm, heads)
    yb = _gmlp_mix(p_b.reshape(b, t, -1), l0_gmlp_ln_w, l0_gmlp_ln_b, l0_gmlp_ws, l0_gmlp_bs)
    wo = l0_w_out.astype(BF16)
    cw = l1_s5_d.shape[0]
    dw = l1_m2_norm_w.shape[0]
    xw = l1_m2_conv_w.shape[1]
    nh = l1_m2_dt_bias.shape[0]
    w1 = l1_w_in
    w_parts = [w1[:, cw:cw + dw], w1[:, cw + dw:cw + dw + xw], padc(w1[:, cw + dw + xw:], nh)]
    x2, z_d, xbc, dtp, u5 = _mix_ffn(
        x2, ya.reshape(n, -1), yb.reshape(n, -1), wo[:aw], wo[aw:], row(l0_norm_post_mix),
        row(l0_norm_pre_ffn), l0_ffn_w_gate.astype(BF16), l0_ffn_w_up.astype(BF16),
        l0_ffn_w_down.astype(BF16), row(l0_norm_post_ffn), row(l1_norm_pre_mix),
        [w.astype(BF16) for w in w_parts], [BF16, BF16, F32], [w1[:, :cw].T.astype(BF16)], [BF16], seq=t)

    s5_w = _s5_weights(l1_s5_a_re, l1_s5_a_im, l1_s5_log_dt, l1_s5_b_re, l1_s5_b_im, l1_s5_c_re,
                       l1_s5_c_im, l1_s5_d, S5_CHUNK)
    y5 = _s5_core(u5, s5_w)
    yd = _ssd_mix(z_d.reshape(b, t, dw), xbc.reshape(b, t, xw), dtp.reshape(b, t, LANES),
                  l1_m2_conv_w, l1_m2_conv_b, l1_m2_dt_bias, l1_m2_a_log, l1_m2_d, l1_m2_norm_w)
    wo1 = l1_w_out.astype(BF16)
    x1, h, idx, gates = _mix_out1(x2, y5, yd.reshape(n, dw), l1_s5_glu_w.T.astype(BF16),
                                  l1_s5_glu_b.reshape(cw, 1), wo1[:cw], wo1[cw:], row(l1_norm_post_mix),
                                  row(l1_norm_pre_ffn), l1_moe_router)
    block_expert, dest, nact, n_blocks = _moe_plan(idx, n)
    ys = _moe_experts(h, block_expert, dest, nact, n_blocks, l1_moe_w_gate, l1_moe_w_up, l1_moe_w_down)
    out = _moe_combine(x1, ys, gates, row(l1_norm_post_ffn))
    return out.reshape(b, t, d)
```
